```python
import jax, jax.numpy as jnp
from jax import lax
import numpy as np

D_MODEL = 2048
BATCH = 8
SEQ = 8192
DEPTH = 4

PLE_DIM = 256
EPS = 1e-6
MIX = D_MODEL

CHUNK = 128
A_HEAD = 128
A_WIDTH = MIX // 4
A_HEADS = A_WIDTH // A_HEAD

B_GROUP = 128
B_WIDTH = MIX // 4
B_GROUPS = B_WIDTH // B_GROUP
CONV_W = 3

C_WIDTH = MIX - A_WIDTH - B_WIDTH
C_V = 128
C_HEADS = C_WIDTH // C_V
C_NOPE = 128
C_ROPE = 64
KV_RANK = 512
ROPE_BASE = 10000.0
Q_BLOCK = 128

IN_SPLITS = (A_WIDTH, A_WIDTH, A_WIDTH,
             B_WIDTH, B_WIDTH, B_WIDTH, B_WIDTH,
             C_HEADS * (C_NOPE + C_ROPE), KV_RANK, C_ROPE, C_WIDTH)
IN_WIDTH = 3 * A_WIDTH + 4 * B_WIDTH + C_HEADS * (C_NOPE + C_ROPE) + KV_RANK + C_ROPE + C_WIDTH

kernel_name = 'hybrid_sgu_shortconv_mla_encoder'


def rms_norm(x, g):
    x32 = x.astype(jnp.float32)
    y = x32 * lax.rsqrt(jnp.mean(x32 * x32, axis=-1, keepdims=True) + EPS)
    return y.astype(x.dtype) * g


def rope_tables(positions):
    inv = 1.0 / (ROPE_BASE ** (jnp.arange(0, C_ROPE, 2, dtype=jnp.float32) / C_ROPE))
    ang = positions.astype(jnp.float32)[..., None] * inv
    return jnp.cos(ang), jnp.sin(ang)


def apply_rope(x, cos, sin):
    half = x.shape[-1] // 2
    x1, x2 = x[..., :half], x[..., half:]
    out = jnp.concatenate([x1 * cos - x2 * sin, x2 * cos + x1 * sin], axis=-1)
    return out.astype(x.dtype)


def spatial_gating(u, v, z, v_gain, w_s, b_s):
    bsz, s_len, _ = u.shape
    v = rms_norm(v.reshape(bsz, s_len, A_HEADS, A_HEAD), v_gain)
    vc = v.reshape(bsz, s_len // CHUNK, CHUNK, A_HEADS, A_HEAD)
    s = jnp.einsum('hnm,bkmhc->bknhc', w_s, vc) + b_s.T[None, None, :, :, None]
    return u * s.reshape(bsz, s_len, A_WIDTH) * jax.nn.silu(z)


def short_conv(gate_b, gate_c, h, z, conv_w, conv_b):
    s_len = h.shape[1]
    pad = CONV_W // 2
    xp = jnp.pad(gate_c * h, ((0, 0), (pad, pad), (0, 0)))
    y = conv_b + sum(xp[:, j:j + s_len] * conv_w[j] for j in range(CONV_W))
    return gate_b * y * jax.nn.silu(z)


def latent_attention(q, c_kv, k_rope, z, cos, sin, kv_gain, w_ukv, qn_g, qr_g, kn_g, kr_g):
    bsz, s_len, _ = q.shape
    q = q.reshape(bsz, s_len, C_HEADS, C_NOPE + C_ROPE)
    q_nope = rms_norm(q[..., :C_NOPE], qn_g)
    q_rope = apply_rope(rms_norm(q[..., C_NOPE:], qr_g), cos[:, :, None], sin[:, :, None])
    kv = (rms_norm(c_kv, kv_gain) @ w_ukv).reshape(bsz, s_len, C_HEADS, C_NOPE + C_V)
    k_nope = rms_norm(kv[..., :C_NOPE], kn_g)
    v = kv[..., C_NOPE:]
    k_r = apply_rope(rms_norm(k_rope, kr_g), cos, sin)
    scale = (C_NOPE + C_ROPE) ** -0.5
    n_blk = s_len // Q_BLOCK

    def to_blocks(t):
        return jnp.moveaxis(t.reshape(bsz, n_blk, Q_BLOCK, *t.shape[2:]), 1, 0)

    def attend(blk):
        qn, qr = blk
        s = jnp.einsum('bqhd,bkhd->bhqk', qn, k_nope) + jnp.einsum('bqhr,bkr->bhqk', qr, k_r)
        w = jax.nn.softmax(s.astype(jnp.float32) * scale, axis=-1).astype(v.dtype)
        return jnp.einsum('bhqk,bkhd->bqhd', w, v)

    o = lax.map(attend, (to_blocks(q_nope), to_blocks(q_rope)))
    o = jnp.moveaxis(o, 0, 1).reshape(bsz, s_len, C_WIDTH)
    return o * jax.nn.silu(z)


def _fwd_setup_inputs(seed: int = 0) -> dict:
    key = jax.random.key(seed)
    ks = jax.random.split(key, 24)
    f32 = jnp.float32

    def nrm(k, shape, scale):
        return jax.random.normal(k, shape, f32) * scale

    def gain(k, shape):
        return 1.0 + 0.01 * jax.random.normal(k, shape, f32)

    return {
        'x': jax.random.normal(ks[0], (BATCH, SEQ, D_MODEL), f32),
        'p': jax.random.normal(ks[1], (DEPTH, BATCH, SEQ, PLE_DIM), f32),
        'positions': jnp.broadcast_to(jnp.arange(SEQ, dtype=jnp.int32), (BATCH, SEQ)),
        'attn_norm': gain(ks[2], (DEPTH, D_MODEL)),
        'w_in': nrm(ks[3], (DEPTH, D_MODEL, IN_WIDTH), D_MODEL ** -0.5),
        'sgu_norm': gain(ks[4], (DEPTH, A_HEADS, A_HEAD)),
        'w_spatial': nrm(ks[5], (DEPTH, A_HEADS, CHUNK, CHUNK), CHUNK ** -0.5),
        'b_spatial': gain(ks[6], (DEPTH, A_HEADS, CHUNK)),
        'conv_w': nrm(ks[7], (DEPTH, CONV_W, B_WIDTH), CONV_W ** -0.5),
        'conv_b': nrm(ks[8], (DEPTH, B_WIDTH), 0.01),
        'kv_norm': gain(ks[9], (DEPTH, KV_RANK)),
        'w_ukv': nrm(ks[10], (DEPTH, KV_RANK, C_HEADS * (C_NOPE + C_V)), KV_RANK ** -0.5),
        'q_nope_norm': gain(ks[11], (DEPTH, C_NOPE)),
        'q_rope_norm': gain(ks[12], (DEPTH, C_ROPE)),
        'k_nope_norm': gain(ks[13], (DEPTH, C_NOPE)),
        'k_rope_norm': gain(ks[14], (DEPTH, C_ROPE)),
        'out_norm': gain(ks[15], (DEPTH, MIX)),
        'w_out': nrm(ks[16], (DEPTH, MIX, D_MODEL), MIX ** -0.5),
        'ple_norm': gain(ks[17], (DEPTH, D_MODEL)),
        'w_ple_gate': nrm(ks[18], (DEPTH, D_MODEL, D_MODEL), D_MODEL ** -0.5),
        'w_ple_proj': nrm(ks[19], (DEPTH, PLE_DIM, D_MODEL), PLE_DIM ** -0.5),
    }


def _fwd_reference(x, p, positions, attn_norm, w_in, sgu_norm, w_spatial, b_spatial, conv_w, conv_b,
              kv_norm, w_ukv, q_nope_norm, q_rope_norm, k_nope_norm, k_rope_norm,
              out_norm, w_out, ple_norm, w_ple_gate, w_ple_proj):
    cos, sin = rope_tables(positions)
    split_pts = [int(c) for c in np.cumsum(IN_SPLITS)[:-1]]
    out_pts = [A_WIDTH, A_WIDTH + B_WIDTH]
    h = x
    for i in range(DEPTH):
        hn = rms_norm(h, attn_norm[i])
        proj = hn @ w_in[i]
        (a_u, a_v, a_z, b_b, b_c, b_h, b_z, c_q, c_kv, c_kr, c_z) = jnp.split(proj, split_pts, axis=-1)
        y_a = spatial_gating(a_u, a_v, a_z, sgu_norm[i], w_spatial[i], b_spatial[i])
        y_b = short_conv(b_b, b_c, b_h, b_z, conv_w[i], conv_b[i])
        y_c = latent_attention(c_q, c_kv, c_kr, c_z, cos, sin, kv_norm[i], w_ukv[i],
                               q_nope_norm[i], q_rope_norm[i], k_nope_norm[i], k_rope_norm[i])
        g_a, g_b, g_c = jnp.split(out_norm[i], out_pts)
        y = jnp.concatenate([rms_norm(y_a, g_a), rms_norm(y_b, g_b), rms_norm(y_c, g_c)], axis=-1)
        h = h + y @ w_out[i]
        gate = jax.nn.sigmoid(rms_norm(h, ple_norm[i]) @ w_ple_gate[i])
        h = h + gate * (p[i] @ w_ple_proj[i])
    return h


import jax as _jax
import jax.numpy as _jnp

TWIN_FORMAT = 'train_step'
FWD_PARAMS = ['x', 'p', 'positions', 'attn_norm', 'w_in', 'sgu_norm', 'w_spatial', 'b_spatial', 'conv_w', 'conv_b', 'kv_norm', 'w_ukv', 'q_nope_norm', 'q_rope_norm', 'k_nope_norm', 'k_rope_norm', 'out_norm', 'w_out', 'ple_norm', 'w_ple_gate', 'w_ple_proj']
TWIN_WEIGHTS = ['attn_norm', 'w_in', 'sgu_norm', 'w_spatial', 'b_spatial', 'conv_w', 'conv_b', 'kv_norm', 'w_ukv', 'q_nope_norm', 'q_rope_norm', 'k_nope_norm', 'k_rope_norm', 'out_norm', 'w_out', 'ple_norm', 'w_ple_gate', 'w_ple_proj']
TWIN_DIFF_INPUT = 'x'
TWIN_INPUTS = ['x', 'p', 'positions', 'attn_norm', 'w_in', 'sgu_norm', 'w_spatial', 'b_spatial', 'conv_w', 'conv_b', 'kv_norm', 'w_ukv', 'q_nope_norm', 'q_rope_norm', 'k_nope_norm', 'k_rope_norm', 'out_norm', 'w_out', 'ple_norm', 'w_ple_gate', 'w_ple_proj', 'loss_target', 'm_attn_norm', 'm_w_in', 'm_sgu_norm', 'm_w_spatial', 'm_b_spatial', 'm_conv_w', 'm_conv_b', 'm_kv_norm', 'm_w_ukv', 'm_q_nope_norm', 'm_q_rope_norm', 'm_k_nope_norm', 'm_k_rope_norm', 'm_out_norm', 'm_w_out', 'm_ple_norm', 'm_w_ple_gate', 'm_w_ple_proj', 'v_attn_norm', 'v_w_in', 'v_sgu_norm', 'v_w_spatial', 'v_b_spatial', 'v_conv_w', 'v_conv_b', 'v_kv_norm', 'v_w_ukv', 'v_q_nope_norm', 'v_q_rope_norm', 'v_k_nope_norm', 'v_k_rope_norm', 'v_out_norm', 'v_w_out', 'v_ple_norm', 'v_w_ple_gate', 'v_w_ple_proj']
TWIN_OUTPUTS = ['loss', 'grad_x', 'grad_attn_norm', 'grad_w_in', 'grad_sgu_norm', 'grad_w_spatial', 'grad_b_spatial', 'grad_conv_w', 'grad_conv_b', 'grad_kv_norm', 'grad_w_ukv', 'grad_q_nope_norm', 'grad_q_rope_norm', 'grad_k_nope_norm', 'grad_k_rope_norm', 'grad_out_norm', 'grad_w_out', 'grad_ple_norm', 'grad_w_ple_gate', 'grad_w_ple_proj', 'delta_attn_norm', 'delta_w_in', 'delta_sgu_norm', 'delta_w_spatial', 'delta_b_spatial', 'delta_conv_w', 'delta_conv_b', 'delta_kv_norm', 'delta_w_ukv', 'delta_q_nope_norm', 'delta_q_rope_norm', 'delta_k_nope_norm', 'delta_k_rope_norm', 'delta_out_norm', 'delta_w_out', 'delta_ple_norm', 'delta_w_ple_gate', 'delta_w_ple_proj', 'new_m_attn_norm', 'new_m_w_in', 'new_m_sgu_norm', 'new_m_w_spatial', 'new_m_b_spatial', 'new_m_conv_w', 'new_m_conv_b', 'new_m_kv_norm', 'new_m_w_ukv', 'new_m_q_nope_norm', 'new_m_q_rope_norm', 'new_m_k_nope_norm', 'new_m_k_rope_norm', 'new_m_out_norm', 'new_m_w_out', 'new_m_ple_norm', 'new_m_w_ple_gate', 'new_m_w_ple_proj', 'new_v_attn_norm', 'new_v_w_in', 'new_v_sgu_norm', 'new_v_w_spatial', 'new_v_b_spatial', 'new_v_conv_w', 'new_v_conv_b', 'new_v_kv_norm', 'new_v_w_ukv', 'new_v_q_nope_norm', 'new_v_q_rope_norm', 'new_v_k_nope_norm', 'new_v_k_rope_norm', 'new_v_out_norm', 'new_v_w_out', 'new_v_ple_norm', 'new_v_w_ple_gate', 'new_v_w_ple_proj']
TWIN_LEAF_KINDS = {'loss': 'loss', 'grad_x': 'grad_x', 'grad_attn_norm': 'grad_w', 'grad_w_in': 'grad_w', 'grad_sgu_norm': 'grad_w', 'grad_w_spatial': 'grad_w', 'grad_b_spatial': 'grad_w', 'grad_conv_w': 'grad_w', 'grad_conv_b': 'grad_w', 'grad_kv_norm': 'grad_w', 'grad_w_ukv': 'grad_w', 'grad_q_nope_norm': 'grad_w', 'grad_q_rope_norm': 'grad_w', 'grad_k_nope_norm': 'grad_w', 'grad_k_rope_norm': 'grad_w', 'grad_out_norm': 'grad_w', 'grad_w_out': 'grad_w', 'grad_ple_norm': 'grad_w', 'grad_w_ple_gate': 'grad_w', 'grad_w_ple_proj': 'grad_w', 'delta_attn_norm': 'delta_w', 'delta_w_in': 'delta_w', 'delta_sgu_norm': 'delta_w', 'delta_w_spatial': 'delta_w', 'delta_b_spatial': 'delta_w', 'delta_conv_w': 'delta_w', 'delta_conv_b': 'delta_w', 'delta_kv_norm': 'delta_w', 'delta_w_ukv': 'delta_w', 'delta_q_nope_norm': 'delta_w', 'delta_q_rope_norm': 'delta_w', 'delta_k_nope_norm': 'delta_w', 'delta_k_rope_norm': 'delta_w', 'delta_out_norm': 'delta_w', 'delta_w_out': 'delta_w', 'delta_ple_norm': 'delta_w', 'delta_w_ple_gate': 'delta_w', 'delta_w_ple_proj': 'delta_w', 'new_m_attn_norm': 'new_m', 'new_m_w_in': 'new_m', 'new_m_sgu_norm': 'new_m', 'new_m_w_spatial': 'new_m', 'new_m_b_spatial': 'new_m', 'new_m_conv_w': 'new_m', 'new_m_conv_b': 'new_m', 'new_m_kv_norm': 'new_m', 'new_m_w_ukv': 'new_m', 'new_m_q_nope_norm': 'new_m', 'new_m_q_rope_norm': 'new_m', 'new_m_k_nope_norm': 'new_m', 'new_m_k_rope_norm': 'new_m', 'new_m_out_norm': 'new_m', 'new_m_w_out': 'new_m', 'new_m_ple_norm': 'new_m', 'new_m_w_ple_gate': 'new_m', 'new_m_w_ple_proj': 'new_m', 'new_v_attn_norm': 'new_v', 'new_v_w_in': 'new_v', 'new_v_sgu_norm': 'new_v', 'new_v_w_spatial': 'new_v', 'new_v_b_spatial': 'new_v', 'new_v_conv_w': 'new_v', 'new_v_conv_b': 'new_v', 'new_v_kv_norm': 'new_v', 'new_v_w_ukv': 'new_v', 'new_v_q_nope_norm': 'new_v', 'new_v_q_rope_norm': 'new_v', 'new_v_k_nope_norm': 'new_v', 'new_v_k_rope_norm': 'new_v', 'new_v_out_norm': 'new_v', 'new_v_w_out': 'new_v', 'new_v_ple_norm': 'new_v', 'new_v_w_ple_gate': 'new_v', 'new_v_w_ple_proj': 'new_v'}


def _forward(args):
    return _fwd_reference(*[args[k] for k in FWD_PARAMS])


def _output_shape():
    def fwd():
        inp = _fwd_setup_inputs(0)
        return _fwd_reference(*[inp[k] for k in FWD_PARAMS])
    out = _jax.eval_shape(fwd)
    return out.shape, out.dtype

N_MICROBATCH = 1
ADAM_LR = 0.001
ADAM_B1 = 0.9
ADAM_B2 = 0.999
ADAM_EPS = 1e-08
ADAM_WD = 0.01
ADAM_STEP = 10
PER_EXAMPLE_BATCH_AXIS = {'x': 0, 'p': 1, 'positions': 0, 'loss_target': 0}
SHARED_INPUTS = []
_WEIGHT_DTYPES = {'attn_norm': _jnp.float32, 'w_in': _jnp.float32, 'sgu_norm': _jnp.float32, 'w_spatial': _jnp.float32, 'b_spatial': _jnp.float32, 'conv_w': _jnp.float32, 'conv_b': _jnp.float32, 'kv_norm': _jnp.float32, 'w_ukv': _jnp.float32, 'q_nope_norm': _jnp.float32, 'q_rope_norm': _jnp.float32, 'k_nope_norm': _jnp.float32, 'k_rope_norm': _jnp.float32, 'out_norm': _jnp.float32, 'w_out': _jnp.float32, 'ple_norm': _jnp.float32, 'w_ple_gate': _jnp.float32, 'w_ple_proj': _jnp.float32}
MOMENT_SCALE = {'attn_norm': 1.418245e+00, 'w_in': 7.749196e-01, 'sgu_norm': 4.709858e-01, 'w_spatial': 4.285893e-01, 'b_spatial': 3.984354e-01, 'conv_w': 6.308708e-01, 'conv_b': 6.674360e-01, 'kv_norm': 2.928656e+00, 'w_ukv': 1.319635e+00, 'q_nope_norm': 1.304815e+00, 'q_rope_norm': 1.121491e+00, 'k_nope_norm': 1.302006e+00, 'k_rope_norm': 1.042879e+00, 'out_norm': 3.513692e+01, 'w_out': 1.391821e+00, 'ple_norm': 8.480678e-01, 'w_ple_gate': 1.446700e-01, 'w_ple_proj': 4.392115e-01}


def _to_microbatches(a, axis):
    t = _jnp.moveaxis(a, axis, 0)
    t = t.reshape((N_MICROBATCH, t.shape[0] // N_MICROBATCH) + t.shape[1:])
    return _jnp.moveaxis(t, 1, axis + 1)


def setup_inputs(seed: int = 0) -> dict:
    inp = _fwd_setup_inputs(seed)
    key = _jax.random.fold_in(_jax.random.key(seed), 7919)
    shape, _ = _output_shape()
    out = dict(inp)
    out["loss_target"] = _jax.random.normal(_jax.random.fold_in(key, 0), shape, _jnp.float32)
    for i, name in enumerate(TWIN_WEIGHTS):
        w = inp[name].astype(_jnp.float32)
        if MOMENT_SCALE is None:
            s = _jnp.sqrt(_jnp.mean(_jnp.square(w)) + 1e-30)
        else:
            s = MOMENT_SCALE[name]
        km, kv = _jax.random.split(_jax.random.fold_in(key, i + 1))
        out[name] = w
        out["m_" + name] = s * _jax.random.normal(km, w.shape, _jnp.float32)
        out["v_" + name] = (s * s) * _jax.random.uniform(kv, w.shape, _jnp.float32, 0.5, 1.5)
    if N_MICROBATCH > 1:
        for name, axis in PER_EXAMPLE_BATCH_AXIS.items():
            out[name] = _to_microbatches(out[name], axis)
    return {'x': out['x'], 'p': out['p'], 'positions': out['positions'], 'attn_norm': out['attn_norm'], 'w_in': out['w_in'], 'sgu_norm': out['sgu_norm'], 'w_spatial': out['w_spatial'], 'b_spatial': out['b_spatial'], 'conv_w': out['conv_w'], 'conv_b': out['conv_b'], 'kv_norm': out['kv_norm'], 'w_ukv': out['w_ukv'], 'q_nope_norm': out['q_nope_norm'], 'q_rope_norm': out['q_rope_norm'], 'k_nope_norm': out['k_nope_norm'], 'k_rope_norm': out['k_rope_norm'], 'out_norm': out['out_norm'], 'w_out': out['w_out'], 'ple_norm': out['ple_norm'], 'w_ple_gate': out['w_ple_gate'], 'w_ple_proj': out['w_ple_proj'], 'loss_target': out['loss_target'], 'm_attn_norm': out['m_attn_norm'], 'm_w_in': out['m_w_in'], 'm_sgu_norm': out['m_sgu_norm'], 'm_w_spatial': out['m_w_spatial'], 'm_b_spatial': out['m_b_spatial'], 'm_conv_w': out['m_conv_w'], 'm_conv_b': out['m_conv_b'], 'm_kv_norm': out['m_kv_norm'], 'm_w_ukv': out['m_w_ukv'], 'm_q_nope_norm': out['m_q_nope_norm'], 'm_q_rope_norm': out['m_q_rope_norm'], 'm_k_nope_norm': out['m_k_nope_norm'], 'm_k_rope_norm': out['m_k_rope_norm'], 'm_out_norm': out['m_out_norm'], 'm_w_out': out['m_w_out'], 'm_ple_norm': out['m_ple_norm'], 'm_w_ple_gate': out['m_w_ple_gate'], 'm_w_ple_proj': out['m_w_ple_proj'], 'v_attn_norm': out['v_attn_norm'], 'v_w_in': out['v_w_in'], 'v_sgu_norm': out['v_sgu_norm'], 'v_w_spatial': out['v_w_spatial'], 'v_b_spatial': out['v_b_spatial'], 'v_conv_w': out['v_conv_w'], 'v_conv_b': out['v_conv_b'], 'v_kv_norm': out['v_kv_norm'], 'v_w_ukv': out['v_w_ukv'], 'v_q_nope_norm': out['v_q_nope_norm'], 'v_q_rope_norm': out['v_q_rope_norm'], 'v_k_nope_norm': out['v_k_nope_norm'], 'v_k_rope_norm': out['v_k_rope_norm'], 'v_out_norm': out['v_out_norm'], 'v_w_out': out['v_w_out'], 'v_ple_norm': out['v_ple_norm'], 'v_w_ple_gate': out['v_w_ple_gate'], 'v_w_ple_proj': out['v_w_ple_proj']}


def _loss(weights, diff, rest, loss_target):
    with _jax.named_scope("forward"):
        args = {**rest, TWIN_DIFF_INPUT: diff, **{k: w.astype(_WEIGHT_DTYPES[k]) for k, w in weights.items()}}
        y = _forward(args)
    with _jax.named_scope("loss_head"):
        err = _jnp.square(y.astype(_jnp.float32) - loss_target)
        return 0.5 * _jnp.sum(_jnp.mean(err, axis=-1)) if err.ndim else 0.5 * err


def _adamw(w, g, m, v):
    m = ADAM_B1 * m + (1.0 - ADAM_B1) * g
    v = ADAM_B2 * v + (1.0 - ADAM_B2) * _jnp.square(g)
    m_hat = m / (1.0 - ADAM_B1 ** ADAM_STEP)
    v_hat = v / (1.0 - ADAM_B2 ** ADAM_STEP)
    delta = -ADAM_LR * (m_hat / (_jnp.sqrt(v_hat) + ADAM_EPS) + ADAM_WD * w)
    return delta, m, v


def reference(x, p, positions, attn_norm, w_in, sgu_norm, w_spatial, b_spatial, conv_w, conv_b, kv_norm, w_ukv, q_nope_norm, q_rope_norm, k_nope_norm, k_rope_norm, out_norm, w_out, ple_norm, w_ple_gate, w_ple_proj, loss_target, m_attn_norm, m_w_in, m_sgu_norm, m_w_spatial, m_b_spatial, m_conv_w, m_conv_b, m_kv_norm, m_w_ukv, m_q_nope_norm, m_q_rope_norm, m_k_nope_norm, m_k_rope_norm, m_out_norm, m_w_out, m_ple_norm, m_w_ple_gate, m_w_ple_proj, v_attn_norm, v_w_in, v_sgu_norm, v_w_spatial, v_b_spatial, v_conv_w, v_conv_b, v_kv_norm, v_w_ukv, v_q_nope_norm, v_q_rope_norm, v_k_nope_norm, v_k_rope_norm, v_out_norm, v_w_out, v_ple_norm, v_w_ple_gate, v_w_ple_proj):
    given = dict(x=x, p=p, positions=positions, attn_norm=attn_norm, w_in=w_in, sgu_norm=sgu_norm, w_spatial=w_spatial, b_spatial=b_spatial, conv_w=conv_w, conv_b=conv_b, kv_norm=kv_norm, w_ukv=w_ukv, q_nope_norm=q_nope_norm, q_rope_norm=q_rope_norm, k_nope_norm=k_nope_norm, k_rope_norm=k_rope_norm, out_norm=out_norm, w_out=w_out, ple_norm=ple_norm, w_ple_gate=w_ple_gate, w_ple_proj=w_ple_proj, loss_target=loss_target, m_attn_norm=m_attn_norm, m_w_in=m_w_in, m_sgu_norm=m_sgu_norm, m_w_spatial=m_w_spatial, m_b_spatial=m_b_spatial, m_conv_w=m_conv_w, m_conv_b=m_conv_b, m_kv_norm=m_kv_norm, m_w_ukv=m_w_ukv, m_q_nope_norm=m_q_nope_norm, m_q_rope_norm=m_q_rope_norm, m_k_nope_norm=m_k_nope_norm, m_k_rope_norm=m_k_rope_norm, m_out_norm=m_out_norm, m_w_out=m_w_out, m_ple_norm=m_ple_norm, m_w_ple_gate=m_w_ple_gate, m_w_ple_proj=m_w_ple_proj, v_attn_norm=v_attn_norm, v_w_in=v_w_in, v_sgu_norm=v_sgu_norm, v_w_spatial=v_w_spatial, v_b_spatial=v_b_spatial, v_conv_w=v_conv_w, v_conv_b=v_conv_b, v_kv_norm=v_kv_norm, v_w_ukv=v_w_ukv, v_q_nope_norm=v_q_nope_norm, v_q_rope_norm=v_q_rope_norm, v_k_nope_norm=v_k_nope_norm, v_k_rope_norm=v_k_rope_norm, v_out_norm=v_out_norm, v_w_out=v_w_out, v_ple_norm=v_ple_norm, v_w_ple_gate=v_w_ple_gate, v_w_ple_proj=v_w_ple_proj)
    weights = {n: given[n] for n in TWIN_WEIGHTS}
    shared = {n: given[n] for n in SHARED_INPUTS}
    per_example = {n: given[n] for n in ['x', 'p', 'positions']}
    grad_fn = _jax.value_and_grad(_loss, argnums=(0, 1))

    def one_microbatch(ex, loss_target):
        ex = dict(ex)
        diff = ex.pop(TWIN_DIFF_INPUT)
        return grad_fn(weights, diff, {**shared, **ex}, loss_target)

    if N_MICROBATCH == 1:
        loss, (grad_w, grad_x) = one_microbatch(per_example, given["loss_target"])
    else:
        def body(carry, xs):
            loss_sum, grad_sum = carry
            l_k, (gw_k, gx_k) = one_microbatch(xs[0], xs[1])
            with _jax.named_scope("update"):
                return (loss_sum + l_k, _jax.tree.map(_jnp.add, grad_sum, gw_k)), gx_k

        init = (_jnp.zeros((), _jnp.float32), _jax.tree.map(_jnp.zeros_like, weights))
        (loss, grad_w), grad_x = _jax.lax.scan(body, init, (per_example, given["loss_target"]))
    with _jax.named_scope("update"):
        delta_w, new_m, new_v = {}, {}, {}
        for n in TWIN_WEIGHTS:
            delta_w[n], new_m[n], new_v[n] = _adamw(weights[n], grad_w[n], given["m_" + n], given["v_" + n])
    return (loss, grad_x, *[grad_w[n] for n in TWIN_WEIGHTS], *[delta_w[n] for n in TWIN_WEIGHTS],
            *[new_m[n] for n in TWIN_WEIGHTS], *[new_v[n] for n in TWIN_WEIGHTS])
```

```python
import functools

import jax
import jax.numpy as jnp
from jax import lax
from jax.experimental import pallas as pl
from jax.experimental.pallas import tpu as pltpu

F32 = jnp.float32
BF16 = jnp.bfloat16
EPS = 1e-6
HEAD = 128
ROPE = 64
ROPE_BASE = 10000.0
CONV_TAPS = 3
N_SHARD = 4
N_DEV = 8
ADAM_LR = 0.001
ADAM_B1 = 0.9
ADAM_B2 = 0.999
ADAM_EPS = 1e-08
ADAM_WD = 0.01
ADAM_STEP = 10
MESH = pl.DeviceIdType.MESH
VMEM_LIMIT = 56 * 1024 * 1024
HALO_ROWS = 16

WEIGHTS = ['attn_norm', 'w_in', 'sgu_norm', 'w_spatial', 'b_spatial', 'conv_w', 'conv_b', 'kv_norm', 'w_ukv',
           'q_nope_norm', 'q_rope_norm', 'k_nope_norm', 'k_rope_norm', 'out_norm', 'w_out', 'ple_norm',
           'w_ple_gate', 'w_ple_proj']
BIG = ['w_in', 'w_ukv', 'w_out', 'w_ple_gate', 'w_ple_proj']
BIG_AXIS = {'w_in': 2, 'w_ukv': 2, 'w_out': 1, 'w_ple_gate': 1, 'w_ple_proj': 2}
SMALL = [n for n in WEIGHTS if n not in BIG]


def _pick(n, cands):
    for c in cands:
        if n % c == 0:
            return c
    return n


def _params(sem=None):
    return pltpu.CompilerParams(dimension_semantics=sem, vmem_limit_bytes=VMEM_LIMIT)


def _matmul(a, b, mode, out_dtype, name, add=None):
    if mode == 'nn':
        (m, k), n = a.shape, b.shape[1]
    elif mode == 'nt':
        (m, k), n = a.shape, b.shape[0]
    else:
        (k, m), n = a.shape, b.shape[1]
    tm = _pick(m, (1024, 512, 256, 128))
    tn = _pick(n, (1536, 1024, 512, 256, 128))
    tk = k if k <= 2048 else _pick(k, (1024, 512, 256, 128))
    nk = k // tk
    if mode == 'tn':
        a_spec = pl.BlockSpec((tk, tm), lambda i, j, kk: (kk, i))
        dims = (((0,), (0,)), ((), ()))
    else:
        a_spec = pl.BlockSpec((tm, tk), lambda i, j, kk: (i, kk))
        dims = (((1,), (0,)), ((), ())) if mode == 'nn' else (((1,), (1,)), ((), ()))
    if mode == 'nt':
        b_spec = pl.BlockSpec((tn, tk), lambda i, j, kk: (j, kk))
    else:
        b_spec = pl.BlockSpec((tk, tn), lambda i, j, kk: (kk, j))
    o_spec = pl.BlockSpec((tm, tn), lambda i, j, kk: (i, j))
    has_add = add is not None

    def body(*refs):
        a_ref, b_ref = refs[0], refs[1]
        add_ref = refs[2] if has_add else None
        o_ref = refs[3] if has_add else refs[2]
        part = lax.dot_general(a_ref[...], b_ref[...], dims, preferred_element_type=F32)

        def finish(res):
            if has_add:
                res = res + add_ref[...]
            o_ref[...] = res.astype(out_dtype)

        if nk == 1:
            finish(part)
        else:
            acc_ref = refs[-1]
            kk = pl.program_id(2)

            @pl.when(kk == 0)
            def _():
                acc_ref[...] = part

            @pl.when(kk > 0)
            def _():
                acc_ref[...] += part

            @pl.when(kk == nk - 1)
            def _():
                finish(acc_ref[...])

    in_specs = [a_spec, b_spec] + ([o_spec] if has_add else [])
    args = [a, b] + ([add] if has_add else [])
    return pl.pallas_call(
        body, name=name, grid=(m // tm, n // tn, nk), in_specs=in_specs, out_specs=o_spec,
        out_shape=jax.ShapeDtypeStruct((m, n), out_dtype),
        scratch_shapes=[pltpu.VMEM((tm, tn), F32)] if nk > 1 else [],
        compiler_params=_params(("parallel", "parallel", "arbitrary")),
    )(*args)


def _rms(x, n):
    r = lax.rsqrt(jnp.sum(x * x, axis=-1, keepdims=True) * (1.0 / n) + EPS)
    return x * r, r


def _rms_bwd(dxhat, xhat, r, n):
    return r * (dxhat - xhat * (jnp.sum(dxhat * xhat, axis=-1, keepdims=True) * (1.0 / n)))


def _sigmoid(z):
    return 1.0 / (1.0 + jnp.exp(-z))


def _silu_and_grad(z):
    sig = _sigmoid(z)
    return z * sig, sig * (1.0 + z * (1.0 - sig))


def _colsum(x):
    return jnp.sum(x, axis=0, keepdims=True)


def _rope(t, cos_t, sin_a, sin_b):
    return t * cos_t + pltpu.roll(t, 96, 1) * sin_a + pltpu.roll(t, 32, 1) * sin_b


def _rope_bwd(d, cos_t, sin_a, sin_b):
    return d * cos_t + pltpu.roll(d * sin_a, 32, 1) + pltpu.roll(d * sin_b, 96, 1)


def _shift_down(g, first_row):
    row = lax.broadcasted_iota(jnp.int32, g.shape, 0)
    return jnp.where(row == 0, first_row, pltpu.roll(g, 1, 0))


def _shift_up(g, last_row):
    n = g.shape[0]
    row = lax.broadcasted_iota(jnp.int32, g.shape, 0)
    return jnp.where(row == n - 1, last_row, pltpu.roll(g, n - 1, 0))


def _row_spec(r, w, col=0):
    return pl.BlockSpec((r, w), lambda i: (i, col))


def _const_spec(shape):
    nd = len(shape)
    return pl.BlockSpec(shape, lambda i: (0,) * nd)


def _col_block(off, w):
    assert off % w == 0, (off, w)
    return off // w


def _zero_at_first_step(refs):
    @pl.when(pl.program_id(0) == 0)
    def _():
        for ref in refs:
            ref[...] = jnp.zeros(ref.shape, ref.dtype)


def _row_call(body, name, t, r, in_specs, args, out_specs, out_shapes, scratch=()):
    return pl.pallas_call(
        body, name=name, grid=(t // r,), in_specs=in_specs, out_specs=out_specs, out_shape=out_shapes,
        scratch_shapes=list(scratch), compiler_params=_params(("arbitrary",)),
    )(*args)


def _norm_fwd(h, g, name):
    t, d = h.shape
    r = _pick(t, (256, 128))

    def body(h_ref, g_ref, o_ref):
        xhat, _ = _rms(h_ref[...], d)
        o_ref[...] = (xhat * g_ref[...]).astype(BF16)

    return _row_call(body, name, t, r, [_row_spec(r, d), _const_spec((1, d))], (h, g),
                     _row_spec(r, d), jax.ShapeDtypeStruct((t, d), BF16))


def _norm_bwd(h, g, d_hn, d_res, name):
    t, d = h.shape
    r = _pick(t, (256, 128))

    def body(h_ref, g_ref, dy_ref, dres_ref, dh_ref, dg_ref):
        _zero_at_first_step([dg_ref])
        xhat, rr = _rms(h_ref[...], d)
        dy = dy_ref[...].astype(F32)
        dg_ref[...] += _colsum(dy * xhat)
        dh_ref[...] = dres_ref[...] + _rms_bwd(dy * g_ref[...], xhat, rr, d)

    return _row_call(body, name, t, r,
                     [_row_spec(r, d), _const_spec((1, d)), _row_spec(r, d), _row_spec(r, d)], (h, g, d_hn, d_res),
                     [_row_spec(r, d), _const_spec((1, d))],
                     [jax.ShapeDtypeStruct((t, d), F32), jax.ShapeDtypeStruct((1, d), F32)])


def _sgu_scores(v, gs_ref, ws_ref, bb_ref, s_scr, r, ah, keep=None):
    for kk in range(r // HEAD):
        for hh in range(ah):
            rows, cols = slice(kk * HEAD, (kk + 1) * HEAD), slice(hh * HEAD, (hh + 1) * HEAD)
            vhat, rv = _rms(v[rows, cols], HEAD)
            vn = vhat * gs_ref[pl.ds(hh, 1), :]
            s_scr[rows, cols] = jnp.dot(ws_ref[hh], vn.astype(BF16), preferred_element_type=F32) + bb_ref[hh]
            if keep is not None:
                keep[(kk, hh)] = (vhat, rv, vn)


def _sgu_fwd(proj, off, aw, gs, ws, bb, ga, name):
    t = proj.shape[0]
    ah = aw // HEAD
    r = _pick(t, (256, 128))
    cb = _col_block(off, aw)

    def body(u_ref, v_ref, z_ref, gs_ref, ws_ref, bb_ref, ga_ref, o_ref, s_scr):
        _sgu_scores(v_ref[...].astype(F32), gs_ref, ws_ref, bb_ref, s_scr, r, ah)
        sil, _ = _silu_and_grad(z_ref[...].astype(F32))
        yhat, _ = _rms(u_ref[...].astype(F32) * s_scr[...] * sil, aw)
        o_ref[...] = (yhat * ga_ref[...]).astype(BF16)

    return _row_call(
        body, name, t, r,
        [_row_spec(r, aw, cb), _row_spec(r, aw, cb + 1), _row_spec(r, aw, cb + 2), _const_spec((ah, HEAD)),
         _const_spec((ah, HEAD, HEAD)), _const_spec((ah, HEAD, HEAD)), _const_spec((1, aw))],
        (proj, proj, proj, gs, ws, bb, ga),
        _row_spec(r, aw), jax.ShapeDtypeStruct((t, aw), BF16), scratch=[pltpu.VMEM((r, aw), F32)])


def _sgu_bwd(proj, off, aw, gs, ws, ws_t, bb, ga, dy, name):
    t = proj.shape[0]
    ah = aw // HEAD
    r = _pick(t, (256, 128))
    cb = _col_block(off, aw)

    def body(u_ref, v_ref, z_ref, gs_ref, ws_ref, wst_ref, bb_ref, ga_ref, dy_ref,
             d_ref, dgs_ref, dws_ref, db_ref, dga_ref, s_scr, dv_scr):
        _zero_at_first_step([dgs_ref, dws_ref, db_ref, dga_ref])
        keep = {}
        _sgu_scores(v_ref[...].astype(F32), gs_ref, ws_ref, bb_ref, s_scr, r, ah, keep)
        u, z, s = u_ref[...].astype(F32), z_ref[...].astype(F32), s_scr[...]
        sil, dsil = _silu_and_grad(z)
        yhat, rr = _rms(u * s * sil, aw)
        dy_f = dy_ref[...].astype(F32)
        dga_ref[...] += _colsum(dy_f * yhat)
        dya = _rms_bwd(dy_f * ga_ref[...], yhat, rr, aw)
        d_ref[:, 0:aw] = (dya * s * sil).astype(BF16)
        d_ref[:, 2 * aw:3 * aw] = (dya * u * s * dsil).astype(BF16)
        ds = dya * u * sil
        for kk in range(r // HEAD):
            for hh in range(ah):
                rows, cols = slice(kk * HEAD, (kk + 1) * HEAD), slice(hh * HEAD, (hh + 1) * HEAD)
                vhat, rv, vn = keep[(kk, hh)]
                ds_blk = ds[rows, cols]
                db_ref[hh] += jnp.sum(ds_blk, axis=1, keepdims=True)
                ds_b = ds_blk.astype(BF16)
                dws_ref[hh] += lax.dot_general(ds_b, vn.astype(BF16), (((1,), (1,)), ((), ())),
                                               preferred_element_type=F32)
                dvn = jnp.dot(wst_ref[hh], ds_b, preferred_element_type=F32)
                dgs_ref[pl.ds(hh, 1), :] += _colsum(dvn * vhat)
                dv_scr[rows, cols] = _rms_bwd(dvn * gs_ref[pl.ds(hh, 1), :], vhat, rv, HEAD)
        d_ref[:, aw:2 * aw] = dv_scr[...].astype(BF16)

    return _row_call(
        body, name, t, r,
        [_row_spec(r, aw, cb), _row_spec(r, aw, cb + 1), _row_spec(r, aw, cb + 2), _const_spec((ah, HEAD)),
         _const_spec((ah, HEAD, HEAD)), _const_spec((ah, HEAD, HEAD)), _const_spec((ah, HEAD, HEAD)),
         _const_spec((1, aw)), _row_spec(r, aw, 0)],
        (proj, proj, proj, gs, ws, ws_t, bb, ga, dy),
        [_row_spec(r, 3 * aw), _const_spec((ah, HEAD)), _const_spec((ah, HEAD, HEAD)), _const_spec((ah, HEAD, 1)),
         _const_spec((1, aw))],
        [jax.ShapeDtypeStruct((t, 3 * aw), BF16), jax.ShapeDtypeStruct((ah, HEAD), F32),
         jax.ShapeDtypeStruct((ah, HEAD, HEAD), F32), jax.ShapeDtypeStruct((ah, HEAD, 1), F32),
         jax.ShapeDtypeStruct((1, aw), F32)],
        scratch=[pltpu.VMEM((r, aw), F32), pltpu.VMEM((r, aw), F32)])


def _halo_specs(t, r, w, col, rows):
    per = r // rows
    last = t // rows - 1
    prev = pl.BlockSpec((rows, w), lambda i: (jnp.maximum(i * per - 1, 0), col))
    nxt = pl.BlockSpec((rows, w), lambda i: (jnp.minimum((i + 1) * per, last), col))
    return prev, nxt


def _edge_rows(prev_ref, next_ref, n_steps):
    i = pl.program_id(0)
    rows = prev_ref.shape[0]
    before = prev_ref[...].astype(F32)[rows - 1:rows, :] * (i > 0).astype(F32)
    after = next_ref[...].astype(F32)[0:1, :] * (i < n_steps - 1).astype(F32)
    return before, after


def _conv_fwd(proj, off, bw, cw, cb_, gb, name):
    t = proj.shape[0]
    r = _pick(t, (256, 128))
    n_steps = t // r
    c0 = _col_block(off, bw)
    cp, cn = _halo_specs(t, r, bw, c0 + 1, HALO_ROWS)
    hp, hn = _halo_specs(t, r, bw, c0 + 2, HALO_ROWS)

    def body(b_ref, c_ref, h_ref, z_ref, cp_ref, cn_ref, hp_ref, hn_ref, cw_ref, cb_ref, gb_ref, o_ref, yc_ref):
        g = c_ref[...].astype(F32) * h_ref[...].astype(F32)
        c_before, c_after = _edge_rows(cp_ref, cn_ref, n_steps)
        h_before, h_after = _edge_rows(hp_ref, hn_ref, n_steps)
        yconv = (cb_ref[...] + cw_ref[0:1, :] * _shift_down(g, c_before * h_before) + cw_ref[1:2, :] * g
                 + cw_ref[2:3, :] * _shift_up(g, c_after * h_after))
        yc_ref[...] = yconv
        sil, _ = _silu_and_grad(z_ref[...].astype(F32))
        yhat, _ = _rms(b_ref[...].astype(F32) * yconv * sil, bw)
        o_ref[...] = (yhat * gb_ref[...]).astype(BF16)

    return _row_call(
        body, name, t, r,
        [_row_spec(r, bw, c0), _row_spec(r, bw, c0 + 1), _row_spec(r, bw, c0 + 2), _row_spec(r, bw, c0 + 3),
         cp, cn, hp, hn, _const_spec((CONV_TAPS, bw)), _const_spec((1, bw)), _const_spec((1, bw))],
        (proj, proj, proj, proj, proj, proj, proj, proj, cw, cb_, gb),
        [_row_spec(r, bw), _row_spec(r, bw)],
        [jax.ShapeDtypeStruct((t, bw), BF16), jax.ShapeDtypeStruct((t, bw), F32)])


def _conv_bwd_gate(proj, off, bw, yconv, gb, dy, name):
    t = proj.shape[0]
    r = _pick(t, (256, 128))
    c0 = _col_block(off, bw)

    def body(b_ref, z_ref, yc_ref, gb_ref, dy_ref, dyc_ref, db_ref, dz_ref, dgb_ref, dcb_ref):
        _zero_at_first_step([dgb_ref, dcb_ref])
        b, z, yconv_v = b_ref[...].astype(F32), z_ref[...].astype(F32), yc_ref[...]
        sil, dsil = _silu_and_grad(z)
        yhat, rr = _rms(b * yconv_v * sil, bw)
        dy_f = dy_ref[...].astype(F32)
        dgb_ref[...] += _colsum(dy_f * yhat)
        dyb = _rms_bwd(dy_f * gb_ref[...], yhat, rr, bw)
        dyc = dyb * b * sil
        dyc_ref[...] = dyc
        dcb_ref[...] += _colsum(dyc)
        db_ref[...] = (dyb * yconv_v * sil).astype(BF16)
        dz_ref[...] = (dyb * b * yconv_v * dsil).astype(BF16)

    return _row_call(
        body, name, t, r,
        [_row_spec(r, bw, c0), _row_spec(r, bw, c0 + 3), _row_spec(r, bw), _const_spec((1, bw)), _row_spec(r, bw, 1)],
        (proj, proj, yconv, gb, dy),
        [_row_spec(r, bw), _row_spec(r, bw), _row_spec(r, bw), _const_spec((1, bw)), _const_spec((1, bw))],
        [jax.ShapeDtypeStruct((t, bw), F32), jax.ShapeDtypeStruct((t, bw), BF16), jax.ShapeDtypeStruct((t, bw), BF16),
         jax.ShapeDtypeStruct((1, bw), F32), jax.ShapeDtypeStruct((1, bw), F32)])


def _conv_bwd_taps(proj, off, bw, dyc, cw, name):
    t = proj.shape[0]
    r = _pick(t, (256, 128))
    n_steps = t // r
    c0 = _col_block(off, bw)
    cp, cn = _halo_specs(t, r, bw, c0 + 1, HALO_ROWS)
    hp, hn = _halo_specs(t, r, bw, c0 + 2, HALO_ROWS)
    dp, dn = _halo_specs(t, r, bw, 0, 8)

    def body(c_ref, h_ref, cp_ref, cn_ref, hp_ref, hn_ref, d_ref, dp_ref, dn_ref, cw_ref, dc_ref, dh_ref, dcw_ref):
        _zero_at_first_step([dcw_ref])
        c, h, d = c_ref[...].astype(F32), h_ref[...].astype(F32), d_ref[...]
        g = c * h
        c_before, c_after = _edge_rows(cp_ref, cn_ref, n_steps)
        h_before, h_after = _edge_rows(hp_ref, hn_ref, n_steps)
        d_before, d_after = _edge_rows(dp_ref, dn_ref, n_steps)
        dg = (cw_ref[0:1, :] * _shift_up(d, d_after) + cw_ref[1:2, :] * d + cw_ref[2:3, :] * _shift_down(d, d_before))
        dc_ref[...] = (dg * h).astype(BF16)
        dh_ref[...] = (dg * c).astype(BF16)
        dcw_ref[0:1, :] += _colsum(d * _shift_down(g, c_before * h_before))
        dcw_ref[1:2, :] += _colsum(d * g)
        dcw_ref[2:3, :] += _colsum(d * _shift_up(g, c_after * h_after))

    return _row_call(
        body, name, t, r,
        [_row_spec(r, bw, c0 + 1), _row_spec(r, bw, c0 + 2), cp, cn, hp, hn, _row_spec(r, bw), dp, dn,
         _const_spec((CONV_TAPS, bw))],
        (proj, proj, proj, proj, proj, proj, dyc, dyc, dyc, cw),
        [_row_spec(r, bw), _row_spec(r, bw), _const_spec((CONV_TAPS, bw))],
        [jax.ShapeDtypeStruct((t, bw), BF16), jax.ShapeDtypeStruct((t, bw), BF16),
         jax.ShapeDtypeStruct((CONV_TAPS, bw), F32)])


def _mla_prep_fwd(proj, q_off, ckv_off, kr_off, ch, kvr, tabs, qn_g, qr_g, kr_g, kv_g, name):
    t = proj.shape[0]
    r = _pick(t, (256, 128))
    qw = ch * 2 * HEAD
    cos_t, sin_a, sin_b = tabs

    def body(q_ref, ckv_ref, kr_ref, cos_ref, sa_ref, sb_ref, qn_ref, qr_ref, krg_ref, kvg_ref,
             qo_ref, co_ref, ko_ref):
        cos_v, sa, sb = cos_ref[...], sa_ref[...], sb_ref[...]
        for hh in range(ch):
            lo = hh * 2 * HEAD
            nhat, _ = _rms(q_ref[:, lo:lo + HEAD].astype(F32), HEAD)
            qo_ref[:, lo:lo + HEAD] = (nhat * qn_ref[...]).astype(BF16)
            rhat, _ = _rms(q_ref[:, lo + HEAD:lo + 2 * HEAD].astype(F32), ROPE)
            qo_ref[:, lo + HEAD:lo + 2 * HEAD] = _rope(rhat * qr_ref[...], cos_v, sa, sb).astype(BF16)
        khat, _ = _rms(kr_ref[...].astype(F32), ROPE)
        ko_ref[...] = _rope(khat * krg_ref[...], cos_v, sa, sb).astype(BF16)
        chat, _ = _rms(ckv_ref[...].astype(F32), kvr)
        co_ref[...] = (chat * kvg_ref[...]).astype(BF16)

    tab = _row_spec(r, HEAD)
    gain = _const_spec((1, HEAD))
    return _row_call(
        body, name, t, r,
        [_row_spec(r, qw, _col_block(q_off, qw)), _row_spec(r, kvr, _col_block(ckv_off, kvr)),
         _row_spec(r, HEAD, _col_block(kr_off, HEAD)), tab, tab, tab, gain, gain, gain, _const_spec((1, kvr))],
        (proj, proj, proj, cos_t, sin_a, sin_b, qn_g, qr_g, kr_g, kv_g),
        [_row_spec(r, qw), _row_spec(r, kvr), _row_spec(r, HEAD)],
        [jax.ShapeDtypeStruct((t, qw), BF16), jax.ShapeDtypeStruct((t, kvr), BF16),
         jax.ShapeDtypeStruct((t, HEAD), BF16)])


def _mla_prep_bwd(proj, q_off, ckv_off, kr_off, ch, kvr, tabs, qn_g, qr_g, kr_g, kv_g, dq_cat, dckv_n, dkr_rot, name):
    t = proj.shape[0]
    r = _pick(t, (256, 128))
    qw = ch * 2 * HEAD
    cos_t, sin_a, sin_b = tabs

    def body(q_ref, ckv_ref, kr_ref, cos_ref, sa_ref, sb_ref, qn_ref, qr_ref, krg_ref, kvg_ref,
             dq_ref, dc_ref, dk_ref, dqo_ref, dco_ref, dko_ref, dqn_ref, dqr_ref, dkrg_ref, dkvg_ref):
        _zero_at_first_step([dqn_ref, dqr_ref, dkrg_ref, dkvg_ref])
        cos_v, sa, sb = cos_ref[...], sa_ref[...], sb_ref[...]
        for hh in range(ch):
            lo = hh * 2 * HEAD
            nhat, nr = _rms(q_ref[:, lo:lo + HEAD].astype(F32), HEAD)
            d_n = dq_ref[:, lo:lo + HEAD].astype(F32)
            dqn_ref[...] += _colsum(d_n * nhat)
            dqo_ref[:, lo:lo + HEAD] = _rms_bwd(d_n * qn_ref[...], nhat, nr, HEAD).astype(BF16)
            rhat, rr = _rms(q_ref[:, lo + HEAD:lo + 2 * HEAD].astype(F32), ROPE)
            d_t = _rope_bwd(dq_ref[:, lo + HEAD:lo + 2 * HEAD].astype(F32), cos_v, sa, sb)
            dqr_ref[...] += _colsum(d_t * rhat)
            dqo_ref[:, lo + HEAD:lo + 2 * HEAD] = _rms_bwd(d_t * qr_ref[...], rhat, rr, ROPE).astype(BF16)
        khat, kr_r = _rms(kr_ref[...].astype(F32), ROPE)
        d_k = _rope_bwd(dk_ref[...], cos_v, sa, sb)
        dkrg_ref[...] += _colsum(d_k * khat)
        dko_ref[...] = _rms_bwd(d_k * krg_ref[...], khat, kr_r, ROPE).astype(BF16)
        chat, cr = _rms(ckv_ref[...].astype(F32), kvr)
        d_c = dc_ref[...].astype(F32)
        dkvg_ref[...] += _colsum(d_c * chat)
        dco_ref[...] = _rms_bwd(d_c * kvg_ref[...], chat, cr, kvr).astype(BF16)

    tab = _row_spec(r, HEAD)
    gain = _const_spec((1, HEAD))
    return _row_call(
        body, name, t, r,
        [_row_spec(r, qw, _col_block(q_off, qw)), _row_spec(r, kvr, _col_block(ckv_off, kvr)),
         _row_spec(r, HEAD, _col_block(kr_off, HEAD)), tab, tab, tab, gain, gain, gain, _const_spec((1, kvr)),
         _row_spec(r, qw), _row_spec(r, kvr), _row_spec(r, HEAD)],
        (proj, proj, proj, cos_t, sin_a, sin_b, qn_g, qr_g, kr_g, kv_g, dq_cat, dckv_n, dkr_rot),
        [_row_spec(r, qw), _row_spec(r, kvr), _row_spec(r, HEAD), gain, gain, gain, _const_spec((1, kvr))],
        [jax.ShapeDtypeStruct((t, qw), BF16), jax.ShapeDtypeStruct((t, kvr), BF16),
         jax.ShapeDtypeStruct((t, HEAD), BF16), jax.ShapeDtypeStruct((1, HEAD), F32),
         jax.ShapeDtypeStruct((1, HEAD), F32), jax.ShapeDtypeStruct((1, HEAD), F32),
         jax.ShapeDtypeStruct((1, kvr), F32)])


def _kv_prep_fwd(kv, kr_rot, ch, kn_g, name):
    t = kv.shape[0]
    r = _pick(t, (256, 128))
    qw = ch * 2 * HEAD

    def body(kv_ref, kr_ref, kn_ref, ko_ref, vo_ref):
        for hh in range(ch):
            lo = hh * 2 * HEAD
            nhat, _ = _rms(kv_ref[:, lo:lo + HEAD], HEAD)
            ko_ref[:, lo:lo + HEAD] = (nhat * kn_ref[...]).astype(BF16)
            ko_ref[:, lo + HEAD:lo + 2 * HEAD] = kr_ref[...]
            vo_ref[:, hh * HEAD:(hh + 1) * HEAD] = kv_ref[:, lo + HEAD:lo + 2 * HEAD].astype(BF16)

    return _row_call(
        body, name, t, r, [_row_spec(r, qw), _row_spec(r, HEAD), _const_spec((1, HEAD))], (kv, kr_rot, kn_g),
        [_row_spec(r, qw), _row_spec(r, ch * HEAD)],
        [jax.ShapeDtypeStruct((t, qw), BF16), jax.ShapeDtypeStruct((t, ch * HEAD), BF16)])


def _kv_prep_bwd(kv, ch, kn_g, dk_cat, dv, name):
    t = kv.shape[0]
    r = _pick(t, (256, 128))
    qw = ch * 2 * HEAD

    def body(kv_ref, kn_ref, dk_ref, dv_ref, dkv_ref, dkr_ref, dkn_ref):
        _zero_at_first_step([dkn_ref])
        dkr = jnp.zeros((r, HEAD), F32)
        for hh in range(ch):
            lo = hh * 2 * HEAD
            nhat, nr = _rms(kv_ref[:, lo:lo + HEAD], HEAD)
            d_n = dk_ref[:, lo:lo + HEAD].astype(F32)
            dkn_ref[...] += _colsum(d_n * nhat)
            dkv_ref[:, lo:lo + HEAD] = _rms_bwd(d_n * kn_ref[...], nhat, nr, HEAD).astype(BF16)
            dkv_ref[:, lo + HEAD:lo + 2 * HEAD] = dv_ref[:, hh * HEAD:(hh + 1) * HEAD]
            dkr = dkr + dk_ref[:, lo + HEAD:lo + 2 * HEAD].astype(F32)
        dkr_ref[...] = dkr

    return _row_call(
        body, name, t, r,
        [_row_spec(r, qw), _const_spec((1, HEAD)), _row_spec(r, qw), _row_spec(r, ch * HEAD)], (kv, kn_g, dk_cat, dv),
        [_row_spec(r, qw), _row_spec(r, HEAD), _const_spec((1, HEAD))],
        [jax.ShapeDtypeStruct((t, qw), BF16), jax.ShapeDtypeStruct((t, HEAD), F32),
         jax.ShapeDtypeStruct((1, HEAD), F32)])


def _attn_tiles(t):
    return _pick(t, (1024, 512, 256, 128)), _pick(t, (1024, 512, 256, 128))


_NT = (((1,), (1,)), ((), ()))
_TN = (((0,), (0,)), ((), ()))


def _attn_fwd(q_cat, k_cat, v, ch, scale, name):
    t = q_cat.shape[0]
    tq, tk = _attn_tiles(t)
    nk = t // tk

    def body(q_ref, k_ref, v_ref, o_ref, lse_ref, m_scr, l_scr, acc_scr):
        ki = pl.program_id(2)

        @pl.when(ki == 0)
        def _():
            m_scr[...] = jnp.full(m_scr.shape, -jnp.inf, F32)
            l_scr[...] = jnp.zeros(l_scr.shape, F32)
            acc_scr[...] = jnp.zeros(acc_scr.shape, F32)

        s = lax.dot_general(q_ref[...], k_ref[...], _NT, preferred_element_type=F32) * scale
        m_old = m_scr[...]
        m_new = jnp.maximum(m_old, jnp.max(s, axis=-1, keepdims=True))
        alpha = jnp.exp(m_old - m_new)
        p = jnp.exp(s - m_new)
        l_scr[...] = alpha * l_scr[...] + jnp.sum(p, axis=-1, keepdims=True)
        acc_scr[...] = alpha * acc_scr[...] + jnp.dot(p.astype(BF16), v_ref[...], preferred_element_type=F32)
        m_scr[...] = m_new

        @pl.when(ki == nk - 1)
        def _():
            o_ref[...] = (acc_scr[...] / l_scr[...]).astype(BF16)
            lse_ref[0] = m_scr[...] + jnp.log(l_scr[...])

    return pl.pallas_call(
        body, name=name, grid=(ch, t // tq, nk),
        in_specs=[pl.BlockSpec((tq, 2 * HEAD), lambda h, i, j: (i, h)),
                  pl.BlockSpec((tk, 2 * HEAD), lambda h, i, j: (j, h)),
                  pl.BlockSpec((tk, HEAD), lambda h, i, j: (j, h))],
        out_specs=[pl.BlockSpec((tq, HEAD), lambda h, i, j: (i, h)),
                   pl.BlockSpec((1, tq, 1), lambda h, i, j: (h, i, 0))],
        out_shape=[jax.ShapeDtypeStruct((t, ch * HEAD), BF16), jax.ShapeDtypeStruct((ch, t, 1), F32)],
        scratch_shapes=[pltpu.VMEM((tq, 1), F32), pltpu.VMEM((tq, 1), F32), pltpu.VMEM((tq, HEAD), F32)],
        compiler_params=_params(("parallel", "parallel", "arbitrary")),
    )(q_cat, k_cat, v)


def _attn_bwd_q(q_cat, k_cat, v, do, lse, dsum, ch, scale, name):
    t = q_cat.shape[0]
    tq, tk = _attn_tiles(t)
    nk = t // tk

    def body(q_ref, k_ref, v_ref, do_ref, lse_ref, ds_ref, dq_ref, acc_scr):
        ki = pl.program_id(2)

        @pl.when(ki == 0)
        def _():
            acc_scr[...] = jnp.zeros(acc_scr.shape, F32)

        s = lax.dot_general(q_ref[...], k_ref[...], _NT, preferred_element_type=F32) * scale
        p = jnp.exp(s - lse_ref[0])
        dp = lax.dot_general(do_ref[...], v_ref[...], _NT, preferred_element_type=F32)
        dsc = (p * (dp - ds_ref[0])).astype(BF16)
        acc_scr[...] += jnp.dot(dsc, k_ref[...], preferred_element_type=F32)

        @pl.when(ki == nk - 1)
        def _():
            dq_ref[...] = (acc_scr[...] * scale).astype(BF16)

    stat = pl.BlockSpec((1, tq, 1), lambda h, i, j: (h, i, 0))
    return pl.pallas_call(
        body, name=name, grid=(ch, t // tq, nk),
        in_specs=[pl.BlockSpec((tq, 2 * HEAD), lambda h, i, j: (i, h)),
                  pl.BlockSpec((tk, 2 * HEAD), lambda h, i, j: (j, h)),
                  pl.BlockSpec((tk, HEAD), lambda h, i, j: (j, h)),
                  pl.BlockSpec((tq, HEAD), lambda h, i, j: (i, h)), stat, stat],
        out_specs=pl.BlockSpec((tq, 2 * HEAD), lambda h, i, j: (i, h)),
        out_shape=jax.ShapeDtypeStruct((t, ch * 2 * HEAD), BF16),
        scratch_shapes=[pltpu.VMEM((tq, 2 * HEAD), F32)],
        compiler_params=_params(("parallel", "parallel", "arbitrary")),
    )(q_cat, k_cat, v, do, lse, dsum)


def _attn_bwd_kv(q_cat, k_cat, v, do, lse, dsum, ch, scale, name):
    t = q_cat.shape[0]
    tq, tk = _attn_tiles(t)
    nq = t // tq

    def body(q_ref, k_ref, v_ref, do_ref, lse_ref, ds_ref, dk_ref, dv_ref, dk_scr, dv_scr):
        qi = pl.program_id(2)

        @pl.when(qi == 0)
        def _():
            dk_scr[...] = jnp.zeros(dk_scr.shape, F32)
            dv_scr[...] = jnp.zeros(dv_scr.shape, F32)

        s = lax.dot_general(q_ref[...], k_ref[...], _NT, preferred_element_type=F32) * scale
        p = jnp.exp(s - lse_ref[0])
        dv_scr[...] += lax.dot_general(p.astype(BF16), do_ref[...], _TN, preferred_element_type=F32)
        dp = lax.dot_general(do_ref[...], v_ref[...], _NT, preferred_element_type=F32)
        dsc = (p * (dp - ds_ref[0])).astype(BF16)
        dk_scr[...] += lax.dot_general(dsc, q_ref[...], _TN, preferred_element_type=F32)

        @pl.when(qi == nq - 1)
        def _():
            dk_ref[...] = (dk_scr[...] * scale).astype(BF16)
            dv_ref[...] = dv_scr[...].astype(BF16)

    stat = pl.BlockSpec((1, tq, 1), lambda h, j, i: (h, i, 0))
    return pl.pallas_call(
        body, name=name, grid=(ch, t // tk, nq),
        in_specs=[pl.BlockSpec((tq, 2 * HEAD), lambda h, j, i: (i, h)),
                  pl.BlockSpec((tk, 2 * HEAD), lambda h, j, i: (j, h)),
                  pl.BlockSpec((tk, HEAD), lambda h, j, i: (j, h)),
                  pl.BlockSpec((tq, HEAD), lambda h, j, i: (i, h)), stat, stat],
        out_specs=[pl.BlockSpec((tk, 2 * HEAD), lambda h, j, i: (j, h)),
                   pl.BlockSpec((tk, HEAD), lambda h, j, i: (j, h))],
        out_shape=[jax.ShapeDtypeStruct((t, ch * 2 * HEAD), BF16), jax.ShapeDtypeStruct((t, ch * HEAD), BF16)],
        scratch_shapes=[pltpu.VMEM((tk, 2 * HEAD), F32), pltpu.VMEM((tk, HEAD), F32)],
        compiler_params=_params(("parallel", "parallel", "arbitrary")),
    )(q_cat, k_cat, v, do, lse, dsum)


def _attn_post_fwd(o, proj, z_off, cw, gc, name):
    t = o.shape[0]
    r = _pick(t, (256, 128))

    def body(o_ref, z_ref, gc_ref, y_ref):
        sil, _ = _silu_and_grad(z_ref[...].astype(F32))
        yhat, _ = _rms(o_ref[...].astype(F32) * sil, cw)
        y_ref[...] = (yhat * gc_ref[...]).astype(BF16)

    return _row_call(body, name, t, r,
                     [_row_spec(r, cw), _row_spec(r, cw, _col_block(z_off, cw)), _const_spec((1, cw))], (o, proj, gc),
                     _row_spec(r, cw), jax.ShapeDtypeStruct((t, cw), BF16))


def _attn_post_bwd(o, proj, z_off, cw, gc, dy, dy_col, name):
    t = o.shape[0]
    ch = cw // HEAD
    r = _pick(t, (256, 128))

    def body(o_ref, z_ref, gc_ref, dy_ref, do_ref, dz_ref, ds_ref, dgc_ref):
        _zero_at_first_step([dgc_ref])
        o_v, z = o_ref[...].astype(F32), z_ref[...].astype(F32)
        sil, dsil = _silu_and_grad(z)
        yhat, rr = _rms(o_v * sil, cw)
        dy_f = dy_ref[...].astype(F32)
        dgc_ref[...] += _colsum(dy_f * yhat)
        dyc = _rms_bwd(dy_f * gc_ref[...], yhat, rr, cw)
        do_b = (dyc * sil).astype(BF16)
        do_ref[...] = do_b
        dz_ref[...] = (dyc * o_v * dsil).astype(BF16)
        prod = do_b.astype(F32) * o_v
        for hh in range(ch):
            ds_ref[hh] = jnp.sum(prod[:, hh * HEAD:(hh + 1) * HEAD], axis=-1, keepdims=True)

    return _row_call(
        body, name, t, r,
        [_row_spec(r, cw), _row_spec(r, cw, _col_block(z_off, cw)), _const_spec((1, cw)), _row_spec(r, cw, dy_col)],
        (o, proj, gc, dy),
        [_row_spec(r, cw), _row_spec(r, cw), pl.BlockSpec((ch, r, 1), lambda i: (0, i, 0)), _const_spec((1, cw))],
        [jax.ShapeDtypeStruct((t, cw), BF16), jax.ShapeDtypeStruct((t, cw), BF16),
         jax.ShapeDtypeStruct((ch, t, 1), F32), jax.ShapeDtypeStruct((1, cw), F32)])


def _ple_fwd(h1, gpre, pp, name):
    t, d = h1.shape
    r = _pick(t, (256, 128))

    def body(h_ref, g_ref, p_ref, o_ref):
        o_ref[...] = h_ref[...] + _sigmoid(g_ref[...]) * p_ref[...]

    return _row_call(body, name, t, r, [_row_spec(r, d)] * 3, (h1, gpre, pp), _row_spec(r, d),
                     jax.ShapeDtypeStruct((t, d), F32))


def _ple_bwd(gpre, pp, dh, name):
    t, d = dh.shape
    r = _pick(t, (256, 128))

    def body(g_ref, p_ref, dh_ref, dg_ref, dp_ref):
        sig = _sigmoid(g_ref[...])
        dh_v = dh_ref[...]
        dg_ref[...] = (dh_v * p_ref[...] * sig * (1.0 - sig)).astype(BF16)
        dp_ref[...] = (dh_v * sig).astype(BF16)

    return _row_call(body, name, t, r, [_row_spec(r, d)] * 3, (gpre, pp, dh), [_row_spec(r, d)] * 2,
                     [jax.ShapeDtypeStruct((t, d), BF16)] * 2)


def _loss_and_grad(h, target, name):
    t, d = h.shape
    r = _pick(t, (256, 128))

    def body(h_ref, t_ref, l_ref, dh_ref):
        _zero_at_first_step([l_ref])
        err = h_ref[...] - t_ref[...]
        l_ref[...] += jnp.sum(jnp.sum(err * err, axis=-1, keepdims=True), axis=0, keepdims=True) * (0.5 / d)
        dh_ref[...] = err * (1.0 / d)

    return _row_call(body, name, t, r, [_row_spec(r, d)] * 2, (h, target), [_const_spec((1, 1)), _row_spec(r, d)],
                     [jax.ShapeDtypeStruct((1, 1), F32), jax.ShapeDtypeStruct((t, d), F32)])


def _ew_rows(rows, cols):
    cap = max(8, (1 << 19) // max(cols, 1))
    for cand in (1024, 512, 256, 128, 64, 32, 16, 8):
        if cand <= cap and rows % cand == 0:
            return cand
    return rows


def _pair_sum_bf16(a, b, name):
    n, rows, cols = a.shape
    rb = _ew_rows(rows, cols)

    def body(a_ref, b_ref, o_ref):
        o_ref[...] = (a_ref[...] + b_ref[...]).astype(BF16)

    spec = pl.BlockSpec((1, rb, cols), lambda s, i: (s, i, 0))
    return pl.pallas_call(body, name=name, grid=(n, rows // rb), in_specs=[spec, spec], out_specs=spec,
                          out_shape=jax.ShapeDtypeStruct(a.shape, BF16),
                          compiler_params=_params(("parallel", "parallel")))(a, b)


def _shard_sum(a, b, recv, name):
    rows, cols = a.shape
    rb = _ew_rows(rows, cols)

    def body(a_ref, b_ref, r_ref, o_ref):
        o_ref[...] = ((a_ref[...] + b_ref[...]) + r_ref[0].astype(F32) + r_ref[1].astype(F32)
                      + r_ref[2].astype(F32))

    spec = pl.BlockSpec((rb, cols), lambda i: (i, 0))
    return pl.pallas_call(body, name=name, grid=(rows // rb,),
                          in_specs=[spec, spec, pl.BlockSpec((N_SHARD - 1, rb, cols), lambda i: (0, i, 0))],
                          out_specs=spec, out_shape=jax.ShapeDtypeStruct(a.shape, F32),
                          compiler_params=_params(("parallel",)))(a, b, recv)


def _sum_devices(g, name):
    n, rows, cols = g.shape
    rb = _ew_rows(rows, cols)

    def body(g_ref, o_ref):
        acc = g_ref[0]
        for k in range(1, n):
            acc = acc + g_ref[k]
        o_ref[...] = acc

    return pl.pallas_call(body, name=name, grid=(rows // rb,),
                          in_specs=[pl.BlockSpec((n, rb, cols), lambda i: (0, i, 0))],
                          out_specs=pl.BlockSpec((rb, cols), lambda i: (i, 0)),
                          out_shape=jax.ShapeDtypeStruct((rows, cols), F32),
                          compiler_params=_params(("parallel",)))(g)


def _adamw(w, g, m, v, name):
    rows, cols = w.shape
    rb = _ew_rows(rows, cols)

    def body(w_ref, g_ref, m_ref, v_ref, d_ref, mo_ref, vo_ref):
        g_v = g_ref[...]
        m_new = ADAM_B1 * m_ref[...] + (1.0 - ADAM_B1) * g_v
        v_new = ADAM_B2 * v_ref[...] + (1.0 - ADAM_B2) * (g_v * g_v)
        m_hat = m_new / (1.0 - ADAM_B1 ** ADAM_STEP)
        v_hat = v_new / (1.0 - ADAM_B2 ** ADAM_STEP)
        d_ref[...] = -ADAM_LR * (m_hat / (jnp.sqrt(v_hat) + ADAM_EPS) + ADAM_WD * w_ref[...])
        mo_ref[...] = m_new
        vo_ref[...] = v_new

    spec = pl.BlockSpec((rb, cols), lambda i: (i, 0))
    return pl.pallas_call(body, name=name, grid=(rows // rb,), in_specs=[spec] * 4, out_specs=[spec] * 3,
                          out_shape=[jax.ShapeDtypeStruct(w.shape, F32)] * 3,
                          compiler_params=_params(("parallel",)))(w, g, m, v)


def _place():
    return lax.axis_index("x"), lax.axis_index("y"), lax.axis_index("c")


def _other_chips(x, y):
    return [(1 - x, y), (x, 1 - y), (1 - x, 1 - y)]


_ANY = pl.BlockSpec(memory_space=pl.ANY)


def _gather_shards(shards, name):
    n = len(shards)
    n_peer = N_SHARD - 1

    def body(*refs):
        ins, outs = refs[:n], refs[n:2 * n]
        send_sems, recv_sems, local_sems = refs[2 * n:]
        x, y, c = _place()
        chips = _other_chips(x, y)
        started = []
        for a in range(n):
            mine = pltpu.make_async_copy(ins[a], outs[a].at[2 * x + y], local_sems.at[a])
            mine.start()
            started.append(mine)
        sends = []
        for a in range(n):
            for k, (px, py) in enumerate(chips):
                cp = pltpu.make_async_remote_copy(
                    src_ref=ins[a], dst_ref=outs[a].at[2 * x + y], send_sem=send_sems.at[a * n_peer + k],
                    recv_sem=recv_sems.at[a * n_peer + k], device_id=(px, py, c), device_id_type=MESH)
                cp.start()
                sends.append(cp)
        for a in range(n):
            for k, (px, py) in enumerate(chips):
                pltpu.make_async_remote_copy(
                    src_ref=ins[a], dst_ref=outs[a].at[2 * px + py], send_sem=send_sems.at[a * n_peer + k],
                    recv_sem=recv_sems.at[a * n_peer + k], device_id=(px, py, c), device_id_type=MESH).wait_recv()
        for cp in sends:
            cp.wait_send()
        for cp in started:
            cp.wait()

    return pl.pallas_call(
        body, name=name, in_specs=[_ANY] * n, out_specs=[_ANY] * n,
        out_shape=[jax.ShapeDtypeStruct((N_SHARD,) + s.shape, s.dtype) for s in shards],
        scratch_shapes=[pltpu.SemaphoreType.DMA((n * n_peer,)), pltpu.SemaphoreType.DMA((n * n_peer,)),
                        pltpu.SemaphoreType.DMA((n,))],
    )(*shards)


def _to_sibling_half(grads, name):
    n = len(grads)

    def body(*refs):
        ins, outs = refs[:n], refs[n:2 * n]
        send_sems, recv_sems = refs[2 * n:]
        x, y, c = _place()
        sends = []
        for a in range(n):
            cp = pltpu.make_async_remote_copy(
                src_ref=ins[a].at[1 - c], dst_ref=outs[a], send_sem=send_sems.at[a], recv_sem=recv_sems.at[a],
                device_id=(x, y, 1 - c), device_id_type=MESH)
            cp.start()
            sends.append(cp)
        for cp in sends:
            cp.wait_recv()
        for cp in sends:
            cp.wait_send()

    return pl.pallas_call(
        body, name=name, in_specs=[_ANY] * n, out_specs=[_ANY] * n,
        out_shape=[jax.ShapeDtypeStruct(g.shape[1:], g.dtype) for g in grads],
        scratch_shapes=[pltpu.SemaphoreType.DMA((n,)), pltpu.SemaphoreType.DMA((n,))],
    )(*grads)


def _to_owner_chips(parts, name):
    n = len(parts)
    n_peer = N_SHARD - 1

    def body(*refs):
        ins, outs = refs[:n], refs[n:2 * n]
        send_sems, recv_sems = refs[2 * n:]
        x, y, c = _place()
        chips = _other_chips(x, y)
        sends = []
        for a in range(n):
            for k, (px, py) in enumerate(chips):
                cp = pltpu.make_async_remote_copy(
                    src_ref=ins[a].at[2 * px + py], dst_ref=outs[a].at[k], send_sem=send_sems.at[a * n_peer + k],
                    recv_sem=recv_sems.at[a * n_peer + k], device_id=(px, py, c), device_id_type=MESH)
                cp.start()
                sends.append(cp)
        for cp in sends:
            cp.wait_recv()
        for cp in sends:
            cp.wait_send()

    return pl.pallas_call(
        body, name=name, in_specs=[_ANY] * n, out_specs=[_ANY] * n,
        out_shape=[jax.ShapeDtypeStruct((n_peer,) + p.shape[1:], p.dtype) for p in parts],
        scratch_shapes=[pltpu.SemaphoreType.DMA((n * n_peer,)), pltpu.SemaphoreType.DMA((n * n_peer,))],
    )(*parts)


def _share_with_sibling(halves, name):
    n = len(halves)

    def body(*refs):
        ins, outs = refs[:n], refs[n:2 * n]
        send_sems, recv_sems, local_sems = refs[2 * n:]
        x, y, c = _place()
        copies = []
        for a in range(n):
            mine = pltpu.make_async_copy(ins[a], outs[a].at[c], local_sems.at[a])
            mine.start()
            copies.append(mine)
        sends = []
        for a in range(n):
            cp = pltpu.make_async_remote_copy(
                src_ref=ins[a], dst_ref=outs[a].at[c], send_sem=send_sems.at[a], recv_sem=recv_sems.at[a],
                device_id=(x, y, 1 - c), device_id_type=MESH)
            cp.start()
            sends.append(cp)
        for a in range(n):
            pltpu.make_async_remote_copy(
                src_ref=ins[a], dst_ref=outs[a].at[1 - c], send_sem=send_sems.at[a], recv_sem=recv_sems.at[a],
                device_id=(x, y, 1 - c), device_id_type=MESH).wait_recv()
        for cp in sends:
            cp.wait_send()
        for cp in copies:
            cp.wait()

    return pl.pallas_call(
        body, name=name, in_specs=[_ANY] * n, out_specs=[_ANY] * n,
        out_shape=[jax.ShapeDtypeStruct((2,) + h.shape, h.dtype) for h in halves],
        scratch_shapes=[pltpu.SemaphoreType.DMA((n,)), pltpu.SemaphoreType.DMA((n,)), pltpu.SemaphoreType.DMA((n,))],
    )(*halves)


def _gather_devices(buf, name):
    n_peer = N_DEV - 1

    def body(in_ref, out_ref, send_sems, recv_sems, local_sem):
        x, y, c = _place()
        me = 4 * x + 2 * y + c
        mine = pltpu.make_async_copy(in_ref, out_ref.at[me], local_sem)
        mine.start()
        peers = []
        for k in range(1, N_DEV):
            fx, fy, fc = (k >> 2) & 1, (k >> 1) & 1, k & 1
            peers.append((x ^ fx, y ^ fy, c ^ fc))
        sends = []
        for k, peer in enumerate(peers):
            cp = pltpu.make_async_remote_copy(
                src_ref=in_ref, dst_ref=out_ref.at[me], send_sem=send_sems.at[k], recv_sem=recv_sems.at[k],
                device_id=peer, device_id_type=MESH)
            cp.start()
            sends.append(cp)
        for k, (px, py, pc) in enumerate(peers):
            pltpu.make_async_remote_copy(
                src_ref=in_ref, dst_ref=out_ref.at[4 * px + 2 * py + pc], send_sem=send_sems.at[k],
                recv_sem=recv_sems.at[k], device_id=(px, py, pc), device_id_type=MESH).wait_recv()
        for cp in sends:
            cp.wait_send()
        mine.wait()

    return pl.pallas_call(
        body, name=name, in_specs=[_ANY], out_specs=_ANY,
        out_shape=jax.ShapeDtypeStruct((N_DEV,) + buf.shape, buf.dtype),
        scratch_shapes=[pltpu.SemaphoreType.DMA((n_peer,)), pltpu.SemaphoreType.DMA((n_peer,)),
                        pltpu.SemaphoreType.DMA(())],
    )(buf)


class _Dims:
    def __init__(self, x, p, w_in, sgu_norm, conv_w, kv_norm, w_ukv, w_out):
        self.t, self.d = x.shape[1], x.shape[2]
        self.depth = w_in.shape[0]
        self.ple = p.shape[3]
        self.in_w = w_in.shape[2] * N_SHARD
        self.ah = sgu_norm.shape[1]
        self.aw = self.ah * HEAD
        self.bw = conv_w.shape[2] * N_SHARD
        self.kvr = kv_norm.shape[1]
        self.ch = w_ukv.shape[2] * N_SHARD // (2 * HEAD)
        self.cw = self.ch * HEAD
        self.mix = w_out.shape[1] * N_SHARD
        assert self.mix == self.aw + self.bw + self.cw and self.aw == self.bw
        self.qw = self.ch * 2 * HEAD
        segs = [('a', 3 * self.aw, self.aw), ('b', 4 * self.bw, self.bw), ('ckv', self.kvr, self.kvr),
                ('q', self.qw, self.qw), ('cz', self.cw, self.cw), ('kr', HEAD, HEAD)]
        off = 0
        self.off = {}
        for nm, width, align in segs:
            off = -(-off // align) * align
            self.off[nm] = off
            off += width
        self.inp = -(-off // 512) * 512
        q_real = self.ch * (HEAD + ROPE)
        widths = [3 * self.aw, 4 * self.bw, q_real, self.kvr, ROPE, self.cw]
        assert sum(widths) == self.in_w
        starts = [0]
        for wd in widths:
            starts.append(starts[-1] + wd)
        self.src = dict(zip(['a', 'b', 'q', 'ckv', 'kr', 'cz'], zip(starts[:-1], widths)))


def _rearrange_w_in(w, dm):
    lead = w.shape[:-1]
    pieces = {}
    for nm in ('a', 'b', 'ckv', 'cz'):
        s, wd = dm.src[nm]
        pieces[nm] = w[..., s:s + wd]
    s, wd = dm.src['q']
    q = w[..., s:s + wd].reshape(lead + (dm.ch, HEAD + ROPE))
    pieces['q'] = jnp.pad(q, [(0, 0)] * (q.ndim - 1) + [(0, 2 * HEAD - HEAD - ROPE)]).reshape(lead + (dm.qw,))
    s, wd = dm.src['kr']
    pieces['kr'] = jnp.pad(w[..., s:s + wd], [(0, 0)] * len(lead) + [(0, HEAD - ROPE)])
    out, cur = [], 0
    for nm in sorted(dm.off, key=lambda k: dm.off[k]):
        if dm.off[nm] > cur:
            out.append(jnp.zeros(lead + (dm.off[nm] - cur,), w.dtype))
        out.append(pieces[nm])
        cur = dm.off[nm] + pieces[nm].shape[-1]
    if dm.inp > cur:
        out.append(jnp.zeros(lead + (dm.inp - cur,), w.dtype))
    return jnp.concatenate(out, axis=-1)


def _unarrange_w_in(g, dm):
    lead = g.shape[:-1]

    def seg(nm, width):
        return g[..., dm.off[nm]:dm.off[nm] + width]

    q = seg('q', dm.qw).reshape(lead + (dm.ch, 2 * HEAD))[..., :HEAD + ROPE].reshape(lead + (dm.ch * (HEAD + ROPE),))
    return jnp.concatenate([seg('a', 3 * dm.aw), seg('b', 4 * dm.bw), q, seg('ckv', dm.kvr), seg('kr', ROPE),
                            seg('cz', dm.cw)], axis=-1)


def _assemble_dproj(parts, dm, t):
    out, cur = [], 0
    for nm in sorted(dm.off, key=lambda k: dm.off[k]):
        if dm.off[nm] > cur:
            out.append(jnp.zeros((t, dm.off[nm] - cur), BF16))
        out.append(parts[nm])
        cur = dm.off[nm] + parts[nm].shape[-1]
    if dm.inp > cur:
        out.append(jnp.zeros((t, dm.inp - cur), BF16))
    return jnp.concatenate(out, axis=-1)


def _rope_tables(positions):
    inv = 1.0 / (ROPE_BASE ** (jnp.arange(0, ROPE, 2, dtype=F32) / ROPE))
    ang = positions.astype(F32)[:, None] * inv
    cos, sin = jnp.cos(ang), jnp.sin(ang)
    t = positions.shape[0]
    half = ROPE // 2
    cos_t = jnp.concatenate([cos, cos, jnp.zeros((t, HEAD - ROPE), F32)], axis=-1)
    sin_a = jnp.concatenate([-sin, jnp.zeros((t, HEAD - half), F32)], axis=-1)
    sin_b = jnp.concatenate([jnp.zeros((t, half), F32), sin, jnp.zeros((t, HEAD - ROPE), F32)], axis=-1)
    return cos_t, sin_a, sin_b


def _pad_gain(g):
    return jnp.pad(g, (0, HEAD - g.shape[0]))[None, :]


def _shard_major(g, axis):
    shape = g.shape
    g = g.reshape(shape[:axis] + (N_SHARD, shape[axis] // N_SHARD) + shape[axis + 1:])
    g = jnp.moveaxis(g, axis, 0)
    cols = g.shape[-1]
    g = g.reshape(N_SHARD, 2, -1, cols)
    return jnp.swapaxes(g, 0, 1)


def _pack(arrs):
    flat = jnp.concatenate([a.reshape(-1) for a in arrs])
    pad = (-flat.shape[0]) % (8 * HEAD)
    return jnp.pad(flat, (0, pad)).reshape(-1, HEAD)


def _unpack(buf, shapes):
    flat = buf.reshape(-1)
    out, cur = [], 0
    for s in shapes:
        size = 1
        for v in s:
            size *= v
        out.append(flat[cur:cur + size].reshape(s))
        cur += size
    return out


def kernel(x, p, positions, attn_norm, w_in, sgu_norm, w_spatial, b_spatial, conv_w, conv_b, kv_norm, w_ukv, q_nope_norm, q_rope_norm, k_nope_norm, k_rope_norm, out_norm, w_out, ple_norm, w_ple_gate, w_ple_proj, loss_target, m_attn_norm, m_w_in, m_sgu_norm, m_w_spatial, m_b_spatial, m_conv_w, m_conv_b, m_kv_norm, m_w_ukv, m_q_nope_norm, m_q_rope_norm, m_k_nope_norm, m_k_rope_norm, m_out_norm, m_w_out, m_ple_norm, m_w_ple_gate, m_w_ple_proj, v_attn_norm, v_w_in, v_sgu_norm, v_w_spatial, v_b_spatial, v_conv_w, v_conv_b, v_kv_norm, v_w_ukv, v_q_nope_norm, v_q_rope_norm, v_k_nope_norm, v_k_rope_norm, v_out_norm, v_w_out, v_ple_norm, v_w_ple_gate, v_w_ple_proj):
    weights = dict(attn_norm=attn_norm, w_in=w_in, sgu_norm=sgu_norm, w_spatial=w_spatial, b_spatial=b_spatial,
                   conv_w=conv_w, conv_b=conv_b, kv_norm=kv_norm, w_ukv=w_ukv, q_nope_norm=q_nope_norm,
                   q_rope_norm=q_rope_norm, k_nope_norm=k_nope_norm, k_rope_norm=k_rope_norm, out_norm=out_norm,
                   w_out=w_out, ple_norm=ple_norm, w_ple_gate=w_ple_gate, w_ple_proj=w_ple_proj)
    mom_m = dict(attn_norm=m_attn_norm, w_in=m_w_in, sgu_norm=m_sgu_norm, w_spatial=m_w_spatial,
                 b_spatial=m_b_spatial, conv_w=m_conv_w, conv_b=m_conv_b, kv_norm=m_kv_norm, w_ukv=m_w_ukv,
                 q_nope_norm=m_q_nope_norm, q_rope_norm=m_q_rope_norm, k_nope_norm=m_k_nope_norm,
                 k_rope_norm=m_k_rope_norm, out_norm=m_out_norm, w_out=m_w_out, ple_norm=m_ple_norm,
                 w_ple_gate=m_w_ple_gate, w_ple_proj=m_w_ple_proj)
    mom_v = dict(attn_norm=v_attn_norm, w_in=v_w_in, sgu_norm=v_sgu_norm, w_spatial=v_w_spatial,
                 b_spatial=v_b_spatial, conv_w=v_conv_w, conv_b=v_conv_b, kv_norm=v_kv_norm, w_ukv=v_w_ukv,
                 q_nope_norm=v_q_nope_norm, q_rope_norm=v_q_rope_norm, k_nope_norm=v_k_nope_norm,
                 k_rope_norm=v_k_rope_norm, out_norm=v_out_norm, w_out=v_w_out, ple_norm=v_ple_norm,
                 w_ple_gate=v_w_ple_gate, w_ple_proj=v_w_ple_proj)
    dm = _Dims(x, p, w_in, sgu_norm, conv_w, kv_norm, w_ukv, w_out)
    t, d, depth = dm.t, dm.d, dm.depth
    shard = 2 * lax.axis_index("x") + lax.axis_index("y")
    core = lax.axis_index("c")
    scale = float(HEAD + ROPE) ** -0.5

    gathered = _gather_shards([weights[n].astype(BF16) for n in BIG] + [conv_w], "gather_weights")
    full = {n: jnp.concatenate([gathered[i][s] for s in range(N_SHARD)], axis=BIG_AXIS[n])
            for i, n in enumerate(BIG)}
    conv_w_full = jnp.concatenate([gathered[len(BIG)][s] for s in range(N_SHARD)], axis=2)
    w_in_r = _rearrange_w_in(full['w_in'], dm)

    tabs = _rope_tables(positions[0])
    h = x[0]
    saved = []
    for i in range(depth):
        tag = f"l{i}_"
        ga, gb, gc = (out_norm[i][None, :dm.aw], out_norm[i][None, dm.aw:dm.aw + dm.bw],
                      out_norm[i][None, dm.aw + dm.bw:])
        ws_b = w_spatial[i].astype(BF16)
        bb = jnp.broadcast_to(b_spatial[i][:, :, None], (dm.ah, HEAD, HEAD))
        qn_g, qr_g = q_nope_norm[i][None, :], _pad_gain(q_rope_norm[i])
        kn_g, kr_g = k_nope_norm[i][None, :], _pad_gain(k_rope_norm[i])
        kv_g = kv_norm[i][None, :]
        hn = _norm_fwd(h, attn_norm[i][None, :], tag + "norm1")
        proj = _matmul(hn, w_in_r[i], 'nn', BF16, tag + "proj")
        ya = _sgu_fwd(proj, dm.off['a'], dm.aw, sgu_norm[i], ws_b, bb, ga, tag + "sgu")
        yb, yconv = _conv_fwd(proj, dm.off['b'], dm.bw, conv_w_full[i], conv_b[i][None, :], gb, tag + "conv")
        q_cat, ckv_n, kr_rot = _mla_prep_fwd(proj, dm.off['q'], dm.off['ckv'], dm.off['kr'], dm.ch, dm.kvr, tabs,
                                             qn_g, qr_g, kr_g, kv_g, tag + "mla_prep")
        kv = _matmul(ckv_n, full['w_ukv'][i], 'nn', F32, tag + "kv_up")
        k_cat, v_heads = _kv_prep_fwd(kv, kr_rot, dm.ch, kn_g, tag + "kv_prep")
        o, lse = _attn_fwd(q_cat, k_cat, v_heads, dm.ch, scale, tag + "attn")
        yc = _attn_post_fwd(o, proj, dm.off['cz'], dm.cw, gc, tag + "attn_post")
        y = jnp.concatenate([ya, yb, yc], axis=-1)
        h1 = _matmul(y, full['w_out'][i], 'nn', F32, tag + "out", add=h)
        hn2 = _norm_fwd(h1, ple_norm[i][None, :], tag + "norm2")
        gpre = _matmul(hn2, full['w_ple_gate'][i], 'nn', F32, tag + "gate")
        p_b = p[i, 0].astype(BF16)
        pp = _matmul(p_b, full['w_ple_proj'][i], 'nn', F32, tag + "ple_proj")
        h2 = _ple_fwd(h1, gpre, pp, tag + "ple")
        saved.append(dict(h=h, hn=hn, proj=proj, yconv=yconv, q_cat=q_cat, ckv_n=ckv_n, kv=kv, k_cat=k_cat,
                          v=v_heads, o=o, lse=lse, y=y, h1=h1, hn2=hn2, gpre=gpre, pp=pp, p_b=p_b, ws_b=ws_b, bb=bb,
                          gains=(ga, gb, gc, qn_g, qr_g, kn_g, kr_g, kv_g)))
        h = h2

    loss_part, dh = _loss_and_grad(h, loss_target[0], "loss")
    loss = lax.psum(loss_part[0, 0], ("x", "y", "c"))

    grads = {n: [None] * depth for n in WEIGHTS}
    for i in reversed(range(depth)):
        tag = f"l{i}_b_"
        sv = saved[i]
        ga, gb, gc, qn_g, qr_g, kn_g, kr_g, kv_g = sv['gains']
        proj = sv['proj']
        dgpre, dpp = _ple_bwd(sv['gpre'], sv['pp'], dh, tag + "ple")
        grads['w_ple_proj'][i] = _matmul(sv['p_b'], dpp, 'tn', F32, tag + "d_w_ple_proj")
        grads['w_ple_gate'][i] = _matmul(sv['hn2'], dgpre, 'tn', F32, tag + "d_w_gate")
        d_hn2 = _matmul(dgpre, full['w_ple_gate'][i], 'nt', BF16, tag + "d_hn2")
        dh1, g_ple = _norm_bwd(sv['h1'], ple_norm[i][None, :], d_hn2, dh, tag + "norm2")
        grads['ple_norm'][i] = g_ple[0]
        dh1_b = dh1.astype(BF16)
        grads['w_out'][i] = _matmul(sv['y'], dh1_b, 'tn', F32, tag + "d_w_out")
        dy = _matmul(dh1_b, full['w_out'][i], 'nt', BF16, tag + "d_y")
        ws_t = jnp.swapaxes(sv['ws_b'], 1, 2)
        d_a, g_sgu, g_ws, g_bs, g_ga = _sgu_bwd(proj, dm.off['a'], dm.aw, sgu_norm[i], sv['ws_b'], ws_t, sv['bb'],
                                                ga, dy, tag + "sgu")
        grads['sgu_norm'][i], grads['w_spatial'][i], grads['b_spatial'][i] = g_sgu, g_ws, g_bs[:, :, 0]
        dyc, d_bb, d_bz, g_gb, g_cb = _conv_bwd_gate(proj, dm.off['b'], dm.bw, sv['yconv'], gb, dy, tag + "conv_gate")
        d_bc, d_bh, g_cw = _conv_bwd_taps(proj, dm.off['b'], dm.bw, dyc, conv_w_full[i], tag + "conv_taps")
        grads['conv_b'][i], grads['conv_w'][i] = g_cb[0], g_cw
        d_b = jnp.concatenate([d_bb, d_bc, d_bh, d_bz], axis=-1)
        d_o, d_cz, dsum, g_gc = _attn_post_bwd(sv['o'], proj, dm.off['cz'], dm.cw, gc, dy,
                                               _col_block(dm.aw + dm.bw, dm.cw), tag + "attn_post")
        grads['out_norm'][i] = jnp.concatenate([g_ga[0], g_gb[0], g_gc[0]])
        dq_cat = _attn_bwd_q(sv['q_cat'], sv['k_cat'], sv['v'], d_o, sv['lse'], dsum, dm.ch, scale, tag + "attn_dq")
        dk_cat, dv = _attn_bwd_kv(sv['q_cat'], sv['k_cat'], sv['v'], d_o, sv['lse'], dsum, dm.ch, scale,
                                  tag + "attn_dkv")
        dkv, dkr_rot, g_kn = _kv_prep_bwd(sv['kv'], dm.ch, kn_g, dk_cat, dv, tag + "kv_prep")
        grads['k_nope_norm'][i] = g_kn[0]
        grads['w_ukv'][i] = _matmul(sv['ckv_n'], dkv, 'tn', F32, tag + "d_w_ukv")
        dckv_n = _matmul(dkv, full['w_ukv'][i], 'nt', BF16, tag + "d_ckv")
        d_q, d_ckv, d_kr, g_qn, g_qr, g_kr, g_kv = _mla_prep_bwd(
            proj, dm.off['q'], dm.off['ckv'], dm.off['kr'], dm.ch, dm.kvr, tabs, qn_g, qr_g, kr_g, kv_g,
            dq_cat, dckv_n, dkr_rot, tag + "mla_prep")
        grads['q_nope_norm'][i], grads['q_rope_norm'][i] = g_qn[0], g_qr[0, :ROPE]
        grads['k_rope_norm'][i], grads['kv_norm'][i] = g_kr[0, :ROPE], g_kv[0]
        dproj = _assemble_dproj(dict(a=d_a, b=d_b, ckv=d_ckv, q=d_q, cz=d_cz, kr=d_kr), dm, t)
        grads['w_in'][i] = _unarrange_w_in(_matmul(sv['hn'], dproj, 'tn', F32, tag + "d_w_in"), dm)
        d_hn = _matmul(dproj, w_in_r[i], 'nt', BF16, tag + "d_hn")
        dh, g_an = _norm_bwd(sv['h'], attn_norm[i][None, :], d_hn, dh1, tag + "norm1")
        grads['attn_norm'][i] = g_an[0]
    grad_x = dh[None]
    grads = {n: jnp.stack(grads[n]) for n in WEIGHTS}

    sm = [_shard_major(grads[n], BIG_AXIS[n]) for n in BIG]
    from_sibling = _to_sibling_half(sm, "grads_to_sibling")
    mine = [lax.dynamic_index_in_dim(g, core, 0, keepdims=False) for g in sm]
    chip_sums = [_pair_sum_bf16(a, b, f"chip_sum_{n}") for a, b, n in zip(mine, from_sibling, BIG)]
    from_chips = _to_owner_chips(chip_sums, "grads_to_owner_chips")
    halves = []
    for a, b, r3, n in zip(mine, from_sibling, from_chips, BIG):
        own_a = lax.dynamic_index_in_dim(a, shard, 0, keepdims=False)
        own_b = lax.dynamic_index_in_dim(b, shard, 0, keepdims=False)
        halves.append(_shard_sum(own_a, own_b, r3, f"shard_sum_{n}"))
    both = _share_with_sibling(halves, "grads_share_sibling")
    out_g, out_d, out_m, out_v = {}, {}, {}, {}
    for n, g2 in zip(BIG, both):
        shp = weights[n].shape
        cols = shp[-1]
        g_flat = g2.reshape(-1, cols)
        d_w, m_new, v_new = _adamw(weights[n].reshape(-1, cols), g_flat, mom_m[n].reshape(-1, cols),
                                   mom_v[n].reshape(-1, cols), f"adamw_{n}")
        out_g[n], out_d[n], out_m[n], out_v[n] = (g_flat.reshape(shp), d_w.reshape(shp), m_new.reshape(shp),
                                                  v_new.reshape(shp))

    shapes = [grads[n].shape for n in SMALL]
    summed = _unpack(_sum_devices(_gather_devices(_pack([grads[n] for n in SMALL]), "gather_small_grads"),
                                  "sum_small_grads"), shapes)
    small_g = dict(zip(SMALL, summed))
    small_g['conv_w'] = lax.dynamic_slice_in_dim(small_g['conv_w'], shard * conv_w.shape[2], conv_w.shape[2], axis=2)
    local_shapes = [weights[n].shape for n in SMALL]
    d_s, m_s, v_s = _adamw(_pack([weights[n] for n in SMALL]), _pack([small_g[n] for n in SMALL]),
                           _pack([mom_m[n] for n in SMALL]), _pack([mom_v[n] for n in SMALL]), "adamw_small")
    for n, dd, mm, vv in zip(SMALL, _unpack(d_s, local_shapes), _unpack(m_s, local_shapes),
                             _unpack(v_s, local_shapes)):
        out_g[n], out_d[n], out_m[n], out_v[n] = small_g[n], dd, mm, vv

    return (loss, grad_x, *[out_g[n] for n in WEIGHTS], *[out_d[n] for n in WEIGHTS],
            *[out_m[n] for n in WEIGHTS], *[out_v[n] for n in WEIGHTS])
```

```python
import functools

import jax
import jax.numpy as jnp
from jax import lax
from jax.experimental import pallas as pl
from jax.experimental.pallas import tpu as pltpu

F32 = jnp.float32
BF16 = jnp.bfloat16
EPS = 1e-6
HEAD = 128
ROPE = 64
ROPE_BASE = 10000.0
CONV_TAPS = 3
N_SHARD = 4
N_DEV = 8
ADAM_LR = 0.001
ADAM_B1 = 0.9
ADAM_B2 = 0.999
ADAM_EPS = 1e-08
ADAM_WD = 0.01
ADAM_STEP = 10
MESH = pl.DeviceIdType.MESH
VMEM_LIMIT = 56 * 1024 * 1024
HALO_ROWS = 16

WEIGHTS = ['attn_norm', 'w_in', 'sgu_norm', 'w_spatial', 'b_spatial', 'conv_w', 'conv_b', 'kv_norm', 'w_ukv',
           'q_nope_norm', 'q_rope_norm', 'k_nope_norm', 'k_rope_norm', 'out_norm', 'w_out', 'ple_norm',
           'w_ple_gate', 'w_ple_proj']
BIG = ['w_in', 'w_ukv', 'w_out', 'w_ple_gate', 'w_ple_proj']
BIG_AXIS = {'w_in': 2, 'w_ukv': 2, 'w_out': 1, 'w_ple_gate': 1, 'w_ple_proj': 2}
SMALL = [n for n in WEIGHTS if n not in BIG]


def _pick(n, cands):
    for c in cands:
        if n % c == 0:
            return c
    return n


def _params(sem=None):
    return pltpu.CompilerParams(dimension_semantics=sem, vmem_limit_bytes=VMEM_LIMIT)


def _matmul(a, b, mode, out_dtype, name, add=None):
    if mode == 'nn':
        (m, k), n = a.shape, b.shape[1]
    elif mode == 'nt':
        (m, k), n = a.shape, b.shape[0]
    else:
        (k, m), n = a.shape, b.shape[1]
    tm = _pick(m, (1024, 512, 256, 128))
    tn = _pick(n, (1536, 1024, 512, 256, 128))
    tk = k if k <= 2048 else _pick(k, (1024, 512, 256, 128))
    nk = k // tk
    if mode == 'tn':
        a_spec = pl.BlockSpec((tk, tm), lambda i, j, kk: (kk, i))
        dims = (((0,), (0,)), ((), ()))
    else:
        a_spec = pl.BlockSpec((tm, tk), lambda i, j, kk: (i, kk))
        dims = (((1,), (0,)), ((), ())) if mode == 'nn' else (((1,), (1,)), ((), ()))
    if mode == 'nt':
        b_spec = pl.BlockSpec((tn, tk), lambda i, j, kk: (j, kk))
    else:
        b_spec = pl.BlockSpec((tk, tn), lambda i, j, kk: (kk, j))
    o_spec = pl.BlockSpec((tm, tn), lambda i, j, kk: (i, j))
    has_add = add is not None

    def body(*refs):
        a_ref, b_ref = refs[0], refs[1]
        add_ref = refs[2] if has_add else None
        o_ref = refs[3] if has_add else refs[2]
        part = lax.dot_general(a_ref[...], b_ref[...], dims, preferred_element_type=F32)

        def finish(res):
            if has_add:
                res = res + add_ref[...]
            o_ref[...] = res.astype(out_dtype)

        if nk == 1:
            finish(part)
        else:
            acc_ref = refs[-1]
            kk = pl.program_id(2)

            @pl.when(kk == 0)
            def _():
                acc_ref[...] = part

            @pl.when(kk > 0)
            def _():
                acc_ref[...] += part

            @pl.when(kk == nk - 1)
            def _():
                finish(acc_ref[...])

    in_specs = [a_spec, b_spec] + ([o_spec] if has_add else [])
    args = [a, b] + ([add] if has_add else [])
    return pl.pallas_call(
        body, name=name, grid=(m // tm, n // tn, nk), in_specs=in_specs, out_specs=o_spec,
        out_shape=jax.ShapeDtypeStruct((m, n), out_dtype),
        scratch_shapes=[pltpu.VMEM((tm, tn), F32)] if nk > 1 else [],
        compiler_params=_params(("parallel", "parallel", "arbitrary")),
    )(*args)


def _rms(x, n):
    r = lax.rsqrt(jnp.sum(x * x, axis=-1, keepdims=True) * (1.0 / n) + EPS)
    return x * r, r


def _rms_bwd(dxhat, xhat, r, n):
    return r * (dxhat - xhat * (jnp.sum(dxhat * xhat, axis=-1, keepdims=True) * (1.0 / n)))


def _sigmoid(z):
    return 1.0 / (1.0 + jnp.exp(-z))


def _silu_and_grad(z):
    sig = _sigmoid(z)
    return z * sig, sig * (1.0 + z * (1.0 - sig))


def _colsum(x):
    return jnp.sum(x, axis=0, keepdims=True)


def _rope(t, cos_t, sin_a, sin_b):
    return t * cos_t + pltpu.roll(t, 96, 1) * sin_a + pltpu.roll(t, 32, 1) * sin_b


def _rope_bwd(d, cos_t, sin_a, sin_b):
    return d * cos_t + pltpu.roll(d * sin_a, 32, 1) + pltpu.roll(d * sin_b, 96, 1)


def _shift_down(g, first_row):
    row = lax.broadcasted_iota(jnp.int32, g.shape, 0)
    return jnp.where(row == 0, first_row, pltpu.roll(g, 1, 0))


def _shift_up(g, last_row):
    n = g.shape[0]
    row = lax.broadcasted_iota(jnp.int32, g.shape, 0)
    return jnp.where(row == n - 1, last_row, pltpu.roll(g, n - 1, 0))


def _row_spec(r, w, col=0):
    return pl.BlockSpec((r, w), lambda i: (i, col))


def _const_spec(shape):
    nd = len(shape)
    return pl.BlockSpec(shape, lambda i: (0,) * nd)


def _col_block(off, w):
    assert off % w == 0, (off, w)
    return off // w


def _zero_at_first_step(refs):
    @pl.when(pl.program_id(0) == 0)
    def _():
        for ref in refs:
            ref[...] = jnp.zeros(ref.shape, ref.dtype)


def _row_call(body, name, t, r, in_specs, args, out_specs, out_shapes, scratch=()):
    return pl.pallas_call(
        body, name=name, grid=(t // r,), in_specs=in_specs, out_specs=out_specs, out_shape=out_shapes,
        scratch_shapes=list(scratch), compiler_params=_params(("arbitrary",)),
    )(*args)


def _norm_fwd(h, g, name):
    t, d = h.shape
    r = _pick(t, (256, 128))

    def body(h_ref, g_ref, o_ref):
        xhat, _ = _rms(h_ref[...], d)
        o_ref[...] = (xhat * g_ref[...]).astype(BF16)

    return _row_call(body, name, t, r, [_row_spec(r, d), _const_spec((1, d))], (h, g),
                     _row_spec(r, d), jax.ShapeDtypeStruct((t, d), BF16))


def _norm_bwd(h, g, d_hn, d_res, name):
    t, d = h.shape
    r = _pick(t, (256, 128))

    def body(h_ref, g_ref, dy_ref, dres_ref, dh_ref, dhb_ref, dg_ref):
        _zero_at_first_step([dg_ref])
        xhat, rr = _rms(h_ref[...], d)
        dy = dy_ref[...].astype(F32)
        dg_ref[...] += _colsum(dy * xhat)
        dh = dres_ref[...] + _rms_bwd(dy * g_ref[...], xhat, rr, d)
        dh_ref[...] = dh
        dhb_ref[...] = dh.astype(BF16)

    return _row_call(body, name, t, r,
                     [_row_spec(r, d), _const_spec((1, d)), _row_spec(r, d), _row_spec(r, d)], (h, g, d_hn, d_res),
                     [_row_spec(r, d), _row_spec(r, d), _const_spec((1, d))],
                     [jax.ShapeDtypeStruct((t, d), F32), jax.ShapeDtypeStruct((t, d), BF16),
                      jax.ShapeDtypeStruct((1, d), F32)])


def _sgu_scores(v, gs_ref, ws_ref, bb_ref, s_scr, r, ah, keep=None):
    for kk in range(r // HEAD):
        for hh in range(ah):
            rows, cols = slice(kk * HEAD, (kk + 1) * HEAD), slice(hh * HEAD, (hh + 1) * HEAD)
            vhat, rv = _rms(v[rows, cols], HEAD)
            vn = vhat * gs_ref[pl.ds(hh, 1), :]
            s_scr[rows, cols] = jnp.dot(ws_ref[hh], vn.astype(BF16), preferred_element_type=F32) + bb_ref[hh]
            if keep is not None:
                keep[(kk, hh)] = (vhat, rv, vn)


def _sgu_fwd(proj, off, aw, gs, ws, bb, ga, name):
    t = proj.shape[0]
    ah = aw // HEAD
    r = _pick(t, (256, 128))
    cb = _col_block(off, aw)

    def body(u_ref, v_ref, z_ref, gs_ref, ws_ref, bb_ref, ga_ref, o_ref, s_scr):
        _sgu_scores(v_ref[...].astype(F32), gs_ref, ws_ref, bb_ref, s_scr, r, ah)
        sil, _ = _silu_and_grad(z_ref[...].astype(F32))
        yhat, _ = _rms(u_ref[...].astype(F32) * s_scr[...] * sil, aw)
        o_ref[...] = (yhat * ga_ref[...]).astype(BF16)

    return _row_call(
        body, name, t, r,
        [_row_spec(r, aw, cb), _row_spec(r, aw, cb + 1), _row_spec(r, aw, cb + 2), _const_spec((ah, HEAD)),
         _const_spec((ah, HEAD, HEAD)), _const_spec((ah, HEAD, HEAD)), _const_spec((1, aw))],
        (proj, proj, proj, gs, ws, bb, ga),
        _row_spec(r, aw), jax.ShapeDtypeStruct((t, aw), BF16), scratch=[pltpu.VMEM((r, aw), F32)])


def _sgu_bwd(proj, off, aw, gs, ws, ws_t, bb, ga, dy, name):
    t = proj.shape[0]
    ah = aw // HEAD
    r = _pick(t, (256, 128))
    cb = _col_block(off, aw)

    def body(u_ref, v_ref, z_ref, gs_ref, ws_ref, wst_ref, bb_ref, ga_ref, dy_ref,
             d_ref, dgs_ref, dws_ref, db_ref, dga_ref, s_scr, dv_scr):
        _zero_at_first_step([dgs_ref, dws_ref, db_ref, dga_ref])
        keep = {}
        _sgu_scores(v_ref[...].astype(F32), gs_ref, ws_ref, bb_ref, s_scr, r, ah, keep)
        u, z, s = u_ref[...].astype(F32), z_ref[...].astype(F32), s_scr[...]
        sil, dsil = _silu_and_grad(z)
        yhat, rr = _rms(u * s * sil, aw)
        dy_f = dy_ref[...].astype(F32)
        dga_ref[...] += _colsum(dy_f * yhat)
        dya = _rms_bwd(dy_f * ga_ref[...], yhat, rr, aw)
        d_ref[:, 0:aw] = (dya * s * sil).astype(BF16)
        d_ref[:, 2 * aw:3 * aw] = (dya * u * s * dsil).astype(BF16)
        ds = dya * u * sil
        for kk in range(r // HEAD):
            for hh in range(ah):
                rows, cols = slice(kk * HEAD, (kk + 1) * HEAD), slice(hh * HEAD, (hh + 1) * HEAD)
                vhat, rv, vn = keep[(kk, hh)]
                ds_blk = ds[rows, cols]
                db_ref[hh] += jnp.sum(ds_blk, axis=1, keepdims=True)
                ds_b = ds_blk.astype(BF16)
                dws_ref[hh] += lax.dot_general(ds_b, vn.astype(BF16), (((1,), (1,)), ((), ())),
                                               preferred_element_type=F32)
                dvn = jnp.dot(wst_ref[hh], ds_b, preferred_element_type=F32)
                dgs_ref[pl.ds(hh, 1), :] += _colsum(dvn * vhat)
                dv_scr[rows, cols] = _rms_bwd(dvn * gs_ref[pl.ds(hh, 1), :], vhat, rv, HEAD)
        d_ref[:, aw:2 * aw] = dv_scr[...].astype(BF16)

    return _row_call(
        body, name, t, r,
        [_row_spec(r, aw, cb), _row_spec(r, aw, cb + 1), _row_spec(r, aw, cb + 2), _const_spec((ah, HEAD)),
         _const_spec((ah, HEAD, HEAD)), _const_spec((ah, HEAD, HEAD)), _const_spec((ah, HEAD, HEAD)),
         _const_spec((1, aw)), _row_spec(r, aw, 0)],
        (proj, proj, proj, gs, ws, ws_t, bb, ga, dy),
        [_row_spec(r, 3 * aw), _const_spec((ah, HEAD)), _const_spec((ah, HEAD, HEAD)), _const_spec((ah, HEAD, 1)),
         _const_spec((1, aw))],
        [jax.ShapeDtypeStruct((t, 3 * aw), BF16), jax.ShapeDtypeStruct((ah, HEAD), F32),
         jax.ShapeDtypeStruct((ah, HEAD, HEAD), F32), jax.ShapeDtypeStruct((ah, HEAD, 1), F32),
         jax.ShapeDtypeStruct((1, aw), F32)],
        scratch=[pltpu.VMEM((r, aw), F32), pltpu.VMEM((r, aw), F32)])


def _halo_specs(t, r, w, col, rows):
    per = r // rows
    last = t // rows - 1
    prev = pl.BlockSpec((rows, w), lambda i: (jnp.maximum(i * per - 1, 0), col))
    nxt = pl.BlockSpec((rows, w), lambda i: (jnp.minimum((i + 1) * per, last), col))
    return prev, nxt


def _edge_rows(prev_ref, next_ref, n_steps):
    i = pl.program_id(0)
    rows = prev_ref.shape[0]
    before = prev_ref[...].astype(F32)[rows - 1:rows, :] * (i > 0).astype(F32)
    after = next_ref[...].astype(F32)[0:1, :] * (i < n_steps - 1).astype(F32)
    return before, after


def _conv_fwd(proj, off, bw, cw, cb_, gb, name):
    t = proj.shape[0]
    r = _pick(t, (256, 128))
    n_steps = t // r
    c0 = _col_block(off, bw)
    cp, cn = _halo_specs(t, r, bw, c0 + 1, HALO_ROWS)
    hp, hn = _halo_specs(t, r, bw, c0 + 2, HALO_ROWS)

    def body(b_ref, c_ref, h_ref, z_ref, cp_ref, cn_ref, hp_ref, hn_ref, cw_ref, cb_ref, gb_ref, o_ref, yc_ref):
        g = c_ref[...].astype(F32) * h_ref[...].astype(F32)
        c_before, c_after = _edge_rows(cp_ref, cn_ref, n_steps)
        h_before, h_after = _edge_rows(hp_ref, hn_ref, n_steps)
        yconv = (cb_ref[...] + cw_ref[0:1, :] * _shift_down(g, c_before * h_before) + cw_ref[1:2, :] * g
                 + cw_ref[2:3, :] * _shift_up(g, c_after * h_after))
        yc_ref[...] = yconv
        sil, _ = _silu_and_grad(z_ref[...].astype(F32))
        yhat, _ = _rms(b_ref[...].astype(F32) * yconv * sil, bw)
        o_ref[...] = (yhat * gb_ref[...]).astype(BF16)

    return _row_call(
        body, name, t, r,
        [_row_spec(r, bw, c0), _row_spec(r, bw, c0 + 1), _row_spec(r, bw, c0 + 2), _row_spec(r, bw, c0 + 3),
         cp, cn, hp, hn, _const_spec((CONV_TAPS, bw)), _const_spec((1, bw)), _const_spec((1, bw))],
        (proj, proj, proj, proj, proj, proj, proj, proj, cw, cb_, gb),
        [_row_spec(r, bw), _row_spec(r, bw)],
        [jax.ShapeDtypeStruct((t, bw), BF16), jax.ShapeDtypeStruct((t, bw), F32)])


def _conv_bwd_gate(proj, off, bw, yconv, gb, dy, name):
    t = proj.shape[0]
    r = _pick(t, (256, 128))
    c0 = _col_block(off, bw)

    def body(b_ref, z_ref, yc_ref, gb_ref, dy_ref, dyc_ref, db_ref, dz_ref, dgb_ref, dcb_ref):
        _zero_at_first_step([dgb_ref, dcb_ref])
        b, z, yconv_v = b_ref[...].astype(F32), z_ref[...].astype(F32), yc_ref[...]
        sil, dsil = _silu_and_grad(z)
        yhat, rr = _rms(b * yconv_v * sil, bw)
        dy_f = dy_ref[...].astype(F32)
        dgb_ref[...] += _colsum(dy_f * yhat)
        dyb = _rms_bwd(dy_f * gb_ref[...], yhat, rr, bw)
        dyc = dyb * b * sil
        dyc_ref[...] = dyc
        dcb_ref[...] += _colsum(dyc)
        db_ref[...] = (dyb * yconv_v * sil).astype(BF16)
        dz_ref[...] = (dyb * b * yconv_v * dsil).astype(BF16)

    return _row_call(
        body, name, t, r,
        [_row_spec(r, bw, c0), _row_spec(r, bw, c0 + 3), _row_spec(r, bw), _const_spec((1, bw)), _row_spec(r, bw, 1)],
        (proj, proj, yconv, gb, dy),
        [_row_spec(r, bw), _row_spec(r, bw), _row_spec(r, bw), _const_spec((1, bw)), _const_spec((1, bw))],
        [jax.ShapeDtypeStruct((t, bw), F32), jax.ShapeDtypeStruct((t, bw), BF16), jax.ShapeDtypeStruct((t, bw), BF16),
         jax.ShapeDtypeStruct((1, bw), F32), jax.ShapeDtypeStruct((1, bw), F32)])


def _conv_bwd_taps(proj, off, bw, dyc, cw, name):
    t = proj.shape[0]
    r = _pick(t, (256, 128))
    n_steps = t // r
    c0 = _col_block(off, bw)
    cp, cn = _halo_specs(t, r, bw, c0 + 1, HALO_ROWS)
    hp, hn = _halo_specs(t, r, bw, c0 + 2, HALO_ROWS)
    dp, dn = _halo_specs(t, r, bw, 0, 8)

    def body(c_ref, h_ref, cp_ref, cn_ref, hp_ref, hn_ref, d_ref, dp_ref, dn_ref, cw_ref, dc_ref, dh_ref, dcw_ref):
        _zero_at_first_step([dcw_ref])
        c, h, d = c_ref[...].astype(F32), h_ref[...].astype(F32), d_ref[...]
        g = c * h
        c_before, c_after = _edge_rows(cp_ref, cn_ref, n_steps)
        h_before, h_after = _edge_rows(hp_ref, hn_ref, n_steps)
        d_before, d_after = _edge_rows(dp_ref, dn_ref, n_steps)
        dg = (cw_ref[0:1, :] * _shift_up(d, d_after) + cw_ref[1:2, :] * d + cw_ref[2:3, :] * _shift_down(d, d_before))
        dc_ref[...] = (dg * h).astype(BF16)
        dh_ref[...] = (dg * c).astype(BF16)
        dcw_ref[0:1, :] += _colsum(d * _shift_down(g, c_before * h_before))
        dcw_ref[1:2, :] += _colsum(d * g)
        dcw_ref[2:3, :] += _colsum(d * _shift_up(g, c_after * h_after))

    return _row_call(
        body, name, t, r,
        [_row_spec(r, bw, c0 + 1), _row_spec(r, bw, c0 + 2), cp, cn, hp, hn, _row_spec(r, bw), dp, dn,
         _const_spec((CONV_TAPS, bw))],
        (proj, proj, proj, proj, proj, proj, dyc, dyc, dyc, cw),
        [_row_spec(r, bw), _row_spec(r, bw), _const_spec((CONV_TAPS, bw))],
        [jax.ShapeDtypeStruct((t, bw), BF16), jax.ShapeDtypeStruct((t, bw), BF16),
         jax.ShapeDtypeStruct((CONV_TAPS, bw), F32)])


def _mla_prep_fwd(proj, q_off, ckv_off, kr_off, ch, kvr, tabs, qn_g, qr_g, kr_g, kv_g, name):
    t = proj.shape[0]
    r = _pick(t, (256, 128))
    qw = ch * 2 * HEAD
    cos_t, sin_a, sin_b = tabs

    def body(q_ref, ckv_ref, kr_ref, cos_ref, sa_ref, sb_ref, qn_ref, qr_ref, krg_ref, kvg_ref,
             qo_ref, co_ref, ko_ref):
        cos_v, sa, sb = cos_ref[...], sa_ref[...], sb_ref[...]
        for hh in range(ch):
            lo = hh * 2 * HEAD
            nhat, _ = _rms(q_ref[:, lo:lo + HEAD].astype(F32), HEAD)
            qo_ref[:, lo:lo + HEAD] = (nhat * qn_ref[...]).astype(BF16)
            rhat, _ = _rms(q_ref[:, lo + HEAD:lo + 2 * HEAD].astype(F32), ROPE)
            qo_ref[:, lo + HEAD:lo + 2 * HEAD] = _rope(rhat * qr_ref[...], cos_v, sa, sb).astype(BF16)
        khat, _ = _rms(kr_ref[...].astype(F32), ROPE)
        ko_ref[...] = _rope(khat * krg_ref[...], cos_v, sa, sb).astype(BF16)
        chat, _ = _rms(ckv_ref[...].astype(F32), kvr)
        co_ref[...] = (chat * kvg_ref[...]).astype(BF16)

    tab = _row_spec(r, HEAD)
    gain = _const_spec((1, HEAD))
    return _row_call(
        body, name, t, r,
        [_row_spec(r, qw, _col_block(q_off, qw)), _row_spec(r, kvr, _col_block(ckv_off, kvr)),
         _row_spec(r, HEAD, _col_block(kr_off, HEAD)), tab, tab, tab, gain, gain, gain, _const_spec((1, kvr))],
        (proj, proj, proj, cos_t, sin_a, sin_b, qn_g, qr_g, kr_g, kv_g),
        [_row_spec(r, qw), _row_spec(r, kvr), _row_spec(r, HEAD)],
        [jax.ShapeDtypeStruct((t, qw), BF16), jax.ShapeDtypeStruct((t, kvr), BF16),
         jax.ShapeDtypeStruct((t, HEAD), BF16)])


def _mla_prep_bwd(proj, q_off, ckv_off, kr_off, ch, kvr, tabs, qn_g, qr_g, kr_g, kv_g, dq_cat, dq_scale, dckv_n, dkr_rot,
                  name):
    t = proj.shape[0]
    r = _pick(t, (256, 128))
    qw = ch * 2 * HEAD
    cos_t, sin_a, sin_b = tabs

    def body(q_ref, ckv_ref, kr_ref, cos_ref, sa_ref, sb_ref, qn_ref, qr_ref, krg_ref, kvg_ref,
             dq_ref, dc_ref, dk_ref, dqo_ref, dco_ref, dko_ref, dqn_ref, dqr_ref, dkrg_ref, dkvg_ref):
        _zero_at_first_step([dqn_ref, dqr_ref, dkrg_ref, dkvg_ref])
        cos_v, sa, sb = cos_ref[...], sa_ref[...], sb_ref[...]
        for hh in range(ch):
            lo = hh * 2 * HEAD
            nhat, nr = _rms(q_ref[:, lo:lo + HEAD].astype(F32), HEAD)
            d_n = dq_ref[:, lo:lo + HEAD].astype(F32) * dq_scale
            dqn_ref[...] += _colsum(d_n * nhat)
            dqo_ref[:, lo:lo + HEAD] = _rms_bwd(d_n * qn_ref[...], nhat, nr, HEAD).astype(BF16)
            rhat, rr = _rms(q_ref[:, lo + HEAD:lo + 2 * HEAD].astype(F32), ROPE)
            d_t = _rope_bwd(dq_ref[:, lo + HEAD:lo + 2 * HEAD].astype(F32) * dq_scale, cos_v, sa, sb)
            dqr_ref[...] += _colsum(d_t * rhat)
            dqo_ref[:, lo + HEAD:lo + 2 * HEAD] = _rms_bwd(d_t * qr_ref[...], rhat, rr, ROPE).astype(BF16)
        khat, kr_r = _rms(kr_ref[...].astype(F32), ROPE)
        d_k = _rope_bwd(dk_ref[...], cos_v, sa, sb)
        dkrg_ref[...] += _colsum(d_k * khat)
        dko_ref[...] = _rms_bwd(d_k * krg_ref[...], khat, kr_r, ROPE).astype(BF16)
        chat, cr = _rms(ckv_ref[...].astype(F32), kvr)
        d_c = dc_ref[...].astype(F32)
        dkvg_ref[...] += _colsum(d_c * chat)
        dco_ref[...] = _rms_bwd(d_c * kvg_ref[...], chat, cr, kvr).astype(BF16)

    tab = _row_spec(r, HEAD)
    gain = _const_spec((1, HEAD))
    return _row_call(
        body, name, t, r,
        [_row_spec(r, qw, _col_block(q_off, qw)), _row_spec(r, kvr, _col_block(ckv_off, kvr)),
         _row_spec(r, HEAD, _col_block(kr_off, HEAD)), tab, tab, tab, gain, gain, gain, _const_spec((1, kvr)),
         _row_spec(r, qw), _row_spec(r, kvr), _row_spec(r, HEAD)],
        (proj, proj, proj, cos_t, sin_a, sin_b, qn_g, qr_g, kr_g, kv_g, dq_cat, dckv_n, dkr_rot),
        [_row_spec(r, qw), _row_spec(r, kvr), _row_spec(r, HEAD), gain, gain, gain, _const_spec((1, kvr))],
        [jax.ShapeDtypeStruct((t, qw), BF16), jax.ShapeDtypeStruct((t, kvr), BF16),
         jax.ShapeDtypeStruct((t, HEAD), BF16), jax.ShapeDtypeStruct((1, HEAD), F32),
         jax.ShapeDtypeStruct((1, HEAD), F32), jax.ShapeDtypeStruct((1, HEAD), F32),
         jax.ShapeDtypeStruct((1, kvr), F32)])


def _kv_prep_fwd(kv, kr_rot, ch, kn_g, name):
    t = kv.shape[0]
    r = _pick(t, (256, 128))
    qw = ch * 2 * HEAD

    def body(kv_ref, kr_ref, kn_ref, ko_ref, vo_ref):
        ones = jnp.ones((r, HEAD), BF16)
        for hh in range(ch):
            lo = hh * 2 * HEAD
            nhat, _ = _rms(kv_ref[:, lo:lo + HEAD], HEAD)
            ko_ref[:, lo:lo + HEAD] = (nhat * kn_ref[...]).astype(BF16)
            ko_ref[:, lo + HEAD:lo + 2 * HEAD] = kr_ref[...]
            vo_ref[:, lo:lo + HEAD] = kv_ref[:, lo + HEAD:lo + 2 * HEAD].astype(BF16)
            vo_ref[:, lo + HEAD:lo + 2 * HEAD] = ones

    return _row_call(
        body, name, t, r, [_row_spec(r, qw), _row_spec(r, HEAD), _const_spec((1, HEAD))], (kv, kr_rot, kn_g),
        [_row_spec(r, qw), _row_spec(r, qw)],
        [jax.ShapeDtypeStruct((t, qw), BF16), jax.ShapeDtypeStruct((t, qw), BF16)])


def _kv_prep_bwd(kv, ch, kn_g, dk_cat, dv, name):
    t = kv.shape[0]
    r = _pick(t, (256, 128))
    qw = ch * 2 * HEAD

    def body(kv_ref, kn_ref, dk_ref, dv_ref, dkv_ref, dkr_ref, dkn_ref):
        _zero_at_first_step([dkn_ref])
        dkr = jnp.zeros((r, HEAD), F32)
        for hh in range(ch):
            lo = hh * 2 * HEAD
            nhat, nr = _rms(kv_ref[:, lo:lo + HEAD], HEAD)
            d_n = dk_ref[:, lo:lo + HEAD].astype(F32)
            dkn_ref[...] += _colsum(d_n * nhat)
            dkv_ref[:, lo:lo + HEAD] = _rms_bwd(d_n * kn_ref[...], nhat, nr, HEAD).astype(BF16)
            dkv_ref[:, lo + HEAD:lo + 2 * HEAD] = dv_ref[:, hh * HEAD:(hh + 1) * HEAD]
            dkr = dkr + dk_ref[:, lo + HEAD:lo + 2 * HEAD].astype(F32)
        dkr_ref[...] = dkr

    return _row_call(
        body, name, t, r,
        [_row_spec(r, qw), _const_spec((1, HEAD)), _row_spec(r, qw), _row_spec(r, ch * HEAD)], (kv, kn_g, dk_cat, dv),
        [_row_spec(r, qw), _row_spec(r, HEAD), _const_spec((1, HEAD))],
        [jax.ShapeDtypeStruct((t, qw), BF16), jax.ShapeDtypeStruct((t, HEAD), F32),
         jax.ShapeDtypeStruct((1, HEAD), F32)])


def _attn_tiles(t):
    return _pick(t, (1024, 512, 256, 128)), _pick(t, (1024, 512, 256, 128))


def _n_chains(tile):
    return 2 if tile >= 256 else 1


_NT = (((1,), (1,)), ((), ()))
LOG2E = 1.4426950408889634


def _attn_fwd(q_cat, k_cat, v_aug, ch, scale, name):
    t = q_cat.shape[0]
    tq, tk = _attn_tiles(t)
    nk = t // tk
    nsub = _n_chains(tq)
    sq = tq // nsub
    c2 = scale * LOG2E

    def body(q_ref, k_ref, v_ref, o_ref, lse_ref, m_scr, acc_scr):
        ki = pl.program_id(2)

        @pl.when(ki == 0)
        def _():
            m_scr[...] = jnp.full(m_scr.shape, -jnp.inf, F32)
            acc_scr[...] = jnp.zeros(acc_scr.shape, F32)

        k, v = k_ref[...], v_ref[...]
        for c in range(nsub):
            rows = slice(c * sq, (c + 1) * sq)
            s = lax.dot_general(q_ref[rows, :], k, _NT, preferred_element_type=F32) * c2
            m_old = m_scr[rows, :]
            m_new = jnp.maximum(m_old, jnp.max(s, axis=-1, keepdims=True))
            p = jnp.exp2(s - m_new).astype(BF16)
            acc_scr[rows, :] = (jnp.exp2(m_old - m_new) * acc_scr[rows, :]
                                + jnp.dot(p, v, preferred_element_type=F32))
            m_scr[rows, :] = m_new

        @pl.when(ki == nk - 1)
        def _():
            acc = acc_scr[...]
            l_sum = acc[:, HEAD:]
            o_ref[...] = (acc[:, :HEAD] / l_sum).astype(BF16)
            lse_ref[0] = m_scr[...] + jnp.log(l_sum[:, 0:1]) * LOG2E

    return pl.pallas_call(
        body, name=name, grid=(ch, t // tq, nk),
        in_specs=[pl.BlockSpec((tq, 2 * HEAD), lambda h, i, j: (i, h)),
                  pl.BlockSpec((tk, 2 * HEAD), lambda h, i, j: (j, h)),
                  pl.BlockSpec((tk, 2 * HEAD), lambda h, i, j: (j, h))],
        out_specs=[pl.BlockSpec((tq, HEAD), lambda h, i, j: (i, h)),
                   pl.BlockSpec((1, tq, 1), lambda h, i, j: (h, i, 0))],
        out_shape=[jax.ShapeDtypeStruct((t, ch * HEAD), BF16), jax.ShapeDtypeStruct((ch, t, 1), F32)],
        scratch_shapes=[pltpu.VMEM((tq, 1), F32), pltpu.VMEM((tq, 2 * HEAD), F32)],
        compiler_params=_params(("parallel", "parallel", "arbitrary")),
    )(q_cat, k_cat, v_aug)


def _attn_bwd(q_cat, k_cat, k_cat_t, v_aug, do, lse_row, d_row, ch, scale, name):
    t = q_cat.shape[0]
    tq, tk = _attn_tiles(t)
    nk, nq = t // tk, t // tq
    nsub = _n_chains(tk)
    sk = tk // nsub
    c2 = scale * LOG2E

    def body(q_ref, k_ref, kt_ref, v_ref, do_ref, lse_ref, d_ref, dqt_ref, dk_ref, dv_ref, dk_scr, dv_scr):
        ki, qi = pl.program_id(1), pl.program_id(2)

        @pl.when(qi == 0)
        def _():
            dk_scr[...] = jnp.zeros(dk_scr.shape, F32)
            dv_scr[...] = jnp.zeros(dv_scr.shape, F32)

        q, do_v, lse, dd = q_ref[...], do_ref[...], lse_ref[0], d_ref[0]
        part = None
        for c in range(nsub):
            rows = slice(c * sk, (c + 1) * sk)
            st = lax.dot_general(k_ref[rows, :], q, _NT, preferred_element_type=F32) * c2
            pt = jnp.exp2(st - lse)
            dv_scr[rows, :] += jnp.dot(pt.astype(BF16), do_v, preferred_element_type=F32)
            dpt = lax.dot_general(v_ref[rows, :], do_v, _NT, preferred_element_type=F32)
            dst = (pt * (dpt - dd)).astype(BF16)
            dk_scr[rows, :] += jnp.dot(dst, q, preferred_element_type=F32)
            contrib = jnp.dot(kt_ref[:, rows], dst, preferred_element_type=F32)
            part = contrib if part is None else part + contrib
        cols = pl.ds(pl.multiple_of(qi * tq, tq), tq)

        @pl.when(ki == 0)
        def _():
            dqt_ref[:, cols] = part

        @pl.when(ki > 0)
        def _():
            dqt_ref[:, cols] += part

        @pl.when(qi == nq - 1)
        def _():
            dk_ref[...] = (dk_scr[...] * scale).astype(BF16)
            dv_ref[...] = dv_scr[...].astype(BF16)

    stat = pl.BlockSpec((1, 1, tq), lambda h, j, i: (h, 0, i))
    return pl.pallas_call(
        body, name=name, grid=(ch, nk, nq),
        in_specs=[pl.BlockSpec((tq, 2 * HEAD), lambda h, j, i: (i, h)),
                  pl.BlockSpec((tk, 2 * HEAD), lambda h, j, i: (j, h)),
                  pl.BlockSpec((2 * HEAD, tk), lambda h, j, i: (h, j)),
                  pl.BlockSpec((tk, HEAD), lambda h, j, i: (j, 2 * h)),
                  pl.BlockSpec((tq, HEAD), lambda h, j, i: (i, h)), stat, stat],
        out_specs=[pl.BlockSpec((2 * HEAD, t), lambda h, j, i: (h, 0)),
                   pl.BlockSpec((tk, 2 * HEAD), lambda h, j, i: (j, h)),
                   pl.BlockSpec((tk, HEAD), lambda h, j, i: (j, h))],
        out_shape=[jax.ShapeDtypeStruct((ch * 2 * HEAD, t), F32), jax.ShapeDtypeStruct((t, ch * 2 * HEAD), BF16),
                   jax.ShapeDtypeStruct((t, ch * HEAD), BF16)],
        scratch_shapes=[pltpu.VMEM((tk, 2 * HEAD), F32), pltpu.VMEM((tk, HEAD), F32)],
        compiler_params=_params(("parallel", "arbitrary", "arbitrary")),
    )(q_cat, k_cat, k_cat_t, v_aug, do, lse_row, d_row)


def _attn_post_fwd(o, proj, z_off, cw, gc, name):
    t = o.shape[0]
    r = _pick(t, (256, 128))

    def body(o_ref, z_ref, gc_ref, y_ref):
        sil, _ = _silu_and_grad(z_ref[...].astype(F32))
        yhat, _ = _rms(o_ref[...].astype(F32) * sil, cw)
        y_ref[...] = (yhat * gc_ref[...]).astype(BF16)

    return _row_call(body, name, t, r,
                     [_row_spec(r, cw), _row_spec(r, cw, _col_block(z_off, cw)), _const_spec((1, cw))], (o, proj, gc),
                     _row_spec(r, cw), jax.ShapeDtypeStruct((t, cw), BF16))


def _attn_post_bwd(o, proj, z_off, cw, gc, dy, dy_col, name):
    t = o.shape[0]
    ch = cw // HEAD
    r = _pick(t, (256, 128))

    def body(o_ref, z_ref, gc_ref, dy_ref, do_ref, dz_ref, ds_ref, dgc_ref):
        _zero_at_first_step([dgc_ref])
        o_v, z = o_ref[...].astype(F32), z_ref[...].astype(F32)
        sil, dsil = _silu_and_grad(z)
        yhat, rr = _rms(o_v * sil, cw)
        dy_f = dy_ref[...].astype(F32)
        dgc_ref[...] += _colsum(dy_f * yhat)
        dyc = _rms_bwd(dy_f * gc_ref[...], yhat, rr, cw)
        do_b = (dyc * sil).astype(BF16)
        do_ref[...] = do_b
        dz_ref[...] = (dyc * o_v * dsil).astype(BF16)
        prod = do_b.astype(F32) * o_v
        for hh in range(ch):
            ds_ref[hh] = jnp.sum(prod[:, hh * HEAD:(hh + 1) * HEAD], axis=-1, keepdims=True)

    return _row_call(
        body, name, t, r,
        [_row_spec(r, cw), _row_spec(r, cw, _col_block(z_off, cw)), _const_spec((1, cw)), _row_spec(r, cw, dy_col)],
        (o, proj, gc, dy),
        [_row_spec(r, cw), _row_spec(r, cw), pl.BlockSpec((ch, r, 1), lambda i: (0, i, 0)), _const_spec((1, cw))],
        [jax.ShapeDtypeStruct((t, cw), BF16), jax.ShapeDtypeStruct((t, cw), BF16),
         jax.ShapeDtypeStruct((ch, t, 1), F32), jax.ShapeDtypeStruct((1, cw), F32)])


def _ple_fwd(h1, gpre, pp, name):
    t, d = h1.shape
    r = _pick(t, (256, 128))

    def body(h_ref, g_ref, p_ref, o_ref):
        o_ref[...] = h_ref[...] + _sigmoid(g_ref[...]) * p_ref[...]

    return _row_call(body, name, t, r, [_row_spec(r, d)] * 3, (h1, gpre, pp), _row_spec(r, d),
                     jax.ShapeDtypeStruct((t, d), F32))


def _ple_bwd(gpre, pp, dh, name):
    t, d = dh.shape
    r = _pick(t, (256, 128))

    def body(g_ref, p_ref, dh_ref, dg_ref, dp_ref):
        sig = _sigmoid(g_ref[...])
        dh_v = dh_ref[...]
        dg_ref[...] = (dh_v * p_ref[...] * sig * (1.0 - sig)).astype(BF16)
        dp_ref[...] = (dh_v * sig).astype(BF16)

    return _row_call(body, name, t, r, [_row_spec(r, d)] * 3, (gpre, pp, dh), [_row_spec(r, d)] * 2,
                     [jax.ShapeDtypeStruct((t, d), BF16)] * 2)


def _loss_and_grad(h, target, name):
    t, d = h.shape
    r = _pick(t, (256, 128))

    def body(h_ref, t_ref, l_ref, dh_ref):
        _zero_at_first_step([l_ref])
        err = h_ref[...] - t_ref[...]
        l_ref[...] += jnp.sum(jnp.sum(err * err, axis=-1, keepdims=True), axis=0, keepdims=True) * (0.5 / d)
        dh_ref[...] = err * (1.0 / d)

    return _row_call(body, name, t, r, [_row_spec(r, d)] * 2, (h, target), [_const_spec((1, 1)), _row_spec(r, d)],
                     [jax.ShapeDtypeStruct((1, 1), F32), jax.ShapeDtypeStruct((t, d), F32)])


def _ew_rows(rows, cols):
    cap = max(8, (1 << 19) // max(cols, 1))
    for cand in (1024, 512, 256, 128, 64, 32, 16, 8):
        if cand <= cap and rows % cand == 0:
            return cand
    return rows


def _pair_sum_bf16(a, b, name):
    n, rows, cols = a.shape
    rb = _ew_rows(rows, cols)

    def body(a_ref, b_ref, o_ref):
        o_ref[...] = (a_ref[...] + b_ref[...]).astype(BF16)

    spec = pl.BlockSpec((1, rb, cols), lambda s, i: (s, i, 0))
    return pl.pallas_call(body, name=name, grid=(n, rows // rb), in_specs=[spec, spec], out_specs=spec,
                          out_shape=jax.ShapeDtypeStruct(a.shape, BF16),
                          compiler_params=_params(("parallel", "parallel")))(a, b)


def _shard_sum(a, b, recv, name):
    rows, cols = a.shape
    rb = _ew_rows(rows, cols)

    def body(a_ref, b_ref, r_ref, o_ref):
        o_ref[...] = ((a_ref[...] + b_ref[...]) + r_ref[0].astype(F32) + r_ref[1].astype(F32)
                      + r_ref[2].astype(F32))

    spec = pl.BlockSpec((rb, cols), lambda i: (i, 0))
    return pl.pallas_call(body, name=name, grid=(rows // rb,),
                          in_specs=[spec, spec, pl.BlockSpec((N_SHARD - 1, rb, cols), lambda i: (0, i, 0))],
                          out_specs=spec, out_shape=jax.ShapeDtypeStruct(a.shape, F32),
                          compiler_params=_params(("parallel",)))(a, b, recv)


def _sum_devices(g, name):
    n, rows, cols = g.shape
    rb = _ew_rows(rows, cols)

    def body(g_ref, o_ref):
        acc = g_ref[0]
        for k in range(1, n):
            acc = acc + g_ref[k]
        o_ref[...] = acc

    return pl.pallas_call(body, name=name, grid=(rows // rb,),
                          in_specs=[pl.BlockSpec((n, rb, cols), lambda i: (0, i, 0))],
                          out_specs=pl.BlockSpec((rb, cols), lambda i: (i, 0)),
                          out_shape=jax.ShapeDtypeStruct((rows, cols), F32),
                          compiler_params=_params(("parallel",)))(g)


def _adamw_update(w, g_v, m, v):
    m_new = ADAM_B1 * m + (1.0 - ADAM_B1) * g_v
    v_new = ADAM_B2 * v + (1.0 - ADAM_B2) * (g_v * g_v)
    m_hat = m_new / (1.0 - ADAM_B1 ** ADAM_STEP)
    v_hat = v_new / (1.0 - ADAM_B2 ** ADAM_STEP)
    return -ADAM_LR * (m_hat / (jnp.sqrt(v_hat) + ADAM_EPS) + ADAM_WD * w), m_new, v_new


def _adamw(w, g, m, v, name):
    rows, cols = w.shape
    rb = _ew_rows(rows, cols)

    def body(w_ref, g_ref, m_ref, v_ref, d_ref, mo_ref, vo_ref):
        d_ref[...], mo_ref[...], vo_ref[...] = _adamw_update(w_ref[...], g_ref[...], m_ref[...], v_ref[...])

    spec = pl.BlockSpec((rb, cols), lambda i: (i, 0))
    return pl.pallas_call(body, name=name, grid=(rows // rb,), in_specs=[spec] * 4, out_specs=[spec] * 3,
                          out_shape=[jax.ShapeDtypeStruct(w.shape, F32)] * 3,
                          compiler_params=_params(("parallel",)))(w, g, m, v)


def _adamw_two_halves(w, own, recv, core_flag, m, v, name):
    rows, cols = own.shape
    rb = _ew_rows(rows, cols)
    nb = rows // rb

    def body(w_ref, own_ref, recv_ref, flag_ref, m_ref, v_ref, g_ref, d_ref, mo_ref, vo_ref):
        half = pl.program_id(0).astype(F32)
        g_v = jnp.where(flag_ref[...] == half, own_ref[...], recv_ref[...])
        g_ref[...] = g_v
        d_ref[...], mo_ref[...], vo_ref[...] = _adamw_update(w_ref[...], g_v, m_ref[...], v_ref[...])

    full = pl.BlockSpec((rb, cols), lambda k, i: (k * nb + i, 0))
    half_spec = pl.BlockSpec((rb, cols), lambda k, i: (i, 0))
    return pl.pallas_call(
        body, name=name, grid=(2, nb),
        in_specs=[full, half_spec, half_spec, pl.BlockSpec((1, 1), lambda k, i: (0, 0)), full, full],
        out_specs=[full] * 4, out_shape=[jax.ShapeDtypeStruct(w.shape, F32)] * 4,
        compiler_params=_params(("parallel", "parallel")))(w, own, recv, core_flag, m, v)


def _place():
    return lax.axis_index("x"), lax.axis_index("y"), lax.axis_index("c")


def _other_chips(x, y):
    return [(1 - x, y), (x, 1 - y), (1 - x, 1 - y)]


_ANY = pl.BlockSpec(memory_space=pl.ANY)


def _gather_shards(shards, name):
    n = len(shards)
    n_peer = N_SHARD - 1

    def body(*refs):
        ins, outs = refs[:n], refs[n:2 * n]
        ici_send, ici_recv, d2d_send, d2d_recv = refs[2 * n:]
        x, y, c = _place()
        chips = _other_chips(x, y)

        def half(a, which):
            h0 = shards[a].shape[0] // 2
            return pl.ds(which * h0, h0)

        def ici_copy(a, k, slot, px, py):
            src = ins[a].at[half(a, c)]
            return pltpu.make_async_remote_copy(
                src_ref=src, dst_ref=outs[a].at[slot, half(a, c)], send_sem=ici_send.at[a * n_peer + k],
                recv_sem=ici_recv.at[a * n_peer + k], device_id=(px, py, c), device_id_type=MESH)

        def d2d_copy(a, k, slot, which):
            rows = outs[a].at[slot, half(a, which)]
            return pltpu.make_async_remote_copy(
                src_ref=rows, dst_ref=rows, send_sem=d2d_send.at[a * n_peer + k],
                recv_sem=d2d_recv.at[a * n_peer + k], device_id=(x, y, 1 - c), device_id_type=MESH)

        sends = []
        for a in range(n):
            for k, (px, py) in enumerate(chips):
                cp = ici_copy(a, k, 2 * x + y, px, py)
                cp.start()
                sends.append(cp)
        for a in range(n):
            for k, (px, py) in enumerate(chips):
                ici_copy(a, k, 2 * px + py, px, py).wait_recv()
                fw = d2d_copy(a, k, 2 * px + py, c)
                fw.start()
                sends.append(fw)
        for a in range(n):
            for k, (px, py) in enumerate(chips):
                d2d_copy(a, k, 2 * px + py, 1 - c).wait_recv()
        for cp in sends:
            cp.wait_send()

    n_sem = n * n_peer
    return pl.pallas_call(
        body, name=name, in_specs=[_ANY] * n, out_specs=[_ANY] * n,
        out_shape=[jax.ShapeDtypeStruct((N_SHARD,) + s.shape, s.dtype) for s in shards],
        scratch_shapes=[pltpu.SemaphoreType.DMA((n_sem,)), pltpu.SemaphoreType.DMA((n_sem,)),
                        pltpu.SemaphoreType.DMA((n_sem,)), pltpu.SemaphoreType.DMA((n_sem,))],
    )(*shards)


def _to_sibling(arrs, other_half, name):
    n = len(arrs)

    def body(*refs):
        ins, outs = refs[:n], refs[n:2 * n]
        send_sems, recv_sems = refs[2 * n:]
        x, y, c = _place()
        sends = []
        for a in range(n):
            cp = pltpu.make_async_remote_copy(
                src_ref=ins[a].at[1 - c] if other_half else ins[a], dst_ref=outs[a], send_sem=send_sems.at[a],
                recv_sem=recv_sems.at[a], device_id=(x, y, 1 - c), device_id_type=MESH)
            cp.start()
            sends.append(cp)
        for cp in sends:
            cp.wait_recv()
        for cp in sends:
            cp.wait_send()

    return pl.pallas_call(
        body, name=name, in_specs=[_ANY] * n, out_specs=[_ANY] * n,
        out_shape=[jax.ShapeDtypeStruct(g.shape[1:] if other_half else g.shape, g.dtype) for g in arrs],
        scratch_shapes=[pltpu.SemaphoreType.DMA((n,)), pltpu.SemaphoreType.DMA((n,))],
    )(*arrs)


def _to_owner_chips(parts, name):
    n = len(parts)
    n_peer = N_SHARD - 1

    def body(*refs):
        ins, outs = refs[:n], refs[n:2 * n]
        send_sems, recv_sems = refs[2 * n:]
        x, y, c = _place()
        chips = _other_chips(x, y)
        sends = []
        for a in range(n):
            for k, (px, py) in enumerate(chips):
                cp = pltpu.make_async_remote_copy(
                    src_ref=ins[a].at[2 * px + py], dst_ref=outs[a].at[k], send_sem=send_sems.at[a * n_peer + k],
                    recv_sem=recv_sems.at[a * n_peer + k], device_id=(px, py, c), device_id_type=MESH)
                cp.start()
                sends.append(cp)
        for cp in sends:
            cp.wait_recv()
        for cp in sends:
            cp.wait_send()

    return pl.pallas_call(
        body, name=name, in_specs=[_ANY] * n, out_specs=[_ANY] * n,
        out_shape=[jax.ShapeDtypeStruct((n_peer,) + p.shape[1:], p.dtype) for p in parts],
        scratch_shapes=[pltpu.SemaphoreType.DMA((n * n_peer,)), pltpu.SemaphoreType.DMA((n * n_peer,))],
    )(*parts)


def _gather_devices(buf, name):
    n_peer = N_DEV - 1

    def body(in_ref, out_ref, send_sems, recv_sems, local_sem):
        x, y, c = _place()
        me = 4 * x + 2 * y + c
        mine = pltpu.make_async_copy(in_ref, out_ref.at[me], local_sem)
        mine.start()
        peers = []
        for k in range(1, N_DEV):
            fx, fy, fc = (k >> 2) & 1, (k >> 1) & 1, k & 1
            peers.append((x ^ fx, y ^ fy, c ^ fc))
        sends = []
        for k, peer in enumerate(peers):
            cp = pltpu.make_async_remote_copy(
                src_ref=in_ref, dst_ref=out_ref.at[me], send_sem=send_sems.at[k], recv_sem=recv_sems.at[k],
                device_id=peer, device_id_type=MESH)
            cp.start()
            sends.append(cp)
        for k, (px, py, pc) in enumerate(peers):
            pltpu.make_async_remote_copy(
                src_ref=in_ref, dst_ref=out_ref.at[4 * px + 2 * py + pc], send_sem=send_sems.at[k],
                recv_sem=recv_sems.at[k], device_id=(px, py, pc), device_id_type=MESH).wait_recv()
        for cp in sends:
            cp.wait_send()
        mine.wait()

    return pl.pallas_call(
        body, name=name, in_specs=[_ANY], out_specs=_ANY,
        out_shape=jax.ShapeDtypeStruct((N_DEV,) + buf.shape, buf.dtype),
        scratch_shapes=[pltpu.SemaphoreType.DMA((n_peer,)), pltpu.SemaphoreType.DMA((n_peer,)),
                        pltpu.SemaphoreType.DMA(())],
    )(buf)


class _Dims:
    def __init__(self, x, p, w_in, sgu_norm, conv_w, kv_norm, w_ukv, w_out):
        self.t, self.d = x.shape[1], x.shape[2]
        self.depth = w_in.shape[0]
        self.ple = p.shape[3]
        self.in_w = w_in.shape[2] * N_SHARD
        self.ah = sgu_norm.shape[1]
        self.aw = self.ah * HEAD
        self.bw = conv_w.shape[2] * N_SHARD
        self.kvr = kv_norm.shape[1]
        self.ch = w_ukv.shape[2] * N_SHARD // (2 * HEAD)
        self.cw = self.ch * HEAD
        self.mix = w_out.shape[1] * N_SHARD
        assert self.mix == self.aw + self.bw + self.cw and self.aw == self.bw
        self.qw = self.ch * 2 * HEAD
        segs = [('a', 3 * self.aw, self.aw), ('b', 4 * self.bw, self.bw), ('ckv', self.kvr, self.kvr),
                ('q', self.qw, self.qw), ('cz', self.cw, self.cw), ('kr', HEAD, HEAD)]
        off = 0
        self.off = {}
        for nm, width, align in segs:
            off = -(-off // align) * align
            self.off[nm] = off
            off += width
        self.inp = -(-off // 512) * 512
        q_real = self.ch * (HEAD + ROPE)
        widths = [3 * self.aw, 4 * self.bw, q_real, self.kvr, ROPE, self.cw]
        assert sum(widths) == self.in_w
        starts = [0]
        for wd in widths:
            starts.append(starts[-1] + wd)
        self.src = dict(zip(['a', 'b', 'q', 'ckv', 'kr', 'cz'], zip(starts[:-1], widths)))


def _rearrange_w_in(w, dm):
    lead = w.shape[:-1]
    pieces = {}
    for nm in ('a', 'b', 'ckv', 'cz'):
        s, wd = dm.src[nm]
        pieces[nm] = w[..., s:s + wd]
    s, wd = dm.src['q']
    q = w[..., s:s + wd].reshape(lead + (dm.ch, HEAD + ROPE))
    pieces['q'] = jnp.pad(q, [(0, 0)] * (q.ndim - 1) + [(0, 2 * HEAD - HEAD - ROPE)]).reshape(lead + (dm.qw,))
    s, wd = dm.src['kr']
    pieces['kr'] = jnp.pad(w[..., s:s + wd], [(0, 0)] * len(lead) + [(0, HEAD - ROPE)])
    out, cur = [], 0
    for nm in sorted(dm.off, key=lambda k: dm.off[k]):
        if dm.off[nm] > cur:
            out.append(jnp.zeros(lead + (dm.off[nm] - cur,), w.dtype))
        out.append(pieces[nm])
        cur = dm.off[nm] + pieces[nm].shape[-1]
    if dm.inp > cur:
        out.append(jnp.zeros(lead + (dm.inp - cur,), w.dtype))
    return jnp.concatenate(out, axis=-1)


def _unarrange_w_in(g, dm):
    lead = g.shape[:-1]

    def seg(nm, width):
        return g[..., dm.off[nm]:dm.off[nm] + width]

    q = seg('q', dm.qw).reshape(lead + (dm.ch, 2 * HEAD))[..., :HEAD + ROPE].reshape(lead + (dm.ch * (HEAD + ROPE),))
    return jnp.concatenate([seg('a', 3 * dm.aw), seg('b', 4 * dm.bw), q, seg('ckv', dm.kvr), seg('kr', ROPE),
                            seg('cz', dm.cw)], axis=-1)


def _assemble_dproj(parts, dm, t):
    out, cur = [], 0
    for nm in sorted(dm.off, key=lambda k: dm.off[k]):
        if dm.off[nm] > cur:
            out.append(jnp.zeros((t, dm.off[nm] - cur), BF16))
        out.append(parts[nm])
        cur = dm.off[nm] + parts[nm].shape[-1]
    if dm.inp > cur:
        out.append(jnp.zeros((t, dm.inp - cur), BF16))
    return jnp.concatenate(out, axis=-1)


def _rope_tables(positions):
    inv = 1.0 / (ROPE_BASE ** (jnp.arange(0, ROPE, 2, dtype=F32) / ROPE))
    ang = positions.astype(F32)[:, None] * inv
    cos, sin = jnp.cos(ang), jnp.sin(ang)
    t = positions.shape[0]
    half = ROPE // 2
    cos_t = jnp.concatenate([cos, cos, jnp.zeros((t, HEAD - ROPE), F32)], axis=-1)
    sin_a = jnp.concatenate([-sin, jnp.zeros((t, HEAD - half), F32)], axis=-1)
    sin_b = jnp.concatenate([jnp.zeros((t, half), F32), sin, jnp.zeros((t, HEAD - ROPE), F32)], axis=-1)
    return cos_t, sin_a, sin_b


def _pad_gain(g):
    return jnp.pad(g, (0, HEAD - g.shape[0]))[None, :]


def _shard_major(g, axis):
    shape = g.shape
    g = g.reshape(shape[:axis] + (N_SHARD, shape[axis] // N_SHARD) + shape[axis + 1:])
    g = jnp.moveaxis(g, axis, 0)
    cols = g.shape[-1]
    g = g.reshape(N_SHARD, 2, -1, cols)
    return jnp.swapaxes(g, 0, 1)


def _pack(arrs):
    flat = jnp.concatenate([a.reshape(-1) for a in arrs])
    pad = (-flat.shape[0]) % (8 * HEAD)
    return jnp.pad(flat, (0, pad)).reshape(-1, HEAD)


def _unpack(buf, shapes):
    flat = buf.reshape(-1)
    out, cur = [], 0
    for s in shapes:
        size = 1
        for v in s:
            size *= v
        out.append(flat[cur:cur + size].reshape(s))
        cur += size
    return out


def kernel(x, p, positions, attn_norm, w_in, sgu_norm, w_spatial, b_spatial, conv_w, conv_b, kv_norm, w_ukv, q_nope_norm, q_rope_norm, k_nope_norm, k_rope_norm, out_norm, w_out, ple_norm, w_ple_gate, w_ple_proj, loss_target, m_attn_norm, m_w_in, m_sgu_norm, m_w_spatial, m_b_spatial, m_conv_w, m_conv_b, m_kv_norm, m_w_ukv, m_q_nope_norm, m_q_rope_norm, m_k_nope_norm, m_k_rope_norm, m_out_norm, m_w_out, m_ple_norm, m_w_ple_gate, m_w_ple_proj, v_attn_norm, v_w_in, v_sgu_norm, v_w_spatial, v_b_spatial, v_conv_w, v_conv_b, v_kv_norm, v_w_ukv, v_q_nope_norm, v_q_rope_norm, v_k_nope_norm, v_k_rope_norm, v_out_norm, v_w_out, v_ple_norm, v_w_ple_gate, v_w_ple_proj):
    weights = dict(attn_norm=attn_norm, w_in=w_in, sgu_norm=sgu_norm, w_spatial=w_spatial, b_spatial=b_spatial,
                   conv_w=conv_w, conv_b=conv_b, kv_norm=kv_norm, w_ukv=w_ukv, q_nope_norm=q_nope_norm,
                   q_rope_norm=q_rope_norm, k_nope_norm=k_nope_norm, k_rope_norm=k_rope_norm, out_norm=out_norm,
                   w_out=w_out, ple_norm=ple_norm, w_ple_gate=w_ple_gate, w_ple_proj=w_ple_proj)
    mom_m = dict(attn_norm=m_attn_norm, w_in=m_w_in, sgu_norm=m_sgu_norm, w_spatial=m_w_spatial,
                 b_spatial=m_b_spatial, conv_w=m_conv_w, conv_b=m_conv_b, kv_norm=m_kv_norm, w_ukv=m_w_ukv,
                 q_nope_norm=m_q_nope_norm, q_rope_norm=m_q_rope_norm, k_nope_norm=m_k_nope_norm,
                 k_rope_norm=m_k_rope_norm, out_norm=m_out_norm, w_out=m_w_out, ple_norm=m_ple_norm,
                 w_ple_gate=m_w_ple_gate, w_ple_proj=m_w_ple_proj)
    mom_v = dict(attn_norm=v_attn_norm, w_in=v_w_in, sgu_norm=v_sgu_norm, w_spatial=v_w_spatial,
                 b_spatial=v_b_spatial, conv_w=v_conv_w, conv_b=v_conv_b, kv_norm=v_kv_norm, w_ukv=v_w_ukv,
                 q_nope_norm=v_q_nope_norm, q_rope_norm=v_q_rope_norm, k_nope_norm=v_k_nope_norm,
                 k_rope_norm=v_k_rope_norm, out_norm=v_out_norm, w_out=v_w_out, ple_norm=v_ple_norm,
                 w_ple_gate=v_w_ple_gate, w_ple_proj=v_w_ple_proj)
    dm = _Dims(x, p, w_in, sgu_norm, conv_w, kv_norm, w_ukv, w_out)
    t, d, depth = dm.t, dm.d, dm.depth
    shard = 2 * lax.axis_index("x") + lax.axis_index("y")
    core = lax.axis_index("c")
    scale = float(HEAD + ROPE) ** -0.5

    local = [weights[n].astype(BF16) for n in BIG] + [conv_w]
    gathered = [lax.dynamic_update_slice(g, mine[None], (shard,) + (0,) * mine.ndim)
                for g, mine in zip(_gather_shards(local, "gather_weights"), local)]
    full = {n: jnp.concatenate([gathered[i][s] for s in range(N_SHARD)], axis=BIG_AXIS[n])
            for i, n in enumerate(BIG)}
    conv_w_full = jnp.concatenate([gathered[len(BIG)][s] for s in range(N_SHARD)], axis=2)
    w_in_r = _rearrange_w_in(full['w_in'], dm)

    tabs = _rope_tables(positions[0])
    h = x[0]
    saved = []
    for i in range(depth):
        tag = f"l{i}_"
        ga, gb, gc = (out_norm[i][None, :dm.aw], out_norm[i][None, dm.aw:dm.aw + dm.bw],
                      out_norm[i][None, dm.aw + dm.bw:])
        ws_b = w_spatial[i].astype(BF16)
        bb = jnp.broadcast_to(b_spatial[i][:, :, None], (dm.ah, HEAD, HEAD))
        qn_g, qr_g = q_nope_norm[i][None, :], _pad_gain(q_rope_norm[i])
        kn_g, kr_g = k_nope_norm[i][None, :], _pad_gain(k_rope_norm[i])
        kv_g = kv_norm[i][None, :]
        hn = _norm_fwd(h, attn_norm[i][None, :], tag + "norm1")
        proj = _matmul(hn, w_in_r[i], 'nn', BF16, tag + "proj")
        ya = _sgu_fwd(proj, dm.off['a'], dm.aw, sgu_norm[i], ws_b, bb, ga, tag + "sgu")
        yb, yconv = _conv_fwd(proj, dm.off['b'], dm.bw, conv_w_full[i], conv_b[i][None, :], gb, tag + "conv")
        q_cat, ckv_n, kr_rot = _mla_prep_fwd(proj, dm.off['q'], dm.off['ckv'], dm.off['kr'], dm.ch, dm.kvr, tabs,
                                             qn_g, qr_g, kr_g, kv_g, tag + "mla_prep")
        kv = _matmul(ckv_n, full['w_ukv'][i], 'nn', F32, tag + "kv_up")
        k_cat, v_aug = _kv_prep_fwd(kv, kr_rot, dm.ch, kn_g, tag + "kv_prep")
        o, lse = _attn_fwd(q_cat, k_cat, v_aug, dm.ch, scale, tag + "attn")
        yc = _attn_post_fwd(o, proj, dm.off['cz'], dm.cw, gc, tag + "attn_post")
        y = jnp.concatenate([ya, yb, yc], axis=-1)
        h1 = _matmul(y, full['w_out'][i], 'nn', F32, tag + "out", add=h)
        hn2 = _norm_fwd(h1, ple_norm[i][None, :], tag + "norm2")
        gpre = _matmul(hn2, full['w_ple_gate'][i], 'nn', F32, tag + "gate")
        p_b = p[i, 0].astype(BF16)
        pp = _matmul(p_b, full['w_ple_proj'][i], 'nn', F32, tag + "ple_proj")
        h2 = _ple_fwd(h1, gpre, pp, tag + "ple")
        saved.append(dict(h=h, hn=hn, proj=proj, yconv=yconv, q_cat=q_cat, ckv_n=ckv_n, kv=kv, k_cat=k_cat,
                          v=v_aug, o=o, lse=lse, y=y, h1=h1, hn2=hn2, gpre=gpre, pp=pp, p_b=p_b, ws_b=ws_b, bb=bb,
                          gains=(ga, gb, gc, qn_g, qr_g, kn_g, kr_g, kv_g)))
        h = h2

    loss_part, dh = _loss_and_grad(h, loss_target[0], "loss")
    loss = lax.psum(loss_part[0, 0], ("x", "y", "c"))

    grads = {n: [None] * depth for n in WEIGHTS}
    for i in reversed(range(depth)):
        tag = f"l{i}_b_"
        sv = saved[i]
        ga, gb, gc, qn_g, qr_g, kn_g, kr_g, kv_g = sv['gains']
        proj = sv['proj']
        dgpre, dpp = _ple_bwd(sv['gpre'], sv['pp'], dh, tag + "ple")
        grads['w_ple_proj'][i] = _matmul(sv['p_b'], dpp, 'tn', F32, tag + "d_w_ple_proj")
        grads['w_ple_gate'][i] = _matmul(sv['hn2'], dgpre, 'tn', F32, tag + "d_w_gate")
        d_hn2 = _matmul(dgpre, full['w_ple_gate'][i], 'nt', BF16, tag + "d_hn2")
        dh1, dh1_b, g_ple = _norm_bwd(sv['h1'], ple_norm[i][None, :], d_hn2, dh, tag + "norm2")
        grads['ple_norm'][i] = g_ple[0]
        grads['w_out'][i] = _matmul(sv['y'], dh1_b, 'tn', F32, tag + "d_w_out")
        dy = _matmul(dh1_b, full['w_out'][i], 'nt', BF16, tag + "d_y")
        ws_t = jnp.swapaxes(sv['ws_b'], 1, 2)
        d_a, g_sgu, g_ws, g_bs, g_ga = _sgu_bwd(proj, dm.off['a'], dm.aw, sgu_norm[i], sv['ws_b'], ws_t, sv['bb'],
                                                ga, dy, tag + "sgu")
        grads['sgu_norm'][i], grads['w_spatial'][i], grads['b_spatial'][i] = g_sgu, g_ws, g_bs[:, :, 0]
        dyc, d_bb, d_bz, g_gb, g_cb = _conv_bwd_gate(proj, dm.off['b'], dm.bw, sv['yconv'], gb, dy, tag + "conv_gate")
        d_bc, d_bh, g_cw = _conv_bwd_taps(proj, dm.off['b'], dm.bw, dyc, conv_w_full[i], tag + "conv_taps")
        grads['conv_b'][i], grads['conv_w'][i] = g_cb[0], g_cw
        d_b = jnp.concatenate([d_bb, d_bc, d_bh, d_bz], axis=-1)
        d_o, d_cz, dsum, g_gc = _attn_post_bwd(sv['o'], proj, dm.off['cz'], dm.cw, gc, dy,
                                               _col_block(dm.aw + dm.bw, dm.cw), tag + "attn_post")
        grads['out_norm'][i] = jnp.concatenate([g_ga[0], g_gb[0], g_gc[0]])
        dq_t, dk_cat, dv = _attn_bwd(sv['q_cat'], sv['k_cat'], sv['k_cat'].T, sv['v'], d_o,
                                     sv['lse'].reshape(dm.ch, 1, t), dsum.reshape(dm.ch, 1, t), dm.ch, scale,
                                     tag + "attn_bwd")
        dq_cat = dq_t.T.astype(BF16)
        dkv, dkr_rot, g_kn = _kv_prep_bwd(sv['kv'], dm.ch, kn_g, dk_cat, dv, tag + "kv_prep")
        grads['k_nope_norm'][i] = g_kn[0]
        grads['w_ukv'][i] = _matmul(sv['ckv_n'], dkv, 'tn', F32, tag + "d_w_ukv")
        dckv_n = _matmul(dkv, full['w_ukv'][i], 'nt', BF16, tag + "d_ckv")
        d_q, d_ckv, d_kr, g_qn, g_qr, g_kr, g_kv = _mla_prep_bwd(
            proj, dm.off['q'], dm.off['ckv'], dm.off['kr'], dm.ch, dm.kvr, tabs, qn_g, qr_g, kr_g, kv_g,
            dq_cat, scale, dckv_n, dkr_rot, tag + "mla_prep")
        grads['q_nope_norm'][i], grads['q_rope_norm'][i] = g_qn[0], g_qr[0, :ROPE]
        grads['k_rope_norm'][i], grads['kv_norm'][i] = g_kr[0, :ROPE], g_kv[0]
        dproj = _assemble_dproj(dict(a=d_a, b=d_b, ckv=d_ckv, q=d_q, cz=d_cz, kr=d_kr), dm, t)
        grads['w_in'][i] = _unarrange_w_in(_matmul(sv['hn'], dproj, 'tn', F32, tag + "d_w_in"), dm)
        d_hn = _matmul(dproj, w_in_r[i], 'nt', BF16, tag + "d_hn")
        dh, _, g_an = _norm_bwd(sv['h'], attn_norm[i][None, :], d_hn, dh1, tag + "norm1")
        grads['attn_norm'][i] = g_an[0]
    grad_x = dh[None]
    grads = {n: jnp.stack(grads[n]) for n in WEIGHTS}

    sm = [_shard_major(grads[n], BIG_AXIS[n]) for n in BIG]
    from_sibling = _to_sibling(sm, True, "grads_to_sibling")
    mine = [lax.dynamic_index_in_dim(g, core, 0, keepdims=False) for g in sm]
    chip_sums = [_pair_sum_bf16(a, b, f"chip_sum_{n}") for a, b, n in zip(mine, from_sibling, BIG)]
    from_chips = _to_owner_chips(chip_sums, "grads_to_owner_chips")
    halves = []
    for a, b, r3, n in zip(mine, from_sibling, from_chips, BIG):
        own_a = lax.dynamic_index_in_dim(a, shard, 0, keepdims=False)
        own_b = lax.dynamic_index_in_dim(b, shard, 0, keepdims=False)
        halves.append(_shard_sum(own_a, own_b, r3, f"shard_sum_{n}"))
    sibling_halves = _to_sibling(halves, False, "grads_share_sibling")
    core_flag = core.astype(F32).reshape(1, 1)
    out_g, out_d, out_m, out_v = {}, {}, {}, {}
    for n, own, recv in zip(BIG, halves, sibling_halves):
        shp = weights[n].shape
        cols = shp[-1]
        g_sum, d_w, m_new, v_new = _adamw_two_halves(
            weights[n].reshape(-1, cols), own, recv, core_flag, mom_m[n].reshape(-1, cols),
            mom_v[n].reshape(-1, cols), f"adamw_{n}")
        out_g[n], out_d[n], out_m[n], out_v[n] = (g_sum.reshape(shp), d_w.reshape(shp), m_new.reshape(shp),
                                                  v_new.reshape(shp))

    shapes = [grads[n].shape for n in SMALL]
    summed = _unpack(_sum_devices(_gather_devices(_pack([grads[n] for n in SMALL]), "gather_small_grads"),
                                  "sum_small_grads"), shapes)
    small_g = dict(zip(SMALL, summed))
    small_g['conv_w'] = lax.dynamic_slice_in_dim(small_g['conv_w'], shard * conv_w.shape[2], conv_w.shape[2], axis=2)
    local_shapes = [weights[n].shape for n in SMALL]
    d_s, m_s, v_s = _adamw(_pack([weights[n] for n in SMALL]), _pack([small_g[n] for n in SMALL]),
                           _pack([mom_m[n] for n in SMALL]), _pack([mom_v[n] for n in SMALL]), "adamw_small")
    for n, dd, mm, vv in zip(SMALL, _unpack(d_s, local_shapes), _unpack(m_s, local_shapes),
                             _unpack(v_s, local_shapes)):
        out_g[n], out_d[n], out_m[n], out_v[n] = small_g[n], dd, mm, vv

    return (loss, grad_x, *[out_g[n] for n in WEIGHTS], *[out_d[n] for n in WEIGHTS],
            *[out_m[n] for n in WEIGHTS], *[out_v[n] for n in WEIGHTS])
```

```python
import functools

import jax
import jax.numpy as jnp
from jax import lax
from jax.experimental import pallas as pl
from jax.experimental.pallas import tpu as pltpu

F32 = jnp.float32
BF16 = jnp.bfloat16
EPS = 1e-6
HEAD = 128
ROPE = 64
ROPE_BASE = 10000.0
CONV_TAPS = 3
N_SHARD = 4
N_DEV = 8
ADAM_LR = 0.001
ADAM_B1 = 0.9
ADAM_B2 = 0.999
ADAM_EPS = 1e-08
ADAM_WD = 0.01
ADAM_STEP = 10
MESH = pl.DeviceIdType.MESH
VMEM_LIMIT = 56 * 1024 * 1024
HALO_ROWS = 16

WEIGHTS = ['attn_norm', 'w_in', 'sgu_norm', 'w_spatial', 'b_spatial', 'conv_w', 'conv_b', 'kv_norm', 'w_ukv',
           'q_nope_norm', 'q_rope_norm', 'k_nope_norm', 'k_rope_norm', 'out_norm', 'w_out', 'ple_norm',
           'w_ple_gate', 'w_ple_proj']
BIG = ['w_in', 'w_ukv', 'w_out', 'w_ple_gate', 'w_ple_proj']
BIG_AXIS = {'w_in': 2, 'w_ukv': 2, 'w_out': 1, 'w_ple_gate': 1, 'w_ple_proj': 2}
SMALL = [n for n in WEIGHTS if n not in BIG]


def _pick(n, cands):
    for c in cands:
        if n % c == 0:
            return c
    return n


def _params(sem=None):
    return pltpu.CompilerParams(dimension_semantics=sem, vmem_limit_bytes=VMEM_LIMIT)


def _matmul(a, b, mode, out_dtype, name, add=None):
    if mode == 'nn':
        (m, k), n = a.shape, b.shape[1]
    elif mode == 'nt':
        (m, k), n = a.shape, b.shape[0]
    else:
        (k, m), n = a.shape, b.shape[1]
    tm = _pick(m, (1024, 512, 256, 128))
    tn = _pick(n, (1536, 1024, 512, 256, 128))
    tk = k if k <= 2048 else _pick(k, (2048, 1536, 1024, 512, 256, 128))
    nk = k // tk
    if mode == 'tn':
        a_spec = pl.BlockSpec((tk, tm), lambda i, j, kk: (kk, i))
        dims = (((0,), (0,)), ((), ()))
    else:
        a_spec = pl.BlockSpec((tm, tk), lambda i, j, kk: (i, kk))
        dims = (((1,), (0,)), ((), ())) if mode == 'nn' else (((1,), (1,)), ((), ()))
    if mode == 'nt':
        b_spec = pl.BlockSpec((tn, tk), lambda i, j, kk: (j, kk))
    else:
        b_spec = pl.BlockSpec((tk, tn), lambda i, j, kk: (kk, j))
    o_spec = pl.BlockSpec((tm, tn), lambda i, j, kk: (i, j))
    has_add = add is not None

    def body(*refs):
        a_ref, b_ref = refs[0], refs[1]
        add_ref = refs[2] if has_add else None
        o_ref = refs[3] if has_add else refs[2]

        def product():
            return lax.dot_general(a_ref[...], b_ref[...], dims, preferred_element_type=F32)

        def finish(res):
            if has_add:
                res = res + add_ref[...]
            o_ref[...] = res.astype(out_dtype)

        if nk == 1:
            finish(product())
        else:
            acc_ref = refs[-1]
            kk = pl.program_id(2)

            @pl.when(kk == 0)
            def _():
                acc_ref[...] = product()

            @pl.when((kk > 0) & (kk < nk - 1))
            def _():
                acc_ref[...] += product()

            @pl.when(kk == nk - 1)
            def _():
                finish(acc_ref[...] + product())

    in_specs = [a_spec, b_spec] + ([o_spec] if has_add else [])
    args = [a, b] + ([add] if has_add else [])
    return pl.pallas_call(
        body, name=name, grid=(m // tm, n // tn, nk), in_specs=in_specs, out_specs=o_spec,
        out_shape=jax.ShapeDtypeStruct((m, n), out_dtype),
        scratch_shapes=[pltpu.VMEM((tm, tn), F32)] if nk > 1 else [],
        compiler_params=_params(("parallel", "parallel", "arbitrary")),
    )(*args)


def _rms(x, n):
    r = lax.rsqrt(jnp.sum(x * x, axis=-1, keepdims=True) * (1.0 / n) + EPS)
    return x * r, r


def _rms_bwd(dxhat, xhat, r, n):
    return r * (dxhat - xhat * (jnp.sum(dxhat * xhat, axis=-1, keepdims=True) * (1.0 / n)))


def _sigmoid(z):
    return 1.0 / (1.0 + jnp.exp(-z))


def _silu_and_grad(z):
    sig = _sigmoid(z)
    return z * sig, sig * (1.0 + z * (1.0 - sig))


def _colsum(x):
    return jnp.sum(x, axis=0, keepdims=True)


def _rope(t, cos_t, sin_a, sin_b):
    return t * cos_t + pltpu.roll(t, 96, 1) * sin_a + pltpu.roll(t, 32, 1) * sin_b


def _rope_bwd(d, cos_t, sin_a, sin_b):
    return d * cos_t + pltpu.roll(d * sin_a, 32, 1) + pltpu.roll(d * sin_b, 96, 1)


def _shift_down(g, first_row):
    row = lax.broadcasted_iota(jnp.int32, g.shape, 0)
    return jnp.where(row == 0, first_row, pltpu.roll(g, 1, 0))


def _shift_up(g, last_row):
    n = g.shape[0]
    row = lax.broadcasted_iota(jnp.int32, g.shape, 0)
    return jnp.where(row == n - 1, last_row, pltpu.roll(g, n - 1, 0))


def _row_spec(r, w, col=0):
    return pl.BlockSpec((r, w), lambda i: (i, col))


def _const_spec(shape):
    nd = len(shape)
    return pl.BlockSpec(shape, lambda i: (0,) * nd)


def _col_block(off, w):
    assert off % w == 0, (off, w)
    return off // w


def _zero_at_first_step(refs):
    @pl.when(pl.program_id(0) == 0)
    def _():
        for ref in refs:
            ref[...] = jnp.zeros(ref.shape, ref.dtype)


def _row_call(body, name, t, r, in_specs, args, out_specs, out_shapes, scratch=()):
    return pl.pallas_call(
        body, name=name, grid=(t // r,), in_specs=in_specs, out_specs=out_specs, out_shape=out_shapes,
        scratch_shapes=list(scratch), compiler_params=_params(("arbitrary",)),
    )(*args)


def _norm_fwd(h, g, name):
    t, d = h.shape
    r = _pick(t, (256, 128))

    def body(h_ref, g_ref, o_ref):
        xhat, _ = _rms(h_ref[...], d)
        o_ref[...] = (xhat * g_ref[...]).astype(BF16)

    return _row_call(body, name, t, r, [_row_spec(r, d), _const_spec((1, d))], (h, g),
                     _row_spec(r, d), jax.ShapeDtypeStruct((t, d), BF16))


def _norm_bwd(h, g, d_hn, d_res, name):
    t, d = h.shape
    r = _pick(t, (256, 128))

    def body(h_ref, g_ref, dy_ref, dres_ref, dh_ref, dhb_ref, dg_ref):
        _zero_at_first_step([dg_ref])
        xhat, rr = _rms(h_ref[...], d)
        dy = dy_ref[...].astype(F32)
        dg_ref[...] += _colsum(dy * xhat)
        dh = dres_ref[...] + _rms_bwd(dy * g_ref[...], xhat, rr, d)
        dh_ref[...] = dh
        dhb_ref[...] = dh.astype(BF16)

    return _row_call(body, name, t, r,
                     [_row_spec(r, d), _const_spec((1, d)), _row_spec(r, d), _row_spec(r, d)], (h, g, d_hn, d_res),
                     [_row_spec(r, d), _row_spec(r, d), _const_spec((1, d))],
                     [jax.ShapeDtypeStruct((t, d), F32), jax.ShapeDtypeStruct((t, d), BF16),
                      jax.ShapeDtypeStruct((1, d), F32)])


def _sgu_scores(v, gs_ref, ws_ref, bb_ref, s_scr, r, ah, keep=None):
    for kk in range(r // HEAD):
        for hh in range(ah):
            rows, cols = slice(kk * HEAD, (kk + 1) * HEAD), slice(hh * HEAD, (hh + 1) * HEAD)
            vhat, rv = _rms(v[rows, cols], HEAD)
            vn = vhat * gs_ref[pl.ds(hh, 1), :]
            s_scr[rows, cols] = jnp.dot(ws_ref[hh], vn.astype(BF16), preferred_element_type=F32) + bb_ref[hh]
            if keep is not None:
                keep[(kk, hh)] = (vhat, rv, vn)


def _sgu_fwd(proj, off, aw, gs, ws, bb, ga, name):
    t = proj.shape[0]
    ah = aw // HEAD
    r = _pick(t, (256, 128))
    cb = _col_block(off, aw)

    def body(u_ref, v_ref, z_ref, gs_ref, ws_ref, bb_ref, ga_ref, o_ref, s_scr):
        _sgu_scores(v_ref[...].astype(F32), gs_ref, ws_ref, bb_ref, s_scr, r, ah)
        sil, _ = _silu_and_grad(z_ref[...].astype(F32))
        yhat, _ = _rms(u_ref[...].astype(F32) * s_scr[...] * sil, aw)
        o_ref[...] = (yhat * ga_ref[...]).astype(BF16)

    return _row_call(
        body, name, t, r,
        [_row_spec(r, aw, cb), _row_spec(r, aw, cb + 1), _row_spec(r, aw, cb + 2), _const_spec((ah, HEAD)),
         _const_spec((ah, HEAD, HEAD)), _const_spec((ah, HEAD, HEAD)), _const_spec((1, aw))],
        (proj, proj, proj, gs, ws, bb, ga),
        _row_spec(r, aw), jax.ShapeDtypeStruct((t, aw), BF16), scratch=[pltpu.VMEM((r, aw), F32)])


def _sgu_bwd(proj, off, aw, gs, ws, ws_t, bb, ga, dy, name):
    t = proj.shape[0]
    ah = aw // HEAD
    r = _pick(t, (256, 128))
    cb = _col_block(off, aw)

    def body(u_ref, v_ref, z_ref, gs_ref, ws_ref, wst_ref, bb_ref, ga_ref, dy_ref,
             d_ref, dgs_ref, dws_ref, db_ref, dga_ref, s_scr, dv_scr):
        _zero_at_first_step([dgs_ref, dws_ref, db_ref, dga_ref])
        keep = {}
        _sgu_scores(v_ref[...].astype(F32), gs_ref, ws_ref, bb_ref, s_scr, r, ah, keep)
        u, z, s = u_ref[...].astype(F32), z_ref[...].astype(F32), s_scr[...]
        sil, dsil = _silu_and_grad(z)
        yhat, rr = _rms(u * s * sil, aw)
        dy_f = dy_ref[...].astype(F32)
        dga_ref[...] += _colsum(dy_f * yhat)
        dya = _rms_bwd(dy_f * ga_ref[...], yhat, rr, aw)
        d_ref[:, 0:aw] = (dya * s * sil).astype(BF16)
        d_ref[:, 2 * aw:3 * aw] = (dya * u * s * dsil).astype(BF16)
        ds = dya * u * sil
        for kk in range(r // HEAD):
            for hh in range(ah):
                rows, cols = slice(kk * HEAD, (kk + 1) * HEAD), slice(hh * HEAD, (hh + 1) * HEAD)
                vhat, rv, vn = keep[(kk, hh)]
                ds_blk = ds[rows, cols]
                db_ref[hh] += jnp.sum(ds_blk, axis=1, keepdims=True)
                ds_b = ds_blk.astype(BF16)
                dws_ref[hh] += lax.dot_general(ds_b, vn.astype(BF16), (((1,), (1,)), ((), ())),
                                               preferred_element_type=F32)
                dvn = jnp.dot(wst_ref[hh], ds_b, preferred_element_type=F32)
                dgs_ref[pl.ds(hh, 1), :] += _colsum(dvn * vhat)
                dv_scr[rows, cols] = _rms_bwd(dvn * gs_ref[pl.ds(hh, 1), :], vhat, rv, HEAD)
        d_ref[:, aw:2 * aw] = dv_scr[...].astype(BF16)

    return _row_call(
        body, name, t, r,
        [_row_spec(r, aw, cb), _row_spec(r, aw, cb + 1), _row_spec(r, aw, cb + 2), _const_spec((ah, HEAD)),
         _const_spec((ah, HEAD, HEAD)), _const_spec((ah, HEAD, HEAD)), _const_spec((ah, HEAD, HEAD)),
         _const_spec((1, aw)), _row_spec(r, aw, 0)],
        (proj, proj, proj, gs, ws, ws_t, bb, ga, dy),
        [_row_spec(r, 3 * aw), _const_spec((ah, HEAD)), _const_spec((ah, HEAD, HEAD)), _const_spec((ah, HEAD, 1)),
         _const_spec((1, aw))],
        [jax.ShapeDtypeStruct((t, 3 * aw), BF16), jax.ShapeDtypeStruct((ah, HEAD), F32),
         jax.ShapeDtypeStruct((ah, HEAD, HEAD), F32), jax.ShapeDtypeStruct((ah, HEAD, 1), F32),
         jax.ShapeDtypeStruct((1, aw), F32)],
        scratch=[pltpu.VMEM((r, aw), F32), pltpu.VMEM((r, aw), F32)])


def _halo_specs(t, r, w, col, rows):
    per = r // rows
    last = t // rows - 1
    prev = pl.BlockSpec((rows, w), lambda i: (jnp.maximum(i * per - 1, 0), col))
    nxt = pl.BlockSpec((rows, w), lambda i: (jnp.minimum((i + 1) * per, last), col))
    return prev, nxt


def _edge_rows(prev_ref, next_ref, n_steps):
    i = pl.program_id(0)
    rows = prev_ref.shape[0]
    before = prev_ref[...].astype(F32)[rows - 1:rows, :] * (i > 0).astype(F32)
    after = next_ref[...].astype(F32)[0:1, :] * (i < n_steps - 1).astype(F32)
    return before, after


def _conv_fwd(proj, off, bw, cw, cb_, gb, name):
    t = proj.shape[0]
    r = _pick(t, (256, 128))
    n_steps = t // r
    c0 = _col_block(off, bw)
    cp, cn = _halo_specs(t, r, bw, c0 + 1, HALO_ROWS)
    hp, hn = _halo_specs(t, r, bw, c0 + 2, HALO_ROWS)

    def body(b_ref, c_ref, h_ref, z_ref, cp_ref, cn_ref, hp_ref, hn_ref, cw_ref, cb_ref, gb_ref, o_ref, yc_ref):
        g = c_ref[...].astype(F32) * h_ref[...].astype(F32)
        c_before, c_after = _edge_rows(cp_ref, cn_ref, n_steps)
        h_before, h_after = _edge_rows(hp_ref, hn_ref, n_steps)
        yconv = (cb_ref[...] + cw_ref[0:1, :] * _shift_down(g, c_before * h_before) + cw_ref[1:2, :] * g
                 + cw_ref[2:3, :] * _shift_up(g, c_after * h_after))
        yc_ref[...] = yconv
        sil, _ = _silu_and_grad(z_ref[...].astype(F32))
        yhat, _ = _rms(b_ref[...].astype(F32) * yconv * sil, bw)
        o_ref[...] = (yhat * gb_ref[...]).astype(BF16)

    return _row_call(
        body, name, t, r,
        [_row_spec(r, bw, c0), _row_spec(r, bw, c0 + 1), _row_spec(r, bw, c0 + 2), _row_spec(r, bw, c0 + 3),
         cp, cn, hp, hn, _const_spec((CONV_TAPS, bw)), _const_spec((1, bw)), _const_spec((1, bw))],
        (proj, proj, proj, proj, proj, proj, proj, proj, cw, cb_, gb),
        [_row_spec(r, bw), _row_spec(r, bw)],
        [jax.ShapeDtypeStruct((t, bw), BF16), jax.ShapeDtypeStruct((t, bw), F32)])


def _conv_bwd_gate(proj, off, bw, yconv, gb, dy, name):
    t = proj.shape[0]
    r = _pick(t, (256, 128))
    c0 = _col_block(off, bw)

    def body(b_ref, z_ref, yc_ref, gb_ref, dy_ref, dyc_ref, db_ref, dz_ref, dgb_ref, dcb_ref):
        _zero_at_first_step([dgb_ref, dcb_ref])
        b, z, yconv_v = b_ref[...].astype(F32), z_ref[...].astype(F32), yc_ref[...]
        sil, dsil = _silu_and_grad(z)
        yhat, rr = _rms(b * yconv_v * sil, bw)
        dy_f = dy_ref[...].astype(F32)
        dgb_ref[...] += _colsum(dy_f * yhat)
        dyb = _rms_bwd(dy_f * gb_ref[...], yhat, rr, bw)
        dyc = dyb * b * sil
        dyc_ref[...] = dyc
        dcb_ref[...] += _colsum(dyc)
        db_ref[...] = (dyb * yconv_v * sil).astype(BF16)
        dz_ref[...] = (dyb * b * yconv_v * dsil).astype(BF16)

    return _row_call(
        body, name, t, r,
        [_row_spec(r, bw, c0), _row_spec(r, bw, c0 + 3), _row_spec(r, bw), _const_spec((1, bw)), _row_spec(r, bw, 1)],
        (proj, proj, yconv, gb, dy),
        [_row_spec(r, bw), _row_spec(r, bw), _row_spec(r, bw), _const_spec((1, bw)), _const_spec((1, bw))],
        [jax.ShapeDtypeStruct((t, bw), F32), jax.ShapeDtypeStruct((t, bw), BF16), jax.ShapeDtypeStruct((t, bw), BF16),
         jax.ShapeDtypeStruct((1, bw), F32), jax.ShapeDtypeStruct((1, bw), F32)])


def _conv_bwd_taps(proj, off, bw, dyc, cw, name):
    t = proj.shape[0]
    r = _pick(t, (256, 128))
    n_steps = t // r
    c0 = _col_block(off, bw)
    cp, cn = _halo_specs(t, r, bw, c0 + 1, HALO_ROWS)
    hp, hn = _halo_specs(t, r, bw, c0 + 2, HALO_ROWS)
    dp, dn = _halo_specs(t, r, bw, 0, 8)

    def body(c_ref, h_ref, cp_ref, cn_ref, hp_ref, hn_ref, d_ref, dp_ref, dn_ref, cw_ref, dc_ref, dh_ref, dcw_ref):
        _zero_at_first_step([dcw_ref])
        c, h, d = c_ref[...].astype(F32), h_ref[...].astype(F32), d_ref[...]
        g = c * h
        c_before, c_after = _edge_rows(cp_ref, cn_ref, n_steps)
        h_before, h_after = _edge_rows(hp_ref, hn_ref, n_steps)
        d_before, d_after = _edge_rows(dp_ref, dn_ref, n_steps)
        dg = (cw_ref[0:1, :] * _shift_up(d, d_after) + cw_ref[1:2, :] * d + cw_ref[2:3, :] * _shift_down(d, d_before))
        dc_ref[...] = (dg * h).astype(BF16)
        dh_ref[...] = (dg * c).astype(BF16)
        dcw_ref[0:1, :] += _colsum(d * _shift_down(g, c_before * h_before))
        dcw_ref[1:2, :] += _colsum(d * g)
        dcw_ref[2:3, :] += _colsum(d * _shift_up(g, c_after * h_after))

    return _row_call(
        body, name, t, r,
        [_row_spec(r, bw, c0 + 1), _row_spec(r, bw, c0 + 2), cp, cn, hp, hn, _row_spec(r, bw), dp, dn,
         _const_spec((CONV_TAPS, bw))],
        (proj, proj, proj, proj, proj, proj, dyc, dyc, dyc, cw),
        [_row_spec(r, bw), _row_spec(r, bw), _const_spec((CONV_TAPS, bw))],
        [jax.ShapeDtypeStruct((t, bw), BF16), jax.ShapeDtypeStruct((t, bw), BF16),
         jax.ShapeDtypeStruct((CONV_TAPS, bw), F32)])


def _mla_prep_fwd(proj, q_off, ckv_off, kr_off, ch, kvr, tabs, qn_g, qr_g, kr_g, kv_g, name):
    t = proj.shape[0]
    r = _pick(t, (256, 128))
    qw = ch * 2 * HEAD
    cos_t, sin_a, sin_b = tabs

    def body(q_ref, ckv_ref, kr_ref, cos_ref, sa_ref, sb_ref, qn_ref, qr_ref, krg_ref, kvg_ref,
             qo_ref, co_ref, ko_ref):
        cos_v, sa, sb = cos_ref[...], sa_ref[...], sb_ref[...]
        for hh in range(ch):
            lo = hh * 2 * HEAD
            nhat, _ = _rms(q_ref[:, lo:lo + HEAD].astype(F32), HEAD)
            qo_ref[:, lo:lo + HEAD] = (nhat * qn_ref[...]).astype(BF16)
            rhat, _ = _rms(q_ref[:, lo + HEAD:lo + 2 * HEAD].astype(F32), ROPE)
            qo_ref[:, lo + HEAD:lo + 2 * HEAD] = _rope(rhat * qr_ref[...], cos_v, sa, sb).astype(BF16)
        khat, _ = _rms(kr_ref[...].astype(F32), ROPE)
        ko_ref[...] = _rope(khat * krg_ref[...], cos_v, sa, sb).astype(BF16)
        chat, _ = _rms(ckv_ref[...].astype(F32), kvr)
        co_ref[...] = (chat * kvg_ref[...]).astype(BF16)

    tab = _row_spec(r, HEAD)
    gain = _const_spec((1, HEAD))
    return _row_call(
        body, name, t, r,
        [_row_spec(r, qw, _col_block(q_off, qw)), _row_spec(r, kvr, _col_block(ckv_off, kvr)),
         _row_spec(r, HEAD, _col_block(kr_off, HEAD)), tab, tab, tab, gain, gain, gain, _const_spec((1, kvr))],
        (proj, proj, proj, cos_t, sin_a, sin_b, qn_g, qr_g, kr_g, kv_g),
        [_row_spec(r, qw), _row_spec(r, kvr), _row_spec(r, HEAD)],
        [jax.ShapeDtypeStruct((t, qw), BF16), jax.ShapeDtypeStruct((t, kvr), BF16),
         jax.ShapeDtypeStruct((t, HEAD), BF16)])


def _mla_prep_bwd(proj, q_off, ckv_off, kr_off, ch, kvr, tabs, qn_g, qr_g, kr_g, kv_g, dq_cat, dq_scale, dckv_n, dkr_rot,
                  name):
    t = proj.shape[0]
    r = _pick(t, (256, 128))
    qw = ch * 2 * HEAD
    cos_t, sin_a, sin_b = tabs

    def body(q_ref, ckv_ref, kr_ref, cos_ref, sa_ref, sb_ref, qn_ref, qr_ref, krg_ref, kvg_ref,
             dq_ref, dc_ref, dk_ref, dqo_ref, dco_ref, dko_ref, dqn_ref, dqr_ref, dkrg_ref, dkvg_ref):
        _zero_at_first_step([dqn_ref, dqr_ref, dkrg_ref, dkvg_ref])
        cos_v, sa, sb = cos_ref[...], sa_ref[...], sb_ref[...]
        for hh in range(ch):
            lo = hh * 2 * HEAD
            nhat, nr = _rms(q_ref[:, lo:lo + HEAD].astype(F32), HEAD)
            d_n = dq_ref[:, lo:lo + HEAD].astype(F32) * dq_scale
            dqn_ref[...] += _colsum(d_n * nhat)
            dqo_ref[:, lo:lo + HEAD] = _rms_bwd(d_n * qn_ref[...], nhat, nr, HEAD).astype(BF16)
            rhat, rr = _rms(q_ref[:, lo + HEAD:lo + 2 * HEAD].astype(F32), ROPE)
            d_t = _rope_bwd(dq_ref[:, lo + HEAD:lo + 2 * HEAD].astype(F32) * dq_scale, cos_v, sa, sb)
            dqr_ref[...] += _colsum(d_t * rhat)
            dqo_ref[:, lo + HEAD:lo + 2 * HEAD] = _rms_bwd(d_t * qr_ref[...], rhat, rr, ROPE).astype(BF16)
        khat, kr_r = _rms(kr_ref[...].astype(F32), ROPE)
        d_k = _rope_bwd(dk_ref[...], cos_v, sa, sb)
        dkrg_ref[...] += _colsum(d_k * khat)
        dko_ref[...] = _rms_bwd(d_k * krg_ref[...], khat, kr_r, ROPE).astype(BF16)
        chat, cr = _rms(ckv_ref[...].astype(F32), kvr)
        d_c = dc_ref[...].astype(F32)
        dkvg_ref[...] += _colsum(d_c * chat)
        dco_ref[...] = _rms_bwd(d_c * kvg_ref[...], chat, cr, kvr).astype(BF16)

    tab = _row_spec(r, HEAD)
    gain = _const_spec((1, HEAD))
    return _row_call(
        body, name, t, r,
        [_row_spec(r, qw, _col_block(q_off, qw)), _row_spec(r, kvr, _col_block(ckv_off, kvr)),
         _row_spec(r, HEAD, _col_block(kr_off, HEAD)), tab, tab, tab, gain, gain, gain, _const_spec((1, kvr)),
         _row_spec(r, qw), _row_spec(r, kvr), _row_spec(r, HEAD)],
        (proj, proj, proj, cos_t, sin_a, sin_b, qn_g, qr_g, kr_g, kv_g, dq_cat, dckv_n, dkr_rot),
        [_row_spec(r, qw), _row_spec(r, kvr), _row_spec(r, HEAD), gain, gain, gain, _const_spec((1, kvr))],
        [jax.ShapeDtypeStruct((t, qw), BF16), jax.ShapeDtypeStruct((t, kvr), BF16),
         jax.ShapeDtypeStruct((t, HEAD), BF16), jax.ShapeDtypeStruct((1, HEAD), F32),
         jax.ShapeDtypeStruct((1, HEAD), F32), jax.ShapeDtypeStruct((1, HEAD), F32),
         jax.ShapeDtypeStruct((1, kvr), F32)])


def _kv_prep_fwd(kv, kr_rot, ch, kn_g, name):
    t = kv.shape[0]
    r = _pick(t, (256, 128))
    qw = ch * 2 * HEAD

    def body(kv_ref, kr_ref, kn_ref, ko_ref, vo_ref):
        ones = jnp.ones((r, HEAD), BF16)
        for hh in range(ch):
            lo = hh * 2 * HEAD
            nhat, _ = _rms(kv_ref[:, lo:lo + HEAD], HEAD)
            ko_ref[:, lo:lo + HEAD] = (nhat * kn_ref[...]).astype(BF16)
            ko_ref[:, lo + HEAD:lo + 2 * HEAD] = kr_ref[...]
            vo_ref[:, lo:lo + HEAD] = kv_ref[:, lo + HEAD:lo + 2 * HEAD].astype(BF16)
            vo_ref[:, lo + HEAD:lo + 2 * HEAD] = ones

    return _row_call(
        body, name, t, r, [_row_spec(r, qw), _row_spec(r, HEAD), _const_spec((1, HEAD))], (kv, kr_rot, kn_g),
        [_row_spec(r, qw), _row_spec(r, qw)],
        [jax.ShapeDtypeStruct((t, qw), BF16), jax.ShapeDtypeStruct((t, qw), BF16)])


def _kv_prep_bwd(kv, ch, kn_g, dk_cat, dv, name):
    t = kv.shape[0]
    r = _pick(t, (256, 128))
    qw = ch * 2 * HEAD

    def body(kv_ref, kn_ref, dk_ref, dv_ref, dkv_ref, dkr_ref, dkn_ref):
        _zero_at_first_step([dkn_ref])
        dkr = jnp.zeros((r, HEAD), F32)
        for hh in range(ch):
            lo = hh * 2 * HEAD
            nhat, nr = _rms(kv_ref[:, lo:lo + HEAD], HEAD)
            d_n = dk_ref[:, lo:lo + HEAD].astype(F32)
            dkn_ref[...] += _colsum(d_n * nhat)
            dkv_ref[:, lo:lo + HEAD] = _rms_bwd(d_n * kn_ref[...], nhat, nr, HEAD).astype(BF16)
            dkv_ref[:, lo + HEAD:lo + 2 * HEAD] = dv_ref[:, hh * HEAD:(hh + 1) * HEAD]
            dkr = dkr + dk_ref[:, lo + HEAD:lo + 2 * HEAD].astype(F32)
        dkr_ref[...] = dkr

    return _row_call(
        body, name, t, r,
        [_row_spec(r, qw), _const_spec((1, HEAD)), _row_spec(r, qw), _row_spec(r, ch * HEAD)], (kv, kn_g, dk_cat, dv),
        [_row_spec(r, qw), _row_spec(r, HEAD), _const_spec((1, HEAD))],
        [jax.ShapeDtypeStruct((t, qw), BF16), jax.ShapeDtypeStruct((t, HEAD), F32),
         jax.ShapeDtypeStruct((1, HEAD), F32)])


def _attn_tiles(t):
    return _pick(t, (1024, 512, 256, 128)), _pick(t, (1024, 512, 256, 128))


_NT = (((1,), (1,)), ((), ()))
LOG2E = 1.4426950408889634


def _attn_fwd(q_cat, k_cat, v_aug, ch, scale, name):
    t = q_cat.shape[0]
    tq, tk = _attn_tiles(t)
    nk = t // tk
    c2 = scale * LOG2E

    def body(q_ref, k_ref, v_ref, o_ref, lse_ref, s_scr, m_scr, acc_scr):
        j = pl.program_id(2)

        def scores(slot):
            s_scr[slot] = lax.dot_general(q_ref[...], k_ref[...], _NT, preferred_element_type=F32) * c2

        def absorb(slot):
            s = s_scr[slot]
            m_old = m_scr[...]
            m_new = jnp.maximum(m_old, jnp.max(s, axis=-1, keepdims=True))
            p = jnp.exp2(s - m_new).astype(BF16)
            acc_scr[...] = (jnp.exp2(m_old - m_new) * acc_scr[...]
                            + jnp.dot(p, v_ref[...], preferred_element_type=F32))
            m_scr[...] = m_new

        @pl.when(j == 0)
        def _():
            m_scr[...] = jnp.full(m_scr.shape, -jnp.inf, F32)
            acc_scr[...] = jnp.zeros(acc_scr.shape, F32)
            scores(0)

        for parity in (0, 1):
            @pl.when((j > 0) & (j < nk) & (j % 2 == parity))
            def _():
                scores(parity)
                absorb(1 - parity)

        @pl.when(j == nk)
        def _():
            absorb((nk - 1) % 2)
            acc = acc_scr[...]
            l_sum = acc[:, HEAD:]
            o_ref[...] = (acc[:, :HEAD] / l_sum).astype(BF16)
            lse_ref[0] = m_scr[...] + jnp.log(l_sum[:, 0:1]) * LOG2E

    return pl.pallas_call(
        body, name=name, grid=(ch, t // tq, nk + 1),
        in_specs=[pl.BlockSpec((tq, 2 * HEAD), lambda h, i, j: (i, h)),
                  pl.BlockSpec((tk, 2 * HEAD), lambda h, i, j: (jnp.minimum(j, nk - 1), h)),
                  pl.BlockSpec((tk, 2 * HEAD), lambda h, i, j: (jnp.maximum(j - 1, 0), h))],
        out_specs=[pl.BlockSpec((tq, HEAD), lambda h, i, j: (i, h)),
                   pl.BlockSpec((1, tq, 1), lambda h, i, j: (h, i, 0))],
        out_shape=[jax.ShapeDtypeStruct((t, ch * HEAD), BF16), jax.ShapeDtypeStruct((ch, t, 1), F32)],
        scratch_shapes=[pltpu.VMEM((2, tq, tk), F32), pltpu.VMEM((tq, 1), F32), pltpu.VMEM((tq, 2 * HEAD), F32)],
        compiler_params=_params(("parallel", "parallel", "arbitrary")),
    )(q_cat, k_cat, v_aug)


def _attn_bwd(q_cat, k_cat, k_cat_t, v_aug, do, lse_row, d_row, ch, scale, name):
    t = q_cat.shape[0]
    tk = _pick(t, (1024, 512, 256, 128))
    tq = _pick(t, (512, 256, 128))
    nk, nq = t // tk, t // tq
    c2 = scale * LOG2E

    def body(q_ref, do_ref, qp_ref, dop_ref, lse_ref, d_ref, k_ref, kt_ref, v_ref,
             dqt_ref, dk_ref, dv_ref, s_scr, dp_scr, dk_scr, dv_scr):
        ki, j = pl.program_id(1), pl.program_id(2)

        def products(slot):
            s_scr[slot] = lax.dot_general(k_ref[...], q_ref[...], _NT, preferred_element_type=F32) * c2
            dp_scr[slot] = lax.dot_general(v_ref[...], do_ref[...], _NT, preferred_element_type=F32)

        def absorb(slot):
            q, do_v = qp_ref[...], dop_ref[...]
            pt = jnp.exp2(s_scr[slot] - lse_ref[0])
            dv_scr[...] += jnp.dot(pt.astype(BF16), do_v, preferred_element_type=F32)
            dst = (pt * (dp_scr[slot] - d_ref[0])).astype(BF16)
            dk_scr[...] += jnp.dot(dst, q, preferred_element_type=F32)
            part = jnp.dot(kt_ref[...], dst, preferred_element_type=F32)
            cols = pl.ds(pl.multiple_of((j - 1) * tq, tq), tq)

            @pl.when(ki == 0)
            def _():
                dqt_ref[:, cols] = part

            @pl.when(ki > 0)
            def _():
                dqt_ref[:, cols] += part

        @pl.when(j == 0)
        def _():
            dk_scr[...] = jnp.zeros(dk_scr.shape, F32)
            dv_scr[...] = jnp.zeros(dv_scr.shape, F32)
            products(0)

        for parity in (0, 1):
            @pl.when((j > 0) & (j < nq) & (j % 2 == parity))
            def _():
                products(parity)
                absorb(1 - parity)

        @pl.when(j == nq)
        def _():
            absorb((nq - 1) % 2)
            dk_ref[...] = (dk_scr[...] * scale).astype(BF16)
            dv_ref[...] = dv_scr[...].astype(BF16)

    def cur(i):
        return jnp.minimum(i, nq - 1)

    def prev(i):
        return jnp.maximum(i - 1, 0)

    stat = pl.BlockSpec((1, 1, tq), lambda h, j, i: (h, 0, prev(i)))
    return pl.pallas_call(
        body, name=name, grid=(ch, nk, nq + 1),
        in_specs=[pl.BlockSpec((tq, 2 * HEAD), lambda h, j, i: (cur(i), h)),
                  pl.BlockSpec((tq, HEAD), lambda h, j, i: (cur(i), h)),
                  pl.BlockSpec((tq, 2 * HEAD), lambda h, j, i: (prev(i), h)),
                  pl.BlockSpec((tq, HEAD), lambda h, j, i: (prev(i), h)), stat, stat,
                  pl.BlockSpec((tk, 2 * HEAD), lambda h, j, i: (j, h)),
                  pl.BlockSpec((2 * HEAD, tk), lambda h, j, i: (h, j)),
                  pl.BlockSpec((tk, HEAD), lambda h, j, i: (j, 2 * h))],
        out_specs=[pl.BlockSpec((2 * HEAD, t), lambda h, j, i: (h, 0)),
                   pl.BlockSpec((tk, 2 * HEAD), lambda h, j, i: (j, h)),
                   pl.BlockSpec((tk, HEAD), lambda h, j, i: (j, h))],
        out_shape=[jax.ShapeDtypeStruct((ch * 2 * HEAD, t), F32), jax.ShapeDtypeStruct((t, ch * 2 * HEAD), BF16),
                   jax.ShapeDtypeStruct((t, ch * HEAD), BF16)],
        scratch_shapes=[pltpu.VMEM((2, tk, tq), F32), pltpu.VMEM((2, tk, tq), F32),
                        pltpu.VMEM((tk, 2 * HEAD), F32), pltpu.VMEM((tk, HEAD), F32)],
        compiler_params=_params(("parallel", "arbitrary", "arbitrary")),
    )(q_cat, do, q_cat, do, lse_row, d_row, k_cat, k_cat_t, v_aug)


def _attn_post_fwd(o, proj, z_off, cw, gc, name):
    t = o.shape[0]
    r = _pick(t, (256, 128))

    def body(o_ref, z_ref, gc_ref, y_ref):
        sil, _ = _silu_and_grad(z_ref[...].astype(F32))
        yhat, _ = _rms(o_ref[...].astype(F32) * sil, cw)
        y_ref[...] = (yhat * gc_ref[...]).astype(BF16)

    return _row_call(body, name, t, r,
                     [_row_spec(r, cw), _row_spec(r, cw, _col_block(z_off, cw)), _const_spec((1, cw))], (o, proj, gc),
                     _row_spec(r, cw), jax.ShapeDtypeStruct((t, cw), BF16))


def _attn_post_bwd(o, proj, z_off, cw, gc, dy, dy_col, name):
    t = o.shape[0]
    ch = cw // HEAD
    r = _pick(t, (256, 128))

    def body(o_ref, z_ref, gc_ref, dy_ref, do_ref, dz_ref, ds_ref, dgc_ref):
        _zero_at_first_step([dgc_ref])
        o_v, z = o_ref[...].astype(F32), z_ref[...].astype(F32)
        sil, dsil = _silu_and_grad(z)
        yhat, rr = _rms(o_v * sil, cw)
        dy_f = dy_ref[...].astype(F32)
        dgc_ref[...] += _colsum(dy_f * yhat)
        dyc = _rms_bwd(dy_f * gc_ref[...], yhat, rr, cw)
        do_b = (dyc * sil).astype(BF16)
        do_ref[...] = do_b
        dz_ref[...] = (dyc * o_v * dsil).astype(BF16)
        prod = do_b.astype(F32) * o_v
        for hh in range(ch):
            ds_ref[hh] = jnp.sum(prod[:, hh * HEAD:(hh + 1) * HEAD], axis=-1, keepdims=True)

    return _row_call(
        body, name, t, r,
        [_row_spec(r, cw), _row_spec(r, cw, _col_block(z_off, cw)), _const_spec((1, cw)), _row_spec(r, cw, dy_col)],
        (o, proj, gc, dy),
        [_row_spec(r, cw), _row_spec(r, cw), pl.BlockSpec((ch, r, 1), lambda i: (0, i, 0)), _const_spec((1, cw))],
        [jax.ShapeDtypeStruct((t, cw), BF16), jax.ShapeDtypeStruct((t, cw), BF16),
         jax.ShapeDtypeStruct((ch, t, 1), F32), jax.ShapeDtypeStruct((1, cw), F32)])


def _ple_fwd(h1, gpre, pp, name):
    t, d = h1.shape
    r = _pick(t, (256, 128))

    def body(h_ref, g_ref, p_ref, o_ref):
        o_ref[...] = h_ref[...] + _sigmoid(g_ref[...]) * p_ref[...]

    return _row_call(body, name, t, r, [_row_spec(r, d)] * 3, (h1, gpre, pp), _row_spec(r, d),
                     jax.ShapeDtypeStruct((t, d), F32))


def _ple_bwd(gpre, pp, dh, name):
    t, d = dh.shape
    r = _pick(t, (256, 128))

    def body(g_ref, p_ref, dh_ref, dg_ref, dp_ref):
        sig = _sigmoid(g_ref[...])
        dh_v = dh_ref[...]
        dg_ref[...] = (dh_v * p_ref[...] * sig * (1.0 - sig)).astype(BF16)
        dp_ref[...] = (dh_v * sig).astype(BF16)

    return _row_call(body, name, t, r, [_row_spec(r, d)] * 3, (gpre, pp, dh), [_row_spec(r, d)] * 2,
                     [jax.ShapeDtypeStruct((t, d), BF16)] * 2)


def _loss_and_grad(h, target, name):
    t, d = h.shape
    r = _pick(t, (256, 128))

    def body(h_ref, t_ref, l_ref, dh_ref):
        _zero_at_first_step([l_ref])
        err = h_ref[...] - t_ref[...]
        l_ref[...] += jnp.sum(jnp.sum(err * err, axis=-1, keepdims=True), axis=0, keepdims=True) * (0.5 / d)
        dh_ref[...] = err * (1.0 / d)

    return _row_call(body, name, t, r, [_row_spec(r, d)] * 2, (h, target), [_const_spec((1, 1)), _row_spec(r, d)],
                     [jax.ShapeDtypeStruct((1, 1), F32), jax.ShapeDtypeStruct((t, d), F32)])


def _ew_rows(rows, cols):
    cap = max(8, (1 << 19) // max(cols, 1))
    for cand in (1024, 512, 256, 128, 64, 32, 16, 8):
        if cand <= cap and rows % cand == 0:
            return cand
    return rows


def _pair_sum_bf16(a, b, name):
    n, rows, cols = a.shape
    rb = _ew_rows(rows, cols)

    def body(a_ref, b_ref, o_ref):
        o_ref[...] = (a_ref[...] + b_ref[...]).astype(BF16)

    spec = pl.BlockSpec((1, rb, cols), lambda s, i: (s, i, 0))
    return pl.pallas_call(body, name=name, grid=(n, rows // rb), in_specs=[spec, spec], out_specs=spec,
                          out_shape=jax.ShapeDtypeStruct(a.shape, BF16),
                          compiler_params=_params(("parallel", "parallel")))(a, b)


def _shard_sum(a, b, recv, name):
    rows, cols = a.shape
    rb = _ew_rows(rows, cols)

    def body(a_ref, b_ref, r_ref, o_ref):
        o_ref[...] = ((a_ref[...] + b_ref[...]) + r_ref[0].astype(F32) + r_ref[1].astype(F32)
                      + r_ref[2].astype(F32))

    spec = pl.BlockSpec((rb, cols), lambda i: (i, 0))
    return pl.pallas_call(body, name=name, grid=(rows // rb,),
                          in_specs=[spec, spec, pl.BlockSpec((N_SHARD - 1, rb, cols), lambda i: (0, i, 0))],
                          out_specs=spec, out_shape=jax.ShapeDtypeStruct(a.shape, F32),
                          compiler_params=_params(("parallel",)))(a, b, recv)


def _sum_devices(g, name):
    n, rows, cols = g.shape
    rb = _ew_rows(rows, cols)

    def body(g_ref, o_ref):
        acc = g_ref[0]
        for k in range(1, n):
            acc = acc + g_ref[k]
        o_ref[...] = acc

    return pl.pallas_call(body, name=name, grid=(rows // rb,),
                          in_specs=[pl.BlockSpec((n, rb, cols), lambda i: (0, i, 0))],
                          out_specs=pl.BlockSpec((rb, cols), lambda i: (i, 0)),
                          out_shape=jax.ShapeDtypeStruct((rows, cols), F32),
                          compiler_params=_params(("parallel",)))(g)


def _adamw_update(w, g_v, m, v):
    m_new = ADAM_B1 * m + (1.0 - ADAM_B1) * g_v
    v_new = ADAM_B2 * v + (1.0 - ADAM_B2) * (g_v * g_v)
    m_hat = m_new / (1.0 - ADAM_B1 ** ADAM_STEP)
    v_hat = v_new / (1.0 - ADAM_B2 ** ADAM_STEP)
    return -ADAM_LR * (m_hat / (jnp.sqrt(v_hat) + ADAM_EPS) + ADAM_WD * w), m_new, v_new


def _adamw(w, g, m, v, name):
    rows, cols = w.shape
    rb = _ew_rows(rows, cols)

    def body(w_ref, g_ref, m_ref, v_ref, d_ref, mo_ref, vo_ref):
        d_ref[...], mo_ref[...], vo_ref[...] = _adamw_update(w_ref[...], g_ref[...], m_ref[...], v_ref[...])

    spec = pl.BlockSpec((rb, cols), lambda i: (i, 0))
    return pl.pallas_call(body, name=name, grid=(rows // rb,), in_specs=[spec] * 4, out_specs=[spec] * 3,
                          out_shape=[jax.ShapeDtypeStruct(w.shape, F32)] * 3,
                          compiler_params=_params(("parallel",)))(w, g, m, v)


def _adamw_two_halves(w, own, recv, core_flag, m, v, name):
    rows, cols = own.shape
    rb = _ew_rows(rows, cols)
    nb = rows // rb

    def body(w_ref, own_ref, recv_ref, flag_ref, m_ref, v_ref, g_ref, d_ref, mo_ref, vo_ref):
        half = pl.program_id(0).astype(F32)
        g_v = jnp.where(flag_ref[...] == half, own_ref[...], recv_ref[...])
        g_ref[...] = g_v
        d_ref[...], mo_ref[...], vo_ref[...] = _adamw_update(w_ref[...], g_v, m_ref[...], v_ref[...])

    full = pl.BlockSpec((rb, cols), lambda k, i: (k * nb + i, 0))
    half_spec = pl.BlockSpec((rb, cols), lambda k, i: (i, 0))
    return pl.pallas_call(
        body, name=name, grid=(2, nb),
        in_specs=[full, half_spec, half_spec, pl.BlockSpec((1, 1), lambda k, i: (0, 0)), full, full],
        out_specs=[full] * 4, out_shape=[jax.ShapeDtypeStruct(w.shape, F32)] * 4,
        compiler_params=_params(("parallel", "parallel")))(w, own, recv, core_flag, m, v)


def _place():
    return lax.axis_index("x"), lax.axis_index("y"), lax.axis_index("c")


def _other_chips(x, y):
    return [(1 - x, y), (x, 1 - y), (1 - x, 1 - y)]


_ANY = pl.BlockSpec(memory_space=pl.ANY)


def _gather_shards(shards, name):
    n = len(shards)
    n_peer = N_SHARD - 1

    def body(*refs):
        ins, outs = refs[:n], refs[n:2 * n]
        ici_send, ici_recv, d2d_send, d2d_recv = refs[2 * n:]
        x, y, c = _place()
        chips = _other_chips(x, y)

        def half(a, which):
            h0 = shards[a].shape[0] // 2
            return pl.ds(which * h0, h0)

        def ici_copy(a, k, slot, px, py):
            src = ins[a].at[half(a, c)]
            return pltpu.make_async_remote_copy(
                src_ref=src, dst_ref=outs[a].at[slot, half(a, c)], send_sem=ici_send.at[a * n_peer + k],
                recv_sem=ici_recv.at[a * n_peer + k], device_id=(px, py, c), device_id_type=MESH)

        def d2d_copy(a, k, slot, which):
            rows = outs[a].at[slot, half(a, which)]
            return pltpu.make_async_remote_copy(
                src_ref=rows, dst_ref=rows, send_sem=d2d_send.at[a * n_peer + k],
                recv_sem=d2d_recv.at[a * n_peer + k], device_id=(x, y, 1 - c), device_id_type=MESH)

        sends = []
        for a in range(n):
            for k, (px, py) in enumerate(chips):
                cp = ici_copy(a, k, 2 * x + y, px, py)
                cp.start()
                sends.append(cp)
        for a in range(n):
            for k, (px, py) in enumerate(chips):
                ici_copy(a, k, 2 * px + py, px, py).wait_recv()
                fw = d2d_copy(a, k, 2 * px + py, c)
                fw.start()
                sends.append(fw)
        for a in range(n):
            for k, (px, py) in enumerate(chips):
                d2d_copy(a, k, 2 * px + py, 1 - c).wait_recv()
        for cp in sends:
            cp.wait_send()

    n_sem = n * n_peer
    return pl.pallas_call(
        body, name=name, in_specs=[_ANY] * n, out_specs=[_ANY] * n,
        out_shape=[jax.ShapeDtypeStruct((N_SHARD,) + s.shape, s.dtype) for s in shards],
        scratch_shapes=[pltpu.SemaphoreType.DMA((n_sem,)), pltpu.SemaphoreType.DMA((n_sem,)),
                        pltpu.SemaphoreType.DMA((n_sem,)), pltpu.SemaphoreType.DMA((n_sem,))],
    )(*shards)


def _to_sibling(arrs, other_half, name):
    n = len(arrs)

    def body(*refs):
        ins, outs = refs[:n], refs[n:2 * n]
        send_sems, recv_sems = refs[2 * n:]
        x, y, c = _place()
        sends = []
        for a in range(n):
            cp = pltpu.make_async_remote_copy(
                src_ref=ins[a].at[1 - c] if other_half else ins[a], dst_ref=outs[a], send_sem=send_sems.at[a],
                recv_sem=recv_sems.at[a], device_id=(x, y, 1 - c), device_id_type=MESH)
            cp.start()
            sends.append(cp)
        for cp in sends:
            cp.wait_recv()
        for cp in sends:
            cp.wait_send()

    return pl.pallas_call(
        body, name=name, in_specs=[_ANY] * n, out_specs=[_ANY] * n,
        out_shape=[jax.ShapeDtypeStruct(g.shape[1:] if other_half else g.shape, g.dtype) for g in arrs],
        scratch_shapes=[pltpu.SemaphoreType.DMA((n,)), pltpu.SemaphoreType.DMA((n,))],
    )(*arrs)


def _to_owner_chips(parts, name):
    n = len(parts)
    n_peer = N_SHARD - 1

    def body(*refs):
        ins, outs = refs[:n], refs[n:2 * n]
        send_sems, recv_sems = refs[2 * n:]
        x, y, c = _place()
        chips = _other_chips(x, y)
        sends = []
        for a in range(n):
            for k, (px, py) in enumerate(chips):
                cp = pltpu.make_async_remote_copy(
                    src_ref=ins[a].at[2 * px + py], dst_ref=outs[a].at[k], send_sem=send_sems.at[a * n_peer + k],
                    recv_sem=recv_sems.at[a * n_peer + k], device_id=(px, py, c), device_id_type=MESH)
                cp.start()
                sends.append(cp)
        for cp in sends:
            cp.wait_recv()
        for cp in sends:
            cp.wait_send()

    return pl.pallas_call(
        body, name=name, in_specs=[_ANY] * n, out_specs=[_ANY] * n,
        out_shape=[jax.ShapeDtypeStruct((n_peer,) + p.shape[1:], p.dtype) for p in parts],
        scratch_shapes=[pltpu.SemaphoreType.DMA((n * n_peer,)), pltpu.SemaphoreType.DMA((n * n_peer,))],
    )(*parts)


def _gather_devices(buf, name):
    n_peer = N_DEV - 1

    def body(in_ref, out_ref, send_sems, recv_sems, local_sem):
        x, y, c = _place()
        me = 4 * x + 2 * y + c
        mine = pltpu.make_async_copy(in_ref, out_ref.at[me], local_sem)
        mine.start()
        peers = []
        for k in range(1, N_DEV):
            fx, fy, fc = (k >> 2) & 1, (k >> 1) & 1, k & 1
            peers.append((x ^ fx, y ^ fy, c ^ fc))
        sends = []
        for k, peer in enumerate(peers):
            cp = pltpu.make_async_remote_copy(
                src_ref=in_ref, dst_ref=out_ref.at[me], send_sem=send_sems.at[k], recv_sem=recv_sems.at[k],
                device_id=peer, device_id_type=MESH)
            cp.start()
            sends.append(cp)
        for k, (px, py, pc) in enumerate(peers):
            pltpu.make_async_remote_copy(
                src_ref=in_ref, dst_ref=out_ref.at[4 * px + 2 * py + pc], send_sem=send_sems.at[k],
                recv_sem=recv_sems.at[k], device_id=(px, py, pc), device_id_type=MESH).wait_recv()
        for cp in sends:
            cp.wait_send()
        mine.wait()

    return pl.pallas_call(
        body, name=name, in_specs=[_ANY], out_specs=_ANY,
        out_shape=jax.ShapeDtypeStruct((N_DEV,) + buf.shape, buf.dtype),
        scratch_shapes=[pltpu.SemaphoreType.DMA((n_peer,)), pltpu.SemaphoreType.DMA((n_peer,)),
                        pltpu.SemaphoreType.DMA(())],
    )(buf)


class _Dims:
    def __init__(self, x, p, w_in, sgu_norm, conv_w, kv_norm, w_ukv, w_out):
        self.t, self.d = x.shape[1], x.shape[2]
        self.depth = w_in.shape[0]
        self.ple = p.shape[3]
        self.in_w = w_in.shape[2] * N_SHARD
        self.ah = sgu_norm.shape[1]
        self.aw = self.ah * HEAD
        self.bw = conv_w.shape[2] * N_SHARD
        self.kvr = kv_norm.shape[1]
        self.ch = w_ukv.shape[2] * N_SHARD // (2 * HEAD)
        self.cw = self.ch * HEAD
        self.mix = w_out.shape[1] * N_SHARD
        assert self.mix == self.aw + self.bw + self.cw and self.aw == self.bw
        self.qw = self.ch * 2 * HEAD
        segs = [('a', 3 * self.aw, self.aw), ('b', 4 * self.bw, self.bw), ('ckv', self.kvr, self.kvr),
                ('q', self.qw, self.qw), ('cz', self.cw, self.cw), ('kr', HEAD, HEAD)]
        off = 0
        self.off = {}
        for nm, width, align in segs:
            off = -(-off // align) * align
            self.off[nm] = off
            off += width
        self.inp = -(-off // 512) * 512
        q_real = self.ch * (HEAD + ROPE)
        widths = [3 * self.aw, 4 * self.bw, q_real, self.kvr, ROPE, self.cw]
        assert sum(widths) == self.in_w
        starts = [0]
        for wd in widths:
            starts.append(starts[-1] + wd)
        self.src = dict(zip(['a', 'b', 'q', 'ckv', 'kr', 'cz'], zip(starts[:-1], widths)))


def _rearrange_w_in(w, dm):
    lead = w.shape[:-1]
    pieces = {}
    for nm in ('a', 'b', 'ckv', 'cz'):
        s, wd = dm.src[nm]
        pieces[nm] = w[..., s:s + wd]
    s, wd = dm.src['q']
    q = w[..., s:s + wd].reshape(lead + (dm.ch, HEAD + ROPE))
    pieces['q'] = jnp.pad(q, [(0, 0)] * (q.ndim - 1) + [(0, 2 * HEAD - HEAD - ROPE)]).reshape(lead + (dm.qw,))
    s, wd = dm.src['kr']
    pieces['kr'] = jnp.pad(w[..., s:s + wd], [(0, 0)] * len(lead) + [(0, HEAD - ROPE)])
    out, cur = [], 0
    for nm in sorted(dm.off, key=lambda k: dm.off[k]):
        if dm.off[nm] > cur:
            out.append(jnp.zeros(lead + (dm.off[nm] - cur,), w.dtype))
        out.append(pieces[nm])
        cur = dm.off[nm] + pieces[nm].shape[-1]
    if dm.inp > cur:
        out.append(jnp.zeros(lead + (dm.inp - cur,), w.dtype))
    return jnp.concatenate(out, axis=-1)


def _unarrange_w_in(g, dm):
    lead = g.shape[:-1]

    def seg(nm, width):
        return g[..., dm.off[nm]:dm.off[nm] + width]

    q = seg('q', dm.qw).reshape(lead + (dm.ch, 2 * HEAD))[..., :HEAD + ROPE].reshape(lead + (dm.ch * (HEAD + ROPE),))
    return jnp.concatenate([seg('a', 3 * dm.aw), seg('b', 4 * dm.bw), q, seg('ckv', dm.kvr), seg('kr', ROPE),
                            seg('cz', dm.cw)], axis=-1)


def _assemble_dproj(parts, dm, t):
    out, cur = [], 0
    for nm in sorted(dm.off, key=lambda k: dm.off[k]):
        if dm.off[nm] > cur:
            out.append(jnp.zeros((t, dm.off[nm] - cur), BF16))
        out.append(parts[nm])
        cur = dm.off[nm] + parts[nm].shape[-1]
    if dm.inp > cur:
        out.append(jnp.zeros((t, dm.inp - cur), BF16))
    return jnp.concatenate(out, axis=-1)


def _rope_tables(positions):
    inv = 1.0 / (ROPE_BASE ** (jnp.arange(0, ROPE, 2, dtype=F32) / ROPE))
    ang = positions.astype(F32)[:, None] * inv
    cos, sin = jnp.cos(ang), jnp.sin(ang)
    t = positions.shape[0]
    half = ROPE // 2
    cos_t = jnp.concatenate([cos, cos, jnp.zeros((t, HEAD - ROPE), F32)], axis=-1)
    sin_a = jnp.concatenate([-sin, jnp.zeros((t, HEAD - half), F32)], axis=-1)
    sin_b = jnp.concatenate([jnp.zeros((t, half), F32), sin, jnp.zeros((t, HEAD - ROPE), F32)], axis=-1)
    return cos_t, sin_a, sin_b


def _pad_gain(g):
    return jnp.pad(g, (0, HEAD - g.shape[0]))[None, :]


def _shard_major(g, axis):
    shape = g.shape
    g = g.reshape(shape[:axis] + (N_SHARD, shape[axis] // N_SHARD) + shape[axis + 1:])
    g = jnp.moveaxis(g, axis, 0)
    cols = g.shape[-1]
    g = g.reshape(N_SHARD, 2, -1, cols)
    return jnp.swapaxes(g, 0, 1)


def _pack(arrs):
    flat = jnp.concatenate([a.reshape(-1) for a in arrs])
    pad = (-flat.shape[0]) % (8 * HEAD)
    return jnp.pad(flat, (0, pad)).reshape(-1, HEAD)


def _unpack(buf, shapes):
    flat = buf.reshape(-1)
    out, cur = [], 0
    for s in shapes:
        size = 1
        for v in s:
            size *= v
        out.append(flat[cur:cur + size].reshape(s))
        cur += size
    return out


def kernel(x, p, positions, attn_norm, w_in, sgu_norm, w_spatial, b_spatial, conv_w, conv_b, kv_norm, w_ukv, q_nope_norm, q_rope_norm, k_nope_norm, k_rope_norm, out_norm, w_out, ple_norm, w_ple_gate, w_ple_proj, loss_target, m_attn_norm, m_w_in, m_sgu_norm, m_w_spatial, m_b_spatial, m_conv_w, m_conv_b, m_kv_norm, m_w_ukv, m_q_nope_norm, m_q_rope_norm, m_k_nope_norm, m_k_rope_norm, m_out_norm, m_w_out, m_ple_norm, m_w_ple_gate, m_w_ple_proj, v_attn_norm, v_w_in, v_sgu_norm, v_w_spatial, v_b_spatial, v_conv_w, v_conv_b, v_kv_norm, v_w_ukv, v_q_nope_norm, v_q_rope_norm, v_k_nope_norm, v_k_rope_norm, v_out_norm, v_w_out, v_ple_norm, v_w_ple_gate, v_w_ple_proj):
    weights = dict(attn_norm=attn_norm, w_in=w_in, sgu_norm=sgu_norm, w_spatial=w_spatial, b_spatial=b_spatial,
                   conv_w=conv_w, conv_b=conv_b, kv_norm=kv_norm, w_ukv=w_ukv, q_nope_norm=q_nope_norm,
                   q_rope_norm=q_rope_norm, k_nope_norm=k_nope_norm, k_rope_norm=k_rope_norm, out_norm=out_norm,
                   w_out=w_out, ple_norm=ple_norm, w_ple_gate=w_ple_gate, w_ple_proj=w_ple_proj)
    mom_m = dict(attn_norm=m_attn_norm, w_in=m_w_in, sgu_norm=m_sgu_norm, w_spatial=m_w_spatial,
                 b_spatial=m_b_spatial, conv_w=m_conv_w, conv_b=m_conv_b, kv_norm=m_kv_norm, w_ukv=m_w_ukv,
                 q_nope_norm=m_q_nope_norm, q_rope_norm=m_q_rope_norm, k_nope_norm=m_k_nope_norm,
                 k_rope_norm=m_k_rope_norm, out_norm=m_out_norm, w_out=m_w_out, ple_norm=m_ple_norm,
                 w_ple_gate=m_w_ple_gate, w_ple_proj=m_w_ple_proj)
    mom_v = dict(attn_norm=v_attn_norm, w_in=v_w_in, sgu_norm=v_sgu_norm, w_spatial=v_w_spatial,
                 b_spatial=v_b_spatial, conv_w=v_conv_w, conv_b=v_conv_b, kv_norm=v_kv_norm, w_ukv=v_w_ukv,
                 q_nope_norm=v_q_nope_norm, q_rope_norm=v_q_rope_norm, k_nope_norm=v_k_nope_norm,
                 k_rope_norm=v_k_rope_norm, out_norm=v_out_norm, w_out=v_w_out, ple_norm=v_ple_norm,
                 w_ple_gate=v_w_ple_gate, w_ple_proj=v_w_ple_proj)
    dm = _Dims(x, p, w_in, sgu_norm, conv_w, kv_norm, w_ukv, w_out)
    t, d, depth = dm.t, dm.d, dm.depth
    shard = 2 * lax.axis_index("x") + lax.axis_index("y")
    core = lax.axis_index("c")
    scale = float(HEAD + ROPE) ** -0.5

    local = [weights[n].astype(BF16) for n in BIG] + [conv_w]
    gathered = [lax.dynamic_update_slice(g, mine[None], (shard,) + (0,) * mine.ndim)
                for g, mine in zip(_gather_shards(local, "gather_weights"), local)]
    full = {n: jnp.concatenate([gathered[i][s] for s in range(N_SHARD)], axis=BIG_AXIS[n])
            for i, n in enumerate(BIG)}
    conv_w_full = jnp.concatenate([gathered[len(BIG)][s] for s in range(N_SHARD)], axis=2)
    w_in_r = _rearrange_w_in(full['w_in'], dm)

    tabs = _rope_tables(positions[0])
    h = x[0]
    saved = []
    for i in range(depth):
        tag = f"l{i}_"
        ga, gb, gc = (out_norm[i][None, :dm.aw], out_norm[i][None, dm.aw:dm.aw + dm.bw],
                      out_norm[i][None, dm.aw + dm.bw:])
        ws_b = w_spatial[i].astype(BF16)
        bb = jnp.broadcast_to(b_spatial[i][:, :, None], (dm.ah, HEAD, HEAD))
        qn_g, qr_g = q_nope_norm[i][None, :], _pad_gain(q_rope_norm[i])
        kn_g, kr_g = k_nope_norm[i][None, :], _pad_gain(k_rope_norm[i])
        kv_g = kv_norm[i][None, :]
        hn = _norm_fwd(h, attn_norm[i][None, :], tag + "norm1")
        proj = _matmul(hn, w_in_r[i], 'nn', BF16, tag + "proj")
        ya = _sgu_fwd(proj, dm.off['a'], dm.aw, sgu_norm[i], ws_b, bb, ga, tag + "sgu")
        yb, yconv = _conv_fwd(proj, dm.off['b'], dm.bw, conv_w_full[i], conv_b[i][None, :], gb, tag + "conv")
        q_cat, ckv_n, kr_rot = _mla_prep_fwd(proj, dm.off['q'], dm.off['ckv'], dm.off['kr'], dm.ch, dm.kvr, tabs,
                                             qn_g, qr_g, kr_g, kv_g, tag + "mla_prep")
        kv = _matmul(ckv_n, full['w_ukv'][i], 'nn', F32, tag + "kv_up")
        k_cat, v_aug = _kv_prep_fwd(kv, kr_rot, dm.ch, kn_g, tag + "kv_prep")
        o, lse = _attn_fwd(q_cat, k_cat, v_aug, dm.ch, scale, tag + "attn")
        yc = _attn_post_fwd(o, proj, dm.off['cz'], dm.cw, gc, tag + "attn_post")
        y = jnp.concatenate([ya, yb, yc], axis=-1)
        h1 = _matmul(y, full['w_out'][i], 'nn', F32, tag + "out", add=h)
        hn2 = _norm_fwd(h1, ple_norm[i][None, :], tag + "norm2")
        gpre = _matmul(hn2, full['w_ple_gate'][i], 'nn', F32, tag + "gate")
        p_b = p[i, 0].astype(BF16)
        pp = _matmul(p_b, full['w_ple_proj'][i], 'nn', F32, tag + "ple_proj")
        h2 = _ple_fwd(h1, gpre, pp, tag + "ple")
        saved.append(dict(h=h, hn=hn, proj=proj, yconv=yconv, q_cat=q_cat, ckv_n=ckv_n, kv=kv, k_cat=k_cat,
                          v=v_aug, o=o, lse=lse, y=y, h1=h1, hn2=hn2, gpre=gpre, pp=pp, p_b=p_b, ws_b=ws_b, bb=bb,
                          gains=(ga, gb, gc, qn_g, qr_g, kn_g, kr_g, kv_g)))
        h = h2

    loss_part, dh = _loss_and_grad(h, loss_target[0], "loss")
    loss = lax.psum(loss_part[0, 0], ("x", "y", "c"))

    grads = {n: [None] * depth for n in WEIGHTS}
    for i in reversed(range(depth)):
        tag = f"l{i}_b_"
        sv = saved[i]
        ga, gb, gc, qn_g, qr_g, kn_g, kr_g, kv_g = sv['gains']
        proj = sv['proj']
        dgpre, dpp = _ple_bwd(sv['gpre'], sv['pp'], dh, tag + "ple")
        grads['w_ple_proj'][i] = _matmul(sv['p_b'], dpp, 'tn', F32, tag + "d_w_ple_proj")
        grads['w_ple_gate'][i] = _matmul(sv['hn2'], dgpre, 'tn', F32, tag + "d_w_gate")
        d_hn2 = _matmul(dgpre, full['w_ple_gate'][i], 'nt', BF16, tag + "d_hn2")
        dh1, dh1_b, g_ple = _norm_bwd(sv['h1'], ple_norm[i][None, :], d_hn2, dh, tag + "norm2")
        grads['ple_norm'][i] = g_ple[0]
        grads['w_out'][i] = _matmul(sv['y'], dh1_b, 'tn', F32, tag + "d_w_out")
        dy = _matmul(dh1_b, full['w_out'][i], 'nt', BF16, tag + "d_y")
        ws_t = jnp.swapaxes(sv['ws_b'], 1, 2)
        d_a, g_sgu, g_ws, g_bs, g_ga = _sgu_bwd(proj, dm.off['a'], dm.aw, sgu_norm[i], sv['ws_b'], ws_t, sv['bb'],
                                                ga, dy, tag + "sgu")
        grads['sgu_norm'][i], grads['w_spatial'][i], grads['b_spatial'][i] = g_sgu, g_ws, g_bs[:, :, 0]
        dyc, d_bb, d_bz, g_gb, g_cb = _conv_bwd_gate(proj, dm.off['b'], dm.bw, sv['yconv'], gb, dy, tag + "conv_gate")
        d_bc, d_bh, g_cw = _conv_bwd_taps(proj, dm.off['b'], dm.bw, dyc, conv_w_full[i], tag + "conv_taps")
        grads['conv_b'][i], grads['conv_w'][i] = g_cb[0], g_cw
        d_b = jnp.concatenate([d_bb, d_bc, d_bh, d_bz], axis=-1)
        d_o, d_cz, dsum, g_gc = _attn_post_bwd(sv['o'], proj, dm.off['cz'], dm.cw, gc, dy,
                                               _col_block(dm.aw + dm.bw, dm.cw), tag + "attn_post")
        grads['out_norm'][i] = jnp.concatenate([g_ga[0], g_gb[0], g_gc[0]])
        dq_t, dk_cat, dv = _attn_bwd(sv['q_cat'], sv['k_cat'], sv['k_cat'].T, sv['v'], d_o,
                                     sv['lse'].reshape(dm.ch, 1, t), dsum.reshape(dm.ch, 1, t), dm.ch, scale,
                                     tag + "attn_bwd")
        dq_cat = dq_t.T.astype(BF16)
        dkv, dkr_rot, g_kn = _kv_prep_bwd(sv['kv'], dm.ch, kn_g, dk_cat, dv, tag + "kv_prep")
        grads['k_nope_norm'][i] = g_kn[0]
        grads['w_ukv'][i] = _matmul(sv['ckv_n'], dkv, 'tn', F32, tag + "d_w_ukv")
        dckv_n = _matmul(dkv, full['w_ukv'][i], 'nt', BF16, tag + "d_ckv")
        d_q, d_ckv, d_kr, g_qn, g_qr, g_kr, g_kv = _mla_prep_bwd(
            proj, dm.off['q'], dm.off['ckv'], dm.off['kr'], dm.ch, dm.kvr, tabs, qn_g, qr_g, kr_g, kv_g,
            dq_cat, scale, dckv_n, dkr_rot, tag + "mla_prep")
        grads['q_nope_norm'][i], grads['q_rope_norm'][i] = g_qn[0], g_qr[0, :ROPE]
        grads['k_rope_norm'][i], grads['kv_norm'][i] = g_kr[0, :ROPE], g_kv[0]
        dproj = _assemble_dproj(dict(a=d_a, b=d_b, ckv=d_ckv, q=d_q, cz=d_cz, kr=d_kr), dm, t)
        grads['w_in'][i] = _unarrange_w_in(_matmul(sv['hn'], dproj, 'tn', F32, tag + "d_w_in"), dm)
        d_hn = _matmul(dproj, w_in_r[i], 'nt', BF16, tag + "d_hn")
        dh, _, g_an = _norm_bwd(sv['h'], attn_norm[i][None, :], d_hn, dh1, tag + "norm1")
        grads['attn_norm'][i] = g_an[0]
    grad_x = dh[None]
    grads = {n: jnp.stack(grads[n]) for n in WEIGHTS}

    sm = [_shard_major(grads[n], BIG_AXIS[n]) for n in BIG]
    from_sibling = _to_sibling(sm, True, "grads_to_sibling")
    mine = [lax.dynamic_index_in_dim(g, core, 0, keepdims=False) for g in sm]
    chip_sums = [_pair_sum_bf16(a, b, f"chip_sum_{n}") for a, b, n in zip(mine, from_sibling, BIG)]
    from_chips = _to_owner_chips(chip_sums, "grads_to_owner_chips")
    halves = []
    for a, b, r3, n in zip(mine, from_sibling, from_chips, BIG):
        own_a = lax.dynamic_index_in_dim(a, shard, 0, keepdims=False)
        own_b = lax.dynamic_index_in_dim(b, shard, 0, keepdims=False)
        halves.append(_shard_sum(own_a, own_b, r3, f"shard_sum_{n}"))
    sibling_halves = _to_sibling(halves, False, "grads_share_sibling")
    core_flag = core.astype(F32).reshape(1, 1)
    out_g, out_d, out_m, out_v = {}, {}, {}, {}
    for n, own, recv in zip(BIG, halves, sibling_halves):
        shp = weights[n].shape
        cols = shp[-1]
        g_sum, d_w, m_new, v_new = _adamw_two_halves(
            weights[n].reshape(-1, cols), own, recv, core_flag, mom_m[n].reshape(-1, cols),
            mom_v[n].reshape(-1, cols), f"adamw_{n}")
        out_g[n], out_d[n], out_m[n], out_v[n] = (g_sum.reshape(shp), d_w.reshape(shp), m_new.reshape(shp),
                                                  v_new.reshape(shp))

    shapes = [grads[n].shape for n in SMALL]
    summed = _unpack(_sum_devices(_gather_devices(_pack([grads[n] for n in SMALL]), "gather_small_grads"),
                                  "sum_small_grads"), shapes)
    small_g = dict(zip(SMALL, summed))
    small_g['conv_w'] = lax.dynamic_slice_in_dim(small_g['conv_w'], shard * conv_w.shape[2], conv_w.shape[2], axis=2)
    local_shapes = [weights[n].shape for n in SMALL]
    d_s, m_s, v_s = _adamw(_pack([weights[n] for n in SMALL]), _pack([small_g[n] for n in SMALL]),
                           _pack([mom_m[n] for n in SMALL]), _pack([mom_v[n] for n in SMALL]), "adamw_small")
    for n, dd, mm, vv in zip(SMALL, _unpack(d_s, local_shapes), _unpack(m_s, local_shapes),
                             _unpack(v_s, local_shapes)):
        out_g[n], out_d[n], out_m[n], out_v[n] = small_g[n], dd, mm, vv

    return (loss, grad_x, *[out_g[n] for n in WEIGHTS], *[out_d[n] for n in WEIGHTS],
            *[out_m[n] for n in WEIGHTS], *[out_v[n] for n in WEIGHTS])
```

```python
import functools

import jax
import jax.numpy as jnp
from jax import lax
from jax.experimental import pallas as pl
from jax.experimental.pallas import tpu as pltpu

F32 = jnp.float32
BF16 = jnp.bfloat16
EPS = 1e-6
HEAD = 128
ROPE = 64
ROPE_BASE = 10000.0
CONV_TAPS = 3
N_SHARD = 4
N_DEV = 8
ADAM_LR = 0.001
ADAM_B1 = 0.9
ADAM_B2 = 0.999
ADAM_EPS = 1e-08
ADAM_WD = 0.01
ADAM_STEP = 10
MESH = pl.DeviceIdType.MESH
VMEM_LIMIT = 56 * 1024 * 1024
HALO_ROWS = 16

WEIGHTS = ['attn_norm', 'w_in', 'sgu_norm', 'w_spatial', 'b_spatial', 'conv_w', 'conv_b', 'kv_norm', 'w_ukv',
           'q_nope_norm', 'q_rope_norm', 'k_nope_norm', 'k_rope_norm', 'out_norm', 'w_out', 'ple_norm',
           'w_ple_gate', 'w_ple_proj']
BIG = ['w_in', 'w_ukv', 'w_out', 'w_ple_gate', 'w_ple_proj']
BIG_AXIS = {'w_in': 2, 'w_ukv': 2, 'w_out': 1, 'w_ple_gate': 1, 'w_ple_proj': 2}
SMALL = [n for n in WEIGHTS if n not in BIG]


def _pick(n, cands):
    for c in cands:
        if n % c == 0:
            return c
    return n


def _params(sem=None):
    return pltpu.CompilerParams(dimension_semantics=sem, vmem_limit_bytes=VMEM_LIMIT)


def _matmul(a, b, mode, out_dtype, name, add=None, b_layer=None):
    b_shape = b.shape if b_layer is None else b.shape[1:]
    if mode == 'nn':
        (m, k), n = a.shape, b_shape[1]
    elif mode == 'nt':
        (m, k), n = a.shape, b_shape[0]
    else:
        (k, m), n = a.shape, b_shape[1]
    tm = _pick(m, (1024, 512, 256, 128))
    tn = _pick(n, (1536, 1024, 512, 256, 128))
    tk = k if k <= 2048 else _pick(k, (2048, 1536, 1024, 512, 256, 128))
    nk = k // tk
    if mode == 'tn':
        a_spec = pl.BlockSpec((tk, tm), lambda i, j, kk: (kk, i))
        dims = (((0,), (0,)), ((), ()))
    else:
        a_spec = pl.BlockSpec((tm, tk), lambda i, j, kk: (i, kk))
        dims = (((1,), (0,)), ((), ())) if mode == 'nn' else (((1,), (1,)), ((), ()))
    b_block = (tn, tk) if mode == 'nt' else (tk, tn)
    if b_layer is None:
        b_spec = pl.BlockSpec(b_block, (lambda i, j, kk: (j, kk)) if mode == 'nt' else (lambda i, j, kk: (kk, j)))
    else:
        b_spec = pl.BlockSpec((None,) + b_block, (lambda i, j, kk: (b_layer, j, kk)) if mode == 'nt'
                              else (lambda i, j, kk: (b_layer, kk, j)))
    o_spec = pl.BlockSpec((tm, tn), lambda i, j, kk: (i, j))
    has_add = add is not None

    def body(*refs):
        a_ref, b_ref = refs[0], refs[1]
        add_ref = refs[2] if has_add else None
        o_ref = refs[3] if has_add else refs[2]

        def product():
            return lax.dot_general(a_ref[...], b_ref[...], dims, preferred_element_type=F32)

        def finish(res):
            if has_add:
                res = res + add_ref[...]
            o_ref[...] = res.astype(out_dtype)

        if nk == 1:
            finish(product())
        else:
            acc_ref = refs[-1]
            kk = pl.program_id(2)

            @pl.when(kk == 0)
            def _():
                acc_ref[...] = product()

            @pl.when((kk > 0) & (kk < nk - 1))
            def _():
                acc_ref[...] += product()

            @pl.when(kk == nk - 1)
            def _():
                finish(acc_ref[...] + product())

    in_specs = [a_spec, b_spec] + ([o_spec] if has_add else [])
    args = [a, b] + ([add] if has_add else [])
    return pl.pallas_call(
        body, name=name, grid=(m // tm, n // tn, nk), in_specs=in_specs, out_specs=o_spec,
        out_shape=jax.ShapeDtypeStruct((m, n), out_dtype),
        scratch_shapes=[pltpu.VMEM((tm, tn), F32)] if nk > 1 else [],
        compiler_params=_params(("parallel", "parallel", "arbitrary")),
    )(*args)


def _rms(x, n):
    r = lax.rsqrt(jnp.sum(x * x, axis=-1, keepdims=True) * (1.0 / n) + EPS)
    return x * r, r


def _rms_bwd(dxhat, xhat, r, n):
    return r * (dxhat - xhat * (jnp.sum(dxhat * xhat, axis=-1, keepdims=True) * (1.0 / n)))


def _sigmoid(z):
    return 1.0 / (1.0 + jnp.exp(-z))


def _silu_and_grad(z):
    sig = _sigmoid(z)
    return z * sig, sig * (1.0 + z * (1.0 - sig))


def _colsum(x):
    return jnp.sum(x, axis=0, keepdims=True)


def _rope(t, cos_t, sin_a, sin_b):
    return t * cos_t + pltpu.roll(t, 96, 1) * sin_a + pltpu.roll(t, 32, 1) * sin_b


def _rope_bwd(d, cos_t, sin_a, sin_b):
    return d * cos_t + pltpu.roll(d * sin_a, 32, 1) + pltpu.roll(d * sin_b, 96, 1)


def _shift_down(g, first_row):
    row = lax.broadcasted_iota(jnp.int32, g.shape, 0)
    return jnp.where(row == 0, first_row, pltpu.roll(g, 1, 0))


def _shift_up(g, last_row):
    n = g.shape[0]
    row = lax.broadcasted_iota(jnp.int32, g.shape, 0)
    return jnp.where(row == n - 1, last_row, pltpu.roll(g, n - 1, 0))


def _row_spec(r, w, col=0):
    return pl.BlockSpec((r, w), lambda i: (i, col))


def _const_spec(shape):
    nd = len(shape)
    return pl.BlockSpec(shape, lambda i: (0,) * nd)


def _col_block(off, w):
    assert off % w == 0, (off, w)
    return off // w


def _zero_at_first_step(refs):
    @pl.when(pl.program_id(0) == 0)
    def _():
        for ref in refs:
            ref[...] = jnp.zeros(ref.shape, ref.dtype)


def _row_call(body, name, t, r, in_specs, args, out_specs, out_shapes, scratch=(), into=None):
    if into is None:
        return pl.pallas_call(
            body, name=name, grid=(t // r,), in_specs=in_specs, out_specs=out_specs, out_shape=out_shapes,
            scratch_shapes=list(scratch), compiler_params=_params(("arbitrary",)),
        )(*args)
    buf, col = into
    single = not isinstance(out_specs, (list, tuple))
    specs = [out_specs] if single else list(out_specs)
    shapes = [out_shapes] if single else list(out_shapes)
    width = shapes[0].shape[1]
    assert shapes[0].dtype == buf.dtype and buf.shape[0] == t
    specs[0] = _row_spec(r, width, col)
    shapes[0] = jax.ShapeDtypeStruct(buf.shape, buf.dtype)
    n_in = len(args)

    def body_in_place(*refs):
        body(*refs[:n_in], *refs[n_in + 1:])

    res = pl.pallas_call(
        body_in_place, name=name, grid=(t // r,), in_specs=list(in_specs) + [_ANY], out_specs=specs, out_shape=shapes,
        scratch_shapes=list(scratch), input_output_aliases={n_in: 0}, compiler_params=_params(("arbitrary",)),
    )(*args, buf)
    return res[0] if single else res


def _norm_fwd(h, g, name):
    t, d = h.shape
    r = _pick(t, (256, 128))

    def body(h_ref, g_ref, o_ref):
        xhat, _ = _rms(h_ref[...], d)
        o_ref[...] = (xhat * g_ref[...]).astype(BF16)

    return _row_call(body, name, t, r, [_row_spec(r, d), _const_spec((1, d))], (h, g),
                     _row_spec(r, d), jax.ShapeDtypeStruct((t, d), BF16))


def _norm_bwd(h, g, d_hn, d_res, name):
    t, d = h.shape
    r = _pick(t, (256, 128))

    def body(h_ref, g_ref, dy_ref, dres_ref, dh_ref, dhb_ref, dg_ref):
        _zero_at_first_step([dg_ref])
        xhat, rr = _rms(h_ref[...], d)
        dy = dy_ref[...].astype(F32)
        dg_ref[...] += _colsum(dy * xhat)
        dh = dres_ref[...] + _rms_bwd(dy * g_ref[...], xhat, rr, d)
        dh_ref[...] = dh
        dhb_ref[...] = dh.astype(BF16)

    return _row_call(body, name, t, r,
                     [_row_spec(r, d), _const_spec((1, d)), _row_spec(r, d), _row_spec(r, d)], (h, g, d_hn, d_res),
                     [_row_spec(r, d), _row_spec(r, d), _const_spec((1, d))],
                     [jax.ShapeDtypeStruct((t, d), F32), jax.ShapeDtypeStruct((t, d), BF16),
                      jax.ShapeDtypeStruct((1, d), F32)])


def _sgu_scores(v, gs_ref, ws_ref, bb_ref, s_scr, r, ah, keep=None):
    for kk in range(r // HEAD):
        for hh in range(ah):
            rows, cols = slice(kk * HEAD, (kk + 1) * HEAD), slice(hh * HEAD, (hh + 1) * HEAD)
            vhat, rv = _rms(v[rows, cols], HEAD)
            vn = vhat * gs_ref[pl.ds(hh, 1), :]
            s_scr[rows, cols] = jnp.dot(ws_ref[hh], vn.astype(BF16), preferred_element_type=F32) + bb_ref[hh]
            if keep is not None:
                keep[(kk, hh)] = (vhat, rv, vn)


def _sgu_fwd(proj, off, aw, gs, ws, bb, ga, name, into=None):
    t = proj.shape[0]
    ah = aw // HEAD
    r = _pick(t, (256, 128))
    cb = _col_block(off, aw)

    def body(u_ref, v_ref, z_ref, gs_ref, ws_ref, bb_ref, ga_ref, o_ref, s_scr):
        _sgu_scores(v_ref[...].astype(F32), gs_ref, ws_ref, bb_ref, s_scr, r, ah)
        sil, _ = _silu_and_grad(z_ref[...].astype(F32))
        yhat, _ = _rms(u_ref[...].astype(F32) * s_scr[...] * sil, aw)
        o_ref[...] = (yhat * ga_ref[...]).astype(BF16)

    return _row_call(
        body, name, t, r,
        [_row_spec(r, aw, cb), _row_spec(r, aw, cb + 1), _row_spec(r, aw, cb + 2), _const_spec((ah, HEAD)),
         _const_spec((ah, HEAD, HEAD)), _const_spec((ah, HEAD, HEAD)), _const_spec((1, aw))],
        (proj, proj, proj, gs, ws, bb, ga),
        _row_spec(r, aw), jax.ShapeDtypeStruct((t, aw), BF16), scratch=[pltpu.VMEM((r, aw), F32)], into=into)


def _sgu_bwd(proj, off, aw, gs, ws, ws_t, bb, ga, dy, name):
    t = proj.shape[0]
    ah = aw // HEAD
    r = _pick(t, (256, 128))
    cb = _col_block(off, aw)

    def body(u_ref, v_ref, z_ref, gs_ref, ws_ref, wst_ref, bb_ref, ga_ref, dy_ref,
             d_ref, dgs_ref, dws_ref, db_ref, dga_ref, s_scr, dv_scr):
        _zero_at_first_step([dgs_ref, dws_ref, db_ref, dga_ref])
        keep = {}
        _sgu_scores(v_ref[...].astype(F32), gs_ref, ws_ref, bb_ref, s_scr, r, ah, keep)
        u, z, s = u_ref[...].astype(F32), z_ref[...].astype(F32), s_scr[...]
        sil, dsil = _silu_and_grad(z)
        yhat, rr = _rms(u * s * sil, aw)
        dy_f = dy_ref[...].astype(F32)
        dga_ref[...] += _colsum(dy_f * yhat)
        dya = _rms_bwd(dy_f * ga_ref[...], yhat, rr, aw)
        d_ref[:, 0:aw] = (dya * s * sil).astype(BF16)
        d_ref[:, 2 * aw:3 * aw] = (dya * u * s * dsil).astype(BF16)
        ds = dya * u * sil
        for kk in range(r // HEAD):
            for hh in range(ah):
                rows, cols = slice(kk * HEAD, (kk + 1) * HEAD), slice(hh * HEAD, (hh + 1) * HEAD)
                vhat, rv, vn = keep[(kk, hh)]
                ds_blk = ds[rows, cols]
                db_ref[hh] += jnp.sum(ds_blk, axis=1, keepdims=True)
                ds_b = ds_blk.astype(BF16)
                dws_ref[hh] += lax.dot_general(ds_b, vn.astype(BF16), (((1,), (1,)), ((), ())),
                                               preferred_element_type=F32)
                dvn = jnp.dot(wst_ref[hh], ds_b, preferred_element_type=F32)
                dgs_ref[pl.ds(hh, 1), :] += _colsum(dvn * vhat)
                dv_scr[rows, cols] = _rms_bwd(dvn * gs_ref[pl.ds(hh, 1), :], vhat, rv, HEAD)
        d_ref[:, aw:2 * aw] = dv_scr[...].astype(BF16)

    return _row_call(
        body, name, t, r,
        [_row_spec(r, aw, cb), _row_spec(r, aw, cb + 1), _row_spec(r, aw, cb + 2), _const_spec((ah, HEAD)),
         _const_spec((ah, HEAD, HEAD)), _const_spec((ah, HEAD, HEAD)), _const_spec((ah, HEAD, HEAD)),
         _const_spec((1, aw)), _row_spec(r, aw, 0)],
        (proj, proj, proj, gs, ws, ws_t, bb, ga, dy),
        [_row_spec(r, 3 * aw), _const_spec((ah, HEAD)), _const_spec((ah, HEAD, HEAD)), _const_spec((ah, HEAD, 1)),
         _const_spec((1, aw))],
        [jax.ShapeDtypeStruct((t, 3 * aw), BF16), jax.ShapeDtypeStruct((ah, HEAD), F32),
         jax.ShapeDtypeStruct((ah, HEAD, HEAD), F32), jax.ShapeDtypeStruct((ah, HEAD, 1), F32),
         jax.ShapeDtypeStruct((1, aw), F32)],
        scratch=[pltpu.VMEM((r, aw), F32), pltpu.VMEM((r, aw), F32)])


def _halo_specs(t, r, w, col, rows):
    per = r // rows
    last = t // rows - 1
    prev = pl.BlockSpec((rows, w), lambda i: (jnp.maximum(i * per - 1, 0), col))
    nxt = pl.BlockSpec((rows, w), lambda i: (jnp.minimum((i + 1) * per, last), col))
    return prev, nxt


def _edge_rows(prev_ref, next_ref, n_steps):
    i = pl.program_id(0)
    rows = prev_ref.shape[0]
    before = prev_ref[...].astype(F32)[rows - 1:rows, :] * (i > 0).astype(F32)
    after = next_ref[...].astype(F32)[0:1, :] * (i < n_steps - 1).astype(F32)
    return before, after


def _conv_fwd(proj, off, bw, cw, cb_, gb, name, into=None):
    t = proj.shape[0]
    r = _pick(t, (256, 128))
    n_steps = t // r
    c0 = _col_block(off, bw)
    cp, cn = _halo_specs(t, r, bw, c0 + 1, HALO_ROWS)
    hp, hn = _halo_specs(t, r, bw, c0 + 2, HALO_ROWS)

    def body(b_ref, c_ref, h_ref, z_ref, cp_ref, cn_ref, hp_ref, hn_ref, cw_ref, cb_ref, gb_ref, o_ref, yc_ref):
        g = c_ref[...].astype(F32) * h_ref[...].astype(F32)
        c_before, c_after = _edge_rows(cp_ref, cn_ref, n_steps)
        h_before, h_after = _edge_rows(hp_ref, hn_ref, n_steps)
        yconv = (cb_ref[...] + cw_ref[0:1, :] * _shift_down(g, c_before * h_before) + cw_ref[1:2, :] * g
                 + cw_ref[2:3, :] * _shift_up(g, c_after * h_after))
        yc_ref[...] = yconv
        sil, _ = _silu_and_grad(z_ref[...].astype(F32))
        yhat, _ = _rms(b_ref[...].astype(F32) * yconv * sil, bw)
        o_ref[...] = (yhat * gb_ref[...]).astype(BF16)

    return _row_call(
        body, name, t, r,
        [_row_spec(r, bw, c0), _row_spec(r, bw, c0 + 1), _row_spec(r, bw, c0 + 2), _row_spec(r, bw, c0 + 3),
         cp, cn, hp, hn, _const_spec((CONV_TAPS, bw)), _const_spec((1, bw)), _const_spec((1, bw))],
        (proj, proj, proj, proj, proj, proj, proj, proj, cw, cb_, gb),
        [_row_spec(r, bw), _row_spec(r, bw)],
        [jax.ShapeDtypeStruct((t, bw), BF16), jax.ShapeDtypeStruct((t, bw), F32)], into=into)


def _conv_bwd_gate(proj, off, bw, yconv, gb, dy, name):
    t = proj.shape[0]
    r = _pick(t, (256, 128))
    c0 = _col_block(off, bw)

    def body(b_ref, z_ref, yc_ref, gb_ref, dy_ref, dyc_ref, db_ref, dz_ref, dgb_ref, dcb_ref):
        _zero_at_first_step([dgb_ref, dcb_ref])
        b, z, yconv_v = b_ref[...].astype(F32), z_ref[...].astype(F32), yc_ref[...]
        sil, dsil = _silu_and_grad(z)
        yhat, rr = _rms(b * yconv_v * sil, bw)
        dy_f = dy_ref[...].astype(F32)
        dgb_ref[...] += _colsum(dy_f * yhat)
        dyb = _rms_bwd(dy_f * gb_ref[...], yhat, rr, bw)
        dyc = dyb * b * sil
        dyc_ref[...] = dyc
        dcb_ref[...] += _colsum(dyc)
        db_ref[...] = (dyb * yconv_v * sil).astype(BF16)
        dz_ref[...] = (dyb * b * yconv_v * dsil).astype(BF16)

    return _row_call(
        body, name, t, r,
        [_row_spec(r, bw, c0), _row_spec(r, bw, c0 + 3), _row_spec(r, bw), _const_spec((1, bw)), _row_spec(r, bw, 1)],
        (proj, proj, yconv, gb, dy),
        [_row_spec(r, bw), _row_spec(r, bw), _row_spec(r, bw), _const_spec((1, bw)), _const_spec((1, bw))],
        [jax.ShapeDtypeStruct((t, bw), F32), jax.ShapeDtypeStruct((t, bw), BF16), jax.ShapeDtypeStruct((t, bw), BF16),
         jax.ShapeDtypeStruct((1, bw), F32), jax.ShapeDtypeStruct((1, bw), F32)])


def _conv_bwd_taps(proj, off, bw, dyc, cw, d_gate_b, d_gate_z, name):
    t = proj.shape[0]
    r = _pick(t, (256, 128))
    n_steps = t // r
    c0 = _col_block(off, bw)
    cp, cn = _halo_specs(t, r, bw, c0 + 1, HALO_ROWS)
    hp, hn = _halo_specs(t, r, bw, c0 + 2, HALO_ROWS)
    dp, dn = _halo_specs(t, r, bw, 0, 8)

    def body(c_ref, h_ref, cp_ref, cn_ref, hp_ref, hn_ref, d_ref, dp_ref, dn_ref, cw_ref, dgb_ref, dgz_ref,
             db_ref, dcw_ref):
        _zero_at_first_step([dcw_ref])
        c, h, d = c_ref[...].astype(F32), h_ref[...].astype(F32), d_ref[...]
        g = c * h
        c_before, c_after = _edge_rows(cp_ref, cn_ref, n_steps)
        h_before, h_after = _edge_rows(hp_ref, hn_ref, n_steps)
        d_before, d_after = _edge_rows(dp_ref, dn_ref, n_steps)
        dg = (cw_ref[0:1, :] * _shift_up(d, d_after) + cw_ref[1:2, :] * d + cw_ref[2:3, :] * _shift_down(d, d_before))
        db_ref[:, 0:bw] = dgb_ref[...]
        db_ref[:, bw:2 * bw] = (dg * h).astype(BF16)
        db_ref[:, 2 * bw:3 * bw] = (dg * c).astype(BF16)
        db_ref[:, 3 * bw:4 * bw] = dgz_ref[...]
        dcw_ref[0:1, :] += _colsum(d * _shift_down(g, c_before * h_before))
        dcw_ref[1:2, :] += _colsum(d * g)
        dcw_ref[2:3, :] += _colsum(d * _shift_up(g, c_after * h_after))

    return _row_call(
        body, name, t, r,
        [_row_spec(r, bw, c0 + 1), _row_spec(r, bw, c0 + 2), cp, cn, hp, hn, _row_spec(r, bw), dp, dn,
         _const_spec((CONV_TAPS, bw)), _row_spec(r, bw), _row_spec(r, bw)],
        (proj, proj, proj, proj, proj, proj, dyc, dyc, dyc, cw, d_gate_b, d_gate_z),
        [_row_spec(r, 4 * bw), _const_spec((CONV_TAPS, bw))],
        [jax.ShapeDtypeStruct((t, 4 * bw), BF16), jax.ShapeDtypeStruct((CONV_TAPS, bw), F32)])


def _mla_prep_fwd(proj, q_off, ckv_off, kr_off, ch, kvr, tabs, qn_g, qr_g, kr_g, kv_g, name):
    t = proj.shape[0]
    r = _pick(t, (256, 128))
    qw = ch * 2 * HEAD
    cos_t, sin_a, sin_b = tabs

    def body(q_ref, ckv_ref, kr_ref, cos_ref, sa_ref, sb_ref, qn_ref, qr_ref, krg_ref, kvg_ref,
             qo_ref, co_ref, ko_ref):
        cos_v, sa, sb = cos_ref[...], sa_ref[...], sb_ref[...]
        for hh in range(ch):
            lo = hh * 2 * HEAD
            nhat, _ = _rms(q_ref[:, lo:lo + HEAD].astype(F32), HEAD)
            qo_ref[:, lo:lo + HEAD] = (nhat * qn_ref[...]).astype(BF16)
            rhat, _ = _rms(q_ref[:, lo + HEAD:lo + 2 * HEAD].astype(F32), ROPE)
            qo_ref[:, lo + HEAD:lo + 2 * HEAD] = _rope(rhat * qr_ref[...], cos_v, sa, sb).astype(BF16)
        khat, _ = _rms(kr_ref[...].astype(F32), ROPE)
        ko_ref[...] = _rope(khat * krg_ref[...], cos_v, sa, sb).astype(BF16)
        chat, _ = _rms(ckv_ref[...].astype(F32), kvr)
        co_ref[...] = (chat * kvg_ref[...]).astype(BF16)

    tab = _row_spec(r, HEAD)
    gain = _const_spec((1, HEAD))
    return _row_call(
        body, name, t, r,
        [_row_spec(r, qw, _col_block(q_off, qw)), _row_spec(r, kvr, _col_block(ckv_off, kvr)),
         _row_spec(r, HEAD, _col_block(kr_off, HEAD)), tab, tab, tab, gain, gain, gain, _const_spec((1, kvr))],
        (proj, proj, proj, cos_t, sin_a, sin_b, qn_g, qr_g, kr_g, kv_g),
        [_row_spec(r, qw), _row_spec(r, kvr), _row_spec(r, HEAD)],
        [jax.ShapeDtypeStruct((t, qw), BF16), jax.ShapeDtypeStruct((t, kvr), BF16),
         jax.ShapeDtypeStruct((t, HEAD), BF16)])


def _mla_prep_bwd(proj, q_off, ckv_off, kr_off, ch, kvr, tabs, qn_g, qr_g, kr_g, kv_g, dq_cat, dq_scale, dckv_n, dkr_rot,
                  kr_width, name):
    t = proj.shape[0]
    r = _pick(t, (256, 128))
    qw = ch * 2 * HEAD
    cos_t, sin_a, sin_b = tabs

    def body(q_ref, ckv_ref, kr_ref, cos_ref, sa_ref, sb_ref, qn_ref, qr_ref, krg_ref, kvg_ref,
             dq_ref, dc_ref, dk_ref, dqo_ref, dco_ref, dko_ref, dqn_ref, dqr_ref, dkrg_ref, dkvg_ref):
        _zero_at_first_step([dqn_ref, dqr_ref, dkrg_ref, dkvg_ref])
        cos_v, sa, sb = cos_ref[...], sa_ref[...], sb_ref[...]
        for hh in range(ch):
            lo = hh * 2 * HEAD
            nhat, nr = _rms(q_ref[:, lo:lo + HEAD].astype(F32), HEAD)
            d_n = dq_ref[:, lo:lo + HEAD].astype(F32) * dq_scale
            dqn_ref[...] += _colsum(d_n * nhat)
            dqo_ref[:, lo:lo + HEAD] = _rms_bwd(d_n * qn_ref[...], nhat, nr, HEAD).astype(BF16)
            rhat, rr = _rms(q_ref[:, lo + HEAD:lo + 2 * HEAD].astype(F32), ROPE)
            d_t = _rope_bwd(dq_ref[:, lo + HEAD:lo + 2 * HEAD].astype(F32) * dq_scale, cos_v, sa, sb)
            dqr_ref[...] += _colsum(d_t * rhat)
            dqo_ref[:, lo + HEAD:lo + 2 * HEAD] = _rms_bwd(d_t * qr_ref[...], rhat, rr, ROPE).astype(BF16)
        khat, kr_r = _rms(kr_ref[...].astype(F32), ROPE)
        d_k = _rope_bwd(dk_ref[...], cos_v, sa, sb)
        dkrg_ref[...] += _colsum(d_k * khat)
        dko_ref[:, 0:HEAD] = _rms_bwd(d_k * krg_ref[...], khat, kr_r, ROPE).astype(BF16)
        if kr_width > HEAD:
            dko_ref[:, HEAD:kr_width] = jnp.zeros((r, kr_width - HEAD), BF16)
        chat, cr = _rms(ckv_ref[...].astype(F32), kvr)
        d_c = dc_ref[...].astype(F32)
        dkvg_ref[...] += _colsum(d_c * chat)
        dco_ref[...] = _rms_bwd(d_c * kvg_ref[...], chat, cr, kvr).astype(BF16)

    tab = _row_spec(r, HEAD)
    gain = _const_spec((1, HEAD))
    return _row_call(
        body, name, t, r,
        [_row_spec(r, qw, _col_block(q_off, qw)), _row_spec(r, kvr, _col_block(ckv_off, kvr)),
         _row_spec(r, HEAD, _col_block(kr_off, HEAD)), tab, tab, tab, gain, gain, gain, _const_spec((1, kvr)),
         _row_spec(r, qw), _row_spec(r, kvr), _row_spec(r, HEAD)],
        (proj, proj, proj, cos_t, sin_a, sin_b, qn_g, qr_g, kr_g, kv_g, dq_cat, dckv_n, dkr_rot),
        [_row_spec(r, qw), _row_spec(r, kvr), _row_spec(r, kr_width), gain, gain, gain, _const_spec((1, kvr))],
        [jax.ShapeDtypeStruct((t, qw), BF16), jax.ShapeDtypeStruct((t, kvr), BF16),
         jax.ShapeDtypeStruct((t, kr_width), BF16), jax.ShapeDtypeStruct((1, HEAD), F32),
         jax.ShapeDtypeStruct((1, HEAD), F32), jax.ShapeDtypeStruct((1, HEAD), F32),
         jax.ShapeDtypeStruct((1, kvr), F32)])


def _kv_prep_fwd(kv, kr_rot, ch, kn_g, name):
    t = kv.shape[0]
    r = _pick(t, (256, 128))
    qw = ch * 2 * HEAD

    def body(kv_ref, kr_ref, kn_ref, ko_ref, vo_ref):
        ones = jnp.ones((r, HEAD), BF16)
        for hh in range(ch):
            lo = hh * 2 * HEAD
            nhat, _ = _rms(kv_ref[:, lo:lo + HEAD], HEAD)
            ko_ref[:, lo:lo + HEAD] = (nhat * kn_ref[...]).astype(BF16)
            ko_ref[:, lo + HEAD:lo + 2 * HEAD] = kr_ref[...]
            vo_ref[:, lo:lo + HEAD] = kv_ref[:, lo + HEAD:lo + 2 * HEAD].astype(BF16)
            vo_ref[:, lo + HEAD:lo + 2 * HEAD] = ones

    return _row_call(
        body, name, t, r, [_row_spec(r, qw), _row_spec(r, HEAD), _const_spec((1, HEAD))], (kv, kr_rot, kn_g),
        [_row_spec(r, qw), _row_spec(r, qw)],
        [jax.ShapeDtypeStruct((t, qw), BF16), jax.ShapeDtypeStruct((t, qw), BF16)])


def _kv_prep_bwd(kv, ch, kn_g, dk_cat, dv, name):
    t = kv.shape[0]
    r = _pick(t, (256, 128))
    qw = ch * 2 * HEAD

    def body(kv_ref, kn_ref, dk_ref, dv_ref, dkv_ref, dkr_ref, dkn_ref):
        _zero_at_first_step([dkn_ref])
        dkr = jnp.zeros((r, HEAD), F32)
        for hh in range(ch):
            lo = hh * 2 * HEAD
            nhat, nr = _rms(kv_ref[:, lo:lo + HEAD], HEAD)
            d_n = dk_ref[:, lo:lo + HEAD].astype(F32)
            dkn_ref[...] += _colsum(d_n * nhat)
            dkv_ref[:, lo:lo + HEAD] = _rms_bwd(d_n * kn_ref[...], nhat, nr, HEAD).astype(BF16)
            dkv_ref[:, lo + HEAD:lo + 2 * HEAD] = dv_ref[:, hh * HEAD:(hh + 1) * HEAD]
            dkr = dkr + dk_ref[:, lo + HEAD:lo + 2 * HEAD].astype(F32)
        dkr_ref[...] = dkr

    return _row_call(
        body, name, t, r,
        [_row_spec(r, qw), _const_spec((1, HEAD)), _row_spec(r, qw), _row_spec(r, ch * HEAD)], (kv, kn_g, dk_cat, dv),
        [_row_spec(r, qw), _row_spec(r, HEAD), _const_spec((1, HEAD))],
        [jax.ShapeDtypeStruct((t, qw), BF16), jax.ShapeDtypeStruct((t, HEAD), F32),
         jax.ShapeDtypeStruct((1, HEAD), F32)])


def _attn_tiles(t):
    return _pick(t, (2048, 1024, 512, 256, 128)), _pick(t, (1024, 512, 256, 128))


_NT = (((1,), (1,)), ((), ()))
LOG2E = 1.4426950408889634


def _attn_fwd(q_cat, k_cat, v_aug, ch, scale, name):
    t = q_cat.shape[0]
    tq, tk = _attn_tiles(t)
    nk = t // tk
    c2 = scale * LOG2E

    def body(q_ref, k_ref, v_ref, o_ref, lse_ref, s_scr, m_scr, acc_scr):
        j = pl.program_id(2)

        def scores(slot):
            s_scr[slot] = lax.dot_general(q_ref[...], k_ref[...], _NT, preferred_element_type=F32) * c2

        def absorb(slot):
            s = s_scr[slot]
            m_old = m_scr[...]
            m_new = jnp.maximum(m_old, jnp.max(s, axis=-1, keepdims=True))
            p = jnp.exp2(s - m_new).astype(BF16)
            acc_scr[...] = (jnp.exp2(m_old - m_new) * acc_scr[...]
                            + jnp.dot(p, v_ref[...], preferred_element_type=F32))
            m_scr[...] = m_new

        @pl.when(j == 0)
        def _():
            m_scr[...] = jnp.full(m_scr.shape, -jnp.inf, F32)
            acc_scr[...] = jnp.zeros(acc_scr.shape, F32)
            scores(0)

        for parity in (0, 1):
            @pl.when((j > 0) & (j < nk) & (j % 2 == parity))
            def _():
                scores(parity)
                absorb(1 - parity)

        @pl.when(j == nk)
        def _():
            absorb((nk - 1) % 2)
            acc = acc_scr[...]
            l_sum = acc[:, HEAD:]
            o_ref[...] = (acc[:, :HEAD] / l_sum).astype(BF16)
            lse_ref[0] = m_scr[...] + jnp.log(l_sum[:, 0:1]) * LOG2E

    return pl.pallas_call(
        body, name=name, grid=(ch, t // tq, nk + 1),
        in_specs=[pl.BlockSpec((tq, 2 * HEAD), lambda h, i, j: (i, h)),
                  pl.BlockSpec((tk, 2 * HEAD), lambda h, i, j: (jnp.minimum(j, nk - 1), h)),
                  pl.BlockSpec((tk, 2 * HEAD), lambda h, i, j: (jnp.maximum(j - 1, 0), h))],
        out_specs=[pl.BlockSpec((tq, HEAD), lambda h, i, j: (i, h)),
                   pl.BlockSpec((1, tq, 1), lambda h, i, j: (h, i, 0))],
        out_shape=[jax.ShapeDtypeStruct((t, ch * HEAD), BF16), jax.ShapeDtypeStruct((ch, t, 1), F32)],
        scratch_shapes=[pltpu.VMEM((2, tq, tk), F32), pltpu.VMEM((tq, 1), F32), pltpu.VMEM((tq, 2 * HEAD), F32)],
        compiler_params=_params(("parallel", "parallel", "arbitrary")),
    )(q_cat, k_cat, v_aug)


def _attn_bwd(q_cat, k_cat, k_cat_t, v_aug, do, lse_row, d_row, ch, scale, name):
    t = q_cat.shape[0]
    tk = _pick(t, (1024, 512, 256, 128))
    tq = _pick(t, (512, 256, 128))
    nk, nq = t // tk, t // tq
    c2 = scale * LOG2E

    def body(q_ref, do_ref, qp_ref, dop_ref, lse_ref, d_ref, k_ref, kt_ref, v_ref,
             dqt_ref, dk_ref, dv_ref, s_scr, dp_scr, dk_scr, dv_scr):
        ki, j = pl.program_id(1), pl.program_id(2)

        def products(slot):
            s_scr[slot] = lax.dot_general(k_ref[...], q_ref[...], _NT, preferred_element_type=F32) * c2
            dp_scr[slot] = lax.dot_general(v_ref[...], do_ref[...], _NT, preferred_element_type=F32)

        def absorb(slot):
            q, do_v = qp_ref[...], dop_ref[...]
            pt = jnp.exp2(s_scr[slot] - lse_ref[0])
            dv_scr[...] += jnp.dot(pt.astype(BF16), do_v, preferred_element_type=F32)
            dst = (pt * (dp_scr[slot] - d_ref[0])).astype(BF16)
            dk_scr[...] += jnp.dot(dst, q, preferred_element_type=F32)
            part = jnp.dot(kt_ref[...], dst, preferred_element_type=F32)
            cols = pl.ds(pl.multiple_of((j - 1) * tq, tq), tq)

            @pl.when(ki == 0)
            def _():
                dqt_ref[:, cols] = part

            @pl.when(ki > 0)
            def _():
                dqt_ref[:, cols] += part

        @pl.when(j == 0)
        def _():
            dk_scr[...] = jnp.zeros(dk_scr.shape, F32)
            dv_scr[...] = jnp.zeros(dv_scr.shape, F32)
            products(0)

        for parity in (0, 1):
            @pl.when((j > 0) & (j < nq) & (j % 2 == parity))
            def _():
                products(parity)
                absorb(1 - parity)

        @pl.when(j == nq)
        def _():
            absorb((nq - 1) % 2)
            dk_ref[...] = (dk_scr[...] * scale).astype(BF16)
            dv_ref[...] = dv_scr[...].astype(BF16)

    def cur(i):
        return jnp.minimum(i, nq - 1)

    def prev(i):
        return jnp.maximum(i - 1, 0)

    stat = pl.BlockSpec((1, 1, tq), lambda h, j, i: (h, 0, prev(i)))
    return pl.pallas_call(
        body, name=name, grid=(ch, nk, nq + 1),
        in_specs=[pl.BlockSpec((tq, 2 * HEAD), lambda h, j, i: (cur(i), h)),
                  pl.BlockSpec((tq, HEAD), lambda h, j, i: (cur(i), h)),
                  pl.BlockSpec((tq, 2 * HEAD), lambda h, j, i: (prev(i), h)),
                  pl.BlockSpec((tq, HEAD), lambda h, j, i: (prev(i), h)), stat, stat,
                  pl.BlockSpec((tk, 2 * HEAD), lambda h, j, i: (j, h)),
                  pl.BlockSpec((2 * HEAD, tk), lambda h, j, i: (h, j)),
                  pl.BlockSpec((tk, HEAD), lambda h, j, i: (j, 2 * h))],
        out_specs=[pl.BlockSpec((2 * HEAD, t), lambda h, j, i: (h, 0)),
                   pl.BlockSpec((tk, 2 * HEAD), lambda h, j, i: (j, h)),
                   pl.BlockSpec((tk, HEAD), lambda h, j, i: (j, h))],
        out_shape=[jax.ShapeDtypeStruct((ch * 2 * HEAD, t), F32), jax.ShapeDtypeStruct((t, ch * 2 * HEAD), BF16),
                   jax.ShapeDtypeStruct((t, ch * HEAD), BF16)],
        scratch_shapes=[pltpu.VMEM((2, tk, tq), F32), pltpu.VMEM((2, tk, tq), F32),
                        pltpu.VMEM((tk, 2 * HEAD), F32), pltpu.VMEM((tk, HEAD), F32)],
        compiler_params=_params(("parallel", "arbitrary", "arbitrary")),
    )(q_cat, do, q_cat, do, lse_row, d_row, k_cat, k_cat_t, v_aug)


def _attn_post_fwd(o, proj, z_off, cw, gc, name, into=None):
    t = o.shape[0]
    r = _pick(t, (256, 128))

    def body(o_ref, z_ref, gc_ref, y_ref):
        sil, _ = _silu_and_grad(z_ref[...].astype(F32))
        yhat, _ = _rms(o_ref[...].astype(F32) * sil, cw)
        y_ref[...] = (yhat * gc_ref[...]).astype(BF16)

    return _row_call(body, name, t, r,
                     [_row_spec(r, cw), _row_spec(r, cw, _col_block(z_off, cw)), _const_spec((1, cw))], (o, proj, gc),
                     _row_spec(r, cw), jax.ShapeDtypeStruct((t, cw), BF16), into=into)


def _attn_post_bwd(o, proj, z_off, cw, gc, dy, dy_col, name):
    t = o.shape[0]
    ch = cw // HEAD
    r = _pick(t, (256, 128))

    def body(o_ref, z_ref, gc_ref, dy_ref, do_ref, dz_ref, ds_ref, dgc_ref):
        _zero_at_first_step([dgc_ref])
        o_v, z = o_ref[...].astype(F32), z_ref[...].astype(F32)
        sil, dsil = _silu_and_grad(z)
        yhat, rr = _rms(o_v * sil, cw)
        dy_f = dy_ref[...].astype(F32)
        dgc_ref[...] += _colsum(dy_f * yhat)
        dyc = _rms_bwd(dy_f * gc_ref[...], yhat, rr, cw)
        do_b = (dyc * sil).astype(BF16)
        do_ref[...] = do_b
        dz_ref[...] = (dyc * o_v * dsil).astype(BF16)
        prod = do_b.astype(F32) * o_v
        for hh in range(ch):
            ds_ref[hh] = jnp.sum(prod[:, hh * HEAD:(hh + 1) * HEAD], axis=-1, keepdims=True)

    return _row_call(
        body, name, t, r,
        [_row_spec(r, cw), _row_spec(r, cw, _col_block(z_off, cw)), _const_spec((1, cw)), _row_spec(r, cw, dy_col)],
        (o, proj, gc, dy),
        [_row_spec(r, cw), _row_spec(r, cw), pl.BlockSpec((ch, r, 1), lambda i: (0, i, 0)), _const_spec((1, cw))],
        [jax.ShapeDtypeStruct((t, cw), BF16), jax.ShapeDtypeStruct((t, cw), BF16),
         jax.ShapeDtypeStruct((ch, t, 1), F32), jax.ShapeDtypeStruct((1, cw), F32)])


def _ple_fwd(h1, gpre, pp, name):
    t, d = h1.shape
    r = _pick(t, (256, 128))

    def body(h_ref, g_ref, p_ref, o_ref):
        o_ref[...] = h_ref[...] + _sigmoid(g_ref[...]) * p_ref[...]

    return _row_call(body, name, t, r, [_row_spec(r, d)] * 3, (h1, gpre, pp), _row_spec(r, d),
                     jax.ShapeDtypeStruct((t, d), F32))


def _ple_bwd(gpre, pp, dh, name):
    t, d = dh.shape
    r = _pick(t, (256, 128))

    def body(g_ref, p_ref, dh_ref, dg_ref, dp_ref):
        sig = _sigmoid(g_ref[...])
        dh_v = dh_ref[...]
        dg_ref[...] = (dh_v * p_ref[...] * sig * (1.0 - sig)).astype(BF16)
        dp_ref[...] = (dh_v * sig).astype(BF16)

    return _row_call(body, name, t, r, [_row_spec(r, d)] * 3, (gpre, pp, dh), [_row_spec(r, d)] * 2,
                     [jax.ShapeDtypeStruct((t, d), BF16)] * 2)


def _loss_and_grad(h, target, name):
    t, d = h.shape
    r = _pick(t, (256, 128))

    def body(h_ref, t_ref, l_ref, dh_ref):
        _zero_at_first_step([l_ref])
        err = h_ref[...] - t_ref[...]
        l_ref[...] += jnp.sum(jnp.sum(err * err, axis=-1, keepdims=True), axis=0, keepdims=True) * (0.5 / d)
        dh_ref[...] = err * (1.0 / d)

    return _row_call(body, name, t, r, [_row_spec(r, d)] * 2, (h, target), [_const_spec((1, 1)), _row_spec(r, d)],
                     [jax.ShapeDtypeStruct((1, 1), F32), jax.ShapeDtypeStruct((t, d), F32)])


def _ew_rows(rows, cols):
    cap = max(8, (1 << 19) // max(cols, 1))
    for cand in (1024, 512, 256, 128, 64, 32, 16, 8):
        if cand <= cap and rows % cand == 0:
            return cand
    return rows


def _pair_sum_bf16(a, b, name):
    n, rows, cols = a.shape
    rb = _ew_rows(rows, cols)

    def body(a_ref, b_ref, o_ref):
        o_ref[...] = (a_ref[...] + b_ref[...]).astype(BF16)

    spec = pl.BlockSpec((1, rb, cols), lambda s, i: (s, i, 0))
    return pl.pallas_call(body, name=name, grid=(n, rows // rb), in_specs=[spec, spec], out_specs=spec,
                          out_shape=jax.ShapeDtypeStruct(a.shape, BF16),
                          compiler_params=_params(("parallel", "parallel")))(a, b)


def _shard_sum(a, b, recv, name):
    rows, cols = a.shape
    rb = _ew_rows(rows, cols)

    def body(a_ref, b_ref, r_ref, o_ref):
        o_ref[...] = ((a_ref[...] + b_ref[...]) + r_ref[0].astype(F32) + r_ref[1].astype(F32)
                      + r_ref[2].astype(F32))

    spec = pl.BlockSpec((rb, cols), lambda i: (i, 0))
    return pl.pallas_call(body, name=name, grid=(rows // rb,),
                          in_specs=[spec, spec, pl.BlockSpec((N_SHARD - 1, rb, cols), lambda i: (0, i, 0))],
                          out_specs=spec, out_shape=jax.ShapeDtypeStruct(a.shape, F32),
                          compiler_params=_params(("parallel",)))(a, b, recv)


def _sum_devices(g, name):
    n, rows, cols = g.shape
    rb = _ew_rows(rows, cols)

    def body(g_ref, o_ref):
        acc = g_ref[0]
        for k in range(1, n):
            acc = acc + g_ref[k]
        o_ref[...] = acc

    return pl.pallas_call(body, name=name, grid=(rows // rb,),
                          in_specs=[pl.BlockSpec((n, rb, cols), lambda i: (0, i, 0))],
                          out_specs=pl.BlockSpec((rb, cols), lambda i: (i, 0)),
                          out_shape=jax.ShapeDtypeStruct((rows, cols), F32),
                          compiler_params=_params(("parallel",)))(g)


def _adamw_update(w, g_v, m, v):
    m_new = ADAM_B1 * m + (1.0 - ADAM_B1) * g_v
    v_new = ADAM_B2 * v + (1.0 - ADAM_B2) * (g_v * g_v)
    m_hat = m_new / (1.0 - ADAM_B1 ** ADAM_STEP)
    v_hat = v_new / (1.0 - ADAM_B2 ** ADAM_STEP)
    return -ADAM_LR * (m_hat / (jnp.sqrt(v_hat) + ADAM_EPS) + ADAM_WD * w), m_new, v_new


def _adamw(w, g, m, v, name):
    rows, cols = w.shape
    rb = _ew_rows(rows, cols)

    def body(w_ref, g_ref, m_ref, v_ref, d_ref, mo_ref, vo_ref):
        d_ref[...], mo_ref[...], vo_ref[...] = _adamw_update(w_ref[...], g_ref[...], m_ref[...], v_ref[...])

    spec = pl.BlockSpec((rb, cols), lambda i: (i, 0))
    return pl.pallas_call(body, name=name, grid=(rows // rb,), in_specs=[spec] * 4, out_specs=[spec] * 3,
                          out_shape=[jax.ShapeDtypeStruct(w.shape, F32)] * 3,
                          compiler_params=_params(("parallel",)))(w, g, m, v)


def _adamw_two_halves(w, own, recv, core_flag, m, v, name):
    depth, rows, cols = w.shape
    assert depth % 2 == 0 and own.shape == (depth // 2 * rows, cols)
    rb = _ew_rows(rows, cols)
    nb = rows // rb
    per = depth // 2

    def body(w_ref, own_ref, recv_ref, flag_ref, m_ref, v_ref, g_ref, d_ref, mo_ref, vo_ref):
        half = pl.program_id(0).astype(F32)
        g_v = jnp.where(flag_ref[...] == half, own_ref[...], recv_ref[...])
        g_ref[...] = g_v
        d_ref[...], mo_ref[...], vo_ref[...] = _adamw_update(w_ref[...], g_v, m_ref[...], v_ref[...])

    full = pl.BlockSpec((None, rb, cols), lambda k, l, i: (k * per + l, i, 0))
    half_spec = pl.BlockSpec((rb, cols), lambda k, l, i: (l * nb + i, 0))
    return pl.pallas_call(
        body, name=name, grid=(2, per, nb),
        in_specs=[full, half_spec, half_spec, pl.BlockSpec((1, 1), lambda k, l, i: (0, 0)), full, full],
        out_specs=[full] * 4, out_shape=[jax.ShapeDtypeStruct(w.shape, F32)] * 4,
        compiler_params=_params(("parallel", "parallel", "parallel")))(w, own, recv, core_flag, m, v)


def _place():
    return lax.axis_index("x"), lax.axis_index("y"), lax.axis_index("c")


def _other_chips(x, y):
    return [(1 - x, y), (x, 1 - y), (1 - x, 1 - y)]


_ANY = pl.BlockSpec(memory_space=pl.ANY)


def _gather_shards(shards, name):
    n = len(shards)
    n_peer = N_SHARD - 1

    def body(*refs):
        ins, outs = refs[:n], refs[n:2 * n]
        ici_send, ici_recv, d2d_send, d2d_recv = refs[2 * n:]
        x, y, c = _place()
        chips = _other_chips(x, y)

        def half(a, which):
            h0 = shards[a].shape[0] // 2
            return pl.ds(which * h0, h0)

        def ici_copy(a, k, slot, px, py):
            src = ins[a].at[half(a, c)]
            return pltpu.make_async_remote_copy(
                src_ref=src, dst_ref=outs[a].at[slot, half(a, c)], send_sem=ici_send.at[a * n_peer + k],
                recv_sem=ici_recv.at[a * n_peer + k], device_id=(px, py, c), device_id_type=MESH)

        def d2d_copy(a, k, slot, which):
            rows = outs[a].at[slot, half(a, which)]
            return pltpu.make_async_remote_copy(
                src_ref=rows, dst_ref=rows, send_sem=d2d_send.at[a * n_peer + k],
                recv_sem=d2d_recv.at[a * n_peer + k], device_id=(x, y, 1 - c), device_id_type=MESH)

        sends = []
        for a in range(n):
            for k, (px, py) in enumerate(chips):
                cp = ici_copy(a, k, 2 * x + y, px, py)
                cp.start()
                sends.append(cp)
        for a in range(n):
            for k, (px, py) in enumerate(chips):
                ici_copy(a, k, 2 * px + py, px, py).wait_recv()
                fw = d2d_copy(a, k, 2 * px + py, c)
                fw.start()
                sends.append(fw)
        for a in range(n):
            for k, (px, py) in enumerate(chips):
                d2d_copy(a, k, 2 * px + py, 1 - c).wait_recv()
        for cp in sends:
            cp.wait_send()

    n_sem = n * n_peer
    return pl.pallas_call(
        body, name=name, in_specs=[_ANY] * n, out_specs=[_ANY] * n,
        out_shape=[jax.ShapeDtypeStruct((N_SHARD,) + s.shape, s.dtype) for s in shards],
        scratch_shapes=[pltpu.SemaphoreType.DMA((n_sem,)), pltpu.SemaphoreType.DMA((n_sem,)),
                        pltpu.SemaphoreType.DMA((n_sem,)), pltpu.SemaphoreType.DMA((n_sem,))],
    )(*shards)


def _to_sibling(arrs, other_half, name):
    n = len(arrs)

    def body(*refs):
        ins, outs = refs[:n], refs[n:2 * n]
        send_sems, recv_sems = refs[2 * n:]
        x, y, c = _place()
        sends = []
        for a in range(n):
            cp = pltpu.make_async_remote_copy(
                src_ref=ins[a].at[1 - c] if other_half else ins[a], dst_ref=outs[a], send_sem=send_sems.at[a],
                recv_sem=recv_sems.at[a], device_id=(x, y, 1 - c), device_id_type=MESH)
            cp.start()
            sends.append(cp)
        for cp in sends:
            cp.wait_recv()
        for cp in sends:
            cp.wait_send()

    return pl.pallas_call(
        body, name=name, in_specs=[_ANY] * n, out_specs=[_ANY] * n,
        out_shape=[jax.ShapeDtypeStruct(g.shape[1:] if other_half else g.shape, g.dtype) for g in arrs],
        scratch_shapes=[pltpu.SemaphoreType.DMA((n,)), pltpu.SemaphoreType.DMA((n,))],
    )(*arrs)


def _to_owner_chips(parts, name):
    n = len(parts)
    n_peer = N_SHARD - 1

    def body(*refs):
        ins, outs = refs[:n], refs[n:2 * n]
        send_sems, recv_sems = refs[2 * n:]
        x, y, c = _place()
        chips = _other_chips(x, y)
        sends = []
        for a in range(n):
            for k, (px, py) in enumerate(chips):
                cp = pltpu.make_async_remote_copy(
                    src_ref=ins[a].at[2 * px + py], dst_ref=outs[a].at[k], send_sem=send_sems.at[a * n_peer + k],
                    recv_sem=recv_sems.at[a * n_peer + k], device_id=(px, py, c), device_id_type=MESH)
                cp.start()
                sends.append(cp)
        for cp in sends:
            cp.wait_recv()
        for cp in sends:
            cp.wait_send()

    return pl.pallas_call(
        body, name=name, in_specs=[_ANY] * n, out_specs=[_ANY] * n,
        out_shape=[jax.ShapeDtypeStruct((n_peer,) + p.shape[1:], p.dtype) for p in parts],
        scratch_shapes=[pltpu.SemaphoreType.DMA((n * n_peer,)), pltpu.SemaphoreType.DMA((n * n_peer,))],
    )(*parts)


def _gather_devices(buf, name):
    n_peer = N_DEV - 1

    def body(in_ref, out_ref, send_sems, recv_sems, local_sem):
        x, y, c = _place()
        me = 4 * x + 2 * y + c
        mine = pltpu.make_async_copy(in_ref, out_ref.at[me], local_sem)
        mine.start()
        peers = []
        for k in range(1, N_DEV):
            fx, fy, fc = (k >> 2) & 1, (k >> 1) & 1, k & 1
            peers.append((x ^ fx, y ^ fy, c ^ fc))
        sends = []
        for k, peer in enumerate(peers):
            cp = pltpu.make_async_remote_copy(
                src_ref=in_ref, dst_ref=out_ref.at[me], send_sem=send_sems.at[k], recv_sem=recv_sems.at[k],
                device_id=peer, device_id_type=MESH)
            cp.start()
            sends.append(cp)
        for k, (px, py, pc) in enumerate(peers):
            pltpu.make_async_remote_copy(
                src_ref=in_ref, dst_ref=out_ref.at[4 * px + 2 * py + pc], send_sem=send_sems.at[k],
                recv_sem=recv_sems.at[k], device_id=(px, py, pc), device_id_type=MESH).wait_recv()
        for cp in sends:
            cp.wait_send()
        mine.wait()

    return pl.pallas_call(
        body, name=name, in_specs=[_ANY], out_specs=_ANY,
        out_shape=jax.ShapeDtypeStruct((N_DEV,) + buf.shape, buf.dtype),
        scratch_shapes=[pltpu.SemaphoreType.DMA((n_peer,)), pltpu.SemaphoreType.DMA((n_peer,)),
                        pltpu.SemaphoreType.DMA(())],
    )(buf)


class _Dims:
    def __init__(self, x, p, w_in, sgu_norm, conv_w, kv_norm, w_ukv, w_out):
        self.t, self.d = x.shape[1], x.shape[2]
        self.depth = w_in.shape[0]
        self.ple = p.shape[3]
        self.in_w = w_in.shape[2] * N_SHARD
        self.ah = sgu_norm.shape[1]
        self.aw = self.ah * HEAD
        self.bw = conv_w.shape[2] * N_SHARD
        self.kvr = kv_norm.shape[1]
        self.ch = w_ukv.shape[2] * N_SHARD // (2 * HEAD)
        self.cw = self.ch * HEAD
        self.mix = w_out.shape[1] * N_SHARD
        assert self.mix == self.aw + self.bw + self.cw and self.aw == self.bw
        self.qw = self.ch * 2 * HEAD
        segs = [('a', 3 * self.aw, self.aw), ('b', 4 * self.bw, self.bw), ('ckv', self.kvr, self.kvr),
                ('q', self.qw, self.qw), ('cz', self.cw, self.cw), ('kr', HEAD, HEAD)]
        off = 0
        self.off = {}
        for nm, width, align in segs:
            off = -(-off // align) * align
            self.off[nm] = off
            off += width
        self.inp = -(-off // 512) * 512
        q_real = self.ch * (HEAD + ROPE)
        widths = [3 * self.aw, 4 * self.bw, q_real, self.kvr, ROPE, self.cw]
        assert sum(widths) == self.in_w
        starts = [0]
        for wd in widths:
            starts.append(starts[-1] + wd)
        self.src = dict(zip(['a', 'b', 'q', 'ckv', 'kr', 'cz'], zip(starts[:-1], widths)))


def _rearrange_w_in(w, dm):
    lead = w.shape[:-1]
    pieces = {}
    for nm in ('a', 'b', 'ckv', 'cz'):
        s, wd = dm.src[nm]
        pieces[nm] = w[..., s:s + wd]
    s, wd = dm.src['q']
    q = w[..., s:s + wd].reshape(lead + (dm.ch, HEAD + ROPE))
    pieces['q'] = jnp.pad(q, [(0, 0)] * (q.ndim - 1) + [(0, 2 * HEAD - HEAD - ROPE)]).reshape(lead + (dm.qw,))
    s, wd = dm.src['kr']
    pieces['kr'] = jnp.pad(w[..., s:s + wd], [(0, 0)] * len(lead) + [(0, HEAD - ROPE)])
    out, cur = [], 0
    for nm in sorted(dm.off, key=lambda k: dm.off[k]):
        if dm.off[nm] > cur:
            out.append(jnp.zeros(lead + (dm.off[nm] - cur,), w.dtype))
        out.append(pieces[nm])
        cur = dm.off[nm] + pieces[nm].shape[-1]
    if dm.inp > cur:
        out.append(jnp.zeros(lead + (dm.inp - cur,), w.dtype))
    return jnp.concatenate(out, axis=-1)


def _unarrange_w_in(g, dm):
    lead = g.shape[:-1]

    def seg(nm, width):
        return g[..., dm.off[nm]:dm.off[nm] + width]

    q = seg('q', dm.qw).reshape(lead + (dm.ch, 2 * HEAD))[..., :HEAD + ROPE].reshape(lead + (dm.ch * (HEAD + ROPE),))
    return jnp.concatenate([seg('a', 3 * dm.aw), seg('b', 4 * dm.bw), q, seg('ckv', dm.kvr), seg('kr', ROPE),
                            seg('cz', dm.cw)], axis=-1)


def _assemble_dproj(parts, dm, t):
    out, cur = [], 0
    for nm in sorted(dm.off, key=lambda k: dm.off[k]):
        if dm.off[nm] > cur:
            out.append(jnp.zeros((t, dm.off[nm] - cur), BF16))
        out.append(parts[nm])
        cur = dm.off[nm] + parts[nm].shape[-1]
    if dm.inp > cur:
        out.append(jnp.zeros((t, dm.inp - cur), BF16))
    return jnp.concatenate(out, axis=-1)


def _rope_tables(positions):
    inv = 1.0 / (ROPE_BASE ** (jnp.arange(0, ROPE, 2, dtype=F32) / ROPE))
    ang = positions.astype(F32)[:, None] * inv
    cos, sin = jnp.cos(ang), jnp.sin(ang)
    t = positions.shape[0]
    half = ROPE // 2
    cos_t = jnp.concatenate([cos, cos, jnp.zeros((t, HEAD - ROPE), F32)], axis=-1)
    sin_a = jnp.concatenate([-sin, jnp.zeros((t, HEAD - half), F32)], axis=-1)
    sin_b = jnp.concatenate([jnp.zeros((t, half), F32), sin, jnp.zeros((t, HEAD - ROPE), F32)], axis=-1)
    return cos_t, sin_a, sin_b


def _pad_gain(g):
    return jnp.pad(g, (0, HEAD - g.shape[0]))[None, :]


def _shard_major(g, axis):
    shape = g.shape
    g = g.reshape(shape[:axis] + (N_SHARD, shape[axis] // N_SHARD) + shape[axis + 1:])
    g = jnp.moveaxis(g, axis, 0)
    cols = g.shape[-1]
    g = g.reshape(N_SHARD, 2, -1, cols)
    return jnp.swapaxes(g, 0, 1)


def _pack(arrs):
    flat = jnp.concatenate([a.reshape(-1) for a in arrs])
    pad = (-flat.shape[0]) % (8 * HEAD)
    return jnp.pad(flat, (0, pad)).reshape(-1, HEAD)


def _unpack(buf, shapes):
    flat = buf.reshape(-1)
    out, cur = [], 0
    for s in shapes:
        size = 1
        for v in s:
            size *= v
        out.append(flat[cur:cur + size].reshape(s))
        cur += size
    return out


def kernel(x, p, positions, attn_norm, w_in, sgu_norm, w_spatial, b_spatial, conv_w, conv_b, kv_norm, w_ukv, q_nope_norm, q_rope_norm, k_nope_norm, k_rope_norm, out_norm, w_out, ple_norm, w_ple_gate, w_ple_proj, loss_target, m_attn_norm, m_w_in, m_sgu_norm, m_w_spatial, m_b_spatial, m_conv_w, m_conv_b, m_kv_norm, m_w_ukv, m_q_nope_norm, m_q_rope_norm, m_k_nope_norm, m_k_rope_norm, m_out_norm, m_w_out, m_ple_norm, m_w_ple_gate, m_w_ple_proj, v_attn_norm, v_w_in, v_sgu_norm, v_w_spatial, v_b_spatial, v_conv_w, v_conv_b, v_kv_norm, v_w_ukv, v_q_nope_norm, v_q_rope_norm, v_k_nope_norm, v_k_rope_norm, v_out_norm, v_w_out, v_ple_norm, v_w_ple_gate, v_w_ple_proj):
    weights = dict(attn_norm=attn_norm, w_in=w_in, sgu_norm=sgu_norm, w_spatial=w_spatial, b_spatial=b_spatial,
                   conv_w=conv_w, conv_b=conv_b, kv_norm=kv_norm, w_ukv=w_ukv, q_nope_norm=q_nope_norm,
                   q_rope_norm=q_rope_norm, k_nope_norm=k_nope_norm, k_rope_norm=k_rope_norm, out_norm=out_norm,
                   w_out=w_out, ple_norm=ple_norm, w_ple_gate=w_ple_gate, w_ple_proj=w_ple_proj)
    mom_m = dict(attn_norm=m_attn_norm, w_in=m_w_in, sgu_norm=m_sgu_norm, w_spatial=m_w_spatial,
                 b_spatial=m_b_spatial, conv_w=m_conv_w, conv_b=m_conv_b, kv_norm=m_kv_norm, w_ukv=m_w_ukv,
                 q_nope_norm=m_q_nope_norm, q_rope_norm=m_q_rope_norm, k_nope_norm=m_k_nope_norm,
                 k_rope_norm=m_k_rope_norm, out_norm=m_out_norm, w_out=m_w_out, ple_norm=m_ple_norm,
                 w_ple_gate=m_w_ple_gate, w_ple_proj=m_w_ple_proj)
    mom_v = dict(attn_norm=v_attn_norm, w_in=v_w_in, sgu_norm=v_sgu_norm, w_spatial=v_w_spatial,
                 b_spatial=v_b_spatial, conv_w=v_conv_w, conv_b=v_conv_b, kv_norm=v_kv_norm, w_ukv=v_w_ukv,
                 q_nope_norm=v_q_nope_norm, q_rope_norm=v_q_rope_norm, k_nope_norm=v_k_nope_norm,
                 k_rope_norm=v_k_rope_norm, out_norm=v_out_norm, w_out=v_w_out, ple_norm=v_ple_norm,
                 w_ple_gate=v_w_ple_gate, w_ple_proj=v_w_ple_proj)
    dm = _Dims(x, p, w_in, sgu_norm, conv_w, kv_norm, w_ukv, w_out)
    t, d, depth = dm.t, dm.d, dm.depth
    shard = 2 * lax.axis_index("x") + lax.axis_index("y")
    core = lax.axis_index("c")
    scale = float(HEAD + ROPE) ** -0.5

    local = [weights[n].astype(BF16) for n in BIG] + [conv_w]
    gathered = [lax.dynamic_update_slice(g, mine[None], (shard,) + (0,) * mine.ndim)
                for g, mine in zip(_gather_shards(local, "gather_weights"), local)]
    full = {n: jnp.concatenate([gathered[i][s] for s in range(N_SHARD)], axis=BIG_AXIS[n])
            for i, n in enumerate(BIG)}
    conv_w_full = jnp.concatenate([gathered[len(BIG)][s] for s in range(N_SHARD)], axis=2)
    w_in_r = _rearrange_w_in(full['w_in'], dm)

    tabs = _rope_tables(positions[0])
    h = x[0]
    saved = []
    for i in range(depth):
        tag = f"l{i}_"
        ga, gb, gc = (out_norm[i][None, :dm.aw], out_norm[i][None, dm.aw:dm.aw + dm.bw],
                      out_norm[i][None, dm.aw + dm.bw:])
        ws_b = w_spatial[i].astype(BF16)
        bb = jnp.broadcast_to(b_spatial[i][:, :, None], (dm.ah, HEAD, HEAD))
        qn_g, qr_g = q_nope_norm[i][None, :], _pad_gain(q_rope_norm[i])
        kn_g, kr_g = k_nope_norm[i][None, :], _pad_gain(k_rope_norm[i])
        kv_g = kv_norm[i][None, :]
        hn = _norm_fwd(h, attn_norm[i][None, :], tag + "norm1")
        proj = _matmul(hn, w_in_r, 'nn', BF16, tag + "proj", b_layer=i)
        y = _sgu_fwd(proj, dm.off['a'], dm.aw, sgu_norm[i], ws_b, bb, ga, tag + "sgu",
                     into=(jnp.zeros((t, dm.mix), BF16), 0))
        y, yconv = _conv_fwd(proj, dm.off['b'], dm.bw, conv_w_full[i], conv_b[i][None, :], gb, tag + "conv",
                             into=(y, _col_block(dm.aw, dm.bw)))
        q_cat, ckv_n, kr_rot = _mla_prep_fwd(proj, dm.off['q'], dm.off['ckv'], dm.off['kr'], dm.ch, dm.kvr, tabs,
                                             qn_g, qr_g, kr_g, kv_g, tag + "mla_prep")
        kv = _matmul(ckv_n, full['w_ukv'], 'nn', F32, tag + "kv_up", b_layer=i)
        k_cat, v_aug = _kv_prep_fwd(kv, kr_rot, dm.ch, kn_g, tag + "kv_prep")
        o, lse = _attn_fwd(q_cat, k_cat, v_aug, dm.ch, scale, tag + "attn")
        y = _attn_post_fwd(o, proj, dm.off['cz'], dm.cw, gc, tag + "attn_post",
                           into=(y, _col_block(dm.aw + dm.bw, dm.cw)))
        h1 = _matmul(y, full['w_out'], 'nn', F32, tag + "out", add=h, b_layer=i)
        hn2 = _norm_fwd(h1, ple_norm[i][None, :], tag + "norm2")
        gpre = _matmul(hn2, full['w_ple_gate'], 'nn', F32, tag + "gate", b_layer=i)
        p_b = p[i, 0].astype(BF16)
        pp = _matmul(p_b, full['w_ple_proj'], 'nn', F32, tag + "ple_proj", b_layer=i)
        h2 = _ple_fwd(h1, gpre, pp, tag + "ple")
        saved.append(dict(h=h, hn=hn, proj=proj, yconv=yconv, q_cat=q_cat, ckv_n=ckv_n, kv=kv, k_cat=k_cat,
                          v=v_aug, o=o, lse=lse, y=y, h1=h1, hn2=hn2, gpre=gpre, pp=pp, p_b=p_b, ws_b=ws_b, bb=bb,
                          gains=(ga, gb, gc, qn_g, qr_g, kn_g, kr_g, kv_g)))
        h = h2

    loss_part, dh = _loss_and_grad(h, loss_target[0], "loss")
    loss = lax.psum(loss_part[0, 0], ("x", "y", "c"))

    grads = {n: [None] * depth for n in WEIGHTS}
    for i in reversed(range(depth)):
        tag = f"l{i}_b_"
        sv = saved[i]
        ga, gb, gc, qn_g, qr_g, kn_g, kr_g, kv_g = sv['gains']
        proj = sv['proj']
        dgpre, dpp = _ple_bwd(sv['gpre'], sv['pp'], dh, tag + "ple")
        grads['w_ple_proj'][i] = _matmul(sv['p_b'], dpp, 'tn', F32, tag + "d_w_ple_proj")
        grads['w_ple_gate'][i] = _matmul(sv['hn2'], dgpre, 'tn', F32, tag + "d_w_gate")
        d_hn2 = _matmul(dgpre, full['w_ple_gate'], 'nt', BF16, tag + "d_hn2", b_layer=i)
        dh1, dh1_b, g_ple = _norm_bwd(sv['h1'], ple_norm[i][None, :], d_hn2, dh, tag + "norm2")
        grads['ple_norm'][i] = g_ple[0]
        grads['w_out'][i] = _matmul(sv['y'], dh1_b, 'tn', F32, tag + "d_w_out")
        dy = _matmul(dh1_b, full['w_out'], 'nt', BF16, tag + "d_y", b_layer=i)
        ws_t = jnp.swapaxes(sv['ws_b'], 1, 2)
        d_a, g_sgu, g_ws, g_bs, g_ga = _sgu_bwd(proj, dm.off['a'], dm.aw, sgu_norm[i], sv['ws_b'], ws_t, sv['bb'],
                                                ga, dy, tag + "sgu")
        grads['sgu_norm'][i], grads['w_spatial'][i], grads['b_spatial'][i] = g_sgu, g_ws, g_bs[:, :, 0]
        dyc, d_bb, d_bz, g_gb, g_cb = _conv_bwd_gate(proj, dm.off['b'], dm.bw, sv['yconv'], gb, dy, tag + "conv_gate")
        d_b, g_cw = _conv_bwd_taps(proj, dm.off['b'], dm.bw, dyc, conv_w_full[i], d_bb, d_bz, tag + "conv_taps")
        grads['conv_b'][i], grads['conv_w'][i] = g_cb[0], g_cw
        d_o, d_cz, dsum, g_gc = _attn_post_bwd(sv['o'], proj, dm.off['cz'], dm.cw, gc, dy,
                                               _col_block(dm.aw + dm.bw, dm.cw), tag + "attn_post")
        grads['out_norm'][i] = jnp.concatenate([g_ga[0], g_gb[0], g_gc[0]])
        dq_t, dk_cat, dv = _attn_bwd(sv['q_cat'], sv['k_cat'], sv['k_cat'].T, sv['v'], d_o,
                                     sv['lse'].reshape(dm.ch, 1, t), dsum.reshape(dm.ch, 1, t), dm.ch, scale,
                                     tag + "attn_bwd")
        dq_cat = dq_t.T.astype(BF16)
        dkv, dkr_rot, g_kn = _kv_prep_bwd(sv['kv'], dm.ch, kn_g, dk_cat, dv, tag + "kv_prep")
        grads['k_nope_norm'][i] = g_kn[0]
        grads['w_ukv'][i] = _matmul(sv['ckv_n'], dkv, 'tn', F32, tag + "d_w_ukv")
        dckv_n = _matmul(dkv, full['w_ukv'], 'nt', BF16, tag + "d_ckv", b_layer=i)
        d_q, d_ckv, d_kr, g_qn, g_qr, g_kr, g_kv = _mla_prep_bwd(
            proj, dm.off['q'], dm.off['ckv'], dm.off['kr'], dm.ch, dm.kvr, tabs, qn_g, qr_g, kr_g, kv_g,
            dq_cat, scale, dckv_n, dkr_rot, dm.inp - dm.off['kr'], tag + "mla_prep")
        grads['q_nope_norm'][i], grads['q_rope_norm'][i] = g_qn[0], g_qr[0, :ROPE]
        grads['k_rope_norm'][i], grads['kv_norm'][i] = g_kr[0, :ROPE], g_kv[0]
        dproj = _assemble_dproj(dict(a=d_a, b=d_b, ckv=d_ckv, q=d_q, cz=d_cz, kr=d_kr), dm, t)
        grads['w_in'][i] = _unarrange_w_in(_matmul(sv['hn'], dproj, 'tn', F32, tag + "d_w_in"), dm)
        d_hn = _matmul(dproj, w_in_r, 'nt', BF16, tag + "d_hn", b_layer=i)
        dh, _, g_an = _norm_bwd(sv['h'], attn_norm[i][None, :], d_hn, dh1, tag + "norm1")
        grads['attn_norm'][i] = g_an[0]
    grad_x = dh[None]
    grads = {n: jnp.stack(grads[n]) for n in WEIGHTS}

    sm = [_shard_major(grads[n], BIG_AXIS[n]) for n in BIG]
    from_sibling = _to_sibling(sm, True, "grads_to_sibling")
    mine = [lax.dynamic_index_in_dim(g, core, 0, keepdims=False) for g in sm]
    chip_sums = [_pair_sum_bf16(a, b, f"chip_sum_{n}") for a, b, n in zip(mine, from_sibling, BIG)]
    from_chips = _to_owner_chips(chip_sums, "grads_to_owner_chips")
    halves = []
    for a, b, r3, n in zip(mine, from_sibling, from_chips, BIG):
        own_a = lax.dynamic_index_in_dim(a, shard, 0, keepdims=False)
        own_b = lax.dynamic_index_in_dim(b, shard, 0, keepdims=False)
        halves.append(_shard_sum(own_a, own_b, r3, f"shard_sum_{n}"))
    sibling_halves = _to_sibling(halves, False, "grads_share_sibling")
    core_flag = core.astype(F32).reshape(1, 1)
    out_g, out_d, out_m, out_v = {}, {}, {}, {}
    for n, own, recv in zip(BIG, halves, sibling_halves):
        out_g[n], out_d[n], out_m[n], out_v[n] = _adamw_two_halves(weights[n], own, recv, core_flag, mom_m[n],
                                                                   mom_v[n], f"adamw_{n}")

    shapes = [grads[n].shape for n in SMALL]
    summed = _unpack(_sum_devices(_gather_devices(_pack([grads[n] for n in SMALL]), "gather_small_grads"),
                                  "sum_small_grads"), shapes)
    small_g = dict(zip(SMALL, summed))
    small_g['conv_w'] = lax.dynamic_slice_in_dim(small_g['conv_w'], shard * conv_w.shape[2], conv_w.shape[2], axis=2)
    local_shapes = [weights[n].shape for n in SMALL]
    d_s, m_s, v_s = _adamw(_pack([weights[n] for n in SMALL]), _pack([small_g[n] for n in SMALL]),
                           _pack([mom_m[n] for n in SMALL]), _pack([mom_v[n] for n in SMALL]), "adamw_small")
    for n, dd, mm, vv in zip(SMALL, _unpack(d_s, local_shapes), _unpack(m_s, local_shapes),
                             _unpack(v_s, local_shapes)):
        out_g[n], out_d[n], out_m[n], out_v[n] = small_g[n], dd, mm, vv

    return (loss, grad_x, *[out_g[n] for n in WEIGHTS], *[out_d[n] for n in WEIGHTS],
            *[out_m[n] for n in WEIGHTS], *[out_v[n] for n in WEIGHTS])
```

```python
import functools

import jax
import jax.numpy as jnp
from jax import lax
from jax.experimental import pallas as pl
from jax.experimental.pallas import tpu as pltpu

F32 = jnp.float32
BF16 = jnp.bfloat16
EPS = 1e-6
HEAD = 128
ROPE = 64
ROPE_BASE = 10000.0
CONV_TAPS = 3
N_SHARD = 4
N_DEV = 8
ADAM_LR = 0.001
ADAM_B1 = 0.9
ADAM_B2 = 0.999
ADAM_EPS = 1e-08
ADAM_WD = 0.01
ADAM_STEP = 10
MESH = pl.DeviceIdType.MESH
VMEM_LIMIT = 56 * 1024 * 1024
HALO_ROWS = 16

WEIGHTS = ['attn_norm', 'w_in', 'sgu_norm', 'w_spatial', 'b_spatial', 'conv_w', 'conv_b', 'kv_norm', 'w_ukv',
           'q_nope_norm', 'q_rope_norm', 'k_nope_norm', 'k_rope_norm', 'out_norm', 'w_out', 'ple_norm',
           'w_ple_gate', 'w_ple_proj']
BIG = ['w_in', 'w_ukv', 'w_out', 'w_ple_gate', 'w_ple_proj']
BIG_AXIS = {'w_in': 2, 'w_ukv': 2, 'w_out': 1, 'w_ple_gate': 1, 'w_ple_proj': 2}
SMALL = [n for n in WEIGHTS if n not in BIG]


def _pick(n, cands):
    for c in cands:
        if n % c == 0:
            return c
    return n


def _params(sem=None):
    return pltpu.CompilerParams(dimension_semantics=sem, vmem_limit_bytes=VMEM_LIMIT)


def _matmul(a, b, mode, out_dtype, name, add=None, b_layer=None, rider=None):
    b_shape = b.shape if b_layer is None else b.shape[1:]
    if mode == 'nn':
        (m, k), n = a.shape, b_shape[1]
    elif mode == 'nt':
        (m, k), n = a.shape, b_shape[0]
    else:
        (k, m), n = a.shape, b_shape[1]
    tm = _pick(m, (1024, 512, 256, 128))
    tn = _pick(n, (1536, 1024, 512, 256, 128))
    tk = k if k <= 2048 else _pick(k, (2048, 1536, 1024, 512, 256, 128))
    nk = k // tk
    if mode == 'tn':
        a_spec = pl.BlockSpec((tk, tm), lambda i, j, kk: (kk, i))
        dims = (((0,), (0,)), ((), ()))
    else:
        a_spec = pl.BlockSpec((tm, tk), lambda i, j, kk: (i, kk))
        dims = (((1,), (0,)), ((), ())) if mode == 'nn' else (((1,), (1,)), ((), ()))
    b_block = (tn, tk) if mode == 'nt' else (tk, tn)
    if b_layer is None:
        b_spec = pl.BlockSpec(b_block, (lambda i, j, kk: (j, kk)) if mode == 'nt' else (lambda i, j, kk: (kk, j)))
    else:
        b_spec = pl.BlockSpec((None,) + b_block, (lambda i, j, kk: (b_layer, j, kk)) if mode == 'nt'
                              else (lambda i, j, kk: (b_layer, kk, j)))
    o_spec = pl.BlockSpec((tm, tn), lambda i, j, kk: (i, j))
    has_add = add is not None

    def body(*refs):
        a_ref, b_ref = refs[0], refs[1]
        add_ref = refs[2] if has_add else None
        o_ref = refs[3] if has_add else refs[2]

        def product():
            return lax.dot_general(a_ref[...], b_ref[...], dims, preferred_element_type=F32)

        def finish(res):
            if has_add:
                res = res + add_ref[...]
            o_ref[...] = res.astype(out_dtype)

        if nk == 1:
            finish(product())
        else:
            acc_ref = refs[-1]
            kk = pl.program_id(2)

            @pl.when(kk == 0)
            def _():
                acc_ref[...] = product()

            @pl.when((kk > 0) & (kk < nk - 1))
            def _():
                acc_ref[...] += product()

            @pl.when(kk == nk - 1)
            def _():
                finish(acc_ref[...] + product())

    in_specs = [a_spec, b_spec] + ([o_spec] if has_add else [])
    args = [a, b] + ([add] if has_add else [])
    grid = (m // tm, n // tn, nk)
    scratch = [pltpu.VMEM((tm, tn), F32)] if nk > 1 else []
    if rider is None:
        return pl.pallas_call(
            body, name=name, grid=grid, in_specs=in_specs, out_specs=o_spec,
            out_shape=jax.ShapeDtypeStruct((m, n), out_dtype), scratch_shapes=scratch,
            compiler_params=_params(("parallel", "parallel", "arbitrary")),
        )(*args)

    n_in, n_rin, n_rout = len(args), len(rider.arrays), len(rider.out_shapes)

    def body_with_rider(*refs):
        r_in = refs[n_in:n_in + n_rin]
        r_out = refs[n_in + n_rin + 1:n_in + n_rin + 1 + n_rout]
        own = refs[:n_in] + refs[n_in + n_rin:n_in + n_rin + 1] + refs[n_in + n_rin + 1 + n_rout:len(refs) - 2]
        send_sems, recv_sems = refs[-2:]
        ids = [pl.program_id(ax) for ax in range(3)]

        @pl.when((ids[0] == 0) & (ids[1] == 0) & (ids[2] == 0))
        def _():
            rider.start(r_in, r_out, send_sems, recv_sems)

        body(*own)

        @pl.when((ids[0] == grid[0] - 1) & (ids[1] == grid[1] - 1) & (ids[2] == grid[2] - 1))
        def _():
            rider.finish(r_in, r_out, send_sems, recv_sems)

    res = pl.pallas_call(
        body_with_rider, name=name, grid=grid, in_specs=in_specs + [_ANY] * n_rin,
        out_specs=[o_spec] + [_ANY] * n_rout,
        out_shape=[jax.ShapeDtypeStruct((m, n), out_dtype)] + rider.out_shapes,
        scratch_shapes=scratch + rider.sems(), input_output_aliases=rider.aliases(n_in, 1),
        compiler_params=_params(("arbitrary", "arbitrary", "arbitrary")),
    )(*args, *rider.arrays)
    return res[0], res[1:]


def _rms(x, n):
    r = lax.rsqrt(jnp.sum(x * x, axis=-1, keepdims=True) * (1.0 / n) + EPS)
    return x * r, r


def _rms_bwd(dxhat, xhat, r, n):
    return r * (dxhat - xhat * (jnp.sum(dxhat * xhat, axis=-1, keepdims=True) * (1.0 / n)))


def _sigmoid(z):
    return 1.0 / (1.0 + jnp.exp(-z))


def _silu_and_grad(z):
    sig = _sigmoid(z)
    return z * sig, sig * (1.0 + z * (1.0 - sig))


def _colsum(x):
    return jnp.sum(x, axis=0, keepdims=True)


def _rope(t, cos_t, sin_a, sin_b):
    return t * cos_t + pltpu.roll(t, 96, 1) * sin_a + pltpu.roll(t, 32, 1) * sin_b


def _rope_bwd(d, cos_t, sin_a, sin_b):
    return d * cos_t + pltpu.roll(d * sin_a, 32, 1) + pltpu.roll(d * sin_b, 96, 1)


def _shift_down(g, first_row):
    row = lax.broadcasted_iota(jnp.int32, g.shape, 0)
    return jnp.where(row == 0, first_row, pltpu.roll(g, 1, 0))


def _shift_up(g, last_row):
    n = g.shape[0]
    row = lax.broadcasted_iota(jnp.int32, g.shape, 0)
    return jnp.where(row == n - 1, last_row, pltpu.roll(g, n - 1, 0))


def _row_spec(r, w, col=0):
    return pl.BlockSpec((r, w), lambda i: (i, col))


def _const_spec(shape):
    nd = len(shape)
    return pl.BlockSpec(shape, lambda i: (0,) * nd)


def _col_block(off, w):
    assert off % w == 0, (off, w)
    return off // w


def _zero_at_first_step(refs):
    @pl.when(pl.program_id(0) == 0)
    def _():
        for ref in refs:
            ref[...] = jnp.zeros(ref.shape, ref.dtype)


def _row_call(body, name, t, r, in_specs, args, out_specs, out_shapes, scratch=(), into=None):
    if into is None:
        return pl.pallas_call(
            body, name=name, grid=(t // r,), in_specs=in_specs, out_specs=out_specs, out_shape=out_shapes,
            scratch_shapes=list(scratch), compiler_params=_params(("arbitrary",)),
        )(*args)
    buf, col = into
    single = not isinstance(out_specs, (list, tuple))
    specs = [out_specs] if single else list(out_specs)
    shapes = [out_shapes] if single else list(out_shapes)
    width = shapes[0].shape[1]
    assert shapes[0].dtype == buf.dtype and buf.shape[0] == t
    specs[0] = _row_spec(r, width, col)
    shapes[0] = jax.ShapeDtypeStruct(buf.shape, buf.dtype)
    n_in = len(args)

    def body_in_place(*refs):
        body(*refs[:n_in], *refs[n_in + 1:])

    res = pl.pallas_call(
        body_in_place, name=name, grid=(t // r,), in_specs=list(in_specs) + [_ANY], out_specs=specs, out_shape=shapes,
        scratch_shapes=list(scratch), input_output_aliases={n_in: 0}, compiler_params=_params(("arbitrary",)),
    )(*args, buf)
    return res[0] if single else res


def _norm_fwd(h, g, name):
    t, d = h.shape
    r = _pick(t, (256, 128))

    def body(h_ref, g_ref, o_ref):
        xhat, _ = _rms(h_ref[...], d)
        o_ref[...] = (xhat * g_ref[...]).astype(BF16)

    return _row_call(body, name, t, r, [_row_spec(r, d), _const_spec((1, d))], (h, g),
                     _row_spec(r, d), jax.ShapeDtypeStruct((t, d), BF16))


def _norm_bwd(h, g, d_hn, d_res, name):
    t, d = h.shape
    r = _pick(t, (256, 128))

    def body(h_ref, g_ref, dy_ref, dres_ref, dh_ref, dhb_ref, dg_ref):
        _zero_at_first_step([dg_ref])
        xhat, rr = _rms(h_ref[...], d)
        dy = dy_ref[...].astype(F32)
        dg_ref[...] += _colsum(dy * xhat)
        dh = dres_ref[...] + _rms_bwd(dy * g_ref[...], xhat, rr, d)
        dh_ref[...] = dh
        dhb_ref[...] = dh.astype(BF16)

    return _row_call(body, name, t, r,
                     [_row_spec(r, d), _const_spec((1, d)), _row_spec(r, d), _row_spec(r, d)], (h, g, d_hn, d_res),
                     [_row_spec(r, d), _row_spec(r, d), _const_spec((1, d))],
                     [jax.ShapeDtypeStruct((t, d), F32), jax.ShapeDtypeStruct((t, d), BF16),
                      jax.ShapeDtypeStruct((1, d), F32)])


def _sgu_scores(v, gs_ref, ws_ref, bb_ref, s_scr, r, ah, keep=None):
    for kk in range(r // HEAD):
        for hh in range(ah):
            rows, cols = slice(kk * HEAD, (kk + 1) * HEAD), slice(hh * HEAD, (hh + 1) * HEAD)
            vhat, rv = _rms(v[rows, cols], HEAD)
            vn = vhat * gs_ref[pl.ds(hh, 1), :]
            s_scr[rows, cols] = jnp.dot(ws_ref[hh], vn.astype(BF16), preferred_element_type=F32) + bb_ref[hh]
            if keep is not None:
                keep[(kk, hh)] = (vhat, rv, vn)


def _sgu_fwd(proj, off, aw, gs, ws, bb, ga, name, into=None):
    t = proj.shape[0]
    ah = aw // HEAD
    r = _pick(t, (256, 128))
    cb = _col_block(off, aw)

    def body(u_ref, v_ref, z_ref, gs_ref, ws_ref, bb_ref, ga_ref, o_ref, s_scr):
        _sgu_scores(v_ref[...].astype(F32), gs_ref, ws_ref, bb_ref, s_scr, r, ah)
        sil, _ = _silu_and_grad(z_ref[...].astype(F32))
        yhat, _ = _rms(u_ref[...].astype(F32) * s_scr[...] * sil, aw)
        o_ref[...] = (yhat * ga_ref[...]).astype(BF16)

    return _row_call(
        body, name, t, r,
        [_row_spec(r, aw, cb), _row_spec(r, aw, cb + 1), _row_spec(r, aw, cb + 2), _const_spec((ah, HEAD)),
         _const_spec((ah, HEAD, HEAD)), _const_spec((ah, HEAD, HEAD)), _const_spec((1, aw))],
        (proj, proj, proj, gs, ws, bb, ga),
        _row_spec(r, aw), jax.ShapeDtypeStruct((t, aw), BF16), scratch=[pltpu.VMEM((r, aw), F32)], into=into)


def _sgu_bwd(proj, off, aw, gs, ws, ws_t, bb, ga, dy, name):
    t = proj.shape[0]
    ah = aw // HEAD
    r = _pick(t, (256, 128))
    cb = _col_block(off, aw)

    def body(u_ref, v_ref, z_ref, gs_ref, ws_ref, wst_ref, bb_ref, ga_ref, dy_ref,
             d_ref, dgs_ref, dws_ref, db_ref, dga_ref, s_scr, dv_scr):
        _zero_at_first_step([dgs_ref, dws_ref, db_ref, dga_ref])
        keep = {}
        _sgu_scores(v_ref[...].astype(F32), gs_ref, ws_ref, bb_ref, s_scr, r, ah, keep)
        u, z, s = u_ref[...].astype(F32), z_ref[...].astype(F32), s_scr[...]
        sil, dsil = _silu_and_grad(z)
        yhat, rr = _rms(u * s * sil, aw)
        dy_f = dy_ref[...].astype(F32)
        dga_ref[...] += _colsum(dy_f * yhat)
        dya = _rms_bwd(dy_f * ga_ref[...], yhat, rr, aw)
        d_ref[:, 0:aw] = (dya * s * sil).astype(BF16)
        d_ref[:, 2 * aw:3 * aw] = (dya * u * s * dsil).astype(BF16)
        ds = dya * u * sil
        for kk in range(r // HEAD):
            for hh in range(ah):
                rows, cols = slice(kk * HEAD, (kk + 1) * HEAD), slice(hh * HEAD, (hh + 1) * HEAD)
                vhat, rv, vn = keep[(kk, hh)]
                ds_blk = ds[rows, cols]
                db_ref[hh] += jnp.sum(ds_blk, axis=1, keepdims=True)
                ds_b = ds_blk.astype(BF16)
                dws_ref[hh] += lax.dot_general(ds_b, vn.astype(BF16), (((1,), (1,)), ((), ())),
                                               preferred_element_type=F32)
                dvn = jnp.dot(wst_ref[hh], ds_b, preferred_element_type=F32)
                dgs_ref[pl.ds(hh, 1), :] += _colsum(dvn * vhat)
                dv_scr[rows, cols] = _rms_bwd(dvn * gs_ref[pl.ds(hh, 1), :], vhat, rv, HEAD)
        d_ref[:, aw:2 * aw] = dv_scr[...].astype(BF16)

    return _row_call(
        body, name, t, r,
        [_row_spec(r, aw, cb), _row_spec(r, aw, cb + 1), _row_spec(r, aw, cb + 2), _const_spec((ah, HEAD)),
         _const_spec((ah, HEAD, HEAD)), _const_spec((ah, HEAD, HEAD)), _const_spec((ah, HEAD, HEAD)),
         _const_spec((1, aw)), _row_spec(r, aw, 0)],
        (proj, proj, proj, gs, ws, ws_t, bb, ga, dy),
        [_row_spec(r, 3 * aw), _const_spec((ah, HEAD)), _const_spec((ah, HEAD, HEAD)), _const_spec((ah, HEAD, 1)),
         _const_spec((1, aw))],
        [jax.ShapeDtypeStruct((t, 3 * aw), BF16), jax.ShapeDtypeStruct((ah, HEAD), F32),
         jax.ShapeDtypeStruct((ah, HEAD, HEAD), F32), jax.ShapeDtypeStruct((ah, HEAD, 1), F32),
         jax.ShapeDtypeStruct((1, aw), F32)],
        scratch=[pltpu.VMEM((r, aw), F32), pltpu.VMEM((r, aw), F32)])


def _halo_specs(t, r, w, col, rows):
    per = r // rows
    last = t // rows - 1
    prev = pl.BlockSpec((rows, w), lambda i: (jnp.maximum(i * per - 1, 0), col))
    nxt = pl.BlockSpec((rows, w), lambda i: (jnp.minimum((i + 1) * per, last), col))
    return prev, nxt


def _edge_rows(prev_ref, next_ref, n_steps):
    i = pl.program_id(0)
    rows = prev_ref.shape[0]
    before = prev_ref[...].astype(F32)[rows - 1:rows, :] * (i > 0).astype(F32)
    after = next_ref[...].astype(F32)[0:1, :] * (i < n_steps - 1).astype(F32)
    return before, after


def _conv_fwd(proj, off, bw, cw, cb_, gb, name, into=None):
    t = proj.shape[0]
    r = _pick(t, (256, 128))
    n_steps = t // r
    c0 = _col_block(off, bw)
    cp, cn = _halo_specs(t, r, bw, c0 + 1, HALO_ROWS)
    hp, hn = _halo_specs(t, r, bw, c0 + 2, HALO_ROWS)

    def body(b_ref, c_ref, h_ref, z_ref, cp_ref, cn_ref, hp_ref, hn_ref, cw_ref, cb_ref, gb_ref, o_ref, yc_ref):
        g = c_ref[...].astype(F32) * h_ref[...].astype(F32)
        c_before, c_after = _edge_rows(cp_ref, cn_ref, n_steps)
        h_before, h_after = _edge_rows(hp_ref, hn_ref, n_steps)
        yconv = (cb_ref[...] + cw_ref[0:1, :] * _shift_down(g, c_before * h_before) + cw_ref[1:2, :] * g
                 + cw_ref[2:3, :] * _shift_up(g, c_after * h_after))
        yc_ref[...] = yconv
        sil, _ = _silu_and_grad(z_ref[...].astype(F32))
        yhat, _ = _rms(b_ref[...].astype(F32) * yconv * sil, bw)
        o_ref[...] = (yhat * gb_ref[...]).astype(BF16)

    return _row_call(
        body, name, t, r,
        [_row_spec(r, bw, c0), _row_spec(r, bw, c0 + 1), _row_spec(r, bw, c0 + 2), _row_spec(r, bw, c0 + 3),
         cp, cn, hp, hn, _const_spec((CONV_TAPS, bw)), _const_spec((1, bw)), _const_spec((1, bw))],
        (proj, proj, proj, proj, proj, proj, proj, proj, cw, cb_, gb),
        [_row_spec(r, bw), _row_spec(r, bw)],
        [jax.ShapeDtypeStruct((t, bw), BF16), jax.ShapeDtypeStruct((t, bw), F32)], into=into)


def _conv_bwd_gate(proj, off, bw, yconv, gb, dy, name):
    t = proj.shape[0]
    r = _pick(t, (256, 128))
    c0 = _col_block(off, bw)

    def body(b_ref, z_ref, yc_ref, gb_ref, dy_ref, dyc_ref, db_ref, dz_ref, dgb_ref, dcb_ref):
        _zero_at_first_step([dgb_ref, dcb_ref])
        b, z, yconv_v = b_ref[...].astype(F32), z_ref[...].astype(F32), yc_ref[...]
        sil, dsil = _silu_and_grad(z)
        yhat, rr = _rms(b * yconv_v * sil, bw)
        dy_f = dy_ref[...].astype(F32)
        dgb_ref[...] += _colsum(dy_f * yhat)
        dyb = _rms_bwd(dy_f * gb_ref[...], yhat, rr, bw)
        dyc = dyb * b * sil
        dyc_ref[...] = dyc
        dcb_ref[...] += _colsum(dyc)
        db_ref[...] = (dyb * yconv_v * sil).astype(BF16)
        dz_ref[...] = (dyb * b * yconv_v * dsil).astype(BF16)

    return _row_call(
        body, name, t, r,
        [_row_spec(r, bw, c0), _row_spec(r, bw, c0 + 3), _row_spec(r, bw), _const_spec((1, bw)), _row_spec(r, bw, 1)],
        (proj, proj, yconv, gb, dy),
        [_row_spec(r, bw), _row_spec(r, bw), _row_spec(r, bw), _const_spec((1, bw)), _const_spec((1, bw))],
        [jax.ShapeDtypeStruct((t, bw), F32), jax.ShapeDtypeStruct((t, bw), BF16), jax.ShapeDtypeStruct((t, bw), BF16),
         jax.ShapeDtypeStruct((1, bw), F32), jax.ShapeDtypeStruct((1, bw), F32)])


def _conv_bwd_taps(proj, off, bw, dyc, cw, d_gate_b, d_gate_z, name):
    t = proj.shape[0]
    r = _pick(t, (256, 128))
    n_steps = t // r
    c0 = _col_block(off, bw)
    cp, cn = _halo_specs(t, r, bw, c0 + 1, HALO_ROWS)
    hp, hn = _halo_specs(t, r, bw, c0 + 2, HALO_ROWS)
    dp, dn = _halo_specs(t, r, bw, 0, 8)

    def body(c_ref, h_ref, cp_ref, cn_ref, hp_ref, hn_ref, d_ref, dp_ref, dn_ref, cw_ref, dgb_ref, dgz_ref,
             db_ref, dcw_ref):
        _zero_at_first_step([dcw_ref])
        c, h, d = c_ref[...].astype(F32), h_ref[...].astype(F32), d_ref[...]
        g = c * h
        c_before, c_after = _edge_rows(cp_ref, cn_ref, n_steps)
        h_before, h_after = _edge_rows(hp_ref, hn_ref, n_steps)
        d_before, d_after = _edge_rows(dp_ref, dn_ref, n_steps)
        dg = (cw_ref[0:1, :] * _shift_up(d, d_after) + cw_ref[1:2, :] * d + cw_ref[2:3, :] * _shift_down(d, d_before))
        db_ref[:, 0:bw] = dgb_ref[...]
        db_ref[:, bw:2 * bw] = (dg * h).astype(BF16)
        db_ref[:, 2 * bw:3 * bw] = (dg * c).astype(BF16)
        db_ref[:, 3 * bw:4 * bw] = dgz_ref[...]
        dcw_ref[0:1, :] += _colsum(d * _shift_down(g, c_before * h_before))
        dcw_ref[1:2, :] += _colsum(d * g)
        dcw_ref[2:3, :] += _colsum(d * _shift_up(g, c_after * h_after))

    return _row_call(
        body, name, t, r,
        [_row_spec(r, bw, c0 + 1), _row_spec(r, bw, c0 + 2), cp, cn, hp, hn, _row_spec(r, bw), dp, dn,
         _const_spec((CONV_TAPS, bw)), _row_spec(r, bw), _row_spec(r, bw)],
        (proj, proj, proj, proj, proj, proj, dyc, dyc, dyc, cw, d_gate_b, d_gate_z),
        [_row_spec(r, 4 * bw), _const_spec((CONV_TAPS, bw))],
        [jax.ShapeDtypeStruct((t, 4 * bw), BF16), jax.ShapeDtypeStruct((CONV_TAPS, bw), F32)])


def _mla_prep_fwd(proj, q_off, ckv_off, kr_off, ch, kvr, tabs, qn_g, qr_g, kr_g, kv_g, name):
    t = proj.shape[0]
    r = _pick(t, (256, 128))
    qw = ch * 2 * HEAD
    cos_t, sin_a, sin_b = tabs

    def body(q_ref, ckv_ref, kr_ref, cos_ref, sa_ref, sb_ref, qn_ref, qr_ref, krg_ref, kvg_ref,
             qo_ref, co_ref, ko_ref):
        cos_v, sa, sb = cos_ref[...], sa_ref[...], sb_ref[...]
        for hh in range(ch):
            lo = hh * 2 * HEAD
            nhat, _ = _rms(q_ref[:, lo:lo + HEAD].astype(F32), HEAD)
            qo_ref[:, lo:lo + HEAD] = (nhat * qn_ref[...]).astype(BF16)
            rhat, _ = _rms(q_ref[:, lo + HEAD:lo + 2 * HEAD].astype(F32), ROPE)
            qo_ref[:, lo + HEAD:lo + 2 * HEAD] = _rope(rhat * qr_ref[...], cos_v, sa, sb).astype(BF16)
        khat, _ = _rms(kr_ref[...].astype(F32), ROPE)
        ko_ref[...] = _rope(khat * krg_ref[...], cos_v, sa, sb).astype(BF16)
        chat, _ = _rms(ckv_ref[...].astype(F32), kvr)
        co_ref[...] = (chat * kvg_ref[...]).astype(BF16)

    tab = _row_spec(r, HEAD)
    gain = _const_spec((1, HEAD))
    return _row_call(
        body, name, t, r,
        [_row_spec(r, qw, _col_block(q_off, qw)), _row_spec(r, kvr, _col_block(ckv_off, kvr)),
         _row_spec(r, HEAD, _col_block(kr_off, HEAD)), tab, tab, tab, gain, gain, gain, _const_spec((1, kvr))],
        (proj, proj, proj, cos_t, sin_a, sin_b, qn_g, qr_g, kr_g, kv_g),
        [_row_spec(r, qw), _row_spec(r, kvr), _row_spec(r, HEAD)],
        [jax.ShapeDtypeStruct((t, qw), BF16), jax.ShapeDtypeStruct((t, kvr), BF16),
         jax.ShapeDtypeStruct((t, HEAD), BF16)])


def _mla_prep_bwd(proj, q_off, ckv_off, kr_off, ch, kvr, tabs, qn_g, qr_g, kr_g, kv_g, dq_cat, dq_scale, dckv_n, dkr_rot,
                  kr_width, name):
    t = proj.shape[0]
    r = _pick(t, (256, 128))
    qw = ch * 2 * HEAD
    cos_t, sin_a, sin_b = tabs

    def body(q_ref, ckv_ref, kr_ref, cos_ref, sa_ref, sb_ref, qn_ref, qr_ref, krg_ref, kvg_ref,
             dq_ref, dc_ref, dk_ref, dqo_ref, dco_ref, dko_ref, dqn_ref, dqr_ref, dkrg_ref, dkvg_ref):
        _zero_at_first_step([dqn_ref, dqr_ref, dkrg_ref, dkvg_ref])
        cos_v, sa, sb = cos_ref[...], sa_ref[...], sb_ref[...]
        for hh in range(ch):
            lo = hh * 2 * HEAD
            nhat, nr = _rms(q_ref[:, lo:lo + HEAD].astype(F32), HEAD)
            d_n = dq_ref[:, lo:lo + HEAD].astype(F32) * dq_scale
            dqn_ref[...] += _colsum(d_n * nhat)
            dqo_ref[:, lo:lo + HEAD] = _rms_bwd(d_n * qn_ref[...], nhat, nr, HEAD).astype(BF16)
            rhat, rr = _rms(q_ref[:, lo + HEAD:lo + 2 * HEAD].astype(F32), ROPE)
            d_t = _rope_bwd(dq_ref[:, lo + HEAD:lo + 2 * HEAD].astype(F32) * dq_scale, cos_v, sa, sb)
            dqr_ref[...] += _colsum(d_t * rhat)
            dqo_ref[:, lo + HEAD:lo + 2 * HEAD] = _rms_bwd(d_t * qr_ref[...], rhat, rr, ROPE).astype(BF16)
        khat, kr_r = _rms(kr_ref[...].astype(F32), ROPE)
        d_k = _rope_bwd(dk_ref[...], cos_v, sa, sb)
        dkrg_ref[...] += _colsum(d_k * khat)
        dko_ref[:, 0:HEAD] = _rms_bwd(d_k * krg_ref[...], khat, kr_r, ROPE).astype(BF16)
        if kr_width > HEAD:
            dko_ref[:, HEAD:kr_width] = jnp.zeros((r, kr_width - HEAD), BF16)
        chat, cr = _rms(ckv_ref[...].astype(F32), kvr)
        d_c = dc_ref[...].astype(F32)
        dkvg_ref[...] += _colsum(d_c * chat)
        dco_ref[...] = _rms_bwd(d_c * kvg_ref[...], chat, cr, kvr).astype(BF16)

    tab = _row_spec(r, HEAD)
    gain = _const_spec((1, HEAD))
    return _row_call(
        body, name, t, r,
        [_row_spec(r, qw, _col_block(q_off, qw)), _row_spec(r, kvr, _col_block(ckv_off, kvr)),
         _row_spec(r, HEAD, _col_block(kr_off, HEAD)), tab, tab, tab, gain, gain, gain, _const_spec((1, kvr)),
         _row_spec(r, qw), _row_spec(r, kvr), _row_spec(r, HEAD)],
        (proj, proj, proj, cos_t, sin_a, sin_b, qn_g, qr_g, kr_g, kv_g, dq_cat, dckv_n, dkr_rot),
        [_row_spec(r, qw), _row_spec(r, kvr), _row_spec(r, kr_width), gain, gain, gain, _const_spec((1, kvr))],
        [jax.ShapeDtypeStruct((t, qw), BF16), jax.ShapeDtypeStruct((t, kvr), BF16),
         jax.ShapeDtypeStruct((t, kr_width), BF16), jax.ShapeDtypeStruct((1, HEAD), F32),
         jax.ShapeDtypeStruct((1, HEAD), F32), jax.ShapeDtypeStruct((1, HEAD), F32),
         jax.ShapeDtypeStruct((1, kvr), F32)])


def _kv_prep_fwd(kv, kr_rot, ch, kn_g, name):
    t = kv.shape[0]
    r = _pick(t, (256, 128))
    qw = ch * 2 * HEAD

    def body(kv_ref, kr_ref, kn_ref, ko_ref, vo_ref):
        ones = jnp.ones((r, HEAD), BF16)
        for hh in range(ch):
            lo = hh * 2 * HEAD
            nhat, _ = _rms(kv_ref[:, lo:lo + HEAD], HEAD)
            ko_ref[:, lo:lo + HEAD] = (nhat * kn_ref[...]).astype(BF16)
            ko_ref[:, lo + HEAD:lo + 2 * HEAD] = kr_ref[...]
            vo_ref[:, lo:lo + HEAD] = kv_ref[:, lo + HEAD:lo + 2 * HEAD].astype(BF16)
            vo_ref[:, lo + HEAD:lo + 2 * HEAD] = ones

    return _row_call(
        body, name, t, r, [_row_spec(r, qw), _row_spec(r, HEAD), _const_spec((1, HEAD))], (kv, kr_rot, kn_g),
        [_row_spec(r, qw), _row_spec(r, qw)],
        [jax.ShapeDtypeStruct((t, qw), BF16), jax.ShapeDtypeStruct((t, qw), BF16)])


def _kv_prep_bwd(kv, ch, kn_g, dk_cat, dv, name):
    t = kv.shape[0]
    r = _pick(t, (256, 128))
    qw = ch * 2 * HEAD

    def body(kv_ref, kn_ref, dk_ref, dv_ref, dkv_ref, dkr_ref, dkn_ref):
        _zero_at_first_step([dkn_ref])
        dkr = jnp.zeros((r, HEAD), F32)
        for hh in range(ch):
            lo = hh * 2 * HEAD
            nhat, nr = _rms(kv_ref[:, lo:lo + HEAD], HEAD)
            d_n = dk_ref[:, lo:lo + HEAD].astype(F32)
            dkn_ref[...] += _colsum(d_n * nhat)
            dkv_ref[:, lo:lo + HEAD] = _rms_bwd(d_n * kn_ref[...], nhat, nr, HEAD).astype(BF16)
            dkv_ref[:, lo + HEAD:lo + 2 * HEAD] = dv_ref[:, hh * HEAD:(hh + 1) * HEAD]
            dkr = dkr + dk_ref[:, lo + HEAD:lo + 2 * HEAD].astype(F32)
        dkr_ref[...] = dkr

    return _row_call(
        body, name, t, r,
        [_row_spec(r, qw), _const_spec((1, HEAD)), _row_spec(r, qw), _row_spec(r, ch * HEAD)], (kv, kn_g, dk_cat, dv),
        [_row_spec(r, qw), _row_spec(r, HEAD), _const_spec((1, HEAD))],
        [jax.ShapeDtypeStruct((t, qw), BF16), jax.ShapeDtypeStruct((t, HEAD), F32),
         jax.ShapeDtypeStruct((1, HEAD), F32)])


def _attn_tiles(t):
    return _pick(t, (2048, 1024, 512, 256, 128)), _pick(t, (1024, 512, 256, 128))


_NT = (((1,), (1,)), ((), ()))
LOG2E = 1.4426950408889634


def _attn_fwd(q_cat, k_cat, v_aug, ch, scale, name):
    t = q_cat.shape[0]
    tq, tk = _attn_tiles(t)
    nk = t // tk
    c2 = scale * LOG2E

    def body(q_ref, k_ref, v_ref, o_ref, lse_ref, s_scr, m_scr, acc_scr):
        j = pl.program_id(2)

        def scores(slot):
            s_scr[slot] = lax.dot_general(q_ref[...], k_ref[...], _NT, preferred_element_type=F32) * c2

        def absorb(slot):
            s = s_scr[slot]
            m_old = m_scr[...]
            m_new = jnp.maximum(m_old, jnp.max(s, axis=-1, keepdims=True))
            p = jnp.exp2(s - m_new).astype(BF16)
            acc_scr[...] = (jnp.exp2(m_old - m_new) * acc_scr[...]
                            + jnp.dot(p, v_ref[...], preferred_element_type=F32))
            m_scr[...] = m_new

        @pl.when(j == 0)
        def _():
            m_scr[...] = jnp.full(m_scr.shape, -jnp.inf, F32)
            acc_scr[...] = jnp.zeros(acc_scr.shape, F32)
            scores(0)

        for parity in (0, 1):
            @pl.when((j > 0) & (j < nk) & (j % 2 == parity))
            def _():
                scores(parity)
                absorb(1 - parity)

        @pl.when(j == nk)
        def _():
            absorb((nk - 1) % 2)
            acc = acc_scr[...]
            l_sum = acc[:, HEAD:]
            o_ref[...] = (acc[:, :HEAD] / l_sum).astype(BF16)
            lse_ref[0] = m_scr[...] + jnp.log(l_sum[:, 0:1]) * LOG2E

    return pl.pallas_call(
        body, name=name, grid=(ch, t // tq, nk + 1),
        in_specs=[pl.BlockSpec((tq, 2 * HEAD), lambda h, i, j: (i, h)),
                  pl.BlockSpec((tk, 2 * HEAD), lambda h, i, j: (jnp.minimum(j, nk - 1), h)),
                  pl.BlockSpec((tk, 2 * HEAD), lambda h, i, j: (jnp.maximum(j - 1, 0), h))],
        out_specs=[pl.BlockSpec((tq, HEAD), lambda h, i, j: (i, h)),
                   pl.BlockSpec((1, tq, 1), lambda h, i, j: (h, i, 0))],
        out_shape=[jax.ShapeDtypeStruct((t, ch * HEAD), BF16), jax.ShapeDtypeStruct((ch, t, 1), F32)],
        scratch_shapes=[pltpu.VMEM((2, tq, tk), F32), pltpu.VMEM((tq, 1), F32), pltpu.VMEM((tq, 2 * HEAD), F32)],
        compiler_params=_params(("parallel", "parallel", "arbitrary")),
    )(q_cat, k_cat, v_aug)


def _attn_bwd(q_cat, k_cat, k_cat_t, v_aug, do, lse_row, d_row, ch, scale, name):
    t = q_cat.shape[0]
    tk = _pick(t, (1024, 512, 256, 128))
    tq = _pick(t, (512, 256, 128))
    nk, nq = t // tk, t // tq
    c2 = scale * LOG2E

    def body(q_ref, do_ref, qp_ref, dop_ref, lse_ref, d_ref, k_ref, kt_ref, v_ref,
             dqt_ref, dk_ref, dv_ref, s_scr, dp_scr, dk_scr, dv_scr):
        ki, j = pl.program_id(1), pl.program_id(2)

        def products(slot):
            s_scr[slot] = lax.dot_general(k_ref[...], q_ref[...], _NT, preferred_element_type=F32) * c2
            dp_scr[slot] = lax.dot_general(v_ref[...], do_ref[...], _NT, preferred_element_type=F32)

        def absorb(slot):
            q, do_v = qp_ref[...], dop_ref[...]
            pt = jnp.exp2(s_scr[slot] - lse_ref[0])
            dv_scr[...] += jnp.dot(pt.astype(BF16), do_v, preferred_element_type=F32)
            dst = (pt * (dp_scr[slot] - d_ref[0])).astype(BF16)
            dk_scr[...] += jnp.dot(dst, q, preferred_element_type=F32)
            part = jnp.dot(kt_ref[...], dst, preferred_element_type=F32)
            cols = pl.ds(pl.multiple_of((j - 1) * tq, tq), tq)

            @pl.when(ki == 0)
            def _():
                dqt_ref[:, cols] = part

            @pl.when(ki > 0)
            def _():
                dqt_ref[:, cols] += part

        @pl.when(j == 0)
        def _():
            dk_scr[...] = jnp.zeros(dk_scr.shape, F32)
            dv_scr[...] = jnp.zeros(dv_scr.shape, F32)
            products(0)

        for parity in (0, 1):
            @pl.when((j > 0) & (j < nq) & (j % 2 == parity))
            def _():
                products(parity)
                absorb(1 - parity)

        @pl.when(j == nq)
        def _():
            absorb((nq - 1) % 2)
            dk_ref[...] = (dk_scr[...] * scale).astype(BF16)
            dv_ref[...] = dv_scr[...].astype(BF16)

    def cur(i):
        return jnp.minimum(i, nq - 1)

    def prev(i):
        return jnp.maximum(i - 1, 0)

    stat = pl.BlockSpec((1, 1, tq), lambda h, j, i: (h, 0, prev(i)))
    return pl.pallas_call(
        body, name=name, grid=(ch, nk, nq + 1),
        in_specs=[pl.BlockSpec((tq, 2 * HEAD), lambda h, j, i: (cur(i), h)),
                  pl.BlockSpec((tq, HEAD), lambda h, j, i: (cur(i), h)),
                  pl.BlockSpec((tq, 2 * HEAD), lambda h, j, i: (prev(i), h)),
                  pl.BlockSpec((tq, HEAD), lambda h, j, i: (prev(i), h)), stat, stat,
                  pl.BlockSpec((tk, 2 * HEAD), lambda h, j, i: (j, h)),
                  pl.BlockSpec((2 * HEAD, tk), lambda h, j, i: (h, j)),
                  pl.BlockSpec((tk, HEAD), lambda h, j, i: (j, 2 * h))],
        out_specs=[pl.BlockSpec((2 * HEAD, t), lambda h, j, i: (h, 0)),
                   pl.BlockSpec((tk, 2 * HEAD), lambda h, j, i: (j, h)),
                   pl.BlockSpec((tk, HEAD), lambda h, j, i: (j, h))],
        out_shape=[jax.ShapeDtypeStruct((ch * 2 * HEAD, t), F32), jax.ShapeDtypeStruct((t, ch * 2 * HEAD), BF16),
                   jax.ShapeDtypeStruct((t, ch * HEAD), BF16)],
        scratch_shapes=[pltpu.VMEM((2, tk, tq), F32), pltpu.VMEM((2, tk, tq), F32),
                        pltpu.VMEM((tk, 2 * HEAD), F32), pltpu.VMEM((tk, HEAD), F32)],
        compiler_params=_params(("parallel", "arbitrary", "arbitrary")),
    )(q_cat, do, q_cat, do, lse_row, d_row, k_cat, k_cat_t, v_aug)


def _attn_post_fwd(o, proj, z_off, cw, gc, name, into=None):
    t = o.shape[0]
    r = _pick(t, (256, 128))

    def body(o_ref, z_ref, gc_ref, y_ref):
        sil, _ = _silu_and_grad(z_ref[...].astype(F32))
        yhat, _ = _rms(o_ref[...].astype(F32) * sil, cw)
        y_ref[...] = (yhat * gc_ref[...]).astype(BF16)

    return _row_call(body, name, t, r,
                     [_row_spec(r, cw), _row_spec(r, cw, _col_block(z_off, cw)), _const_spec((1, cw))], (o, proj, gc),
                     _row_spec(r, cw), jax.ShapeDtypeStruct((t, cw), BF16), into=into)


def _attn_post_bwd(o, proj, z_off, cw, gc, dy, dy_col, name):
    t = o.shape[0]
    ch = cw // HEAD
    r = _pick(t, (256, 128))

    def body(o_ref, z_ref, gc_ref, dy_ref, do_ref, dz_ref, ds_ref, dgc_ref):
        _zero_at_first_step([dgc_ref])
        o_v, z = o_ref[...].astype(F32), z_ref[...].astype(F32)
        sil, dsil = _silu_and_grad(z)
        yhat, rr = _rms(o_v * sil, cw)
        dy_f = dy_ref[...].astype(F32)
        dgc_ref[...] += _colsum(dy_f * yhat)
        dyc = _rms_bwd(dy_f * gc_ref[...], yhat, rr, cw)
        do_b = (dyc * sil).astype(BF16)
        do_ref[...] = do_b
        dz_ref[...] = (dyc * o_v * dsil).astype(BF16)
        prod = do_b.astype(F32) * o_v
        for hh in range(ch):
            ds_ref[hh] = jnp.sum(prod[:, hh * HEAD:(hh + 1) * HEAD], axis=-1, keepdims=True)

    return _row_call(
        body, name, t, r,
        [_row_spec(r, cw), _row_spec(r, cw, _col_block(z_off, cw)), _const_spec((1, cw)), _row_spec(r, cw, dy_col)],
        (o, proj, gc, dy),
        [_row_spec(r, cw), _row_spec(r, cw), pl.BlockSpec((ch, r, 1), lambda i: (0, i, 0)), _const_spec((1, cw))],
        [jax.ShapeDtypeStruct((t, cw), BF16), jax.ShapeDtypeStruct((t, cw), BF16),
         jax.ShapeDtypeStruct((ch, t, 1), F32), jax.ShapeDtypeStruct((1, cw), F32)])


def _ple_fwd(h1, gpre, pp, name):
    t, d = h1.shape
    r = _pick(t, (256, 128))

    def body(h_ref, g_ref, p_ref, o_ref):
        o_ref[...] = h_ref[...] + _sigmoid(g_ref[...]) * p_ref[...]

    return _row_call(body, name, t, r, [_row_spec(r, d)] * 3, (h1, gpre, pp), _row_spec(r, d),
                     jax.ShapeDtypeStruct((t, d), F32))


def _ple_bwd(gpre, pp, dh, name):
    t, d = dh.shape
    r = _pick(t, (256, 128))

    def body(g_ref, p_ref, dh_ref, dg_ref, dp_ref):
        sig = _sigmoid(g_ref[...])
        dh_v = dh_ref[...]
        dg_ref[...] = (dh_v * p_ref[...] * sig * (1.0 - sig)).astype(BF16)
        dp_ref[...] = (dh_v * sig).astype(BF16)

    return _row_call(body, name, t, r, [_row_spec(r, d)] * 3, (gpre, pp, dh), [_row_spec(r, d)] * 2,
                     [jax.ShapeDtypeStruct((t, d), BF16)] * 2)


def _loss_and_grad(h, target, name):
    t, d = h.shape
    r = _pick(t, (256, 128))

    def body(h_ref, t_ref, l_ref, dh_ref):
        _zero_at_first_step([l_ref])
        err = h_ref[...] - t_ref[...]
        l_ref[...] += jnp.sum(jnp.sum(err * err, axis=-1, keepdims=True), axis=0, keepdims=True) * (0.5 / d)
        dh_ref[...] = err * (1.0 / d)

    return _row_call(body, name, t, r, [_row_spec(r, d)] * 2, (h, target), [_const_spec((1, 1)), _row_spec(r, d)],
                     [jax.ShapeDtypeStruct((1, 1), F32), jax.ShapeDtypeStruct((t, d), F32)])


def _ew_rows(rows, cols):
    cap = max(8, (1 << 19) // max(cols, 1))
    for cand in (1024, 512, 256, 128, 64, 32, 16, 8):
        if cand <= cap and rows % cand == 0:
            return cand
    return rows


def _pair_sum_bf16(a, b, name):
    n, rows, cols = a.shape
    rb = _ew_rows(rows, cols)

    def body(a_ref, b_ref, o_ref):
        o_ref[...] = (a_ref[...] + b_ref[...]).astype(BF16)

    spec = pl.BlockSpec((1, rb, cols), lambda s, i: (s, i, 0))
    return pl.pallas_call(body, name=name, grid=(n, rows // rb), in_specs=[spec, spec], out_specs=spec,
                          out_shape=jax.ShapeDtypeStruct(a.shape, BF16),
                          compiler_params=_params(("parallel", "parallel")))(a, b)


def _shard_sum(a, b, recv, name):
    rows, cols = a.shape
    rb = _ew_rows(rows, cols)

    def body(a_ref, b_ref, r_ref, o_ref):
        o_ref[...] = ((a_ref[...] + b_ref[...]) + r_ref[0].astype(F32) + r_ref[1].astype(F32)
                      + r_ref[2].astype(F32))

    spec = pl.BlockSpec((rb, cols), lambda i: (i, 0))
    return pl.pallas_call(body, name=name, grid=(rows // rb,),
                          in_specs=[spec, spec, pl.BlockSpec((N_SHARD - 1, rb, cols), lambda i: (0, i, 0))],
                          out_specs=spec, out_shape=jax.ShapeDtypeStruct(a.shape, F32),
                          compiler_params=_params(("parallel",)))(a, b, recv)


def _sum_devices(g, name):
    n, rows, cols = g.shape
    rb = _ew_rows(rows, cols)

    def body(g_ref, o_ref):
        acc = g_ref[0]
        for k in range(1, n):
            acc = acc + g_ref[k]
        o_ref[...] = acc

    return pl.pallas_call(body, name=name, grid=(rows // rb,),
                          in_specs=[pl.BlockSpec((n, rb, cols), lambda i: (0, i, 0))],
                          out_specs=pl.BlockSpec((rb, cols), lambda i: (i, 0)),
                          out_shape=jax.ShapeDtypeStruct((rows, cols), F32),
                          compiler_params=_params(("parallel",)))(g)


def _adamw_update(w, g_v, m, v):
    m_new = ADAM_B1 * m + (1.0 - ADAM_B1) * g_v
    v_new = ADAM_B2 * v + (1.0 - ADAM_B2) * (g_v * g_v)
    m_hat = m_new / (1.0 - ADAM_B1 ** ADAM_STEP)
    v_hat = v_new / (1.0 - ADAM_B2 ** ADAM_STEP)
    return -ADAM_LR * (m_hat / (jnp.sqrt(v_hat) + ADAM_EPS) + ADAM_WD * w), m_new, v_new


def _adamw(w, g, m, v, name):
    rows, cols = w.shape
    rb = _ew_rows(rows, cols)

    def body(w_ref, g_ref, m_ref, v_ref, d_ref, mo_ref, vo_ref):
        d_ref[...], mo_ref[...], vo_ref[...] = _adamw_update(w_ref[...], g_ref[...], m_ref[...], v_ref[...])

    spec = pl.BlockSpec((rb, cols), lambda i: (i, 0))
    return pl.pallas_call(body, name=name, grid=(rows // rb,), in_specs=[spec] * 4, out_specs=[spec] * 3,
                          out_shape=[jax.ShapeDtypeStruct(w.shape, F32)] * 3,
                          compiler_params=_params(("parallel",)))(w, g, m, v)


def _adamw_two_halves(w, own, recv, core_flag, m, v, name):
    depth, rows, cols = w.shape
    assert rows % 2 == 0 and own.shape == (depth, rows // 2, cols)
    rb = _ew_rows(rows // 2, cols)
    nb = rows // 2 // rb

    def body(w_ref, own_ref, recv_ref, flag_ref, m_ref, v_ref, g_ref, d_ref, mo_ref, vo_ref):
        half = pl.program_id(1).astype(F32)
        g_v = jnp.where(flag_ref[...] == half, own_ref[...], recv_ref[...])
        g_ref[...] = g_v
        d_ref[...], mo_ref[...], vo_ref[...] = _adamw_update(w_ref[...], g_v, m_ref[...], v_ref[...])

    full = pl.BlockSpec((None, rb, cols), lambda l, k, i: (l, k * nb + i, 0))
    half_spec = pl.BlockSpec((None, rb, cols), lambda l, k, i: (l, i, 0))
    return pl.pallas_call(
        body, name=name, grid=(depth, 2, nb),
        in_specs=[full, half_spec, half_spec, pl.BlockSpec((1, 1), lambda l, k, i: (0, 0)), full, full],
        out_specs=[full] * 4, out_shape=[jax.ShapeDtypeStruct(w.shape, F32)] * 4,
        compiler_params=_params(("parallel", "parallel", "parallel")))(w, own, recv, core_flag, m, v)


def _place():
    return lax.axis_index("x"), lax.axis_index("y"), lax.axis_index("c")


def _other_chips(x, y):
    return [(1 - x, y), (x, 1 - y), (1 - x, 1 - y)]


_ANY = pl.BlockSpec(memory_space=pl.ANY)


class _Rider:
    def __init__(self, arrays, out_shapes, n_sems, start, finish, in_place=False):
        self.arrays, self.out_shapes, self.n_sems = list(arrays), list(out_shapes), n_sems
        self.start, self.finish, self.in_place = start, finish, in_place

    def sems(self):
        return [pltpu.SemaphoreType.DMA((self.n_sems,)), pltpu.SemaphoreType.DMA((self.n_sems,))]

    def aliases(self, first_in, first_out):
        return {first_in + a: first_out + a for a in range(len(self.arrays))} if self.in_place else {}


def _comm_call(rider, name):
    n_in, n_out = len(rider.arrays), len(rider.out_shapes)

    def body(*refs):
        ins, outs = refs[:n_in], refs[n_in:n_in + n_out]
        send_sems, recv_sems = refs[n_in + n_out:]
        rider.start(ins, outs, send_sems, recv_sems)
        rider.finish(ins, outs, send_sems, recv_sems)

    return pl.pallas_call(
        body, name=name, in_specs=[_ANY] * n_in, out_specs=[_ANY] * n_out, out_shape=rider.out_shapes,
        scratch_shapes=rider.sems(), input_output_aliases=rider.aliases(0, 0))(*rider.arrays)


def _half(shape, which):
    for axis, size in enumerate(shape):
        if size % 2 == 0:
            return (slice(None),) * axis + (pl.ds(which * (size // 2), size // 2),)
    raise ValueError(f"no axis of even length in {shape}")


def _start_then_wait(copies):
    def start(*refs):
        for send, _ in copies(*refs):
            send.start()

    def finish(*refs):
        pairs = copies(*refs)
        for _, landing in pairs:
            landing.wait_recv()
        for send, _ in pairs:
            send.wait_send()

    return start, finish


def _fetch_rider(shards):
    n, n_peer = len(shards), N_SHARD - 1
    shapes = [s.shape for s in shards]

    def copies(ins, outs, send_sems, recv_sems):
        x, y, c = _place()
        pairs = []
        for a in range(n):
            mine = _half(shapes[a], c)
            for k, (px, py) in enumerate(_other_chips(x, y)):
                def into(slot):
                    return pltpu.make_async_remote_copy(
                        src_ref=ins[a].at[mine], dst_ref=outs[a].at[(slot,) + mine],
                        send_sem=send_sems.at[a * n_peer + k], recv_sem=recv_sems.at[a * n_peer + k],
                        device_id=(px, py, c), device_id_type=MESH)
                pairs.append((into(2 * x + y), into(2 * px + py)))
        return pairs

    start, finish = _start_then_wait(copies)
    return _Rider(shards, [jax.ShapeDtypeStruct((N_SHARD,) + s.shape, s.dtype) for s in shards], n * n_peer,
                  start, finish)


def _forward_rider(gathered, shards):
    n, n_peer = len(gathered), N_SHARD - 1
    shapes = [s.shape for s in shards]

    def copies(ins, outs, send_sems, recv_sems):
        x, y, c = _place()
        pairs = []
        for a in range(n):
            for k, (px, py) in enumerate(_other_chips(x, y)):
                def half_of_slot(which):
                    rows = outs[a].at[(2 * px + py,) + _half(shapes[a], which)]
                    return pltpu.make_async_remote_copy(
                        src_ref=rows, dst_ref=rows, send_sem=send_sems.at[a * n_peer + k],
                        recv_sem=recv_sems.at[a * n_peer + k], device_id=(x, y, 1 - c), device_id_type=MESH)
                pairs.append((half_of_slot(c), half_of_slot(1 - c)))
        return pairs

    start, finish = _start_then_wait(copies)
    return _Rider(gathered, [jax.ShapeDtypeStruct(g.shape, g.dtype) for g in gathered], n * n_peer, start, finish,
                  in_place=True)


def _sibling_rider(arrs, other_half):
    n = len(arrs)

    def copies(ins, outs, send_sems, recv_sems):
        x, y, c = _place()
        pairs = []
        for a in range(n):
            cp = pltpu.make_async_remote_copy(
                src_ref=ins[a].at[1 - c] if other_half else ins[a], dst_ref=outs[a], send_sem=send_sems.at[a],
                recv_sem=recv_sems.at[a], device_id=(x, y, 1 - c), device_id_type=MESH)
            pairs.append((cp, cp))
        return pairs

    start, finish = _start_then_wait(copies)
    return _Rider(arrs, [jax.ShapeDtypeStruct(g.shape[1:] if other_half else g.shape, g.dtype) for g in arrs], n,
                  start, finish)


def _owner_rider(parts):
    n, n_peer = len(parts), N_SHARD - 1

    def copies(ins, outs, send_sems, recv_sems):
        x, y, c = _place()
        pairs = []
        for a in range(n):
            for k, (px, py) in enumerate(_other_chips(x, y)):
                cp = pltpu.make_async_remote_copy(
                    src_ref=ins[a].at[2 * px + py], dst_ref=outs[a].at[k], send_sem=send_sems.at[a * n_peer + k],
                    recv_sem=recv_sems.at[a * n_peer + k], device_id=(px, py, c), device_id_type=MESH)
                pairs.append((cp, cp))
        return pairs

    start, finish = _start_then_wait(copies)
    return _Rider(parts, [jax.ShapeDtypeStruct((n_peer,) + p.shape[1:], p.dtype) for p in parts], n * n_peer,
                  start, finish)


def _gather_devices(buf, name):
    n_peer = N_DEV - 1

    def body(in_ref, out_ref, send_sems, recv_sems, local_sem):
        x, y, c = _place()
        me = 4 * x + 2 * y + c
        mine = pltpu.make_async_copy(in_ref, out_ref.at[me], local_sem)
        mine.start()
        peers = []
        for k in range(1, N_DEV):
            fx, fy, fc = (k >> 2) & 1, (k >> 1) & 1, k & 1
            peers.append((x ^ fx, y ^ fy, c ^ fc))
        sends = []
        for k, peer in enumerate(peers):
            cp = pltpu.make_async_remote_copy(
                src_ref=in_ref, dst_ref=out_ref.at[me], send_sem=send_sems.at[k], recv_sem=recv_sems.at[k],
                device_id=peer, device_id_type=MESH)
            cp.start()
            sends.append(cp)
        for k, (px, py, pc) in enumerate(peers):
            pltpu.make_async_remote_copy(
                src_ref=in_ref, dst_ref=out_ref.at[4 * px + 2 * py + pc], send_sem=send_sems.at[k],
                recv_sem=recv_sems.at[k], device_id=(px, py, pc), device_id_type=MESH).wait_recv()
        for cp in sends:
            cp.wait_send()
        mine.wait()

    return pl.pallas_call(
        body, name=name, in_specs=[_ANY], out_specs=_ANY,
        out_shape=jax.ShapeDtypeStruct((N_DEV,) + buf.shape, buf.dtype),
        scratch_shapes=[pltpu.SemaphoreType.DMA((n_peer,)), pltpu.SemaphoreType.DMA((n_peer,)),
                        pltpu.SemaphoreType.DMA(())],
    )(buf)


class _Dims:
    def __init__(self, x, p, w_in, sgu_norm, conv_w, kv_norm, w_ukv, w_out):
        self.t, self.d = x.shape[1], x.shape[2]
        self.depth = w_in.shape[0]
        self.ple = p.shape[3]
        self.in_w = w_in.shape[2] * N_SHARD
        self.ah = sgu_norm.shape[1]
        self.aw = self.ah * HEAD
        self.bw = conv_w.shape[2] * N_SHARD
        self.kvr = kv_norm.shape[1]
        self.ch = w_ukv.shape[2] * N_SHARD // (2 * HEAD)
        self.cw = self.ch * HEAD
        self.mix = w_out.shape[1] * N_SHARD
        assert self.mix == self.aw + self.bw + self.cw and self.aw == self.bw
        self.qw = self.ch * 2 * HEAD
        segs = [('a', 3 * self.aw, self.aw), ('b', 4 * self.bw, self.bw), ('ckv', self.kvr, self.kvr),
                ('q', self.qw, self.qw), ('cz', self.cw, self.cw), ('kr', HEAD, HEAD)]
        off = 0
        self.off = {}
        for nm, width, align in segs:
            off = -(-off // align) * align
            self.off[nm] = off
            off += width
        self.inp = -(-off // 512) * 512
        q_real = self.ch * (HEAD + ROPE)
        widths = [3 * self.aw, 4 * self.bw, q_real, self.kvr, ROPE, self.cw]
        assert sum(widths) == self.in_w
        starts = [0]
        for wd in widths:
            starts.append(starts[-1] + wd)
        self.src = dict(zip(['a', 'b', 'q', 'ckv', 'kr', 'cz'], zip(starts[:-1], widths)))


def _rearrange_w_in(w, dm):
    lead = w.shape[:-1]
    pieces = {}
    for nm in ('a', 'b', 'ckv', 'cz'):
        s, wd = dm.src[nm]
        pieces[nm] = w[..., s:s + wd]
    s, wd = dm.src['q']
    q = w[..., s:s + wd].reshape(lead + (dm.ch, HEAD + ROPE))
    pieces['q'] = jnp.pad(q, [(0, 0)] * (q.ndim - 1) + [(0, 2 * HEAD - HEAD - ROPE)]).reshape(lead + (dm.qw,))
    s, wd = dm.src['kr']
    pieces['kr'] = jnp.pad(w[..., s:s + wd], [(0, 0)] * len(lead) + [(0, HEAD - ROPE)])
    out, cur = [], 0
    for nm in sorted(dm.off, key=lambda k: dm.off[k]):
        if dm.off[nm] > cur:
            out.append(jnp.zeros(lead + (dm.off[nm] - cur,), w.dtype))
        out.append(pieces[nm])
        cur = dm.off[nm] + pieces[nm].shape[-1]
    if dm.inp > cur:
        out.append(jnp.zeros(lead + (dm.inp - cur,), w.dtype))
    return jnp.concatenate(out, axis=-1)


def _unarrange_w_in(g, dm):
    lead = g.shape[:-1]

    def seg(nm, width):
        return g[..., dm.off[nm]:dm.off[nm] + width]

    q = seg('q', dm.qw).reshape(lead + (dm.ch, 2 * HEAD))[..., :HEAD + ROPE].reshape(lead + (dm.ch * (HEAD + ROPE),))
    return jnp.concatenate([seg('a', 3 * dm.aw), seg('b', 4 * dm.bw), q, seg('ckv', dm.kvr), seg('kr', ROPE),
                            seg('cz', dm.cw)], axis=-1)


def _assemble_dproj(parts, dm, t):
    out, cur = [], 0
    for nm in sorted(dm.off, key=lambda k: dm.off[k]):
        if dm.off[nm] > cur:
            out.append(jnp.zeros((t, dm.off[nm] - cur), BF16))
        out.append(parts[nm])
        cur = dm.off[nm] + parts[nm].shape[-1]
    if dm.inp > cur:
        out.append(jnp.zeros((t, dm.inp - cur), BF16))
    return jnp.concatenate(out, axis=-1)


def _rope_tables(positions):
    inv = 1.0 / (ROPE_BASE ** (jnp.arange(0, ROPE, 2, dtype=F32) / ROPE))
    ang = positions.astype(F32)[:, None] * inv
    cos, sin = jnp.cos(ang), jnp.sin(ang)
    t = positions.shape[0]
    half = ROPE // 2
    cos_t = jnp.concatenate([cos, cos, jnp.zeros((t, HEAD - ROPE), F32)], axis=-1)
    sin_a = jnp.concatenate([-sin, jnp.zeros((t, HEAD - half), F32)], axis=-1)
    sin_b = jnp.concatenate([jnp.zeros((t, half), F32), sin, jnp.zeros((t, HEAD - ROPE), F32)], axis=-1)
    return cos_t, sin_a, sin_b


def _pad_gain(g):
    return jnp.pad(g, (0, HEAD - g.shape[0]))[None, :]


def _shard_major(g, axis):
    shape = g.shape
    g = g.reshape(shape[:axis] + (N_SHARD, shape[axis] // N_SHARD) + shape[axis + 1:])
    g = jnp.moveaxis(g, axis, 0)
    rows, cols = g.shape[1], g.shape[2]
    return jnp.swapaxes(g.reshape(N_SHARD, 2, rows // 2, cols), 0, 1)


def _pack(arrs):
    flat = jnp.concatenate([a.reshape(-1) for a in arrs])
    pad = (-flat.shape[0]) % (8 * HEAD)
    return jnp.pad(flat, (0, pad)).reshape(-1, HEAD)


def _unpack(buf, shapes):
    flat = buf.reshape(-1)
    out, cur = [], 0
    for s in shapes:
        size = 1
        for v in s:
            size *= v
        out.append(flat[cur:cur + size].reshape(s))
        cur += size
    return out


def kernel(x, p, positions, attn_norm, w_in, sgu_norm, w_spatial, b_spatial, conv_w, conv_b, kv_norm, w_ukv, q_nope_norm, q_rope_norm, k_nope_norm, k_rope_norm, out_norm, w_out, ple_norm, w_ple_gate, w_ple_proj, loss_target, m_attn_norm, m_w_in, m_sgu_norm, m_w_spatial, m_b_spatial, m_conv_w, m_conv_b, m_kv_norm, m_w_ukv, m_q_nope_norm, m_q_rope_norm, m_k_nope_norm, m_k_rope_norm, m_out_norm, m_w_out, m_ple_norm, m_w_ple_gate, m_w_ple_proj, v_attn_norm, v_w_in, v_sgu_norm, v_w_spatial, v_b_spatial, v_conv_w, v_conv_b, v_kv_norm, v_w_ukv, v_q_nope_norm, v_q_rope_norm, v_k_nope_norm, v_k_rope_norm, v_out_norm, v_w_out, v_ple_norm, v_w_ple_gate, v_w_ple_proj):
    weights = dict(attn_norm=attn_norm, w_in=w_in, sgu_norm=sgu_norm, w_spatial=w_spatial, b_spatial=b_spatial,
                   conv_w=conv_w, conv_b=conv_b, kv_norm=kv_norm, w_ukv=w_ukv, q_nope_norm=q_nope_norm,
                   q_rope_norm=q_rope_norm, k_nope_norm=k_nope_norm, k_rope_norm=k_rope_norm, out_norm=out_norm,
                   w_out=w_out, ple_norm=ple_norm, w_ple_gate=w_ple_gate, w_ple_proj=w_ple_proj)
    mom_m = dict(attn_norm=m_attn_norm, w_in=m_w_in, sgu_norm=m_sgu_norm, w_spatial=m_w_spatial,
                 b_spatial=m_b_spatial, conv_w=m_conv_w, conv_b=m_conv_b, kv_norm=m_kv_norm, w_ukv=m_w_ukv,
                 q_nope_norm=m_q_nope_norm, q_rope_norm=m_q_rope_norm, k_nope_norm=m_k_nope_norm,
                 k_rope_norm=m_k_rope_norm, out_norm=m_out_norm, w_out=m_w_out, ple_norm=m_ple_norm,
                 w_ple_gate=m_w_ple_gate, w_ple_proj=m_w_ple_proj)
    mom_v = dict(attn_norm=v_attn_norm, w_in=v_w_in, sgu_norm=v_sgu_norm, w_spatial=v_w_spatial,
                 b_spatial=v_b_spatial, conv_w=v_conv_w, conv_b=v_conv_b, kv_norm=v_kv_norm, w_ukv=v_w_ukv,
                 q_nope_norm=v_q_nope_norm, q_rope_norm=v_q_rope_norm, k_nope_norm=v_k_nope_norm,
                 k_rope_norm=v_k_rope_norm, out_norm=v_out_norm, w_out=v_w_out, ple_norm=v_ple_norm,
                 w_ple_gate=v_w_ple_gate, w_ple_proj=v_w_ple_proj)
    dm = _Dims(x, p, w_in, sgu_norm, conv_w, kv_norm, w_ukv, w_out)
    t, d, depth = dm.t, dm.d, dm.depth
    shard = 2 * lax.axis_index("x") + lax.axis_index("y")
    core = lax.axis_index("c")
    scale = float(HEAD + ROPE) ** -0.5

    def local_layer(i):
        return [weights[n][i:i + 1].astype(BF16) for n in BIG]

    def fill_own_slot(gathered, local):
        return [lax.dynamic_update_slice(g, mine[None], (shard,) + (0,) * mine.ndim)
                for g, mine in zip(gathered, local)]

    def layer_weights(filled):
        w = {n: jnp.concatenate([filled[j][s] for s in range(N_SHARD)], axis=BIG_AXIS[n])
             for j, n in enumerate(BIG)}
        w['w_in'] = _rearrange_w_in(w['w_in'], dm)
        return w

    first = local_layer(0) + [conv_w]
    fetched = _comm_call(_fetch_rider(first), "fetch_weights_l0")
    filled = fill_own_slot(_comm_call(_forward_rider(fetched, first), "forward_weights_l0"), first)
    layer_w = [layer_weights(filled)] + [None] * (depth - 1)
    conv_w_full = jnp.concatenate([filled[len(BIG)][s] for s in range(N_SHARD)], axis=2)

    tabs = _rope_tables(positions[0])
    h = x[0]
    saved = []
    for i in range(depth):
        tag = f"l{i}_"
        ga, gb, gc = (out_norm[i][None, :dm.aw], out_norm[i][None, dm.aw:dm.aw + dm.bw],
                      out_norm[i][None, dm.aw + dm.bw:])
        ws_b = w_spatial[i].astype(BF16)
        bb = jnp.broadcast_to(b_spatial[i][:, :, None], (dm.ah, HEAD, HEAD))
        qn_g, qr_g = q_nope_norm[i][None, :], _pad_gain(q_rope_norm[i])
        kn_g, kr_g = k_nope_norm[i][None, :], _pad_gain(k_rope_norm[i])
        kv_g = kv_norm[i][None, :]
        wl = layer_w[i]
        nxt = local_layer(i + 1) if i + 1 < depth else None
        hn = _norm_fwd(h, attn_norm[i][None, :], tag + "norm1")
        if nxt is None:
            proj = _matmul(hn, wl['w_in'], 'nn', BF16, tag + "proj", b_layer=0)
        else:
            proj, fetched = _matmul(hn, wl['w_in'], 'nn', BF16, tag + "proj", b_layer=0, rider=_fetch_rider(nxt))
        y = _sgu_fwd(proj, dm.off['a'], dm.aw, sgu_norm[i], ws_b, bb, ga, tag + "sgu",
                     into=(jnp.zeros((t, dm.mix), BF16), 0))
        y, yconv = _conv_fwd(proj, dm.off['b'], dm.bw, conv_w_full[i], conv_b[i][None, :], gb, tag + "conv",
                             into=(y, _col_block(dm.aw, dm.bw)))
        q_cat, ckv_n, kr_rot = _mla_prep_fwd(proj, dm.off['q'], dm.off['ckv'], dm.off['kr'], dm.ch, dm.kvr, tabs,
                                             qn_g, qr_g, kr_g, kv_g, tag + "mla_prep")
        kv = _matmul(ckv_n, wl['w_ukv'], 'nn', F32, tag + "kv_up", b_layer=0)
        k_cat, v_aug = _kv_prep_fwd(kv, kr_rot, dm.ch, kn_g, tag + "kv_prep")
        o, lse = _attn_fwd(q_cat, k_cat, v_aug, dm.ch, scale, tag + "attn")
        y = _attn_post_fwd(o, proj, dm.off['cz'], dm.cw, gc, tag + "attn_post",
                           into=(y, _col_block(dm.aw + dm.bw, dm.cw)))
        if nxt is None:
            h1 = _matmul(y, wl['w_out'], 'nn', F32, tag + "out", add=h, b_layer=0)
        else:
            h1, gathered = _matmul(y, wl['w_out'], 'nn', F32, tag + "out", add=h, b_layer=0,
                                   rider=_forward_rider(list(fetched), nxt))
            layer_w[i + 1] = layer_weights(fill_own_slot(gathered, nxt))
        hn2 = _norm_fwd(h1, ple_norm[i][None, :], tag + "norm2")
        gpre = _matmul(hn2, wl['w_ple_gate'], 'nn', F32, tag + "gate", b_layer=0)
        p_b = p[i, 0].astype(BF16)
        pp = _matmul(p_b, wl['w_ple_proj'], 'nn', F32, tag + "ple_proj", b_layer=0)
        h2 = _ple_fwd(h1, gpre, pp, tag + "ple")
        saved.append(dict(h=h, hn=hn, proj=proj, yconv=yconv, q_cat=q_cat, ckv_n=ckv_n, kv=kv, k_cat=k_cat,
                          v=v_aug, o=o, lse=lse, y=y, h1=h1, hn2=hn2, gpre=gpre, pp=pp, p_b=p_b, ws_b=ws_b, bb=bb,
                          gains=(ga, gb, gc, qn_g, qr_g, kn_g, kr_g, kv_g)))
        h = h2

    loss_part, dh = _loss_and_grad(h, loss_target[0], "loss")
    loss = lax.psum(loss_part[0, 0], ("x", "y", "c"))

    def chip_sums(sm, from_sibling, tag):
        mine = [lax.dynamic_index_in_dim(g, core, 0, keepdims=False) for g in sm]
        return mine, [_pair_sum_bf16(a, b, f"{tag}chip_sum_{n}") for a, b, n in zip(mine, from_sibling, BIG)]

    def shard_sums(mine, from_sibling, from_chips, tag):
        out = []
        for a, b, r3, n in zip(mine, from_sibling, from_chips, BIG):
            own_a = lax.dynamic_index_in_dim(a, shard, 0, keepdims=False)
            own_b = lax.dynamic_index_in_dim(b, shard, 0, keepdims=False)
            out.append(_shard_sum(own_a, own_b, r3, f"{tag}shard_sum_{n}"))
        return out

    grads = {n: [None] * depth for n in WEIGHTS}
    own_half = {n: [None] * depth for n in BIG}
    sibling_half = {n: [None] * depth for n in BIG}
    carry = None
    for i in reversed(range(depth)):
        tag = f"l{i}_b_"
        gtag = f"l{i + 1}_g_"
        sv = saved[i]
        wl = layer_w[i]
        ga, gb, gc, qn_g, qr_g, kn_g, kr_g, kv_g = sv['gains']
        proj = sv['proj']
        dgpre, dpp = _ple_bwd(sv['gpre'], sv['pp'], dh, tag + "ple")
        grads['w_ple_proj'][i] = _matmul(sv['p_b'], dpp, 'tn', F32, tag + "d_w_ple_proj")
        if carry is None:
            grads['w_ple_gate'][i] = _matmul(sv['hn2'], dgpre, 'tn', F32, tag + "d_w_gate")
        else:
            grads['w_ple_gate'][i], from_sibling = _matmul(sv['hn2'], dgpre, 'tn', F32, tag + "d_w_gate",
                                                           rider=_sibling_rider(carry, True))
            mine, sums = chip_sums(carry, from_sibling, gtag)
        d_hn2 = _matmul(dgpre, wl['w_ple_gate'], 'nt', BF16, tag + "d_hn2", b_layer=0)
        dh1, dh1_b, g_ple = _norm_bwd(sv['h1'], ple_norm[i][None, :], d_hn2, dh, tag + "norm2")
        grads['ple_norm'][i] = g_ple[0]
        grads['w_out'][i] = _matmul(sv['y'], dh1_b, 'tn', F32, tag + "d_w_out")
        dy = _matmul(dh1_b, wl['w_out'], 'nt', BF16, tag + "d_y", b_layer=0)
        ws_t = jnp.swapaxes(sv['ws_b'], 1, 2)
        d_a, g_sgu, g_ws, g_bs, g_ga = _sgu_bwd(proj, dm.off['a'], dm.aw, sgu_norm[i], sv['ws_b'], ws_t, sv['bb'],
                                                ga, dy, tag + "sgu")
        grads['sgu_norm'][i], grads['w_spatial'][i], grads['b_spatial'][i] = g_sgu, g_ws, g_bs[:, :, 0]
        dyc, d_bb, d_bz, g_gb, g_cb = _conv_bwd_gate(proj, dm.off['b'], dm.bw, sv['yconv'], gb, dy, tag + "conv_gate")
        d_b, g_cw = _conv_bwd_taps(proj, dm.off['b'], dm.bw, dyc, conv_w_full[i], d_bb, d_bz, tag + "conv_taps")
        grads['conv_b'][i], grads['conv_w'][i] = g_cb[0], g_cw
        d_o, d_cz, dsum, g_gc = _attn_post_bwd(sv['o'], proj, dm.off['cz'], dm.cw, gc, dy,
                                               _col_block(dm.aw + dm.bw, dm.cw), tag + "attn_post")
        grads['out_norm'][i] = jnp.concatenate([g_ga[0], g_gb[0], g_gc[0]])
        dq_t, dk_cat, dv = _attn_bwd(sv['q_cat'], sv['k_cat'], sv['k_cat'].T, sv['v'], d_o,
                                     sv['lse'].reshape(dm.ch, 1, t), dsum.reshape(dm.ch, 1, t), dm.ch, scale,
                                     tag + "attn_bwd")
        dq_cat = dq_t.T.astype(BF16)
        dkv, dkr_rot, g_kn = _kv_prep_bwd(sv['kv'], dm.ch, kn_g, dk_cat, dv, tag + "kv_prep")
        grads['k_nope_norm'][i] = g_kn[0]
        grads['w_ukv'][i] = _matmul(sv['ckv_n'], dkv, 'tn', F32, tag + "d_w_ukv")
        dckv_n = _matmul(dkv, wl['w_ukv'], 'nt', BF16, tag + "d_ckv", b_layer=0)
        d_q, d_ckv, d_kr, g_qn, g_qr, g_kr, g_kv = _mla_prep_bwd(
            proj, dm.off['q'], dm.off['ckv'], dm.off['kr'], dm.ch, dm.kvr, tabs, qn_g, qr_g, kr_g, kv_g,
            dq_cat, scale, dckv_n, dkr_rot, dm.inp - dm.off['kr'], tag + "mla_prep")
        grads['q_nope_norm'][i], grads['q_rope_norm'][i] = g_qn[0], g_qr[0, :ROPE]
        grads['k_rope_norm'][i], grads['kv_norm'][i] = g_kr[0, :ROPE], g_kv[0]
        dproj = _assemble_dproj(dict(a=d_a, b=d_b, ckv=d_ckv, q=d_q, cz=d_cz, kr=d_kr), dm, t)
        if carry is None:
            d_w_in = _matmul(sv['hn'], dproj, 'tn', F32, tag + "d_w_in")
            d_hn = _matmul(dproj, wl['w_in'], 'nt', BF16, tag + "d_hn", b_layer=0)
        else:
            d_w_in, from_chips = _matmul(sv['hn'], dproj, 'tn', F32, tag + "d_w_in", rider=_owner_rider(sums))
            halves = shard_sums(mine, from_sibling, from_chips, gtag)
            d_hn, from_core = _matmul(dproj, wl['w_in'], 'nt', BF16, tag + "d_hn", b_layer=0,
                                      rider=_sibling_rider(halves, False))
            for n, own, recv in zip(BIG, halves, from_core):
                own_half[n][i + 1], sibling_half[n][i + 1] = own, recv
        grads['w_in'][i] = _unarrange_w_in(d_w_in, dm)
        dh, _, g_an = _norm_bwd(sv['h'], attn_norm[i][None, :], d_hn, dh1, tag + "norm1")
        grads['attn_norm'][i] = g_an[0]
        carry = [_shard_major(grads[n][i], BIG_AXIS[n] - 1) for n in BIG]
    grad_x = dh[None]

    from_sibling = _comm_call(_sibling_rider(carry, True), "l0_g_to_sibling")
    mine, sums = chip_sums(carry, from_sibling, "l0_g_")
    from_chips = _comm_call(_owner_rider(sums), "l0_g_to_owner_chips")
    halves = shard_sums(mine, from_sibling, from_chips, "l0_g_")
    from_core = _comm_call(_sibling_rider(halves, False), "l0_g_share_sibling")
    for n, own, recv in zip(BIG, halves, from_core):
        own_half[n][0], sibling_half[n][0] = own, recv

    core_flag = core.astype(F32).reshape(1, 1)
    out_g, out_d, out_m, out_v = {}, {}, {}, {}
    for n in BIG:
        out_g[n], out_d[n], out_m[n], out_v[n] = _adamw_two_halves(
            weights[n], jnp.stack(own_half[n]), jnp.stack(sibling_half[n]), core_flag, mom_m[n], mom_v[n],
            f"adamw_{n}")
    grads = {n: jnp.stack(grads[n]) for n in SMALL}

    shapes = [grads[n].shape for n in SMALL]
    summed = _unpack(_sum_devices(_gather_devices(_pack([grads[n] for n in SMALL]), "gather_small_grads"),
                                  "sum_small_grads"), shapes)
    small_g = dict(zip(SMALL, summed))
    small_g['conv_w'] = lax.dynamic_slice_in_dim(small_g['conv_w'], shard * conv_w.shape[2], conv_w.shape[2], axis=2)
    local_shapes = [weights[n].shape for n in SMALL]
    d_s, m_s, v_s = _adamw(_pack([weights[n] for n in SMALL]), _pack([small_g[n] for n in SMALL]),
                           _pack([mom_m[n] for n in SMALL]), _pack([mom_v[n] for n in SMALL]), "adamw_small")
    for n, dd, mm, vv in zip(SMALL, _unpack(d_s, local_shapes), _unpack(m_s, local_shapes),
                             _unpack(v_s, local_shapes)):
        out_g[n], out_d[n], out_m[n], out_v[n] = small_g[n], dd, mm, vv

    return (loss, grad_x, *[out_g[n] for n in WEIGHTS], *[out_d[n] for n in WEIGHTS],
            *[out_m[n] for n in WEIGHTS], *[out_v[n] for n in WEIGHTS])
```

```python
import functools

import jax
import jax.numpy as jnp
from jax import lax
from jax.experimental import pallas as pl
from jax.experimental.pallas import tpu as pltpu

F32 = jnp.float32
BF16 = jnp.bfloat16
EPS = 1e-6
HEAD = 128
ROPE = 64
ROPE_BASE = 10000.0
CONV_TAPS = 3
N_SHARD = 4
N_DEV = 8
ADAM_LR = 0.001
ADAM_B1 = 0.9
ADAM_B2 = 0.999
ADAM_EPS = 1e-08
ADAM_WD = 0.01
ADAM_STEP = 10
MESH = pl.DeviceIdType.MESH
VMEM_LIMIT = 56 * 1024 * 1024
HALO_ROWS = 16

WEIGHTS = ['attn_norm', 'w_in', 'sgu_norm', 'w_spatial', 'b_spatial', 'conv_w', 'conv_b', 'kv_norm', 'w_ukv',
           'q_nope_norm', 'q_rope_norm', 'k_nope_norm', 'k_rope_norm', 'out_norm', 'w_out', 'ple_norm',
           'w_ple_gate', 'w_ple_proj']
BIG = ['w_in', 'w_ukv', 'w_out', 'w_ple_gate', 'w_ple_proj']
BIG_AXIS = {'w_in': 1, 'w_ukv': 2, 'w_out': 1, 'w_ple_gate': 1, 'w_ple_proj': 2}
SMALL = [n for n in WEIGHTS if n not in BIG]


def _pick(n, cands):
    for c in cands:
        if n % c == 0:
            return c
    return n


def _params(sem=None):
    return pltpu.CompilerParams(dimension_semantics=sem, vmem_limit_bytes=VMEM_LIMIT)


def _matmul(a, b, mode, out_dtype, name, add=None, b_layer=None, rider=None):
    b_shape = b.shape if b_layer is None else b.shape[1:]
    if mode == 'nn':
        (m, k), n = a.shape, b_shape[1]
    elif mode == 'nt':
        (m, k), n = a.shape, b_shape[0]
    else:
        (k, m), n = a.shape, b_shape[1]
    tm = _pick(m, (1024, 512, 256, 128))
    tn = _pick(n, (1536, 1024, 512, 256, 128))
    tk = k if k <= 2048 else _pick(k, (2048, 1536, 1024, 512, 256, 128))
    nk = k // tk
    if mode == 'tn':
        a_spec = pl.BlockSpec((tk, tm), lambda i, j, kk: (kk, i))
        dims = (((0,), (0,)), ((), ()))
    else:
        a_spec = pl.BlockSpec((tm, tk), lambda i, j, kk: (i, kk))
        dims = (((1,), (0,)), ((), ())) if mode == 'nn' else (((1,), (1,)), ((), ()))
    b_block = (tn, tk) if mode == 'nt' else (tk, tn)
    if b_layer is None:
        b_spec = pl.BlockSpec(b_block, (lambda i, j, kk: (j, kk)) if mode == 'nt' else (lambda i, j, kk: (kk, j)))
    else:
        b_spec = pl.BlockSpec((None,) + b_block, (lambda i, j, kk: (b_layer, j, kk)) if mode == 'nt'
                              else (lambda i, j, kk: (b_layer, kk, j)))
    o_spec = pl.BlockSpec((tm, tn), lambda i, j, kk: (i, j))
    has_add = add is not None

    def body(*refs):
        a_ref, b_ref = refs[0], refs[1]
        add_ref = refs[2] if has_add else None
        o_ref = refs[3] if has_add else refs[2]

        def product():
            return lax.dot_general(a_ref[...], b_ref[...], dims, preferred_element_type=F32)

        def finish(res):
            if has_add:
                res = res + add_ref[...]
            o_ref[...] = res.astype(out_dtype)

        if nk == 1:
            finish(product())
        else:
            acc_ref = refs[-1]
            kk = pl.program_id(2)

            @pl.when(kk == 0)
            def _():
                acc_ref[...] = product()

            @pl.when((kk > 0) & (kk < nk - 1))
            def _():
                acc_ref[...] += product()

            @pl.when(kk == nk - 1)
            def _():
                finish(acc_ref[...] + product())

    in_specs = [a_spec, b_spec] + ([o_spec] if has_add else [])
    args = [a, b] + ([add] if has_add else [])
    grid = (m // tm, n // tn, nk)
    scratch = [pltpu.VMEM((tm, tn), F32)] if nk > 1 else []
    if rider is None:
        return pl.pallas_call(
            body, name=name, grid=grid, in_specs=in_specs, out_specs=o_spec,
            out_shape=jax.ShapeDtypeStruct((m, n), out_dtype), scratch_shapes=scratch,
            compiler_params=_params(("parallel", "parallel", "arbitrary")),
        )(*args)

    n_in, n_rin, n_rout = len(args), len(rider.arrays), len(rider.out_shapes)

    def body_with_rider(*refs):
        r_in = refs[n_in:n_in + n_rin]
        r_out = refs[n_in + n_rin + 1:n_in + n_rin + 1 + n_rout]
        own = refs[:n_in] + refs[n_in + n_rin:n_in + n_rin + 1] + refs[n_in + n_rin + 1 + n_rout:len(refs) - 2]
        send_sems, recv_sems = refs[-2:]
        ids = [pl.program_id(ax) for ax in range(3)]

        @pl.when((ids[0] == 0) & (ids[1] == 0) & (ids[2] == 0))
        def _():
            rider.start(r_in, r_out, send_sems, recv_sems)

        body(*own)

        @pl.when((ids[0] == grid[0] - 1) & (ids[1] == grid[1] - 1) & (ids[2] == grid[2] - 1))
        def _():
            rider.finish(r_in, r_out, send_sems, recv_sems)

    res = pl.pallas_call(
        body_with_rider, name=name, grid=grid, in_specs=in_specs + [_ANY] * n_rin,
        out_specs=[o_spec] + [_ANY] * n_rout,
        out_shape=[jax.ShapeDtypeStruct((m, n), out_dtype)] + rider.out_shapes,
        scratch_shapes=scratch + rider.sems(), input_output_aliases=rider.aliases(n_in, 1),
        compiler_params=_params(("arbitrary", "arbitrary", "arbitrary")),
    )(*args, *rider.arrays)
    return res[0], res[1:]


def _rms(x, n):
    r = lax.rsqrt(jnp.sum(x * x, axis=-1, keepdims=True) * (1.0 / n) + EPS)
    return x * r, r


def _rms_bwd(dxhat, xhat, r, n):
    return r * (dxhat - xhat * (jnp.sum(dxhat * xhat, axis=-1, keepdims=True) * (1.0 / n)))


def _sigmoid(z):
    return 1.0 / (1.0 + jnp.exp(-z))


def _silu_and_grad(z):
    sig = _sigmoid(z)
    return z * sig, sig * (1.0 + z * (1.0 - sig))


def _colsum(x):
    return jnp.sum(x, axis=0, keepdims=True)


def _rope(t, cos_t, sin_a, sin_b):
    return t * cos_t + pltpu.roll(t, 96, 1) * sin_a + pltpu.roll(t, 32, 1) * sin_b


def _rope_bwd(d, cos_t, sin_a, sin_b):
    return d * cos_t + pltpu.roll(d * sin_a, 32, 1) + pltpu.roll(d * sin_b, 96, 1)


def _shift_down(g, first_row):
    row = lax.broadcasted_iota(jnp.int32, g.shape, 0)
    return jnp.where(row == 0, first_row, pltpu.roll(g, 1, 0))


def _shift_up(g, last_row):
    n = g.shape[0]
    row = lax.broadcasted_iota(jnp.int32, g.shape, 0)
    return jnp.where(row == n - 1, last_row, pltpu.roll(g, n - 1, 0))


def _row_spec(r, w, col=0):
    return pl.BlockSpec((r, w), lambda i: (i, col))


def _const_spec(shape):
    nd = len(shape)
    return pl.BlockSpec(shape, lambda i: (0,) * nd)


def _col_block(off, w):
    assert off % w == 0, (off, w)
    return off // w


def _zero_at_first_step(refs):
    @pl.when(pl.program_id(0) == 0)
    def _():
        for ref in refs:
            ref[...] = jnp.zeros(ref.shape, ref.dtype)


def _row_call(body, name, t, r, in_specs, args, out_specs, out_shapes, scratch=(), into=None):
    if into is None:
        return pl.pallas_call(
            body, name=name, grid=(t // r,), in_specs=in_specs, out_specs=out_specs, out_shape=out_shapes,
            scratch_shapes=list(scratch), compiler_params=_params(("arbitrary",)),
        )(*args)
    buf, col = into
    single = not isinstance(out_specs, (list, tuple))
    specs = [out_specs] if single else list(out_specs)
    shapes = [out_shapes] if single else list(out_shapes)
    width = shapes[0].shape[1]
    assert shapes[0].dtype == buf.dtype and buf.shape[0] == t
    specs[0] = _row_spec(r, width, col)
    shapes[0] = jax.ShapeDtypeStruct(buf.shape, buf.dtype)
    n_in = len(args)

    def body_in_place(*refs):
        body(*refs[:n_in], *refs[n_in + 1:])

    res = pl.pallas_call(
        body_in_place, name=name, grid=(t // r,), in_specs=list(in_specs) + [_ANY], out_specs=specs, out_shape=shapes,
        scratch_shapes=list(scratch), input_output_aliases={n_in: 0}, compiler_params=_params(("arbitrary",)),
    )(*args, buf)
    return res[0] if single else res


def _norm_fwd(h, g, name):
    t, d = h.shape
    r = _pick(t, (256, 128))

    def body(h_ref, g_ref, o_ref):
        xhat, _ = _rms(h_ref[...], d)
        o_ref[...] = (xhat * g_ref[...]).astype(BF16)

    return _row_call(body, name, t, r, [_row_spec(r, d), _const_spec((1, d))], (h, g),
                     _row_spec(r, d), jax.ShapeDtypeStruct((t, d), BF16))


def _norm_bwd(h, g, d_hn, d_res, name):
    t, d = h.shape
    r = _pick(t, (256, 128))

    def body(h_ref, g_ref, dy_ref, dres_ref, dh_ref, dhb_ref, dg_ref):
        _zero_at_first_step([dg_ref])
        xhat, rr = _rms(h_ref[...], d)
        dy = dy_ref[...].astype(F32)
        dg_ref[...] += _colsum(dy * xhat)
        dh = dres_ref[...] + _rms_bwd(dy * g_ref[...], xhat, rr, d)
        dh_ref[...] = dh
        dhb_ref[...] = dh.astype(BF16)

    return _row_call(body, name, t, r,
                     [_row_spec(r, d), _const_spec((1, d)), _row_spec(r, d), _row_spec(r, d)], (h, g, d_hn, d_res),
                     [_row_spec(r, d), _row_spec(r, d), _const_spec((1, d))],
                     [jax.ShapeDtypeStruct((t, d), F32), jax.ShapeDtypeStruct((t, d), BF16),
                      jax.ShapeDtypeStruct((1, d), F32)])


def _sgu_scores(v, gs_ref, ws_ref, bb_ref, s_scr, r, ah, keep=None):
    for kk in range(r // HEAD):
        for hh in range(ah):
            rows, cols = slice(kk * HEAD, (kk + 1) * HEAD), slice(hh * HEAD, (hh + 1) * HEAD)
            vhat, rv = _rms(v[rows, cols], HEAD)
            vn = vhat * gs_ref[pl.ds(hh, 1), :]
            s_scr[rows, cols] = jnp.dot(ws_ref[hh], vn.astype(BF16), preferred_element_type=F32) + bb_ref[hh]
            if keep is not None:
                keep[(kk, hh)] = (vhat, rv, vn)


def _sgu_fwd(proj, off, aw, gs, ws, bb, ga, name, into=None):
    t = proj.shape[0]
    ah = aw // HEAD
    r = _pick(t, (256, 128))
    cb = _col_block(off, aw)

    def body(u_ref, v_ref, z_ref, gs_ref, ws_ref, bb_ref, ga_ref, o_ref, s_scr):
        _sgu_scores(v_ref[...].astype(F32), gs_ref, ws_ref, bb_ref, s_scr, r, ah)
        sil, _ = _silu_and_grad(z_ref[...].astype(F32))
        yhat, _ = _rms(u_ref[...].astype(F32) * s_scr[...] * sil, aw)
        o_ref[...] = (yhat * ga_ref[...]).astype(BF16)

    return _row_call(
        body, name, t, r,
        [_row_spec(r, aw, cb), _row_spec(r, aw, cb + 1), _row_spec(r, aw, cb + 2), _const_spec((ah, HEAD)),
         _const_spec((ah, HEAD, HEAD)), _const_spec((ah, HEAD, HEAD)), _const_spec((1, aw))],
        (proj, proj, proj, gs, ws, bb, ga),
        _row_spec(r, aw), jax.ShapeDtypeStruct((t, aw), BF16), scratch=[pltpu.VMEM((r, aw), F32)], into=into)


def _sgu_bwd(proj, off, aw, gs, ws, ws_t, bb, ga, dy, name):
    t = proj.shape[0]
    ah = aw // HEAD
    r = _pick(t, (256, 128))
    cb = _col_block(off, aw)

    def body(u_ref, v_ref, z_ref, gs_ref, ws_ref, wst_ref, bb_ref, ga_ref, dy_ref,
             d_ref, dgs_ref, dws_ref, db_ref, dga_ref, s_scr, dv_scr):
        _zero_at_first_step([dgs_ref, dws_ref, db_ref, dga_ref])
        keep = {}
        _sgu_scores(v_ref[...].astype(F32), gs_ref, ws_ref, bb_ref, s_scr, r, ah, keep)
        u, z, s = u_ref[...].astype(F32), z_ref[...].astype(F32), s_scr[...]
        sil, dsil = _silu_and_grad(z)
        yhat, rr = _rms(u * s * sil, aw)
        dy_f = dy_ref[...].astype(F32)
        dga_ref[...] += _colsum(dy_f * yhat)
        dya = _rms_bwd(dy_f * ga_ref[...], yhat, rr, aw)
        d_ref[:, 0:aw] = (dya * s * sil).astype(BF16)
        d_ref[:, 2 * aw:3 * aw] = (dya * u * s * dsil).astype(BF16)
        ds = dya * u * sil
        for kk in range(r // HEAD):
            for hh in range(ah):
                rows, cols = slice(kk * HEAD, (kk + 1) * HEAD), slice(hh * HEAD, (hh + 1) * HEAD)
                vhat, rv, vn = keep[(kk, hh)]
                ds_blk = ds[rows, cols]
                db_ref[hh] += jnp.sum(ds_blk, axis=1, keepdims=True)
                ds_b = ds_blk.astype(BF16)
                dws_ref[hh] += lax.dot_general(ds_b, vn.astype(BF16), (((1,), (1,)), ((), ())),
                                               preferred_element_type=F32)
                dvn = jnp.dot(wst_ref[hh], ds_b, preferred_element_type=F32)
                dgs_ref[pl.ds(hh, 1), :] += _colsum(dvn * vhat)
                dv_scr[rows, cols] = _rms_bwd(dvn * gs_ref[pl.ds(hh, 1), :], vhat, rv, HEAD)
        d_ref[:, aw:2 * aw] = dv_scr[...].astype(BF16)

    return _row_call(
        body, name, t, r,
        [_row_spec(r, aw, cb), _row_spec(r, aw, cb + 1), _row_spec(r, aw, cb + 2), _const_spec((ah, HEAD)),
         _const_spec((ah, HEAD, HEAD)), _const_spec((ah, HEAD, HEAD)), _const_spec((ah, HEAD, HEAD)),
         _const_spec((1, aw)), _row_spec(r, aw, 0)],
        (proj, proj, proj, gs, ws, ws_t, bb, ga, dy),
        [_row_spec(r, 3 * aw), _const_spec((ah, HEAD)), _const_spec((ah, HEAD, HEAD)), _const_spec((ah, HEAD, 1)),
         _const_spec((1, aw))],
        [jax.ShapeDtypeStruct((t, 3 * aw), BF16), jax.ShapeDtypeStruct((ah, HEAD), F32),
         jax.ShapeDtypeStruct((ah, HEAD, HEAD), F32), jax.ShapeDtypeStruct((ah, HEAD, 1), F32),
         jax.ShapeDtypeStruct((1, aw), F32)],
        scratch=[pltpu.VMEM((r, aw), F32), pltpu.VMEM((r, aw), F32)])


def _halo_specs(t, r, w, col, rows):
    per = r // rows
    last = t // rows - 1
    prev = pl.BlockSpec((rows, w), lambda i: (jnp.maximum(i * per - 1, 0), col))
    nxt = pl.BlockSpec((rows, w), lambda i: (jnp.minimum((i + 1) * per, last), col))
    return prev, nxt


def _edge_rows(prev_ref, next_ref, n_steps):
    i = pl.program_id(0)
    rows = prev_ref.shape[0]
    before = prev_ref[...].astype(F32)[rows - 1:rows, :] * (i > 0).astype(F32)
    after = next_ref[...].astype(F32)[0:1, :] * (i < n_steps - 1).astype(F32)
    return before, after


def _conv_fwd(proj, off, bw, cw, cb_, gb, name, into=None):
    t = proj.shape[0]
    r = _pick(t, (256, 128))
    n_steps = t // r
    c0 = _col_block(off, bw)
    cp, cn = _halo_specs(t, r, bw, c0 + 1, HALO_ROWS)
    hp, hn = _halo_specs(t, r, bw, c0 + 2, HALO_ROWS)

    def body(b_ref, c_ref, h_ref, z_ref, cp_ref, cn_ref, hp_ref, hn_ref, cw_ref, cb_ref, gb_ref, o_ref, yc_ref):
        g = c_ref[...].astype(F32) * h_ref[...].astype(F32)
        c_before, c_after = _edge_rows(cp_ref, cn_ref, n_steps)
        h_before, h_after = _edge_rows(hp_ref, hn_ref, n_steps)
        yconv = (cb_ref[...] + cw_ref[0:1, :] * _shift_down(g, c_before * h_before) + cw_ref[1:2, :] * g
                 + cw_ref[2:3, :] * _shift_up(g, c_after * h_after))
        yc_ref[...] = yconv
        sil, _ = _silu_and_grad(z_ref[...].astype(F32))
        yhat, _ = _rms(b_ref[...].astype(F32) * yconv * sil, bw)
        o_ref[...] = (yhat * gb_ref[...]).astype(BF16)

    return _row_call(
        body, name, t, r,
        [_row_spec(r, bw, c0), _row_spec(r, bw, c0 + 1), _row_spec(r, bw, c0 + 2), _row_spec(r, bw, c0 + 3),
         cp, cn, hp, hn, _const_spec((CONV_TAPS, bw)), _const_spec((1, bw)), _const_spec((1, bw))],
        (proj, proj, proj, proj, proj, proj, proj, proj, cw, cb_, gb),
        [_row_spec(r, bw), _row_spec(r, bw)],
        [jax.ShapeDtypeStruct((t, bw), BF16), jax.ShapeDtypeStruct((t, bw), F32)], into=into)


def _conv_bwd_gate(proj, off, bw, yconv, gb, dy, name):
    t = proj.shape[0]
    r = _pick(t, (256, 128))
    c0 = _col_block(off, bw)

    def body(b_ref, z_ref, yc_ref, gb_ref, dy_ref, dyc_ref, db_ref, dz_ref, dgb_ref, dcb_ref):
        _zero_at_first_step([dgb_ref, dcb_ref])
        b, z, yconv_v = b_ref[...].astype(F32), z_ref[...].astype(F32), yc_ref[...]
        sil, dsil = _silu_and_grad(z)
        yhat, rr = _rms(b * yconv_v * sil, bw)
        dy_f = dy_ref[...].astype(F32)
        dgb_ref[...] += _colsum(dy_f * yhat)
        dyb = _rms_bwd(dy_f * gb_ref[...], yhat, rr, bw)
        dyc = dyb * b * sil
        dyc_ref[...] = dyc
        dcb_ref[...] += _colsum(dyc)
        db_ref[...] = (dyb * yconv_v * sil).astype(BF16)
        dz_ref[...] = (dyb * b * yconv_v * dsil).astype(BF16)

    return _row_call(
        body, name, t, r,
        [_row_spec(r, bw, c0), _row_spec(r, bw, c0 + 3), _row_spec(r, bw), _const_spec((1, bw)), _row_spec(r, bw, 1)],
        (proj, proj, yconv, gb, dy),
        [_row_spec(r, bw), _row_spec(r, bw), _row_spec(r, bw), _const_spec((1, bw)), _const_spec((1, bw))],
        [jax.ShapeDtypeStruct((t, bw), F32), jax.ShapeDtypeStruct((t, bw), BF16), jax.ShapeDtypeStruct((t, bw), BF16),
         jax.ShapeDtypeStruct((1, bw), F32), jax.ShapeDtypeStruct((1, bw), F32)])


def _conv_bwd_taps(proj, off, bw, dyc, cw, d_gate_b, d_gate_z, name):
    t = proj.shape[0]
    r = _pick(t, (256, 128))
    n_steps = t // r
    c0 = _col_block(off, bw)
    cp, cn = _halo_specs(t, r, bw, c0 + 1, HALO_ROWS)
    hp, hn = _halo_specs(t, r, bw, c0 + 2, HALO_ROWS)
    dp, dn = _halo_specs(t, r, bw, 0, 8)

    def body(c_ref, h_ref, cp_ref, cn_ref, hp_ref, hn_ref, d_ref, dp_ref, dn_ref, cw_ref, dgb_ref, dgz_ref,
             db_ref, dcw_ref):
        _zero_at_first_step([dcw_ref])
        c, h, d = c_ref[...].astype(F32), h_ref[...].astype(F32), d_ref[...]
        g = c * h
        c_before, c_after = _edge_rows(cp_ref, cn_ref, n_steps)
        h_before, h_after = _edge_rows(hp_ref, hn_ref, n_steps)
        d_before, d_after = _edge_rows(dp_ref, dn_ref, n_steps)
        dg = (cw_ref[0:1, :] * _shift_up(d, d_after) + cw_ref[1:2, :] * d + cw_ref[2:3, :] * _shift_down(d, d_before))
        db_ref[:, 0:bw] = dgb_ref[...]
        db_ref[:, bw:2 * bw] = (dg * h).astype(BF16)
        db_ref[:, 2 * bw:3 * bw] = (dg * c).astype(BF16)
        db_ref[:, 3 * bw:4 * bw] = dgz_ref[...]
        dcw_ref[0:1, :] += _colsum(d * _shift_down(g, c_before * h_before))
        dcw_ref[1:2, :] += _colsum(d * g)
        dcw_ref[2:3, :] += _colsum(d * _shift_up(g, c_after * h_after))

    return _row_call(
        body, name, t, r,
        [_row_spec(r, bw, c0 + 1), _row_spec(r, bw, c0 + 2), cp, cn, hp, hn, _row_spec(r, bw), dp, dn,
         _const_spec((CONV_TAPS, bw)), _row_spec(r, bw), _row_spec(r, bw)],
        (proj, proj, proj, proj, proj, proj, dyc, dyc, dyc, cw, d_gate_b, d_gate_z),
        [_row_spec(r, 4 * bw), _const_spec((CONV_TAPS, bw))],
        [jax.ShapeDtypeStruct((t, 4 * bw), BF16), jax.ShapeDtypeStruct((CONV_TAPS, bw), F32)])


def _mla_prep_fwd(proj, q_off, ckv_off, kr_off, ch, kvr, tabs, qn_g, qr_g, kr_g, kv_g, name):
    t = proj.shape[0]
    r = _pick(t, (256, 128))
    qw = ch * 2 * HEAD
    cos_t, sin_a, sin_b = tabs

    def body(q_ref, ckv_ref, kr_ref, cos_ref, sa_ref, sb_ref, qn_ref, qr_ref, krg_ref, kvg_ref,
             qo_ref, co_ref, ko_ref):
        cos_v, sa, sb = cos_ref[...], sa_ref[...], sb_ref[...]
        for hh in range(ch):
            lo = hh * 2 * HEAD
            nhat, _ = _rms(q_ref[:, lo:lo + HEAD].astype(F32), HEAD)
            qo_ref[:, lo:lo + HEAD] = (nhat * qn_ref[...]).astype(BF16)
            rhat, _ = _rms(q_ref[:, lo + HEAD:lo + 2 * HEAD].astype(F32), ROPE)
            qo_ref[:, lo + HEAD:lo + 2 * HEAD] = _rope(rhat * qr_ref[...], cos_v, sa, sb).astype(BF16)
        khat, _ = _rms(kr_ref[...].astype(F32), ROPE)
        ko_ref[...] = _rope(khat * krg_ref[...], cos_v, sa, sb).astype(BF16)
        chat, _ = _rms(ckv_ref[...].astype(F32), kvr)
        co_ref[...] = (chat * kvg_ref[...]).astype(BF16)

    tab = _row_spec(r, HEAD)
    gain = _const_spec((1, HEAD))
    return _row_call(
        body, name, t, r,
        [_row_spec(r, qw, _col_block(q_off, qw)), _row_spec(r, kvr, _col_block(ckv_off, kvr)),
         _row_spec(r, HEAD, _col_block(kr_off, HEAD)), tab, tab, tab, gain, gain, gain, _const_spec((1, kvr))],
        (proj, proj, proj, cos_t, sin_a, sin_b, qn_g, qr_g, kr_g, kv_g),
        [_row_spec(r, qw), _row_spec(r, kvr), _row_spec(r, HEAD)],
        [jax.ShapeDtypeStruct((t, qw), BF16), jax.ShapeDtypeStruct((t, kvr), BF16),
         jax.ShapeDtypeStruct((t, HEAD), BF16)])


def _mla_prep_bwd(proj, q_off, ckv_off, kr_off, ch, kvr, tabs, qn_g, qr_g, kr_g, kv_g, dq_cat_t, dq_scale, dckv_n,
                  dkr_rot, kr_width, name):
    t = proj.shape[0]
    r = _pick(t, (256, 128))
    qw = ch * 2 * HEAD
    cos_t, sin_a, sin_b = tabs

    def body(q_ref, ckv_ref, kr_ref, cos_ref, sa_ref, sb_ref, qn_ref, qr_ref, krg_ref, kvg_ref,
             dq_ref, dc_ref, dk_ref, dqo_ref, dco_ref, dko_ref, dqn_ref, dqr_ref, dkrg_ref, dkvg_ref):
        _zero_at_first_step([dqn_ref, dqr_ref, dkrg_ref, dkvg_ref])
        cos_v, sa, sb = cos_ref[...], sa_ref[...], sb_ref[...]
        dq = dq_ref[...].T * dq_scale
        for hh in range(ch):
            lo = hh * 2 * HEAD
            nhat, nr = _rms(q_ref[:, lo:lo + HEAD].astype(F32), HEAD)
            d_n = dq[:, lo:lo + HEAD]
            dqn_ref[...] += _colsum(d_n * nhat)
            dqo_ref[:, lo:lo + HEAD] = _rms_bwd(d_n * qn_ref[...], nhat, nr, HEAD).astype(BF16)
            rhat, rr = _rms(q_ref[:, lo + HEAD:lo + 2 * HEAD].astype(F32), ROPE)
            d_t = _rope_bwd(dq[:, lo + HEAD:lo + 2 * HEAD], cos_v, sa, sb)
            dqr_ref[...] += _colsum(d_t * rhat)
            dqo_ref[:, lo + HEAD:lo + 2 * HEAD] = _rms_bwd(d_t * qr_ref[...], rhat, rr, ROPE).astype(BF16)
        khat, kr_r = _rms(kr_ref[...].astype(F32), ROPE)
        d_k = _rope_bwd(dk_ref[...], cos_v, sa, sb)
        dkrg_ref[...] += _colsum(d_k * khat)
        dko_ref[:, 0:HEAD] = _rms_bwd(d_k * krg_ref[...], khat, kr_r, ROPE).astype(BF16)
        if kr_width > HEAD:
            dko_ref[:, HEAD:kr_width] = jnp.zeros((r, kr_width - HEAD), BF16)
        chat, cr = _rms(ckv_ref[...].astype(F32), kvr)
        d_c = dc_ref[...].astype(F32)
        dkvg_ref[...] += _colsum(d_c * chat)
        dco_ref[...] = _rms_bwd(d_c * kvg_ref[...], chat, cr, kvr).astype(BF16)

    tab = _row_spec(r, HEAD)
    gain = _const_spec((1, HEAD))
    return _row_call(
        body, name, t, r,
        [_row_spec(r, qw, _col_block(q_off, qw)), _row_spec(r, kvr, _col_block(ckv_off, kvr)),
         _row_spec(r, HEAD, _col_block(kr_off, HEAD)), tab, tab, tab, gain, gain, gain, _const_spec((1, kvr)),
         pl.BlockSpec((qw, r), lambda i: (0, i)), _row_spec(r, kvr), _row_spec(r, HEAD)],
        (proj, proj, proj, cos_t, sin_a, sin_b, qn_g, qr_g, kr_g, kv_g, dq_cat_t, dckv_n, dkr_rot),
        [_row_spec(r, qw), _row_spec(r, kvr), _row_spec(r, kr_width), gain, gain, gain, _const_spec((1, kvr))],
        [jax.ShapeDtypeStruct((t, qw), BF16), jax.ShapeDtypeStruct((t, kvr), BF16),
         jax.ShapeDtypeStruct((t, kr_width), BF16), jax.ShapeDtypeStruct((1, HEAD), F32),
         jax.ShapeDtypeStruct((1, HEAD), F32), jax.ShapeDtypeStruct((1, HEAD), F32),
         jax.ShapeDtypeStruct((1, kvr), F32)])


def _kv_prep_fwd(kv, kr_rot, ch, kn_g, name):
    t = kv.shape[0]
    r = _pick(t, (256, 128))
    qw = ch * 2 * HEAD

    def body(kv_ref, kr_ref, kn_ref, ko_ref, kt_ref, vo_ref):
        ones = jnp.ones((r, HEAD), BF16)
        for hh in range(ch):
            lo = hh * 2 * HEAD
            nhat, _ = _rms(kv_ref[:, lo:lo + HEAD], HEAD)
            ko_ref[:, lo:lo + HEAD] = (nhat * kn_ref[...]).astype(BF16)
            ko_ref[:, lo + HEAD:lo + 2 * HEAD] = kr_ref[...]
            vo_ref[:, lo:lo + HEAD] = kv_ref[:, lo + HEAD:lo + 2 * HEAD].astype(BF16)
            vo_ref[:, lo + HEAD:lo + 2 * HEAD] = ones
        kt_ref[...] = ko_ref[...].astype(F32).T.astype(BF16)

    return _row_call(
        body, name, t, r, [_row_spec(r, qw), _row_spec(r, HEAD), _const_spec((1, HEAD))], (kv, kr_rot, kn_g),
        [_row_spec(r, qw), pl.BlockSpec((qw, r), lambda i: (0, i)), _row_spec(r, qw)],
        [jax.ShapeDtypeStruct((t, qw), BF16), jax.ShapeDtypeStruct((qw, t), BF16),
         jax.ShapeDtypeStruct((t, qw), BF16)])


def _kv_prep_bwd(kv, ch, kn_g, dk_cat, dv, name):
    t = kv.shape[0]
    r = _pick(t, (256, 128))
    qw = ch * 2 * HEAD

    def body(kv_ref, kn_ref, dk_ref, dv_ref, dkv_ref, dkr_ref, dkn_ref):
        _zero_at_first_step([dkn_ref])
        dkr = jnp.zeros((r, HEAD), F32)
        for hh in range(ch):
            lo = hh * 2 * HEAD
            nhat, nr = _rms(kv_ref[:, lo:lo + HEAD], HEAD)
            d_n = dk_ref[:, lo:lo + HEAD].astype(F32)
            dkn_ref[...] += _colsum(d_n * nhat)
            dkv_ref[:, lo:lo + HEAD] = _rms_bwd(d_n * kn_ref[...], nhat, nr, HEAD).astype(BF16)
            dkv_ref[:, lo + HEAD:lo + 2 * HEAD] = dv_ref[:, hh * HEAD:(hh + 1) * HEAD]
            dkr = dkr + dk_ref[:, lo + HEAD:lo + 2 * HEAD].astype(F32)
        dkr_ref[...] = dkr

    return _row_call(
        body, name, t, r,
        [_row_spec(r, qw), _const_spec((1, HEAD)), _row_spec(r, qw), _row_spec(r, ch * HEAD)], (kv, kn_g, dk_cat, dv),
        [_row_spec(r, qw), _row_spec(r, HEAD), _const_spec((1, HEAD))],
        [jax.ShapeDtypeStruct((t, qw), BF16), jax.ShapeDtypeStruct((t, HEAD), F32),
         jax.ShapeDtypeStruct((1, HEAD), F32)])


def _attn_tiles(t):
    return _pick(t, (2048, 1024, 512, 256, 128)), _pick(t, (1024, 512, 256, 128))


_NT = (((1,), (1,)), ((), ()))
LOG2E = 1.4426950408889634


def _attn_fwd(q_cat, k_cat, v_aug, ch, scale, name):
    t = q_cat.shape[0]
    tq, tk = _attn_tiles(t)
    nk = t // tk
    c2 = scale * LOG2E

    def body(q_ref, k_ref, v_ref, o_ref, lse_ref, s_scr, m_scr, acc_scr):
        j = pl.program_id(2)

        def scores(slot):
            s_scr[slot] = lax.dot_general(q_ref[...], k_ref[...], _NT, preferred_element_type=F32) * c2

        def absorb(slot):
            s = s_scr[slot]
            m_old = m_scr[...]
            m_new = jnp.maximum(m_old, jnp.max(s, axis=-1, keepdims=True))
            p = jnp.exp2(s - m_new).astype(BF16)
            acc_scr[...] = (jnp.exp2(m_old - m_new) * acc_scr[...]
                            + jnp.dot(p, v_ref[...], preferred_element_type=F32))
            m_scr[...] = m_new

        @pl.when(j == 0)
        def _():
            m_scr[...] = jnp.full(m_scr.shape, -jnp.inf, F32)
            acc_scr[...] = jnp.zeros(acc_scr.shape, F32)
            scores(0)

        for parity in (0, 1):
            @pl.when((j > 0) & (j < nk) & (j % 2 == parity))
            def _():
                scores(parity)
                absorb(1 - parity)

        @pl.when(j == nk)
        def _():
            absorb((nk - 1) % 2)
            acc = acc_scr[...]
            l_sum = acc[:, HEAD:]
            o_ref[...] = (acc[:, :HEAD] / l_sum).astype(BF16)
            lse_ref[0] = m_scr[...] + jnp.log(l_sum[:, 0:1]) * LOG2E

    return pl.pallas_call(
        body, name=name, grid=(ch, t // tq, nk + 1),
        in_specs=[pl.BlockSpec((tq, 2 * HEAD), lambda h, i, j: (i, h)),
                  pl.BlockSpec((tk, 2 * HEAD), lambda h, i, j: (jnp.minimum(j, nk - 1), h)),
                  pl.BlockSpec((tk, 2 * HEAD), lambda h, i, j: (jnp.maximum(j - 1, 0), h))],
        out_specs=[pl.BlockSpec((tq, HEAD), lambda h, i, j: (i, h)),
                   pl.BlockSpec((1, tq, 1), lambda h, i, j: (h, i, 0))],
        out_shape=[jax.ShapeDtypeStruct((t, ch * HEAD), BF16), jax.ShapeDtypeStruct((ch, t, 1), F32)],
        scratch_shapes=[pltpu.VMEM((2, tq, tk), F32), pltpu.VMEM((tq, 1), F32), pltpu.VMEM((tq, 2 * HEAD), F32)],
        compiler_params=_params(("parallel", "parallel", "arbitrary")),
    )(q_cat, k_cat, v_aug)


def _attn_bwd(q_cat, k_cat, k_cat_t, v_aug, do, lse_row, d_row, ch, scale, name):
    t = q_cat.shape[0]
    tk = _pick(t, (1024, 512, 256, 128))
    tq = _pick(t, (512, 256, 128))
    nk, nq = t // tk, t // tq
    c2 = scale * LOG2E

    def body(q_ref, do_ref, qp_ref, dop_ref, lse_ref, d_ref, k_ref, kt_ref, v_ref,
             dqt_ref, dk_ref, dv_ref, s_scr, dp_scr, dk_scr, dv_scr):
        ki, j = pl.program_id(1), pl.program_id(2)

        def products(slot):
            s_scr[slot] = lax.dot_general(k_ref[...], q_ref[...], _NT, preferred_element_type=F32) * c2
            dp_scr[slot] = lax.dot_general(v_ref[...], do_ref[...], _NT, preferred_element_type=F32)

        def absorb(slot):
            q, do_v = qp_ref[...], dop_ref[...]
            pt = jnp.exp2(s_scr[slot] - lse_ref[0])
            dv_scr[...] += jnp.dot(pt.astype(BF16), do_v, preferred_element_type=F32)
            dst = (pt * (dp_scr[slot] - d_ref[0])).astype(BF16)
            dk_scr[...] += jnp.dot(dst, q, preferred_element_type=F32)
            part = jnp.dot(kt_ref[...], dst, preferred_element_type=F32)
            cols = pl.ds(pl.multiple_of((j - 1) * tq, tq), tq)

            @pl.when(ki == 0)
            def _():
                dqt_ref[:, cols] = part

            @pl.when(ki > 0)
            def _():
                dqt_ref[:, cols] += part

        @pl.when(j == 0)
        def _():
            dk_scr[...] = jnp.zeros(dk_scr.shape, F32)
            dv_scr[...] = jnp.zeros(dv_scr.shape, F32)
            products(0)

        for parity in (0, 1):
            @pl.when((j > 0) & (j < nq) & (j % 2 == parity))
            def _():
                products(parity)
                absorb(1 - parity)

        @pl.when(j == nq)
        def _():
            absorb((nq - 1) % 2)
            dk_ref[...] = (dk_scr[...] * scale).astype(BF16)
            dv_ref[...] = dv_scr[...].astype(BF16)

    def cur(i):
        return jnp.minimum(i, nq - 1)

    def prev(i):
        return jnp.maximum(i - 1, 0)

    stat = pl.BlockSpec((1, 1, tq), lambda h, j, i: (h, 0, prev(i)))
    return pl.pallas_call(
        body, name=name, grid=(ch, nk, nq + 1),
        in_specs=[pl.BlockSpec((tq, 2 * HEAD), lambda h, j, i: (cur(i), h)),
                  pl.BlockSpec((tq, HEAD), lambda h, j, i: (cur(i), h)),
                  pl.BlockSpec((tq, 2 * HEAD), lambda h, j, i: (prev(i), h)),
                  pl.BlockSpec((tq, HEAD), lambda h, j, i: (prev(i), h)), stat, stat,
                  pl.BlockSpec((tk, 2 * HEAD), lambda h, j, i: (j, h)),
                  pl.BlockSpec((2 * HEAD, tk), lambda h, j, i: (h, j)),
                  pl.BlockSpec((tk, HEAD), lambda h, j, i: (j, 2 * h))],
        out_specs=[pl.BlockSpec((2 * HEAD, t), lambda h, j, i: (h, 0)),
                   pl.BlockSpec((tk, 2 * HEAD), lambda h, j, i: (j, h)),
                   pl.BlockSpec((tk, HEAD), lambda h, j, i: (j, h))],
        out_shape=[jax.ShapeDtypeStruct((ch * 2 * HEAD, t), F32), jax.ShapeDtypeStruct((t, ch * 2 * HEAD), BF16),
                   jax.ShapeDtypeStruct((t, ch * HEAD), BF16)],
        scratch_shapes=[pltpu.VMEM((2, tk, tq), F32), pltpu.VMEM((2, tk, tq), F32),
                        pltpu.VMEM((tk, 2 * HEAD), F32), pltpu.VMEM((tk, HEAD), F32)],
        compiler_params=_params(("parallel", "arbitrary", "arbitrary")),
    )(q_cat, do, q_cat, do, lse_row, d_row, k_cat, k_cat_t, v_aug)


def _attn_post_fwd(o, proj, z_off, cw, gc, name, into=None):
    t = o.shape[0]
    r = _pick(t, (256, 128))

    def body(o_ref, z_ref, gc_ref, y_ref):
        sil, _ = _silu_and_grad(z_ref[...].astype(F32))
        yhat, _ = _rms(o_ref[...].astype(F32) * sil, cw)
        y_ref[...] = (yhat * gc_ref[...]).astype(BF16)

    return _row_call(body, name, t, r,
                     [_row_spec(r, cw), _row_spec(r, cw, _col_block(z_off, cw)), _const_spec((1, cw))], (o, proj, gc),
                     _row_spec(r, cw), jax.ShapeDtypeStruct((t, cw), BF16), into=into)


def _attn_post_bwd(o, proj, z_off, cw, gc, dy, dy_col, name):
    t = o.shape[0]
    ch = cw // HEAD
    r = _pick(t, (256, 128))

    def body(o_ref, z_ref, gc_ref, dy_ref, do_ref, dz_ref, ds_ref, dgc_ref):
        _zero_at_first_step([dgc_ref])
        o_v, z = o_ref[...].astype(F32), z_ref[...].astype(F32)
        sil, dsil = _silu_and_grad(z)
        yhat, rr = _rms(o_v * sil, cw)
        dy_f = dy_ref[...].astype(F32)
        dgc_ref[...] += _colsum(dy_f * yhat)
        dyc = _rms_bwd(dy_f * gc_ref[...], yhat, rr, cw)
        do_b = (dyc * sil).astype(BF16)
        do_ref[...] = do_b
        dz_ref[...] = (dyc * o_v * dsil).astype(BF16)
        prod = do_b.astype(F32) * o_v
        for hh in range(ch):
            ds_ref[hh] = jnp.sum(prod[:, hh * HEAD:(hh + 1) * HEAD], axis=-1, keepdims=True)

    return _row_call(
        body, name, t, r,
        [_row_spec(r, cw), _row_spec(r, cw, _col_block(z_off, cw)), _const_spec((1, cw)), _row_spec(r, cw, dy_col)],
        (o, proj, gc, dy),
        [_row_spec(r, cw), _row_spec(r, cw), pl.BlockSpec((ch, r, 1), lambda i: (0, i, 0)), _const_spec((1, cw))],
        [jax.ShapeDtypeStruct((t, cw), BF16), jax.ShapeDtypeStruct((t, cw), BF16),
         jax.ShapeDtypeStruct((ch, t, 1), F32), jax.ShapeDtypeStruct((1, cw), F32)])


def _ple_fwd(h1, gpre, pp, name):
    t, d = h1.shape
    r = _pick(t, (256, 128))

    def body(h_ref, g_ref, p_ref, o_ref):
        o_ref[...] = h_ref[...] + _sigmoid(g_ref[...]) * p_ref[...]

    return _row_call(body, name, t, r, [_row_spec(r, d)] * 3, (h1, gpre, pp), _row_spec(r, d),
                     jax.ShapeDtypeStruct((t, d), F32))


def _ple_bwd(gpre, pp, dh, name):
    t, d = dh.shape
    r = _pick(t, (256, 128))

    def body(g_ref, p_ref, dh_ref, dg_ref, dp_ref):
        sig = _sigmoid(g_ref[...])
        dh_v = dh_ref[...]
        dg_ref[...] = (dh_v * p_ref[...] * sig * (1.0 - sig)).astype(BF16)
        dp_ref[...] = (dh_v * sig).astype(BF16)

    return _row_call(body, name, t, r, [_row_spec(r, d)] * 3, (gpre, pp, dh), [_row_spec(r, d)] * 2,
                     [jax.ShapeDtypeStruct((t, d), BF16)] * 2)


def _loss_and_grad(h, target, name):
    t, d = h.shape
    r = _pick(t, (256, 128))

    def body(h_ref, t_ref, l_ref, dh_ref):
        _zero_at_first_step([l_ref])
        err = h_ref[...] - t_ref[...]
        l_ref[...] += jnp.sum(jnp.sum(err * err, axis=-1, keepdims=True), axis=0, keepdims=True) * (0.5 / d)
        dh_ref[...] = err * (1.0 / d)

    return _row_call(body, name, t, r, [_row_spec(r, d)] * 2, (h, target), [_const_spec((1, 1)), _row_spec(r, d)],
                     [jax.ShapeDtypeStruct((1, 1), F32), jax.ShapeDtypeStruct((t, d), F32)])


def _ew_rows(rows, cols):
    cap = max(8, (1 << 19) // max(cols, 1))
    for cand in (1024, 512, 256, 128, 64, 32, 16, 8):
        if cand <= cap and rows % cand == 0:
            return cand
    return rows


def _pair_sum_bf16(a, b, name):
    n, rows, cols = a.shape
    rb = _ew_rows(rows, cols)

    def body(a_ref, b_ref, o_ref):
        o_ref[...] = (a_ref[...] + b_ref[...]).astype(BF16)

    spec = pl.BlockSpec((1, rb, cols), lambda s, i: (s, i, 0))
    return pl.pallas_call(body, name=name, grid=(n, rows // rb), in_specs=[spec, spec], out_specs=spec,
                          out_shape=jax.ShapeDtypeStruct(a.shape, BF16),
                          compiler_params=_params(("parallel", "parallel")))(a, b)


def _shard_sum(a, b, recv, name):
    rows, cols = a.shape
    rb = _ew_rows(rows, cols)

    def body(a_ref, b_ref, r_ref, o_ref):
        o_ref[...] = ((a_ref[...] + b_ref[...]) + r_ref[0].astype(F32) + r_ref[1].astype(F32)
                      + r_ref[2].astype(F32))

    spec = pl.BlockSpec((rb, cols), lambda i: (i, 0))
    return pl.pallas_call(body, name=name, grid=(rows // rb,),
                          in_specs=[spec, spec, pl.BlockSpec((N_SHARD - 1, rb, cols), lambda i: (0, i, 0))],
                          out_specs=spec, out_shape=jax.ShapeDtypeStruct(a.shape, F32),
                          compiler_params=_params(("parallel",)))(a, b, recv)


def _sum_devices(g, name):
    n, rows, cols = g.shape
    rb = _ew_rows(rows, cols)

    def body(g_ref, o_ref):
        acc = g_ref[0]
        for k in range(1, n):
            acc = acc + g_ref[k]
        o_ref[...] = acc

    return pl.pallas_call(body, name=name, grid=(rows // rb,),
                          in_specs=[pl.BlockSpec((n, rb, cols), lambda i: (0, i, 0))],
                          out_specs=pl.BlockSpec((rb, cols), lambda i: (i, 0)),
                          out_shape=jax.ShapeDtypeStruct((rows, cols), F32),
                          compiler_params=_params(("parallel",)))(g)


def _adamw_update(w, g_v, m, v):
    m_new = ADAM_B1 * m + (1.0 - ADAM_B1) * g_v
    v_new = ADAM_B2 * v + (1.0 - ADAM_B2) * (g_v * g_v)
    m_hat = m_new / (1.0 - ADAM_B1 ** ADAM_STEP)
    v_hat = v_new / (1.0 - ADAM_B2 ** ADAM_STEP)
    return -ADAM_LR * (m_hat / (jnp.sqrt(v_hat) + ADAM_EPS) + ADAM_WD * w), m_new, v_new


def _adamw(w, g, m, v, name):
    rows, cols = w.shape
    rb = _ew_rows(rows, cols)

    def body(w_ref, g_ref, m_ref, v_ref, d_ref, mo_ref, vo_ref):
        d_ref[...], mo_ref[...], vo_ref[...] = _adamw_update(w_ref[...], g_ref[...], m_ref[...], v_ref[...])

    spec = pl.BlockSpec((rb, cols), lambda i: (i, 0))
    return pl.pallas_call(body, name=name, grid=(rows // rb,), in_specs=[spec] * 4, out_specs=[spec] * 3,
                          out_shape=[jax.ShapeDtypeStruct(w.shape, F32)] * 3,
                          compiler_params=_params(("parallel",)))(w, g, m, v)


def _adamw_two_halves(w, own, recv, core_flag, m, v, name):
    depth, rows, cols = w.shape
    assert rows % 2 == 0 and own.shape == (depth, rows // 2, cols)
    rb = _ew_rows(rows // 2, cols)
    nb = rows // 2 // rb

    def body(w_ref, own_ref, recv_ref, flag_ref, m_ref, v_ref, g_ref, d_ref, mo_ref, vo_ref):
        half = pl.program_id(1).astype(F32)
        g_v = jnp.where(flag_ref[...] == half, own_ref[...], recv_ref[...])
        g_ref[...] = g_v
        d_ref[...], mo_ref[...], vo_ref[...] = _adamw_update(w_ref[...], g_v, m_ref[...], v_ref[...])

    full = pl.BlockSpec((None, rb, cols), lambda l, k, i: (l, k * nb + i, 0))
    half_spec = pl.BlockSpec((None, rb, cols), lambda l, k, i: (l, i, 0))
    return pl.pallas_call(
        body, name=name, grid=(depth, 2, nb),
        in_specs=[full, half_spec, half_spec, pl.BlockSpec((1, 1), lambda l, k, i: (0, 0)), full, full],
        out_specs=[full] * 4, out_shape=[jax.ShapeDtypeStruct(w.shape, F32)] * 4,
        compiler_params=_params(("parallel", "parallel", "parallel")))(w, own, recv, core_flag, m, v)


def _place():
    return lax.axis_index("x"), lax.axis_index("y"), lax.axis_index("c")


def _other_chips(x, y):
    return [(1 - x, y), (x, 1 - y), (1 - x, 1 - y)]


_ANY = pl.BlockSpec(memory_space=pl.ANY)


class _Rider:
    def __init__(self, arrays, out_shapes, n_sems, start, finish, in_place=False):
        self.arrays, self.out_shapes, self.n_sems = list(arrays), list(out_shapes), n_sems
        self.start, self.finish, self.in_place = start, finish, in_place

    def sems(self):
        return [pltpu.SemaphoreType.DMA((self.n_sems,)), pltpu.SemaphoreType.DMA((self.n_sems,))]

    def aliases(self, first_in, first_out):
        return {first_in + a: first_out + a for a in range(len(self.arrays))} if self.in_place else {}


def _comm_call(rider, name):
    n_in, n_out = len(rider.arrays), len(rider.out_shapes)

    def body(*refs):
        ins, outs = refs[:n_in], refs[n_in:n_in + n_out]
        send_sems, recv_sems = refs[n_in + n_out:]
        rider.start(ins, outs, send_sems, recv_sems)
        rider.finish(ins, outs, send_sems, recv_sems)

    return pl.pallas_call(
        body, name=name, in_specs=[_ANY] * n_in, out_specs=[_ANY] * n_out, out_shape=rider.out_shapes,
        scratch_shapes=rider.sems(), input_output_aliases=rider.aliases(0, 0))(*rider.arrays)


def _half(shape, which):
    for axis, size in enumerate(shape):
        if size % 2 == 0:
            return (slice(None),) * axis + (pl.ds(which * (size // 2), size // 2),)
    raise ValueError(f"no axis of even length in {shape}")


def _start_then_wait(copies):
    def start(*refs):
        for send, _ in copies(*refs):
            send.start()

    def finish(*refs):
        pairs = copies(*refs)
        for _, landing in pairs:
            landing.wait_recv()
        for send, _ in pairs:
            send.wait_send()

    return start, finish


def _fetch_rider(shards):
    n, n_peer = len(shards), N_SHARD - 1
    shapes = [s.shape for s in shards]

    def copies(ins, outs, send_sems, recv_sems):
        x, y, c = _place()
        pairs = []
        for a in range(n):
            mine = _half(shapes[a], c)
            for k, (px, py) in enumerate(_other_chips(x, y)):
                def into(slot):
                    return pltpu.make_async_remote_copy(
                        src_ref=ins[a].at[mine], dst_ref=outs[a].at[(slot,) + mine],
                        send_sem=send_sems.at[a * n_peer + k], recv_sem=recv_sems.at[a * n_peer + k],
                        device_id=(px, py, c), device_id_type=MESH)
                pairs.append((into(2 * x + y), into(2 * px + py)))
        return pairs

    start, finish = _start_then_wait(copies)
    return _Rider(shards, [jax.ShapeDtypeStruct((N_SHARD,) + s.shape, s.dtype) for s in shards], n * n_peer,
                  start, finish)


def _forward_rider(gathered, shards):
    n, n_peer = len(gathered), N_SHARD - 1
    shapes = [s.shape for s in shards]

    def copies(ins, outs, send_sems, recv_sems):
        x, y, c = _place()
        pairs = []
        for a in range(n):
            for k, (px, py) in enumerate(_other_chips(x, y)):
                def half_of_slot(which):
                    rows = outs[a].at[(2 * px + py,) + _half(shapes[a], which)]
                    return pltpu.make_async_remote_copy(
                        src_ref=rows, dst_ref=rows, send_sem=send_sems.at[a * n_peer + k],
                        recv_sem=recv_sems.at[a * n_peer + k], device_id=(x, y, 1 - c), device_id_type=MESH)
                pairs.append((half_of_slot(c), half_of_slot(1 - c)))
        return pairs

    start, finish = _start_then_wait(copies)
    return _Rider(gathered, [jax.ShapeDtypeStruct(g.shape, g.dtype) for g in gathered], n * n_peer, start, finish,
                  in_place=True)


def _sibling_rider(arrs, other_half):
    n = len(arrs)

    def copies(ins, outs, send_sems, recv_sems):
        x, y, c = _place()
        pairs = []
        for a in range(n):
            cp = pltpu.make_async_remote_copy(
                src_ref=ins[a].at[1 - c] if other_half else ins[a], dst_ref=outs[a], send_sem=send_sems.at[a],
                recv_sem=recv_sems.at[a], device_id=(x, y, 1 - c), device_id_type=MESH)
            pairs.append((cp, cp))
        return pairs

    start, finish = _start_then_wait(copies)
    return _Rider(arrs, [jax.ShapeDtypeStruct(g.shape[1:] if other_half else g.shape, g.dtype) for g in arrs], n,
                  start, finish)


def _owner_rider(parts):
    n, n_peer = len(parts), N_SHARD - 1

    def copies(ins, outs, send_sems, recv_sems):
        x, y, c = _place()
        pairs = []
        for a in range(n):
            for k, (px, py) in enumerate(_other_chips(x, y)):
                cp = pltpu.make_async_remote_copy(
                    src_ref=ins[a].at[2 * px + py], dst_ref=outs[a].at[k], send_sem=send_sems.at[a * n_peer + k],
                    recv_sem=recv_sems.at[a * n_peer + k], device_id=(px, py, c), device_id_type=MESH)
                pairs.append((cp, cp))
        return pairs

    start, finish = _start_then_wait(copies)
    return _Rider(parts, [jax.ShapeDtypeStruct((n_peer,) + p.shape[1:], p.dtype) for p in parts], n * n_peer,
                  start, finish)


def _gather_devices(buf, name):
    n_peer = N_DEV - 1

    def body(in_ref, out_ref, send_sems, recv_sems, local_sem):
        x, y, c = _place()
        me = 4 * x + 2 * y + c
        mine = pltpu.make_async_copy(in_ref, out_ref.at[me], local_sem)
        mine.start()
        peers = []
        for k in range(1, N_DEV):
            fx, fy, fc = (k >> 2) & 1, (k >> 1) & 1, k & 1
            peers.append((x ^ fx, y ^ fy, c ^ fc))
        sends = []
        for k, peer in enumerate(peers):
            cp = pltpu.make_async_remote_copy(
                src_ref=in_ref, dst_ref=out_ref.at[me], send_sem=send_sems.at[k], recv_sem=recv_sems.at[k],
                device_id=peer, device_id_type=MESH)
            cp.start()
            sends.append(cp)
        for k, (px, py, pc) in enumerate(peers):
            pltpu.make_async_remote_copy(
                src_ref=in_ref, dst_ref=out_ref.at[4 * px + 2 * py + pc], send_sem=send_sems.at[k],
                recv_sem=recv_sems.at[k], device_id=(px, py, pc), device_id_type=MESH).wait_recv()
        for cp in sends:
            cp.wait_send()
        mine.wait()

    return pl.pallas_call(
        body, name=name, in_specs=[_ANY], out_specs=_ANY,
        out_shape=jax.ShapeDtypeStruct((N_DEV,) + buf.shape, buf.dtype),
        scratch_shapes=[pltpu.SemaphoreType.DMA((n_peer,)), pltpu.SemaphoreType.DMA((n_peer,)),
                        pltpu.SemaphoreType.DMA(())],
    )(buf)


class _Dims:
    def __init__(self, x, p, w_in, sgu_norm, conv_w, kv_norm, w_ukv, w_out):
        self.t, self.d = x.shape[1], x.shape[2]
        self.depth = w_in.shape[0]
        self.ple = p.shape[3]
        self.in_w = w_in.shape[2] * N_SHARD
        self.ah = sgu_norm.shape[1]
        self.aw = self.ah * HEAD
        self.bw = conv_w.shape[2] * N_SHARD
        self.kvr = kv_norm.shape[1]
        self.ch = w_ukv.shape[2] * N_SHARD // (2 * HEAD)
        self.cw = self.ch * HEAD
        self.mix = w_out.shape[1] * N_SHARD
        assert self.mix == self.aw + self.bw + self.cw and self.aw == self.bw
        self.qw = self.ch * 2 * HEAD
        segs = [('a', 3 * self.aw, self.aw), ('b', 4 * self.bw, self.bw), ('ckv', self.kvr, self.kvr),
                ('q', self.qw, self.qw), ('cz', self.cw, self.cw), ('kr', HEAD, HEAD)]
        off = 0
        self.off = {}
        for nm, width, align in segs:
            off = -(-off // align) * align
            self.off[nm] = off
            off += width
        self.inp = -(-off // 512) * 512
        q_real = self.ch * (HEAD + ROPE)
        widths = [3 * self.aw, 4 * self.bw, q_real, self.kvr, ROPE, self.cw]
        assert sum(widths) == self.in_w
        starts = [0]
        for wd in widths:
            starts.append(starts[-1] + wd)
        self.src = dict(zip(['a', 'b', 'q', 'ckv', 'kr', 'cz'], zip(starts[:-1], widths)))


def _rearrange_w_in(w, dm):
    lead, d = w.shape[:-2], w.shape[-1]
    axis = w.ndim - 2

    def rows(nm):
        s, wd = dm.src[nm]
        return lax.slice_in_dim(w, s, s + wd, axis=axis)

    pieces = {nm: rows(nm) for nm in ('a', 'b', 'ckv', 'cz')}
    q = rows('q').reshape(lead + (dm.ch, HEAD + ROPE, d))
    pieces['q'] = jnp.pad(q, [(0, 0)] * (len(lead) + 1) + [(0, HEAD - ROPE), (0, 0)]).reshape(lead + (dm.qw, d))
    pieces['kr'] = jnp.pad(rows('kr'), [(0, 0)] * len(lead) + [(0, HEAD - ROPE), (0, 0)])
    out, cur = [], 0
    for nm in sorted(dm.off, key=lambda k: dm.off[k]):
        if dm.off[nm] > cur:
            out.append(jnp.zeros(lead + (dm.off[nm] - cur, d), w.dtype))
        out.append(pieces[nm])
        cur = dm.off[nm] + pieces[nm].shape[axis]
    if dm.inp > cur:
        out.append(jnp.zeros(lead + (dm.inp - cur, d), w.dtype))
    return jnp.concatenate(out, axis=axis)


def _unarrange_w_in(g, dm):
    d = g.shape[1]

    def seg(nm, width):
        return g[dm.off[nm]:dm.off[nm] + width]

    q = seg('q', dm.qw).reshape(dm.ch, 2 * HEAD, d)[:, :HEAD + ROPE].reshape(dm.ch * (HEAD + ROPE), d)
    return jnp.concatenate([seg('a', 3 * dm.aw), seg('b', 4 * dm.bw), q, seg('ckv', dm.kvr), seg('kr', ROPE),
                            seg('cz', dm.cw)], axis=0)


def _assemble_dproj(parts, dm, t):
    out, cur = [], 0
    for nm in sorted(dm.off, key=lambda k: dm.off[k]):
        if dm.off[nm] > cur:
            out.append(jnp.zeros((t, dm.off[nm] - cur), BF16))
        out.append(parts[nm])
        cur = dm.off[nm] + parts[nm].shape[-1]
    if dm.inp > cur:
        out.append(jnp.zeros((t, dm.inp - cur), BF16))
    return jnp.concatenate(out, axis=-1)


def _rope_tables(positions):
    inv = 1.0 / (ROPE_BASE ** (jnp.arange(0, ROPE, 2, dtype=F32) / ROPE))
    ang = positions.astype(F32)[:, None] * inv
    cos, sin = jnp.cos(ang), jnp.sin(ang)
    t = positions.shape[0]
    half = ROPE // 2
    cos_t = jnp.concatenate([cos, cos, jnp.zeros((t, HEAD - ROPE), F32)], axis=-1)
    sin_a = jnp.concatenate([-sin, jnp.zeros((t, HEAD - half), F32)], axis=-1)
    sin_b = jnp.concatenate([jnp.zeros((t, half), F32), sin, jnp.zeros((t, HEAD - ROPE), F32)], axis=-1)
    return cos_t, sin_a, sin_b


def _pad_gain(g):
    return jnp.pad(g, (0, HEAD - g.shape[0]))[None, :]


def _shard_major(g, axis):
    shape = g.shape
    g = g.reshape(shape[:axis] + (N_SHARD, shape[axis] // N_SHARD) + shape[axis + 1:])
    g = jnp.moveaxis(g, axis, 0)
    rows, cols = g.shape[1], g.shape[2]
    return jnp.swapaxes(g.reshape(N_SHARD, 2, rows // 2, cols), 0, 1)


def _pack(arrs):
    flat = jnp.concatenate([a.reshape(-1) for a in arrs])
    pad = (-flat.shape[0]) % (8 * HEAD)
    return jnp.pad(flat, (0, pad)).reshape(-1, HEAD)


def _unpack(buf, shapes):
    flat = buf.reshape(-1)
    out, cur = [], 0
    for s in shapes:
        size = 1
        for v in s:
            size *= v
        out.append(flat[cur:cur + size].reshape(s))
        cur += size
    return out


def kernel(x, p, positions, attn_norm, w_in, sgu_norm, w_spatial, b_spatial, conv_w, conv_b, kv_norm, w_ukv, q_nope_norm, q_rope_norm, k_nope_norm, k_rope_norm, out_norm, w_out, ple_norm, w_ple_gate, w_ple_proj, loss_target, m_attn_norm, m_w_in, m_sgu_norm, m_w_spatial, m_b_spatial, m_conv_w, m_conv_b, m_kv_norm, m_w_ukv, m_q_nope_norm, m_q_rope_norm, m_k_nope_norm, m_k_rope_norm, m_out_norm, m_w_out, m_ple_norm, m_w_ple_gate, m_w_ple_proj, v_attn_norm, v_w_in, v_sgu_norm, v_w_spatial, v_b_spatial, v_conv_w, v_conv_b, v_kv_norm, v_w_ukv, v_q_nope_norm, v_q_rope_norm, v_k_nope_norm, v_k_rope_norm, v_out_norm, v_w_out, v_ple_norm, v_w_ple_gate, v_w_ple_proj):
    weights = dict(attn_norm=attn_norm, w_in=w_in, sgu_norm=sgu_norm, w_spatial=w_spatial, b_spatial=b_spatial,
                   conv_w=conv_w, conv_b=conv_b, kv_norm=kv_norm, w_ukv=w_ukv, q_nope_norm=q_nope_norm,
                   q_rope_norm=q_rope_norm, k_nope_norm=k_nope_norm, k_rope_norm=k_rope_norm, out_norm=out_norm,
                   w_out=w_out, ple_norm=ple_norm, w_ple_gate=w_ple_gate, w_ple_proj=w_ple_proj)
    mom_m = dict(attn_norm=m_attn_norm, w_in=m_w_in, sgu_norm=m_sgu_norm, w_spatial=m_w_spatial,
                 b_spatial=m_b_spatial, conv_w=m_conv_w, conv_b=m_conv_b, kv_norm=m_kv_norm, w_ukv=m_w_ukv,
                 q_nope_norm=m_q_nope_norm, q_rope_norm=m_q_rope_norm, k_nope_norm=m_k_nope_norm,
                 k_rope_norm=m_k_rope_norm, out_norm=m_out_norm, w_out=m_w_out, ple_norm=m_ple_norm,
                 w_ple_gate=m_w_ple_gate, w_ple_proj=m_w_ple_proj)
    mom_v = dict(attn_norm=v_attn_norm, w_in=v_w_in, sgu_norm=v_sgu_norm, w_spatial=v_w_spatial,
                 b_spatial=v_b_spatial, conv_w=v_conv_w, conv_b=v_conv_b, kv_norm=v_kv_norm, w_ukv=v_w_ukv,
                 q_nope_norm=v_q_nope_norm, q_rope_norm=v_q_rope_norm, k_nope_norm=v_k_nope_norm,
                 k_rope_norm=v_k_rope_norm, out_norm=v_out_norm, w_out=v_w_out, ple_norm=v_ple_norm,
                 w_ple_gate=v_w_ple_gate, w_ple_proj=v_w_ple_proj)
    dm = _Dims(x, p, w_in, sgu_norm, conv_w, kv_norm, w_ukv, w_out)
    for group in (weights, mom_m, mom_v):
        group['w_in'] = jnp.swapaxes(group['w_in'], 1, 2)
    t, d, depth = dm.t, dm.d, dm.depth
    shard = 2 * lax.axis_index("x") + lax.axis_index("y")
    core = lax.axis_index("c")
    scale = float(HEAD + ROPE) ** -0.5

    def local_layer(i):
        return [weights[n][i:i + 1].astype(BF16) for n in BIG]

    def fill_own_slot(gathered, local):
        return [lax.dynamic_update_slice(g, mine[None], (shard,) + (0,) * mine.ndim)
                for g, mine in zip(gathered, local)]

    def layer_weights(filled):
        w = {n: jnp.concatenate([filled[j][s] for s in range(N_SHARD)], axis=BIG_AXIS[n])
             for j, n in enumerate(BIG)}
        w['w_in'] = _rearrange_w_in(w['w_in'], dm)
        return w

    first = local_layer(0) + [conv_w]
    fetched = _comm_call(_fetch_rider(first), "fetch_weights_l0")
    filled = fill_own_slot(_comm_call(_forward_rider(fetched, first), "forward_weights_l0"), first)
    layer_w = [layer_weights(filled)] + [None] * (depth - 1)
    conv_w_full = jnp.concatenate([filled[len(BIG)][s] for s in range(N_SHARD)], axis=2)

    tabs = _rope_tables(positions[0])
    h = x[0]
    saved = []
    for i in range(depth):
        tag = f"l{i}_"
        ga, gb, gc = (out_norm[i][None, :dm.aw], out_norm[i][None, dm.aw:dm.aw + dm.bw],
                      out_norm[i][None, dm.aw + dm.bw:])
        ws_b = w_spatial[i].astype(BF16)
        bb = jnp.broadcast_to(b_spatial[i][:, :, None], (dm.ah, HEAD, HEAD))
        qn_g, qr_g = q_nope_norm[i][None, :], _pad_gain(q_rope_norm[i])
        kn_g, kr_g = k_nope_norm[i][None, :], _pad_gain(k_rope_norm[i])
        kv_g = kv_norm[i][None, :]
        wl = layer_w[i]
        nxt = local_layer(i + 1) if i + 1 < depth else None
        hn = _norm_fwd(h, attn_norm[i][None, :], tag + "norm1")
        if nxt is None:
            proj = _matmul(hn, wl['w_in'], 'nt', BF16, tag + "proj", b_layer=0)
        else:
            proj, fetched = _matmul(hn, wl['w_in'], 'nt', BF16, tag + "proj", b_layer=0, rider=_fetch_rider(nxt))
        y = _sgu_fwd(proj, dm.off['a'], dm.aw, sgu_norm[i], ws_b, bb, ga, tag + "sgu",
                     into=(jnp.zeros((t, dm.mix), BF16), 0))
        y, yconv = _conv_fwd(proj, dm.off['b'], dm.bw, conv_w_full[i], conv_b[i][None, :], gb, tag + "conv",
                             into=(y, _col_block(dm.aw, dm.bw)))
        q_cat, ckv_n, kr_rot = _mla_prep_fwd(proj, dm.off['q'], dm.off['ckv'], dm.off['kr'], dm.ch, dm.kvr, tabs,
                                             qn_g, qr_g, kr_g, kv_g, tag + "mla_prep")
        kv = _matmul(ckv_n, wl['w_ukv'], 'nn', F32, tag + "kv_up", b_layer=0)
        k_cat, k_cat_t, v_aug = _kv_prep_fwd(kv, kr_rot, dm.ch, kn_g, tag + "kv_prep")
        o, lse = _attn_fwd(q_cat, k_cat, v_aug, dm.ch, scale, tag + "attn")
        y = _attn_post_fwd(o, proj, dm.off['cz'], dm.cw, gc, tag + "attn_post",
                           into=(y, _col_block(dm.aw + dm.bw, dm.cw)))
        if nxt is None:
            h1 = _matmul(y, wl['w_out'], 'nn', F32, tag + "out", add=h, b_layer=0)
        else:
            h1, gathered = _matmul(y, wl['w_out'], 'nn', F32, tag + "out", add=h, b_layer=0,
                                   rider=_forward_rider(list(fetched), nxt))
            layer_w[i + 1] = layer_weights(fill_own_slot(gathered, nxt))
        hn2 = _norm_fwd(h1, ple_norm[i][None, :], tag + "norm2")
        gpre = _matmul(hn2, wl['w_ple_gate'], 'nn', F32, tag + "gate", b_layer=0)
        p_b = p[i, 0].astype(BF16)
        pp = _matmul(p_b, wl['w_ple_proj'], 'nn', F32, tag + "ple_proj", b_layer=0)
        h2 = _ple_fwd(h1, gpre, pp, tag + "ple")
        saved.append(dict(h=h, hn=hn, proj=proj, yconv=yconv, q_cat=q_cat, ckv_n=ckv_n, kv=kv, k_cat=k_cat,
                          k_cat_t=k_cat_t,
                          v=v_aug, o=o, lse=lse, y=y, h1=h1, hn2=hn2, gpre=gpre, pp=pp, p_b=p_b, ws_b=ws_b, bb=bb,
                          gains=(ga, gb, gc, qn_g, qr_g, kn_g, kr_g, kv_g)))
        h = h2

    loss_part, dh = _loss_and_grad(h, loss_target[0], "loss")
    loss = lax.psum(loss_part[0, 0], ("x", "y", "c"))

    def chip_sums(sm, from_sibling, tag):
        mine = [lax.dynamic_index_in_dim(g, core, 0, keepdims=False) for g in sm]
        return mine, [_pair_sum_bf16(a, b, f"{tag}chip_sum_{n}") for a, b, n in zip(mine, from_sibling, BIG)]

    def shard_sums(mine, from_sibling, from_chips, tag):
        out = []
        for a, b, r3, n in zip(mine, from_sibling, from_chips, BIG):
            own_a = lax.dynamic_index_in_dim(a, shard, 0, keepdims=False)
            own_b = lax.dynamic_index_in_dim(b, shard, 0, keepdims=False)
            out.append(_shard_sum(own_a, own_b, r3, f"{tag}shard_sum_{n}"))
        return out

    grads = {n: [None] * depth for n in WEIGHTS}
    own_half = {n: [None] * depth for n in BIG}
    sibling_half = {n: [None] * depth for n in BIG}
    carry = None
    for i in reversed(range(depth)):
        tag = f"l{i}_b_"
        gtag = f"l{i + 1}_g_"
        sv = saved[i]
        wl = layer_w[i]
        ga, gb, gc, qn_g, qr_g, kn_g, kr_g, kv_g = sv['gains']
        proj = sv['proj']
        dgpre, dpp = _ple_bwd(sv['gpre'], sv['pp'], dh, tag + "ple")
        grads['w_ple_proj'][i] = _matmul(sv['p_b'], dpp, 'tn', F32, tag + "d_w_ple_proj")
        if carry is None:
            grads['w_ple_gate'][i] = _matmul(sv['hn2'], dgpre, 'tn', F32, tag + "d_w_gate")
        else:
            grads['w_ple_gate'][i], from_sibling = _matmul(sv['hn2'], dgpre, 'tn', F32, tag + "d_w_gate",
                                                           rider=_sibling_rider(carry, True))
            mine, sums = chip_sums(carry, from_sibling, gtag)
        d_hn2 = _matmul(dgpre, wl['w_ple_gate'], 'nt', BF16, tag + "d_hn2", b_layer=0)
        dh1, dh1_b, g_ple = _norm_bwd(sv['h1'], ple_norm[i][None, :], d_hn2, dh, tag + "norm2")
        grads['ple_norm'][i] = g_ple[0]
        grads['w_out'][i] = _matmul(sv['y'], dh1_b, 'tn', F32, tag + "d_w_out")
        dy = _matmul(dh1_b, wl['w_out'], 'nt', BF16, tag + "d_y", b_layer=0)
        ws_t = jnp.swapaxes(sv['ws_b'], 1, 2)
        d_a, g_sgu, g_ws, g_bs, g_ga = _sgu_bwd(proj, dm.off['a'], dm.aw, sgu_norm[i], sv['ws_b'], ws_t, sv['bb'],
                                                ga, dy, tag + "sgu")
        grads['sgu_norm'][i], grads['w_spatial'][i], grads['b_spatial'][i] = g_sgu, g_ws, g_bs[:, :, 0]
        dyc, d_bb, d_bz, g_gb, g_cb = _conv_bwd_gate(proj, dm.off['b'], dm.bw, sv['yconv'], gb, dy, tag + "conv_gate")
        d_b, g_cw = _conv_bwd_taps(proj, dm.off['b'], dm.bw, dyc, conv_w_full[i], d_bb, d_bz, tag + "conv_taps")
        grads['conv_b'][i], grads['conv_w'][i] = g_cb[0], g_cw
        d_o, d_cz, dsum, g_gc = _attn_post_bwd(sv['o'], proj, dm.off['cz'], dm.cw, gc, dy,
                                               _col_block(dm.aw + dm.bw, dm.cw), tag + "attn_post")
        grads['out_norm'][i] = jnp.concatenate([g_ga[0], g_gb[0], g_gc[0]])
        dq_t, dk_cat, dv = _attn_bwd(sv['q_cat'], sv['k_cat'], sv['k_cat_t'], sv['v'], d_o,
                                     sv['lse'].reshape(dm.ch, 1, t), dsum.reshape(dm.ch, 1, t), dm.ch, scale,
                                     tag + "attn_bwd")
        dkv, dkr_rot, g_kn = _kv_prep_bwd(sv['kv'], dm.ch, kn_g, dk_cat, dv, tag + "kv_prep")
        grads['k_nope_norm'][i] = g_kn[0]
        grads['w_ukv'][i] = _matmul(sv['ckv_n'], dkv, 'tn', F32, tag + "d_w_ukv")
        dckv_n = _matmul(dkv, wl['w_ukv'], 'nt', BF16, tag + "d_ckv", b_layer=0)
        d_q, d_ckv, d_kr, g_qn, g_qr, g_kr, g_kv = _mla_prep_bwd(
            proj, dm.off['q'], dm.off['ckv'], dm.off['kr'], dm.ch, dm.kvr, tabs, qn_g, qr_g, kr_g, kv_g,
            dq_t, scale, dckv_n, dkr_rot, dm.inp - dm.off['kr'], tag + "mla_prep")
        grads['q_nope_norm'][i], grads['q_rope_norm'][i] = g_qn[0], g_qr[0, :ROPE]
        grads['k_rope_norm'][i], grads['kv_norm'][i] = g_kr[0, :ROPE], g_kv[0]
        dproj = _assemble_dproj(dict(a=d_a, b=d_b, ckv=d_ckv, q=d_q, cz=d_cz, kr=d_kr), dm, t)
        if carry is None:
            d_w_in = _matmul(dproj, sv['hn'], 'tn', F32, tag + "d_w_in")
            d_hn = _matmul(dproj, wl['w_in'], 'nn', BF16, tag + "d_hn", b_layer=0)
        else:
            d_w_in, from_chips = _matmul(dproj, sv['hn'], 'tn', F32, tag + "d_w_in", rider=_owner_rider(sums))
            halves = shard_sums(mine, from_sibling, from_chips, gtag)
            d_hn, from_core = _matmul(dproj, wl['w_in'], 'nn', BF16, tag + "d_hn", b_layer=0,
                                      rider=_sibling_rider(halves, False))
            for n, own, recv in zip(BIG, halves, from_core):
                own_half[n][i + 1], sibling_half[n][i + 1] = own, recv
        grads['w_in'][i] = _unarrange_w_in(d_w_in, dm)
        dh, _, g_an = _norm_bwd(sv['h'], attn_norm[i][None, :], d_hn, dh1, tag + "norm1")
        grads['attn_norm'][i] = g_an[0]
        carry = [_shard_major(grads[n][i], BIG_AXIS[n] - 1) for n in BIG]
    grad_x = dh[None]

    from_sibling = _comm_call(_sibling_rider(carry, True), "l0_g_to_sibling")
    mine, sums = chip_sums(carry, from_sibling, "l0_g_")
    from_chips = _comm_call(_owner_rider(sums), "l0_g_to_owner_chips")
    halves = shard_sums(mine, from_sibling, from_chips, "l0_g_")
    from_core = _comm_call(_sibling_rider(halves, False), "l0_g_share_sibling")
    for n, own, recv in zip(BIG, halves, from_core):
        own_half[n][0], sibling_half[n][0] = own, recv

    core_flag = core.astype(F32).reshape(1, 1)
    out_g, out_d, out_m, out_v = {}, {}, {}, {}
    for n in BIG:
        out_g[n], out_d[n], out_m[n], out_v[n] = _adamw_two_halves(
            weights[n], jnp.stack(own_half[n]), jnp.stack(sibling_half[n]), core_flag, mom_m[n], mom_v[n],
            f"adamw_{n}")
    for out in (out_g, out_d, out_m, out_v):
        out['w_in'] = jnp.swapaxes(out['w_in'], 1, 2)
    grads = {n: jnp.stack(grads[n]) for n in SMALL}

    shapes = [grads[n].shape for n in SMALL]
    summed = _unpack(_sum_devices(_gather_devices(_pack([grads[n] for n in SMALL]), "gather_small_grads"),
                                  "sum_small_grads"), shapes)
    small_g = dict(zip(SMALL, summed))
    small_g['conv_w'] = lax.dynamic_slice_in_dim(small_g['conv_w'], shard * conv_w.shape[2], conv_w.shape[2], axis=2)
    local_shapes = [weights[n].shape for n in SMALL]
    d_s, m_s, v_s = _adamw(_pack([weights[n] for n in SMALL]), _pack([small_g[n] for n in SMALL]),
                           _pack([mom_m[n] for n in SMALL]), _pack([mom_v[n] for n in SMALL]), "adamw_small")
    for n, dd, mm, vv in zip(SMALL, _unpack(d_s, local_shapes), _unpack(m_s, local_shapes),
                             _unpack(v_s, local_shapes)):
        out_g[n], out_d[n], out_m[n], out_v[n] = small_g[n], dd, mm, vv

    return (loss, grad_x, *[out_g[n] for n in WEIGHTS], *[out_d[n] for n in WEIGHTS],
            *[out_m[n] for n in WEIGHTS], *[out_v[n] for n in WEIGHTS])
```

```python
import functools

import jax
import jax.numpy as jnp
from jax import lax
from jax.experimental import pallas as pl
from jax.experimental.pallas import tpu as pltpu

F32 = jnp.float32
BF16 = jnp.bfloat16
EPS = 1e-6
HEAD = 128
ROPE = 64
ROPE_BASE = 10000.0
CONV_TAPS = 3
N_SHARD = 4
N_DEV = 8
ADAM_LR = 0.001
ADAM_B1 = 0.9
ADAM_B2 = 0.999
ADAM_EPS = 1e-08
ADAM_WD = 0.01
ADAM_STEP = 10
MESH = pl.DeviceIdType.MESH
VMEM_LIMIT = 56 * 1024 * 1024
HALO_ROWS = 16

WEIGHTS = ['attn_norm', 'w_in', 'sgu_norm', 'w_spatial', 'b_spatial', 'conv_w', 'conv_b', 'kv_norm', 'w_ukv',
           'q_nope_norm', 'q_rope_norm', 'k_nope_norm', 'k_rope_norm', 'out_norm', 'w_out', 'ple_norm',
           'w_ple_gate', 'w_ple_proj']
BIG = ['w_in', 'w_ukv', 'w_out', 'w_ple_gate', 'w_ple_proj']
BIG_AXIS = {'w_in': 1, 'w_ukv': 2, 'w_out': 1, 'w_ple_gate': 1, 'w_ple_proj': 2}
SMALL = [n for n in WEIGHTS if n not in BIG]


def _pick(n, cands):
    for c in cands:
        if n % c == 0:
            return c
    return n


def _params(sem=None):
    return pltpu.CompilerParams(dimension_semantics=sem, vmem_limit_bytes=VMEM_LIMIT)


def _matmul(a, b, mode, out_dtype, name, add=None, b_layer=None, rider=None):
    b_shape = b.shape if b_layer is None else b.shape[1:]
    if mode == 'nn':
        (m, k), n = a.shape, b_shape[1]
    elif mode == 'nt':
        (m, k), n = a.shape, b_shape[0]
    else:
        (k, m), n = a.shape, b_shape[1]
    tm = _pick(m, (1280, 1024, 512, 256, 128))
    tn = _pick(n, (1536, 1024, 512, 256, 128))
    tk = k if k <= 2048 else _pick(k, (2048, 1536, 1024, 512, 256, 128))
    nk = k // tk
    if mode == 'tn':
        a_spec = pl.BlockSpec((tk, tm), lambda i, j, kk: (kk, i))
        dims = (((0,), (0,)), ((), ()))
    else:
        a_spec = pl.BlockSpec((tm, tk), lambda i, j, kk: (i, kk))
        dims = (((1,), (0,)), ((), ())) if mode == 'nn' else (((1,), (1,)), ((), ()))
    b_block = (tn, tk) if mode == 'nt' else (tk, tn)
    if b_layer is None:
        b_spec = pl.BlockSpec(b_block, (lambda i, j, kk: (j, kk)) if mode == 'nt' else (lambda i, j, kk: (kk, j)))
    else:
        b_spec = pl.BlockSpec((None,) + b_block, (lambda i, j, kk: (b_layer, j, kk)) if mode == 'nt'
                              else (lambda i, j, kk: (b_layer, kk, j)))
    o_spec = pl.BlockSpec((tm, tn), lambda i, j, kk: (i, j))
    has_add = add is not None

    def body(*refs):
        a_ref, b_ref = refs[0], refs[1]
        add_ref = refs[2] if has_add else None
        o_ref = refs[3] if has_add else refs[2]

        def product():
            return lax.dot_general(a_ref[...], b_ref[...], dims, preferred_element_type=F32)

        def finish(res):
            if has_add:
                res = res + add_ref[...]
            o_ref[...] = res.astype(out_dtype)

        if nk == 1:
            finish(product())
        else:
            acc_ref = refs[-1]
            kk = pl.program_id(2)

            @pl.when(kk == 0)
            def _():
                acc_ref[...] = product()

            @pl.when((kk > 0) & (kk < nk - 1))
            def _():
                acc_ref[...] += product()

            @pl.when(kk == nk - 1)
            def _():
                finish(acc_ref[...] + product())

    in_specs = [a_spec, b_spec] + ([o_spec] if has_add else [])
    args = [a, b] + ([add] if has_add else [])
    grid = (m // tm, n // tn, nk)
    scratch = [pltpu.VMEM((tm, tn), F32)] if nk > 1 else []
    if rider is None:
        return pl.pallas_call(
            body, name=name, grid=grid, in_specs=in_specs, out_specs=o_spec,
            out_shape=jax.ShapeDtypeStruct((m, n), out_dtype), scratch_shapes=scratch,
            compiler_params=_params(("parallel", "parallel", "arbitrary")),
        )(*args)

    n_in, n_rin, n_rout = len(args), len(rider.arrays), len(rider.out_shapes)

    def body_with_rider(*refs):
        r_in = refs[n_in:n_in + n_rin]
        r_out = refs[n_in + n_rin + 1:n_in + n_rin + 1 + n_rout]
        own = refs[:n_in] + refs[n_in + n_rin:n_in + n_rin + 1] + refs[n_in + n_rin + 1 + n_rout:len(refs) - 2]
        send_sems, recv_sems = refs[-2:]
        ids = [pl.program_id(ax) for ax in range(3)]

        @pl.when((ids[0] == 0) & (ids[1] == 0) & (ids[2] == 0))
        def _():
            rider.start(r_in, r_out, send_sems, recv_sems)

        body(*own)

        @pl.when((ids[0] == grid[0] - 1) & (ids[1] == grid[1] - 1) & (ids[2] == grid[2] - 1))
        def _():
            rider.finish(r_in, r_out, send_sems, recv_sems)

    res = pl.pallas_call(
        body_with_rider, name=name, grid=grid, in_specs=in_specs + [_ANY] * n_rin,
        out_specs=[o_spec] + [_ANY] * n_rout,
        out_shape=[jax.ShapeDtypeStruct((m, n), out_dtype)] + rider.out_shapes,
        scratch_shapes=scratch + rider.sems(), input_output_aliases=rider.aliases(n_in, 1),
        compiler_params=_params(("arbitrary", "arbitrary", "arbitrary")),
    )(*args, *rider.arrays)
    return res[0], res[1:]


def _rms(x, n):
    r = lax.rsqrt(jnp.sum(x * x, axis=-1, keepdims=True) * (1.0 / n) + EPS)
    return x * r, r


def _rms_bwd(dxhat, xhat, r, n):
    return r * (dxhat - xhat * (jnp.sum(dxhat * xhat, axis=-1, keepdims=True) * (1.0 / n)))


def _sigmoid(z):
    return 1.0 / (1.0 + jnp.exp(-z))


def _silu_and_grad(z):
    sig = _sigmoid(z)
    return z * sig, sig * (1.0 + z * (1.0 - sig))


def _colsum(x):
    return jnp.sum(x, axis=0, keepdims=True)


def _rope(t, cos_t, sin_a, sin_b):
    return t * cos_t + pltpu.roll(t, 96, 1) * sin_a + pltpu.roll(t, 32, 1) * sin_b


def _rope_bwd(d, cos_t, sin_a, sin_b):
    return d * cos_t + pltpu.roll(d * sin_a, 32, 1) + pltpu.roll(d * sin_b, 96, 1)


def _shift_down(g, first_row):
    row = lax.broadcasted_iota(jnp.int32, g.shape, 0)
    return jnp.where(row == 0, first_row, pltpu.roll(g, 1, 0))


def _shift_up(g, last_row):
    n = g.shape[0]
    row = lax.broadcasted_iota(jnp.int32, g.shape, 0)
    return jnp.where(row == n - 1, last_row, pltpu.roll(g, n - 1, 0))


def _row_spec(r, w, col=0):
    return pl.BlockSpec((r, w), lambda i: (i, col))


def _const_spec(shape):
    nd = len(shape)
    return pl.BlockSpec(shape, lambda i: (0,) * nd)


def _col_block(off, w):
    assert off % w == 0, (off, w)
    return off // w


def _zero_at_first_step(refs):
    @pl.when(pl.program_id(0) == 0)
    def _():
        for ref in refs:
            ref[...] = jnp.zeros(ref.shape, ref.dtype)


def _row_call(body, name, t, r, in_specs, args, out_specs, out_shapes, scratch=(), into=None):
    if into is None:
        return pl.pallas_call(
            body, name=name, grid=(t // r,), in_specs=in_specs, out_specs=out_specs, out_shape=out_shapes,
            scratch_shapes=list(scratch), compiler_params=_params(("arbitrary",)),
        )(*args)
    buf, col = into
    single = not isinstance(out_specs, (list, tuple))
    specs = [out_specs] if single else list(out_specs)
    shapes = [out_shapes] if single else list(out_shapes)
    width = shapes[0].shape[1]
    assert shapes[0].dtype == buf.dtype and buf.shape[0] == t
    specs[0] = _row_spec(r, width, col)
    shapes[0] = jax.ShapeDtypeStruct(buf.shape, buf.dtype)
    n_in = len(args)

    def body_in_place(*refs):
        body(*refs[:n_in], *refs[n_in + 1:])

    res = pl.pallas_call(
        body_in_place, name=name, grid=(t // r,), in_specs=list(in_specs) + [_ANY], out_specs=specs, out_shape=shapes,
        scratch_shapes=list(scratch), input_output_aliases={n_in: 0}, compiler_params=_params(("arbitrary",)),
    )(*args, buf)
    return res[0] if single else res


def _norm_fwd(h, g, name):
    t, d = h.shape
    r = _pick(t, (256, 128))

    def body(h_ref, g_ref, o_ref):
        xhat, _ = _rms(h_ref[...], d)
        o_ref[...] = (xhat * g_ref[...]).astype(BF16)

    return _row_call(body, name, t, r, [_row_spec(r, d), _const_spec((1, d))], (h, g),
                     _row_spec(r, d), jax.ShapeDtypeStruct((t, d), BF16))


def _norm_bwd(h, g, d_hn, d_res, name):
    t, d = h.shape
    r = _pick(t, (256, 128))

    def body(h_ref, g_ref, dy_ref, dres_ref, dh_ref, dhb_ref, dg_ref):
        _zero_at_first_step([dg_ref])
        xhat, rr = _rms(h_ref[...], d)
        dy = dy_ref[...].astype(F32)
        dg_ref[...] += _colsum(dy * xhat)
        dh = dres_ref[...] + _rms_bwd(dy * g_ref[...], xhat, rr, d)
        dh_ref[...] = dh
        dhb_ref[...] = dh.astype(BF16)

    return _row_call(body, name, t, r,
                     [_row_spec(r, d), _const_spec((1, d)), _row_spec(r, d), _row_spec(r, d)], (h, g, d_hn, d_res),
                     [_row_spec(r, d), _row_spec(r, d), _const_spec((1, d))],
                     [jax.ShapeDtypeStruct((t, d), F32), jax.ShapeDtypeStruct((t, d), BF16),
                      jax.ShapeDtypeStruct((1, d), F32)])


def _sgu_scores(v, gs_ref, ws_ref, bb_ref, s_scr, r, ah, keep=None):
    for kk in range(r // HEAD):
        for hh in range(ah):
            rows, cols = slice(kk * HEAD, (kk + 1) * HEAD), slice(hh * HEAD, (hh + 1) * HEAD)
            vhat, rv = _rms(v[rows, cols], HEAD)
            vn = vhat * gs_ref[pl.ds(hh, 1), :]
            s_scr[rows, cols] = jnp.dot(ws_ref[hh], vn.astype(BF16), preferred_element_type=F32) + bb_ref[hh]
            if keep is not None:
                keep[(kk, hh)] = (vhat, rv, vn)


def _sgu_fwd(proj, off, aw, gs, ws, bb, ga, name, into=None):
    t = proj.shape[0]
    ah = aw // HEAD
    r = _pick(t, (256, 128))
    cb = _col_block(off, aw)

    def body(u_ref, v_ref, z_ref, gs_ref, ws_ref, bb_ref, ga_ref, o_ref, s_scr):
        _sgu_scores(v_ref[...].astype(F32), gs_ref, ws_ref, bb_ref, s_scr, r, ah)
        sil, _ = _silu_and_grad(z_ref[...].astype(F32))
        yhat, _ = _rms(u_ref[...].astype(F32) * s_scr[...] * sil, aw)
        o_ref[...] = (yhat * ga_ref[...]).astype(BF16)

    return _row_call(
        body, name, t, r,
        [_row_spec(r, aw, cb), _row_spec(r, aw, cb + 1), _row_spec(r, aw, cb + 2), _const_spec((ah, HEAD)),
         _const_spec((ah, HEAD, HEAD)), _const_spec((ah, HEAD, HEAD)), _const_spec((1, aw))],
        (proj, proj, proj, gs, ws, bb, ga),
        _row_spec(r, aw), jax.ShapeDtypeStruct((t, aw), BF16), scratch=[pltpu.VMEM((r, aw), F32)], into=into)


def _sgu_bwd(proj, off, aw, gs, ws, ws_t, bb, ga, dy, name):
    t = proj.shape[0]
    ah = aw // HEAD
    r = _pick(t, (256, 128))
    cb = _col_block(off, aw)

    def body(u_ref, v_ref, z_ref, gs_ref, ws_ref, wst_ref, bb_ref, ga_ref, dy_ref,
             d_ref, dgs_ref, dws_ref, db_ref, dga_ref, s_scr, dv_scr):
        _zero_at_first_step([dgs_ref, dws_ref, db_ref, dga_ref])
        keep = {}
        _sgu_scores(v_ref[...].astype(F32), gs_ref, ws_ref, bb_ref, s_scr, r, ah, keep)
        u, z, s = u_ref[...].astype(F32), z_ref[...].astype(F32), s_scr[...]
        sil, dsil = _silu_and_grad(z)
        yhat, rr = _rms(u * s * sil, aw)
        dy_f = dy_ref[...].astype(F32)
        dga_ref[...] += _colsum(dy_f * yhat)
        dya = _rms_bwd(dy_f * ga_ref[...], yhat, rr, aw)
        d_ref[:, 0:aw] = (dya * s * sil).astype(BF16)
        d_ref[:, 2 * aw:3 * aw] = (dya * u * s * dsil).astype(BF16)
        ds = dya * u * sil
        for kk in range(r // HEAD):
            for hh in range(ah):
                rows, cols = slice(kk * HEAD, (kk + 1) * HEAD), slice(hh * HEAD, (hh + 1) * HEAD)
                vhat, rv, vn = keep[(kk, hh)]
                ds_blk = ds[rows, cols]
                db_ref[hh] += jnp.sum(ds_blk, axis=1, keepdims=True)
                ds_b = ds_blk.astype(BF16)
                dws_ref[hh] += lax.dot_general(ds_b, vn.astype(BF16), (((1,), (1,)), ((), ())),
                                               preferred_element_type=F32)
                dvn = jnp.dot(wst_ref[hh], ds_b, preferred_element_type=F32)
                dgs_ref[pl.ds(hh, 1), :] += _colsum(dvn * vhat)
                dv_scr[rows, cols] = _rms_bwd(dvn * gs_ref[pl.ds(hh, 1), :], vhat, rv, HEAD)
        d_ref[:, aw:2 * aw] = dv_scr[...].astype(BF16)

    return _row_call(
        body, name, t, r,
        [_row_spec(r, aw, cb), _row_spec(r, aw, cb + 1), _row_spec(r, aw, cb + 2), _const_spec((ah, HEAD)),
         _const_spec((ah, HEAD, HEAD)), _const_spec((ah, HEAD, HEAD)), _const_spec((ah, HEAD, HEAD)),
         _const_spec((1, aw)), _row_spec(r, aw, 0)],
        (proj, proj, proj, gs, ws, ws_t, bb, ga, dy),
        [_row_spec(r, 3 * aw), _const_spec((ah, HEAD)), _const_spec((ah, HEAD, HEAD)), _const_spec((ah, HEAD, 1)),
         _const_spec((1, aw))],
        [jax.ShapeDtypeStruct((t, 3 * aw), BF16), jax.ShapeDtypeStruct((ah, HEAD), F32),
         jax.ShapeDtypeStruct((ah, HEAD, HEAD), F32), jax.ShapeDtypeStruct((ah, HEAD, 1), F32),
         jax.ShapeDtypeStruct((1, aw), F32)],
        scratch=[pltpu.VMEM((r, aw), F32), pltpu.VMEM((r, aw), F32)])


def _halo_specs(t, r, w, col, rows):
    per = r // rows
    last = t // rows - 1
    prev = pl.BlockSpec((rows, w), lambda i: (jnp.maximum(i * per - 1, 0), col))
    nxt = pl.BlockSpec((rows, w), lambda i: (jnp.minimum((i + 1) * per, last), col))
    return prev, nxt


def _edge_rows(prev_ref, next_ref, n_steps):
    i = pl.program_id(0)
    rows = prev_ref.shape[0]
    before = prev_ref[...].astype(F32)[rows - 1:rows, :] * (i > 0).astype(F32)
    after = next_ref[...].astype(F32)[0:1, :] * (i < n_steps - 1).astype(F32)
    return before, after


def _conv_fwd(proj, off, bw, cw, cb_, gb, name, into=None):
    t = proj.shape[0]
    r = _pick(t, (256, 128))
    n_steps = t // r
    c0 = _col_block(off, bw)
    cp, cn = _halo_specs(t, r, bw, c0 + 1, HALO_ROWS)
    hp, hn = _halo_specs(t, r, bw, c0 + 2, HALO_ROWS)

    def body(b_ref, c_ref, h_ref, z_ref, cp_ref, cn_ref, hp_ref, hn_ref, cw_ref, cb_ref, gb_ref, o_ref, yc_ref):
        g = c_ref[...].astype(F32) * h_ref[...].astype(F32)
        c_before, c_after = _edge_rows(cp_ref, cn_ref, n_steps)
        h_before, h_after = _edge_rows(hp_ref, hn_ref, n_steps)
        yconv = (cb_ref[...] + cw_ref[0:1, :] * _shift_down(g, c_before * h_before) + cw_ref[1:2, :] * g
                 + cw_ref[2:3, :] * _shift_up(g, c_after * h_after))
        yc_ref[...] = yconv
        sil, _ = _silu_and_grad(z_ref[...].astype(F32))
        yhat, _ = _rms(b_ref[...].astype(F32) * yconv * sil, bw)
        o_ref[...] = (yhat * gb_ref[...]).astype(BF16)

    return _row_call(
        body, name, t, r,
        [_row_spec(r, bw, c0), _row_spec(r, bw, c0 + 1), _row_spec(r, bw, c0 + 2), _row_spec(r, bw, c0 + 3),
         cp, cn, hp, hn, _const_spec((CONV_TAPS, bw)), _const_spec((1, bw)), _const_spec((1, bw))],
        (proj, proj, proj, proj, proj, proj, proj, proj, cw, cb_, gb),
        [_row_spec(r, bw), _row_spec(r, bw)],
        [jax.ShapeDtypeStruct((t, bw), BF16), jax.ShapeDtypeStruct((t, bw), F32)], into=into)


def _conv_bwd_gate(proj, off, bw, yconv, gb, dy, name):
    t = proj.shape[0]
    r = _pick(t, (256, 128))
    c0 = _col_block(off, bw)

    def body(b_ref, z_ref, yc_ref, gb_ref, dy_ref, dyc_ref, db_ref, dz_ref, dgb_ref, dcb_ref):
        _zero_at_first_step([dgb_ref, dcb_ref])
        b, z, yconv_v = b_ref[...].astype(F32), z_ref[...].astype(F32), yc_ref[...]
        sil, dsil = _silu_and_grad(z)
        yhat, rr = _rms(b * yconv_v * sil, bw)
        dy_f = dy_ref[...].astype(F32)
        dgb_ref[...] += _colsum(dy_f * yhat)
        dyb = _rms_bwd(dy_f * gb_ref[...], yhat, rr, bw)
        dyc = dyb * b * sil
        dyc_ref[...] = dyc
        dcb_ref[...] += _colsum(dyc)
        db_ref[...] = (dyb * yconv_v * sil).astype(BF16)
        dz_ref[...] = (dyb * b * yconv_v * dsil).astype(BF16)

    return _row_call(
        body, name, t, r,
        [_row_spec(r, bw, c0), _row_spec(r, bw, c0 + 3), _row_spec(r, bw), _const_spec((1, bw)), _row_spec(r, bw, 1)],
        (proj, proj, yconv, gb, dy),
        [_row_spec(r, bw), _row_spec(r, bw), _row_spec(r, bw), _const_spec((1, bw)), _const_spec((1, bw))],
        [jax.ShapeDtypeStruct((t, bw), F32), jax.ShapeDtypeStruct((t, bw), BF16), jax.ShapeDtypeStruct((t, bw), BF16),
         jax.ShapeDtypeStruct((1, bw), F32), jax.ShapeDtypeStruct((1, bw), F32)])


def _conv_bwd_taps(proj, off, bw, dyc, cw, d_gate_b, d_gate_z, name):
    t = proj.shape[0]
    r = _pick(t, (256, 128))
    n_steps = t // r
    c0 = _col_block(off, bw)
    cp, cn = _halo_specs(t, r, bw, c0 + 1, HALO_ROWS)
    hp, hn = _halo_specs(t, r, bw, c0 + 2, HALO_ROWS)
    dp, dn = _halo_specs(t, r, bw, 0, 8)

    def body(c_ref, h_ref, cp_ref, cn_ref, hp_ref, hn_ref, d_ref, dp_ref, dn_ref, cw_ref, dgb_ref, dgz_ref,
             db_ref, dcw_ref):
        _zero_at_first_step([dcw_ref])
        c, h, d = c_ref[...].astype(F32), h_ref[...].astype(F32), d_ref[...]
        g = c * h
        c_before, c_after = _edge_rows(cp_ref, cn_ref, n_steps)
        h_before, h_after = _edge_rows(hp_ref, hn_ref, n_steps)
        d_before, d_after = _edge_rows(dp_ref, dn_ref, n_steps)
        dg = (cw_ref[0:1, :] * _shift_up(d, d_after) + cw_ref[1:2, :] * d + cw_ref[2:3, :] * _shift_down(d, d_before))
        db_ref[:, 0:bw] = dgb_ref[...]
        db_ref[:, bw:2 * bw] = (dg * h).astype(BF16)
        db_ref[:, 2 * bw:3 * bw] = (dg * c).astype(BF16)
        db_ref[:, 3 * bw:4 * bw] = dgz_ref[...]
        dcw_ref[0:1, :] += _colsum(d * _shift_down(g, c_before * h_before))
        dcw_ref[1:2, :] += _colsum(d * g)
        dcw_ref[2:3, :] += _colsum(d * _shift_up(g, c_after * h_after))

    return _row_call(
        body, name, t, r,
        [_row_spec(r, bw, c0 + 1), _row_spec(r, bw, c0 + 2), cp, cn, hp, hn, _row_spec(r, bw), dp, dn,
         _const_spec((CONV_TAPS, bw)), _row_spec(r, bw), _row_spec(r, bw)],
        (proj, proj, proj, proj, proj, proj, dyc, dyc, dyc, cw, d_gate_b, d_gate_z),
        [_row_spec(r, 4 * bw), _const_spec((CONV_TAPS, bw))],
        [jax.ShapeDtypeStruct((t, 4 * bw), BF16), jax.ShapeDtypeStruct((CONV_TAPS, bw), F32)])


def _mla_prep_fwd(proj, q_off, ckv_off, kr_off, ch, kvr, tabs, qn_g, qr_g, kr_g, kv_g, name):
    t = proj.shape[0]
    r = _pick(t, (256, 128))
    qw = ch * 2 * HEAD
    cos_t, sin_a, sin_b = tabs

    def body(q_ref, ckv_ref, kr_ref, cos_ref, sa_ref, sb_ref, qn_ref, qr_ref, krg_ref, kvg_ref,
             qo_ref, co_ref, ko_ref):
        cos_v, sa, sb = cos_ref[...], sa_ref[...], sb_ref[...]
        for hh in range(ch):
            lo = hh * 2 * HEAD
            nhat, _ = _rms(q_ref[:, lo:lo + HEAD].astype(F32), HEAD)
            qo_ref[:, lo:lo + HEAD] = (nhat * qn_ref[...]).astype(BF16)
            rhat, _ = _rms(q_ref[:, lo + HEAD:lo + 2 * HEAD].astype(F32), ROPE)
            qo_ref[:, lo + HEAD:lo + 2 * HEAD] = _rope(rhat * qr_ref[...], cos_v, sa, sb).astype(BF16)
        khat, _ = _rms(kr_ref[...].astype(F32), ROPE)
        ko_ref[...] = _rope(khat * krg_ref[...], cos_v, sa, sb).astype(BF16)
        chat, _ = _rms(ckv_ref[...].astype(F32), kvr)
        co_ref[...] = (chat * kvg_ref[...]).astype(BF16)

    tab = _row_spec(r, HEAD)
    gain = _const_spec((1, HEAD))
    return _row_call(
        body, name, t, r,
        [_row_spec(r, qw, _col_block(q_off, qw)), _row_spec(r, kvr, _col_block(ckv_off, kvr)),
         _row_spec(r, HEAD, _col_block(kr_off, HEAD)), tab, tab, tab, gain, gain, gain, _const_spec((1, kvr))],
        (proj, proj, proj, cos_t, sin_a, sin_b, qn_g, qr_g, kr_g, kv_g),
        [_row_spec(r, qw), _row_spec(r, kvr), _row_spec(r, HEAD)],
        [jax.ShapeDtypeStruct((t, qw), BF16), jax.ShapeDtypeStruct((t, kvr), BF16),
         jax.ShapeDtypeStruct((t, HEAD), BF16)])


def _mla_prep_bwd(proj, q_off, ckv_off, kr_off, ch, kvr, tabs, qn_g, qr_g, kr_g, kv_g, dq_cat_t, dq_scale, dckv_n,
                  dkr_rot, kr_width, name):
    t = proj.shape[0]
    r = _pick(t, (256, 128))
    qw = ch * 2 * HEAD
    cos_t, sin_a, sin_b = tabs

    def body(q_ref, ckv_ref, kr_ref, cos_ref, sa_ref, sb_ref, qn_ref, qr_ref, krg_ref, kvg_ref,
             dq_ref, dc_ref, dk_ref, dqo_ref, dco_ref, dko_ref, dqn_ref, dqr_ref, dkrg_ref, dkvg_ref):
        _zero_at_first_step([dqn_ref, dqr_ref, dkrg_ref, dkvg_ref])
        cos_v, sa, sb = cos_ref[...], sa_ref[...], sb_ref[...]
        dq = dq_ref[...].T * dq_scale
        for hh in range(ch):
            lo = hh * 2 * HEAD
            nhat, nr = _rms(q_ref[:, lo:lo + HEAD].astype(F32), HEAD)
            d_n = dq[:, lo:lo + HEAD]
            dqn_ref[...] += _colsum(d_n * nhat)
            dqo_ref[:, lo:lo + HEAD] = _rms_bwd(d_n * qn_ref[...], nhat, nr, HEAD).astype(BF16)
            rhat, rr = _rms(q_ref[:, lo + HEAD:lo + 2 * HEAD].astype(F32), ROPE)
            d_t = _rope_bwd(dq[:, lo + HEAD:lo + 2 * HEAD], cos_v, sa, sb)
            dqr_ref[...] += _colsum(d_t * rhat)
            dqo_ref[:, lo + HEAD:lo + 2 * HEAD] = _rms_bwd(d_t * qr_ref[...], rhat, rr, ROPE).astype(BF16)
        khat, kr_r = _rms(kr_ref[...].astype(F32), ROPE)
        d_k = _rope_bwd(dk_ref[...], cos_v, sa, sb)
        dkrg_ref[...] += _colsum(d_k * khat)
        dko_ref[:, 0:HEAD] = _rms_bwd(d_k * krg_ref[...], khat, kr_r, ROPE).astype(BF16)
        if kr_width > HEAD:
            dko_ref[:, HEAD:kr_width] = jnp.zeros((r, kr_width - HEAD), BF16)
        chat, cr = _rms(ckv_ref[...].astype(F32), kvr)
        d_c = dc_ref[...].astype(F32)
        dkvg_ref[...] += _colsum(d_c * chat)
        dco_ref[...] = _rms_bwd(d_c * kvg_ref[...], chat, cr, kvr).astype(BF16)

    tab = _row_spec(r, HEAD)
    gain = _const_spec((1, HEAD))
    return _row_call(
        body, name, t, r,
        [_row_spec(r, qw, _col_block(q_off, qw)), _row_spec(r, kvr, _col_block(ckv_off, kvr)),
         _row_spec(r, HEAD, _col_block(kr_off, HEAD)), tab, tab, tab, gain, gain, gain, _const_spec((1, kvr)),
         pl.BlockSpec((qw, r), lambda i: (0, i)), _row_spec(r, kvr), _row_spec(r, HEAD)],
        (proj, proj, proj, cos_t, sin_a, sin_b, qn_g, qr_g, kr_g, kv_g, dq_cat_t, dckv_n, dkr_rot),
        [_row_spec(r, qw), _row_spec(r, kvr), _row_spec(r, kr_width), gain, gain, gain, _const_spec((1, kvr))],
        [jax.ShapeDtypeStruct((t, qw), BF16), jax.ShapeDtypeStruct((t, kvr), BF16),
         jax.ShapeDtypeStruct((t, kr_width), BF16), jax.ShapeDtypeStruct((1, HEAD), F32),
         jax.ShapeDtypeStruct((1, HEAD), F32), jax.ShapeDtypeStruct((1, HEAD), F32),
         jax.ShapeDtypeStruct((1, kvr), F32)])


def _kv_prep_fwd(kv, kr_rot, ch, kn_g, name):
    t = kv.shape[0]
    r = _pick(t, (256, 128))
    qw = ch * 2 * HEAD

    def body(kv_ref, kr_ref, kn_ref, ko_ref, kt_ref, vo_ref):
        ones = jnp.ones((r, HEAD), BF16)
        for hh in range(ch):
            lo = hh * 2 * HEAD
            nhat, _ = _rms(kv_ref[:, lo:lo + HEAD], HEAD)
            ko_ref[:, lo:lo + HEAD] = (nhat * kn_ref[...]).astype(BF16)
            ko_ref[:, lo + HEAD:lo + 2 * HEAD] = kr_ref[...]
            vo_ref[:, lo:lo + HEAD] = kv_ref[:, lo + HEAD:lo + 2 * HEAD].astype(BF16)
            vo_ref[:, lo + HEAD:lo + 2 * HEAD] = ones
        kt_ref[...] = ko_ref[...].astype(F32).T.astype(BF16)

    return _row_call(
        body, name, t, r, [_row_spec(r, qw), _row_spec(r, HEAD), _const_spec((1, HEAD))], (kv, kr_rot, kn_g),
        [_row_spec(r, qw), pl.BlockSpec((qw, r), lambda i: (0, i)), _row_spec(r, qw)],
        [jax.ShapeDtypeStruct((t, qw), BF16), jax.ShapeDtypeStruct((qw, t), BF16),
         jax.ShapeDtypeStruct((t, qw), BF16)])


def _kv_prep_bwd(kv, ch, kn_g, dk_cat, dv, name):
    t = kv.shape[0]
    r = _pick(t, (256, 128))
    qw = ch * 2 * HEAD

    def body(kv_ref, kn_ref, dk_ref, dv_ref, dkv_ref, dkr_ref, dkn_ref):
        _zero_at_first_step([dkn_ref])
        dkr = jnp.zeros((r, HEAD), F32)
        for hh in range(ch):
            lo = hh * 2 * HEAD
            nhat, nr = _rms(kv_ref[:, lo:lo + HEAD], HEAD)
            d_n = dk_ref[:, lo:lo + HEAD].astype(F32)
            dkn_ref[...] += _colsum(d_n * nhat)
            dkv_ref[:, lo:lo + HEAD] = _rms_bwd(d_n * kn_ref[...], nhat, nr, HEAD).astype(BF16)
            dkv_ref[:, lo + HEAD:lo + 2 * HEAD] = dv_ref[:, hh * HEAD:(hh + 1) * HEAD]
            dkr = dkr + dk_ref[:, lo + HEAD:lo + 2 * HEAD].astype(F32)
        dkr_ref[...] = dkr

    return _row_call(
        body, name, t, r,
        [_row_spec(r, qw), _const_spec((1, HEAD)), _row_spec(r, qw), _row_spec(r, ch * HEAD)], (kv, kn_g, dk_cat, dv),
        [_row_spec(r, qw), _row_spec(r, HEAD), _const_spec((1, HEAD))],
        [jax.ShapeDtypeStruct((t, qw), BF16), jax.ShapeDtypeStruct((t, HEAD), F32),
         jax.ShapeDtypeStruct((1, HEAD), F32)])


def _attn_tiles(t):
    return _pick(t, (2048, 1024, 512, 256, 128)), _pick(t, (1024, 512, 256, 128))


_NT = (((1,), (1,)), ((), ()))
LOG2E = 1.4426950408889634


def _attn_fwd(q_cat, k_cat, v_aug, ch, scale, name):
    t = q_cat.shape[0]
    tq, tk = _attn_tiles(t)
    nk = t // tk
    c2 = scale * LOG2E

    def body(q_ref, k_ref, v_ref, o_ref, lse_ref, s_scr, m_scr, acc_scr):
        j = pl.program_id(2)

        def scores(slot):
            s_scr[slot] = lax.dot_general(q_ref[...], k_ref[...], _NT, preferred_element_type=F32) * c2

        def absorb(slot):
            s = s_scr[slot]
            m_old = m_scr[...]
            m_new = jnp.maximum(m_old, jnp.max(s, axis=-1, keepdims=True))
            p = jnp.exp2(s - m_new).astype(BF16)
            acc_scr[...] = (jnp.exp2(m_old - m_new) * acc_scr[...]
                            + jnp.dot(p, v_ref[...], preferred_element_type=F32))
            m_scr[...] = m_new

        @pl.when(j == 0)
        def _():
            m_scr[...] = jnp.full(m_scr.shape, -jnp.inf, F32)
            acc_scr[...] = jnp.zeros(acc_scr.shape, F32)
            scores(0)

        for parity in (0, 1):
            @pl.when((j > 0) & (j < nk) & (j % 2 == parity))
            def _():
                scores(parity)
                absorb(1 - parity)

        @pl.when(j == nk)
        def _():
            absorb((nk - 1) % 2)
            acc = acc_scr[...]
            l_sum = acc[:, HEAD:]
            o_ref[...] = (acc[:, :HEAD] / l_sum).astype(BF16)
            lse_ref[0] = m_scr[...] + jnp.log(l_sum[:, 0:1]) * LOG2E

    return pl.pallas_call(
        body, name=name, grid=(ch, t // tq, nk + 1),
        in_specs=[pl.BlockSpec((tq, 2 * HEAD), lambda h, i, j: (i, h)),
                  pl.BlockSpec((tk, 2 * HEAD), lambda h, i, j: (jnp.minimum(j, nk - 1), h)),
                  pl.BlockSpec((tk, 2 * HEAD), lambda h, i, j: (jnp.maximum(j - 1, 0), h))],
        out_specs=[pl.BlockSpec((tq, HEAD), lambda h, i, j: (i, h)),
                   pl.BlockSpec((1, tq, 1), lambda h, i, j: (h, i, 0))],
        out_shape=[jax.ShapeDtypeStruct((t, ch * HEAD), BF16), jax.ShapeDtypeStruct((ch, t, 1), F32)],
        scratch_shapes=[pltpu.VMEM((2, tq, tk), F32), pltpu.VMEM((tq, 1), F32), pltpu.VMEM((tq, 2 * HEAD), F32)],
        compiler_params=_params(("parallel", "parallel", "arbitrary")),
    )(q_cat, k_cat, v_aug)


def _attn_bwd(q_cat, k_cat, k_cat_t, v_aug, do, lse_row, d_row, ch, scale, name):
    t = q_cat.shape[0]
    tk = _pick(t, (1024, 512, 256, 128))
    tq = _pick(t, (1024, 512, 256, 128))
    nk, nq = t // tk, t // tq
    c2 = scale * LOG2E

    def body(q_ref, do_ref, qp_ref, dop_ref, lse_ref, d_ref, k_ref, kt_ref, v_ref,
             dqt_ref, dk_ref, dv_ref, s_scr, dp_scr, dk_scr, dv_scr):
        ki, j = pl.program_id(1), pl.program_id(2)

        def products(slot):
            s_scr[slot] = lax.dot_general(k_ref[...], q_ref[...], _NT, preferred_element_type=F32) * c2
            dp_scr[slot] = lax.dot_general(v_ref[...], do_ref[...], _NT, preferred_element_type=F32)

        def absorb(slot):
            q, do_v = qp_ref[...], dop_ref[...]
            pt = jnp.exp2(s_scr[slot] - lse_ref[0])
            dv_scr[...] += jnp.dot(pt.astype(BF16), do_v, preferred_element_type=F32)
            dst = (pt * (dp_scr[slot] - d_ref[0])).astype(BF16)
            dk_scr[...] += jnp.dot(dst, q, preferred_element_type=F32)
            part = jnp.dot(kt_ref[...], dst, preferred_element_type=F32)
            cols = pl.ds(pl.multiple_of((j - 1) * tq, tq), tq)

            @pl.when(ki == 0)
            def _():
                dqt_ref[:, cols] = part

            @pl.when(ki > 0)
            def _():
                dqt_ref[:, cols] += part

        @pl.when(j == 0)
        def _():
            dk_scr[...] = jnp.zeros(dk_scr.shape, F32)
            dv_scr[...] = jnp.zeros(dv_scr.shape, F32)
            products(0)

        for parity in (0, 1):
            @pl.when((j > 0) & (j < nq) & (j % 2 == parity))
            def _():
                products(parity)
                absorb(1 - parity)

        @pl.when(j == nq)
        def _():
            absorb((nq - 1) % 2)
            dk_ref[...] = (dk_scr[...] * scale).astype(BF16)
            dv_ref[...] = dv_scr[...].astype(BF16)

    def cur(i):
        return jnp.minimum(i, nq - 1)

    def prev(i):
        return jnp.maximum(i - 1, 0)

    stat = pl.BlockSpec((1, 1, tq), lambda h, j, i: (h, 0, prev(i)))
    return pl.pallas_call(
        body, name=name, grid=(ch, nk, nq + 1),
        in_specs=[pl.BlockSpec((tq, 2 * HEAD), lambda h, j, i: (cur(i), h)),
                  pl.BlockSpec((tq, HEAD), lambda h, j, i: (cur(i), h)),
                  pl.BlockSpec((tq, 2 * HEAD), lambda h, j, i: (prev(i), h)),
                  pl.BlockSpec((tq, HEAD), lambda h, j, i: (prev(i), h)), stat, stat,
                  pl.BlockSpec((tk, 2 * HEAD), lambda h, j, i: (j, h)),
                  pl.BlockSpec((2 * HEAD, tk), lambda h, j, i: (h, j)),
                  pl.BlockSpec((tk, HEAD), lambda h, j, i: (j, 2 * h))],
        out_specs=[pl.BlockSpec((2 * HEAD, t), lambda h, j, i: (h, 0)),
                   pl.BlockSpec((tk, 2 * HEAD), lambda h, j, i: (j, h)),
                   pl.BlockSpec((tk, HEAD), lambda h, j, i: (j, h))],
        out_shape=[jax.ShapeDtypeStruct((ch * 2 * HEAD, t), F32), jax.ShapeDtypeStruct((t, ch * 2 * HEAD), BF16),
                   jax.ShapeDtypeStruct((t, ch * HEAD), BF16)],
        scratch_shapes=[pltpu.VMEM((2, tk, tq), F32), pltpu.VMEM((2, tk, tq), F32),
                        pltpu.VMEM((tk, 2 * HEAD), F32), pltpu.VMEM((tk, HEAD), F32)],
        compiler_params=_params(("parallel", "arbitrary", "arbitrary")),
    )(q_cat, do, q_cat, do, lse_row, d_row, k_cat, k_cat_t, v_aug)


def _attn_post_fwd(o, proj, z_off, cw, gc, name, into=None):
    t = o.shape[0]
    r = _pick(t, (256, 128))

    def body(o_ref, z_ref, gc_ref, y_ref):
        sil, _ = _silu_and_grad(z_ref[...].astype(F32))
        yhat, _ = _rms(o_ref[...].astype(F32) * sil, cw)
        y_ref[...] = (yhat * gc_ref[...]).astype(BF16)

    return _row_call(body, name, t, r,
                     [_row_spec(r, cw), _row_spec(r, cw, _col_block(z_off, cw)), _const_spec((1, cw))], (o, proj, gc),
                     _row_spec(r, cw), jax.ShapeDtypeStruct((t, cw), BF16), into=into)


def _attn_post_bwd(o, proj, z_off, cw, gc, dy, dy_col, name):
    t = o.shape[0]
    ch = cw // HEAD
    r = _pick(t, (256, 128))

    def body(o_ref, z_ref, gc_ref, dy_ref, do_ref, dz_ref, ds_ref, dgc_ref):
        _zero_at_first_step([dgc_ref])
        o_v, z = o_ref[...].astype(F32), z_ref[...].astype(F32)
        sil, dsil = _silu_and_grad(z)
        yhat, rr = _rms(o_v * sil, cw)
        dy_f = dy_ref[...].astype(F32)
        dgc_ref[...] += _colsum(dy_f * yhat)
        dyc = _rms_bwd(dy_f * gc_ref[...], yhat, rr, cw)
        do_b = (dyc * sil).astype(BF16)
        do_ref[...] = do_b
        dz_ref[...] = (dyc * o_v * dsil).astype(BF16)
        prod = do_b.astype(F32) * o_v
        for hh in range(ch):
            ds_ref[hh] = jnp.sum(prod[:, hh * HEAD:(hh + 1) * HEAD], axis=-1, keepdims=True)

    return _row_call(
        body, name, t, r,
        [_row_spec(r, cw), _row_spec(r, cw, _col_block(z_off, cw)), _const_spec((1, cw)), _row_spec(r, cw, dy_col)],
        (o, proj, gc, dy),
        [_row_spec(r, cw), _row_spec(r, cw), pl.BlockSpec((ch, r, 1), lambda i: (0, i, 0)), _const_spec((1, cw))],
        [jax.ShapeDtypeStruct((t, cw), BF16), jax.ShapeDtypeStruct((t, cw), BF16),
         jax.ShapeDtypeStruct((ch, t, 1), F32), jax.ShapeDtypeStruct((1, cw), F32)])


def _ple_fwd(h1, gpre, pp, name):
    t, d = h1.shape
    r = _pick(t, (256, 128))

    def body(h_ref, g_ref, p_ref, o_ref):
        o_ref[...] = h_ref[...] + _sigmoid(g_ref[...]) * p_ref[...]

    return _row_call(body, name, t, r, [_row_spec(r, d)] * 3, (h1, gpre, pp), _row_spec(r, d),
                     jax.ShapeDtypeStruct((t, d), F32))


def _ple_bwd(gpre, pp, dh, name):
    t, d = dh.shape
    r = _pick(t, (256, 128))

    def body(g_ref, p_ref, dh_ref, dg_ref, dp_ref):
        sig = _sigmoid(g_ref[...])
        dh_v = dh_ref[...]
        dg_ref[...] = (dh_v * p_ref[...] * sig * (1.0 - sig)).astype(BF16)
        dp_ref[...] = (dh_v * sig).astype(BF16)

    return _row_call(body, name, t, r, [_row_spec(r, d)] * 3, (gpre, pp, dh), [_row_spec(r, d)] * 2,
                     [jax.ShapeDtypeStruct((t, d), BF16)] * 2)


def _loss_and_grad(h, target, name):
    t, d = h.shape
    r = _pick(t, (256, 128))

    def body(h_ref, t_ref, l_ref, dh_ref):
        _zero_at_first_step([l_ref])
        err = h_ref[...] - t_ref[...]
        l_ref[...] += jnp.sum(jnp.sum(err * err, axis=-1, keepdims=True), axis=0, keepdims=True) * (0.5 / d)
        dh_ref[...] = err * (1.0 / d)

    return _row_call(body, name, t, r, [_row_spec(r, d)] * 2, (h, target), [_const_spec((1, 1)), _row_spec(r, d)],
                     [jax.ShapeDtypeStruct((1, 1), F32), jax.ShapeDtypeStruct((t, d), F32)])


def _ew_rows(rows, cols):
    cap = min(1024, max(8, (1 << 19) // max(cols, 1)))
    for cand in range(cap - cap % 8, 7, -8):
        if rows % cand == 0:
            return cand
    return rows


def _pair_sum_bf16(a, b, name):
    n, rows, cols = a.shape
    rb = _ew_rows(rows, cols)

    def body(a_ref, b_ref, o_ref):
        o_ref[...] = (a_ref[...] + b_ref[...]).astype(BF16)

    spec = pl.BlockSpec((1, rb, cols), lambda s, i: (s, i, 0))
    return pl.pallas_call(body, name=name, grid=(n, rows // rb), in_specs=[spec, spec], out_specs=spec,
                          out_shape=jax.ShapeDtypeStruct(a.shape, BF16),
                          compiler_params=_params(("parallel", "parallel")))(a, b)


def _shard_sum(a, b, recv, name):
    rows, cols = a.shape
    rb = _ew_rows(rows, cols)

    def body(a_ref, b_ref, r_ref, o_ref):
        o_ref[...] = ((a_ref[...] + b_ref[...]) + r_ref[0].astype(F32) + r_ref[1].astype(F32)
                      + r_ref[2].astype(F32))

    spec = pl.BlockSpec((rb, cols), lambda i: (i, 0))
    return pl.pallas_call(body, name=name, grid=(rows // rb,),
                          in_specs=[spec, spec, pl.BlockSpec((N_SHARD - 1, rb, cols), lambda i: (0, i, 0))],
                          out_specs=spec, out_shape=jax.ShapeDtypeStruct(a.shape, F32),
                          compiler_params=_params(("parallel",)))(a, b, recv)


def _sum_devices(g, name):
    n, rows, cols = g.shape
    rb = _ew_rows(rows, cols)

    def body(g_ref, o_ref):
        acc = g_ref[0]
        for k in range(1, n):
            acc = acc + g_ref[k]
        o_ref[...] = acc

    return pl.pallas_call(body, name=name, grid=(rows // rb,),
                          in_specs=[pl.BlockSpec((n, rb, cols), lambda i: (0, i, 0))],
                          out_specs=pl.BlockSpec((rb, cols), lambda i: (i, 0)),
                          out_shape=jax.ShapeDtypeStruct((rows, cols), F32),
                          compiler_params=_params(("parallel",)))(g)


def _adamw_update(w, g_v, m, v):
    m_new = ADAM_B1 * m + (1.0 - ADAM_B1) * g_v
    v_new = ADAM_B2 * v + (1.0 - ADAM_B2) * (g_v * g_v)
    m_hat = m_new / (1.0 - ADAM_B1 ** ADAM_STEP)
    v_hat = v_new / (1.0 - ADAM_B2 ** ADAM_STEP)
    return -ADAM_LR * (m_hat / (jnp.sqrt(v_hat) + ADAM_EPS) + ADAM_WD * w), m_new, v_new


def _adamw(w, g, m, v, name):
    rows, cols = w.shape
    rb = _ew_rows(rows, cols)

    def body(w_ref, g_ref, m_ref, v_ref, d_ref, mo_ref, vo_ref):
        d_ref[...], mo_ref[...], vo_ref[...] = _adamw_update(w_ref[...], g_ref[...], m_ref[...], v_ref[...])

    spec = pl.BlockSpec((rb, cols), lambda i: (i, 0))
    return pl.pallas_call(body, name=name, grid=(rows // rb,), in_specs=[spec] * 4, out_specs=[spec] * 3,
                          out_shape=[jax.ShapeDtypeStruct(w.shape, F32)] * 3,
                          compiler_params=_params(("parallel",)))(w, g, m, v)


def _adamw_two_halves(w, own, recv, core_flag, m, v, name):
    depth, rows, cols = w.shape
    assert rows % 2 == 0 and own.shape == (depth, rows // 2, cols)
    rb = _ew_rows(rows // 2, cols)
    nb = rows // 2 // rb

    def body(w_ref, own_ref, recv_ref, flag_ref, m_ref, v_ref, g_ref, d_ref, mo_ref, vo_ref):
        half = pl.program_id(1).astype(F32)
        g_v = jnp.where(flag_ref[...] == half, own_ref[...], recv_ref[...])
        g_ref[...] = g_v
        d_ref[...], mo_ref[...], vo_ref[...] = _adamw_update(w_ref[...], g_v, m_ref[...], v_ref[...])

    full = pl.BlockSpec((None, rb, cols), lambda l, k, i: (l, k * nb + i, 0))
    half_spec = pl.BlockSpec((None, rb, cols), lambda l, k, i: (l, i, 0))
    return pl.pallas_call(
        body, name=name, grid=(depth, 2, nb),
        in_specs=[full, half_spec, half_spec, pl.BlockSpec((1, 1), lambda l, k, i: (0, 0)), full, full],
        out_specs=[full] * 4, out_shape=[jax.ShapeDtypeStruct(w.shape, F32)] * 4,
        compiler_params=_params(("parallel", "parallel", "parallel")))(w, own, recv, core_flag, m, v)


def _place():
    return lax.axis_index("x"), lax.axis_index("y"), lax.axis_index("c")


def _other_chips(x, y):
    return [(1 - x, y), (x, 1 - y), (1 - x, 1 - y)]


_ANY = pl.BlockSpec(memory_space=pl.ANY)


class _Rider:
    def __init__(self, arrays, out_shapes, n_sems, start, finish, in_place=False):
        self.arrays, self.out_shapes, self.n_sems = list(arrays), list(out_shapes), n_sems
        self.start, self.finish, self.in_place = start, finish, in_place

    def sems(self):
        return [pltpu.SemaphoreType.DMA((self.n_sems,)), pltpu.SemaphoreType.DMA((self.n_sems,))]

    def aliases(self, first_in, first_out):
        return {first_in + a: first_out + a for a in range(len(self.arrays))} if self.in_place else {}


def _comm_call(rider, name):
    n_in, n_out = len(rider.arrays), len(rider.out_shapes)

    def body(*refs):
        ins, outs = refs[:n_in], refs[n_in:n_in + n_out]
        send_sems, recv_sems = refs[n_in + n_out:]
        rider.start(ins, outs, send_sems, recv_sems)
        rider.finish(ins, outs, send_sems, recv_sems)

    return pl.pallas_call(
        body, name=name, in_specs=[_ANY] * n_in, out_specs=[_ANY] * n_out, out_shape=rider.out_shapes,
        scratch_shapes=rider.sems(), input_output_aliases=rider.aliases(0, 0))(*rider.arrays)


def _half(shape, which):
    for axis, size in enumerate(shape):
        if size % 2 == 0:
            return (slice(None),) * axis + (pl.ds(which * (size // 2), size // 2),)
    raise ValueError(f"no axis of even length in {shape}")


def _start_then_wait(copies):
    def start(*refs):
        for send, _ in copies(*refs):
            send.start()

    def finish(*refs):
        pairs = copies(*refs)
        for _, landing in pairs:
            landing.wait_recv()
        for send, _ in pairs:
            send.wait_send()

    return start, finish


def _fetch_rider(shards):
    n, n_peer = len(shards), N_SHARD - 1
    shapes = [s.shape for s in shards]

    def copies(ins, outs, send_sems, recv_sems):
        x, y, c = _place()
        pairs = []
        for a in range(n):
            mine = _half(shapes[a], c)
            for k, (px, py) in enumerate(_other_chips(x, y)):
                def into(slot):
                    return pltpu.make_async_remote_copy(
                        src_ref=ins[a].at[mine], dst_ref=outs[a].at[(slot,) + mine],
                        send_sem=send_sems.at[a * n_peer + k], recv_sem=recv_sems.at[a * n_peer + k],
                        device_id=(px, py, c), device_id_type=MESH)
                pairs.append((into(2 * x + y), into(2 * px + py)))
        return pairs

    start, finish = _start_then_wait(copies)
    return _Rider(shards, [jax.ShapeDtypeStruct((N_SHARD,) + s.shape, s.dtype) for s in shards], n * n_peer,
                  start, finish)


def _forward_rider(gathered, shards):
    n, n_peer = len(gathered), N_SHARD - 1
    shapes = [s.shape for s in shards]

    def copies(ins, outs, send_sems, recv_sems):
        x, y, c = _place()
        pairs = []
        for a in range(n):
            for k, (px, py) in enumerate(_other_chips(x, y)):
                def half_of_slot(which):
                    rows = outs[a].at[(2 * px + py,) + _half(shapes[a], which)]
                    return pltpu.make_async_remote_copy(
                        src_ref=rows, dst_ref=rows, send_sem=send_sems.at[a * n_peer + k],
                        recv_sem=recv_sems.at[a * n_peer + k], device_id=(x, y, 1 - c), device_id_type=MESH)
                pairs.append((half_of_slot(c), half_of_slot(1 - c)))
        return pairs

    start, finish = _start_then_wait(copies)
    return _Rider(gathered, [jax.ShapeDtypeStruct(g.shape, g.dtype) for g in gathered], n * n_peer, start, finish,
                  in_place=True)


def _sibling_rider(arrs, other_half):
    n = len(arrs)

    def copies(ins, outs, send_sems, recv_sems):
        x, y, c = _place()
        pairs = []
        for a in range(n):
            cp = pltpu.make_async_remote_copy(
                src_ref=ins[a].at[1 - c] if other_half else ins[a], dst_ref=outs[a], send_sem=send_sems.at[a],
                recv_sem=recv_sems.at[a], device_id=(x, y, 1 - c), device_id_type=MESH)
            pairs.append((cp, cp))
        return pairs

    start, finish = _start_then_wait(copies)
    return _Rider(arrs, [jax.ShapeDtypeStruct(g.shape[1:] if other_half else g.shape, g.dtype) for g in arrs], n,
                  start, finish)


def _owner_rider(parts):
    n, n_peer = len(parts), N_SHARD - 1

    def copies(ins, outs, send_sems, recv_sems):
        x, y, c = _place()
        pairs = []
        for a in range(n):
            for k, (px, py) in enumerate(_other_chips(x, y)):
                cp = pltpu.make_async_remote_copy(
                    src_ref=ins[a].at[2 * px + py], dst_ref=outs[a].at[k], send_sem=send_sems.at[a * n_peer + k],
                    recv_sem=recv_sems.at[a * n_peer + k], device_id=(px, py, c), device_id_type=MESH)
                pairs.append((cp, cp))
        return pairs

    start, finish = _start_then_wait(copies)
    return _Rider(parts, [jax.ShapeDtypeStruct((n_peer,) + p.shape[1:], p.dtype) for p in parts], n * n_peer,
                  start, finish)


def _gather_devices(buf, name):
    n_peer = N_DEV - 1

    def body(in_ref, out_ref, send_sems, recv_sems, local_sem):
        x, y, c = _place()
        me = 4 * x + 2 * y + c
        mine = pltpu.make_async_copy(in_ref, out_ref.at[me], local_sem)
        mine.start()
        peers = []
        for k in range(1, N_DEV):
            fx, fy, fc = (k >> 2) & 1, (k >> 1) & 1, k & 1
            peers.append((x ^ fx, y ^ fy, c ^ fc))
        sends = []
        for k, peer in enumerate(peers):
            cp = pltpu.make_async_remote_copy(
                src_ref=in_ref, dst_ref=out_ref.at[me], send_sem=send_sems.at[k], recv_sem=recv_sems.at[k],
                device_id=peer, device_id_type=MESH)
            cp.start()
            sends.append(cp)
        for k, (px, py, pc) in enumerate(peers):
            pltpu.make_async_remote_copy(
                src_ref=in_ref, dst_ref=out_ref.at[4 * px + 2 * py + pc], send_sem=send_sems.at[k],
                recv_sem=recv_sems.at[k], device_id=(px, py, pc), device_id_type=MESH).wait_recv()
        for cp in sends:
            cp.wait_send()
        mine.wait()

    return pl.pallas_call(
        body, name=name, in_specs=[_ANY], out_specs=_ANY,
        out_shape=jax.ShapeDtypeStruct((N_DEV,) + buf.shape, buf.dtype),
        scratch_shapes=[pltpu.SemaphoreType.DMA((n_peer,)), pltpu.SemaphoreType.DMA((n_peer,)),
                        pltpu.SemaphoreType.DMA(())],
    )(buf)


class _Dims:
    def __init__(self, x, p, w_in, sgu_norm, conv_w, kv_norm, w_ukv, w_out):
        self.t, self.d = x.shape[1], x.shape[2]
        self.depth = w_in.shape[0]
        self.ple = p.shape[3]
        self.in_w = w_in.shape[2] * N_SHARD
        self.ah = sgu_norm.shape[1]
        self.aw = self.ah * HEAD
        self.bw = conv_w.shape[2] * N_SHARD
        self.kvr = kv_norm.shape[1]
        self.ch = w_ukv.shape[2] * N_SHARD // (2 * HEAD)
        self.cw = self.ch * HEAD
        self.mix = w_out.shape[1] * N_SHARD
        assert self.mix == self.aw + self.bw + self.cw and self.aw == self.bw
        self.qw = self.ch * 2 * HEAD
        segs = [('a', 3 * self.aw, self.aw), ('b', 4 * self.bw, self.bw), ('ckv', self.kvr, self.kvr),
                ('q', self.qw, self.qw), ('cz', self.cw, self.cw), ('kr', HEAD, HEAD)]
        off = 0
        self.off = {}
        for nm, width, align in segs:
            off = -(-off // align) * align
            self.off[nm] = off
            off += width
        self.inp = -(-off // 512) * 512
        q_real = self.ch * (HEAD + ROPE)
        widths = [3 * self.aw, 4 * self.bw, q_real, self.kvr, ROPE, self.cw]
        assert sum(widths) == self.in_w
        starts = [0]
        for wd in widths:
            starts.append(starts[-1] + wd)
        self.src = dict(zip(['a', 'b', 'q', 'ckv', 'kr', 'cz'], zip(starts[:-1], widths)))


def _rearrange_w_in(w, dm):
    lead, d = w.shape[:-2], w.shape[-1]
    axis = w.ndim - 2

    def rows(nm):
        s, wd = dm.src[nm]
        return lax.slice_in_dim(w, s, s + wd, axis=axis)

    pieces = {nm: rows(nm) for nm in ('a', 'b', 'ckv', 'cz')}
    q = rows('q').reshape(lead + (dm.ch, HEAD + ROPE, d))
    pieces['q'] = jnp.pad(q, [(0, 0)] * (len(lead) + 1) + [(0, HEAD - ROPE), (0, 0)]).reshape(lead + (dm.qw, d))
    pieces['kr'] = jnp.pad(rows('kr'), [(0, 0)] * len(lead) + [(0, HEAD - ROPE), (0, 0)])
    out, cur = [], 0
    for nm in sorted(dm.off, key=lambda k: dm.off[k]):
        if dm.off[nm] > cur:
            out.append(jnp.zeros(lead + (dm.off[nm] - cur, d), w.dtype))
        out.append(pieces[nm])
        cur = dm.off[nm] + pieces[nm].shape[axis]
    if dm.inp > cur:
        out.append(jnp.zeros(lead + (dm.inp - cur, d), w.dtype))
    return jnp.concatenate(out, axis=axis)


def _unarrange_w_in(g, dm):
    d = g.shape[1]

    def seg(nm, width):
        return g[dm.off[nm]:dm.off[nm] + width]

    q = seg('q', dm.qw).reshape(dm.ch, 2 * HEAD, d)[:, :HEAD + ROPE].reshape(dm.ch * (HEAD + ROPE), d)
    return jnp.concatenate([seg('a', 3 * dm.aw), seg('b', 4 * dm.bw), q, seg('ckv', dm.kvr), seg('kr', ROPE),
                            seg('cz', dm.cw)], axis=0)


def _assemble_dproj(parts, dm, t):
    out, cur = [], 0
    for nm in sorted(dm.off, key=lambda k: dm.off[k]):
        if dm.off[nm] > cur:
            out.append(jnp.zeros((t, dm.off[nm] - cur), BF16))
        out.append(parts[nm])
        cur = dm.off[nm] + parts[nm].shape[-1]
    if dm.inp > cur:
        out.append(jnp.zeros((t, dm.inp - cur), BF16))
    return jnp.concatenate(out, axis=-1)


def _rope_tables(positions):
    inv = 1.0 / (ROPE_BASE ** (jnp.arange(0, ROPE, 2, dtype=F32) / ROPE))
    ang = positions.astype(F32)[:, None] * inv
    cos, sin = jnp.cos(ang), jnp.sin(ang)
    t = positions.shape[0]
    half = ROPE // 2
    cos_t = jnp.concatenate([cos, cos, jnp.zeros((t, HEAD - ROPE), F32)], axis=-1)
    sin_a = jnp.concatenate([-sin, jnp.zeros((t, HEAD - half), F32)], axis=-1)
    sin_b = jnp.concatenate([jnp.zeros((t, half), F32), sin, jnp.zeros((t, HEAD - ROPE), F32)], axis=-1)
    return cos_t, sin_a, sin_b


def _pad_gain(g):
    return jnp.pad(g, (0, HEAD - g.shape[0]))[None, :]


def _shard_major(g, axis):
    shape = g.shape
    g = g.reshape(shape[:axis] + (N_SHARD, shape[axis] // N_SHARD) + shape[axis + 1:])
    g = jnp.moveaxis(g, axis, 0)
    rows, cols = g.shape[1], g.shape[2]
    return jnp.swapaxes(g.reshape(N_SHARD, 2, rows // 2, cols), 0, 1)


def _pack(arrs):
    flat = jnp.concatenate([a.reshape(-1) for a in arrs])
    pad = (-flat.shape[0]) % (8 * HEAD)
    return jnp.pad(flat, (0, pad)).reshape(-1, HEAD)


def _unpack(buf, shapes):
    flat = buf.reshape(-1)
    out, cur = [], 0
    for s in shapes:
        size = 1
        for v in s:
            size *= v
        out.append(flat[cur:cur + size].reshape(s))
        cur += size
    return out


def kernel(x, p, positions, attn_norm, w_in, sgu_norm, w_spatial, b_spatial, conv_w, conv_b, kv_norm, w_ukv, q_nope_norm, q_rope_norm, k_nope_norm, k_rope_norm, out_norm, w_out, ple_norm, w_ple_gate, w_ple_proj, loss_target, m_attn_norm, m_w_in, m_sgu_norm, m_w_spatial, m_b_spatial, m_conv_w, m_conv_b, m_kv_norm, m_w_ukv, m_q_nope_norm, m_q_rope_norm, m_k_nope_norm, m_k_rope_norm, m_out_norm, m_w_out, m_ple_norm, m_w_ple_gate, m_w_ple_proj, v_attn_norm, v_w_in, v_sgu_norm, v_w_spatial, v_b_spatial, v_conv_w, v_conv_b, v_kv_norm, v_w_ukv, v_q_nope_norm, v_q_rope_norm, v_k_nope_norm, v_k_rope_norm, v_out_norm, v_w_out, v_ple_norm, v_w_ple_gate, v_w_ple_proj):
    weights = dict(attn_norm=attn_norm, w_in=w_in, sgu_norm=sgu_norm, w_spatial=w_spatial, b_spatial=b_spatial,
                   conv_w=conv_w, conv_b=conv_b, kv_norm=kv_norm, w_ukv=w_ukv, q_nope_norm=q_nope_norm,
                   q_rope_norm=q_rope_norm, k_nope_norm=k_nope_norm, k_rope_norm=k_rope_norm, out_norm=out_norm,
                   w_out=w_out, ple_norm=ple_norm, w_ple_gate=w_ple_gate, w_ple_proj=w_ple_proj)
    mom_m = dict(attn_norm=m_attn_norm, w_in=m_w_in, sgu_norm=m_sgu_norm, w_spatial=m_w_spatial,
                 b_spatial=m_b_spatial, conv_w=m_conv_w, conv_b=m_conv_b, kv_norm=m_kv_norm, w_ukv=m_w_ukv,
                 q_nope_norm=m_q_nope_norm, q_rope_norm=m_q_rope_norm, k_nope_norm=m_k_nope_norm,
                 k_rope_norm=m_k_rope_norm, out_norm=m_out_norm, w_out=m_w_out, ple_norm=m_ple_norm,
                 w_ple_gate=m_w_ple_gate, w_ple_proj=m_w_ple_proj)
    mom_v = dict(attn_norm=v_attn_norm, w_in=v_w_in, sgu_norm=v_sgu_norm, w_spatial=v_w_spatial,
                 b_spatial=v_b_spatial, conv_w=v_conv_w, conv_b=v_conv_b, kv_norm=v_kv_norm, w_ukv=v_w_ukv,
                 q_nope_norm=v_q_nope_norm, q_rope_norm=v_q_rope_norm, k_nope_norm=v_k_nope_norm,
                 k_rope_norm=v_k_rope_norm, out_norm=v_out_norm, w_out=v_w_out, ple_norm=v_ple_norm,
                 w_ple_gate=v_w_ple_gate, w_ple_proj=v_w_ple_proj)
    dm = _Dims(x, p, w_in, sgu_norm, conv_w, kv_norm, w_ukv, w_out)
    for group in (weights, mom_m, mom_v):
        group['w_in'] = jnp.swapaxes(group['w_in'], 1, 2)
    t, d, depth = dm.t, dm.d, dm.depth
    shard = 2 * lax.axis_index("x") + lax.axis_index("y")
    core = lax.axis_index("c")
    scale = float(HEAD + ROPE) ** -0.5

    def local_layer(i):
        return [weights[n][i:i + 1].astype(BF16) for n in BIG]

    def fill_own_slot(gathered, local):
        return [lax.dynamic_update_slice(g, mine[None], (shard,) + (0,) * mine.ndim)
                for g, mine in zip(gathered, local)]

    def layer_weights(filled):
        w = {n: jnp.concatenate([filled[j][s] for s in range(N_SHARD)], axis=BIG_AXIS[n])
             for j, n in enumerate(BIG)}
        w['w_in'] = _rearrange_w_in(w['w_in'], dm)
        return w

    first = local_layer(0) + [conv_w]
    fetched = _comm_call(_fetch_rider(first), "fetch_weights_l0")
    filled = fill_own_slot(_comm_call(_forward_rider(fetched, first), "forward_weights_l0"), first)
    layer_w = [layer_weights(filled)] + [None] * (depth - 1)
    conv_w_full = jnp.concatenate([filled[len(BIG)][s] for s in range(N_SHARD)], axis=2)

    tabs = _rope_tables(positions[0])
    h = x[0]
    saved = []
    for i in range(depth):
        tag = f"l{i}_"
        ga, gb, gc = (out_norm[i][None, :dm.aw], out_norm[i][None, dm.aw:dm.aw + dm.bw],
                      out_norm[i][None, dm.aw + dm.bw:])
        ws_b = w_spatial[i].astype(BF16)
        bb = jnp.broadcast_to(b_spatial[i][:, :, None], (dm.ah, HEAD, HEAD))
        qn_g, qr_g = q_nope_norm[i][None, :], _pad_gain(q_rope_norm[i])
        kn_g, kr_g = k_nope_norm[i][None, :], _pad_gain(k_rope_norm[i])
        kv_g = kv_norm[i][None, :]
        wl = layer_w[i]
        nxt = local_layer(i + 1) if i + 1 < depth else None
        hn = _norm_fwd(h, attn_norm[i][None, :], tag + "norm1")
        if nxt is None:
            proj = _matmul(hn, wl['w_in'], 'nt', BF16, tag + "proj", b_layer=0)
        else:
            proj, fetched = _matmul(hn, wl['w_in'], 'nt', BF16, tag + "proj", b_layer=0, rider=_fetch_rider(nxt))
        y = _sgu_fwd(proj, dm.off['a'], dm.aw, sgu_norm[i], ws_b, bb, ga, tag + "sgu",
                     into=(jnp.zeros((t, dm.mix), BF16), 0))
        y, yconv = _conv_fwd(proj, dm.off['b'], dm.bw, conv_w_full[i], conv_b[i][None, :], gb, tag + "conv",
                             into=(y, _col_block(dm.aw, dm.bw)))
        q_cat, ckv_n, kr_rot = _mla_prep_fwd(proj, dm.off['q'], dm.off['ckv'], dm.off['kr'], dm.ch, dm.kvr, tabs,
                                             qn_g, qr_g, kr_g, kv_g, tag + "mla_prep")
        kv = _matmul(ckv_n, wl['w_ukv'], 'nn', F32, tag + "kv_up", b_layer=0)
        k_cat, k_cat_t, v_aug = _kv_prep_fwd(kv, kr_rot, dm.ch, kn_g, tag + "kv_prep")
        o, lse = _attn_fwd(q_cat, k_cat, v_aug, dm.ch, scale, tag + "attn")
        y = _attn_post_fwd(o, proj, dm.off['cz'], dm.cw, gc, tag + "attn_post",
                           into=(y, _col_block(dm.aw + dm.bw, dm.cw)))
        if nxt is None:
            h1 = _matmul(y, wl['w_out'], 'nn', F32, tag + "out", add=h, b_layer=0)
        else:
            h1, gathered = _matmul(y, wl['w_out'], 'nn', F32, tag + "out", add=h, b_layer=0,
                                   rider=_forward_rider(list(fetched), nxt))
            layer_w[i + 1] = layer_weights(fill_own_slot(gathered, nxt))
        hn2 = _norm_fwd(h1, ple_norm[i][None, :], tag + "norm2")
        gpre = _matmul(hn2, wl['w_ple_gate'], 'nn', F32, tag + "gate", b_layer=0)
        p_b = p[i, 0].astype(BF16)
        pp = _matmul(p_b, wl['w_ple_proj'], 'nn', F32, tag + "ple_proj", b_layer=0)
        h2 = _ple_fwd(h1, gpre, pp, tag + "ple")
        saved.append(dict(h=h, hn=hn, proj=proj, yconv=yconv, q_cat=q_cat, ckv_n=ckv_n, kv=kv, k_cat=k_cat,
                          k_cat_t=k_cat_t,
                          v=v_aug, o=o, lse=lse, y=y, h1=h1, hn2=hn2, gpre=gpre, pp=pp, p_b=p_b, ws_b=ws_b, bb=bb,
                          gains=(ga, gb, gc, qn_g, qr_g, kn_g, kr_g, kv_g)))
        h = h2

    loss_part, dh = _loss_and_grad(h, loss_target[0], "loss")
    loss = lax.psum(loss_part[0, 0], ("x", "y", "c"))

    def chip_sums(sm, from_sibling, tag):
        mine = [lax.dynamic_index_in_dim(g, core, 0, keepdims=False) for g in sm]
        return mine, [_pair_sum_bf16(a, b, f"{tag}chip_sum_{n}") for a, b, n in zip(mine, from_sibling, BIG)]

    def shard_sums(mine, from_sibling, from_chips, tag):
        out = []
        for a, b, r3, n in zip(mine, from_sibling, from_chips, BIG):
            own_a = lax.dynamic_index_in_dim(a, shard, 0, keepdims=False)
            own_b = lax.dynamic_index_in_dim(b, shard, 0, keepdims=False)
            out.append(_shard_sum(own_a, own_b, r3, f"{tag}shard_sum_{n}"))
        return out

    grads = {n: [None] * depth for n in WEIGHTS}
    own_half = {n: [None] * depth for n in BIG}
    sibling_half = {n: [None] * depth for n in BIG}
    carry = None
    for i in reversed(range(depth)):
        tag = f"l{i}_b_"
        gtag = f"l{i + 1}_g_"
        sv = saved[i]
        wl = layer_w[i]
        ga, gb, gc, qn_g, qr_g, kn_g, kr_g, kv_g = sv['gains']
        proj = sv['proj']
        dgpre, dpp = _ple_bwd(sv['gpre'], sv['pp'], dh, tag + "ple")
        grads['w_ple_proj'][i] = _matmul(sv['p_b'], dpp, 'tn', F32, tag + "d_w_ple_proj")
        if carry is None:
            grads['w_ple_gate'][i] = _matmul(sv['hn2'], dgpre, 'tn', F32, tag + "d_w_gate")
        else:
            grads['w_ple_gate'][i], from_sibling = _matmul(sv['hn2'], dgpre, 'tn', F32, tag + "d_w_gate",
                                                           rider=_sibling_rider(carry, True))
            mine, sums = chip_sums(carry, from_sibling, gtag)
        d_hn2 = _matmul(dgpre, wl['w_ple_gate'], 'nt', BF16, tag + "d_hn2", b_layer=0)
        dh1, dh1_b, g_ple = _norm_bwd(sv['h1'], ple_norm[i][None, :], d_hn2, dh, tag + "norm2")
        grads['ple_norm'][i] = g_ple[0]
        grads['w_out'][i] = _matmul(sv['y'], dh1_b, 'tn', F32, tag + "d_w_out")
        dy = _matmul(dh1_b, wl['w_out'], 'nt', BF16, tag + "d_y", b_layer=0)
        ws_t = jnp.swapaxes(sv['ws_b'], 1, 2)
        d_a, g_sgu, g_ws, g_bs, g_ga = _sgu_bwd(proj, dm.off['a'], dm.aw, sgu_norm[i], sv['ws_b'], ws_t, sv['bb'],
                                                ga, dy, tag + "sgu")
        grads['sgu_norm'][i], grads['w_spatial'][i], grads['b_spatial'][i] = g_sgu, g_ws, g_bs[:, :, 0]
        dyc, d_bb, d_bz, g_gb, g_cb = _conv_bwd_gate(proj, dm.off['b'], dm.bw, sv['yconv'], gb, dy, tag + "conv_gate")
        d_b, g_cw = _conv_bwd_taps(proj, dm.off['b'], dm.bw, dyc, conv_w_full[i], d_bb, d_bz, tag + "conv_taps")
        grads['conv_b'][i], grads['conv_w'][i] = g_cb[0], g_cw
        d_o, d_cz, dsum, g_gc = _attn_post_bwd(sv['o'], proj, dm.off['cz'], dm.cw, gc, dy,
                                               _col_block(dm.aw + dm.bw, dm.cw), tag + "attn_post")
        grads['out_norm'][i] = jnp.concatenate([g_ga[0], g_gb[0], g_gc[0]])
        dq_t, dk_cat, dv = _attn_bwd(sv['q_cat'], sv['k_cat'], sv['k_cat_t'], sv['v'], d_o,
                                     sv['lse'].reshape(dm.ch, 1, t), dsum.reshape(dm.ch, 1, t), dm.ch, scale,
                                     tag + "attn_bwd")
        dkv, dkr_rot, g_kn = _kv_prep_bwd(sv['kv'], dm.ch, kn_g, dk_cat, dv, tag + "kv_prep")
        grads['k_nope_norm'][i] = g_kn[0]
        grads['w_ukv'][i] = _matmul(sv['ckv_n'], dkv, 'tn', F32, tag + "d_w_ukv")
        dckv_n = _matmul(dkv, wl['w_ukv'], 'nt', BF16, tag + "d_ckv", b_layer=0)
        d_q, d_ckv, d_kr, g_qn, g_qr, g_kr, g_kv = _mla_prep_bwd(
            proj, dm.off['q'], dm.off['ckv'], dm.off['kr'], dm.ch, dm.kvr, tabs, qn_g, qr_g, kr_g, kv_g,
            dq_t, scale, dckv_n, dkr_rot, dm.inp - dm.off['kr'], tag + "mla_prep")
        grads['q_nope_norm'][i], grads['q_rope_norm'][i] = g_qn[0], g_qr[0, :ROPE]
        grads['k_rope_norm'][i], grads['kv_norm'][i] = g_kr[0, :ROPE], g_kv[0]
        dproj = _assemble_dproj(dict(a=d_a, b=d_b, ckv=d_ckv, q=d_q, cz=d_cz, kr=d_kr), dm, t)
        if carry is None:
            d_w_in = _matmul(dproj, sv['hn'], 'tn', F32, tag + "d_w_in")
            d_hn = _matmul(dproj, wl['w_in'], 'nn', BF16, tag + "d_hn", b_layer=0)
        else:
            d_w_in, from_chips = _matmul(dproj, sv['hn'], 'tn', F32, tag + "d_w_in", rider=_owner_rider(sums))
            halves = shard_sums(mine, from_sibling, from_chips, gtag)
            d_hn, from_core = _matmul(dproj, wl['w_in'], 'nn', BF16, tag + "d_hn", b_layer=0,
                                      rider=_sibling_rider(halves, False))
            for n, own, recv in zip(BIG, halves, from_core):
                own_half[n][i + 1], sibling_half[n][i + 1] = own, recv
        grads['w_in'][i] = _unarrange_w_in(d_w_in, dm)
        dh, _, g_an = _norm_bwd(sv['h'], attn_norm[i][None, :], d_hn, dh1, tag + "norm1")
        grads['attn_norm'][i] = g_an[0]
        carry = [_shard_major(grads[n][i], BIG_AXIS[n] - 1) for n in BIG]
    grad_x = dh[None]

    from_sibling = _comm_call(_sibling_rider(carry, True), "l0_g_to_sibling")
    mine, sums = chip_sums(carry, from_sibling, "l0_g_")
    from_chips = _comm_call(_owner_rider(sums), "l0_g_to_owner_chips")
    halves = shard_sums(mine, from_sibling, from_chips, "l0_g_")
    from_core = _comm_call(_sibling_rider(halves, False), "l0_g_share_sibling")
    for n, own, recv in zip(BIG, halves, from_core):
        own_half[n][0], sibling_half[n][0] = own, recv

    core_flag = core.astype(F32).reshape(1, 1)
    out_g, out_d, out_m, out_v = {}, {}, {}, {}
    for n in BIG:
        out_g[n], out_d[n], out_m[n], out_v[n] = _adamw_two_halves(
            weights[n], jnp.stack(own_half[n]), jnp.stack(sibling_half[n]), core_flag, mom_m[n], mom_v[n],
            f"adamw_{n}")
    for out in (out_g, out_d, out_m, out_v):
        out['w_in'] = jnp.swapaxes(out['w_in'], 1, 2)
    grads = {n: jnp.stack(grads[n]) for n in SMALL}

    shapes = [grads[n].shape for n in SMALL]
    summed = _unpack(_sum_devices(_gather_devices(_pack([grads[n] for n in SMALL]), "gather_small_grads"),
                                  "sum_small_grads"), shapes)
    small_g = dict(zip(SMALL, summed))
    small_g['conv_w'] = lax.dynamic_slice_in_dim(small_g['conv_w'], shard * conv_w.shape[2], conv_w.shape[2], axis=2)
    local_shapes = [weights[n].shape for n in SMALL]
    d_s, m_s, v_s = _adamw(_pack([weights[n] for n in SMALL]), _pack([small_g[n] for n in SMALL]),
                           _pack([mom_m[n] for n in SMALL]), _pack([mom_v[n] for n in SMALL]), "adamw_small")
    for n, dd, mm, vv in zip(SMALL, _unpack(d_s, local_shapes), _unpack(m_s, local_shapes),
                             _unpack(v_s, local_shapes)):
        out_g[n], out_d[n], out_m[n], out_v[n] = small_g[n], dd, mm, vv

    return (loss, grad_x, *[out_g[n] for n in WEIGHTS], *[out_d[n] for n in WEIGHTS],
            *[out_m[n] for n in WEIGHTS], *[out_v[n] for n in WEIGHTS])
```

```python
import functools

import jax
import jax.numpy as jnp
from jax import lax
from jax.experimental import pallas as pl
from jax.experimental.pallas import tpu as pltpu

F32 = jnp.float32
BF16 = jnp.bfloat16
EPS = 1e-6
HEAD = 128
ROPE = 64
ROPE_BASE = 10000.0
CONV_TAPS = 3
N_SHARD = 4
N_DEV = 8
ADAM_LR = 0.001
ADAM_B1 = 0.9
ADAM_B2 = 0.999
ADAM_EPS = 1e-08
ADAM_WD = 0.01
ADAM_STEP = 10
MESH = pl.DeviceIdType.MESH
VMEM_LIMIT = 56 * 1024 * 1024
HALO_ROWS = 16

WEIGHTS = ['attn_norm', 'w_in', 'sgu_norm', 'w_spatial', 'b_spatial', 'conv_w', 'conv_b', 'kv_norm', 'w_ukv',
           'q_nope_norm', 'q_rope_norm', 'k_nope_norm', 'k_rope_norm', 'out_norm', 'w_out', 'ple_norm',
           'w_ple_gate', 'w_ple_proj']
BIG = ['w_in', 'w_ukv', 'w_out', 'w_ple_gate', 'w_ple_proj']
BIG_AXIS = {'w_in': 1, 'w_ukv': 2, 'w_out': 1, 'w_ple_gate': 1, 'w_ple_proj': 2}
SMALL = [n for n in WEIGHTS if n not in BIG]


def _pick(n, cands):
    for c in cands:
        if n % c == 0:
            return c
    return n


def _params(sem=None):
    return pltpu.CompilerParams(dimension_semantics=sem, vmem_limit_bytes=VMEM_LIMIT)


def _matmul(a, b, mode, out_dtype, name, add=None, b_layer=None, rider=None):
    b_shape = b.shape if b_layer is None else b.shape[1:]
    if mode == 'nn':
        (m, k), n = a.shape, b_shape[1]
    elif mode == 'nt':
        (m, k), n = a.shape, b_shape[0]
    else:
        (k, m), n = a.shape, b_shape[1]
    tm = _pick(m, (1280, 1024, 512, 256, 128))
    tn = _pick(n, (1536, 1024, 512, 256, 128))
    tk = k if k <= 2048 else _pick(k, (2048, 1536, 1024, 512, 256, 128))
    nk = k // tk
    if mode == 'tn':
        a_spec = pl.BlockSpec((tk, tm), lambda i, j, kk: (kk, i))
        dims = (((0,), (0,)), ((), ()))
    else:
        a_spec = pl.BlockSpec((tm, tk), lambda i, j, kk: (i, kk))
        dims = (((1,), (0,)), ((), ())) if mode == 'nn' else (((1,), (1,)), ((), ()))
    b_block = (tn, tk) if mode == 'nt' else (tk, tn)
    if b_layer is None:
        b_spec = pl.BlockSpec(b_block, (lambda i, j, kk: (j, kk)) if mode == 'nt' else (lambda i, j, kk: (kk, j)))
    else:
        b_spec = pl.BlockSpec((None,) + b_block, (lambda i, j, kk: (b_layer, j, kk)) if mode == 'nt'
                              else (lambda i, j, kk: (b_layer, kk, j)))
    o_spec = pl.BlockSpec((tm, tn), lambda i, j, kk: (i, j))
    has_add = add is not None

    def body(*refs):
        a_ref, b_ref = refs[0], refs[1]
        add_ref = refs[2] if has_add else None
        o_ref = refs[3] if has_add else refs[2]

        def product():
            return lax.dot_general(a_ref[...], b_ref[...], dims, preferred_element_type=F32)

        def finish(res):
            if has_add:
                res = res + add_ref[...]
            o_ref[...] = res.astype(out_dtype)

        if nk == 1:
            finish(product())
        else:
            acc_ref = refs[-1]
            kk = pl.program_id(2)

            @pl.when(kk == 0)
            def _():
                acc_ref[...] = product()

            @pl.when((kk > 0) & (kk < nk - 1))
            def _():
                acc_ref[...] += product()

            @pl.when(kk == nk - 1)
            def _():
                finish(acc_ref[...] + product())

    in_specs = [a_spec, b_spec] + ([o_spec] if has_add else [])
    args = [a, b] + ([add] if has_add else [])
    grid = (m // tm, n // tn, nk)
    scratch = [pltpu.VMEM((tm, tn), F32)] if nk > 1 else []
    if rider is None:
        return pl.pallas_call(
            body, name=name, grid=grid, in_specs=in_specs, out_specs=o_spec,
            out_shape=jax.ShapeDtypeStruct((m, n), out_dtype), scratch_shapes=scratch,
            compiler_params=_params(("parallel", "parallel", "arbitrary")),
        )(*args)

    n_in, n_rin, n_rout = len(args), len(rider.arrays), len(rider.out_shapes)

    def body_with_rider(*refs):
        r_in = refs[n_in:n_in + n_rin]
        r_out = refs[n_in + n_rin + 1:n_in + n_rin + 1 + n_rout]
        own = refs[:n_in] + refs[n_in + n_rin:n_in + n_rin + 1] + refs[n_in + n_rin + 1 + n_rout:len(refs) - 2]
        send_sems, recv_sems = refs[-2:]
        ids = [pl.program_id(ax) for ax in range(3)]

        @pl.when((ids[0] == 0) & (ids[1] == 0) & (ids[2] == 0))
        def _():
            rider.start(r_in, r_out, send_sems, recv_sems)

        body(*own)

        @pl.when((ids[0] == grid[0] - 1) & (ids[1] == grid[1] - 1) & (ids[2] == grid[2] - 1))
        def _():
            rider.finish(r_in, r_out, send_sems, recv_sems)

    res = pl.pallas_call(
        body_with_rider, name=name, grid=grid, in_specs=in_specs + [_ANY] * n_rin,
        out_specs=[o_spec] + [_ANY] * n_rout,
        out_shape=[jax.ShapeDtypeStruct((m, n), out_dtype)] + rider.out_shapes,
        scratch_shapes=scratch + rider.sems(), input_output_aliases=rider.aliases(n_in, 1),
        compiler_params=_params(("arbitrary", "arbitrary", "arbitrary")),
    )(*args, *rider.arrays)
    return res[0], res[1:]


def _rms(x, n):
    r = lax.rsqrt(jnp.sum(x * x, axis=-1, keepdims=True) * (1.0 / n) + EPS)
    return x * r, r


def _rms_bwd(dxhat, xhat, r, n):
    return r * (dxhat - xhat * (jnp.sum(dxhat * xhat, axis=-1, keepdims=True) * (1.0 / n)))


def _sigmoid(z):
    return 1.0 / (1.0 + jnp.exp(-z))


def _silu_and_grad(z):
    sig = _sigmoid(z)
    return z * sig, sig * (1.0 + z * (1.0 - sig))


def _colsum(x):
    return jnp.sum(x, axis=0, keepdims=True)


def _rope(t, cos_t, sin_a, sin_b):
    return t * cos_t + pltpu.roll(t, 96, 1) * sin_a + pltpu.roll(t, 32, 1) * sin_b


def _rope_bwd(d, cos_t, sin_a, sin_b):
    return d * cos_t + pltpu.roll(d * sin_a, 32, 1) + pltpu.roll(d * sin_b, 96, 1)


def _shift_down(g, first_row):
    row = lax.broadcasted_iota(jnp.int32, g.shape, 0)
    return jnp.where(row == 0, first_row, pltpu.roll(g, 1, 0))


def _shift_up(g, last_row):
    n = g.shape[0]
    row = lax.broadcasted_iota(jnp.int32, g.shape, 0)
    return jnp.where(row == n - 1, last_row, pltpu.roll(g, n - 1, 0))


def _row_spec(r, w, col=0):
    return pl.BlockSpec((r, w), lambda i: (i, col))


def _const_spec(shape):
    nd = len(shape)
    return pl.BlockSpec(shape, lambda i: (0,) * nd)


def _col_block(off, w):
    assert off % w == 0, (off, w)
    return off // w


def _zero_at_first_step(refs):
    @pl.when(pl.program_id(0) == 0)
    def _():
        for ref in refs:
            ref[...] = jnp.zeros(ref.shape, ref.dtype)


def _row_call(body, name, t, r, in_specs, args, out_specs, out_shapes, scratch=(), into=None):
    if into is None:
        return pl.pallas_call(
            body, name=name, grid=(t // r,), in_specs=in_specs, out_specs=out_specs, out_shape=out_shapes,
            scratch_shapes=list(scratch), compiler_params=_params(("arbitrary",)),
        )(*args)
    buf, col = into
    single = not isinstance(out_specs, (list, tuple))
    specs = [out_specs] if single else list(out_specs)
    shapes = [out_shapes] if single else list(out_shapes)
    width = shapes[0].shape[1]
    assert shapes[0].dtype == buf.dtype and buf.shape[0] == t
    specs[0] = _row_spec(r, width, col)
    shapes[0] = jax.ShapeDtypeStruct(buf.shape, buf.dtype)
    n_in = len(args)

    def body_in_place(*refs):
        body(*refs[:n_in], *refs[n_in + 1:])

    res = pl.pallas_call(
        body_in_place, name=name, grid=(t // r,), in_specs=list(in_specs) + [_ANY], out_specs=specs, out_shape=shapes,
        scratch_shapes=list(scratch), input_output_aliases={n_in: 0}, compiler_params=_params(("arbitrary",)),
    )(*args, buf)
    return res[0] if single else res


def _norm_fwd(h, g, name):
    t, d = h.shape
    r = _pick(t, (256, 128))

    def body(h_ref, g_ref, o_ref):
        xhat, _ = _rms(h_ref[...], d)
        o_ref[...] = (xhat * g_ref[...]).astype(BF16)

    return _row_call(body, name, t, r, [_row_spec(r, d), _const_spec((1, d))], (h, g),
                     _row_spec(r, d), jax.ShapeDtypeStruct((t, d), BF16))


def _norm_bwd(h, g, d_hn, d_res, name):
    t, d = h.shape
    r = _pick(t, (256, 128))

    def body(h_ref, g_ref, dy_ref, dres_ref, dh_ref, dhb_ref, dg_ref):
        _zero_at_first_step([dg_ref])
        xhat, rr = _rms(h_ref[...], d)
        dy = dy_ref[...].astype(F32)
        dg_ref[...] += _colsum(dy * xhat)
        dh = dres_ref[...] + _rms_bwd(dy * g_ref[...], xhat, rr, d)
        dh_ref[...] = dh
        dhb_ref[...] = dh.astype(BF16)

    return _row_call(body, name, t, r,
                     [_row_spec(r, d), _const_spec((1, d)), _row_spec(r, d), _row_spec(r, d)], (h, g, d_hn, d_res),
                     [_row_spec(r, d), _row_spec(r, d), _const_spec((1, d))],
                     [jax.ShapeDtypeStruct((t, d), F32), jax.ShapeDtypeStruct((t, d), BF16),
                      jax.ShapeDtypeStruct((1, d), F32)])


def _sgu_scores(v, gs_ref, ws_ref, bb_ref, s_scr, r, ah, keep=None):
    for kk in range(r // HEAD):
        for hh in range(ah):
            rows, cols = slice(kk * HEAD, (kk + 1) * HEAD), slice(hh * HEAD, (hh + 1) * HEAD)
            vhat, rv = _rms(v[rows, cols], HEAD)
            vn = vhat * gs_ref[pl.ds(hh, 1), :]
            s_scr[rows, cols] = jnp.dot(ws_ref[hh], vn.astype(BF16), preferred_element_type=F32) + bb_ref[hh]
            if keep is not None:
                keep[(kk, hh)] = (vhat, rv, vn)


def _sgu_fwd(proj, off, aw, gs, ws, bb, ga, name, into=None):
    t = proj.shape[0]
    ah = aw // HEAD
    r = _pick(t, (256, 128))
    cb = _col_block(off, aw)

    def body(u_ref, v_ref, z_ref, gs_ref, ws_ref, bb_ref, ga_ref, o_ref, s_scr):
        _sgu_scores(v_ref[...].astype(F32), gs_ref, ws_ref, bb_ref, s_scr, r, ah)
        sil, _ = _silu_and_grad(z_ref[...].astype(F32))
        yhat, _ = _rms(u_ref[...].astype(F32) * s_scr[...] * sil, aw)
        o_ref[...] = (yhat * ga_ref[...]).astype(BF16)

    return _row_call(
        body, name, t, r,
        [_row_spec(r, aw, cb), _row_spec(r, aw, cb + 1), _row_spec(r, aw, cb + 2), _const_spec((ah, HEAD)),
         _const_spec((ah, HEAD, HEAD)), _const_spec((ah, HEAD, HEAD)), _const_spec((1, aw))],
        (proj, proj, proj, gs, ws, bb, ga),
        _row_spec(r, aw), jax.ShapeDtypeStruct((t, aw), BF16), scratch=[pltpu.VMEM((r, aw), F32)], into=into)


def _sgu_bwd(proj, off, aw, gs, ws, ws_t, bb, ga, dy, out_width, name):
    t = proj.shape[0]
    ah = aw // HEAD
    r = _pick(t, (256, 128))
    assert off == 0
    cb = _col_block(off, aw)

    def body(u_ref, v_ref, z_ref, gs_ref, ws_ref, wst_ref, bb_ref, ga_ref, dy_ref,
             d_ref, dgs_ref, dws_ref, db_ref, dga_ref, s_scr, dv_scr):
        _zero_at_first_step([dgs_ref, dws_ref, db_ref, dga_ref])
        keep = {}
        _sgu_scores(v_ref[...].astype(F32), gs_ref, ws_ref, bb_ref, s_scr, r, ah, keep)
        u, z, s = u_ref[...].astype(F32), z_ref[...].astype(F32), s_scr[...]
        sil, dsil = _silu_and_grad(z)
        yhat, rr = _rms(u * s * sil, aw)
        dy_f = dy_ref[...].astype(F32)
        dga_ref[...] += _colsum(dy_f * yhat)
        dya = _rms_bwd(dy_f * ga_ref[...], yhat, rr, aw)
        d_ref[:, 0:aw] = (dya * s * sil).astype(BF16)
        d_ref[:, 2 * aw:3 * aw] = (dya * u * s * dsil).astype(BF16)
        ds = dya * u * sil
        for kk in range(r // HEAD):
            for hh in range(ah):
                rows, cols = slice(kk * HEAD, (kk + 1) * HEAD), slice(hh * HEAD, (hh + 1) * HEAD)
                vhat, rv, vn = keep[(kk, hh)]
                ds_blk = ds[rows, cols]
                db_ref[hh] += jnp.sum(ds_blk, axis=1, keepdims=True)
                ds_b = ds_blk.astype(BF16)
                dws_ref[hh] += lax.dot_general(ds_b, vn.astype(BF16), (((1,), (1,)), ((), ())),
                                               preferred_element_type=F32)
                dvn = jnp.dot(wst_ref[hh], ds_b, preferred_element_type=F32)
                dgs_ref[pl.ds(hh, 1), :] += _colsum(dvn * vhat)
                dv_scr[rows, cols] = _rms_bwd(dvn * gs_ref[pl.ds(hh, 1), :], vhat, rv, HEAD)
        d_ref[:, aw:2 * aw] = dv_scr[...].astype(BF16)

    return _row_call(
        body, name, t, r,
        [_row_spec(r, aw, cb), _row_spec(r, aw, cb + 1), _row_spec(r, aw, cb + 2), _const_spec((ah, HEAD)),
         _const_spec((ah, HEAD, HEAD)), _const_spec((ah, HEAD, HEAD)), _const_spec((ah, HEAD, HEAD)),
         _const_spec((1, aw)), _row_spec(r, aw, 0)],
        (proj, proj, proj, gs, ws, ws_t, bb, ga, dy),
        [_row_spec(r, 3 * aw), _const_spec((ah, HEAD)), _const_spec((ah, HEAD, HEAD)), _const_spec((ah, HEAD, 1)),
         _const_spec((1, aw))],
        [jax.ShapeDtypeStruct((t, out_width), BF16), jax.ShapeDtypeStruct((ah, HEAD), F32),
         jax.ShapeDtypeStruct((ah, HEAD, HEAD), F32), jax.ShapeDtypeStruct((ah, HEAD, 1), F32),
         jax.ShapeDtypeStruct((1, aw), F32)],
        scratch=[pltpu.VMEM((r, aw), F32), pltpu.VMEM((r, aw), F32)])


def _halo_specs(t, r, w, col, rows):
    per = r // rows
    last = t // rows - 1
    prev = pl.BlockSpec((rows, w), lambda i: (jnp.maximum(i * per - 1, 0), col))
    nxt = pl.BlockSpec((rows, w), lambda i: (jnp.minimum((i + 1) * per, last), col))
    return prev, nxt


def _edge_rows(prev_ref, next_ref, n_steps):
    i = pl.program_id(0)
    rows = prev_ref.shape[0]
    before = prev_ref[...].astype(F32)[rows - 1:rows, :] * (i > 0).astype(F32)
    after = next_ref[...].astype(F32)[0:1, :] * (i < n_steps - 1).astype(F32)
    return before, after


def _conv_fwd(proj, off, bw, cw, cb_, gb, name, into=None):
    t = proj.shape[0]
    r = _pick(t, (256, 128))
    n_steps = t // r
    c0 = _col_block(off, bw)
    cp, cn = _halo_specs(t, r, bw, c0 + 1, HALO_ROWS)
    hp, hn = _halo_specs(t, r, bw, c0 + 2, HALO_ROWS)

    def body(b_ref, c_ref, h_ref, z_ref, cp_ref, cn_ref, hp_ref, hn_ref, cw_ref, cb_ref, gb_ref, o_ref, yc_ref):
        g = c_ref[...].astype(F32) * h_ref[...].astype(F32)
        c_before, c_after = _edge_rows(cp_ref, cn_ref, n_steps)
        h_before, h_after = _edge_rows(hp_ref, hn_ref, n_steps)
        yconv = (cb_ref[...] + cw_ref[0:1, :] * _shift_down(g, c_before * h_before) + cw_ref[1:2, :] * g
                 + cw_ref[2:3, :] * _shift_up(g, c_after * h_after))
        yc_ref[...] = yconv
        sil, _ = _silu_and_grad(z_ref[...].astype(F32))
        yhat, _ = _rms(b_ref[...].astype(F32) * yconv * sil, bw)
        o_ref[...] = (yhat * gb_ref[...]).astype(BF16)

    return _row_call(
        body, name, t, r,
        [_row_spec(r, bw, c0), _row_spec(r, bw, c0 + 1), _row_spec(r, bw, c0 + 2), _row_spec(r, bw, c0 + 3),
         cp, cn, hp, hn, _const_spec((CONV_TAPS, bw)), _const_spec((1, bw)), _const_spec((1, bw))],
        (proj, proj, proj, proj, proj, proj, proj, proj, cw, cb_, gb),
        [_row_spec(r, bw), _row_spec(r, bw)],
        [jax.ShapeDtypeStruct((t, bw), BF16), jax.ShapeDtypeStruct((t, bw), F32)], into=into)


def _conv_bwd_gate(proj, off, bw, yconv, gb, dy, name):
    t = proj.shape[0]
    r = _pick(t, (256, 128))
    c0 = _col_block(off, bw)

    def body(b_ref, z_ref, yc_ref, gb_ref, dy_ref, dyc_ref, db_ref, dz_ref, dgb_ref, dcb_ref):
        _zero_at_first_step([dgb_ref, dcb_ref])
        b, z, yconv_v = b_ref[...].astype(F32), z_ref[...].astype(F32), yc_ref[...]
        sil, dsil = _silu_and_grad(z)
        yhat, rr = _rms(b * yconv_v * sil, bw)
        dy_f = dy_ref[...].astype(F32)
        dgb_ref[...] += _colsum(dy_f * yhat)
        dyb = _rms_bwd(dy_f * gb_ref[...], yhat, rr, bw)
        dyc = dyb * b * sil
        dyc_ref[...] = dyc
        dcb_ref[...] += _colsum(dyc)
        db_ref[...] = (dyb * yconv_v * sil).astype(BF16)
        dz_ref[...] = (dyb * b * yconv_v * dsil).astype(BF16)

    return _row_call(
        body, name, t, r,
        [_row_spec(r, bw, c0), _row_spec(r, bw, c0 + 3), _row_spec(r, bw), _const_spec((1, bw)), _row_spec(r, bw, 1)],
        (proj, proj, yconv, gb, dy),
        [_row_spec(r, bw), _row_spec(r, bw), _row_spec(r, bw), _const_spec((1, bw)), _const_spec((1, bw))],
        [jax.ShapeDtypeStruct((t, bw), F32), jax.ShapeDtypeStruct((t, bw), BF16), jax.ShapeDtypeStruct((t, bw), BF16),
         jax.ShapeDtypeStruct((1, bw), F32), jax.ShapeDtypeStruct((1, bw), F32)])


def _conv_bwd_taps(proj, off, bw, dyc, cw, d_gate_b, d_gate_z, name, into=None):
    t = proj.shape[0]
    r = _pick(t, (256, 128))
    n_steps = t // r
    c0 = _col_block(off, bw)
    cp, cn = _halo_specs(t, r, bw, c0 + 1, HALO_ROWS)
    hp, hn = _halo_specs(t, r, bw, c0 + 2, HALO_ROWS)
    dp, dn = _halo_specs(t, r, bw, 0, 8)

    def body(c_ref, h_ref, cp_ref, cn_ref, hp_ref, hn_ref, d_ref, dp_ref, dn_ref, cw_ref, dgb_ref, dgz_ref,
             db_ref, dcw_ref):
        _zero_at_first_step([dcw_ref])
        c, h, d = c_ref[...].astype(F32), h_ref[...].astype(F32), d_ref[...]
        g = c * h
        c_before, c_after = _edge_rows(cp_ref, cn_ref, n_steps)
        h_before, h_after = _edge_rows(hp_ref, hn_ref, n_steps)
        d_before, d_after = _edge_rows(dp_ref, dn_ref, n_steps)
        dg = (cw_ref[0:1, :] * _shift_up(d, d_after) + cw_ref[1:2, :] * d + cw_ref[2:3, :] * _shift_down(d, d_before))
        db_ref[:, 0:bw] = dgb_ref[...]
        db_ref[:, bw:2 * bw] = (dg * h).astype(BF16)
        db_ref[:, 2 * bw:3 * bw] = (dg * c).astype(BF16)
        db_ref[:, 3 * bw:4 * bw] = dgz_ref[...]
        dcw_ref[0:1, :] += _colsum(d * _shift_down(g, c_before * h_before))
        dcw_ref[1:2, :] += _colsum(d * g)
        dcw_ref[2:3, :] += _colsum(d * _shift_up(g, c_after * h_after))

    return _row_call(
        body, name, t, r,
        [_row_spec(r, bw, c0 + 1), _row_spec(r, bw, c0 + 2), cp, cn, hp, hn, _row_spec(r, bw), dp, dn,
         _const_spec((CONV_TAPS, bw)), _row_spec(r, bw), _row_spec(r, bw)],
        (proj, proj, proj, proj, proj, proj, dyc, dyc, dyc, cw, d_gate_b, d_gate_z),
        [_row_spec(r, 4 * bw), _const_spec((CONV_TAPS, bw))],
        [jax.ShapeDtypeStruct((t, 4 * bw), BF16), jax.ShapeDtypeStruct((CONV_TAPS, bw), F32)], into=into)


def _mla_prep_fwd(proj, q_off, ckv_off, kr_off, ch, kvr, tabs, qn_g, qr_g, kr_g, kv_g, name):
    t = proj.shape[0]
    r = _pick(t, (256, 128))
    qw = ch * 2 * HEAD
    cos_t, sin_a, sin_b = tabs

    def body(q_ref, ckv_ref, kr_ref, cos_ref, sa_ref, sb_ref, qn_ref, qr_ref, krg_ref, kvg_ref,
             qo_ref, co_ref, ko_ref):
        cos_v, sa, sb = cos_ref[...], sa_ref[...], sb_ref[...]
        for hh in range(ch):
            lo = hh * 2 * HEAD
            nhat, _ = _rms(q_ref[:, lo:lo + HEAD].astype(F32), HEAD)
            qo_ref[:, lo:lo + HEAD] = (nhat * qn_ref[...]).astype(BF16)
            rhat, _ = _rms(q_ref[:, lo + HEAD:lo + 2 * HEAD].astype(F32), ROPE)
            qo_ref[:, lo + HEAD:lo + 2 * HEAD] = _rope(rhat * qr_ref[...], cos_v, sa, sb).astype(BF16)
        khat, _ = _rms(kr_ref[...].astype(F32), ROPE)
        ko_ref[...] = _rope(khat * krg_ref[...], cos_v, sa, sb).astype(BF16)
        chat, _ = _rms(ckv_ref[...].astype(F32), kvr)
        co_ref[...] = (chat * kvg_ref[...]).astype(BF16)

    tab = _row_spec(r, HEAD)
    gain = _const_spec((1, HEAD))
    return _row_call(
        body, name, t, r,
        [_row_spec(r, qw, _col_block(q_off, qw)), _row_spec(r, kvr, _col_block(ckv_off, kvr)),
         _row_spec(r, HEAD, _col_block(kr_off, HEAD)), tab, tab, tab, gain, gain, gain, _const_spec((1, kvr))],
        (proj, proj, proj, cos_t, sin_a, sin_b, qn_g, qr_g, kr_g, kv_g),
        [_row_spec(r, qw), _row_spec(r, kvr), _row_spec(r, HEAD)],
        [jax.ShapeDtypeStruct((t, qw), BF16), jax.ShapeDtypeStruct((t, kvr), BF16),
         jax.ShapeDtypeStruct((t, HEAD), BF16)])


def _mla_prep_bwd(proj, q_off, ckv_off, kr_off, ch, kvr, tabs, qn_g, qr_g, kr_g, kv_g, dq_cat_t, dq_scale, dckv_n,
                  dkr_rot, kr_width, name, into=None):
    t = proj.shape[0]
    r = _pick(t, (256, 128))
    qw = ch * 2 * HEAD
    cos_t, sin_a, sin_b = tabs

    def body(q_ref, ckv_ref, kr_ref, cos_ref, sa_ref, sb_ref, qn_ref, qr_ref, krg_ref, kvg_ref,
             dq_ref, dc_ref, dk_ref, dqo_ref, dco_ref, dko_ref, dqn_ref, dqr_ref, dkrg_ref, dkvg_ref):
        _zero_at_first_step([dqn_ref, dqr_ref, dkrg_ref, dkvg_ref])
        cos_v, sa, sb = cos_ref[...], sa_ref[...], sb_ref[...]
        dq = dq_ref[...].T * dq_scale
        for hh in range(ch):
            lo = hh * 2 * HEAD
            nhat, nr = _rms(q_ref[:, lo:lo + HEAD].astype(F32), HEAD)
            d_n = dq[:, lo:lo + HEAD]
            dqn_ref[...] += _colsum(d_n * nhat)
            dqo_ref[:, lo:lo + HEAD] = _rms_bwd(d_n * qn_ref[...], nhat, nr, HEAD).astype(BF16)
            rhat, rr = _rms(q_ref[:, lo + HEAD:lo + 2 * HEAD].astype(F32), ROPE)
            d_t = _rope_bwd(dq[:, lo + HEAD:lo + 2 * HEAD], cos_v, sa, sb)
            dqr_ref[...] += _colsum(d_t * rhat)
            dqo_ref[:, lo + HEAD:lo + 2 * HEAD] = _rms_bwd(d_t * qr_ref[...], rhat, rr, ROPE).astype(BF16)
        khat, kr_r = _rms(kr_ref[...].astype(F32), ROPE)
        d_k = _rope_bwd(dk_ref[...], cos_v, sa, sb)
        dkrg_ref[...] += _colsum(d_k * khat)
        dko_ref[:, 0:HEAD] = _rms_bwd(d_k * krg_ref[...], khat, kr_r, ROPE).astype(BF16)
        if kr_width > HEAD:
            dko_ref[:, HEAD:kr_width] = jnp.zeros((r, kr_width - HEAD), BF16)
        chat, cr = _rms(ckv_ref[...].astype(F32), kvr)
        d_c = dc_ref[...].astype(F32)
        dkvg_ref[...] += _colsum(d_c * chat)
        dco_ref[...] = _rms_bwd(d_c * kvg_ref[...], chat, cr, kvr).astype(BF16)

    tab = _row_spec(r, HEAD)
    gain = _const_spec((1, HEAD))
    return _row_call(
        body, name, t, r,
        [_row_spec(r, qw, _col_block(q_off, qw)), _row_spec(r, kvr, _col_block(ckv_off, kvr)),
         _row_spec(r, HEAD, _col_block(kr_off, HEAD)), tab, tab, tab, gain, gain, gain, _const_spec((1, kvr)),
         pl.BlockSpec((qw, r), lambda i: (0, i)), _row_spec(r, kvr), _row_spec(r, HEAD)],
        (proj, proj, proj, cos_t, sin_a, sin_b, qn_g, qr_g, kr_g, kv_g, dq_cat_t, dckv_n, dkr_rot),
        [_row_spec(r, qw), _row_spec(r, kvr), _row_spec(r, kr_width), gain, gain, gain, _const_spec((1, kvr))],
        [jax.ShapeDtypeStruct((t, qw), BF16), jax.ShapeDtypeStruct((t, kvr), BF16),
         jax.ShapeDtypeStruct((t, kr_width), BF16), jax.ShapeDtypeStruct((1, HEAD), F32),
         jax.ShapeDtypeStruct((1, HEAD), F32), jax.ShapeDtypeStruct((1, HEAD), F32),
         jax.ShapeDtypeStruct((1, kvr), F32)], into=into)


def _kv_prep_fwd(kv, kr_rot, ch, kn_g, name):
    t = kv.shape[0]
    r = _pick(t, (256, 128))
    qw = ch * 2 * HEAD

    def body(kv_ref, kr_ref, kn_ref, ko_ref, kt_ref, vo_ref):
        ones = jnp.ones((r, HEAD), BF16)
        for hh in range(ch):
            lo = hh * 2 * HEAD
            nhat, _ = _rms(kv_ref[:, lo:lo + HEAD], HEAD)
            ko_ref[:, lo:lo + HEAD] = (nhat * kn_ref[...]).astype(BF16)
            ko_ref[:, lo + HEAD:lo + 2 * HEAD] = kr_ref[...]
            vo_ref[:, lo:lo + HEAD] = kv_ref[:, lo + HEAD:lo + 2 * HEAD].astype(BF16)
            vo_ref[:, lo + HEAD:lo + 2 * HEAD] = ones
        kt_ref[...] = ko_ref[...].astype(F32).T.astype(BF16)

    return _row_call(
        body, name, t, r, [_row_spec(r, qw), _row_spec(r, HEAD), _const_spec((1, HEAD))], (kv, kr_rot, kn_g),
        [_row_spec(r, qw), pl.BlockSpec((qw, r), lambda i: (0, i)), _row_spec(r, qw)],
        [jax.ShapeDtypeStruct((t, qw), BF16), jax.ShapeDtypeStruct((qw, t), BF16),
         jax.ShapeDtypeStruct((t, qw), BF16)])


def _kv_prep_bwd(kv, ch, kn_g, dk_cat, dv, name):
    t = kv.shape[0]
    r = _pick(t, (256, 128))
    qw = ch * 2 * HEAD

    def body(kv_ref, kn_ref, dk_ref, dv_ref, dkv_ref, dkr_ref, dkn_ref):
        _zero_at_first_step([dkn_ref])
        dkr = jnp.zeros((r, HEAD), F32)
        for hh in range(ch):
            lo = hh * 2 * HEAD
            nhat, nr = _rms(kv_ref[:, lo:lo + HEAD], HEAD)
            d_n = dk_ref[:, lo:lo + HEAD].astype(F32)
            dkn_ref[...] += _colsum(d_n * nhat)
            dkv_ref[:, lo:lo + HEAD] = _rms_bwd(d_n * kn_ref[...], nhat, nr, HEAD).astype(BF16)
            dkv_ref[:, lo + HEAD:lo + 2 * HEAD] = dv_ref[:, hh * HEAD:(hh + 1) * HEAD]
            dkr = dkr + dk_ref[:, lo + HEAD:lo + 2 * HEAD].astype(F32)
        dkr_ref[...] = dkr

    return _row_call(
        body, name, t, r,
        [_row_spec(r, qw), _const_spec((1, HEAD)), _row_spec(r, qw), _row_spec(r, ch * HEAD)], (kv, kn_g, dk_cat, dv),
        [_row_spec(r, qw), _row_spec(r, HEAD), _const_spec((1, HEAD))],
        [jax.ShapeDtypeStruct((t, qw), BF16), jax.ShapeDtypeStruct((t, HEAD), F32),
         jax.ShapeDtypeStruct((1, HEAD), F32)])


def _attn_tiles(t):
    return _pick(t, (2048, 1024, 512, 256, 128)), _pick(t, (1024, 512, 256, 128))


_NT = (((1,), (1,)), ((), ()))
LOG2E = 1.4426950408889634


def _attn_fwd(q_cat, k_cat, v_aug, ch, scale, name):
    t = q_cat.shape[0]
    tq, tk = _attn_tiles(t)
    nk = t // tk
    c2 = scale * LOG2E

    def body(q_ref, k_ref, v_ref, o_ref, lse_ref, s_scr, m_scr, acc_scr):
        j = pl.program_id(2)

        def scores(slot):
            s_scr[slot] = lax.dot_general(q_ref[...], k_ref[...], _NT, preferred_element_type=F32) * c2

        def absorb(slot):
            s = s_scr[slot]
            m_old = m_scr[...]
            m_new = jnp.maximum(m_old, jnp.max(s, axis=-1, keepdims=True))
            p = jnp.exp2(s - m_new).astype(BF16)
            acc_scr[...] = (jnp.exp2(m_old - m_new) * acc_scr[...]
                            + jnp.dot(p, v_ref[...], preferred_element_type=F32))
            m_scr[...] = m_new

        @pl.when(j == 0)
        def _():
            m_scr[...] = jnp.full(m_scr.shape, -jnp.inf, F32)
            acc_scr[...] = jnp.zeros(acc_scr.shape, F32)
            scores(0)

        for parity in (0, 1):
            @pl.when((j > 0) & (j < nk) & (j % 2 == parity))
            def _():
                scores(parity)
                absorb(1 - parity)

        @pl.when(j == nk)
        def _():
            absorb((nk - 1) % 2)
            acc = acc_scr[...]
            l_sum = acc[:, HEAD:]
            o_ref[...] = (acc[:, :HEAD] / l_sum).astype(BF16)
            lse_ref[0] = m_scr[...] + jnp.log(l_sum[:, 0:1]) * LOG2E

    return pl.pallas_call(
        body, name=name, grid=(ch, t // tq, nk + 1),
        in_specs=[pl.BlockSpec((tq, 2 * HEAD), lambda h, i, j: (i, h)),
                  pl.BlockSpec((tk, 2 * HEAD), lambda h, i, j: (jnp.minimum(j, nk - 1), h)),
                  pl.BlockSpec((tk, 2 * HEAD), lambda h, i, j: (jnp.maximum(j - 1, 0), h))],
        out_specs=[pl.BlockSpec((tq, HEAD), lambda h, i, j: (i, h)),
                   pl.BlockSpec((1, tq, 1), lambda h, i, j: (h, i, 0))],
        out_shape=[jax.ShapeDtypeStruct((t, ch * HEAD), BF16), jax.ShapeDtypeStruct((ch, t, 1), F32)],
        scratch_shapes=[pltpu.VMEM((2, tq, tk), F32), pltpu.VMEM((tq, 1), F32), pltpu.VMEM((tq, 2 * HEAD), F32)],
        compiler_params=_params(("parallel", "parallel", "arbitrary")),
    )(q_cat, k_cat, v_aug)


def _attn_bwd(q_cat, k_cat, k_cat_t, v_aug, do, lse_row, d_row, ch, scale, name):
    t = q_cat.shape[0]
    tk = _pick(t, (1024, 512, 256, 128))
    tq = _pick(t, (1024, 512, 256, 128))
    nk, nq = t // tk, t // tq
    c2 = scale * LOG2E

    def body(q_ref, do_ref, qp_ref, dop_ref, lse_ref, d_ref, k_ref, kt_ref, v_ref,
             dqt_ref, dk_ref, dv_ref, s_scr, dp_scr, dk_scr, dv_scr):
        ki, j = pl.program_id(1), pl.program_id(2)

        def products(slot):
            s_scr[slot] = lax.dot_general(k_ref[...], q_ref[...], _NT, preferred_element_type=F32) * c2
            dp_scr[slot] = lax.dot_general(v_ref[...], do_ref[...], _NT, preferred_element_type=F32)

        def absorb(slot):
            q, do_v = qp_ref[...], dop_ref[...]
            pt = jnp.exp2(s_scr[slot] - lse_ref[0])
            dv_scr[...] += jnp.dot(pt.astype(BF16), do_v, preferred_element_type=F32)
            dst = (pt * (dp_scr[slot] - d_ref[0])).astype(BF16)
            dk_scr[...] += jnp.dot(dst, q, preferred_element_type=F32)
            part = jnp.dot(kt_ref[...], dst, preferred_element_type=F32)
            cols = pl.ds(pl.multiple_of((j - 1) * tq, tq), tq)

            @pl.when(ki == 0)
            def _():
                dqt_ref[:, cols] = part

            @pl.when(ki > 0)
            def _():
                dqt_ref[:, cols] += part

        @pl.when(j == 0)
        def _():
            dk_scr[...] = jnp.zeros(dk_scr.shape, F32)
            dv_scr[...] = jnp.zeros(dv_scr.shape, F32)
            products(0)

        for parity in (0, 1):
            @pl.when((j > 0) & (j < nq) & (j % 2 == parity))
            def _():
                products(parity)
                absorb(1 - parity)

        @pl.when(j == nq)
        def _():
            absorb((nq - 1) % 2)
            dk_ref[...] = (dk_scr[...] * scale).astype(BF16)
            dv_ref[...] = dv_scr[...].astype(BF16)

    def cur(i):
        return jnp.minimum(i, nq - 1)

    def prev(i):
        return jnp.maximum(i - 1, 0)

    stat = pl.BlockSpec((1, 1, tq), lambda h, j, i: (h, 0, prev(i)))
    return pl.pallas_call(
        body, name=name, grid=(ch, nk, nq + 1),
        in_specs=[pl.BlockSpec((tq, 2 * HEAD), lambda h, j, i: (cur(i), h)),
                  pl.BlockSpec((tq, HEAD), lambda h, j, i: (cur(i), h)),
                  pl.BlockSpec((tq, 2 * HEAD), lambda h, j, i: (prev(i), h)),
                  pl.BlockSpec((tq, HEAD), lambda h, j, i: (prev(i), h)), stat, stat,
                  pl.BlockSpec((tk, 2 * HEAD), lambda h, j, i: (j, h)),
                  pl.BlockSpec((2 * HEAD, tk), lambda h, j, i: (h, j)),
                  pl.BlockSpec((tk, HEAD), lambda h, j, i: (j, 2 * h))],
        out_specs=[pl.BlockSpec((2 * HEAD, t), lambda h, j, i: (h, 0)),
                   pl.BlockSpec((tk, 2 * HEAD), lambda h, j, i: (j, h)),
                   pl.BlockSpec((tk, HEAD), lambda h, j, i: (j, h))],
        out_shape=[jax.ShapeDtypeStruct((ch * 2 * HEAD, t), F32), jax.ShapeDtypeStruct((t, ch * 2 * HEAD), BF16),
                   jax.ShapeDtypeStruct((t, ch * HEAD), BF16)],
        scratch_shapes=[pltpu.VMEM((2, tk, tq), F32), pltpu.VMEM((2, tk, tq), F32),
                        pltpu.VMEM((tk, 2 * HEAD), F32), pltpu.VMEM((tk, HEAD), F32)],
        compiler_params=_params(("parallel", "arbitrary", "arbitrary")),
    )(q_cat, do, q_cat, do, lse_row, d_row, k_cat, k_cat_t, v_aug)


def _attn_post_fwd(o, proj, z_off, cw, gc, name, into=None):
    t = o.shape[0]
    r = _pick(t, (256, 128))

    def body(o_ref, z_ref, gc_ref, y_ref):
        sil, _ = _silu_and_grad(z_ref[...].astype(F32))
        yhat, _ = _rms(o_ref[...].astype(F32) * sil, cw)
        y_ref[...] = (yhat * gc_ref[...]).astype(BF16)

    return _row_call(body, name, t, r,
                     [_row_spec(r, cw), _row_spec(r, cw, _col_block(z_off, cw)), _const_spec((1, cw))], (o, proj, gc),
                     _row_spec(r, cw), jax.ShapeDtypeStruct((t, cw), BF16), into=into)


def _attn_post_bwd(o, proj, z_off, cw, gc, dy, dy_col, name, into=None):
    t = o.shape[0]
    ch = cw // HEAD
    r = _pick(t, (256, 128))

    def body(o_ref, z_ref, gc_ref, dy_ref, dz_ref, do_ref, ds_ref, dgc_ref):
        _zero_at_first_step([dgc_ref])
        o_v, z = o_ref[...].astype(F32), z_ref[...].astype(F32)
        sil, dsil = _silu_and_grad(z)
        yhat, rr = _rms(o_v * sil, cw)
        dy_f = dy_ref[...].astype(F32)
        dgc_ref[...] += _colsum(dy_f * yhat)
        dyc = _rms_bwd(dy_f * gc_ref[...], yhat, rr, cw)
        do_b = (dyc * sil).astype(BF16)
        do_ref[...] = do_b
        dz_ref[...] = (dyc * o_v * dsil).astype(BF16)
        prod = do_b.astype(F32) * o_v
        for hh in range(ch):
            ds_ref[hh] = jnp.sum(prod[:, hh * HEAD:(hh + 1) * HEAD], axis=-1, keepdims=True)

    return _row_call(
        body, name, t, r,
        [_row_spec(r, cw), _row_spec(r, cw, _col_block(z_off, cw)), _const_spec((1, cw)), _row_spec(r, cw, dy_col)],
        (o, proj, gc, dy),
        [_row_spec(r, cw), _row_spec(r, cw), pl.BlockSpec((ch, r, 1), lambda i: (0, i, 0)), _const_spec((1, cw))],
        [jax.ShapeDtypeStruct((t, cw), BF16), jax.ShapeDtypeStruct((t, cw), BF16),
         jax.ShapeDtypeStruct((ch, t, 1), F32), jax.ShapeDtypeStruct((1, cw), F32)], into=into)


def _ple_fwd(h1, gpre, pp, name):
    t, d = h1.shape
    r = _pick(t, (256, 128))

    def body(h_ref, g_ref, p_ref, o_ref):
        o_ref[...] = h_ref[...] + _sigmoid(g_ref[...]) * p_ref[...]

    return _row_call(body, name, t, r, [_row_spec(r, d)] * 3, (h1, gpre, pp), _row_spec(r, d),
                     jax.ShapeDtypeStruct((t, d), F32))


def _ple_bwd(gpre, pp, dh, name):
    t, d = dh.shape
    r = _pick(t, (256, 128))

    def body(g_ref, p_ref, dh_ref, dg_ref, dp_ref):
        sig = _sigmoid(g_ref[...])
        dh_v = dh_ref[...]
        dg_ref[...] = (dh_v * p_ref[...] * sig * (1.0 - sig)).astype(BF16)
        dp_ref[...] = (dh_v * sig).astype(BF16)

    return _row_call(body, name, t, r, [_row_spec(r, d)] * 3, (gpre, pp, dh), [_row_spec(r, d)] * 2,
                     [jax.ShapeDtypeStruct((t, d), BF16)] * 2)


def _loss_and_grad(h, target, name):
    t, d = h.shape
    r = _pick(t, (256, 128))

    def body(h_ref, t_ref, l_ref, dh_ref):
        _zero_at_first_step([l_ref])
        err = h_ref[...] - t_ref[...]
        l_ref[...] += jnp.sum(jnp.sum(err * err, axis=-1, keepdims=True), axis=0, keepdims=True) * (0.5 / d)
        dh_ref[...] = err * (1.0 / d)

    return _row_call(body, name, t, r, [_row_spec(r, d)] * 2, (h, target), [_const_spec((1, 1)), _row_spec(r, d)],
                     [jax.ShapeDtypeStruct((1, 1), F32), jax.ShapeDtypeStruct((t, d), F32)])


def _ew_rows(rows, cols):
    cap = min(1024, max(8, (1 << 19) // max(cols, 1)))
    for cand in range(cap - cap % 8, 7, -8):
        if rows % cand == 0:
            return cand
    return rows


def _pair_sum_bf16(a, b, name):
    n, rows, cols = a.shape
    rb = _ew_rows(rows, cols)

    def body(a_ref, b_ref, o_ref):
        o_ref[...] = (a_ref[...] + b_ref[...]).astype(BF16)

    spec = pl.BlockSpec((1, rb, cols), lambda s, i: (s, i, 0))
    return pl.pallas_call(body, name=name, grid=(n, rows // rb), in_specs=[spec, spec], out_specs=spec,
                          out_shape=jax.ShapeDtypeStruct(a.shape, BF16),
                          compiler_params=_params(("parallel", "parallel")))(a, b)


def _shard_sum(a, b, recv, name):
    rows, cols = a.shape
    rb = _ew_rows(rows, cols)

    def body(a_ref, b_ref, r_ref, o_ref):
        o_ref[...] = ((a_ref[...] + b_ref[...]) + r_ref[0].astype(F32) + r_ref[1].astype(F32)
                      + r_ref[2].astype(F32))

    spec = pl.BlockSpec((rb, cols), lambda i: (i, 0))
    return pl.pallas_call(body, name=name, grid=(rows // rb,),
                          in_specs=[spec, spec, pl.BlockSpec((N_SHARD - 1, rb, cols), lambda i: (0, i, 0))],
                          out_specs=spec, out_shape=jax.ShapeDtypeStruct(a.shape, F32),
                          compiler_params=_params(("parallel",)))(a, b, recv)


def _sum_devices(g, name):
    n, rows, cols = g.shape
    rb = _ew_rows(rows, cols)

    def body(g_ref, o_ref):
        acc = g_ref[0]
        for k in range(1, n):
            acc = acc + g_ref[k]
        o_ref[...] = acc

    return pl.pallas_call(body, name=name, grid=(rows // rb,),
                          in_specs=[pl.BlockSpec((n, rb, cols), lambda i: (0, i, 0))],
                          out_specs=pl.BlockSpec((rb, cols), lambda i: (i, 0)),
                          out_shape=jax.ShapeDtypeStruct((rows, cols), F32),
                          compiler_params=_params(("parallel",)))(g)


def _adamw_update(w, g_v, m, v):
    m_new = ADAM_B1 * m + (1.0 - ADAM_B1) * g_v
    v_new = ADAM_B2 * v + (1.0 - ADAM_B2) * (g_v * g_v)
    m_hat = m_new / (1.0 - ADAM_B1 ** ADAM_STEP)
    v_hat = v_new / (1.0 - ADAM_B2 ** ADAM_STEP)
    return -ADAM_LR * (m_hat / (jnp.sqrt(v_hat) + ADAM_EPS) + ADAM_WD * w), m_new, v_new


def _adamw(w, g, m, v, name):
    rows, cols = w.shape
    rb = _ew_rows(rows, cols)

    def body(w_ref, g_ref, m_ref, v_ref, d_ref, mo_ref, vo_ref):
        d_ref[...], mo_ref[...], vo_ref[...] = _adamw_update(w_ref[...], g_ref[...], m_ref[...], v_ref[...])

    spec = pl.BlockSpec((rb, cols), lambda i: (i, 0))
    return pl.pallas_call(body, name=name, grid=(rows // rb,), in_specs=[spec] * 4, out_specs=[spec] * 3,
                          out_shape=[jax.ShapeDtypeStruct(w.shape, F32)] * 3,
                          compiler_params=_params(("parallel",)))(w, g, m, v)


def _adamw_two_halves(w, own, recv, core_flag, m, v, name):
    depth, rows, cols = w.shape
    assert rows % 2 == 0 and own.shape == (depth, rows // 2, cols)
    rb = _ew_rows(rows // 2, cols)
    nb = rows // 2 // rb

    def body(w_ref, own_ref, recv_ref, flag_ref, m_ref, v_ref, g_ref, d_ref, mo_ref, vo_ref):
        half = pl.program_id(1).astype(F32)
        g_v = jnp.where(flag_ref[...] == half, own_ref[...], recv_ref[...])
        g_ref[...] = g_v
        d_ref[...], mo_ref[...], vo_ref[...] = _adamw_update(w_ref[...], g_v, m_ref[...], v_ref[...])

    full = pl.BlockSpec((None, rb, cols), lambda l, k, i: (l, k * nb + i, 0))
    half_spec = pl.BlockSpec((None, rb, cols), lambda l, k, i: (l, i, 0))
    return pl.pallas_call(
        body, name=name, grid=(depth, 2, nb),
        in_specs=[full, half_spec, half_spec, pl.BlockSpec((1, 1), lambda l, k, i: (0, 0)), full, full],
        out_specs=[full] * 4, out_shape=[jax.ShapeDtypeStruct(w.shape, F32)] * 4,
        compiler_params=_params(("parallel", "parallel", "parallel")))(w, own, recv, core_flag, m, v)


def _place():
    return lax.axis_index("x"), lax.axis_index("y"), lax.axis_index("c")


def _other_chips(x, y):
    return [(1 - x, y), (x, 1 - y), (1 - x, 1 - y)]


_ANY = pl.BlockSpec(memory_space=pl.ANY)


class _Rider:
    def __init__(self, arrays, out_shapes, n_sems, start, finish, in_place=False):
        self.arrays, self.out_shapes, self.n_sems = list(arrays), list(out_shapes), n_sems
        self.start, self.finish, self.in_place = start, finish, in_place

    def sems(self):
        return [pltpu.SemaphoreType.DMA((self.n_sems,)), pltpu.SemaphoreType.DMA((self.n_sems,))]

    def aliases(self, first_in, first_out):
        return {first_in + a: first_out + a for a in range(len(self.arrays))} if self.in_place else {}


def _comm_call(rider, name):
    n_in, n_out = len(rider.arrays), len(rider.out_shapes)

    def body(*refs):
        ins, outs = refs[:n_in], refs[n_in:n_in + n_out]
        send_sems, recv_sems = refs[n_in + n_out:]
        rider.start(ins, outs, send_sems, recv_sems)
        rider.finish(ins, outs, send_sems, recv_sems)

    return pl.pallas_call(
        body, name=name, in_specs=[_ANY] * n_in, out_specs=[_ANY] * n_out, out_shape=rider.out_shapes,
        scratch_shapes=rider.sems(), input_output_aliases=rider.aliases(0, 0))(*rider.arrays)


def _half(shape, which):
    for axis, size in enumerate(shape):
        if size % 2 == 0:
            return (slice(None),) * axis + (pl.ds(which * (size // 2), size // 2),)
    raise ValueError(f"no axis of even length in {shape}")


def _start_then_wait(copies):
    def start(*refs):
        for send, _ in copies(*refs):
            send.start()

    def finish(*refs):
        pairs = copies(*refs)
        for _, landing in pairs:
            landing.wait_recv()
        for send, _ in pairs:
            send.wait_send()

    return start, finish


def _fetch_rider(shards):
    n, n_peer = len(shards), N_SHARD - 1
    shapes = [s.shape for s in shards]

    def copies(ins, outs, send_sems, recv_sems):
        x, y, c = _place()
        pairs = []
        for a in range(n):
            mine = _half(shapes[a], c)
            for k, (px, py) in enumerate(_other_chips(x, y)):
                def into(slot):
                    return pltpu.make_async_remote_copy(
                        src_ref=ins[a].at[mine], dst_ref=outs[a].at[(slot,) + mine],
                        send_sem=send_sems.at[a * n_peer + k], recv_sem=recv_sems.at[a * n_peer + k],
                        device_id=(px, py, c), device_id_type=MESH)
                pairs.append((into(2 * x + y), into(2 * px + py)))
        return pairs

    start, finish = _start_then_wait(copies)
    return _Rider(shards, [jax.ShapeDtypeStruct((N_SHARD,) + s.shape, s.dtype) for s in shards], n * n_peer,
                  start, finish)


def _forward_rider(gathered, shards):
    n, n_peer = len(gathered), N_SHARD - 1
    shapes = [s.shape for s in shards]

    def copies(ins, outs, send_sems, recv_sems):
        x, y, c = _place()
        pairs = []
        for a in range(n):
            for k, (px, py) in enumerate(_other_chips(x, y)):
                def half_of_slot(which):
                    rows = outs[a].at[(2 * px + py,) + _half(shapes[a], which)]
                    return pltpu.make_async_remote_copy(
                        src_ref=rows, dst_ref=rows, send_sem=send_sems.at[a * n_peer + k],
                        recv_sem=recv_sems.at[a * n_peer + k], device_id=(x, y, 1 - c), device_id_type=MESH)
                pairs.append((half_of_slot(c), half_of_slot(1 - c)))
        return pairs

    start, finish = _start_then_wait(copies)
    return _Rider(gathered, [jax.ShapeDtypeStruct(g.shape, g.dtype) for g in gathered], n * n_peer, start, finish,
                  in_place=True)


def _sibling_rider(arrs, other_half):
    n = len(arrs)

    def copies(ins, outs, send_sems, recv_sems):
        x, y, c = _place()
        pairs = []
        for a in range(n):
            cp = pltpu.make_async_remote_copy(
                src_ref=ins[a].at[1 - c] if other_half else ins[a], dst_ref=outs[a], send_sem=send_sems.at[a],
                recv_sem=recv_sems.at[a], device_id=(x, y, 1 - c), device_id_type=MESH)
            pairs.append((cp, cp))
        return pairs

    start, finish = _start_then_wait(copies)
    return _Rider(arrs, [jax.ShapeDtypeStruct(g.shape[1:] if other_half else g.shape, g.dtype) for g in arrs], n,
                  start, finish)


def _owner_rider(parts):
    n, n_peer = len(parts), N_SHARD - 1

    def copies(ins, outs, send_sems, recv_sems):
        x, y, c = _place()
        pairs = []
        for a in range(n):
            for k, (px, py) in enumerate(_other_chips(x, y)):
                cp = pltpu.make_async_remote_copy(
                    src_ref=ins[a].at[2 * px + py], dst_ref=outs[a].at[k], send_sem=send_sems.at[a * n_peer + k],
                    recv_sem=recv_sems.at[a * n_peer + k], device_id=(px, py, c), device_id_type=MESH)
                pairs.append((cp, cp))
        return pairs

    start, finish = _start_then_wait(copies)
    return _Rider(parts, [jax.ShapeDtypeStruct((n_peer,) + p.shape[1:], p.dtype) for p in parts], n * n_peer,
                  start, finish)


def _gather_devices(buf, name):
    n_peer = N_DEV - 1

    def body(in_ref, out_ref, send_sems, recv_sems, local_sem):
        x, y, c = _place()
        me = 4 * x + 2 * y + c
        mine = pltpu.make_async_copy(in_ref, out_ref.at[me], local_sem)
        mine.start()
        peers = []
        for k in range(1, N_DEV):
            fx, fy, fc = (k >> 2) & 1, (k >> 1) & 1, k & 1
            peers.append((x ^ fx, y ^ fy, c ^ fc))
        sends = []
        for k, peer in enumerate(peers):
            cp = pltpu.make_async_remote_copy(
                src_ref=in_ref, dst_ref=out_ref.at[me], send_sem=send_sems.at[k], recv_sem=recv_sems.at[k],
                device_id=peer, device_id_type=MESH)
            cp.start()
            sends.append(cp)
        for k, (px, py, pc) in enumerate(peers):
            pltpu.make_async_remote_copy(
                src_ref=in_ref, dst_ref=out_ref.at[4 * px + 2 * py + pc], send_sem=send_sems.at[k],
                recv_sem=recv_sems.at[k], device_id=(px, py, pc), device_id_type=MESH).wait_recv()
        for cp in sends:
            cp.wait_send()
        mine.wait()

    return pl.pallas_call(
        body, name=name, in_specs=[_ANY], out_specs=_ANY,
        out_shape=jax.ShapeDtypeStruct((N_DEV,) + buf.shape, buf.dtype),
        scratch_shapes=[pltpu.SemaphoreType.DMA((n_peer,)), pltpu.SemaphoreType.DMA((n_peer,)),
                        pltpu.SemaphoreType.DMA(())],
    )(buf)


class _Dims:
    def __init__(self, x, p, w_in, sgu_norm, conv_w, kv_norm, w_ukv, w_out):
        self.t, self.d = x.shape[1], x.shape[2]
        self.depth = w_in.shape[0]
        self.ple = p.shape[3]
        self.in_w = w_in.shape[2] * N_SHARD
        self.ah = sgu_norm.shape[1]
        self.aw = self.ah * HEAD
        self.bw = conv_w.shape[2] * N_SHARD
        self.kvr = kv_norm.shape[1]
        self.ch = w_ukv.shape[2] * N_SHARD // (2 * HEAD)
        self.cw = self.ch * HEAD
        self.mix = w_out.shape[1] * N_SHARD
        assert self.mix == self.aw + self.bw + self.cw and self.aw == self.bw
        self.qw = self.ch * 2 * HEAD
        segs = [('a', 3 * self.aw, 3 * self.aw), ('ckv', self.kvr, self.kvr), ('b', 4 * self.bw, 4 * self.bw),
                ('q', self.qw, self.qw), ('cz', self.cw, self.cw), ('kr', HEAD, HEAD)]
        off = 0
        self.off = {}
        for nm, width, align in segs:
            off = -(-off // align) * align
            self.off[nm] = off
            off += width
        self.inp = -(-off // 512) * 512
        q_real = self.ch * (HEAD + ROPE)
        widths = [3 * self.aw, 4 * self.bw, q_real, self.kvr, ROPE, self.cw]
        assert sum(widths) == self.in_w
        starts = [0]
        for wd in widths:
            starts.append(starts[-1] + wd)
        self.src = dict(zip(['a', 'b', 'q', 'ckv', 'kr', 'cz'], zip(starts[:-1], widths)))


def _rearrange_w_in(w, dm):
    lead, d = w.shape[:-2], w.shape[-1]
    axis = w.ndim - 2

    def rows(nm):
        s, wd = dm.src[nm]
        return lax.slice_in_dim(w, s, s + wd, axis=axis)

    pieces = {nm: rows(nm) for nm in ('a', 'b', 'ckv', 'cz')}
    q = rows('q').reshape(lead + (dm.ch, HEAD + ROPE, d))
    pieces['q'] = jnp.pad(q, [(0, 0)] * (len(lead) + 1) + [(0, HEAD - ROPE), (0, 0)]).reshape(lead + (dm.qw, d))
    pieces['kr'] = jnp.pad(rows('kr'), [(0, 0)] * len(lead) + [(0, HEAD - ROPE), (0, 0)])
    out, cur = [], 0
    for nm in sorted(dm.off, key=lambda k: dm.off[k]):
        if dm.off[nm] > cur:
            out.append(jnp.zeros(lead + (dm.off[nm] - cur, d), w.dtype))
        out.append(pieces[nm])
        cur = dm.off[nm] + pieces[nm].shape[axis]
    if dm.inp > cur:
        out.append(jnp.zeros(lead + (dm.inp - cur, d), w.dtype))
    return jnp.concatenate(out, axis=axis)


def _unarrange_w_in(g, dm):
    d = g.shape[1]

    def seg(nm, width):
        return g[dm.off[nm]:dm.off[nm] + width]

    q = seg('q', dm.qw).reshape(dm.ch, 2 * HEAD, d)[:, :HEAD + ROPE].reshape(dm.ch * (HEAD + ROPE), d)
    return jnp.concatenate([seg('a', 3 * dm.aw), seg('b', 4 * dm.bw), q, seg('ckv', dm.kvr), seg('kr', ROPE),
                            seg('cz', dm.cw)], axis=0)


def _finish_dproj(dproj, parts, widths, dm):
    t = dproj.shape[0]
    for nm, part in parts.items():
        dproj = lax.dynamic_update_slice(dproj, part, (0, dm.off[nm]))
    cur = 0
    for nm in sorted(dm.off, key=lambda k: dm.off[k]):
        if dm.off[nm] > cur:
            dproj = lax.dynamic_update_slice(dproj, jnp.zeros((t, dm.off[nm] - cur), BF16), (0, cur))
        cur = dm.off[nm] + widths[nm]
    assert cur == dm.inp
    return dproj


def _rope_tables(positions):
    inv = 1.0 / (ROPE_BASE ** (jnp.arange(0, ROPE, 2, dtype=F32) / ROPE))
    ang = positions.astype(F32)[:, None] * inv
    cos, sin = jnp.cos(ang), jnp.sin(ang)
    t = positions.shape[0]
    half = ROPE // 2
    cos_t = jnp.concatenate([cos, cos, jnp.zeros((t, HEAD - ROPE), F32)], axis=-1)
    sin_a = jnp.concatenate([-sin, jnp.zeros((t, HEAD - half), F32)], axis=-1)
    sin_b = jnp.concatenate([jnp.zeros((t, half), F32), sin, jnp.zeros((t, HEAD - ROPE), F32)], axis=-1)
    return cos_t, sin_a, sin_b


def _pad_gain(g):
    return jnp.pad(g, (0, HEAD - g.shape[0]))[None, :]


def _shard_major(g, axis):
    shape = g.shape
    g = g.reshape(shape[:axis] + (N_SHARD, shape[axis] // N_SHARD) + shape[axis + 1:])
    g = jnp.moveaxis(g, axis, 0)
    rows, cols = g.shape[1], g.shape[2]
    return jnp.swapaxes(g.reshape(N_SHARD, 2, rows // 2, cols), 0, 1)


def _pack(arrs):
    flat = jnp.concatenate([a.reshape(-1) for a in arrs])
    pad = (-flat.shape[0]) % (8 * HEAD)
    return jnp.pad(flat, (0, pad)).reshape(-1, HEAD)


def _unpack(buf, shapes):
    flat = buf.reshape(-1)
    out, cur = [], 0
    for s in shapes:
        size = 1
        for v in s:
            size *= v
        out.append(flat[cur:cur + size].reshape(s))
        cur += size
    return out


def kernel(x, p, positions, attn_norm, w_in, sgu_norm, w_spatial, b_spatial, conv_w, conv_b, kv_norm, w_ukv, q_nope_norm, q_rope_norm, k_nope_norm, k_rope_norm, out_norm, w_out, ple_norm, w_ple_gate, w_ple_proj, loss_target, m_attn_norm, m_w_in, m_sgu_norm, m_w_spatial, m_b_spatial, m_conv_w, m_conv_b, m_kv_norm, m_w_ukv, m_q_nope_norm, m_q_rope_norm, m_k_nope_norm, m_k_rope_norm, m_out_norm, m_w_out, m_ple_norm, m_w_ple_gate, m_w_ple_proj, v_attn_norm, v_w_in, v_sgu_norm, v_w_spatial, v_b_spatial, v_conv_w, v_conv_b, v_kv_norm, v_w_ukv, v_q_nope_norm, v_q_rope_norm, v_k_nope_norm, v_k_rope_norm, v_out_norm, v_w_out, v_ple_norm, v_w_ple_gate, v_w_ple_proj):
    weights = dict(attn_norm=attn_norm, w_in=w_in, sgu_norm=sgu_norm, w_spatial=w_spatial, b_spatial=b_spatial,
                   conv_w=conv_w, conv_b=conv_b, kv_norm=kv_norm, w_ukv=w_ukv, q_nope_norm=q_nope_norm,
                   q_rope_norm=q_rope_norm, k_nope_norm=k_nope_norm, k_rope_norm=k_rope_norm, out_norm=out_norm,
                   w_out=w_out, ple_norm=ple_norm, w_ple_gate=w_ple_gate, w_ple_proj=w_ple_proj)
    mom_m = dict(attn_norm=m_attn_norm, w_in=m_w_in, sgu_norm=m_sgu_norm, w_spatial=m_w_spatial,
                 b_spatial=m_b_spatial, conv_w=m_conv_w, conv_b=m_conv_b, kv_norm=m_kv_norm, w_ukv=m_w_ukv,
                 q_nope_norm=m_q_nope_norm, q_rope_norm=m_q_rope_norm, k_nope_norm=m_k_nope_norm,
                 k_rope_norm=m_k_rope_norm, out_norm=m_out_norm, w_out=m_w_out, ple_norm=m_ple_norm,
                 w_ple_gate=m_w_ple_gate, w_ple_proj=m_w_ple_proj)
    mom_v = dict(attn_norm=v_attn_norm, w_in=v_w_in, sgu_norm=v_sgu_norm, w_spatial=v_w_spatial,
                 b_spatial=v_b_spatial, conv_w=v_conv_w, conv_b=v_conv_b, kv_norm=v_kv_norm, w_ukv=v_w_ukv,
                 q_nope_norm=v_q_nope_norm, q_rope_norm=v_q_rope_norm, k_nope_norm=v_k_nope_norm,
                 k_rope_norm=v_k_rope_norm, out_norm=v_out_norm, w_out=v_w_out, ple_norm=v_ple_norm,
                 w_ple_gate=v_w_ple_gate, w_ple_proj=v_w_ple_proj)
    dm = _Dims(x, p, w_in, sgu_norm, conv_w, kv_norm, w_ukv, w_out)
    for group in (weights, mom_m, mom_v):
        group['w_in'] = jnp.swapaxes(group['w_in'], 1, 2)
    t, d, depth = dm.t, dm.d, dm.depth
    shard = 2 * lax.axis_index("x") + lax.axis_index("y")
    core = lax.axis_index("c")
    scale = float(HEAD + ROPE) ** -0.5

    def local_layer(i):
        return [weights[n][i:i + 1].astype(BF16) for n in BIG]

    def fill_own_slot(gathered, local):
        return [lax.dynamic_update_slice(g, mine[None], (shard,) + (0,) * mine.ndim)
                for g, mine in zip(gathered, local)]

    def layer_weights(filled):
        w = {n: jnp.concatenate([filled[j][s] for s in range(N_SHARD)], axis=BIG_AXIS[n])
             for j, n in enumerate(BIG)}
        w['w_in'] = _rearrange_w_in(w['w_in'], dm)
        return w

    first = local_layer(0) + [conv_w]
    fetched = _comm_call(_fetch_rider(first), "fetch_weights_l0")
    filled = fill_own_slot(_comm_call(_forward_rider(fetched, first), "forward_weights_l0"), first)
    layer_w = [layer_weights(filled)] + [None] * (depth - 1)
    conv_w_full = jnp.concatenate([filled[len(BIG)][s] for s in range(N_SHARD)], axis=2)

    tabs = _rope_tables(positions[0])
    h = x[0]
    saved = []
    for i in range(depth):
        tag = f"l{i}_"
        ga, gb, gc = (out_norm[i][None, :dm.aw], out_norm[i][None, dm.aw:dm.aw + dm.bw],
                      out_norm[i][None, dm.aw + dm.bw:])
        ws_b = w_spatial[i].astype(BF16)
        bb = jnp.broadcast_to(b_spatial[i][:, :, None], (dm.ah, HEAD, HEAD))
        qn_g, qr_g = q_nope_norm[i][None, :], _pad_gain(q_rope_norm[i])
        kn_g, kr_g = k_nope_norm[i][None, :], _pad_gain(k_rope_norm[i])
        kv_g = kv_norm[i][None, :]
        wl = layer_w[i]
        nxt = local_layer(i + 1) if i + 1 < depth else None
        hn = _norm_fwd(h, attn_norm[i][None, :], tag + "norm1")
        if nxt is None:
            proj = _matmul(hn, wl['w_in'], 'nt', BF16, tag + "proj", b_layer=0)
        else:
            proj, fetched = _matmul(hn, wl['w_in'], 'nt', BF16, tag + "proj", b_layer=0, rider=_fetch_rider(nxt))
        y = _sgu_fwd(proj, dm.off['a'], dm.aw, sgu_norm[i], ws_b, bb, ga, tag + "sgu",
                     into=(jnp.zeros((t, dm.mix), BF16), 0))
        y, yconv = _conv_fwd(proj, dm.off['b'], dm.bw, conv_w_full[i], conv_b[i][None, :], gb, tag + "conv",
                             into=(y, _col_block(dm.aw, dm.bw)))
        q_cat, ckv_n, kr_rot = _mla_prep_fwd(proj, dm.off['q'], dm.off['ckv'], dm.off['kr'], dm.ch, dm.kvr, tabs,
                                             qn_g, qr_g, kr_g, kv_g, tag + "mla_prep")
        kv = _matmul(ckv_n, wl['w_ukv'], 'nn', F32, tag + "kv_up", b_layer=0)
        k_cat, k_cat_t, v_aug = _kv_prep_fwd(kv, kr_rot, dm.ch, kn_g, tag + "kv_prep")
        o, lse = _attn_fwd(q_cat, k_cat, v_aug, dm.ch, scale, tag + "attn")
        y = _attn_post_fwd(o, proj, dm.off['cz'], dm.cw, gc, tag + "attn_post",
                           into=(y, _col_block(dm.aw + dm.bw, dm.cw)))
        if nxt is None:
            h1 = _matmul(y, wl['w_out'], 'nn', F32, tag + "out", add=h, b_layer=0)
        else:
            h1, gathered = _matmul(y, wl['w_out'], 'nn', F32, tag + "out", add=h, b_layer=0,
                                   rider=_forward_rider(list(fetched), nxt))
            layer_w[i + 1] = layer_weights(fill_own_slot(gathered, nxt))
        hn2 = _norm_fwd(h1, ple_norm[i][None, :], tag + "norm2")
        gpre = _matmul(hn2, wl['w_ple_gate'], 'nn', F32, tag + "gate", b_layer=0)
        p_b = p[i, 0].astype(BF16)
        pp = _matmul(p_b, wl['w_ple_proj'], 'nn', F32, tag + "ple_proj", b_layer=0)
        h2 = _ple_fwd(h1, gpre, pp, tag + "ple")
        saved.append(dict(h=h, hn=hn, proj=proj, yconv=yconv, q_cat=q_cat, ckv_n=ckv_n, kv=kv, k_cat=k_cat,
                          k_cat_t=k_cat_t,
                          v=v_aug, o=o, lse=lse, y=y, h1=h1, hn2=hn2, gpre=gpre, pp=pp, p_b=p_b, ws_b=ws_b, bb=bb,
                          gains=(ga, gb, gc, qn_g, qr_g, kn_g, kr_g, kv_g)))
        h = h2

    loss_part, dh = _loss_and_grad(h, loss_target[0], "loss")
    loss = lax.psum(loss_part[0, 0], ("x", "y", "c"))

    def chip_sums(sm, from_sibling, tag):
        mine = [lax.dynamic_index_in_dim(g, core, 0, keepdims=False) for g in sm]
        return mine, [_pair_sum_bf16(a, b, f"{tag}chip_sum_{n}") for a, b, n in zip(mine, from_sibling, BIG)]

    def shard_sums(mine, from_sibling, from_chips, tag):
        out = []
        for a, b, r3, n in zip(mine, from_sibling, from_chips, BIG):
            own_a = lax.dynamic_index_in_dim(a, shard, 0, keepdims=False)
            own_b = lax.dynamic_index_in_dim(b, shard, 0, keepdims=False)
            out.append(_shard_sum(own_a, own_b, r3, f"{tag}shard_sum_{n}"))
        return out

    grads = {n: [None] * depth for n in WEIGHTS}
    own_half = {n: [None] * depth for n in BIG}
    sibling_half = {n: [None] * depth for n in BIG}
    carry = None
    for i in reversed(range(depth)):
        tag = f"l{i}_b_"
        gtag = f"l{i + 1}_g_"
        sv = saved[i]
        wl = layer_w[i]
        ga, gb, gc, qn_g, qr_g, kn_g, kr_g, kv_g = sv['gains']
        proj = sv['proj']
        dgpre, dpp = _ple_bwd(sv['gpre'], sv['pp'], dh, tag + "ple")
        grads['w_ple_proj'][i] = _matmul(sv['p_b'], dpp, 'tn', F32, tag + "d_w_ple_proj")
        if carry is None:
            grads['w_ple_gate'][i] = _matmul(sv['hn2'], dgpre, 'tn', F32, tag + "d_w_gate")
        else:
            grads['w_ple_gate'][i], from_sibling = _matmul(sv['hn2'], dgpre, 'tn', F32, tag + "d_w_gate",
                                                           rider=_sibling_rider(carry, True))
            mine, sums = chip_sums(carry, from_sibling, gtag)
        d_hn2 = _matmul(dgpre, wl['w_ple_gate'], 'nt', BF16, tag + "d_hn2", b_layer=0)
        dh1, dh1_b, g_ple = _norm_bwd(sv['h1'], ple_norm[i][None, :], d_hn2, dh, tag + "norm2")
        grads['ple_norm'][i] = g_ple[0]
        grads['w_out'][i] = _matmul(sv['y'], dh1_b, 'tn', F32, tag + "d_w_out")
        dy = _matmul(dh1_b, wl['w_out'], 'nt', BF16, tag + "d_y", b_layer=0)
        ws_t = jnp.swapaxes(sv['ws_b'], 1, 2)
        dproj, g_sgu, g_ws, g_bs, g_ga = _sgu_bwd(proj, dm.off['a'], dm.aw, sgu_norm[i], sv['ws_b'], ws_t, sv['bb'],
                                                  ga, dy, dm.inp, tag + "sgu")
        grads['sgu_norm'][i], grads['w_spatial'][i], grads['b_spatial'][i] = g_sgu, g_ws, g_bs[:, :, 0]
        dyc, d_bb, d_bz, g_gb, g_cb = _conv_bwd_gate(proj, dm.off['b'], dm.bw, sv['yconv'], gb, dy, tag + "conv_gate")
        dproj, g_cw = _conv_bwd_taps(proj, dm.off['b'], dm.bw, dyc, conv_w_full[i], d_bb, d_bz, tag + "conv_taps",
                                     into=(dproj, _col_block(dm.off['b'], 4 * dm.bw)))
        grads['conv_b'][i], grads['conv_w'][i] = g_cb[0], g_cw
        dproj, d_o, dsum, g_gc = _attn_post_bwd(sv['o'], proj, dm.off['cz'], dm.cw, gc, dy,
                                                _col_block(dm.aw + dm.bw, dm.cw), tag + "attn_post",
                                                into=(dproj, _col_block(dm.off['cz'], dm.cw)))
        grads['out_norm'][i] = jnp.concatenate([g_ga[0], g_gb[0], g_gc[0]])
        dq_t, dk_cat, dv = _attn_bwd(sv['q_cat'], sv['k_cat'], sv['k_cat_t'], sv['v'], d_o,
                                     sv['lse'].reshape(dm.ch, 1, t), dsum.reshape(dm.ch, 1, t), dm.ch, scale,
                                     tag + "attn_bwd")
        dkv, dkr_rot, g_kn = _kv_prep_bwd(sv['kv'], dm.ch, kn_g, dk_cat, dv, tag + "kv_prep")
        grads['k_nope_norm'][i] = g_kn[0]
        grads['w_ukv'][i] = _matmul(sv['ckv_n'], dkv, 'tn', F32, tag + "d_w_ukv")
        dckv_n = _matmul(dkv, wl['w_ukv'], 'nt', BF16, tag + "d_ckv", b_layer=0)
        kr_width = dm.inp - dm.off['kr']
        dproj, d_ckv, d_kr, g_qn, g_qr, g_kr, g_kv = _mla_prep_bwd(
            proj, dm.off['q'], dm.off['ckv'], dm.off['kr'], dm.ch, dm.kvr, tabs, qn_g, qr_g, kr_g, kv_g,
            dq_t, scale, dckv_n, dkr_rot, kr_width, tag + "mla_prep", into=(dproj, _col_block(dm.off['q'], dm.qw)))
        grads['q_nope_norm'][i], grads['q_rope_norm'][i] = g_qn[0], g_qr[0, :ROPE]
        grads['k_rope_norm'][i], grads['kv_norm'][i] = g_kr[0, :ROPE], g_kv[0]
        dproj = _finish_dproj(dproj, dict(ckv=d_ckv, kr=d_kr),
                              dict(a=3 * dm.aw, b=4 * dm.bw, ckv=dm.kvr, q=dm.qw, cz=dm.cw, kr=kr_width), dm)
        if carry is None:
            d_w_in = _matmul(dproj, sv['hn'], 'tn', F32, tag + "d_w_in")
            d_hn = _matmul(dproj, wl['w_in'], 'nn', BF16, tag + "d_hn", b_layer=0)
        else:
            d_w_in, from_chips = _matmul(dproj, sv['hn'], 'tn', F32, tag + "d_w_in", rider=_owner_rider(sums))
            halves = shard_sums(mine, from_sibling, from_chips, gtag)
            d_hn, from_core = _matmul(dproj, wl['w_in'], 'nn', BF16, tag + "d_hn", b_layer=0,
                                      rider=_sibling_rider(halves, False))
            for n, own, recv in zip(BIG, halves, from_core):
                own_half[n][i + 1], sibling_half[n][i + 1] = own, recv
        grads['w_in'][i] = _unarrange_w_in(d_w_in, dm)
        dh, _, g_an = _norm_bwd(sv['h'], attn_norm[i][None, :], d_hn, dh1, tag + "norm1")
        grads['attn_norm'][i] = g_an[0]
        carry = [_shard_major(grads[n][i], BIG_AXIS[n] - 1) for n in BIG]
    grad_x = dh[None]

    from_sibling = _comm_call(_sibling_rider(carry, True), "l0_g_to_sibling")
    mine, sums = chip_sums(carry, from_sibling, "l0_g_")
    from_chips = _comm_call(_owner_rider(sums), "l0_g_to_owner_chips")
    halves = shard_sums(mine, from_sibling, from_chips, "l0_g_")
    from_core = _comm_call(_sibling_rider(halves, False), "l0_g_share_sibling")
    for n, own, recv in zip(BIG, halves, from_core):
        own_half[n][0], sibling_half[n][0] = own, recv

    core_flag = core.astype(F32).reshape(1, 1)
    out_g, out_d, out_m, out_v = {}, {}, {}, {}
    for n in BIG:
        out_g[n], out_d[n], out_m[n], out_v[n] = _adamw_two_halves(
            weights[n], jnp.stack(own_half[n]), jnp.stack(sibling_half[n]), core_flag, mom_m[n], mom_v[n],
            f"adamw_{n}")
    for out in (out_g, out_d, out_m, out_v):
        out['w_in'] = jnp.swapaxes(out['w_in'], 1, 2)
    grads = {n: jnp.stack(grads[n]) for n in SMALL}

    shapes = [grads[n].shape for n in SMALL]
    summed = _unpack(_sum_devices(_gather_devices(_pack([grads[n] for n in SMALL]), "gather_small_grads"),
                                  "sum_small_grads"), shapes)
    small_g = dict(zip(SMALL, summed))
    small_g['conv_w'] = lax.dynamic_slice_in_dim(small_g['conv_w'], shard * conv_w.shape[2], conv_w.shape[2], axis=2)
    local_shapes = [weights[n].shape for n in SMALL]
    d_s, m_s, v_s = _adamw(_pack([weights[n] for n in SMALL]), _pack([small_g[n] for n in SMALL]),
                           _pack([mom_m[n] for n in SMALL]), _pack([mom_v[n] for n in SMALL]), "adamw_small")
    for n, dd, mm, vv in zip(SMALL, _unpack(d_s, local_shapes), _unpack(m_s, local_shapes),
                             _unpack(v_s, local_shapes)):
        out_g[n], out_d[n], out_m[n], out_v[n] = small_g[n], dd, mm, vv

    return (loss, grad_x, *[out_g[n] for n in WEIGHTS], *[out_d[n] for n in WEIGHTS],
            *[out_m[n] for n in WEIGHTS], *[out_v[n] for n in WEIGHTS])
```

```python
import functools

import jax
import jax.numpy as jnp
from jax import lax
from jax.experimental import pallas as pl
from jax.experimental.pallas import tpu as pltpu

F32 = jnp.float32
BF16 = jnp.bfloat16
EPS = 1e-6
HEAD = 128
ROPE = 64
ROPE_BASE = 10000.0
CONV_TAPS = 3
N_SHARD = 4
N_DEV = 8
ADAM_LR = 0.001
ADAM_B1 = 0.9
ADAM_B2 = 0.999
ADAM_EPS = 1e-08
ADAM_WD = 0.01
ADAM_STEP = 10
MESH = pl.DeviceIdType.MESH
VMEM_LIMIT = 56 * 1024 * 1024
HALO_ROWS = 16

WEIGHTS = ['attn_norm', 'w_in', 'sgu_norm', 'w_spatial', 'b_spatial', 'conv_w', 'conv_b', 'kv_norm', 'w_ukv',
           'q_nope_norm', 'q_rope_norm', 'k_nope_norm', 'k_rope_norm', 'out_norm', 'w_out', 'ple_norm',
           'w_ple_gate', 'w_ple_proj']
BIG = ['w_in', 'w_ukv', 'w_out', 'w_ple_gate', 'w_ple_proj']
BIG_AXIS = {'w_in': 1, 'w_ukv': 2, 'w_out': 1, 'w_ple_gate': 1, 'w_ple_proj': 2}
SMALL = [n for n in WEIGHTS if n not in BIG]


def _pick(n, cands):
    for c in cands:
        if n % c == 0:
            return c
    return n


def _params(sem=None):
    return pltpu.CompilerParams(dimension_semantics=sem, vmem_limit_bytes=VMEM_LIMIT)


def _matmul(a, b, mode, out_dtype, name, add=None, b_layer=None, rider=None):
    b_shape = b.shape if b_layer is None else b.shape[1:]
    if mode == 'nn':
        (m, k), n = a.shape, b_shape[1]
    elif mode == 'nt':
        (m, k), n = a.shape, b_shape[0]
    else:
        (k, m), n = a.shape, b_shape[1]
    tm = _pick(m, (1280, 1024, 512, 256, 128))
    tn = _pick(n, (1536, 1024, 512, 256, 128))
    tk = k if k <= 2048 else _pick(k, (2048, 1536, 1024, 512, 256, 128))
    nk = k // tk
    if mode == 'tn':
        a_spec = pl.BlockSpec((tk, tm), lambda i, j, kk: (kk, i))
        dims = (((0,), (0,)), ((), ()))
    else:
        a_spec = pl.BlockSpec((tm, tk), lambda i, j, kk: (i, kk))
        dims = (((1,), (0,)), ((), ())) if mode == 'nn' else (((1,), (1,)), ((), ()))
    b_block = (tn, tk) if mode == 'nt' else (tk, tn)
    if b_layer is None:
        b_spec = pl.BlockSpec(b_block, (lambda i, j, kk: (j, kk)) if mode == 'nt' else (lambda i, j, kk: (kk, j)))
    else:
        b_spec = pl.BlockSpec((None,) + b_block, (lambda i, j, kk: (b_layer, j, kk)) if mode == 'nt'
                              else (lambda i, j, kk: (b_layer, kk, j)))
    o_spec = pl.BlockSpec((tm, tn), lambda i, j, kk: (i, j))
    has_add = add is not None

    def body(*refs):
        a_ref, b_ref = refs[0], refs[1]
        add_ref = refs[2] if has_add else None
        o_ref = refs[3] if has_add else refs[2]

        def product():
            return lax.dot_general(a_ref[...], b_ref[...], dims, preferred_element_type=F32)

        def finish(res):
            if has_add:
                res = res + add_ref[...]
            o_ref[...] = res.astype(out_dtype)

        if nk == 1:
            finish(product())
        else:
            acc_ref = refs[-1]
            kk = pl.program_id(2)

            @pl.when(kk == 0)
            def _():
                acc_ref[...] = product()

            @pl.when((kk > 0) & (kk < nk - 1))
            def _():
                acc_ref[...] += product()

            @pl.when(kk == nk - 1)
            def _():
                finish(acc_ref[...] + product())

    in_specs = [a_spec, b_spec] + ([o_spec] if has_add else [])
    args = [a, b] + ([add] if has_add else [])
    grid = (m // tm, n // tn, nk)
    scratch = [pltpu.VMEM((tm, tn), F32)] if nk > 1 else []
    if rider is None:
        return pl.pallas_call(
            body, name=name, grid=grid, in_specs=in_specs, out_specs=o_spec,
            out_shape=jax.ShapeDtypeStruct((m, n), out_dtype), scratch_shapes=scratch,
            compiler_params=_params(("parallel", "parallel", "arbitrary")),
        )(*args)

    n_in, n_rin, n_rout = len(args), len(rider.arrays), len(rider.out_shapes)

    def body_with_rider(*refs):
        r_in = refs[n_in:n_in + n_rin]
        r_out = refs[n_in + n_rin + 1:n_in + n_rin + 1 + n_rout]
        own = refs[:n_in] + refs[n_in + n_rin:n_in + n_rin + 1] + refs[n_in + n_rin + 1 + n_rout:len(refs) - 2]
        send_sems, recv_sems = refs[-2:]
        ids = [pl.program_id(ax) for ax in range(3)]

        @pl.when((ids[0] == 0) & (ids[1] == 0) & (ids[2] == 0))
        def _():
            rider.start(r_in, r_out, send_sems, recv_sems)

        body(*own)

        @pl.when((ids[0] == grid[0] - 1) & (ids[1] == grid[1] - 1) & (ids[2] == grid[2] - 1))
        def _():
            rider.finish(r_in, r_out, send_sems, recv_sems)

    res = pl.pallas_call(
        body_with_rider, name=name, grid=grid, in_specs=in_specs + [_ANY] * n_rin,
        out_specs=[o_spec] + [_ANY] * n_rout,
        out_shape=[jax.ShapeDtypeStruct((m, n), out_dtype)] + rider.out_shapes,
        scratch_shapes=scratch + rider.sems(), input_output_aliases=rider.aliases(n_in, 1),
        compiler_params=_params(("arbitrary", "arbitrary", "arbitrary")),
    )(*args, *rider.arrays)
    return res[0], res[1:]


def _rms(x, n):
    r = lax.rsqrt(jnp.sum(x * x, axis=-1, keepdims=True) * (1.0 / n) + EPS)
    return x * r, r


def _rms_bwd(dxhat, xhat, r, n):
    return r * (dxhat - xhat * (jnp.sum(dxhat * xhat, axis=-1, keepdims=True) * (1.0 / n)))


def _sigmoid(z):
    return 1.0 / (1.0 + jnp.exp(-z))


def _silu_and_grad(z):
    sig = _sigmoid(z)
    return z * sig, sig * (1.0 + z * (1.0 - sig))


def _colsum(x):
    return jnp.sum(x, axis=0, keepdims=True)


def _rope(t, cos_t, sin_a, sin_b):
    return t * cos_t + pltpu.roll(t, 96, 1) * sin_a + pltpu.roll(t, 32, 1) * sin_b


def _rope_bwd(d, cos_t, sin_a, sin_b):
    return d * cos_t + pltpu.roll(d * sin_a, 32, 1) + pltpu.roll(d * sin_b, 96, 1)


def _shift_down(g, first_row):
    row = lax.broadcasted_iota(jnp.int32, g.shape, 0)
    return jnp.where(row == 0, first_row, pltpu.roll(g, 1, 0))


def _shift_up(g, last_row):
    n = g.shape[0]
    row = lax.broadcasted_iota(jnp.int32, g.shape, 0)
    return jnp.where(row == n - 1, last_row, pltpu.roll(g, n - 1, 0))


def _row_spec(r, w, col=0):
    return pl.BlockSpec((r, w), lambda i: (i, col))


def _const_spec(shape):
    nd = len(shape)
    return pl.BlockSpec(shape, lambda i: (0,) * nd)


def _col_block(off, w):
    assert off % w == 0, (off, w)
    return off // w


def _zero_at_first_step(refs):
    @pl.when(pl.program_id(0) == 0)
    def _():
        for ref in refs:
            ref[...] = jnp.zeros(ref.shape, ref.dtype)


def _row_call(body, name, t, r, in_specs, args, out_specs, out_shapes, scratch=(), into=None):
    if into is None:
        return pl.pallas_call(
            body, name=name, grid=(t // r,), in_specs=in_specs, out_specs=out_specs, out_shape=out_shapes,
            scratch_shapes=list(scratch), compiler_params=_params(("arbitrary",)),
        )(*args)
    buf, col = into
    single = not isinstance(out_specs, (list, tuple))
    specs = [out_specs] if single else list(out_specs)
    shapes = [out_shapes] if single else list(out_shapes)
    width = shapes[0].shape[1]
    assert shapes[0].dtype == buf.dtype and buf.shape[0] == t
    specs[0] = _row_spec(r, width, col)
    shapes[0] = jax.ShapeDtypeStruct(buf.shape, buf.dtype)
    n_in = len(args)

    def body_in_place(*refs):
        body(*refs[:n_in], *refs[n_in + 1:])

    res = pl.pallas_call(
        body_in_place, name=name, grid=(t // r,), in_specs=list(in_specs) + [_ANY], out_specs=specs, out_shape=shapes,
        scratch_shapes=list(scratch), input_output_aliases={n_in: 0}, compiler_params=_params(("arbitrary",)),
    )(*args, buf)
    return res[0] if single else res


def _norm_fwd(h, g, name):
    t, d = h.shape
    r = _pick(t, (256, 128))

    def body(h_ref, g_ref, o_ref):
        xhat, _ = _rms(h_ref[...], d)
        o_ref[...] = (xhat * g_ref[...]).astype(BF16)

    return _row_call(body, name, t, r, [_row_spec(r, d), _const_spec((1, d))], (h, g),
                     _row_spec(r, d), jax.ShapeDtypeStruct((t, d), BF16))


def _norm_bwd(h, g, d_hn, d_res, name):
    t, d = h.shape
    r = _pick(t, (256, 128))

    def body(h_ref, g_ref, dy_ref, dres_ref, dh_ref, dhb_ref, dg_ref):
        _zero_at_first_step([dg_ref])
        xhat, rr = _rms(h_ref[...], d)
        dy = dy_ref[...].astype(F32)
        dg_ref[...] += _colsum(dy * xhat)
        dh = dres_ref[...] + _rms_bwd(dy * g_ref[...], xhat, rr, d)
        dh_ref[...] = dh
        dhb_ref[...] = dh.astype(BF16)

    return _row_call(body, name, t, r,
                     [_row_spec(r, d), _const_spec((1, d)), _row_spec(r, d), _row_spec(r, d)], (h, g, d_hn, d_res),
                     [_row_spec(r, d), _row_spec(r, d), _const_spec((1, d))],
                     [jax.ShapeDtypeStruct((t, d), F32), jax.ShapeDtypeStruct((t, d), BF16),
                      jax.ShapeDtypeStruct((1, d), F32)])


def _sgu_scores(v, gs_ref, ws_ref, bb_ref, s_scr, r, ah, keep=None):
    for kk in range(r // HEAD):
        for hh in range(ah):
            rows, cols = slice(kk * HEAD, (kk + 1) * HEAD), slice(hh * HEAD, (hh + 1) * HEAD)
            vhat, rv = _rms(v[rows, cols], HEAD)
            vn = vhat * gs_ref[pl.ds(hh, 1), :]
            s_scr[rows, cols] = jnp.dot(ws_ref[hh], vn.astype(BF16), preferred_element_type=F32) + bb_ref[hh]
            if keep is not None:
                keep[(kk, hh)] = (vhat, rv, vn)


def _sgu_fwd(proj, off, aw, gs, ws, bb, ga, name, into=None):
    t = proj.shape[0]
    ah = aw // HEAD
    r = _pick(t, (256, 128))
    cb = _col_block(off, aw)

    def body(u_ref, v_ref, z_ref, gs_ref, ws_ref, bb_ref, ga_ref, o_ref, s_scr):
        _sgu_scores(v_ref[...].astype(F32), gs_ref, ws_ref, bb_ref, s_scr, r, ah)
        sil, _ = _silu_and_grad(z_ref[...].astype(F32))
        yhat, _ = _rms(u_ref[...].astype(F32) * s_scr[...] * sil, aw)
        o_ref[...] = (yhat * ga_ref[...]).astype(BF16)

    return _row_call(
        body, name, t, r,
        [_row_spec(r, aw, cb), _row_spec(r, aw, cb + 1), _row_spec(r, aw, cb + 2), _const_spec((ah, HEAD)),
         _const_spec((ah, HEAD, HEAD)), _const_spec((ah, HEAD, HEAD)), _const_spec((1, aw))],
        (proj, proj, proj, gs, ws, bb, ga),
        _row_spec(r, aw), jax.ShapeDtypeStruct((t, aw), BF16), scratch=[pltpu.VMEM((r, aw), F32)], into=into)


def _sgu_bwd(proj, off, aw, gs, ws, ws_t, bb, ga, dy, out_width, name):
    t = proj.shape[0]
    ah = aw // HEAD
    r = _pick(t, (256, 128))
    assert off == 0
    cb = _col_block(off, aw)

    def body(u_ref, v_ref, z_ref, gs_ref, ws_ref, wst_ref, bb_ref, ga_ref, dy_ref,
             d_ref, dgs_ref, dws_ref, db_ref, dga_ref, s_scr, dv_scr):
        _zero_at_first_step([dgs_ref, dws_ref, db_ref, dga_ref])
        keep = {}
        _sgu_scores(v_ref[...].astype(F32), gs_ref, ws_ref, bb_ref, s_scr, r, ah, keep)
        u, z, s = u_ref[...].astype(F32), z_ref[...].astype(F32), s_scr[...]
        sil, dsil = _silu_and_grad(z)
        yhat, rr = _rms(u * s * sil, aw)
        dy_f = dy_ref[...].astype(F32)
        dga_ref[...] += _colsum(dy_f * yhat)
        dya = _rms_bwd(dy_f * ga_ref[...], yhat, rr, aw)
        d_ref[:, 0:aw] = (dya * s * sil).astype(BF16)
        d_ref[:, 2 * aw:3 * aw] = (dya * u * s * dsil).astype(BF16)
        ds = dya * u * sil
        for kk in range(r // HEAD):
            for hh in range(ah):
                rows, cols = slice(kk * HEAD, (kk + 1) * HEAD), slice(hh * HEAD, (hh + 1) * HEAD)
                vhat, rv, vn = keep[(kk, hh)]
                ds_blk = ds[rows, cols]
                db_ref[hh] += jnp.sum(ds_blk, axis=1, keepdims=True)
                ds_b = ds_blk.astype(BF16)
                dws_ref[hh] += lax.dot_general(ds_b, vn.astype(BF16), (((1,), (1,)), ((), ())),
                                               preferred_element_type=F32)
                dvn = jnp.dot(wst_ref[hh], ds_b, preferred_element_type=F32)
                dgs_ref[pl.ds(hh, 1), :] += _colsum(dvn * vhat)
                dv_scr[rows, cols] = _rms_bwd(dvn * gs_ref[pl.ds(hh, 1), :], vhat, rv, HEAD)
        d_ref[:, aw:2 * aw] = dv_scr[...].astype(BF16)

    return _row_call(
        body, name, t, r,
        [_row_spec(r, aw, cb), _row_spec(r, aw, cb + 1), _row_spec(r, aw, cb + 2), _const_spec((ah, HEAD)),
         _const_spec((ah, HEAD, HEAD)), _const_spec((ah, HEAD, HEAD)), _const_spec((ah, HEAD, HEAD)),
         _const_spec((1, aw)), _row_spec(r, aw, 0)],
        (proj, proj, proj, gs, ws, ws_t, bb, ga, dy),
        [_row_spec(r, 3 * aw), _const_spec((ah, HEAD)), _const_spec((ah, HEAD, HEAD)), _const_spec((ah, HEAD, 1)),
         _const_spec((1, aw))],
        [jax.ShapeDtypeStruct((t, out_width), BF16), jax.ShapeDtypeStruct((ah, HEAD), F32),
         jax.ShapeDtypeStruct((ah, HEAD, HEAD), F32), jax.ShapeDtypeStruct((ah, HEAD, 1), F32),
         jax.ShapeDtypeStruct((1, aw), F32)],
        scratch=[pltpu.VMEM((r, aw), F32), pltpu.VMEM((r, aw), F32)])


def _halo_specs(t, r, w, col, rows):
    per = r // rows
    last = t // rows - 1
    prev = pl.BlockSpec((rows, w), lambda i: (jnp.maximum(i * per - 1, 0), col))
    nxt = pl.BlockSpec((rows, w), lambda i: (jnp.minimum((i + 1) * per, last), col))
    return prev, nxt


def _edge_rows(prev_ref, next_ref, n_steps):
    i = pl.program_id(0)
    rows = prev_ref.shape[0]
    before = prev_ref[...].astype(F32)[rows - 1:rows, :] * (i > 0).astype(F32)
    after = next_ref[...].astype(F32)[0:1, :] * (i < n_steps - 1).astype(F32)
    return before, after


def _conv_fwd(proj, off, bw, cw, cb_, gb, name, into=None):
    t = proj.shape[0]
    r = _pick(t, (256, 128))
    n_steps = t // r
    c0 = _col_block(off, bw)
    cp, cn = _halo_specs(t, r, bw, c0 + 1, HALO_ROWS)
    hp, hn = _halo_specs(t, r, bw, c0 + 2, HALO_ROWS)

    def body(b_ref, c_ref, h_ref, z_ref, cp_ref, cn_ref, hp_ref, hn_ref, cw_ref, cb_ref, gb_ref, o_ref, yc_ref):
        g = c_ref[...].astype(F32) * h_ref[...].astype(F32)
        c_before, c_after = _edge_rows(cp_ref, cn_ref, n_steps)
        h_before, h_after = _edge_rows(hp_ref, hn_ref, n_steps)
        yconv = (cb_ref[...] + cw_ref[0:1, :] * _shift_down(g, c_before * h_before) + cw_ref[1:2, :] * g
                 + cw_ref[2:3, :] * _shift_up(g, c_after * h_after))
        yc_ref[...] = yconv
        sil, _ = _silu_and_grad(z_ref[...].astype(F32))
        yhat, _ = _rms(b_ref[...].astype(F32) * yconv * sil, bw)
        o_ref[...] = (yhat * gb_ref[...]).astype(BF16)

    return _row_call(
        body, name, t, r,
        [_row_spec(r, bw, c0), _row_spec(r, bw, c0 + 1), _row_spec(r, bw, c0 + 2), _row_spec(r, bw, c0 + 3),
         cp, cn, hp, hn, _const_spec((CONV_TAPS, bw)), _const_spec((1, bw)), _const_spec((1, bw))],
        (proj, proj, proj, proj, proj, proj, proj, proj, cw, cb_, gb),
        [_row_spec(r, bw), _row_spec(r, bw)],
        [jax.ShapeDtypeStruct((t, bw), BF16), jax.ShapeDtypeStruct((t, bw), F32)], into=into)


def _conv_bwd_gate(proj, off, bw, yconv, gb, dy, name):
    t = proj.shape[0]
    r = _pick(t, (256, 128))
    c0 = _col_block(off, bw)

    def body(b_ref, z_ref, yc_ref, gb_ref, dy_ref, dyc_ref, db_ref, dz_ref, dgb_ref, dcb_ref):
        _zero_at_first_step([dgb_ref, dcb_ref])
        b, z, yconv_v = b_ref[...].astype(F32), z_ref[...].astype(F32), yc_ref[...]
        sil, dsil = _silu_and_grad(z)
        yhat, rr = _rms(b * yconv_v * sil, bw)
        dy_f = dy_ref[...].astype(F32)
        dgb_ref[...] += _colsum(dy_f * yhat)
        dyb = _rms_bwd(dy_f * gb_ref[...], yhat, rr, bw)
        dyc = dyb * b * sil
        dyc_ref[...] = dyc
        dcb_ref[...] += _colsum(dyc)
        db_ref[...] = (dyb * yconv_v * sil).astype(BF16)
        dz_ref[...] = (dyb * b * yconv_v * dsil).astype(BF16)

    return _row_call(
        body, name, t, r,
        [_row_spec(r, bw, c0), _row_spec(r, bw, c0 + 3), _row_spec(r, bw), _const_spec((1, bw)), _row_spec(r, bw, 1)],
        (proj, proj, yconv, gb, dy),
        [_row_spec(r, bw), _row_spec(r, bw), _row_spec(r, bw), _const_spec((1, bw)), _const_spec((1, bw))],
        [jax.ShapeDtypeStruct((t, bw), F32), jax.ShapeDtypeStruct((t, bw), BF16), jax.ShapeDtypeStruct((t, bw), BF16),
         jax.ShapeDtypeStruct((1, bw), F32), jax.ShapeDtypeStruct((1, bw), F32)])


def _conv_bwd_taps(proj, off, bw, dyc, cw, d_gate_b, d_gate_z, name, into=None):
    t = proj.shape[0]
    r = _pick(t, (256, 128))
    n_steps = t // r
    c0 = _col_block(off, bw)
    cp, cn = _halo_specs(t, r, bw, c0 + 1, HALO_ROWS)
    hp, hn = _halo_specs(t, r, bw, c0 + 2, HALO_ROWS)
    dp, dn = _halo_specs(t, r, bw, 0, 8)

    def body(c_ref, h_ref, cp_ref, cn_ref, hp_ref, hn_ref, d_ref, dp_ref, dn_ref, cw_ref, dgb_ref, dgz_ref,
             db_ref, dcw_ref):
        _zero_at_first_step([dcw_ref])
        c, h, d = c_ref[...].astype(F32), h_ref[...].astype(F32), d_ref[...]
        g = c * h
        c_before, c_after = _edge_rows(cp_ref, cn_ref, n_steps)
        h_before, h_after = _edge_rows(hp_ref, hn_ref, n_steps)
        d_before, d_after = _edge_rows(dp_ref, dn_ref, n_steps)
        dg = (cw_ref[0:1, :] * _shift_up(d, d_after) + cw_ref[1:2, :] * d + cw_ref[2:3, :] * _shift_down(d, d_before))
        db_ref[:, 0:bw] = dgb_ref[...]
        db_ref[:, bw:2 * bw] = (dg * h).astype(BF16)
        db_ref[:, 2 * bw:3 * bw] = (dg * c).astype(BF16)
        db_ref[:, 3 * bw:4 * bw] = dgz_ref[...]
        dcw_ref[0:1, :] += _colsum(d * _shift_down(g, c_before * h_before))
        dcw_ref[1:2, :] += _colsum(d * g)
        dcw_ref[2:3, :] += _colsum(d * _shift_up(g, c_after * h_after))

    return _row_call(
        body, name, t, r,
        [_row_spec(r, bw, c0 + 1), _row_spec(r, bw, c0 + 2), cp, cn, hp, hn, _row_spec(r, bw), dp, dn,
         _const_spec((CONV_TAPS, bw)), _row_spec(r, bw), _row_spec(r, bw)],
        (proj, proj, proj, proj, proj, proj, dyc, dyc, dyc, cw, d_gate_b, d_gate_z),
        [_row_spec(r, 4 * bw), _const_spec((CONV_TAPS, bw))],
        [jax.ShapeDtypeStruct((t, 4 * bw), BF16), jax.ShapeDtypeStruct((CONV_TAPS, bw), F32)], into=into)


def _mla_prep_fwd(proj, q_off, ckv_off, kr_off, ch, kvr, tabs, qn_g, qr_g, kr_g, kv_g, name):
    t = proj.shape[0]
    r = _pick(t, (256, 128))
    qw = ch * 2 * HEAD
    cos_t, sin_a, sin_b = tabs

    def body(q_ref, ckv_ref, kr_ref, cos_ref, sa_ref, sb_ref, qn_ref, qr_ref, krg_ref, kvg_ref,
             qo_ref, co_ref, ko_ref):
        cos_v, sa, sb = cos_ref[...], sa_ref[...], sb_ref[...]
        for hh in range(ch):
            lo = hh * 2 * HEAD
            nhat, _ = _rms(q_ref[:, lo:lo + HEAD].astype(F32), HEAD)
            qo_ref[:, lo:lo + HEAD] = (nhat * qn_ref[...]).astype(BF16)
            rhat, _ = _rms(q_ref[:, lo + HEAD:lo + 2 * HEAD].astype(F32), ROPE)
            qo_ref[:, lo + HEAD:lo + 2 * HEAD] = _rope(rhat * qr_ref[...], cos_v, sa, sb).astype(BF16)
        khat, _ = _rms(kr_ref[...].astype(F32), ROPE)
        ko_ref[...] = _rope(khat * krg_ref[...], cos_v, sa, sb).astype(BF16)
        chat, _ = _rms(ckv_ref[...].astype(F32), kvr)
        co_ref[...] = (chat * kvg_ref[...]).astype(BF16)

    tab = _row_spec(r, HEAD)
    gain = _const_spec((1, HEAD))
    return _row_call(
        body, name, t, r,
        [_row_spec(r, qw, _col_block(q_off, qw)), _row_spec(r, kvr, _col_block(ckv_off, kvr)),
         _row_spec(r, HEAD, _col_block(kr_off, HEAD)), tab, tab, tab, gain, gain, gain, _const_spec((1, kvr))],
        (proj, proj, proj, cos_t, sin_a, sin_b, qn_g, qr_g, kr_g, kv_g),
        [_row_spec(r, qw), _row_spec(r, kvr), _row_spec(r, HEAD)],
        [jax.ShapeDtypeStruct((t, qw), BF16), jax.ShapeDtypeStruct((t, kvr), BF16),
         jax.ShapeDtypeStruct((t, HEAD), BF16)])


def _mla_prep_bwd(proj, q_off, ckv_off, kr_off, ch, kvr, tabs, qn_g, qr_g, kr_g, kv_g, dq_cat_t, dq_scale, dckv_n,
                  dkr_rot, kr_width, name, into=None):
    t = proj.shape[0]
    r = _pick(t, (256, 128))
    qw = ch * 2 * HEAD
    cos_t, sin_a, sin_b = tabs

    def body(q_ref, ckv_ref, kr_ref, cos_ref, sa_ref, sb_ref, qn_ref, qr_ref, krg_ref, kvg_ref,
             dq_ref, dc_ref, dk_ref, dqo_ref, dco_ref, dko_ref, dqn_ref, dqr_ref, dkrg_ref, dkvg_ref):
        _zero_at_first_step([dqn_ref, dqr_ref, dkrg_ref, dkvg_ref])
        cos_v, sa, sb = cos_ref[...], sa_ref[...], sb_ref[...]
        dq = dq_ref[...].T * dq_scale
        for hh in range(ch):
            lo = hh * 2 * HEAD
            nhat, nr = _rms(q_ref[:, lo:lo + HEAD].astype(F32), HEAD)
            d_n = dq[:, lo:lo + HEAD]
            dqn_ref[...] += _colsum(d_n * nhat)
            dqo_ref[:, lo:lo + HEAD] = _rms_bwd(d_n * qn_ref[...], nhat, nr, HEAD).astype(BF16)
            rhat, rr = _rms(q_ref[:, lo + HEAD:lo + 2 * HEAD].astype(F32), ROPE)
            d_t = _rope_bwd(dq[:, lo + HEAD:lo + 2 * HEAD], cos_v, sa, sb)
            dqr_ref[...] += _colsum(d_t * rhat)
            dqo_ref[:, lo + HEAD:lo + 2 * HEAD] = _rms_bwd(d_t * qr_ref[...], rhat, rr, ROPE).astype(BF16)
        khat, kr_r = _rms(kr_ref[...].astype(F32), ROPE)
        d_k = _rope_bwd(dk_ref[...], cos_v, sa, sb)
        dkrg_ref[...] += _colsum(d_k * khat)
        dko_ref[:, 0:HEAD] = _rms_bwd(d_k * krg_ref[...], khat, kr_r, ROPE).astype(BF16)
        if kr_width > HEAD:
            dko_ref[:, HEAD:kr_width] = jnp.zeros((r, kr_width - HEAD), BF16)
        chat, cr = _rms(ckv_ref[...].astype(F32), kvr)
        d_c = dc_ref[...].astype(F32)
        dkvg_ref[...] += _colsum(d_c * chat)
        dco_ref[...] = _rms_bwd(d_c * kvg_ref[...], chat, cr, kvr).astype(BF16)

    tab = _row_spec(r, HEAD)
    gain = _const_spec((1, HEAD))
    return _row_call(
        body, name, t, r,
        [_row_spec(r, qw, _col_block(q_off, qw)), _row_spec(r, kvr, _col_block(ckv_off, kvr)),
         _row_spec(r, HEAD, _col_block(kr_off, HEAD)), tab, tab, tab, gain, gain, gain, _const_spec((1, kvr)),
         pl.BlockSpec((qw, r), lambda i: (0, i)), _row_spec(r, kvr), _row_spec(r, HEAD)],
        (proj, proj, proj, cos_t, sin_a, sin_b, qn_g, qr_g, kr_g, kv_g, dq_cat_t, dckv_n, dkr_rot),
        [_row_spec(r, qw), _row_spec(r, kvr), _row_spec(r, kr_width), gain, gain, gain, _const_spec((1, kvr))],
        [jax.ShapeDtypeStruct((t, qw), BF16), jax.ShapeDtypeStruct((t, kvr), BF16),
         jax.ShapeDtypeStruct((t, kr_width), BF16), jax.ShapeDtypeStruct((1, HEAD), F32),
         jax.ShapeDtypeStruct((1, HEAD), F32), jax.ShapeDtypeStruct((1, HEAD), F32),
         jax.ShapeDtypeStruct((1, kvr), F32)], into=into)


def _kv_prep_fwd(kv, kr_rot, ch, kn_g, name):
    t = kv.shape[0]
    r = _pick(t, (256, 128))
    qw = ch * 2 * HEAD

    def body(kv_ref, kr_ref, kn_ref, ko_ref, kt_ref, vo_ref):
        ones = jnp.ones((r, HEAD), BF16)
        for hh in range(ch):
            lo = hh * 2 * HEAD
            nhat, _ = _rms(kv_ref[:, lo:lo + HEAD], HEAD)
            ko_ref[:, lo:lo + HEAD] = (nhat * kn_ref[...]).astype(BF16)
            ko_ref[:, lo + HEAD:lo + 2 * HEAD] = kr_ref[...]
            vo_ref[:, lo:lo + HEAD] = kv_ref[:, lo + HEAD:lo + 2 * HEAD].astype(BF16)
            vo_ref[:, lo + HEAD:lo + 2 * HEAD] = ones
        kt_ref[...] = ko_ref[...].astype(F32).T.astype(BF16)

    return _row_call(
        body, name, t, r, [_row_spec(r, qw), _row_spec(r, HEAD), _const_spec((1, HEAD))], (kv, kr_rot, kn_g),
        [_row_spec(r, qw), pl.BlockSpec((qw, r), lambda i: (0, i)), _row_spec(r, qw)],
        [jax.ShapeDtypeStruct((t, qw), BF16), jax.ShapeDtypeStruct((qw, t), BF16),
         jax.ShapeDtypeStruct((t, qw), BF16)])


def _kv_prep_bwd(kv, ch, kn_g, dk_cat, dv, name):
    t = kv.shape[0]
    r = _pick(t, (256, 128))
    qw = ch * 2 * HEAD

    def body(kv_ref, kn_ref, dk_ref, dv_ref, dkv_ref, dkr_ref, dkn_ref):
        _zero_at_first_step([dkn_ref])
        dkr = jnp.zeros((r, HEAD), F32)
        for hh in range(ch):
            lo = hh * 2 * HEAD
            nhat, nr = _rms(kv_ref[:, lo:lo + HEAD], HEAD)
            d_n = dk_ref[:, lo:lo + HEAD].astype(F32)
            dkn_ref[...] += _colsum(d_n * nhat)
            dkv_ref[:, lo:lo + HEAD] = _rms_bwd(d_n * kn_ref[...], nhat, nr, HEAD).astype(BF16)
            dkv_ref[:, lo + HEAD:lo + 2 * HEAD] = dv_ref[:, hh * HEAD:(hh + 1) * HEAD]
            dkr = dkr + dk_ref[:, lo + HEAD:lo + 2 * HEAD].astype(F32)
        dkr_ref[...] = dkr

    return _row_call(
        body, name, t, r,
        [_row_spec(r, qw), _const_spec((1, HEAD)), _row_spec(r, qw), _row_spec(r, ch * HEAD)], (kv, kn_g, dk_cat, dv),
        [_row_spec(r, qw), _row_spec(r, HEAD), _const_spec((1, HEAD))],
        [jax.ShapeDtypeStruct((t, qw), BF16), jax.ShapeDtypeStruct((t, HEAD), F32),
         jax.ShapeDtypeStruct((1, HEAD), F32)])


def _attn_tiles(t):
    return _pick(t, (2048, 1024, 512, 256, 128)), _pick(t, (1024, 512, 256, 128))


_NT = (((1,), (1,)), ((), ()))
LOG2E = 1.4426950408889634


def _attn_fwd(q_cat, k_cat, v_aug, ch, scale, name):
    t = q_cat.shape[0]
    tq, tk = _attn_tiles(t)
    nk = t // tk
    c2 = scale * LOG2E

    def body(q_ref, k_ref, v_ref, o_ref, lse_ref, s_scr, m_scr, acc_scr):
        j = pl.program_id(2)

        def scores(slot):
            s_scr[slot] = lax.dot_general(q_ref[...], k_ref[...], _NT, preferred_element_type=F32) * c2

        def absorb(slot):
            s = s_scr[slot]
            m_old = m_scr[...]
            m_new = jnp.maximum(m_old, jnp.max(s, axis=-1, keepdims=True))
            p = jnp.exp2(s - m_new).astype(BF16)
            acc_scr[...] = (jnp.exp2(m_old - m_new) * acc_scr[...]
                            + jnp.dot(p, v_ref[...], preferred_element_type=F32))
            m_scr[...] = m_new

        @pl.when(j == 0)
        def _():
            m_scr[...] = jnp.full(m_scr.shape, -jnp.inf, F32)
            acc_scr[...] = jnp.zeros(acc_scr.shape, F32)
            scores(0)

        for parity in (0, 1):
            @pl.when((j > 0) & (j < nk) & (j % 2 == parity))
            def _():
                scores(parity)
                absorb(1 - parity)

        @pl.when(j == nk)
        def _():
            absorb((nk - 1) % 2)
            acc = acc_scr[...]
            l_sum = acc[:, HEAD:]
            o_ref[...] = (acc[:, :HEAD] / l_sum).astype(BF16)
            lse_ref[0] = m_scr[...] + jnp.log(l_sum[:, 0:1]) * LOG2E

    return pl.pallas_call(
        body, name=name, grid=(ch, t // tq, nk + 1),
        in_specs=[pl.BlockSpec((tq, 2 * HEAD), lambda h, i, j: (i, h)),
                  pl.BlockSpec((tk, 2 * HEAD), lambda h, i, j: (jnp.minimum(j, nk - 1), h)),
                  pl.BlockSpec((tk, 2 * HEAD), lambda h, i, j: (jnp.maximum(j - 1, 0), h))],
        out_specs=[pl.BlockSpec((tq, HEAD), lambda h, i, j: (i, h)),
                   pl.BlockSpec((1, tq, 1), lambda h, i, j: (h, i, 0))],
        out_shape=[jax.ShapeDtypeStruct((t, ch * HEAD), BF16), jax.ShapeDtypeStruct((ch, t, 1), F32)],
        scratch_shapes=[pltpu.VMEM((2, tq, tk), F32), pltpu.VMEM((tq, 1), F32), pltpu.VMEM((tq, 2 * HEAD), F32)],
        compiler_params=_params(("parallel", "parallel", "arbitrary")),
    )(q_cat, k_cat, v_aug)


def _attn_bwd(q_cat, k_cat, k_cat_t, v_aug, do, lse_row, d_row, ch, scale, name):
    t = q_cat.shape[0]
    tk = _pick(t, (1024, 512, 256, 128))
    tq = _pick(t, (1024, 512, 256, 128))
    nk, nq = t // tk, t // tq
    c2 = scale * LOG2E

    def body(q_ref, do_ref, qp_ref, dop_ref, lse_ref, d_ref, k_ref, kt_ref, v_ref,
             dqt_ref, dk_ref, dv_ref, s_scr, dp_scr, dk_scr, dv_scr):
        ki, j = pl.program_id(1), pl.program_id(2)

        def products(slot):
            s_scr[slot] = lax.dot_general(k_ref[...], q_ref[...], _NT, preferred_element_type=F32) * c2
            dp_scr[slot] = lax.dot_general(v_ref[...], do_ref[...], _NT, preferred_element_type=F32)

        def absorb(slot):
            q, do_v = qp_ref[...], dop_ref[...]
            pt = jnp.exp2(s_scr[slot] - lse_ref[0])
            dv_scr[...] += jnp.dot(pt.astype(BF16), do_v, preferred_element_type=F32)
            dst = (pt * (dp_scr[slot] - d_ref[0])).astype(BF16)
            dk_scr[...] += jnp.dot(dst, q, preferred_element_type=F32)
            part = jnp.dot(kt_ref[...], dst, preferred_element_type=F32)
            cols = pl.ds(pl.multiple_of((j - 1) * tq, tq), tq)

            @pl.when(ki == 0)
            def _():
                dqt_ref[:, cols] = part

            @pl.when(ki > 0)
            def _():
                dqt_ref[:, cols] += part

        @pl.when(j == 0)
        def _():
            dk_scr[...] = jnp.zeros(dk_scr.shape, F32)
            dv_scr[...] = jnp.zeros(dv_scr.shape, F32)
            products(0)

        for parity in (0, 1):
            @pl.when((j > 0) & (j < nq) & (j % 2 == parity))
            def _():
                products(parity)
                absorb(1 - parity)

        @pl.when(j == nq)
        def _():
            absorb((nq - 1) % 2)
            dk_ref[...] = (dk_scr[...] * scale).astype(BF16)
            dv_ref[...] = dv_scr[...].astype(BF16)

    def cur(i):
        return jnp.minimum(i, nq - 1)

    def prev(i):
        return jnp.maximum(i - 1, 0)

    stat = pl.BlockSpec((1, 1, tq), lambda h, j, i: (h, 0, prev(i)))
    return pl.pallas_call(
        body, name=name, grid=(ch, nk, nq + 1),
        in_specs=[pl.BlockSpec((tq, 2 * HEAD), lambda h, j, i: (cur(i), h)),
                  pl.BlockSpec((tq, HEAD), lambda h, j, i: (cur(i), h)),
                  pl.BlockSpec((tq, 2 * HEAD), lambda h, j, i: (prev(i), h)),
                  pl.BlockSpec((tq, HEAD), lambda h, j, i: (prev(i), h)), stat, stat,
                  pl.BlockSpec((tk, 2 * HEAD), lambda h, j, i: (j, h)),
                  pl.BlockSpec((2 * HEAD, tk), lambda h, j, i: (h, j)),
                  pl.BlockSpec((tk, HEAD), lambda h, j, i: (j, 2 * h))],
        out_specs=[pl.BlockSpec((2 * HEAD, t), lambda h, j, i: (h, 0)),
                   pl.BlockSpec((tk, 2 * HEAD), lambda h, j, i: (j, h)),
                   pl.BlockSpec((tk, HEAD), lambda h, j, i: (j, h))],
        out_shape=[jax.ShapeDtypeStruct((ch * 2 * HEAD, t), F32), jax.ShapeDtypeStruct((t, ch * 2 * HEAD), BF16),
                   jax.ShapeDtypeStruct((t, ch * HEAD), BF16)],
        scratch_shapes=[pltpu.VMEM((2, tk, tq), F32), pltpu.VMEM((2, tk, tq), F32),
                        pltpu.VMEM((tk, 2 * HEAD), F32), pltpu.VMEM((tk, HEAD), F32)],
        compiler_params=_params(("parallel", "arbitrary", "arbitrary")),
    )(q_cat, do, q_cat, do, lse_row, d_row, k_cat, k_cat_t, v_aug)


def _attn_post_fwd(o, proj, z_off, cw, gc, name, into=None):
    t = o.shape[0]
    r = _pick(t, (256, 128))

    def body(o_ref, z_ref, gc_ref, y_ref):
        sil, _ = _silu_and_grad(z_ref[...].astype(F32))
        yhat, _ = _rms(o_ref[...].astype(F32) * sil, cw)
        y_ref[...] = (yhat * gc_ref[...]).astype(BF16)

    return _row_call(body, name, t, r,
                     [_row_spec(r, cw), _row_spec(r, cw, _col_block(z_off, cw)), _const_spec((1, cw))], (o, proj, gc),
                     _row_spec(r, cw), jax.ShapeDtypeStruct((t, cw), BF16), into=into)


def _attn_post_bwd(o, proj, z_off, cw, gc, dy, dy_col, name, into=None):
    t = o.shape[0]
    ch = cw // HEAD
    r = _pick(t, (256, 128))

    def body(o_ref, z_ref, gc_ref, dy_ref, dz_ref, do_ref, ds_ref, dgc_ref):
        _zero_at_first_step([dgc_ref])
        o_v, z = o_ref[...].astype(F32), z_ref[...].astype(F32)
        sil, dsil = _silu_and_grad(z)
        yhat, rr = _rms(o_v * sil, cw)
        dy_f = dy_ref[...].astype(F32)
        dgc_ref[...] += _colsum(dy_f * yhat)
        dyc = _rms_bwd(dy_f * gc_ref[...], yhat, rr, cw)
        do_b = (dyc * sil).astype(BF16)
        do_ref[...] = do_b
        dz_ref[...] = (dyc * o_v * dsil).astype(BF16)
        prod = do_b.astype(F32) * o_v
        for hh in range(ch):
            ds_ref[hh] = jnp.sum(prod[:, hh * HEAD:(hh + 1) * HEAD], axis=-1, keepdims=True)

    return _row_call(
        body, name, t, r,
        [_row_spec(r, cw), _row_spec(r, cw, _col_block(z_off, cw)), _const_spec((1, cw)), _row_spec(r, cw, dy_col)],
        (o, proj, gc, dy),
        [_row_spec(r, cw), _row_spec(r, cw), pl.BlockSpec((ch, r, 1), lambda i: (0, i, 0)), _const_spec((1, cw))],
        [jax.ShapeDtypeStruct((t, cw), BF16), jax.ShapeDtypeStruct((t, cw), BF16),
         jax.ShapeDtypeStruct((ch, t, 1), F32), jax.ShapeDtypeStruct((1, cw), F32)], into=into)


def _ple_fwd(h1, gpre, pp, name):
    t, d = h1.shape
    r = _pick(t, (256, 128))

    def body(h_ref, g_ref, p_ref, o_ref):
        o_ref[...] = h_ref[...] + _sigmoid(g_ref[...]) * p_ref[...]

    return _row_call(body, name, t, r, [_row_spec(r, d)] * 3, (h1, gpre, pp), _row_spec(r, d),
                     jax.ShapeDtypeStruct((t, d), F32))


def _ple_bwd(gpre, pp, dh, name):
    t, d = dh.shape
    r = _pick(t, (256, 128))

    def body(g_ref, p_ref, dh_ref, dg_ref, dp_ref):
        sig = _sigmoid(g_ref[...])
        dh_v = dh_ref[...]
        dg_ref[...] = (dh_v * p_ref[...] * sig * (1.0 - sig)).astype(BF16)
        dp_ref[...] = (dh_v * sig).astype(BF16)

    return _row_call(body, name, t, r, [_row_spec(r, d)] * 3, (gpre, pp, dh), [_row_spec(r, d)] * 2,
                     [jax.ShapeDtypeStruct((t, d), BF16)] * 2)


def _loss_and_grad(h, target, name):
    t, d = h.shape
    r = _pick(t, (256, 128))

    def body(h_ref, t_ref, l_ref, dh_ref):
        _zero_at_first_step([l_ref])
        err = h_ref[...] - t_ref[...]
        l_ref[...] += jnp.sum(jnp.sum(err * err, axis=-1, keepdims=True), axis=0, keepdims=True) * (0.5 / d)
        dh_ref[...] = err * (1.0 / d)

    return _row_call(body, name, t, r, [_row_spec(r, d)] * 2, (h, target), [_const_spec((1, 1)), _row_spec(r, d)],
                     [jax.ShapeDtypeStruct((1, 1), F32), jax.ShapeDtypeStruct((t, d), F32)])


def _ew_rows(rows, cols):
    cap = min(1024, max(8, (1 << 19) // max(cols, 1)))
    for cand in range(cap - cap % 8, 7, -8):
        if rows % cand == 0:
            return cand
    return rows


def _pair_sum_bf16(both, core_flag, b, name):
    _, n, rows, cols = both.shape
    rb = _ew_rows(rows, cols)

    def body(a_ref, flag_ref, b_ref, o_ref):
        mine = jnp.where(flag_ref[...] == 0.0, a_ref[0, 0], a_ref[1, 0])
        o_ref[0] = (mine + b_ref[0]).astype(BF16)

    spec = pl.BlockSpec((1, rb, cols), lambda s, i: (s, i, 0))
    return pl.pallas_call(
        body, name=name, grid=(n, rows // rb),
        in_specs=[pl.BlockSpec((2, 1, rb, cols), lambda s, i: (0, s, i, 0)),
                  pl.BlockSpec((1, 1), lambda s, i: (0, 0)), spec],
        out_specs=spec, out_shape=jax.ShapeDtypeStruct(b.shape, BF16),
        compiler_params=_params(("parallel", "parallel")))(both, core_flag, b)


def _shard_sum(a, b, recv, name):
    rows, cols = a.shape
    rb = _ew_rows(rows, cols)

    def body(a_ref, b_ref, r_ref, o_ref):
        o_ref[...] = ((a_ref[...] + b_ref[...]) + r_ref[0].astype(F32) + r_ref[1].astype(F32)
                      + r_ref[2].astype(F32))

    spec = pl.BlockSpec((rb, cols), lambda i: (i, 0))
    return pl.pallas_call(body, name=name, grid=(rows // rb,),
                          in_specs=[spec, spec, pl.BlockSpec((N_SHARD - 1, rb, cols), lambda i: (0, i, 0))],
                          out_specs=spec, out_shape=jax.ShapeDtypeStruct(a.shape, F32),
                          compiler_params=_params(("parallel",)))(a, b, recv)


def _sum_devices(g, name):
    n, rows, cols = g.shape
    rb = _ew_rows(rows, cols)

    def body(g_ref, o_ref):
        acc = g_ref[0]
        for k in range(1, n):
            acc = acc + g_ref[k]
        o_ref[...] = acc

    return pl.pallas_call(body, name=name, grid=(rows // rb,),
                          in_specs=[pl.BlockSpec((n, rb, cols), lambda i: (0, i, 0))],
                          out_specs=pl.BlockSpec((rb, cols), lambda i: (i, 0)),
                          out_shape=jax.ShapeDtypeStruct((rows, cols), F32),
                          compiler_params=_params(("parallel",)))(g)


def _adamw_update(w, g_v, m, v):
    m_new = ADAM_B1 * m + (1.0 - ADAM_B1) * g_v
    v_new = ADAM_B2 * v + (1.0 - ADAM_B2) * (g_v * g_v)
    m_hat = m_new / (1.0 - ADAM_B1 ** ADAM_STEP)
    v_hat = v_new / (1.0 - ADAM_B2 ** ADAM_STEP)
    return -ADAM_LR * (m_hat / (jnp.sqrt(v_hat) + ADAM_EPS) + ADAM_WD * w), m_new, v_new


def _adamw(w, g, m, v, name):
    rows, cols = w.shape
    rb = _ew_rows(rows, cols)

    def body(w_ref, g_ref, m_ref, v_ref, d_ref, mo_ref, vo_ref):
        d_ref[...], mo_ref[...], vo_ref[...] = _adamw_update(w_ref[...], g_ref[...], m_ref[...], v_ref[...])

    spec = pl.BlockSpec((rb, cols), lambda i: (i, 0))
    return pl.pallas_call(body, name=name, grid=(rows // rb,), in_specs=[spec] * 4, out_specs=[spec] * 3,
                          out_shape=[jax.ShapeDtypeStruct(w.shape, F32)] * 3,
                          compiler_params=_params(("parallel",)))(w, g, m, v)


def _adamw_two_halves(w, own, recv, core_flag, m, v, name):
    depth, rows, cols = w.shape
    assert rows % 2 == 0 and own.shape == (depth, rows // 2, cols)
    rb = _ew_rows(rows // 2, cols)
    nb = rows // 2 // rb

    def body(w_ref, own_ref, recv_ref, flag_ref, m_ref, v_ref, g_ref, d_ref, mo_ref, vo_ref):
        half = pl.program_id(1).astype(F32)
        g_v = jnp.where(flag_ref[...] == half, own_ref[...], recv_ref[...])
        g_ref[...] = g_v
        d_ref[...], mo_ref[...], vo_ref[...] = _adamw_update(w_ref[...], g_v, m_ref[...], v_ref[...])

    full = pl.BlockSpec((None, rb, cols), lambda l, k, i: (l, k * nb + i, 0))
    half_spec = pl.BlockSpec((None, rb, cols), lambda l, k, i: (l, i, 0))
    return pl.pallas_call(
        body, name=name, grid=(depth, 2, nb),
        in_specs=[full, half_spec, half_spec, pl.BlockSpec((1, 1), lambda l, k, i: (0, 0)), full, full],
        out_specs=[full] * 4, out_shape=[jax.ShapeDtypeStruct(w.shape, F32)] * 4,
        compiler_params=_params(("parallel", "parallel", "parallel")))(w, own, recv, core_flag, m, v)


def _place():
    return lax.axis_index("x"), lax.axis_index("y"), lax.axis_index("c")


def _other_chips(x, y):
    return [(1 - x, y), (x, 1 - y), (1 - x, 1 - y)]


_ANY = pl.BlockSpec(memory_space=pl.ANY)


class _Rider:
    def __init__(self, arrays, out_shapes, n_sems, start, finish, in_place=False):
        self.arrays, self.out_shapes, self.n_sems = list(arrays), list(out_shapes), n_sems
        self.start, self.finish, self.in_place = start, finish, in_place

    def sems(self):
        return [pltpu.SemaphoreType.DMA((self.n_sems,)), pltpu.SemaphoreType.DMA((self.n_sems,))]

    def aliases(self, first_in, first_out):
        return {first_in + a: first_out + a for a in range(len(self.arrays))} if self.in_place else {}


def _comm_call(rider, name):
    n_in, n_out = len(rider.arrays), len(rider.out_shapes)

    def body(*refs):
        ins, outs = refs[:n_in], refs[n_in:n_in + n_out]
        send_sems, recv_sems = refs[n_in + n_out:]
        rider.start(ins, outs, send_sems, recv_sems)
        rider.finish(ins, outs, send_sems, recv_sems)

    return pl.pallas_call(
        body, name=name, in_specs=[_ANY] * n_in, out_specs=[_ANY] * n_out, out_shape=rider.out_shapes,
        scratch_shapes=rider.sems(), input_output_aliases=rider.aliases(0, 0))(*rider.arrays)


def _half(shape, which):
    for axis, size in enumerate(shape):
        if size % 2 == 0:
            return (slice(None),) * axis + (pl.ds(which * (size // 2), size // 2),)
    raise ValueError(f"no axis of even length in {shape}")


def _start_then_wait(copies):
    def start(*refs):
        for send, _ in copies(*refs):
            send.start()

    def finish(*refs):
        pairs = copies(*refs)
        for _, landing in pairs:
            landing.wait_recv()
        for send, _ in pairs:
            send.wait_send()

    return start, finish


def _fetch_rider(shards):
    n, n_peer = len(shards), N_SHARD - 1
    shapes = [s.shape for s in shards]

    def copies(ins, outs, send_sems, recv_sems):
        x, y, c = _place()
        pairs = []
        for a in range(n):
            mine = _half(shapes[a], c)
            for k, (px, py) in enumerate(_other_chips(x, y)):
                def into(slot):
                    return pltpu.make_async_remote_copy(
                        src_ref=ins[a].at[mine], dst_ref=outs[a].at[(slot,) + mine],
                        send_sem=send_sems.at[a * n_peer + k], recv_sem=recv_sems.at[a * n_peer + k],
                        device_id=(px, py, c), device_id_type=MESH)
                pairs.append((into(2 * x + y), into(2 * px + py)))
        return pairs

    start, finish = _start_then_wait(copies)
    return _Rider(shards, [jax.ShapeDtypeStruct((N_SHARD,) + s.shape, s.dtype) for s in shards], n * n_peer,
                  start, finish)


def _forward_rider(gathered, shards):
    n, n_peer = len(gathered), N_SHARD - 1
    shapes = [s.shape for s in shards]

    def copies(ins, outs, send_sems, recv_sems):
        x, y, c = _place()
        pairs = []
        for a in range(n):
            for k, (px, py) in enumerate(_other_chips(x, y)):
                def half_of_slot(which):
                    rows = outs[a].at[(2 * px + py,) + _half(shapes[a], which)]
                    return pltpu.make_async_remote_copy(
                        src_ref=rows, dst_ref=rows, send_sem=send_sems.at[a * n_peer + k],
                        recv_sem=recv_sems.at[a * n_peer + k], device_id=(x, y, 1 - c), device_id_type=MESH)
                pairs.append((half_of_slot(c), half_of_slot(1 - c)))
        return pairs

    start, finish = _start_then_wait(copies)
    return _Rider(gathered, [jax.ShapeDtypeStruct(g.shape, g.dtype) for g in gathered], n * n_peer, start, finish,
                  in_place=True)


def _sibling_rider(arrs, other_half):
    n = len(arrs)

    def copies(ins, outs, send_sems, recv_sems):
        x, y, c = _place()
        pairs = []
        for a in range(n):
            cp = pltpu.make_async_remote_copy(
                src_ref=ins[a].at[1 - c] if other_half else ins[a], dst_ref=outs[a], send_sem=send_sems.at[a],
                recv_sem=recv_sems.at[a], device_id=(x, y, 1 - c), device_id_type=MESH)
            pairs.append((cp, cp))
        return pairs

    start, finish = _start_then_wait(copies)
    return _Rider(arrs, [jax.ShapeDtypeStruct(g.shape[1:] if other_half else g.shape, g.dtype) for g in arrs], n,
                  start, finish)


def _owner_rider(parts):
    n, n_peer = len(parts), N_SHARD - 1

    def copies(ins, outs, send_sems, recv_sems):
        x, y, c = _place()
        pairs = []
        for a in range(n):
            for k, (px, py) in enumerate(_other_chips(x, y)):
                cp = pltpu.make_async_remote_copy(
                    src_ref=ins[a].at[2 * px + py], dst_ref=outs[a].at[k], send_sem=send_sems.at[a * n_peer + k],
                    recv_sem=recv_sems.at[a * n_peer + k], device_id=(px, py, c), device_id_type=MESH)
                pairs.append((cp, cp))
        return pairs

    start, finish = _start_then_wait(copies)
    return _Rider(parts, [jax.ShapeDtypeStruct((n_peer,) + p.shape[1:], p.dtype) for p in parts], n * n_peer,
                  start, finish)


def _gather_devices(buf, name):
    n_peer = N_DEV - 1

    def body(in_ref, out_ref, send_sems, recv_sems, local_sem):
        x, y, c = _place()
        me = 4 * x + 2 * y + c
        mine = pltpu.make_async_copy(in_ref, out_ref.at[me], local_sem)
        mine.start()
        peers = []
        for k in range(1, N_DEV):
            fx, fy, fc = (k >> 2) & 1, (k >> 1) & 1, k & 1
            peers.append((x ^ fx, y ^ fy, c ^ fc))
        sends = []
        for k, peer in enumerate(peers):
            cp = pltpu.make_async_remote_copy(
                src_ref=in_ref, dst_ref=out_ref.at[me], send_sem=send_sems.at[k], recv_sem=recv_sems.at[k],
                device_id=peer, device_id_type=MESH)
            cp.start()
            sends.append(cp)
        for k, (px, py, pc) in enumerate(peers):
            pltpu.make_async_remote_copy(
                src_ref=in_ref, dst_ref=out_ref.at[4 * px + 2 * py + pc], send_sem=send_sems.at[k],
                recv_sem=recv_sems.at[k], device_id=(px, py, pc), device_id_type=MESH).wait_recv()
        for cp in sends:
            cp.wait_send()
        mine.wait()

    return pl.pallas_call(
        body, name=name, in_specs=[_ANY], out_specs=_ANY,
        out_shape=jax.ShapeDtypeStruct((N_DEV,) + buf.shape, buf.dtype),
        scratch_shapes=[pltpu.SemaphoreType.DMA((n_peer,)), pltpu.SemaphoreType.DMA((n_peer,)),
                        pltpu.SemaphoreType.DMA(())],
    )(buf)


class _Dims:
    def __init__(self, x, p, w_in, sgu_norm, conv_w, kv_norm, w_ukv, w_out):
        self.t, self.d = x.shape[1], x.shape[2]
        self.depth = w_in.shape[0]
        self.ple = p.shape[3]
        self.in_w = w_in.shape[2] * N_SHARD
        self.ah = sgu_norm.shape[1]
        self.aw = self.ah * HEAD
        self.bw = conv_w.shape[2] * N_SHARD
        self.kvr = kv_norm.shape[1]
        self.ch = w_ukv.shape[2] * N_SHARD // (2 * HEAD)
        self.cw = self.ch * HEAD
        self.mix = w_out.shape[1] * N_SHARD
        assert self.mix == self.aw + self.bw + self.cw and self.aw == self.bw
        self.qw = self.ch * 2 * HEAD
        segs = [('a', 3 * self.aw, 3 * self.aw), ('ckv', self.kvr, self.kvr), ('b', 4 * self.bw, 4 * self.bw),
                ('q', self.qw, self.qw), ('cz', self.cw, self.cw), ('kr', HEAD, HEAD)]
        off = 0
        self.off = {}
        for nm, width, align in segs:
            off = -(-off // align) * align
            self.off[nm] = off
            off += width
        self.inp = -(-off // 512) * 512
        q_real = self.ch * (HEAD + ROPE)
        widths = [3 * self.aw, 4 * self.bw, q_real, self.kvr, ROPE, self.cw]
        assert sum(widths) == self.in_w
        starts = [0]
        for wd in widths:
            starts.append(starts[-1] + wd)
        self.src = dict(zip(['a', 'b', 'q', 'ckv', 'kr', 'cz'], zip(starts[:-1], widths)))


def _rearrange_w_in(w, dm):
    lead, d = w.shape[:-2], w.shape[-1]
    axis = w.ndim - 2

    def rows(nm):
        s, wd = dm.src[nm]
        return lax.slice_in_dim(w, s, s + wd, axis=axis)

    pieces = {nm: rows(nm) for nm in ('a', 'b', 'ckv', 'cz')}
    q = rows('q').reshape(lead + (dm.ch, HEAD + ROPE, d))
    pieces['q'] = jnp.pad(q, [(0, 0)] * (len(lead) + 1) + [(0, HEAD - ROPE), (0, 0)]).reshape(lead + (dm.qw, d))
    pieces['kr'] = jnp.pad(rows('kr'), [(0, 0)] * len(lead) + [(0, HEAD - ROPE), (0, 0)])
    out, cur = [], 0
    for nm in sorted(dm.off, key=lambda k: dm.off[k]):
        if dm.off[nm] > cur:
            out.append(jnp.zeros(lead + (dm.off[nm] - cur, d), w.dtype))
        out.append(pieces[nm])
        cur = dm.off[nm] + pieces[nm].shape[axis]
    if dm.inp > cur:
        out.append(jnp.zeros(lead + (dm.inp - cur, d), w.dtype))
    return jnp.concatenate(out, axis=axis)


def _unarrange_w_in(g, dm):
    d = g.shape[1]

    def seg(nm, width):
        return g[dm.off[nm]:dm.off[nm] + width]

    q = seg('q', dm.qw).reshape(dm.ch, 2 * HEAD, d)[:, :HEAD + ROPE].reshape(dm.ch * (HEAD + ROPE), d)
    return jnp.concatenate([seg('a', 3 * dm.aw), seg('b', 4 * dm.bw), q, seg('ckv', dm.kvr), seg('kr', ROPE),
                            seg('cz', dm.cw)], axis=0)


def _finish_dproj(dproj, parts, widths, dm):
    t = dproj.shape[0]
    for nm, part in parts.items():
        dproj = lax.dynamic_update_slice(dproj, part, (0, dm.off[nm]))
    cur = 0
    for nm in sorted(dm.off, key=lambda k: dm.off[k]):
        if dm.off[nm] > cur:
            dproj = lax.dynamic_update_slice(dproj, jnp.zeros((t, dm.off[nm] - cur), BF16), (0, cur))
        cur = dm.off[nm] + widths[nm]
    assert cur == dm.inp
    return dproj


def _rope_tables(positions):
    inv = 1.0 / (ROPE_BASE ** (jnp.arange(0, ROPE, 2, dtype=F32) / ROPE))
    ang = positions.astype(F32)[:, None] * inv
    cos, sin = jnp.cos(ang), jnp.sin(ang)
    t = positions.shape[0]
    half = ROPE // 2
    cos_t = jnp.concatenate([cos, cos, jnp.zeros((t, HEAD - ROPE), F32)], axis=-1)
    sin_a = jnp.concatenate([-sin, jnp.zeros((t, HEAD - half), F32)], axis=-1)
    sin_b = jnp.concatenate([jnp.zeros((t, half), F32), sin, jnp.zeros((t, HEAD - ROPE), F32)], axis=-1)
    return cos_t, sin_a, sin_b


def _pad_gain(g):
    return jnp.pad(g, (0, HEAD - g.shape[0]))[None, :]


def _shard_major(g, axis):
    shape = g.shape
    g = g.reshape(shape[:axis] + (N_SHARD, shape[axis] // N_SHARD) + shape[axis + 1:])
    g = jnp.moveaxis(g, axis, 0)
    rows, cols = g.shape[1], g.shape[2]
    return jnp.swapaxes(g.reshape(N_SHARD, 2, rows // 2, cols), 0, 1)


def _pack(arrs):
    flat = jnp.concatenate([a.reshape(-1) for a in arrs])
    pad = (-flat.shape[0]) % (8 * HEAD)
    return jnp.pad(flat, (0, pad)).reshape(-1, HEAD)


def _unpack(buf, shapes):
    flat = buf.reshape(-1)
    out, cur = [], 0
    for s in shapes:
        size = 1
        for v in s:
            size *= v
        out.append(flat[cur:cur + size].reshape(s))
        cur += size
    return out


def kernel(x, p, positions, attn_norm, w_in, sgu_norm, w_spatial, b_spatial, conv_w, conv_b, kv_norm, w_ukv, q_nope_norm, q_rope_norm, k_nope_norm, k_rope_norm, out_norm, w_out, ple_norm, w_ple_gate, w_ple_proj, loss_target, m_attn_norm, m_w_in, m_sgu_norm, m_w_spatial, m_b_spatial, m_conv_w, m_conv_b, m_kv_norm, m_w_ukv, m_q_nope_norm, m_q_rope_norm, m_k_nope_norm, m_k_rope_norm, m_out_norm, m_w_out, m_ple_norm, m_w_ple_gate, m_w_ple_proj, v_attn_norm, v_w_in, v_sgu_norm, v_w_spatial, v_b_spatial, v_conv_w, v_conv_b, v_kv_norm, v_w_ukv, v_q_nope_norm, v_q_rope_norm, v_k_nope_norm, v_k_rope_norm, v_out_norm, v_w_out, v_ple_norm, v_w_ple_gate, v_w_ple_proj):
    weights = dict(attn_norm=attn_norm, w_in=w_in, sgu_norm=sgu_norm, w_spatial=w_spatial, b_spatial=b_spatial,
                   conv_w=conv_w, conv_b=conv_b, kv_norm=kv_norm, w_ukv=w_ukv, q_nope_norm=q_nope_norm,
                   q_rope_norm=q_rope_norm, k_nope_norm=k_nope_norm, k_rope_norm=k_rope_norm, out_norm=out_norm,
                   w_out=w_out, ple_norm=ple_norm, w_ple_gate=w_ple_gate, w_ple_proj=w_ple_proj)
    mom_m = dict(attn_norm=m_attn_norm, w_in=m_w_in, sgu_norm=m_sgu_norm, w_spatial=m_w_spatial,
                 b_spatial=m_b_spatial, conv_w=m_conv_w, conv_b=m_conv_b, kv_norm=m_kv_norm, w_ukv=m_w_ukv,
                 q_nope_norm=m_q_nope_norm, q_rope_norm=m_q_rope_norm, k_nope_norm=m_k_nope_norm,
                 k_rope_norm=m_k_rope_norm, out_norm=m_out_norm, w_out=m_w_out, ple_norm=m_ple_norm,
                 w_ple_gate=m_w_ple_gate, w_ple_proj=m_w_ple_proj)
    mom_v = dict(attn_norm=v_attn_norm, w_in=v_w_in, sgu_norm=v_sgu_norm, w_spatial=v_w_spatial,
                 b_spatial=v_b_spatial, conv_w=v_conv_w, conv_b=v_conv_b, kv_norm=v_kv_norm, w_ukv=v_w_ukv,
                 q_nope_norm=v_q_nope_norm, q_rope_norm=v_q_rope_norm, k_nope_norm=v_k_nope_norm,
                 k_rope_norm=v_k_rope_norm, out_norm=v_out_norm, w_out=v_w_out, ple_norm=v_ple_norm,
                 w_ple_gate=v_w_ple_gate, w_ple_proj=v_w_ple_proj)
    dm = _Dims(x, p, w_in, sgu_norm, conv_w, kv_norm, w_ukv, w_out)
    for group in (weights, mom_m, mom_v):
        group['w_in'] = jnp.swapaxes(group['w_in'], 1, 2)
    t, d, depth = dm.t, dm.d, dm.depth
    shard = 2 * lax.axis_index("x") + lax.axis_index("y")
    core = lax.axis_index("c")
    scale = float(HEAD + ROPE) ** -0.5

    def local_layer(i):
        return [weights[n][i:i + 1].astype(BF16) for n in BIG]

    def fill_own_slot(gathered, local):
        return [lax.dynamic_update_slice(g, mine[None], (shard,) + (0,) * mine.ndim)
                for g, mine in zip(gathered, local)]

    def layer_weights(filled):
        w = {n: jnp.concatenate([filled[j][s] for s in range(N_SHARD)], axis=BIG_AXIS[n])
             for j, n in enumerate(BIG)}
        w['w_in'] = _rearrange_w_in(w['w_in'], dm)
        return w

    first = local_layer(0) + [conv_w]
    fetched = _comm_call(_fetch_rider(first), "fetch_weights_l0")
    filled = fill_own_slot(_comm_call(_forward_rider(fetched, first), "forward_weights_l0"), first)
    layer_w = [layer_weights(filled)] + [None] * (depth - 1)
    conv_w_full = jnp.concatenate([filled[len(BIG)][s] for s in range(N_SHARD)], axis=2)

    tabs = _rope_tables(positions[0])
    h = x[0]
    saved = []
    for i in range(depth):
        tag = f"l{i}_"
        ga, gb, gc = (out_norm[i][None, :dm.aw], out_norm[i][None, dm.aw:dm.aw + dm.bw],
                      out_norm[i][None, dm.aw + dm.bw:])
        ws_b = w_spatial[i].astype(BF16)
        bb = jnp.broadcast_to(b_spatial[i][:, :, None], (dm.ah, HEAD, HEAD))
        qn_g, qr_g = q_nope_norm[i][None, :], _pad_gain(q_rope_norm[i])
        kn_g, kr_g = k_nope_norm[i][None, :], _pad_gain(k_rope_norm[i])
        kv_g = kv_norm[i][None, :]
        wl = layer_w[i]
        nxt = local_layer(i + 1) if i + 1 < depth else None
        hn = _norm_fwd(h, attn_norm[i][None, :], tag + "norm1")
        if nxt is None:
            proj = _matmul(hn, wl['w_in'], 'nt', BF16, tag + "proj", b_layer=0)
        else:
            proj, fetched = _matmul(hn, wl['w_in'], 'nt', BF16, tag + "proj", b_layer=0, rider=_fetch_rider(nxt))
        y = _sgu_fwd(proj, dm.off['a'], dm.aw, sgu_norm[i], ws_b, bb, ga, tag + "sgu",
                     into=(jnp.zeros((t, dm.mix), BF16), 0))
        y, yconv = _conv_fwd(proj, dm.off['b'], dm.bw, conv_w_full[i], conv_b[i][None, :], gb, tag + "conv",
                             into=(y, _col_block(dm.aw, dm.bw)))
        q_cat, ckv_n, kr_rot = _mla_prep_fwd(proj, dm.off['q'], dm.off['ckv'], dm.off['kr'], dm.ch, dm.kvr, tabs,
                                             qn_g, qr_g, kr_g, kv_g, tag + "mla_prep")
        kv = _matmul(ckv_n, wl['w_ukv'], 'nn', F32, tag + "kv_up", b_layer=0)
        k_cat, k_cat_t, v_aug = _kv_prep_fwd(kv, kr_rot, dm.ch, kn_g, tag + "kv_prep")
        o, lse = _attn_fwd(q_cat, k_cat, v_aug, dm.ch, scale, tag + "attn")
        y = _attn_post_fwd(o, proj, dm.off['cz'], dm.cw, gc, tag + "attn_post",
                           into=(y, _col_block(dm.aw + dm.bw, dm.cw)))
        if nxt is None:
            h1 = _matmul(y, wl['w_out'], 'nn', F32, tag + "out", add=h, b_layer=0)
        else:
            h1, gathered = _matmul(y, wl['w_out'], 'nn', F32, tag + "out", add=h, b_layer=0,
                                   rider=_forward_rider(list(fetched), nxt))
            layer_w[i + 1] = layer_weights(fill_own_slot(gathered, nxt))
        hn2 = _norm_fwd(h1, ple_norm[i][None, :], tag + "norm2")
        gpre = _matmul(hn2, wl['w_ple_gate'], 'nn', F32, tag + "gate", b_layer=0)
        p_b = p[i, 0].astype(BF16)
        pp = _matmul(p_b, wl['w_ple_proj'], 'nn', F32, tag + "ple_proj", b_layer=0)
        h2 = _ple_fwd(h1, gpre, pp, tag + "ple")
        saved.append(dict(h=h, hn=hn, proj=proj, yconv=yconv, q_cat=q_cat, ckv_n=ckv_n, kv=kv, k_cat=k_cat,
                          k_cat_t=k_cat_t,
                          v=v_aug, o=o, lse=lse, y=y, h1=h1, hn2=hn2, gpre=gpre, pp=pp, p_b=p_b, ws_b=ws_b, bb=bb,
                          gains=(ga, gb, gc, qn_g, qr_g, kn_g, kr_g, kv_g)))
        h = h2

    loss_part, dh = _loss_and_grad(h, loss_target[0], "loss")
    loss = lax.psum(loss_part[0, 0], ("x", "y", "c"))

    core_flag = core.astype(F32).reshape(1, 1)

    def chip_sums(sm, from_sibling, tag):
        return sm, [_pair_sum_bf16(a, core_flag, b, f"{tag}chip_sum_{n}") for a, b, n in zip(sm, from_sibling, BIG)]

    def shard_sums(sm, from_sibling, from_chips, tag):
        out = []
        for a, b, r3, n in zip(sm, from_sibling, from_chips, BIG):
            own_a = lax.dynamic_slice(a, (core, shard, 0, 0), (1, 1) + a.shape[2:]).reshape(a.shape[2:])
            own_b = lax.dynamic_index_in_dim(b, shard, 0, keepdims=False)
            out.append(_shard_sum(own_a, own_b, r3, f"{tag}shard_sum_{n}"))
        return out

    grads = {n: [None] * depth for n in WEIGHTS}
    own_half = {n: [None] * depth for n in BIG}
    sibling_half = {n: [None] * depth for n in BIG}
    carry = None
    for i in reversed(range(depth)):
        tag = f"l{i}_b_"
        gtag = f"l{i + 1}_g_"
        sv = saved[i]
        wl = layer_w[i]
        ga, gb, gc, qn_g, qr_g, kn_g, kr_g, kv_g = sv['gains']
        proj = sv['proj']
        dgpre, dpp = _ple_bwd(sv['gpre'], sv['pp'], dh, tag + "ple")
        grads['w_ple_proj'][i] = _matmul(sv['p_b'], dpp, 'tn', F32, tag + "d_w_ple_proj")
        if carry is None:
            grads['w_ple_gate'][i] = _matmul(sv['hn2'], dgpre, 'tn', F32, tag + "d_w_gate")
        else:
            grads['w_ple_gate'][i], from_sibling = _matmul(sv['hn2'], dgpre, 'tn', F32, tag + "d_w_gate",
                                                           rider=_sibling_rider(carry, True))
            mine, sums = chip_sums(carry, from_sibling, gtag)
        d_hn2 = _matmul(dgpre, wl['w_ple_gate'], 'nt', BF16, tag + "d_hn2", b_layer=0)
        dh1, dh1_b, g_ple = _norm_bwd(sv['h1'], ple_norm[i][None, :], d_hn2, dh, tag + "norm2")
        grads['ple_norm'][i] = g_ple[0]
        grads['w_out'][i] = _matmul(sv['y'], dh1_b, 'tn', F32, tag + "d_w_out")
        dy = _matmul(dh1_b, wl['w_out'], 'nt', BF16, tag + "d_y", b_layer=0)
        ws_t = jnp.swapaxes(sv['ws_b'], 1, 2)
        dproj, g_sgu, g_ws, g_bs, g_ga = _sgu_bwd(proj, dm.off['a'], dm.aw, sgu_norm[i], sv['ws_b'], ws_t, sv['bb'],
                                                  ga, dy, dm.inp, tag + "sgu")
        grads['sgu_norm'][i], grads['w_spatial'][i], grads['b_spatial'][i] = g_sgu, g_ws, g_bs[:, :, 0]
        dyc, d_bb, d_bz, g_gb, g_cb = _conv_bwd_gate(proj, dm.off['b'], dm.bw, sv['yconv'], gb, dy, tag + "conv_gate")
        dproj, g_cw = _conv_bwd_taps(proj, dm.off['b'], dm.bw, dyc, conv_w_full[i], d_bb, d_bz, tag + "conv_taps",
                                     into=(dproj, _col_block(dm.off['b'], 4 * dm.bw)))
        grads['conv_b'][i], grads['conv_w'][i] = g_cb[0], g_cw
        dproj, d_o, dsum, g_gc = _attn_post_bwd(sv['o'], proj, dm.off['cz'], dm.cw, gc, dy,
                                                _col_block(dm.aw + dm.bw, dm.cw), tag + "attn_post",
                                                into=(dproj, _col_block(dm.off['cz'], dm.cw)))
        grads['out_norm'][i] = jnp.concatenate([g_ga[0], g_gb[0], g_gc[0]])
        dq_t, dk_cat, dv = _attn_bwd(sv['q_cat'], sv['k_cat'], sv['k_cat_t'], sv['v'], d_o,
                                     sv['lse'].reshape(dm.ch, 1, t), dsum.reshape(dm.ch, 1, t), dm.ch, scale,
                                     tag + "attn_bwd")
        dkv, dkr_rot, g_kn = _kv_prep_bwd(sv['kv'], dm.ch, kn_g, dk_cat, dv, tag + "kv_prep")
        grads['k_nope_norm'][i] = g_kn[0]
        grads['w_ukv'][i] = _matmul(sv['ckv_n'], dkv, 'tn', F32, tag + "d_w_ukv")
        dckv_n = _matmul(dkv, wl['w_ukv'], 'nt', BF16, tag + "d_ckv", b_layer=0)
        kr_width = dm.inp - dm.off['kr']
        dproj, d_ckv, d_kr, g_qn, g_qr, g_kr, g_kv = _mla_prep_bwd(
            proj, dm.off['q'], dm.off['ckv'], dm.off['kr'], dm.ch, dm.kvr, tabs, qn_g, qr_g, kr_g, kv_g,
            dq_t, scale, dckv_n, dkr_rot, kr_width, tag + "mla_prep", into=(dproj, _col_block(dm.off['q'], dm.qw)))
        grads['q_nope_norm'][i], grads['q_rope_norm'][i] = g_qn[0], g_qr[0, :ROPE]
        grads['k_rope_norm'][i], grads['kv_norm'][i] = g_kr[0, :ROPE], g_kv[0]
        dproj = _finish_dproj(dproj, dict(ckv=d_ckv, kr=d_kr),
                              dict(a=3 * dm.aw, b=4 * dm.bw, ckv=dm.kvr, q=dm.qw, cz=dm.cw, kr=kr_width), dm)
        if carry is None:
            d_w_in = _matmul(dproj, sv['hn'], 'tn', F32, tag + "d_w_in")
            d_hn = _matmul(dproj, wl['w_in'], 'nn', BF16, tag + "d_hn", b_layer=0)
        else:
            d_w_in, from_chips = _matmul(dproj, sv['hn'], 'tn', F32, tag + "d_w_in", rider=_owner_rider(sums))
            halves = shard_sums(mine, from_sibling, from_chips, gtag)
            d_hn, from_core = _matmul(dproj, wl['w_in'], 'nn', BF16, tag + "d_hn", b_layer=0,
                                      rider=_sibling_rider(halves, False))
            for n, own, recv in zip(BIG, halves, from_core):
                own_half[n][i + 1], sibling_half[n][i + 1] = own, recv
        grads['w_in'][i] = _unarrange_w_in(d_w_in, dm)
        dh, _, g_an = _norm_bwd(sv['h'], attn_norm[i][None, :], d_hn, dh1, tag + "norm1")
        grads['attn_norm'][i] = g_an[0]
        carry = [_shard_major(grads[n][i], BIG_AXIS[n] - 1) for n in BIG]
    grad_x = dh[None]

    from_sibling = _comm_call(_sibling_rider(carry, True), "l0_g_to_sibling")
    mine, sums = chip_sums(carry, from_sibling, "l0_g_")
    from_chips = _comm_call(_owner_rider(sums), "l0_g_to_owner_chips")
    halves = shard_sums(mine, from_sibling, from_chips, "l0_g_")
    from_core = _comm_call(_sibling_rider(halves, False), "l0_g_share_sibling")
    for n, own, recv in zip(BIG, halves, from_core):
        own_half[n][0], sibling_half[n][0] = own, recv

    out_g, out_d, out_m, out_v = {}, {}, {}, {}
    for n in BIG:
        out_g[n], out_d[n], out_m[n], out_v[n] = _adamw_two_halves(
            weights[n], jnp.stack(own_half[n]), jnp.stack(sibling_half[n]), core_flag, mom_m[n], mom_v[n],
            f"adamw_{n}")
    for out in (out_g, out_d, out_m, out_v):
        out['w_in'] = jnp.swapaxes(out['w_in'], 1, 2)
    grads = {n: jnp.stack(grads[n]) for n in SMALL}

    shapes = [grads[n].shape for n in SMALL]
    summed = _unpack(_sum_devices(_gather_devices(_pack([grads[n] for n in SMALL]), "gather_small_grads"),
                                  "sum_small_grads"), shapes)
    small_g = dict(zip(SMALL, summed))
    small_g['conv_w'] = lax.dynamic_slice_in_dim(small_g['conv_w'], shard * conv_w.shape[2], conv_w.shape[2], axis=2)
    local_shapes = [weights[n].shape for n in SMALL]
    d_s, m_s, v_s = _adamw(_pack([weights[n] for n in SMALL]), _pack([small_g[n] for n in SMALL]),
                           _pack([mom_m[n] for n in SMALL]), _pack([mom_v[n] for n in SMALL]), "adamw_small")
    for n, dd, mm, vv in zip(SMALL, _unpack(d_s, local_shapes), _unpack(m_s, local_shapes),
                             _unpack(v_s, local_shapes)):
        out_g[n], out_d[n], out_m[n], out_v[n] = small_g[n], dd, mm, vv

    return (loss, grad_x, *[out_g[n] for n in WEIGHTS], *[out_d[n] for n in WEIGHTS],
            *[out_m[n] for n in WEIGHTS], *[out_v[n] for n in WEIGHTS])
```

```python
import functools

import jax
import jax.numpy as jnp
from jax import lax
from jax.experimental import pallas as pl
from jax.experimental.pallas import tpu as pltpu

F32 = jnp.float32
BF16 = jnp.bfloat16
EPS = 1e-6
HEAD = 128
ROPE = 64
ROPE_BASE = 10000.0
CONV_TAPS = 3
N_SHARD = 4
N_DEV = 8
ADAM_LR = 0.001
ADAM_B1 = 0.9
ADAM_B2 = 0.999
ADAM_EPS = 1e-08
ADAM_WD = 0.01
ADAM_STEP = 10
MESH = pl.DeviceIdType.MESH
VMEM_LIMIT = 56 * 1024 * 1024
HALO_ROWS = 16
ROW_TILES = (512, 256, 128)

WEIGHTS = ['attn_norm', 'w_in', 'sgu_norm', 'w_spatial', 'b_spatial', 'conv_w', 'conv_b', 'kv_norm', 'w_ukv',
           'q_nope_norm', 'q_rope_norm', 'k_nope_norm', 'k_rope_norm', 'out_norm', 'w_out', 'ple_norm',
           'w_ple_gate', 'w_ple_proj']
BIG = ['w_in', 'w_ukv', 'w_out', 'w_ple_gate', 'w_ple_proj']
BIG_AXIS = {'w_in': 1, 'w_ukv': 2, 'w_out': 1, 'w_ple_gate': 1, 'w_ple_proj': 2}
SMALL = [n for n in WEIGHTS if n not in BIG]


def _pick(n, cands):
    for c in cands:
        if n % c == 0:
            return c
    return n


def _params(sem=None):
    return pltpu.CompilerParams(dimension_semantics=sem, vmem_limit_bytes=VMEM_LIMIT)


def _matmul(a, b, mode, out_dtype, name, add=None, b_layer=None, rider=None):
    b_shape = b.shape if b_layer is None else b.shape[1:]
    if mode == 'nn':
        (m, k), n = a.shape, b_shape[1]
    elif mode == 'nt':
        (m, k), n = a.shape, b_shape[0]
    else:
        (k, m), n = a.shape, b_shape[1]
    tm = _pick(m, (1280, 1024, 512, 256, 128))
    tn = _pick(n, (1536, 1024, 512, 256, 128))
    tk = k if k <= 2048 else _pick(k, (2048, 1536, 1024, 512, 256, 128))
    nk = k // tk
    if mode == 'tn':
        a_spec = pl.BlockSpec((tk, tm), lambda i, j, kk: (kk, i))
        dims = (((0,), (0,)), ((), ()))
    else:
        a_spec = pl.BlockSpec((tm, tk), lambda i, j, kk: (i, kk))
        dims = (((1,), (0,)), ((), ())) if mode == 'nn' else (((1,), (1,)), ((), ()))
    b_block = (tn, tk) if mode == 'nt' else (tk, tn)
    if b_layer is None:
        b_spec = pl.BlockSpec(b_block, (lambda i, j, kk: (j, kk)) if mode == 'nt' else (lambda i, j, kk: (kk, j)))
    else:
        b_spec = pl.BlockSpec((None,) + b_block, (lambda i, j, kk: (b_layer, j, kk)) if mode == 'nt'
                              else (lambda i, j, kk: (b_layer, kk, j)))
    o_spec = pl.BlockSpec((tm, tn), lambda i, j, kk: (i, j))
    has_add = add is not None

    def body(*refs):
        a_ref, b_ref = refs[0], refs[1]
        add_ref = refs[2] if has_add else None
        o_ref = refs[3] if has_add else refs[2]

        def product():
            return lax.dot_general(a_ref[...], b_ref[...], dims, preferred_element_type=F32)

        def finish(res):
            if has_add:
                res = res + add_ref[...]
            o_ref[...] = res.astype(out_dtype)

        if nk == 1:
            finish(product())
        else:
            acc_ref = refs[-1]
            kk = pl.program_id(2)

            @pl.when(kk == 0)
            def _():
                acc_ref[...] = product()

            @pl.when((kk > 0) & (kk < nk - 1))
            def _():
                acc_ref[...] += product()

            @pl.when(kk == nk - 1)
            def _():
                finish(acc_ref[...] + product())

    in_specs = [a_spec, b_spec] + ([o_spec] if has_add else [])
    args = [a, b] + ([add] if has_add else [])
    grid = (m // tm, n // tn, nk)
    scratch = [pltpu.VMEM((tm, tn), F32)] if nk > 1 else []
    if rider is None:
        return pl.pallas_call(
            body, name=name, grid=grid, in_specs=in_specs, out_specs=o_spec,
            out_shape=jax.ShapeDtypeStruct((m, n), out_dtype), scratch_shapes=scratch,
            compiler_params=_params(("parallel", "parallel", "arbitrary")),
        )(*args)

    n_in, n_rin, n_rout = len(args), len(rider.arrays), len(rider.out_shapes)

    def body_with_rider(*refs):
        r_in = refs[n_in:n_in + n_rin]
        r_out = refs[n_in + n_rin + 1:n_in + n_rin + 1 + n_rout]
        own = refs[:n_in] + refs[n_in + n_rin:n_in + n_rin + 1] + refs[n_in + n_rin + 1 + n_rout:len(refs) - 2]
        send_sems, recv_sems = refs[-2:]
        ids = [pl.program_id(ax) for ax in range(3)]

        @pl.when((ids[0] == 0) & (ids[1] == 0) & (ids[2] == 0))
        def _():
            rider.start(r_in, r_out, send_sems, recv_sems)

        body(*own)

        @pl.when((ids[0] == grid[0] - 1) & (ids[1] == grid[1] - 1) & (ids[2] == grid[2] - 1))
        def _():
            rider.finish(r_in, r_out, send_sems, recv_sems)

    res = pl.pallas_call(
        body_with_rider, name=name, grid=grid, in_specs=in_specs + [_ANY] * n_rin,
        out_specs=[o_spec] + [_ANY] * n_rout,
        out_shape=[jax.ShapeDtypeStruct((m, n), out_dtype)] + rider.out_shapes,
        scratch_shapes=scratch + rider.sems(), input_output_aliases=rider.aliases(n_in, 1),
        compiler_params=_params(("arbitrary", "arbitrary", "arbitrary")),
    )(*args, *rider.arrays)
    return res[0], res[1:]


def _rms(x, n):
    r = lax.rsqrt(jnp.sum(x * x, axis=-1, keepdims=True) * (1.0 / n) + EPS)
    return x * r, r


def _rms_bwd(dxhat, xhat, r, n):
    return r * (dxhat - xhat * (jnp.sum(dxhat * xhat, axis=-1, keepdims=True) * (1.0 / n)))


def _sigmoid(z):
    return 1.0 / (1.0 + jnp.exp(-z))


def _silu_and_grad(z):
    sig = _sigmoid(z)
    return z * sig, sig * (1.0 + z * (1.0 - sig))


def _colsum(x):
    return jnp.sum(x, axis=0, keepdims=True)


def _rope(t, cos_t, sin_a, sin_b):
    return t * cos_t + pltpu.roll(t, 96, 1) * sin_a + pltpu.roll(t, 32, 1) * sin_b


def _rope_bwd(d, cos_t, sin_a, sin_b):
    return d * cos_t + pltpu.roll(d * sin_a, 32, 1) + pltpu.roll(d * sin_b, 96, 1)


def _shift_down(g, first_row):
    row = lax.broadcasted_iota(jnp.int32, g.shape, 0)
    return jnp.where(row == 0, first_row, pltpu.roll(g, 1, 0))


def _shift_up(g, last_row):
    n = g.shape[0]
    row = lax.broadcasted_iota(jnp.int32, g.shape, 0)
    return jnp.where(row == n - 1, last_row, pltpu.roll(g, n - 1, 0))


def _row_spec(r, w, col=0):
    return pl.BlockSpec((r, w), lambda i: (i, col))


def _const_spec(shape):
    nd = len(shape)
    return pl.BlockSpec(shape, lambda i: (0,) * nd)


def _col_block(off, w):
    assert off % w == 0, (off, w)
    return off // w


def _zero_at_first_step(refs):
    @pl.when(pl.program_id(0) == 0)
    def _():
        for ref in refs:
            ref[...] = jnp.zeros(ref.shape, ref.dtype)


def _row_call(body, name, t, r, in_specs, args, out_specs, out_shapes, scratch=(), into=None):
    if into is None:
        return pl.pallas_call(
            body, name=name, grid=(t // r,), in_specs=in_specs, out_specs=out_specs, out_shape=out_shapes,
            scratch_shapes=list(scratch), compiler_params=_params(("arbitrary",)),
        )(*args)
    buf, col = into
    single = not isinstance(out_specs, (list, tuple))
    specs = [out_specs] if single else list(out_specs)
    shapes = [out_shapes] if single else list(out_shapes)
    width = shapes[0].shape[1]
    assert shapes[0].dtype == buf.dtype and buf.shape[0] == t
    specs[0] = _row_spec(r, width, col)
    shapes[0] = jax.ShapeDtypeStruct(buf.shape, buf.dtype)
    n_in = len(args)

    def body_in_place(*refs):
        body(*refs[:n_in], *refs[n_in + 1:])

    res = pl.pallas_call(
        body_in_place, name=name, grid=(t // r,), in_specs=list(in_specs) + [_ANY], out_specs=specs, out_shape=shapes,
        scratch_shapes=list(scratch), input_output_aliases={n_in: 0}, compiler_params=_params(("arbitrary",)),
    )(*args, buf)
    return res[0] if single else res


def _norm_fwd(h, g, name):
    t, d = h.shape
    r = _pick(t, ROW_TILES)

    def body(h_ref, g_ref, o_ref):
        xhat, _ = _rms(h_ref[...], d)
        o_ref[...] = (xhat * g_ref[...]).astype(BF16)

    return _row_call(body, name, t, r, [_row_spec(r, d), _const_spec((1, d))], (h, g),
                     _row_spec(r, d), jax.ShapeDtypeStruct((t, d), BF16))


def _norm_bwd(h, g, d_hn, d_res, name):
    t, d = h.shape
    r = _pick(t, ROW_TILES)

    def body(h_ref, g_ref, dy_ref, dres_ref, dh_ref, dhb_ref, dg_ref):
        _zero_at_first_step([dg_ref])
        xhat, rr = _rms(h_ref[...], d)
        dy = dy_ref[...].astype(F32)
        dg_ref[...] += _colsum(dy * xhat)
        dh = dres_ref[...] + _rms_bwd(dy * g_ref[...], xhat, rr, d)
        dh_ref[...] = dh
        dhb_ref[...] = dh.astype(BF16)

    return _row_call(body, name, t, r,
                     [_row_spec(r, d), _const_spec((1, d)), _row_spec(r, d), _row_spec(r, d)], (h, g, d_hn, d_res),
                     [_row_spec(r, d), _row_spec(r, d), _const_spec((1, d))],
                     [jax.ShapeDtypeStruct((t, d), F32), jax.ShapeDtypeStruct((t, d), BF16),
                      jax.ShapeDtypeStruct((1, d), F32)])


def _sgu_scores(v, gs_ref, ws_ref, bb_ref, s_scr, r, ah, keep=None):
    for kk in range(r // HEAD):
        for hh in range(ah):
            rows, cols = slice(kk * HEAD, (kk + 1) * HEAD), slice(hh * HEAD, (hh + 1) * HEAD)
            vhat, rv = _rms(v[rows, cols], HEAD)
            vn = vhat * gs_ref[pl.ds(hh, 1), :]
            s_scr[rows, cols] = jnp.dot(ws_ref[hh], vn.astype(BF16), preferred_element_type=F32) + bb_ref[hh]
            if keep is not None:
                keep[(kk, hh)] = (vhat, rv, vn)


def _sgu_fwd(proj, off, aw, gs, ws, bb, ga, name, into=None):
    t = proj.shape[0]
    ah = aw // HEAD
    r = _pick(t, ROW_TILES)
    cb = _col_block(off, aw)

    def body(u_ref, v_ref, z_ref, gs_ref, ws_ref, bb_ref, ga_ref, o_ref, s_scr):
        _sgu_scores(v_ref[...].astype(F32), gs_ref, ws_ref, bb_ref, s_scr, r, ah)
        sil, _ = _silu_and_grad(z_ref[...].astype(F32))
        yhat, _ = _rms(u_ref[...].astype(F32) * s_scr[...] * sil, aw)
        o_ref[...] = (yhat * ga_ref[...]).astype(BF16)

    return _row_call(
        body, name, t, r,
        [_row_spec(r, aw, cb), _row_spec(r, aw, cb + 1), _row_spec(r, aw, cb + 2), _const_spec((ah, HEAD)),
         _const_spec((ah, HEAD, HEAD)), _const_spec((ah, HEAD, HEAD)), _const_spec((1, aw))],
        (proj, proj, proj, gs, ws, bb, ga),
        _row_spec(r, aw), jax.ShapeDtypeStruct((t, aw), BF16), scratch=[pltpu.VMEM((r, aw), F32)], into=into)


def _sgu_bwd(proj, off, aw, gs, ws, ws_t, bb, ga, dy, out_width, name):
    t = proj.shape[0]
    ah = aw // HEAD
    r = _pick(t, ROW_TILES)
    assert off == 0
    cb = _col_block(off, aw)

    def body(u_ref, v_ref, z_ref, gs_ref, ws_ref, wst_ref, bb_ref, ga_ref, dy_ref,
             d_ref, dgs_ref, dws_ref, db_ref, dga_ref, s_scr, dv_scr):
        _zero_at_first_step([dgs_ref, dws_ref, db_ref, dga_ref])
        keep = {}
        _sgu_scores(v_ref[...].astype(F32), gs_ref, ws_ref, bb_ref, s_scr, r, ah, keep)
        u, z, s = u_ref[...].astype(F32), z_ref[...].astype(F32), s_scr[...]
        sil, dsil = _silu_and_grad(z)
        yhat, rr = _rms(u * s * sil, aw)
        dy_f = dy_ref[...].astype(F32)
        dga_ref[...] += _colsum(dy_f * yhat)
        dya = _rms_bwd(dy_f * ga_ref[...], yhat, rr, aw)
        d_ref[:, 0:aw] = (dya * s * sil).astype(BF16)
        d_ref[:, 2 * aw:3 * aw] = (dya * u * s * dsil).astype(BF16)
        ds = dya * u * sil
        for kk in range(r // HEAD):
            for hh in range(ah):
                rows, cols = slice(kk * HEAD, (kk + 1) * HEAD), slice(hh * HEAD, (hh + 1) * HEAD)
                vhat, rv, vn = keep[(kk, hh)]
                ds_blk = ds[rows, cols]
                db_ref[hh] += jnp.sum(ds_blk, axis=1, keepdims=True)
                ds_b = ds_blk.astype(BF16)
                dws_ref[hh] += lax.dot_general(ds_b, vn.astype(BF16), (((1,), (1,)), ((), ())),
                                               preferred_element_type=F32)
                dvn = jnp.dot(wst_ref[hh], ds_b, preferred_element_type=F32)
                dgs_ref[pl.ds(hh, 1), :] += _colsum(dvn * vhat)
                dv_scr[rows, cols] = _rms_bwd(dvn * gs_ref[pl.ds(hh, 1), :], vhat, rv, HEAD)
        d_ref[:, aw:2 * aw] = dv_scr[...].astype(BF16)

    return _row_call(
        body, name, t, r,
        [_row_spec(r, aw, cb), _row_spec(r, aw, cb + 1), _row_spec(r, aw, cb + 2), _const_spec((ah, HEAD)),
         _const_spec((ah, HEAD, HEAD)), _const_spec((ah, HEAD, HEAD)), _const_spec((ah, HEAD, HEAD)),
         _const_spec((1, aw)), _row_spec(r, aw, 0)],
        (proj, proj, proj, gs, ws, ws_t, bb, ga, dy),
        [_row_spec(r, 3 * aw), _const_spec((ah, HEAD)), _const_spec((ah, HEAD, HEAD)), _const_spec((ah, HEAD, 1)),
         _const_spec((1, aw))],
        [jax.ShapeDtypeStruct((t, out_width), BF16), jax.ShapeDtypeStruct((ah, HEAD), F32),
         jax.ShapeDtypeStruct((ah, HEAD, HEAD), F32), jax.ShapeDtypeStruct((ah, HEAD, 1), F32),
         jax.ShapeDtypeStruct((1, aw), F32)],
        scratch=[pltpu.VMEM((r, aw), F32), pltpu.VMEM((r, aw), F32)])


def _halo_specs(t, r, w, col, rows):
    per = r // rows
    last = t // rows - 1
    prev = pl.BlockSpec((rows, w), lambda i: (jnp.maximum(i * per - 1, 0), col))
    nxt = pl.BlockSpec((rows, w), lambda i: (jnp.minimum((i + 1) * per, last), col))
    return prev, nxt


def _edge_rows(prev_ref, next_ref, n_steps):
    i = pl.program_id(0)
    rows = prev_ref.shape[0]
    before = prev_ref[...].astype(F32)[rows - 1:rows, :] * (i > 0).astype(F32)
    after = next_ref[...].astype(F32)[0:1, :] * (i < n_steps - 1).astype(F32)
    return before, after


def _conv_fwd(proj, off, bw, cw, cb_, gb, name, into=None):
    t = proj.shape[0]
    r = _pick(t, ROW_TILES)
    n_steps = t // r
    c0 = _col_block(off, bw)
    cp, cn = _halo_specs(t, r, bw, c0 + 1, HALO_ROWS)
    hp, hn = _halo_specs(t, r, bw, c0 + 2, HALO_ROWS)

    def body(b_ref, c_ref, h_ref, z_ref, cp_ref, cn_ref, hp_ref, hn_ref, cw_ref, cb_ref, gb_ref, o_ref, yc_ref):
        g = c_ref[...].astype(F32) * h_ref[...].astype(F32)
        c_before, c_after = _edge_rows(cp_ref, cn_ref, n_steps)
        h_before, h_after = _edge_rows(hp_ref, hn_ref, n_steps)
        yconv = (cb_ref[...] + cw_ref[0:1, :] * _shift_down(g, c_before * h_before) + cw_ref[1:2, :] * g
                 + cw_ref[2:3, :] * _shift_up(g, c_after * h_after))
        yc_ref[...] = yconv
        sil, _ = _silu_and_grad(z_ref[...].astype(F32))
        yhat, _ = _rms(b_ref[...].astype(F32) * yconv * sil, bw)
        o_ref[...] = (yhat * gb_ref[...]).astype(BF16)

    return _row_call(
        body, name, t, r,
        [_row_spec(r, bw, c0), _row_spec(r, bw, c0 + 1), _row_spec(r, bw, c0 + 2), _row_spec(r, bw, c0 + 3),
         cp, cn, hp, hn, _const_spec((CONV_TAPS, bw)), _const_spec((1, bw)), _const_spec((1, bw))],
        (proj, proj, proj, proj, proj, proj, proj, proj, cw, cb_, gb),
        [_row_spec(r, bw), _row_spec(r, bw)],
        [jax.ShapeDtypeStruct((t, bw), BF16), jax.ShapeDtypeStruct((t, bw), F32)], into=into)


def _conv_bwd_gate(proj, off, bw, yconv, gb, dy, name):
    t = proj.shape[0]
    r = _pick(t, ROW_TILES)
    c0 = _col_block(off, bw)

    def body(b_ref, z_ref, yc_ref, gb_ref, dy_ref, dyc_ref, db_ref, dz_ref, dgb_ref, dcb_ref):
        _zero_at_first_step([dgb_ref, dcb_ref])
        b, z, yconv_v = b_ref[...].astype(F32), z_ref[...].astype(F32), yc_ref[...]
        sil, dsil = _silu_and_grad(z)
        yhat, rr = _rms(b * yconv_v * sil, bw)
        dy_f = dy_ref[...].astype(F32)
        dgb_ref[...] += _colsum(dy_f * yhat)
        dyb = _rms_bwd(dy_f * gb_ref[...], yhat, rr, bw)
        dyc = dyb * b * sil
        dyc_ref[...] = dyc
        dcb_ref[...] += _colsum(dyc)
        db_ref[...] = (dyb * yconv_v * sil).astype(BF16)
        dz_ref[...] = (dyb * b * yconv_v * dsil).astype(BF16)

    return _row_call(
        body, name, t, r,
        [_row_spec(r, bw, c0), _row_spec(r, bw, c0 + 3), _row_spec(r, bw), _const_spec((1, bw)), _row_spec(r, bw, 1)],
        (proj, proj, yconv, gb, dy),
        [_row_spec(r, bw), _row_spec(r, bw), _row_spec(r, bw), _const_spec((1, bw)), _const_spec((1, bw))],
        [jax.ShapeDtypeStruct((t, bw), F32), jax.ShapeDtypeStruct((t, bw), BF16), jax.ShapeDtypeStruct((t, bw), BF16),
         jax.ShapeDtypeStruct((1, bw), F32), jax.ShapeDtypeStruct((1, bw), F32)])


def _conv_bwd_taps(proj, off, bw, dyc, cw, d_gate_b, d_gate_z, name, into=None):
    t = proj.shape[0]
    r = _pick(t, ROW_TILES)
    n_steps = t // r
    c0 = _col_block(off, bw)
    cp, cn = _halo_specs(t, r, bw, c0 + 1, HALO_ROWS)
    hp, hn = _halo_specs(t, r, bw, c0 + 2, HALO_ROWS)
    dp, dn = _halo_specs(t, r, bw, 0, 8)

    def body(c_ref, h_ref, cp_ref, cn_ref, hp_ref, hn_ref, d_ref, dp_ref, dn_ref, cw_ref, dgb_ref, dgz_ref,
             db_ref, dcw_ref):
        _zero_at_first_step([dcw_ref])
        c, h, d = c_ref[...].astype(F32), h_ref[...].astype(F32), d_ref[...]
        g = c * h
        c_before, c_after = _edge_rows(cp_ref, cn_ref, n_steps)
        h_before, h_after = _edge_rows(hp_ref, hn_ref, n_steps)
        d_before, d_after = _edge_rows(dp_ref, dn_ref, n_steps)
        dg = (cw_ref[0:1, :] * _shift_up(d, d_after) + cw_ref[1:2, :] * d + cw_ref[2:3, :] * _shift_down(d, d_before))
        db_ref[:, 0:bw] = dgb_ref[...]
        db_ref[:, bw:2 * bw] = (dg * h).astype(BF16)
        db_ref[:, 2 * bw:3 * bw] = (dg * c).astype(BF16)
        db_ref[:, 3 * bw:4 * bw] = dgz_ref[...]
        dcw_ref[0:1, :] += _colsum(d * _shift_down(g, c_before * h_before))
        dcw_ref[1:2, :] += _colsum(d * g)
        dcw_ref[2:3, :] += _colsum(d * _shift_up(g, c_after * h_after))

    return _row_call(
        body, name, t, r,
        [_row_spec(r, bw, c0 + 1), _row_spec(r, bw, c0 + 2), cp, cn, hp, hn, _row_spec(r, bw), dp, dn,
         _const_spec((CONV_TAPS, bw)), _row_spec(r, bw), _row_spec(r, bw)],
        (proj, proj, proj, proj, proj, proj, dyc, dyc, dyc, cw, d_gate_b, d_gate_z),
        [_row_spec(r, 4 * bw), _const_spec((CONV_TAPS, bw))],
        [jax.ShapeDtypeStruct((t, 4 * bw), BF16), jax.ShapeDtypeStruct((CONV_TAPS, bw), F32)], into=into)


def _mla_prep_fwd(proj, q_off, ckv_off, kr_off, ch, kvr, tabs, qn_g, qr_g, kr_g, kv_g, name):
    t = proj.shape[0]
    r = _pick(t, ROW_TILES)
    qw = ch * 2 * HEAD
    cos_t, sin_a, sin_b = tabs

    def body(q_ref, ckv_ref, kr_ref, cos_ref, sa_ref, sb_ref, qn_ref, qr_ref, krg_ref, kvg_ref,
             qo_ref, co_ref, ko_ref):
        cos_v, sa, sb = cos_ref[...], sa_ref[...], sb_ref[...]
        for hh in range(ch):
            lo = hh * 2 * HEAD
            nhat, _ = _rms(q_ref[:, lo:lo + HEAD].astype(F32), HEAD)
            qo_ref[:, lo:lo + HEAD] = (nhat * qn_ref[...]).astype(BF16)
            rhat, _ = _rms(q_ref[:, lo + HEAD:lo + 2 * HEAD].astype(F32), ROPE)
            qo_ref[:, lo + HEAD:lo + 2 * HEAD] = _rope(rhat * qr_ref[...], cos_v, sa, sb).astype(BF16)
        khat, _ = _rms(kr_ref[...].astype(F32), ROPE)
        ko_ref[...] = _rope(khat * krg_ref[...], cos_v, sa, sb).astype(BF16)
        chat, _ = _rms(ckv_ref[...].astype(F32), kvr)
        co_ref[...] = (chat * kvg_ref[...]).astype(BF16)

    tab = _row_spec(r, HEAD)
    gain = _const_spec((1, HEAD))
    return _row_call(
        body, name, t, r,
        [_row_spec(r, qw, _col_block(q_off, qw)), _row_spec(r, kvr, _col_block(ckv_off, kvr)),
         _row_spec(r, HEAD, _col_block(kr_off, HEAD)), tab, tab, tab, gain, gain, gain, _const_spec((1, kvr))],
        (proj, proj, proj, cos_t, sin_a, sin_b, qn_g, qr_g, kr_g, kv_g),
        [_row_spec(r, qw), _row_spec(r, kvr), _row_spec(r, HEAD)],
        [jax.ShapeDtypeStruct((t, qw), BF16), jax.ShapeDtypeStruct((t, kvr), BF16),
         jax.ShapeDtypeStruct((t, HEAD), BF16)])


def _mla_prep_bwd(proj, q_off, ckv_off, kr_off, ch, kvr, tabs, qn_g, qr_g, kr_g, kv_g, dq_cat_t, dq_scale, dckv_n,
                  dkr_rot, kr_width, name, into=None):
    t = proj.shape[0]
    r = _pick(t, ROW_TILES)
    qw = ch * 2 * HEAD
    cos_t, sin_a, sin_b = tabs

    def body(q_ref, ckv_ref, kr_ref, cos_ref, sa_ref, sb_ref, qn_ref, qr_ref, krg_ref, kvg_ref,
             dq_ref, dc_ref, dk_ref, dqo_ref, dco_ref, dko_ref, dqn_ref, dqr_ref, dkrg_ref, dkvg_ref):
        _zero_at_first_step([dqn_ref, dqr_ref, dkrg_ref, dkvg_ref])
        cos_v, sa, sb = cos_ref[...], sa_ref[...], sb_ref[...]
        dq = dq_ref[...].T * dq_scale
        for hh in range(ch):
            lo = hh * 2 * HEAD
            nhat, nr = _rms(q_ref[:, lo:lo + HEAD].astype(F32), HEAD)
            d_n = dq[:, lo:lo + HEAD]
            dqn_ref[...] += _colsum(d_n * nhat)
            dqo_ref[:, lo:lo + HEAD] = _rms_bwd(d_n * qn_ref[...], nhat, nr, HEAD).astype(BF16)
            rhat, rr = _rms(q_ref[:, lo + HEAD:lo + 2 * HEAD].astype(F32), ROPE)
            d_t = _rope_bwd(dq[:, lo + HEAD:lo + 2 * HEAD], cos_v, sa, sb)
            dqr_ref[...] += _colsum(d_t * rhat)
            dqo_ref[:, lo + HEAD:lo + 2 * HEAD] = _rms_bwd(d_t * qr_ref[...], rhat, rr, ROPE).astype(BF16)
        khat, kr_r = _rms(kr_ref[...].astype(F32), ROPE)
        d_k = _rope_bwd(dk_ref[...], cos_v, sa, sb)
        dkrg_ref[...] += _colsum(d_k * khat)
        dko_ref[:, 0:HEAD] = _rms_bwd(d_k * krg_ref[...], khat, kr_r, ROPE).astype(BF16)
        if kr_width > HEAD:
            dko_ref[:, HEAD:kr_width] = jnp.zeros((r, kr_width - HEAD), BF16)
        chat, cr = _rms(ckv_ref[...].astype(F32), kvr)
        d_c = dc_ref[...].astype(F32)
        dkvg_ref[...] += _colsum(d_c * chat)
        dco_ref[...] = _rms_bwd(d_c * kvg_ref[...], chat, cr, kvr).astype(BF16)

    tab = _row_spec(r, HEAD)
    gain = _const_spec((1, HEAD))
    return _row_call(
        body, name, t, r,
        [_row_spec(r, qw, _col_block(q_off, qw)), _row_spec(r, kvr, _col_block(ckv_off, kvr)),
         _row_spec(r, HEAD, _col_block(kr_off, HEAD)), tab, tab, tab, gain, gain, gain, _const_spec((1, kvr)),
         pl.BlockSpec((qw, r), lambda i: (0, i)), _row_spec(r, kvr), _row_spec(r, HEAD)],
        (proj, proj, proj, cos_t, sin_a, sin_b, qn_g, qr_g, kr_g, kv_g, dq_cat_t, dckv_n, dkr_rot),
        [_row_spec(r, qw), _row_spec(r, kvr), _row_spec(r, kr_width), gain, gain, gain, _const_spec((1, kvr))],
        [jax.ShapeDtypeStruct((t, qw), BF16), jax.ShapeDtypeStruct((t, kvr), BF16),
         jax.ShapeDtypeStruct((t, kr_width), BF16), jax.ShapeDtypeStruct((1, HEAD), F32),
         jax.ShapeDtypeStruct((1, HEAD), F32), jax.ShapeDtypeStruct((1, HEAD), F32),
         jax.ShapeDtypeStruct((1, kvr), F32)], into=into)


def _kv_prep_fwd(kv, kr_rot, ch, kn_g, name):
    t = kv.shape[0]
    r = _pick(t, ROW_TILES)
    qw = ch * 2 * HEAD

    def body(kv_ref, kr_ref, kn_ref, ko_ref, kt_ref, vo_ref):
        ones = jnp.ones((r, HEAD), BF16)
        for hh in range(ch):
            lo = hh * 2 * HEAD
            nhat, _ = _rms(kv_ref[:, lo:lo + HEAD], HEAD)
            ko_ref[:, lo:lo + HEAD] = (nhat * kn_ref[...]).astype(BF16)
            ko_ref[:, lo + HEAD:lo + 2 * HEAD] = kr_ref[...]
            vo_ref[:, lo:lo + HEAD] = kv_ref[:, lo + HEAD:lo + 2 * HEAD].astype(BF16)
            vo_ref[:, lo + HEAD:lo + 2 * HEAD] = ones
        kt_ref[...] = ko_ref[...].astype(F32).T.astype(BF16)

    return _row_call(
        body, name, t, r, [_row_spec(r, qw), _row_spec(r, HEAD), _const_spec((1, HEAD))], (kv, kr_rot, kn_g),
        [_row_spec(r, qw), pl.BlockSpec((qw, r), lambda i: (0, i)), _row_spec(r, qw)],
        [jax.ShapeDtypeStruct((t, qw), BF16), jax.ShapeDtypeStruct((qw, t), BF16),
         jax.ShapeDtypeStruct((t, qw), BF16)])


def _kv_prep_bwd(kv, ch, kn_g, dk_cat, dv, name):
    t = kv.shape[0]
    r = _pick(t, ROW_TILES)
    qw = ch * 2 * HEAD

    def body(kv_ref, kn_ref, dk_ref, dv_ref, dkv_ref, dkr_ref, dkn_ref):
        _zero_at_first_step([dkn_ref])
        dkr = jnp.zeros((r, HEAD), F32)
        for hh in range(ch):
            lo = hh * 2 * HEAD
            nhat, nr = _rms(kv_ref[:, lo:lo + HEAD], HEAD)
            d_n = dk_ref[:, lo:lo + HEAD].astype(F32)
            dkn_ref[...] += _colsum(d_n * nhat)
            dkv_ref[:, lo:lo + HEAD] = _rms_bwd(d_n * kn_ref[...], nhat, nr, HEAD).astype(BF16)
            dkv_ref[:, lo + HEAD:lo + 2 * HEAD] = dv_ref[:, hh * HEAD:(hh + 1) * HEAD]
            dkr = dkr + dk_ref[:, lo + HEAD:lo + 2 * HEAD].astype(F32)
        dkr_ref[...] = dkr

    return _row_call(
        body, name, t, r,
        [_row_spec(r, qw), _const_spec((1, HEAD)), _row_spec(r, qw), _row_spec(r, ch * HEAD)], (kv, kn_g, dk_cat, dv),
        [_row_spec(r, qw), _row_spec(r, HEAD), _const_spec((1, HEAD))],
        [jax.ShapeDtypeStruct((t, qw), BF16), jax.ShapeDtypeStruct((t, HEAD), F32),
         jax.ShapeDtypeStruct((1, HEAD), F32)])


def _attn_tiles(t):
    return _pick(t, (2048, 1024, 512, 256, 128)), _pick(t, (1024, 512, 256, 128))


_NT = (((1,), (1,)), ((), ()))
LOG2E = 1.4426950408889634


def _attn_fwd(q_cat, k_cat, v_aug, ch, scale, name):
    t = q_cat.shape[0]
    tq, tk = _attn_tiles(t)
    nk = t // tk
    c2 = scale * LOG2E

    def body(q_ref, k_ref, v_ref, o_ref, lse_ref, s_scr, m_scr, acc_scr):
        j = pl.program_id(2)

        def scores(slot):
            s_scr[slot] = lax.dot_general(q_ref[...], k_ref[...], _NT, preferred_element_type=F32) * c2

        def absorb(slot):
            s = s_scr[slot]
            m_old = m_scr[...]
            m_new = jnp.maximum(m_old, jnp.max(s, axis=-1, keepdims=True))
            p = jnp.exp2(s - m_new).astype(BF16)
            acc_scr[...] = (jnp.exp2(m_old - m_new) * acc_scr[...]
                            + jnp.dot(p, v_ref[...], preferred_element_type=F32))
            m_scr[...] = m_new

        @pl.when(j == 0)
        def _():
            m_scr[...] = jnp.full(m_scr.shape, -jnp.inf, F32)
            acc_scr[...] = jnp.zeros(acc_scr.shape, F32)
            scores(0)

        for parity in (0, 1):
            @pl.when((j > 0) & (j < nk) & (j % 2 == parity))
            def _():
                scores(parity)
                absorb(1 - parity)

        @pl.when(j == nk)
        def _():
            absorb((nk - 1) % 2)
            acc = acc_scr[...]
            l_sum = acc[:, HEAD:]
            o_ref[...] = (acc[:, :HEAD] / l_sum).astype(BF16)
            lse_ref[0] = m_scr[...] + jnp.log(l_sum[:, 0:1]) * LOG2E

    return pl.pallas_call(
        body, name=name, grid=(ch, t // tq, nk + 1),
        in_specs=[pl.BlockSpec((tq, 2 * HEAD), lambda h, i, j: (i, h)),
                  pl.BlockSpec((tk, 2 * HEAD), lambda h, i, j: (jnp.minimum(j, nk - 1), h)),
                  pl.BlockSpec((tk, 2 * HEAD), lambda h, i, j: (jnp.maximum(j - 1, 0), h))],
        out_specs=[pl.BlockSpec((tq, HEAD), lambda h, i, j: (i, h)),
                   pl.BlockSpec((1, tq, 1), lambda h, i, j: (h, i, 0))],
        out_shape=[jax.ShapeDtypeStruct((t, ch * HEAD), BF16), jax.ShapeDtypeStruct((ch, t, 1), F32)],
        scratch_shapes=[pltpu.VMEM((2, tq, tk), F32), pltpu.VMEM((tq, 1), F32), pltpu.VMEM((tq, 2 * HEAD), F32)],
        compiler_params=_params(("parallel", "parallel", "arbitrary")),
    )(q_cat, k_cat, v_aug)


def _attn_bwd(q_cat, k_cat, k_cat_t, v_aug, do, lse_row, d_row, ch, scale, name):
    t = q_cat.shape[0]
    tk = _pick(t, (1024, 512, 256, 128))
    tq = _pick(t, (1024, 512, 256, 128))
    nk, nq = t // tk, t // tq
    c2 = scale * LOG2E

    def body(q_ref, do_ref, qp_ref, dop_ref, lse_ref, d_ref, k_ref, kt_ref, v_ref,
             dqt_ref, dk_ref, dv_ref, s_scr, dp_scr, dk_scr, dv_scr):
        ki, j = pl.program_id(1), pl.program_id(2)

        def products(slot):
            s_scr[slot] = lax.dot_general(k_ref[...], q_ref[...], _NT, preferred_element_type=F32) * c2
            dp_scr[slot] = lax.dot_general(v_ref[...], do_ref[...], _NT, preferred_element_type=F32)

        def absorb(slot):
            q, do_v = qp_ref[...], dop_ref[...]
            pt = jnp.exp2(s_scr[slot] - lse_ref[0])
            dv_scr[...] += jnp.dot(pt.astype(BF16), do_v, preferred_element_type=F32)
            dst = (pt * (dp_scr[slot] - d_ref[0])).astype(BF16)
            dk_scr[...] += jnp.dot(dst, q, preferred_element_type=F32)
            part = jnp.dot(kt_ref[...], dst, preferred_element_type=F32)
            cols = pl.ds(pl.multiple_of((j - 1) * tq, tq), tq)

            @pl.when(ki == 0)
            def _():
                dqt_ref[:, cols] = part

            @pl.when(ki > 0)
            def _():
                dqt_ref[:, cols] += part

        @pl.when(j == 0)
        def _():
            dk_scr[...] = jnp.zeros(dk_scr.shape, F32)
            dv_scr[...] = jnp.zeros(dv_scr.shape, F32)
            products(0)

        for parity in (0, 1):
            @pl.when((j > 0) & (j < nq) & (j % 2 == parity))
            def _():
                products(parity)
                absorb(1 - parity)

        @pl.when(j == nq)
        def _():
            absorb((nq - 1) % 2)
            dk_ref[...] = (dk_scr[...] * scale).astype(BF16)
            dv_ref[...] = dv_scr[...].astype(BF16)

    def cur(i):
        return jnp.minimum(i, nq - 1)

    def prev(i):
        return jnp.maximum(i - 1, 0)

    stat = pl.BlockSpec((1, 1, tq), lambda h, j, i: (h, 0, prev(i)))
    return pl.pallas_call(
        body, name=name, grid=(ch, nk, nq + 1),
        in_specs=[pl.BlockSpec((tq, 2 * HEAD), lambda h, j, i: (cur(i), h)),
                  pl.BlockSpec((tq, HEAD), lambda h, j, i: (cur(i), h)),
                  pl.BlockSpec((tq, 2 * HEAD), lambda h, j, i: (prev(i), h)),
                  pl.BlockSpec((tq, HEAD), lambda h, j, i: (prev(i), h)), stat, stat,
                  pl.BlockSpec((tk, 2 * HEAD), lambda h, j, i: (j, h)),
                  pl.BlockSpec((2 * HEAD, tk), lambda h, j, i: (h, j)),
                  pl.BlockSpec((tk, HEAD), lambda h, j, i: (j, 2 * h))],
        out_specs=[pl.BlockSpec((2 * HEAD, t), lambda h, j, i: (h, 0)),
                   pl.BlockSpec((tk, 2 * HEAD), lambda h, j, i: (j, h)),
                   pl.BlockSpec((tk, HEAD), lambda h, j, i: (j, h))],
        out_shape=[jax.ShapeDtypeStruct((ch * 2 * HEAD, t), F32), jax.ShapeDtypeStruct((t, ch * 2 * HEAD), BF16),
                   jax.ShapeDtypeStruct((t, ch * HEAD), BF16)],
        scratch_shapes=[pltpu.VMEM((2, tk, tq), F32), pltpu.VMEM((2, tk, tq), F32),
                        pltpu.VMEM((tk, 2 * HEAD), F32), pltpu.VMEM((tk, HEAD), F32)],
        compiler_params=_params(("parallel", "arbitrary", "arbitrary")),
    )(q_cat, do, q_cat, do, lse_row, d_row, k_cat, k_cat_t, v_aug)


def _attn_post_fwd(o, proj, z_off, cw, gc, name, into=None):
    t = o.shape[0]
    r = _pick(t, ROW_TILES)

    def body(o_ref, z_ref, gc_ref, y_ref):
        sil, _ = _silu_and_grad(z_ref[...].astype(F32))
        yhat, _ = _rms(o_ref[...].astype(F32) * sil, cw)
        y_ref[...] = (yhat * gc_ref[...]).astype(BF16)

    return _row_call(body, name, t, r,
                     [_row_spec(r, cw), _row_spec(r, cw, _col_block(z_off, cw)), _const_spec((1, cw))], (o, proj, gc),
                     _row_spec(r, cw), jax.ShapeDtypeStruct((t, cw), BF16), into=into)


def _attn_post_bwd(o, proj, z_off, cw, gc, dy, dy_col, name, into=None):
    t = o.shape[0]
    ch = cw // HEAD
    r = _pick(t, ROW_TILES)

    def body(o_ref, z_ref, gc_ref, dy_ref, dz_ref, do_ref, ds_ref, dgc_ref):
        _zero_at_first_step([dgc_ref])
        o_v, z = o_ref[...].astype(F32), z_ref[...].astype(F32)
        sil, dsil = _silu_and_grad(z)
        yhat, rr = _rms(o_v * sil, cw)
        dy_f = dy_ref[...].astype(F32)
        dgc_ref[...] += _colsum(dy_f * yhat)
        dyc = _rms_bwd(dy_f * gc_ref[...], yhat, rr, cw)
        do_b = (dyc * sil).astype(BF16)
        do_ref[...] = do_b
        dz_ref[...] = (dyc * o_v * dsil).astype(BF16)
        prod = do_b.astype(F32) * o_v
        for hh in range(ch):
            ds_ref[hh] = jnp.sum(prod[:, hh * HEAD:(hh + 1) * HEAD], axis=-1, keepdims=True)

    return _row_call(
        body, name, t, r,
        [_row_spec(r, cw), _row_spec(r, cw, _col_block(z_off, cw)), _const_spec((1, cw)), _row_spec(r, cw, dy_col)],
        (o, proj, gc, dy),
        [_row_spec(r, cw), _row_spec(r, cw), pl.BlockSpec((ch, r, 1), lambda i: (0, i, 0)), _const_spec((1, cw))],
        [jax.ShapeDtypeStruct((t, cw), BF16), jax.ShapeDtypeStruct((t, cw), BF16),
         jax.ShapeDtypeStruct((ch, t, 1), F32), jax.ShapeDtypeStruct((1, cw), F32)], into=into)


def _ple_fwd(h1, gpre, pp, name):
    t, d = h1.shape
    r = _pick(t, ROW_TILES)

    def body(h_ref, g_ref, p_ref, o_ref):
        o_ref[...] = h_ref[...] + _sigmoid(g_ref[...]) * p_ref[...]

    return _row_call(body, name, t, r, [_row_spec(r, d)] * 3, (h1, gpre, pp), _row_spec(r, d),
                     jax.ShapeDtypeStruct((t, d), F32))


def _ple_bwd(gpre, pp, dh, name):
    t, d = dh.shape
    r = _pick(t, ROW_TILES)

    def body(g_ref, p_ref, dh_ref, dg_ref, dp_ref):
        sig = _sigmoid(g_ref[...])
        dh_v = dh_ref[...]
        dg_ref[...] = (dh_v * p_ref[...] * sig * (1.0 - sig)).astype(BF16)
        dp_ref[...] = (dh_v * sig).astype(BF16)

    return _row_call(body, name, t, r, [_row_spec(r, d)] * 3, (gpre, pp, dh), [_row_spec(r, d)] * 2,
                     [jax.ShapeDtypeStruct((t, d), BF16)] * 2)


def _loss_and_grad(h, target, name):
    t, d = h.shape
    r = _pick(t, ROW_TILES)

    def body(h_ref, t_ref, l_ref, dh_ref):
        _zero_at_first_step([l_ref])
        err = h_ref[...] - t_ref[...]
        l_ref[...] += jnp.sum(jnp.sum(err * err, axis=-1, keepdims=True), axis=0, keepdims=True) * (0.5 / d)
        dh_ref[...] = err * (1.0 / d)

    return _row_call(body, name, t, r, [_row_spec(r, d)] * 2, (h, target), [_const_spec((1, 1)), _row_spec(r, d)],
                     [jax.ShapeDtypeStruct((1, 1), F32), jax.ShapeDtypeStruct((t, d), F32)])


def _ew_rows(rows, cols):
    cap = min(1024, max(8, (1 << 19) // max(cols, 1)))
    for cand in range(cap - cap % 8, 7, -8):
        if rows % cand == 0:
            return cand
    return rows


def _pair_sum_bf16(both, core_flag, b, name):
    _, n, rows, cols = both.shape
    rb = _ew_rows(rows, cols)

    def body(a_ref, flag_ref, b_ref, o_ref):
        mine = jnp.where(flag_ref[...] == 0.0, a_ref[0, 0], a_ref[1, 0])
        o_ref[0] = (mine + b_ref[0]).astype(BF16)

    spec = pl.BlockSpec((1, rb, cols), lambda s, i: (s, i, 0))
    return pl.pallas_call(
        body, name=name, grid=(n, rows // rb),
        in_specs=[pl.BlockSpec((2, 1, rb, cols), lambda s, i: (0, s, i, 0)),
                  pl.BlockSpec((1, 1), lambda s, i: (0, 0)), spec],
        out_specs=spec, out_shape=jax.ShapeDtypeStruct(b.shape, BF16),
        compiler_params=_params(("parallel", "parallel")))(both, core_flag, b)


def _shard_sum(a, b, recv, name):
    rows, cols = a.shape
    rb = _ew_rows(rows, cols)

    def body(a_ref, b_ref, r_ref, o_ref):
        o_ref[...] = ((a_ref[...] + b_ref[...]) + r_ref[0].astype(F32) + r_ref[1].astype(F32)
                      + r_ref[2].astype(F32))

    spec = pl.BlockSpec((rb, cols), lambda i: (i, 0))
    return pl.pallas_call(body, name=name, grid=(rows // rb,),
                          in_specs=[spec, spec, pl.BlockSpec((N_SHARD - 1, rb, cols), lambda i: (0, i, 0))],
                          out_specs=spec, out_shape=jax.ShapeDtypeStruct(a.shape, F32),
                          compiler_params=_params(("parallel",)))(a, b, recv)


def _sum_devices(g, name):
    n, rows, cols = g.shape
    rb = _ew_rows(rows, cols)

    def body(g_ref, o_ref):
        acc = g_ref[0]
        for k in range(1, n):
            acc = acc + g_ref[k]
        o_ref[...] = acc

    return pl.pallas_call(body, name=name, grid=(rows // rb,),
                          in_specs=[pl.BlockSpec((n, rb, cols), lambda i: (0, i, 0))],
                          out_specs=pl.BlockSpec((rb, cols), lambda i: (i, 0)),
                          out_shape=jax.ShapeDtypeStruct((rows, cols), F32),
                          compiler_params=_params(("parallel",)))(g)


def _adamw_update(w, g_v, m, v):
    m_new = ADAM_B1 * m + (1.0 - ADAM_B1) * g_v
    v_new = ADAM_B2 * v + (1.0 - ADAM_B2) * (g_v * g_v)
    m_hat = m_new / (1.0 - ADAM_B1 ** ADAM_STEP)
    v_hat = v_new / (1.0 - ADAM_B2 ** ADAM_STEP)
    return -ADAM_LR * (m_hat / (jnp.sqrt(v_hat) + ADAM_EPS) + ADAM_WD * w), m_new, v_new


def _adamw(w, g, m, v, name):
    rows, cols = w.shape
    rb = _ew_rows(rows, cols)

    def body(w_ref, g_ref, m_ref, v_ref, d_ref, mo_ref, vo_ref):
        d_ref[...], mo_ref[...], vo_ref[...] = _adamw_update(w_ref[...], g_ref[...], m_ref[...], v_ref[...])

    spec = pl.BlockSpec((rb, cols), lambda i: (i, 0))
    return pl.pallas_call(body, name=name, grid=(rows // rb,), in_specs=[spec] * 4, out_specs=[spec] * 3,
                          out_shape=[jax.ShapeDtypeStruct(w.shape, F32)] * 3,
                          compiler_params=_params(("parallel",)))(w, g, m, v)


def _adamw_two_halves(w, own, recv, core_flag, m, v, name):
    depth, rows, cols = w.shape
    assert rows % 2 == 0 and own.shape == (depth, rows // 2, cols)
    rb = _ew_rows(rows // 2, cols)
    nb = rows // 2 // rb

    def body(w_ref, own_ref, recv_ref, flag_ref, m_ref, v_ref, g_ref, d_ref, mo_ref, vo_ref):
        half = pl.program_id(1).astype(F32)
        g_v = jnp.where(flag_ref[...] == half, own_ref[...], recv_ref[...])
        g_ref[...] = g_v
        d_ref[...], mo_ref[...], vo_ref[...] = _adamw_update(w_ref[...], g_v, m_ref[...], v_ref[...])

    full = pl.BlockSpec((None, rb, cols), lambda l, k, i: (l, k * nb + i, 0))
    half_spec = pl.BlockSpec((None, rb, cols), lambda l, k, i: (l, i, 0))
    return pl.pallas_call(
        body, name=name, grid=(depth, 2, nb),
        in_specs=[full, half_spec, half_spec, pl.BlockSpec((1, 1), lambda l, k, i: (0, 0)), full, full],
        out_specs=[full] * 4, out_shape=[jax.ShapeDtypeStruct(w.shape, F32)] * 4,
        compiler_params=_params(("parallel", "parallel", "parallel")))(w, own, recv, core_flag, m, v)


def _place():
    return lax.axis_index("x"), lax.axis_index("y"), lax.axis_index("c")


def _other_chips(x, y):
    return [(1 - x, y), (x, 1 - y), (1 - x, 1 - y)]


_ANY = pl.BlockSpec(memory_space=pl.ANY)


class _Rider:
    def __init__(self, arrays, out_shapes, n_sems, start, finish, in_place=False):
        self.arrays, self.out_shapes, self.n_sems = list(arrays), list(out_shapes), n_sems
        self.start, self.finish, self.in_place = start, finish, in_place

    def sems(self):
        return [pltpu.SemaphoreType.DMA((self.n_sems,)), pltpu.SemaphoreType.DMA((self.n_sems,))]

    def aliases(self, first_in, first_out):
        return {first_in + a: first_out + a for a in range(len(self.arrays))} if self.in_place else {}


def _comm_call(rider, name):
    n_in, n_out = len(rider.arrays), len(rider.out_shapes)

    def body(*refs):
        ins, outs = refs[:n_in], refs[n_in:n_in + n_out]
        send_sems, recv_sems = refs[n_in + n_out:]
        rider.start(ins, outs, send_sems, recv_sems)
        rider.finish(ins, outs, send_sems, recv_sems)

    return pl.pallas_call(
        body, name=name, in_specs=[_ANY] * n_in, out_specs=[_ANY] * n_out, out_shape=rider.out_shapes,
        scratch_shapes=rider.sems(), input_output_aliases=rider.aliases(0, 0))(*rider.arrays)


def _half(shape, which):
    for axis, size in enumerate(shape):
        if size % 2 == 0:
            return (slice(None),) * axis + (pl.ds(which * (size // 2), size // 2),)
    raise ValueError(f"no axis of even length in {shape}")


def _start_then_wait(copies):
    def start(*refs):
        for send, _ in copies(*refs):
            send.start()

    def finish(*refs):
        pairs = copies(*refs)
        for _, landing in pairs:
            landing.wait_recv()
        for send, _ in pairs:
            send.wait_send()

    return start, finish


def _fetch_rider(shards):
    n, n_peer = len(shards), N_SHARD - 1
    shapes = [s.shape for s in shards]

    def copies(ins, outs, send_sems, recv_sems):
        x, y, c = _place()
        pairs = []
        for a in range(n):
            mine = _half(shapes[a], c)
            for k, (px, py) in enumerate(_other_chips(x, y)):
                def into(slot):
                    return pltpu.make_async_remote_copy(
                        src_ref=ins[a].at[mine], dst_ref=outs[a].at[(slot,) + mine],
                        send_sem=send_sems.at[a * n_peer + k], recv_sem=recv_sems.at[a * n_peer + k],
                        device_id=(px, py, c), device_id_type=MESH)
                pairs.append((into(2 * x + y), into(2 * px + py)))
        return pairs

    start, finish = _start_then_wait(copies)
    return _Rider(shards, [jax.ShapeDtypeStruct((N_SHARD,) + s.shape, s.dtype) for s in shards], n * n_peer,
                  start, finish)


def _forward_rider(gathered, shards):
    n, n_peer = len(gathered), N_SHARD - 1
    shapes = [s.shape for s in shards]

    def copies(ins, outs, send_sems, recv_sems):
        x, y, c = _place()
        pairs = []
        for a in range(n):
            for k, (px, py) in enumerate(_other_chips(x, y)):
                def half_of_slot(which):
                    rows = outs[a].at[(2 * px + py,) + _half(shapes[a], which)]
                    return pltpu.make_async_remote_copy(
                        src_ref=rows, dst_ref=rows, send_sem=send_sems.at[a * n_peer + k],
                        recv_sem=recv_sems.at[a * n_peer + k], device_id=(x, y, 1 - c), device_id_type=MESH)
                pairs.append((half_of_slot(c), half_of_slot(1 - c)))
        return pairs

    start, finish = _start_then_wait(copies)
    return _Rider(gathered, [jax.ShapeDtypeStruct(g.shape, g.dtype) for g in gathered], n * n_peer, start, finish,
                  in_place=True)


def _sibling_rider(arrs, other_half):
    n = len(arrs)

    def copies(ins, outs, send_sems, recv_sems):
        x, y, c = _place()
        pairs = []
        for a in range(n):
            cp = pltpu.make_async_remote_copy(
                src_ref=ins[a].at[1 - c] if other_half else ins[a], dst_ref=outs[a], send_sem=send_sems.at[a],
                recv_sem=recv_sems.at[a], device_id=(x, y, 1 - c), device_id_type=MESH)
            pairs.append((cp, cp))
        return pairs

    start, finish = _start_then_wait(copies)
    return _Rider(arrs, [jax.ShapeDtypeStruct(g.shape[1:] if other_half else g.shape, g.dtype) for g in arrs], n,
                  start, finish)


def _owner_rider(parts):
    n, n_peer = len(parts), N_SHARD - 1

    def copies(ins, outs, send_sems, recv_sems):
        x, y, c = _place()
        pairs = []
        for a in range(n):
            for k, (px, py) in enumerate(_other_chips(x, y)):
                cp = pltpu.make_async_remote_copy(
                    src_ref=ins[a].at[2 * px + py], dst_ref=outs[a].at[k], send_sem=send_sems.at[a * n_peer + k],
                    recv_sem=recv_sems.at[a * n_peer + k], device_id=(px, py, c), device_id_type=MESH)
                pairs.append((cp, cp))
        return pairs

    start, finish = _start_then_wait(copies)
    return _Rider(parts, [jax.ShapeDtypeStruct((n_peer,) + p.shape[1:], p.dtype) for p in parts], n * n_peer,
                  start, finish)


def _gather_devices(buf, name):
    n_peer = N_DEV - 1

    def body(in_ref, out_ref, send_sems, recv_sems, local_sem):
        x, y, c = _place()
        me = 4 * x + 2 * y + c
        mine = pltpu.make_async_copy(in_ref, out_ref.at[me], local_sem)
        mine.start()
        peers = []
        for k in range(1, N_DEV):
            fx, fy, fc = (k >> 2) & 1, (k >> 1) & 1, k & 1
            peers.append((x ^ fx, y ^ fy, c ^ fc))
        sends = []
        for k, peer in enumerate(peers):
            cp = pltpu.make_async_remote_copy(
                src_ref=in_ref, dst_ref=out_ref.at[me], send_sem=send_sems.at[k], recv_sem=recv_sems.at[k],
                device_id=peer, device_id_type=MESH)
            cp.start()
            sends.append(cp)
        for k, (px, py, pc) in enumerate(peers):
            pltpu.make_async_remote_copy(
                src_ref=in_ref, dst_ref=out_ref.at[4 * px + 2 * py + pc], send_sem=send_sems.at[k],
                recv_sem=recv_sems.at[k], device_id=(px, py, pc), device_id_type=MESH).wait_recv()
        for cp in sends:
            cp.wait_send()
        mine.wait()

    return pl.pallas_call(
        body, name=name, in_specs=[_ANY], out_specs=_ANY,
        out_shape=jax.ShapeDtypeStruct((N_DEV,) + buf.shape, buf.dtype),
        scratch_shapes=[pltpu.SemaphoreType.DMA((n_peer,)), pltpu.SemaphoreType.DMA((n_peer,)),
                        pltpu.SemaphoreType.DMA(())],
    )(buf)


class _Dims:
    def __init__(self, x, p, w_in, sgu_norm, conv_w, kv_norm, w_ukv, w_out):
        self.t, self.d = x.shape[1], x.shape[2]
        self.depth = w_in.shape[0]
        self.ple = p.shape[3]
        self.in_w = w_in.shape[2] * N_SHARD
        self.ah = sgu_norm.shape[1]
        self.aw = self.ah * HEAD
        self.bw = conv_w.shape[2] * N_SHARD
        self.kvr = kv_norm.shape[1]
        self.ch = w_ukv.shape[2] * N_SHARD // (2 * HEAD)
        self.cw = self.ch * HEAD
        self.mix = w_out.shape[1] * N_SHARD
        assert self.mix == self.aw + self.bw + self.cw and self.aw == self.bw
        self.qw = self.ch * 2 * HEAD
        segs = [('a', 3 * self.aw, 3 * self.aw), ('ckv', self.kvr, self.kvr), ('b', 4 * self.bw, 4 * self.bw),
                ('q', self.qw, self.qw), ('cz', self.cw, self.cw), ('kr', HEAD, HEAD)]
        off = 0
        self.off = {}
        for nm, width, align in segs:
            off = -(-off // align) * align
            self.off[nm] = off
            off += width
        self.inp = -(-off // 512) * 512
        q_real = self.ch * (HEAD + ROPE)
        widths = [3 * self.aw, 4 * self.bw, q_real, self.kvr, ROPE, self.cw]
        assert sum(widths) == self.in_w
        starts = [0]
        for wd in widths:
            starts.append(starts[-1] + wd)
        self.src = dict(zip(['a', 'b', 'q', 'ckv', 'kr', 'cz'], zip(starts[:-1], widths)))


def _rearrange_w_in(w, dm):
    lead, d = w.shape[:-2], w.shape[-1]
    axis = w.ndim - 2

    def rows(nm):
        s, wd = dm.src[nm]
        return lax.slice_in_dim(w, s, s + wd, axis=axis)

    pieces = {nm: rows(nm) for nm in ('a', 'b', 'ckv', 'cz')}
    q = rows('q').reshape(lead + (dm.ch, HEAD + ROPE, d))
    pieces['q'] = jnp.pad(q, [(0, 0)] * (len(lead) + 1) + [(0, HEAD - ROPE), (0, 0)]).reshape(lead + (dm.qw, d))
    pieces['kr'] = jnp.pad(rows('kr'), [(0, 0)] * len(lead) + [(0, HEAD - ROPE), (0, 0)])
    out, cur = [], 0
    for nm in sorted(dm.off, key=lambda k: dm.off[k]):
        if dm.off[nm] > cur:
            out.append(jnp.zeros(lead + (dm.off[nm] - cur, d), w.dtype))
        out.append(pieces[nm])
        cur = dm.off[nm] + pieces[nm].shape[axis]
    if dm.inp > cur:
        out.append(jnp.zeros(lead + (dm.inp - cur, d), w.dtype))
    return jnp.concatenate(out, axis=axis)


def _unarrange_w_in(g, dm):
    d = g.shape[1]

    def seg(nm, width):
        return g[dm.off[nm]:dm.off[nm] + width]

    q = seg('q', dm.qw).reshape(dm.ch, 2 * HEAD, d)[:, :HEAD + ROPE].reshape(dm.ch * (HEAD + ROPE), d)
    return jnp.concatenate([seg('a', 3 * dm.aw), seg('b', 4 * dm.bw), q, seg('ckv', dm.kvr), seg('kr', ROPE),
                            seg('cz', dm.cw)], axis=0)


def _finish_dproj(dproj, parts, widths, dm):
    t = dproj.shape[0]
    for nm, part in parts.items():
        dproj = lax.dynamic_update_slice(dproj, part, (0, dm.off[nm]))
    cur = 0
    for nm in sorted(dm.off, key=lambda k: dm.off[k]):
        if dm.off[nm] > cur:
            dproj = lax.dynamic_update_slice(dproj, jnp.zeros((t, dm.off[nm] - cur), BF16), (0, cur))
        cur = dm.off[nm] + widths[nm]
    assert cur == dm.inp
    return dproj


def _rope_tables(positions):
    inv = 1.0 / (ROPE_BASE ** (jnp.arange(0, ROPE, 2, dtype=F32) / ROPE))
    ang = positions.astype(F32)[:, None] * inv
    cos, sin = jnp.cos(ang), jnp.sin(ang)
    t = positions.shape[0]
    half = ROPE // 2
    cos_t = jnp.concatenate([cos, cos, jnp.zeros((t, HEAD - ROPE), F32)], axis=-1)
    sin_a = jnp.concatenate([-sin, jnp.zeros((t, HEAD - half), F32)], axis=-1)
    sin_b = jnp.concatenate([jnp.zeros((t, half), F32), sin, jnp.zeros((t, HEAD - ROPE), F32)], axis=-1)
    return cos_t, sin_a, sin_b


def _pad_gain(g):
    return jnp.pad(g, (0, HEAD - g.shape[0]))[None, :]


def _shard_major(g, axis):
    shape = g.shape
    g = g.reshape(shape[:axis] + (N_SHARD, shape[axis] // N_SHARD) + shape[axis + 1:])
    g = jnp.moveaxis(g, axis, 0)
    rows, cols = g.shape[1], g.shape[2]
    return jnp.swapaxes(g.reshape(N_SHARD, 2, rows // 2, cols), 0, 1)


def _pack(arrs):
    flat = jnp.concatenate([a.reshape(-1) for a in arrs])
    pad = (-flat.shape[0]) % (8 * HEAD)
    return jnp.pad(flat, (0, pad)).reshape(-1, HEAD)


def _unpack(buf, shapes):
    flat = buf.reshape(-1)
    out, cur = [], 0
    for s in shapes:
        size = 1
        for v in s:
            size *= v
        out.append(flat[cur:cur + size].reshape(s))
        cur += size
    return out


def kernel(x, p, positions, attn_norm, w_in, sgu_norm, w_spatial, b_spatial, conv_w, conv_b, kv_norm, w_ukv, q_nope_norm, q_rope_norm, k_nope_norm, k_rope_norm, out_norm, w_out, ple_norm, w_ple_gate, w_ple_proj, loss_target, m_attn_norm, m_w_in, m_sgu_norm, m_w_spatial, m_b_spatial, m_conv_w, m_conv_b, m_kv_norm, m_w_ukv, m_q_nope_norm, m_q_rope_norm, m_k_nope_norm, m_k_rope_norm, m_out_norm, m_w_out, m_ple_norm, m_w_ple_gate, m_w_ple_proj, v_attn_norm, v_w_in, v_sgu_norm, v_w_spatial, v_b_spatial, v_conv_w, v_conv_b, v_kv_norm, v_w_ukv, v_q_nope_norm, v_q_rope_norm, v_k_nope_norm, v_k_rope_norm, v_out_norm, v_w_out, v_ple_norm, v_w_ple_gate, v_w_ple_proj):
    weights = dict(attn_norm=attn_norm, w_in=w_in, sgu_norm=sgu_norm, w_spatial=w_spatial, b_spatial=b_spatial,
                   conv_w=conv_w, conv_b=conv_b, kv_norm=kv_norm, w_ukv=w_ukv, q_nope_norm=q_nope_norm,
                   q_rope_norm=q_rope_norm, k_nope_norm=k_nope_norm, k_rope_norm=k_rope_norm, out_norm=out_norm,
                   w_out=w_out, ple_norm=ple_norm, w_ple_gate=w_ple_gate, w_ple_proj=w_ple_proj)
    mom_m = dict(attn_norm=m_attn_norm, w_in=m_w_in, sgu_norm=m_sgu_norm, w_spatial=m_w_spatial,
                 b_spatial=m_b_spatial, conv_w=m_conv_w, conv_b=m_conv_b, kv_norm=m_kv_norm, w_ukv=m_w_ukv,
                 q_nope_norm=m_q_nope_norm, q_rope_norm=m_q_rope_norm, k_nope_norm=m_k_nope_norm,
                 k_rope_norm=m_k_rope_norm, out_norm=m_out_norm, w_out=m_w_out, ple_norm=m_ple_norm,
                 w_ple_gate=m_w_ple_gate, w_ple_proj=m_w_ple_proj)
    mom_v = dict(attn_norm=v_attn_norm, w_in=v_w_in, sgu_norm=v_sgu_norm, w_spatial=v_w_spatial,
                 b_spatial=v_b_spatial, conv_w=v_conv_w, conv_b=v_conv_b, kv_norm=v_kv_norm, w_ukv=v_w_ukv,
                 q_nope_norm=v_q_nope_norm, q_rope_norm=v_q_rope_norm, k_nope_norm=v_k_nope_norm,
                 k_rope_norm=v_k_rope_norm, out_norm=v_out_norm, w_out=v_w_out, ple_norm=v_ple_norm,
                 w_ple_gate=v_w_ple_gate, w_ple_proj=v_w_ple_proj)
    dm = _Dims(x, p, w_in, sgu_norm, conv_w, kv_norm, w_ukv, w_out)
    for group in (weights, mom_m, mom_v):
        group['w_in'] = jnp.swapaxes(group['w_in'], 1, 2)
    t, d, depth = dm.t, dm.d, dm.depth
    shard = 2 * lax.axis_index("x") + lax.axis_index("y")
    core = lax.axis_index("c")
    scale = float(HEAD + ROPE) ** -0.5

    def local_layer(i):
        return [weights[n][i:i + 1].astype(BF16) for n in BIG]

    def fill_own_slot(gathered, local):
        return [lax.dynamic_update_slice(g, mine[None], (shard,) + (0,) * mine.ndim)
                for g, mine in zip(gathered, local)]

    def layer_weights(filled):
        w = {n: jnp.concatenate([filled[j][s] for s in range(N_SHARD)], axis=BIG_AXIS[n])
             for j, n in enumerate(BIG)}
        w['w_in'] = _rearrange_w_in(w['w_in'], dm)
        return w

    first = local_layer(0) + [conv_w]
    fetched = _comm_call(_fetch_rider(first), "fetch_weights_l0")
    filled = fill_own_slot(_comm_call(_forward_rider(fetched, first), "forward_weights_l0"), first)
    layer_w = [layer_weights(filled)] + [None] * (depth - 1)
    conv_w_full = jnp.concatenate([filled[len(BIG)][s] for s in range(N_SHARD)], axis=2)

    tabs = _rope_tables(positions[0])
    h = x[0]
    saved = []
    for i in range(depth):
        tag = f"l{i}_"
        ga, gb, gc = (out_norm[i][None, :dm.aw], out_norm[i][None, dm.aw:dm.aw + dm.bw],
                      out_norm[i][None, dm.aw + dm.bw:])
        ws_b = w_spatial[i].astype(BF16)
        bb = jnp.broadcast_to(b_spatial[i][:, :, None], (dm.ah, HEAD, HEAD))
        qn_g, qr_g = q_nope_norm[i][None, :], _pad_gain(q_rope_norm[i])
        kn_g, kr_g = k_nope_norm[i][None, :], _pad_gain(k_rope_norm[i])
        kv_g = kv_norm[i][None, :]
        wl = layer_w[i]
        nxt = local_layer(i + 1) if i + 1 < depth else None
        hn = _norm_fwd(h, attn_norm[i][None, :], tag + "norm1")
        if nxt is None:
            proj = _matmul(hn, wl['w_in'], 'nt', BF16, tag + "proj", b_layer=0)
        else:
            proj, fetched = _matmul(hn, wl['w_in'], 'nt', BF16, tag + "proj", b_layer=0, rider=_fetch_rider(nxt))
        y = _sgu_fwd(proj, dm.off['a'], dm.aw, sgu_norm[i], ws_b, bb, ga, tag + "sgu",
                     into=(jnp.zeros((t, dm.mix), BF16), 0))
        y, yconv = _conv_fwd(proj, dm.off['b'], dm.bw, conv_w_full[i], conv_b[i][None, :], gb, tag + "conv",
                             into=(y, _col_block(dm.aw, dm.bw)))
        q_cat, ckv_n, kr_rot = _mla_prep_fwd(proj, dm.off['q'], dm.off['ckv'], dm.off['kr'], dm.ch, dm.kvr, tabs,
                                             qn_g, qr_g, kr_g, kv_g, tag + "mla_prep")
        kv = _matmul(ckv_n, wl['w_ukv'], 'nn', F32, tag + "kv_up", b_layer=0)
        k_cat, k_cat_t, v_aug = _kv_prep_fwd(kv, kr_rot, dm.ch, kn_g, tag + "kv_prep")
        o, lse = _attn_fwd(q_cat, k_cat, v_aug, dm.ch, scale, tag + "attn")
        y = _attn_post_fwd(o, proj, dm.off['cz'], dm.cw, gc, tag + "attn_post",
                           into=(y, _col_block(dm.aw + dm.bw, dm.cw)))
        if nxt is None:
            h1 = _matmul(y, wl['w_out'], 'nn', F32, tag + "out", add=h, b_layer=0)
        else:
            h1, gathered = _matmul(y, wl['w_out'], 'nn', F32, tag + "out", add=h, b_layer=0,
                                   rider=_forward_rider(list(fetched), nxt))
            layer_w[i + 1] = layer_weights(fill_own_slot(gathered, nxt))
        hn2 = _norm_fwd(h1, ple_norm[i][None, :], tag + "norm2")
        gpre = _matmul(hn2, wl['w_ple_gate'], 'nn', F32, tag + "gate", b_layer=0)
        p_b = p[i, 0].astype(BF16)
        pp = _matmul(p_b, wl['w_ple_proj'], 'nn', F32, tag + "ple_proj", b_layer=0)
        h2 = _ple_fwd(h1, gpre, pp, tag + "ple")
        saved.append(dict(h=h, hn=hn, proj=proj, yconv=yconv, q_cat=q_cat, ckv_n=ckv_n, kv=kv, k_cat=k_cat,
                          k_cat_t=k_cat_t,
                          v=v_aug, o=o, lse=lse, y=y, h1=h1, hn2=hn2, gpre=gpre, pp=pp, p_b=p_b, ws_b=ws_b, bb=bb,
                          gains=(ga, gb, gc, qn_g, qr_g, kn_g, kr_g, kv_g)))
        h = h2

    loss_part, dh = _loss_and_grad(h, loss_target[0], "loss")
    loss = lax.psum(loss_part[0, 0], ("x", "y", "c"))

    core_flag = core.astype(F32).reshape(1, 1)

    def chip_sums(sm, from_sibling, tag):
        return sm, [_pair_sum_bf16(a, core_flag, b, f"{tag}chip_sum_{n}") for a, b, n in zip(sm, from_sibling, BIG)]

    def shard_sums(sm, from_sibling, from_chips, tag):
        out = []
        for a, b, r3, n in zip(sm, from_sibling, from_chips, BIG):
            own_a = lax.dynamic_slice(a, (core, shard, 0, 0), (1, 1) + a.shape[2:]).reshape(a.shape[2:])
            own_b = lax.dynamic_index_in_dim(b, shard, 0, keepdims=False)
            out.append(_shard_sum(own_a, own_b, r3, f"{tag}shard_sum_{n}"))
        return out

    grads = {n: [None] * depth for n in WEIGHTS}
    own_half = {n: [None] * depth for n in BIG}
    sibling_half = {n: [None] * depth for n in BIG}
    carry = None
    for i in reversed(range(depth)):
        tag = f"l{i}_b_"
        gtag = f"l{i + 1}_g_"
        sv = saved[i]
        wl = layer_w[i]
        ga, gb, gc, qn_g, qr_g, kn_g, kr_g, kv_g = sv['gains']
        proj = sv['proj']
        dgpre, dpp = _ple_bwd(sv['gpre'], sv['pp'], dh, tag + "ple")
        grads['w_ple_proj'][i] = _matmul(sv['p_b'], dpp, 'tn', F32, tag + "d_w_ple_proj")
        if carry is None:
            grads['w_ple_gate'][i] = _matmul(sv['hn2'], dgpre, 'tn', F32, tag + "d_w_gate")
        else:
            grads['w_ple_gate'][i], from_sibling = _matmul(sv['hn2'], dgpre, 'tn', F32, tag + "d_w_gate",
                                                           rider=_sibling_rider(carry, True))
            mine, sums = chip_sums(carry, from_sibling, gtag)
        d_hn2 = _matmul(dgpre, wl['w_ple_gate'], 'nt', BF16, tag + "d_hn2", b_layer=0)
        dh1, dh1_b, g_ple = _norm_bwd(sv['h1'], ple_norm[i][None, :], d_hn2, dh, tag + "norm2")
        grads['ple_norm'][i] = g_ple[0]
        grads['w_out'][i] = _matmul(sv['y'], dh1_b, 'tn', F32, tag + "d_w_out")
        dy = _matmul(dh1_b, wl['w_out'], 'nt', BF16, tag + "d_y", b_layer=0)
        ws_t = jnp.swapaxes(sv['ws_b'], 1, 2)
        dproj, g_sgu, g_ws, g_bs, g_ga = _sgu_bwd(proj, dm.off['a'], dm.aw, sgu_norm[i], sv['ws_b'], ws_t, sv['bb'],
                                                  ga, dy, dm.inp, tag + "sgu")
        grads['sgu_norm'][i], grads['w_spatial'][i], grads['b_spatial'][i] = g_sgu, g_ws, g_bs[:, :, 0]
        dyc, d_bb, d_bz, g_gb, g_cb = _conv_bwd_gate(proj, dm.off['b'], dm.bw, sv['yconv'], gb, dy, tag + "conv_gate")
        dproj, g_cw = _conv_bwd_taps(proj, dm.off['b'], dm.bw, dyc, conv_w_full[i], d_bb, d_bz, tag + "conv_taps",
                                     into=(dproj, _col_block(dm.off['b'], 4 * dm.bw)))
        grads['conv_b'][i], grads['conv_w'][i] = g_cb[0], g_cw
        dproj, d_o, dsum, g_gc = _attn_post_bwd(sv['o'], proj, dm.off['cz'], dm.cw, gc, dy,
                                                _col_block(dm.aw + dm.bw, dm.cw), tag + "attn_post",
                                                into=(dproj, _col_block(dm.off['cz'], dm.cw)))
        grads['out_norm'][i] = jnp.concatenate([g_ga[0], g_gb[0], g_gc[0]])
        dq_t, dk_cat, dv = _attn_bwd(sv['q_cat'], sv['k_cat'], sv['k_cat_t'], sv['v'], d_o,
                                     sv['lse'].reshape(dm.ch, 1, t), dsum.reshape(dm.ch, 1, t), dm.ch, scale,
                                     tag + "attn_bwd")
        dkv, dkr_rot, g_kn = _kv_prep_bwd(sv['kv'], dm.ch, kn_g, dk_cat, dv, tag + "kv_prep")
        grads['k_nope_norm'][i] = g_kn[0]
        grads['w_ukv'][i] = _matmul(sv['ckv_n'], dkv, 'tn', F32, tag + "d_w_ukv")
        dckv_n = _matmul(dkv, wl['w_ukv'], 'nt', BF16, tag + "d_ckv", b_layer=0)
        kr_width = dm.inp - dm.off['kr']
        dproj, d_ckv, d_kr, g_qn, g_qr, g_kr, g_kv = _mla_prep_bwd(
            proj, dm.off['q'], dm.off['ckv'], dm.off['kr'], dm.ch, dm.kvr, tabs, qn_g, qr_g, kr_g, kv_g,
            dq_t, scale, dckv_n, dkr_rot, kr_width, tag + "mla_prep", into=(dproj, _col_block(dm.off['q'], dm.qw)))
        grads['q_nope_norm'][i], grads['q_rope_norm'][i] = g_qn[0], g_qr[0, :ROPE]
        grads['k_rope_norm'][i], grads['kv_norm'][i] = g_kr[0, :ROPE], g_kv[0]
        dproj = _finish_dproj(dproj, dict(ckv=d_ckv, kr=d_kr),
                              dict(a=3 * dm.aw, b=4 * dm.bw, ckv=dm.kvr, q=dm.qw, cz=dm.cw, kr=kr_width), dm)
        if carry is None:
            d_w_in = _matmul(dproj, sv['hn'], 'tn', F32, tag + "d_w_in")
            d_hn = _matmul(dproj, wl['w_in'], 'nn', BF16, tag + "d_hn", b_layer=0)
        else:
            d_w_in, from_chips = _matmul(dproj, sv['hn'], 'tn', F32, tag + "d_w_in", rider=_owner_rider(sums))
            halves = shard_sums(mine, from_sibling, from_chips, gtag)
            d_hn, from_core = _matmul(dproj, wl['w_in'], 'nn', BF16, tag + "d_hn", b_layer=0,
                                      rider=_sibling_rider(halves, False))
            for n, own, recv in zip(BIG, halves, from_core):
                own_half[n][i + 1], sibling_half[n][i + 1] = own, recv
        grads['w_in'][i] = _unarrange_w_in(d_w_in, dm)
        dh, _, g_an = _norm_bwd(sv['h'], attn_norm[i][None, :], d_hn, dh1, tag + "norm1")
        grads['attn_norm'][i] = g_an[0]
        carry = [_shard_major(grads[n][i], BIG_AXIS[n] - 1) for n in BIG]
    grad_x = dh[None]

    from_sibling = _comm_call(_sibling_rider(carry, True), "l0_g_to_sibling")
    mine, sums = chip_sums(carry, from_sibling, "l0_g_")
    from_chips = _comm_call(_owner_rider(sums), "l0_g_to_owner_chips")
    halves = shard_sums(mine, from_sibling, from_chips, "l0_g_")
    from_core = _comm_call(_sibling_rider(halves, False), "l0_g_share_sibling")
    for n, own, recv in zip(BIG, halves, from_core):
        own_half[n][0], sibling_half[n][0] = own, recv

    out_g, out_d, out_m, out_v = {}, {}, {}, {}
    for n in BIG:
        out_g[n], out_d[n], out_m[n], out_v[n] = _adamw_two_halves(
            weights[n], jnp.stack(own_half[n]), jnp.stack(sibling_half[n]), core_flag, mom_m[n], mom_v[n],
            f"adamw_{n}")
    for out in (out_g, out_d, out_m, out_v):
        out['w_in'] = jnp.swapaxes(out['w_in'], 1, 2)
    grads = {n: jnp.stack(grads[n]) for n in SMALL}

    shapes = [grads[n].shape for n in SMALL]
    summed = _unpack(_sum_devices(_gather_devices(_pack([grads[n] for n in SMALL]), "gather_small_grads"),
                                  "sum_small_grads"), shapes)
    small_g = dict(zip(SMALL, summed))
    small_g['conv_w'] = lax.dynamic_slice_in_dim(small_g['conv_w'], shard * conv_w.shape[2], conv_w.shape[2], axis=2)
    local_shapes = [weights[n].shape for n in SMALL]
    d_s, m_s, v_s = _adamw(_pack([weights[n] for n in SMALL]), _pack([small_g[n] for n in SMALL]),
                           _pack([mom_m[n] for n in SMALL]), _pack([mom_v[n] for n in SMALL]), "adamw_small")
    for n, dd, mm, vv in zip(SMALL, _unpack(d_s, local_shapes), _unpack(m_s, local_shapes),
                             _unpack(v_s, local_shapes)):
        out_g[n], out_d[n], out_m[n], out_v[n] = small_g[n], dd, mm, vv

    return (loss, grad_x, *[out_g[n] for n in WEIGHTS], *[out_d[n] for n in WEIGHTS],
            *[out_m[n] for n in WEIGHTS], *[out_v[n] for n in WEIGHTS])
```

```python
import functools

import jax
import jax.numpy as jnp
from jax import lax
from jax.experimental import pallas as pl
from jax.experimental.pallas import tpu as pltpu

F32 = jnp.float32
BF16 = jnp.bfloat16
EPS = 1e-6
HEAD = 128
ROPE = 64
ROPE_BASE = 10000.0
CONV_TAPS = 3
N_SHARD = 4
N_DEV = 8
ADAM_LR = 0.001
ADAM_B1 = 0.9
ADAM_B2 = 0.999
ADAM_EPS = 1e-08
ADAM_WD = 0.01
ADAM_STEP = 10
MESH = pl.DeviceIdType.MESH
VMEM_LIMIT = 56 * 1024 * 1024
HALO_ROWS = 16
ROW_TILES = (512, 256, 128)

WEIGHTS = ['attn_norm', 'w_in', 'sgu_norm', 'w_spatial', 'b_spatial', 'conv_w', 'conv_b', 'kv_norm', 'w_ukv',
           'q_nope_norm', 'q_rope_norm', 'k_nope_norm', 'k_rope_norm', 'out_norm', 'w_out', 'ple_norm',
           'w_ple_gate', 'w_ple_proj']
BIG = ['w_in', 'w_ukv', 'w_out', 'w_ple_gate', 'w_ple_proj']
BIG_AXIS = {'w_in': 1, 'w_ukv': 2, 'w_out': 1, 'w_ple_gate': 1, 'w_ple_proj': 2}
SMALL = [n for n in WEIGHTS if n not in BIG]


def _pick(n, cands):
    for c in cands:
        if n % c == 0:
            return c
    return n


def _params(sem=None):
    return pltpu.CompilerParams(dimension_semantics=sem, vmem_limit_bytes=VMEM_LIMIT)


def _matmul(a, b, mode, out_dtype, name, add=None, b_layer=None, rider=None):
    b_shape = b.shape if b_layer is None else b.shape[1:]
    if mode == 'nn':
        (m, k), n = a.shape, b_shape[1]
    elif mode == 'nt':
        (m, k), n = a.shape, b_shape[0]
    else:
        (k, m), n = a.shape, b_shape[1]
    tm = _pick(m, (1280, 1024, 512, 256, 128))
    tn = _pick(n, ((2048,) if out_dtype == BF16 and add is None else ()) + (1536, 1024, 512, 256, 128))
    tk = k if k <= 2048 else _pick(k, (2048, 1536, 1024, 512, 256, 128))
    nk = k // tk
    if mode == 'tn':
        a_spec = pl.BlockSpec((tk, tm), lambda i, j, kk: (kk, i))
        dims = (((0,), (0,)), ((), ()))
    else:
        a_spec = pl.BlockSpec((tm, tk), lambda i, j, kk: (i, kk))
        dims = (((1,), (0,)), ((), ())) if mode == 'nn' else (((1,), (1,)), ((), ()))
    b_block = (tn, tk) if mode == 'nt' else (tk, tn)
    if b_layer is None:
        b_spec = pl.BlockSpec(b_block, (lambda i, j, kk: (j, kk)) if mode == 'nt' else (lambda i, j, kk: (kk, j)))
    else:
        b_spec = pl.BlockSpec((None,) + b_block, (lambda i, j, kk: (b_layer, j, kk)) if mode == 'nt'
                              else (lambda i, j, kk: (b_layer, kk, j)))
    o_spec = pl.BlockSpec((tm, tn), lambda i, j, kk: (i, j))
    has_add = add is not None

    def body(*refs):
        a_ref, b_ref = refs[0], refs[1]
        add_ref = refs[2] if has_add else None
        o_ref = refs[3] if has_add else refs[2]

        def product():
            return lax.dot_general(a_ref[...], b_ref[...], dims, preferred_element_type=F32)

        def finish(res):
            if has_add:
                res = res + add_ref[...]
            o_ref[...] = res.astype(out_dtype)

        if nk == 1:
            finish(product())
        else:
            acc_ref = refs[-1]
            kk = pl.program_id(2)

            @pl.when(kk == 0)
            def _():
                acc_ref[...] = product()

            @pl.when((kk > 0) & (kk < nk - 1))
            def _():
                acc_ref[...] += product()

            @pl.when(kk == nk - 1)
            def _():
                finish(acc_ref[...] + product())

    in_specs = [a_spec, b_spec] + ([o_spec] if has_add else [])
    args = [a, b] + ([add] if has_add else [])
    grid = (m // tm, n // tn, nk)
    scratch = [pltpu.VMEM((tm, tn), F32)] if nk > 1 else []
    if rider is None:
        return pl.pallas_call(
            body, name=name, grid=grid, in_specs=in_specs, out_specs=o_spec,
            out_shape=jax.ShapeDtypeStruct((m, n), out_dtype), scratch_shapes=scratch,
            compiler_params=_params(("parallel", "parallel", "arbitrary")),
        )(*args)

    n_in, n_rin, n_rout = len(args), len(rider.arrays), len(rider.out_shapes)

    def body_with_rider(*refs):
        r_in = refs[n_in:n_in + n_rin]
        r_out = refs[n_in + n_rin + 1:n_in + n_rin + 1 + n_rout]
        own = refs[:n_in] + refs[n_in + n_rin:n_in + n_rin + 1] + refs[n_in + n_rin + 1 + n_rout:len(refs) - 2]
        send_sems, recv_sems = refs[-2:]
        ids = [pl.program_id(ax) for ax in range(3)]

        @pl.when((ids[0] == 0) & (ids[1] == 0) & (ids[2] == 0))
        def _():
            rider.start(r_in, r_out, send_sems, recv_sems)

        body(*own)

        @pl.when((ids[0] == grid[0] - 1) & (ids[1] == grid[1] - 1) & (ids[2] == grid[2] - 1))
        def _():
            rider.finish(r_in, r_out, send_sems, recv_sems)

    res = pl.pallas_call(
        body_with_rider, name=name, grid=grid, in_specs=in_specs + [_ANY] * n_rin,
        out_specs=[o_spec] + [_ANY] * n_rout,
        out_shape=[jax.ShapeDtypeStruct((m, n), out_dtype)] + rider.out_shapes,
        scratch_shapes=scratch + rider.sems(), input_output_aliases=rider.aliases(n_in, 1),
        compiler_params=_params(("arbitrary", "arbitrary", "arbitrary")),
    )(*args, *rider.arrays)
    return res[0], res[1:]


def _rms(x, n):
    r = lax.rsqrt(jnp.sum(x * x, axis=-1, keepdims=True) * (1.0 / n) + EPS)
    return x * r, r


def _rms_bwd(dxhat, xhat, r, n):
    return r * (dxhat - xhat * (jnp.sum(dxhat * xhat, axis=-1, keepdims=True) * (1.0 / n)))


def _sigmoid(z):
    return 1.0 / (1.0 + jnp.exp(-z))


def _silu_and_grad(z):
    sig = _sigmoid(z)
    return z * sig, sig * (1.0 + z * (1.0 - sig))


def _colsum(x):
    return jnp.sum(x, axis=0, keepdims=True)


def _rope(t, cos_t, sin_a, sin_b):
    return t * cos_t + pltpu.roll(t, 96, 1) * sin_a + pltpu.roll(t, 32, 1) * sin_b


def _rope_bwd(d, cos_t, sin_a, sin_b):
    return d * cos_t + pltpu.roll(d * sin_a, 32, 1) + pltpu.roll(d * sin_b, 96, 1)


def _shift_down(g, first_row):
    row = lax.broadcasted_iota(jnp.int32, g.shape, 0)
    return jnp.where(row == 0, first_row, pltpu.roll(g, 1, 0))


def _shift_up(g, last_row):
    n = g.shape[0]
    row = lax.broadcasted_iota(jnp.int32, g.shape, 0)
    return jnp.where(row == n - 1, last_row, pltpu.roll(g, n - 1, 0))


def _row_spec(r, w, col=0):
    return pl.BlockSpec((r, w), lambda i: (i, col))


def _const_spec(shape):
    nd = len(shape)
    return pl.BlockSpec(shape, lambda i: (0,) * nd)


def _col_block(off, w):
    assert off % w == 0, (off, w)
    return off // w


def _zero_at_first_step(refs):
    @pl.when(pl.program_id(0) == 0)
    def _():
        for ref in refs:
            ref[...] = jnp.zeros(ref.shape, ref.dtype)


def _row_call(body, name, t, r, in_specs, args, out_specs, out_shapes, scratch=(), into=None):
    if into is None:
        return pl.pallas_call(
            body, name=name, grid=(t // r,), in_specs=in_specs, out_specs=out_specs, out_shape=out_shapes,
            scratch_shapes=list(scratch), compiler_params=_params(("arbitrary",)),
        )(*args)
    buf, col = into
    single = not isinstance(out_specs, (list, tuple))
    specs = [out_specs] if single else list(out_specs)
    shapes = [out_shapes] if single else list(out_shapes)
    width = shapes[0].shape[1]
    assert shapes[0].dtype == buf.dtype and buf.shape[0] == t
    specs[0] = _row_spec(r, width, col)
    shapes[0] = jax.ShapeDtypeStruct(buf.shape, buf.dtype)
    n_in = len(args)

    def body_in_place(*refs):
        body(*refs[:n_in], *refs[n_in + 1:])

    res = pl.pallas_call(
        body_in_place, name=name, grid=(t // r,), in_specs=list(in_specs) + [_ANY], out_specs=specs, out_shape=shapes,
        scratch_shapes=list(scratch), input_output_aliases={n_in: 0}, compiler_params=_params(("arbitrary",)),
    )(*args, buf)
    return res[0] if single else res


def _norm_fwd(h, g, name):
    t, d = h.shape
    r = _pick(t, ROW_TILES)

    def body(h_ref, g_ref, o_ref):
        xhat, _ = _rms(h_ref[...], d)
        o_ref[...] = (xhat * g_ref[...]).astype(BF16)

    return _row_call(body, name, t, r, [_row_spec(r, d), _const_spec((1, d))], (h, g),
                     _row_spec(r, d), jax.ShapeDtypeStruct((t, d), BF16))


def _norm_bwd(h, g, d_hn, d_res, name):
    t, d = h.shape
    r = _pick(t, ROW_TILES)

    def body(h_ref, g_ref, dy_ref, dres_ref, dh_ref, dhb_ref, dg_ref):
        _zero_at_first_step([dg_ref])
        xhat, rr = _rms(h_ref[...], d)
        dy = dy_ref[...].astype(F32)
        dg_ref[...] += _colsum(dy * xhat)
        dh = dres_ref[...] + _rms_bwd(dy * g_ref[...], xhat, rr, d)
        dh_ref[...] = dh
        dhb_ref[...] = dh.astype(BF16)

    return _row_call(body, name, t, r,
                     [_row_spec(r, d), _const_spec((1, d)), _row_spec(r, d), _row_spec(r, d)], (h, g, d_hn, d_res),
                     [_row_spec(r, d), _row_spec(r, d), _const_spec((1, d))],
                     [jax.ShapeDtypeStruct((t, d), F32), jax.ShapeDtypeStruct((t, d), BF16),
                      jax.ShapeDtypeStruct((1, d), F32)])


def _sgu_scores(v, gs_ref, ws_ref, bb_ref, s_scr, r, ah, keep=None):
    for kk in range(r // HEAD):
        for hh in range(ah):
            rows, cols = slice(kk * HEAD, (kk + 1) * HEAD), slice(hh * HEAD, (hh + 1) * HEAD)
            vhat, rv = _rms(v[rows, cols], HEAD)
            vn = vhat * gs_ref[pl.ds(hh, 1), :]
            s_scr[rows, cols] = jnp.dot(ws_ref[hh], vn.astype(BF16), preferred_element_type=F32) + bb_ref[hh]
            if keep is not None:
                keep[(kk, hh)] = (vhat, rv, vn)


def _sgu_fwd(proj, off, aw, gs, ws, bb, ga, name, into=None):
    t = proj.shape[0]
    ah = aw // HEAD
    r = _pick(t, ROW_TILES)
    cb = _col_block(off, aw)

    def body(u_ref, v_ref, z_ref, gs_ref, ws_ref, bb_ref, ga_ref, o_ref, s_scr):
        _sgu_scores(v_ref[...].astype(F32), gs_ref, ws_ref, bb_ref, s_scr, r, ah)
        sil, _ = _silu_and_grad(z_ref[...].astype(F32))
        yhat, _ = _rms(u_ref[...].astype(F32) * s_scr[...] * sil, aw)
        o_ref[...] = (yhat * ga_ref[...]).astype(BF16)

    return _row_call(
        body, name, t, r,
        [_row_spec(r, aw, cb), _row_spec(r, aw, cb + 1), _row_spec(r, aw, cb + 2), _const_spec((ah, HEAD)),
         _const_spec((ah, HEAD, HEAD)), _const_spec((ah, HEAD, HEAD)), _const_spec((1, aw))],
        (proj, proj, proj, gs, ws, bb, ga),
        _row_spec(r, aw), jax.ShapeDtypeStruct((t, aw), BF16), scratch=[pltpu.VMEM((r, aw), F32)], into=into)


def _sgu_bwd(proj, off, aw, gs, ws, ws_t, bb, ga, dy, out_width, name):
    t = proj.shape[0]
    ah = aw // HEAD
    r = _pick(t, ROW_TILES)
    assert off == 0
    cb = _col_block(off, aw)

    def body(u_ref, v_ref, z_ref, gs_ref, ws_ref, wst_ref, bb_ref, ga_ref, dy_ref,
             d_ref, dgs_ref, dws_ref, db_ref, dga_ref, s_scr, dv_scr):
        _zero_at_first_step([dgs_ref, dws_ref, db_ref, dga_ref])
        keep = {}
        _sgu_scores(v_ref[...].astype(F32), gs_ref, ws_ref, bb_ref, s_scr, r, ah, keep)
        u, z, s = u_ref[...].astype(F32), z_ref[...].astype(F32), s_scr[...]
        sil, dsil = _silu_and_grad(z)
        yhat, rr = _rms(u * s * sil, aw)
        dy_f = dy_ref[...].astype(F32)
        dga_ref[...] += _colsum(dy_f * yhat)
        dya = _rms_bwd(dy_f * ga_ref[...], yhat, rr, aw)
        d_ref[:, 0:aw] = (dya * s * sil).astype(BF16)
        d_ref[:, 2 * aw:3 * aw] = (dya * u * s * dsil).astype(BF16)
        ds = dya * u * sil
        for kk in range(r // HEAD):
            for hh in range(ah):
                rows, cols = slice(kk * HEAD, (kk + 1) * HEAD), slice(hh * HEAD, (hh + 1) * HEAD)
                vhat, rv, vn = keep[(kk, hh)]
                ds_blk = ds[rows, cols]
                db_ref[hh] += jnp.sum(ds_blk, axis=1, keepdims=True)
                ds_b = ds_blk.astype(BF16)
                dws_ref[hh] += lax.dot_general(ds_b, vn.astype(BF16), (((1,), (1,)), ((), ())),
                                               preferred_element_type=F32)
                dvn = jnp.dot(wst_ref[hh], ds_b, preferred_element_type=F32)
                dgs_ref[pl.ds(hh, 1), :] += _colsum(dvn * vhat)
                dv_scr[rows, cols] = _rms_bwd(dvn * gs_ref[pl.ds(hh, 1), :], vhat, rv, HEAD)
        d_ref[:, aw:2 * aw] = dv_scr[...].astype(BF16)

    return _row_call(
        body, name, t, r,
        [_row_spec(r, aw, cb), _row_spec(r, aw, cb + 1), _row_spec(r, aw, cb + 2), _const_spec((ah, HEAD)),
         _const_spec((ah, HEAD, HEAD)), _const_spec((ah, HEAD, HEAD)), _const_spec((ah, HEAD, HEAD)),
         _const_spec((1, aw)), _row_spec(r, aw, 0)],
        (proj, proj, proj, gs, ws, ws_t, bb, ga, dy),
        [_row_spec(r, 3 * aw), _const_spec((ah, HEAD)), _const_spec((ah, HEAD, HEAD)), _const_spec((ah, HEAD, 1)),
         _const_spec((1, aw))],
        [jax.ShapeDtypeStruct((t, out_width), BF16), jax.ShapeDtypeStruct((ah, HEAD), F32),
         jax.ShapeDtypeStruct((ah, HEAD, HEAD), F32), jax.ShapeDtypeStruct((ah, HEAD, 1), F32),
         jax.ShapeDtypeStruct((1, aw), F32)],
        scratch=[pltpu.VMEM((r, aw), F32), pltpu.VMEM((r, aw), F32)])


def _halo_specs(t, r, w, col, rows):
    per = r // rows
    last = t // rows - 1
    prev = pl.BlockSpec((rows, w), lambda i: (jnp.maximum(i * per - 1, 0), col))
    nxt = pl.BlockSpec((rows, w), lambda i: (jnp.minimum((i + 1) * per, last), col))
    return prev, nxt


def _edge_rows(prev_ref, next_ref, n_steps):
    i = pl.program_id(0)
    rows = prev_ref.shape[0]
    before = prev_ref[...].astype(F32)[rows - 1:rows, :] * (i > 0).astype(F32)
    after = next_ref[...].astype(F32)[0:1, :] * (i < n_steps - 1).astype(F32)
    return before, after


def _conv_fwd(proj, off, bw, cw, cb_, gb, name, into=None):
    t = proj.shape[0]
    r = _pick(t, ROW_TILES)
    n_steps = t // r
    c0 = _col_block(off, bw)
    cp, cn = _halo_specs(t, r, bw, c0 + 1, HALO_ROWS)
    hp, hn = _halo_specs(t, r, bw, c0 + 2, HALO_ROWS)

    def body(b_ref, c_ref, h_ref, z_ref, cp_ref, cn_ref, hp_ref, hn_ref, cw_ref, cb_ref, gb_ref, o_ref, yc_ref):
        g = c_ref[...].astype(F32) * h_ref[...].astype(F32)
        c_before, c_after = _edge_rows(cp_ref, cn_ref, n_steps)
        h_before, h_after = _edge_rows(hp_ref, hn_ref, n_steps)
        yconv = (cb_ref[...] + cw_ref[0:1, :] * _shift_down(g, c_before * h_before) + cw_ref[1:2, :] * g
                 + cw_ref[2:3, :] * _shift_up(g, c_after * h_after))
        yc_ref[...] = yconv
        sil, _ = _silu_and_grad(z_ref[...].astype(F32))
        yhat, _ = _rms(b_ref[...].astype(F32) * yconv * sil, bw)
        o_ref[...] = (yhat * gb_ref[...]).astype(BF16)

    return _row_call(
        body, name, t, r,
        [_row_spec(r, bw, c0), _row_spec(r, bw, c0 + 1), _row_spec(r, bw, c0 + 2), _row_spec(r, bw, c0 + 3),
         cp, cn, hp, hn, _const_spec((CONV_TAPS, bw)), _const_spec((1, bw)), _const_spec((1, bw))],
        (proj, proj, proj, proj, proj, proj, proj, proj, cw, cb_, gb),
        [_row_spec(r, bw), _row_spec(r, bw)],
        [jax.ShapeDtypeStruct((t, bw), BF16), jax.ShapeDtypeStruct((t, bw), F32)], into=into)


def _conv_bwd_gate(proj, off, bw, yconv, gb, dy, name):
    t = proj.shape[0]
    r = _pick(t, ROW_TILES)
    c0 = _col_block(off, bw)

    def body(b_ref, z_ref, yc_ref, gb_ref, dy_ref, dyc_ref, db_ref, dz_ref, dgb_ref, dcb_ref):
        _zero_at_first_step([dgb_ref, dcb_ref])
        b, z, yconv_v = b_ref[...].astype(F32), z_ref[...].astype(F32), yc_ref[...]
        sil, dsil = _silu_and_grad(z)
        yhat, rr = _rms(b * yconv_v * sil, bw)
        dy_f = dy_ref[...].astype(F32)
        dgb_ref[...] += _colsum(dy_f * yhat)
        dyb = _rms_bwd(dy_f * gb_ref[...], yhat, rr, bw)
        dyc = dyb * b * sil
        dyc_ref[...] = dyc
        dcb_ref[...] += _colsum(dyc)
        db_ref[...] = (dyb * yconv_v * sil).astype(BF16)
        dz_ref[...] = (dyb * b * yconv_v * dsil).astype(BF16)

    return _row_call(
        body, name, t, r,
        [_row_spec(r, bw, c0), _row_spec(r, bw, c0 + 3), _row_spec(r, bw), _const_spec((1, bw)), _row_spec(r, bw, 1)],
        (proj, proj, yconv, gb, dy),
        [_row_spec(r, bw), _row_spec(r, bw), _row_spec(r, bw), _const_spec((1, bw)), _const_spec((1, bw))],
        [jax.ShapeDtypeStruct((t, bw), F32), jax.ShapeDtypeStruct((t, bw), BF16), jax.ShapeDtypeStruct((t, bw), BF16),
         jax.ShapeDtypeStruct((1, bw), F32), jax.ShapeDtypeStruct((1, bw), F32)])


def _conv_bwd_taps(proj, off, bw, dyc, cw, d_gate_b, d_gate_z, name, into=None):
    t = proj.shape[0]
    r = _pick(t, ROW_TILES)
    n_steps = t // r
    c0 = _col_block(off, bw)
    cp, cn = _halo_specs(t, r, bw, c0 + 1, HALO_ROWS)
    hp, hn = _halo_specs(t, r, bw, c0 + 2, HALO_ROWS)
    dp, dn = _halo_specs(t, r, bw, 0, 8)

    def body(c_ref, h_ref, cp_ref, cn_ref, hp_ref, hn_ref, d_ref, dp_ref, dn_ref, cw_ref, dgb_ref, dgz_ref,
             db_ref, dcw_ref):
        _zero_at_first_step([dcw_ref])
        c, h, d = c_ref[...].astype(F32), h_ref[...].astype(F32), d_ref[...]
        g = c * h
        c_before, c_after = _edge_rows(cp_ref, cn_ref, n_steps)
        h_before, h_after = _edge_rows(hp_ref, hn_ref, n_steps)
        d_before, d_after = _edge_rows(dp_ref, dn_ref, n_steps)
        dg = (cw_ref[0:1, :] * _shift_up(d, d_after) + cw_ref[1:2, :] * d + cw_ref[2:3, :] * _shift_down(d, d_before))
        db_ref[:, 0:bw] = dgb_ref[...]
        db_ref[:, bw:2 * bw] = (dg * h).astype(BF16)
        db_ref[:, 2 * bw:3 * bw] = (dg * c).astype(BF16)
        db_ref[:, 3 * bw:4 * bw] = dgz_ref[...]
        dcw_ref[0:1, :] += _colsum(d * _shift_down(g, c_before * h_before))
        dcw_ref[1:2, :] += _colsum(d * g)
        dcw_ref[2:3, :] += _colsum(d * _shift_up(g, c_after * h_after))

    return _row_call(
        body, name, t, r,
        [_row_spec(r, bw, c0 + 1), _row_spec(r, bw, c0 + 2), cp, cn, hp, hn, _row_spec(r, bw), dp, dn,
         _const_spec((CONV_TAPS, bw)), _row_spec(r, bw), _row_spec(r, bw)],
        (proj, proj, proj, proj, proj, proj, dyc, dyc, dyc, cw, d_gate_b, d_gate_z),
        [_row_spec(r, 4 * bw), _const_spec((CONV_TAPS, bw))],
        [jax.ShapeDtypeStruct((t, 4 * bw), BF16), jax.ShapeDtypeStruct((CONV_TAPS, bw), F32)], into=into)


def _mla_prep_fwd(proj, q_off, ckv_off, kr_off, ch, kvr, tabs, qn_g, qr_g, kr_g, kv_g, name):
    t = proj.shape[0]
    r = _pick(t, ROW_TILES)
    qw = ch * 2 * HEAD
    cos_t, sin_a, sin_b = tabs

    def body(q_ref, ckv_ref, kr_ref, cos_ref, sa_ref, sb_ref, qn_ref, qr_ref, krg_ref, kvg_ref,
             qo_ref, co_ref, ko_ref):
        cos_v, sa, sb = cos_ref[...], sa_ref[...], sb_ref[...]
        for hh in range(ch):
            lo = hh * 2 * HEAD
            nhat, _ = _rms(q_ref[:, lo:lo + HEAD].astype(F32), HEAD)
            qo_ref[:, lo:lo + HEAD] = (nhat * qn_ref[...]).astype(BF16)
            rhat, _ = _rms(q_ref[:, lo + HEAD:lo + 2 * HEAD].astype(F32), ROPE)
            qo_ref[:, lo + HEAD:lo + 2 * HEAD] = _rope(rhat * qr_ref[...], cos_v, sa, sb).astype(BF16)
        khat, _ = _rms(kr_ref[...].astype(F32), ROPE)
        ko_ref[...] = _rope(khat * krg_ref[...], cos_v, sa, sb).astype(BF16)
        chat, _ = _rms(ckv_ref[...].astype(F32), kvr)
        co_ref[...] = (chat * kvg_ref[...]).astype(BF16)

    tab = _row_spec(r, HEAD)
    gain = _const_spec((1, HEAD))
    return _row_call(
        body, name, t, r,
        [_row_spec(r, qw, _col_block(q_off, qw)), _row_spec(r, kvr, _col_block(ckv_off, kvr)),
         _row_spec(r, HEAD, _col_block(kr_off, HEAD)), tab, tab, tab, gain, gain, gain, _const_spec((1, kvr))],
        (proj, proj, proj, cos_t, sin_a, sin_b, qn_g, qr_g, kr_g, kv_g),
        [_row_spec(r, qw), _row_spec(r, kvr), _row_spec(r, HEAD)],
        [jax.ShapeDtypeStruct((t, qw), BF16), jax.ShapeDtypeStruct((t, kvr), BF16),
         jax.ShapeDtypeStruct((t, HEAD), BF16)])


def _mla_prep_bwd(proj, q_off, ckv_off, kr_off, ch, kvr, tabs, qn_g, qr_g, kr_g, kv_g, dq_cat_t, dq_scale, dckv_n,
                  dkr_rot, kr_width, name, into=None):
    t = proj.shape[0]
    r = _pick(t, ROW_TILES)
    qw = ch * 2 * HEAD
    cos_t, sin_a, sin_b = tabs

    def body(q_ref, ckv_ref, kr_ref, cos_ref, sa_ref, sb_ref, qn_ref, qr_ref, krg_ref, kvg_ref,
             dq_ref, dc_ref, dk_ref, dqo_ref, dco_ref, dko_ref, dqn_ref, dqr_ref, dkrg_ref, dkvg_ref):
        _zero_at_first_step([dqn_ref, dqr_ref, dkrg_ref, dkvg_ref])
        cos_v, sa, sb = cos_ref[...], sa_ref[...], sb_ref[...]
        dq = dq_ref[...].T * dq_scale
        for hh in range(ch):
            lo = hh * 2 * HEAD
            nhat, nr = _rms(q_ref[:, lo:lo + HEAD].astype(F32), HEAD)
            d_n = dq[:, lo:lo + HEAD]
            dqn_ref[...] += _colsum(d_n * nhat)
            dqo_ref[:, lo:lo + HEAD] = _rms_bwd(d_n * qn_ref[...], nhat, nr, HEAD).astype(BF16)
            rhat, rr = _rms(q_ref[:, lo + HEAD:lo + 2 * HEAD].astype(F32), ROPE)
            d_t = _rope_bwd(dq[:, lo + HEAD:lo + 2 * HEAD], cos_v, sa, sb)
            dqr_ref[...] += _colsum(d_t * rhat)
            dqo_ref[:, lo + HEAD:lo + 2 * HEAD] = _rms_bwd(d_t * qr_ref[...], rhat, rr, ROPE).astype(BF16)
        khat, kr_r = _rms(kr_ref[...].astype(F32), ROPE)
        d_k = _rope_bwd(dk_ref[...], cos_v, sa, sb)
        dkrg_ref[...] += _colsum(d_k * khat)
        dko_ref[:, 0:HEAD] = _rms_bwd(d_k * krg_ref[...], khat, kr_r, ROPE).astype(BF16)
        if kr_width > HEAD:
            dko_ref[:, HEAD:kr_width] = jnp.zeros((r, kr_width - HEAD), BF16)
        chat, cr = _rms(ckv_ref[...].astype(F32), kvr)
        d_c = dc_ref[...].astype(F32)
        dkvg_ref[...] += _colsum(d_c * chat)
        dco_ref[...] = _rms_bwd(d_c * kvg_ref[...], chat, cr, kvr).astype(BF16)

    tab = _row_spec(r, HEAD)
    gain = _const_spec((1, HEAD))
    return _row_call(
        body, name, t, r,
        [_row_spec(r, qw, _col_block(q_off, qw)), _row_spec(r, kvr, _col_block(ckv_off, kvr)),
         _row_spec(r, HEAD, _col_block(kr_off, HEAD)), tab, tab, tab, gain, gain, gain, _const_spec((1, kvr)),
         pl.BlockSpec((qw, r), lambda i: (0, i)), _row_spec(r, kvr), _row_spec(r, HEAD)],
        (proj, proj, proj, cos_t, sin_a, sin_b, qn_g, qr_g, kr_g, kv_g, dq_cat_t, dckv_n, dkr_rot),
        [_row_spec(r, qw), _row_spec(r, kvr), _row_spec(r, kr_width), gain, gain, gain, _const_spec((1, kvr))],
        [jax.ShapeDtypeStruct((t, qw), BF16), jax.ShapeDtypeStruct((t, kvr), BF16),
         jax.ShapeDtypeStruct((t, kr_width), BF16), jax.ShapeDtypeStruct((1, HEAD), F32),
         jax.ShapeDtypeStruct((1, HEAD), F32), jax.ShapeDtypeStruct((1, HEAD), F32),
         jax.ShapeDtypeStruct((1, kvr), F32)], into=into)


def _kv_prep_fwd(kv, kr_rot, ch, kn_g, name):
    t = kv.shape[0]
    r = _pick(t, ROW_TILES)
    qw = ch * 2 * HEAD

    def body(kv_ref, kr_ref, kn_ref, ko_ref, kt_ref, vo_ref):
        ones = jnp.ones((r, HEAD), BF16)
        for hh in range(ch):
            lo = hh * 2 * HEAD
            nhat, _ = _rms(kv_ref[:, lo:lo + HEAD], HEAD)
            ko_ref[:, lo:lo + HEAD] = (nhat * kn_ref[...]).astype(BF16)
            ko_ref[:, lo + HEAD:lo + 2 * HEAD] = kr_ref[...]
            vo_ref[:, lo:lo + HEAD] = kv_ref[:, lo + HEAD:lo + 2 * HEAD].astype(BF16)
            vo_ref[:, lo + HEAD:lo + 2 * HEAD] = ones
        kt_ref[...] = ko_ref[...].astype(F32).T.astype(BF16)

    return _row_call(
        body, name, t, r, [_row_spec(r, qw), _row_spec(r, HEAD), _const_spec((1, HEAD))], (kv, kr_rot, kn_g),
        [_row_spec(r, qw), pl.BlockSpec((qw, r), lambda i: (0, i)), _row_spec(r, qw)],
        [jax.ShapeDtypeStruct((t, qw), BF16), jax.ShapeDtypeStruct((qw, t), BF16),
         jax.ShapeDtypeStruct((t, qw), BF16)])


def _kv_prep_bwd(kv, ch, kn_g, dk_cat, dv, name):
    t = kv.shape[0]
    r = _pick(t, ROW_TILES)
    qw = ch * 2 * HEAD

    def body(kv_ref, kn_ref, dk_ref, dv_ref, dkv_ref, dkr_ref, dkn_ref):
        _zero_at_first_step([dkn_ref])
        dkr = jnp.zeros((r, HEAD), F32)
        for hh in range(ch):
            lo = hh * 2 * HEAD
            nhat, nr = _rms(kv_ref[:, lo:lo + HEAD], HEAD)
            d_n = dk_ref[:, lo:lo + HEAD].astype(F32)
            dkn_ref[...] += _colsum(d_n * nhat)
            dkv_ref[:, lo:lo + HEAD] = _rms_bwd(d_n * kn_ref[...], nhat, nr, HEAD).astype(BF16)
            dkv_ref[:, lo + HEAD:lo + 2 * HEAD] = dv_ref[:, hh * HEAD:(hh + 1) * HEAD]
            dkr = dkr + dk_ref[:, lo + HEAD:lo + 2 * HEAD].astype(F32)
        dkr_ref[...] = dkr

    return _row_call(
        body, name, t, r,
        [_row_spec(r, qw), _const_spec((1, HEAD)), _row_spec(r, qw), _row_spec(r, ch * HEAD)], (kv, kn_g, dk_cat, dv),
        [_row_spec(r, qw), _row_spec(r, HEAD), _const_spec((1, HEAD))],
        [jax.ShapeDtypeStruct((t, qw), BF16), jax.ShapeDtypeStruct((t, HEAD), F32),
         jax.ShapeDtypeStruct((1, HEAD), F32)])


def _attn_tiles(t):
    return _pick(t, (2048, 1024, 512, 256, 128)), _pick(t, (1024, 512, 256, 128))


_NT = (((1,), (1,)), ((), ()))
LOG2E = 1.4426950408889634


def _attn_fwd(q_cat, k_cat, v_aug, ch, scale, name):
    t = q_cat.shape[0]
    tq, tk = _attn_tiles(t)
    nk = t // tk
    c2 = scale * LOG2E

    def body(q_ref, k_ref, v_ref, o_ref, lse_ref, s_scr, m_scr, acc_scr):
        j = pl.program_id(2)

        def scores(slot):
            s_scr[slot] = lax.dot_general(q_ref[...], k_ref[...], _NT, preferred_element_type=F32) * c2

        def absorb(slot):
            s = s_scr[slot]
            m_old = m_scr[...]
            m_new = jnp.maximum(m_old, jnp.max(s, axis=-1, keepdims=True))
            p = jnp.exp2(s - m_new).astype(BF16)
            acc_scr[...] = (jnp.exp2(m_old - m_new) * acc_scr[...]
                            + jnp.dot(p, v_ref[...], preferred_element_type=F32))
            m_scr[...] = m_new

        @pl.when(j == 0)
        def _():
            m_scr[...] = jnp.full(m_scr.shape, -jnp.inf, F32)
            acc_scr[...] = jnp.zeros(acc_scr.shape, F32)
            scores(0)

        for parity in (0, 1):
            @pl.when((j > 0) & (j < nk) & (j % 2 == parity))
            def _():
                scores(parity)
                absorb(1 - parity)

        @pl.when(j == nk)
        def _():
            absorb((nk - 1) % 2)
            acc = acc_scr[...]
            l_sum = acc[:, HEAD:]
            o_ref[...] = (acc[:, :HEAD] / l_sum).astype(BF16)
            lse_ref[0] = m_scr[...] + jnp.log(l_sum[:, 0:1]) * LOG2E

    return pl.pallas_call(
        body, name=name, grid=(ch, t // tq, nk + 1),
        in_specs=[pl.BlockSpec((tq, 2 * HEAD), lambda h, i, j: (i, h)),
                  pl.BlockSpec((tk, 2 * HEAD), lambda h, i, j: (jnp.minimum(j, nk - 1), h)),
                  pl.BlockSpec((tk, 2 * HEAD), lambda h, i, j: (jnp.maximum(j - 1, 0), h))],
        out_specs=[pl.BlockSpec((tq, HEAD), lambda h, i, j: (i, h)),
                   pl.BlockSpec((1, tq, 1), lambda h, i, j: (h, i, 0))],
        out_shape=[jax.ShapeDtypeStruct((t, ch * HEAD), BF16), jax.ShapeDtypeStruct((ch, t, 1), F32)],
        scratch_shapes=[pltpu.VMEM((2, tq, tk), F32), pltpu.VMEM((tq, 1), F32), pltpu.VMEM((tq, 2 * HEAD), F32)],
        compiler_params=_params(("parallel", "parallel", "arbitrary")),
    )(q_cat, k_cat, v_aug)


def _attn_bwd(q_cat, k_cat, k_cat_t, v_aug, do, lse_row, d_row, ch, scale, name):
    t = q_cat.shape[0]
    tk = _pick(t, (1024, 512, 256, 128))
    tq = _pick(t, (1024, 512, 256, 128))
    nk, nq = t // tk, t // tq
    c2 = scale * LOG2E

    def body(q_ref, do_ref, qp_ref, dop_ref, lse_ref, d_ref, k_ref, kt_ref, v_ref,
             dqt_ref, dk_ref, dv_ref, s_scr, dp_scr, dk_scr, dv_scr):
        ki, j = pl.program_id(1), pl.program_id(2)

        def products(slot):
            s_scr[slot] = lax.dot_general(k_ref[...], q_ref[...], _NT, preferred_element_type=F32) * c2
            dp_scr[slot] = lax.dot_general(v_ref[...], do_ref[...], _NT, preferred_element_type=F32)

        def absorb(slot):
            q, do_v = qp_ref[...], dop_ref[...]
            pt = jnp.exp2(s_scr[slot] - lse_ref[0])
            dv_scr[...] += jnp.dot(pt.astype(BF16), do_v, preferred_element_type=F32)
            dst = (pt * (dp_scr[slot] - d_ref[0])).astype(BF16)
            dk_scr[...] += jnp.dot(dst, q, preferred_element_type=F32)
            part = jnp.dot(kt_ref[...], dst, preferred_element_type=F32)
            cols = pl.ds(pl.multiple_of((j - 1) * tq, tq), tq)

            @pl.when(ki == 0)
            def _():
                dqt_ref[:, cols] = part

            @pl.when(ki > 0)
            def _():
                dqt_ref[:, cols] += part

        @pl.when(j == 0)
        def _():
            dk_scr[...] = jnp.zeros(dk_scr.shape, F32)
            dv_scr[...] = jnp.zeros(dv_scr.shape, F32)
            products(0)

        for parity in (0, 1):
            @pl.when((j > 0) & (j < nq) & (j % 2 == parity))
            def _():
                products(parity)
                absorb(1 - parity)

        @pl.when(j == nq)
        def _():
            absorb((nq - 1) % 2)
            dk_ref[...] = (dk_scr[...] * scale).astype(BF16)
            dv_ref[...] = dv_scr[...].astype(BF16)

    def cur(i):
        return jnp.minimum(i, nq - 1)

    def prev(i):
        return jnp.maximum(i - 1, 0)

    stat = pl.BlockSpec((1, 1, tq), lambda h, j, i: (h, 0, prev(i)))
    return pl.pallas_call(
        body, name=name, grid=(ch, nk, nq + 1),
        in_specs=[pl.BlockSpec((tq, 2 * HEAD), lambda h, j, i: (cur(i), h)),
                  pl.BlockSpec((tq, HEAD), lambda h, j, i: (cur(i), h)),
                  pl.BlockSpec((tq, 2 * HEAD), lambda h, j, i: (prev(i), h)),
                  pl.BlockSpec((tq, HEAD), lambda h, j, i: (prev(i), h)), stat, stat,
                  pl.BlockSpec((tk, 2 * HEAD), lambda h, j, i: (j, h)),
                  pl.BlockSpec((2 * HEAD, tk), lambda h, j, i: (h, j)),
                  pl.BlockSpec((tk, HEAD), lambda h, j, i: (j, 2 * h))],
        out_specs=[pl.BlockSpec((2 * HEAD, t), lambda h, j, i: (h, 0)),
                   pl.BlockSpec((tk, 2 * HEAD), lambda h, j, i: (j, h)),
                   pl.BlockSpec((tk, HEAD), lambda h, j, i: (j, h))],
        out_shape=[jax.ShapeDtypeStruct((ch * 2 * HEAD, t), F32), jax.ShapeDtypeStruct((t, ch * 2 * HEAD), BF16),
                   jax.ShapeDtypeStruct((t, ch * HEAD), BF16)],
        scratch_shapes=[pltpu.VMEM((2, tk, tq), F32), pltpu.VMEM((2, tk, tq), F32),
                        pltpu.VMEM((tk, 2 * HEAD), F32), pltpu.VMEM((tk, HEAD), F32)],
        compiler_params=_params(("parallel", "arbitrary", "arbitrary")),
    )(q_cat, do, q_cat, do, lse_row, d_row, k_cat, k_cat_t, v_aug)


def _attn_post_fwd(o, proj, z_off, cw, gc, name, into=None):
    t = o.shape[0]
    r = _pick(t, ROW_TILES)

    def body(o_ref, z_ref, gc_ref, y_ref):
        sil, _ = _silu_and_grad(z_ref[...].astype(F32))
        yhat, _ = _rms(o_ref[...].astype(F32) * sil, cw)
        y_ref[...] = (yhat * gc_ref[...]).astype(BF16)

    return _row_call(body, name, t, r,
                     [_row_spec(r, cw), _row_spec(r, cw, _col_block(z_off, cw)), _const_spec((1, cw))], (o, proj, gc),
                     _row_spec(r, cw), jax.ShapeDtypeStruct((t, cw), BF16), into=into)


def _attn_post_bwd(o, proj, z_off, cw, gc, dy, dy_col, name, into=None):
    t = o.shape[0]
    ch = cw // HEAD
    r = _pick(t, ROW_TILES)

    def body(o_ref, z_ref, gc_ref, dy_ref, dz_ref, do_ref, ds_ref, dgc_ref):
        _zero_at_first_step([dgc_ref])
        o_v, z = o_ref[...].astype(F32), z_ref[...].astype(F32)
        sil, dsil = _silu_and_grad(z)
        yhat, rr = _rms(o_v * sil, cw)
        dy_f = dy_ref[...].astype(F32)
        dgc_ref[...] += _colsum(dy_f * yhat)
        dyc = _rms_bwd(dy_f * gc_ref[...], yhat, rr, cw)
        do_b = (dyc * sil).astype(BF16)
        do_ref[...] = do_b
        dz_ref[...] = (dyc * o_v * dsil).astype(BF16)
        prod = do_b.astype(F32) * o_v
        for hh in range(ch):
            ds_ref[hh] = jnp.sum(prod[:, hh * HEAD:(hh + 1) * HEAD], axis=-1, keepdims=True)

    return _row_call(
        body, name, t, r,
        [_row_spec(r, cw), _row_spec(r, cw, _col_block(z_off, cw)), _const_spec((1, cw)), _row_spec(r, cw, dy_col)],
        (o, proj, gc, dy),
        [_row_spec(r, cw), _row_spec(r, cw), pl.BlockSpec((ch, r, 1), lambda i: (0, i, 0)), _const_spec((1, cw))],
        [jax.ShapeDtypeStruct((t, cw), BF16), jax.ShapeDtypeStruct((t, cw), BF16),
         jax.ShapeDtypeStruct((ch, t, 1), F32), jax.ShapeDtypeStruct((1, cw), F32)], into=into)


def _ple_fwd(h1, gpre, pp, name):
    t, d = h1.shape
    r = _pick(t, ROW_TILES)

    def body(h_ref, g_ref, p_ref, o_ref):
        o_ref[...] = h_ref[...] + _sigmoid(g_ref[...]) * p_ref[...]

    return _row_call(body, name, t, r, [_row_spec(r, d)] * 3, (h1, gpre, pp), _row_spec(r, d),
                     jax.ShapeDtypeStruct((t, d), F32))


def _ple_bwd(gpre, pp, dh, name):
    t, d = dh.shape
    r = _pick(t, ROW_TILES)

    def body(g_ref, p_ref, dh_ref, dg_ref, dp_ref):
        sig = _sigmoid(g_ref[...])
        dh_v = dh_ref[...]
        dg_ref[...] = (dh_v * p_ref[...] * sig * (1.0 - sig)).astype(BF16)
        dp_ref[...] = (dh_v * sig).astype(BF16)

    return _row_call(body, name, t, r, [_row_spec(r, d)] * 3, (gpre, pp, dh), [_row_spec(r, d)] * 2,
                     [jax.ShapeDtypeStruct((t, d), BF16)] * 2)


def _loss_and_grad(h, target, name):
    t, d = h.shape
    r = _pick(t, ROW_TILES)

    def body(h_ref, t_ref, l_ref, dh_ref):
        _zero_at_first_step([l_ref])
        err = h_ref[...] - t_ref[...]
        l_ref[...] += jnp.sum(jnp.sum(err * err, axis=-1, keepdims=True), axis=0, keepdims=True) * (0.5 / d)
        dh_ref[...] = err * (1.0 / d)

    return _row_call(body, name, t, r, [_row_spec(r, d)] * 2, (h, target), [_const_spec((1, 1)), _row_spec(r, d)],
                     [jax.ShapeDtypeStruct((1, 1), F32), jax.ShapeDtypeStruct((t, d), F32)])


def _ew_rows(rows, cols):
    cap = min(1024, max(8, (1 << 19) // max(cols, 1)))
    for cand in range(cap - cap % 8, 7, -8):
        if rows % cand == 0:
            return cand
    return rows


def _pair_sum_bf16(both, core_flag, b, name):
    _, n, rows, cols = both.shape
    rb = _ew_rows(rows, cols)

    def body(a_ref, flag_ref, b_ref, o_ref):
        mine = jnp.where(flag_ref[...] == 0.0, a_ref[0, 0], a_ref[1, 0])
        o_ref[0] = (mine + b_ref[0]).astype(BF16)

    spec = pl.BlockSpec((1, rb, cols), lambda s, i: (s, i, 0))
    return pl.pallas_call(
        body, name=name, grid=(n, rows // rb),
        in_specs=[pl.BlockSpec((2, 1, rb, cols), lambda s, i: (0, s, i, 0)),
                  pl.BlockSpec((1, 1), lambda s, i: (0, 0)), spec],
        out_specs=spec, out_shape=jax.ShapeDtypeStruct(b.shape, BF16),
        compiler_params=_params(("parallel", "parallel")))(both, core_flag, b)


def _shard_sum(a, b, recv, name):
    rows, cols = a.shape
    rb = _ew_rows(rows, cols)

    def body(a_ref, b_ref, r_ref, o_ref):
        o_ref[...] = ((a_ref[...] + b_ref[...]) + r_ref[0].astype(F32) + r_ref[1].astype(F32)
                      + r_ref[2].astype(F32))

    spec = pl.BlockSpec((rb, cols), lambda i: (i, 0))
    return pl.pallas_call(body, name=name, grid=(rows // rb,),
                          in_specs=[spec, spec, pl.BlockSpec((N_SHARD - 1, rb, cols), lambda i: (0, i, 0))],
                          out_specs=spec, out_shape=jax.ShapeDtypeStruct(a.shape, F32),
                          compiler_params=_params(("parallel",)))(a, b, recv)


def _sum_devices(g, name):
    n, rows, cols = g.shape
    rb = _ew_rows(rows, cols)

    def body(g_ref, o_ref):
        acc = g_ref[0]
        for k in range(1, n):
            acc = acc + g_ref[k]
        o_ref[...] = acc

    return pl.pallas_call(body, name=name, grid=(rows // rb,),
                          in_specs=[pl.BlockSpec((n, rb, cols), lambda i: (0, i, 0))],
                          out_specs=pl.BlockSpec((rb, cols), lambda i: (i, 0)),
                          out_shape=jax.ShapeDtypeStruct((rows, cols), F32),
                          compiler_params=_params(("parallel",)))(g)


def _adamw_update(w, g_v, m, v):
    m_new = ADAM_B1 * m + (1.0 - ADAM_B1) * g_v
    v_new = ADAM_B2 * v + (1.0 - ADAM_B2) * (g_v * g_v)
    m_hat = m_new / (1.0 - ADAM_B1 ** ADAM_STEP)
    v_hat = v_new / (1.0 - ADAM_B2 ** ADAM_STEP)
    return -ADAM_LR * (m_hat / (jnp.sqrt(v_hat) + ADAM_EPS) + ADAM_WD * w), m_new, v_new


def _adamw(w, g, m, v, name):
    rows, cols = w.shape
    rb = _ew_rows(rows, cols)

    def body(w_ref, g_ref, m_ref, v_ref, d_ref, mo_ref, vo_ref):
        d_ref[...], mo_ref[...], vo_ref[...] = _adamw_update(w_ref[...], g_ref[...], m_ref[...], v_ref[...])

    spec = pl.BlockSpec((rb, cols), lambda i: (i, 0))
    return pl.pallas_call(body, name=name, grid=(rows // rb,), in_specs=[spec] * 4, out_specs=[spec] * 3,
                          out_shape=[jax.ShapeDtypeStruct(w.shape, F32)] * 3,
                          compiler_params=_params(("parallel",)))(w, g, m, v)


def _adamw_two_halves(w, own, recv, core_flag, m, v, name):
    depth, rows, cols = w.shape
    assert rows % 2 == 0 and own.shape == (depth, rows // 2, cols)
    rb = _ew_rows(rows // 2, cols)
    nb = rows // 2 // rb

    def body(w_ref, own_ref, recv_ref, flag_ref, m_ref, v_ref, g_ref, d_ref, mo_ref, vo_ref):
        half = pl.program_id(1).astype(F32)
        g_v = jnp.where(flag_ref[...] == half, own_ref[...], recv_ref[...])
        g_ref[...] = g_v
        d_ref[...], mo_ref[...], vo_ref[...] = _adamw_update(w_ref[...], g_v, m_ref[...], v_ref[...])

    full = pl.BlockSpec((None, rb, cols), lambda l, k, i: (l, k * nb + i, 0))
    half_spec = pl.BlockSpec((None, rb, cols), lambda l, k, i: (l, i, 0))
    return pl.pallas_call(
        body, name=name, grid=(depth, 2, nb),
        in_specs=[full, half_spec, half_spec, pl.BlockSpec((1, 1), lambda l, k, i: (0, 0)), full, full],
        out_specs=[full] * 4, out_shape=[jax.ShapeDtypeStruct(w.shape, F32)] * 4,
        compiler_params=_params(("parallel", "parallel", "parallel")))(w, own, recv, core_flag, m, v)


def _place():
    return lax.axis_index("x"), lax.axis_index("y"), lax.axis_index("c")


def _other_chips(x, y):
    return [(1 - x, y), (x, 1 - y), (1 - x, 1 - y)]


_ANY = pl.BlockSpec(memory_space=pl.ANY)


class _Rider:
    def __init__(self, arrays, out_shapes, n_sems, start, finish, in_place=False):
        self.arrays, self.out_shapes, self.n_sems = list(arrays), list(out_shapes), n_sems
        self.start, self.finish, self.in_place = start, finish, in_place

    def sems(self):
        return [pltpu.SemaphoreType.DMA((self.n_sems,)), pltpu.SemaphoreType.DMA((self.n_sems,))]

    def aliases(self, first_in, first_out):
        return {first_in + a: first_out + a for a in range(len(self.arrays))} if self.in_place else {}


def _comm_call(rider, name):
    n_in, n_out = len(rider.arrays), len(rider.out_shapes)

    def body(*refs):
        ins, outs = refs[:n_in], refs[n_in:n_in + n_out]
        send_sems, recv_sems = refs[n_in + n_out:]
        rider.start(ins, outs, send_sems, recv_sems)
        rider.finish(ins, outs, send_sems, recv_sems)

    return pl.pallas_call(
        body, name=name, in_specs=[_ANY] * n_in, out_specs=[_ANY] * n_out, out_shape=rider.out_shapes,
        scratch_shapes=rider.sems(), input_output_aliases=rider.aliases(0, 0))(*rider.arrays)


def _half(shape, which):
    for axis, size in enumerate(shape):
        if size % 2 == 0:
            return (slice(None),) * axis + (pl.ds(which * (size // 2), size // 2),)
    raise ValueError(f"no axis of even length in {shape}")


def _start_then_wait(copies):
    def start(*refs):
        for send, _ in copies(*refs):
            send.start()

    def finish(*refs):
        pairs = copies(*refs)
        for _, landing in pairs:
            landing.wait_recv()
        for send, _ in pairs:
            send.wait_send()

    return start, finish


def _fetch_rider(shards):
    n, n_peer = len(shards), N_SHARD - 1
    shapes = [s.shape for s in shards]

    def copies(ins, outs, send_sems, recv_sems):
        x, y, c = _place()
        pairs = []
        for a in range(n):
            mine = _half(shapes[a], c)
            for k, (px, py) in enumerate(_other_chips(x, y)):
                def into(slot):
                    return pltpu.make_async_remote_copy(
                        src_ref=ins[a].at[mine], dst_ref=outs[a].at[(slot,) + mine],
                        send_sem=send_sems.at[a * n_peer + k], recv_sem=recv_sems.at[a * n_peer + k],
                        device_id=(px, py, c), device_id_type=MESH)
                pairs.append((into(2 * x + y), into(2 * px + py)))
        return pairs

    start, finish = _start_then_wait(copies)
    return _Rider(shards, [jax.ShapeDtypeStruct((N_SHARD,) + s.shape, s.dtype) for s in shards], n * n_peer,
                  start, finish)


def _forward_rider(gathered, shards):
    n, n_peer = len(gathered), N_SHARD - 1
    shapes = [s.shape for s in shards]

    def copies(ins, outs, send_sems, recv_sems):
        x, y, c = _place()
        pairs = []
        for a in range(n):
            for k, (px, py) in enumerate(_other_chips(x, y)):
                def half_of_slot(which):
                    rows = outs[a].at[(2 * px + py,) + _half(shapes[a], which)]
                    return pltpu.make_async_remote_copy(
                        src_ref=rows, dst_ref=rows, send_sem=send_sems.at[a * n_peer + k],
                        recv_sem=recv_sems.at[a * n_peer + k], device_id=(x, y, 1 - c), device_id_type=MESH)
                pairs.append((half_of_slot(c), half_of_slot(1 - c)))
        return pairs

    start, finish = _start_then_wait(copies)
    return _Rider(gathered, [jax.ShapeDtypeStruct(g.shape, g.dtype) for g in gathered], n * n_peer, start, finish,
                  in_place=True)


def _sibling_rider(arrs, other_half):
    n = len(arrs)

    def copies(ins, outs, send_sems, recv_sems):
        x, y, c = _place()
        pairs = []
        for a in range(n):
            cp = pltpu.make_async_remote_copy(
                src_ref=ins[a].at[1 - c] if other_half else ins[a], dst_ref=outs[a], send_sem=send_sems.at[a],
                recv_sem=recv_sems.at[a], device_id=(x, y, 1 - c), device_id_type=MESH)
            pairs.append((cp, cp))
        return pairs

    start, finish = _start_then_wait(copies)
    return _Rider(arrs, [jax.ShapeDtypeStruct(g.shape[1:] if other_half else g.shape, g.dtype) for g in arrs], n,
                  start, finish)


def _owner_rider(parts):
    n, n_peer = len(parts), N_SHARD - 1

    def copies(ins, outs, send_sems, recv_sems):
        x, y, c = _place()
        pairs = []
        for a in range(n):
            for k, (px, py) in enumerate(_other_chips(x, y)):
                cp = pltpu.make_async_remote_copy(
                    src_ref=ins[a].at[2 * px + py], dst_ref=outs[a].at[k], send_sem=send_sems.at[a * n_peer + k],
                    recv_sem=recv_sems.at[a * n_peer + k], device_id=(px, py, c), device_id_type=MESH)
                pairs.append((cp, cp))
        return pairs

    start, finish = _start_then_wait(copies)
    return _Rider(parts, [jax.ShapeDtypeStruct((n_peer,) + p.shape[1:], p.dtype) for p in parts], n * n_peer,
                  start, finish)


def _gather_devices(buf, name):
    n_peer = N_DEV - 1

    def body(in_ref, out_ref, send_sems, recv_sems, local_sem):
        x, y, c = _place()
        me = 4 * x + 2 * y + c
        mine = pltpu.make_async_copy(in_ref, out_ref.at[me], local_sem)
        mine.start()
        peers = []
        for k in range(1, N_DEV):
            fx, fy, fc = (k >> 2) & 1, (k >> 1) & 1, k & 1
            peers.append((x ^ fx, y ^ fy, c ^ fc))
        sends = []
        for k, peer in enumerate(peers):
            cp = pltpu.make_async_remote_copy(
                src_ref=in_ref, dst_ref=out_ref.at[me], send_sem=send_sems.at[k], recv_sem=recv_sems.at[k],
                device_id=peer, device_id_type=MESH)
            cp.start()
            sends.append(cp)
        for k, (px, py, pc) in enumerate(peers):
            pltpu.make_async_remote_copy(
                src_ref=in_ref, dst_ref=out_ref.at[4 * px + 2 * py + pc], send_sem=send_sems.at[k],
                recv_sem=recv_sems.at[k], device_id=(px, py, pc), device_id_type=MESH).wait_recv()
        for cp in sends:
            cp.wait_send()
        mine.wait()

    return pl.pallas_call(
        body, name=name, in_specs=[_ANY], out_specs=_ANY,
        out_shape=jax.ShapeDtypeStruct((N_DEV,) + buf.shape, buf.dtype),
        scratch_shapes=[pltpu.SemaphoreType.DMA((n_peer,)), pltpu.SemaphoreType.DMA((n_peer,)),
                        pltpu.SemaphoreType.DMA(())],
    )(buf)


class _Dims:
    def __init__(self, x, p, w_in, sgu_norm, conv_w, kv_norm, w_ukv, w_out):
        self.t, self.d = x.shape[1], x.shape[2]
        self.depth = w_in.shape[0]
        self.ple = p.shape[3]
        self.in_w = w_in.shape[2] * N_SHARD
        self.ah = sgu_norm.shape[1]
        self.aw = self.ah * HEAD
        self.bw = conv_w.shape[2] * N_SHARD
        self.kvr = kv_norm.shape[1]
        self.ch = w_ukv.shape[2] * N_SHARD // (2 * HEAD)
        self.cw = self.ch * HEAD
        self.mix = w_out.shape[1] * N_SHARD
        assert self.mix == self.aw + self.bw + self.cw and self.aw == self.bw
        self.qw = self.ch * 2 * HEAD
        segs = [('a', 3 * self.aw, 3 * self.aw), ('ckv', self.kvr, self.kvr), ('b', 4 * self.bw, 4 * self.bw),
                ('q', self.qw, self.qw), ('cz', self.cw, self.cw), ('kr', HEAD, HEAD)]
        off = 0
        self.off = {}
        for nm, width, align in segs:
            off = -(-off // align) * align
            self.off[nm] = off
            off += width
        self.inp = -(-off // 512) * 512
        q_real = self.ch * (HEAD + ROPE)
        widths = [3 * self.aw, 4 * self.bw, q_real, self.kvr, ROPE, self.cw]
        assert sum(widths) == self.in_w
        starts = [0]
        for wd in widths:
            starts.append(starts[-1] + wd)
        self.src = dict(zip(['a', 'b', 'q', 'ckv', 'kr', 'cz'], zip(starts[:-1], widths)))


def _rearrange_w_in(w, dm):
    lead, d = w.shape[:-2], w.shape[-1]
    axis = w.ndim - 2

    def rows(nm):
        s, wd = dm.src[nm]
        return lax.slice_in_dim(w, s, s + wd, axis=axis)

    pieces = {nm: rows(nm) for nm in ('a', 'b', 'ckv', 'cz')}
    q = rows('q').reshape(lead + (dm.ch, HEAD + ROPE, d))
    pieces['q'] = jnp.pad(q, [(0, 0)] * (len(lead) + 1) + [(0, HEAD - ROPE), (0, 0)]).reshape(lead + (dm.qw, d))
    pieces['kr'] = jnp.pad(rows('kr'), [(0, 0)] * len(lead) + [(0, HEAD - ROPE), (0, 0)])
    out, cur = [], 0
    for nm in sorted(dm.off, key=lambda k: dm.off[k]):
        if dm.off[nm] > cur:
            out.append(jnp.zeros(lead + (dm.off[nm] - cur, d), w.dtype))
        out.append(pieces[nm])
        cur = dm.off[nm] + pieces[nm].shape[axis]
    if dm.inp > cur:
        out.append(jnp.zeros(lead + (dm.inp - cur, d), w.dtype))
    return jnp.concatenate(out, axis=axis)


def _unarrange_w_in(g, dm):
    d = g.shape[1]

    def seg(nm, width):
        return g[dm.off[nm]:dm.off[nm] + width]

    q = seg('q', dm.qw).reshape(dm.ch, 2 * HEAD, d)[:, :HEAD + ROPE].reshape(dm.ch * (HEAD + ROPE), d)
    return jnp.concatenate([seg('a', 3 * dm.aw), seg('b', 4 * dm.bw), q, seg('ckv', dm.kvr), seg('kr', ROPE),
                            seg('cz', dm.cw)], axis=0)


def _finish_dproj(dproj, parts, widths, dm):
    t = dproj.shape[0]
    for nm, part in parts.items():
        dproj = lax.dynamic_update_slice(dproj, part, (0, dm.off[nm]))
    cur = 0
    for nm in sorted(dm.off, key=lambda k: dm.off[k]):
        if dm.off[nm] > cur:
            dproj = lax.dynamic_update_slice(dproj, jnp.zeros((t, dm.off[nm] - cur), BF16), (0, cur))
        cur = dm.off[nm] + widths[nm]
    assert cur == dm.inp
    return dproj


def _rope_tables(positions):
    inv = 1.0 / (ROPE_BASE ** (jnp.arange(0, ROPE, 2, dtype=F32) / ROPE))
    ang = positions.astype(F32)[:, None] * inv
    cos, sin = jnp.cos(ang), jnp.sin(ang)
    t = positions.shape[0]
    half = ROPE // 2
    cos_t = jnp.concatenate([cos, cos, jnp.zeros((t, HEAD - ROPE), F32)], axis=-1)
    sin_a = jnp.concatenate([-sin, jnp.zeros((t, HEAD - half), F32)], axis=-1)
    sin_b = jnp.concatenate([jnp.zeros((t, half), F32), sin, jnp.zeros((t, HEAD - ROPE), F32)], axis=-1)
    return cos_t, sin_a, sin_b


def _pad_gain(g):
    return jnp.pad(g, (0, HEAD - g.shape[0]))[None, :]


def _shard_major(g, axis):
    shape = g.shape
    g = g.reshape(shape[:axis] + (N_SHARD, shape[axis] // N_SHARD) + shape[axis + 1:])
    g = jnp.moveaxis(g, axis, 0)
    rows, cols = g.shape[1], g.shape[2]
    return jnp.swapaxes(g.reshape(N_SHARD, 2, rows // 2, cols), 0, 1)


def _pack(arrs):
    flat = jnp.concatenate([a.reshape(-1) for a in arrs])
    pad = (-flat.shape[0]) % (8 * HEAD)
    return jnp.pad(flat, (0, pad)).reshape(-1, HEAD)


def _unpack(buf, shapes):
    flat = buf.reshape(-1)
    out, cur = [], 0
    for s in shapes:
        size = 1
        for v in s:
            size *= v
        out.append(flat[cur:cur + size].reshape(s))
        cur += size
    return out


def kernel(x, p, positions, attn_norm, w_in, sgu_norm, w_spatial, b_spatial, conv_w, conv_b, kv_norm, w_ukv, q_nope_norm, q_rope_norm, k_nope_norm, k_rope_norm, out_norm, w_out, ple_norm, w_ple_gate, w_ple_proj, loss_target, m_attn_norm, m_w_in, m_sgu_norm, m_w_spatial, m_b_spatial, m_conv_w, m_conv_b, m_kv_norm, m_w_ukv, m_q_nope_norm, m_q_rope_norm, m_k_nope_norm, m_k_rope_norm, m_out_norm, m_w_out, m_ple_norm, m_w_ple_gate, m_w_ple_proj, v_attn_norm, v_w_in, v_sgu_norm, v_w_spatial, v_b_spatial, v_conv_w, v_conv_b, v_kv_norm, v_w_ukv, v_q_nope_norm, v_q_rope_norm, v_k_nope_norm, v_k_rope_norm, v_out_norm, v_w_out, v_ple_norm, v_w_ple_gate, v_w_ple_proj):
    weights = dict(attn_norm=attn_norm, w_in=w_in, sgu_norm=sgu_norm, w_spatial=w_spatial, b_spatial=b_spatial,
                   conv_w=conv_w, conv_b=conv_b, kv_norm=kv_norm, w_ukv=w_ukv, q_nope_norm=q_nope_norm,
                   q_rope_norm=q_rope_norm, k_nope_norm=k_nope_norm, k_rope_norm=k_rope_norm, out_norm=out_norm,
                   w_out=w_out, ple_norm=ple_norm, w_ple_gate=w_ple_gate, w_ple_proj=w_ple_proj)
    mom_m = dict(attn_norm=m_attn_norm, w_in=m_w_in, sgu_norm=m_sgu_norm, w_spatial=m_w_spatial,
                 b_spatial=m_b_spatial, conv_w=m_conv_w, conv_b=m_conv_b, kv_norm=m_kv_norm, w_ukv=m_w_ukv,
                 q_nope_norm=m_q_nope_norm, q_rope_norm=m_q_rope_norm, k_nope_norm=m_k_nope_norm,
                 k_rope_norm=m_k_rope_norm, out_norm=m_out_norm, w_out=m_w_out, ple_norm=m_ple_norm,
                 w_ple_gate=m_w_ple_gate, w_ple_proj=m_w_ple_proj)
    mom_v = dict(attn_norm=v_attn_norm, w_in=v_w_in, sgu_norm=v_sgu_norm, w_spatial=v_w_spatial,
                 b_spatial=v_b_spatial, conv_w=v_conv_w, conv_b=v_conv_b, kv_norm=v_kv_norm, w_ukv=v_w_ukv,
                 q_nope_norm=v_q_nope_norm, q_rope_norm=v_q_rope_norm, k_nope_norm=v_k_nope_norm,
                 k_rope_norm=v_k_rope_norm, out_norm=v_out_norm, w_out=v_w_out, ple_norm=v_ple_norm,
                 w_ple_gate=v_w_ple_gate, w_ple_proj=v_w_ple_proj)
    dm = _Dims(x, p, w_in, sgu_norm, conv_w, kv_norm, w_ukv, w_out)
    for group in (weights, mom_m, mom_v):
        group['w_in'] = jnp.swapaxes(group['w_in'], 1, 2)
    t, d, depth = dm.t, dm.d, dm.depth
    shard = 2 * lax.axis_index("x") + lax.axis_index("y")
    core = lax.axis_index("c")
    scale = float(HEAD + ROPE) ** -0.5

    def local_layer(i):
        return [weights[n][i:i + 1].astype(BF16) for n in BIG]

    def fill_own_slot(gathered, local):
        return [lax.dynamic_update_slice(g, mine[None], (shard,) + (0,) * mine.ndim)
                for g, mine in zip(gathered, local)]

    def layer_weights(filled):
        w = {n: jnp.concatenate([filled[j][s] for s in range(N_SHARD)], axis=BIG_AXIS[n])
             for j, n in enumerate(BIG)}
        w['w_in'] = _rearrange_w_in(w['w_in'], dm)
        return w

    first = local_layer(0) + [conv_w]
    fetched = _comm_call(_fetch_rider(first), "fetch_weights_l0")
    filled = fill_own_slot(_comm_call(_forward_rider(fetched, first), "forward_weights_l0"), first)
    layer_w = [layer_weights(filled)] + [None] * (depth - 1)
    conv_w_full = jnp.concatenate([filled[len(BIG)][s] for s in range(N_SHARD)], axis=2)

    tabs = _rope_tables(positions[0])
    h = x[0]
    saved = []
    for i in range(depth):
        tag = f"l{i}_"
        ga, gb, gc = (out_norm[i][None, :dm.aw], out_norm[i][None, dm.aw:dm.aw + dm.bw],
                      out_norm[i][None, dm.aw + dm.bw:])
        ws_b = w_spatial[i].astype(BF16)
        bb = jnp.broadcast_to(b_spatial[i][:, :, None], (dm.ah, HEAD, HEAD))
        qn_g, qr_g = q_nope_norm[i][None, :], _pad_gain(q_rope_norm[i])
        kn_g, kr_g = k_nope_norm[i][None, :], _pad_gain(k_rope_norm[i])
        kv_g = kv_norm[i][None, :]
        wl = layer_w[i]
        nxt = local_layer(i + 1) if i + 1 < depth else None
        hn = _norm_fwd(h, attn_norm[i][None, :], tag + "norm1")
        if nxt is None:
            proj = _matmul(hn, wl['w_in'], 'nt', BF16, tag + "proj", b_layer=0)
        else:
            proj, fetched = _matmul(hn, wl['w_in'], 'nt', BF16, tag + "proj", b_layer=0, rider=_fetch_rider(nxt))
        y = _sgu_fwd(proj, dm.off['a'], dm.aw, sgu_norm[i], ws_b, bb, ga, tag + "sgu",
                     into=(jnp.zeros((t, dm.mix), BF16), 0))
        y, yconv = _conv_fwd(proj, dm.off['b'], dm.bw, conv_w_full[i], conv_b[i][None, :], gb, tag + "conv",
                             into=(y, _col_block(dm.aw, dm.bw)))
        q_cat, ckv_n, kr_rot = _mla_prep_fwd(proj, dm.off['q'], dm.off['ckv'], dm.off['kr'], dm.ch, dm.kvr, tabs,
                                             qn_g, qr_g, kr_g, kv_g, tag + "mla_prep")
        kv = _matmul(ckv_n, wl['w_ukv'], 'nn', F32, tag + "kv_up", b_layer=0)
        k_cat, k_cat_t, v_aug = _kv_prep_fwd(kv, kr_rot, dm.ch, kn_g, tag + "kv_prep")
        o, lse = _attn_fwd(q_cat, k_cat, v_aug, dm.ch, scale, tag + "attn")
        y = _attn_post_fwd(o, proj, dm.off['cz'], dm.cw, gc, tag + "attn_post",
                           into=(y, _col_block(dm.aw + dm.bw, dm.cw)))
        if nxt is None:
            h1 = _matmul(y, wl['w_out'], 'nn', F32, tag + "out", add=h, b_layer=0)
        else:
            h1, gathered = _matmul(y, wl['w_out'], 'nn', F32, tag + "out", add=h, b_layer=0,
                                   rider=_forward_rider(list(fetched), nxt))
            layer_w[i + 1] = layer_weights(fill_own_slot(gathered, nxt))
        hn2 = _norm_fwd(h1, ple_norm[i][None, :], tag + "norm2")
        gpre = _matmul(hn2, wl['w_ple_gate'], 'nn', F32, tag + "gate", b_layer=0)
        p_b = p[i, 0].astype(BF16)
        pp = _matmul(p_b, wl['w_ple_proj'], 'nn', F32, tag + "ple_proj", b_layer=0)
        h2 = _ple_fwd(h1, gpre, pp, tag + "ple")
        saved.append(dict(h=h, hn=hn, proj=proj, yconv=yconv, q_cat=q_cat, ckv_n=ckv_n, kv=kv, k_cat=k_cat,
                          k_cat_t=k_cat_t,
                          v=v_aug, o=o, lse=lse, y=y, h1=h1, hn2=hn2, gpre=gpre, pp=pp, p_b=p_b, ws_b=ws_b, bb=bb,
                          gains=(ga, gb, gc, qn_g, qr_g, kn_g, kr_g, kv_g)))
        h = h2

    loss_part, dh = _loss_and_grad(h, loss_target[0], "loss")
    loss = lax.psum(loss_part[0, 0], ("x", "y", "c"))

    core_flag = core.astype(F32).reshape(1, 1)

    def chip_sums(sm, from_sibling, tag):
        return sm, [_pair_sum_bf16(a, core_flag, b, f"{tag}chip_sum_{n}") for a, b, n in zip(sm, from_sibling, BIG)]

    def shard_sums(sm, from_sibling, from_chips, tag):
        out = []
        for a, b, r3, n in zip(sm, from_sibling, from_chips, BIG):
            own_a = lax.dynamic_slice(a, (core, shard, 0, 0), (1, 1) + a.shape[2:]).reshape(a.shape[2:])
            own_b = lax.dynamic_index_in_dim(b, shard, 0, keepdims=False)
            out.append(_shard_sum(own_a, own_b, r3, f"{tag}shard_sum_{n}"))
        return out

    grads = {n: [None] * depth for n in WEIGHTS}
    own_half = {n: [None] * depth for n in BIG}
    sibling_half = {n: [None] * depth for n in BIG}
    carry = None
    for i in reversed(range(depth)):
        tag = f"l{i}_b_"
        gtag = f"l{i + 1}_g_"
        sv = saved[i]
        wl = layer_w[i]
        ga, gb, gc, qn_g, qr_g, kn_g, kr_g, kv_g = sv['gains']
        proj = sv['proj']
        dgpre, dpp = _ple_bwd(sv['gpre'], sv['pp'], dh, tag + "ple")
        grads['w_ple_proj'][i] = _matmul(sv['p_b'], dpp, 'tn', F32, tag + "d_w_ple_proj")
        if carry is None:
            grads['w_ple_gate'][i] = _matmul(sv['hn2'], dgpre, 'tn', F32, tag + "d_w_gate")
        else:
            grads['w_ple_gate'][i], from_sibling = _matmul(sv['hn2'], dgpre, 'tn', F32, tag + "d_w_gate",
                                                           rider=_sibling_rider(carry, True))
            mine, sums = chip_sums(carry, from_sibling, gtag)
        d_hn2 = _matmul(dgpre, wl['w_ple_gate'], 'nt', BF16, tag + "d_hn2", b_layer=0)
        dh1, dh1_b, g_ple = _norm_bwd(sv['h1'], ple_norm[i][None, :], d_hn2, dh, tag + "norm2")
        grads['ple_norm'][i] = g_ple[0]
        grads['w_out'][i] = _matmul(sv['y'], dh1_b, 'tn', F32, tag + "d_w_out")
        dy = _matmul(dh1_b, wl['w_out'], 'nt', BF16, tag + "d_y", b_layer=0)
        ws_t = jnp.swapaxes(sv['ws_b'], 1, 2)
        dproj, g_sgu, g_ws, g_bs, g_ga = _sgu_bwd(proj, dm.off['a'], dm.aw, sgu_norm[i], sv['ws_b'], ws_t, sv['bb'],
                                                  ga, dy, dm.inp, tag + "sgu")
        grads['sgu_norm'][i], grads['w_spatial'][i], grads['b_spatial'][i] = g_sgu, g_ws, g_bs[:, :, 0]
        dyc, d_bb, d_bz, g_gb, g_cb = _conv_bwd_gate(proj, dm.off['b'], dm.bw, sv['yconv'], gb, dy, tag + "conv_gate")
        dproj, g_cw = _conv_bwd_taps(proj, dm.off['b'], dm.bw, dyc, conv_w_full[i], d_bb, d_bz, tag + "conv_taps",
                                     into=(dproj, _col_block(dm.off['b'], 4 * dm.bw)))
        grads['conv_b'][i], grads['conv_w'][i] = g_cb[0], g_cw
        dproj, d_o, dsum, g_gc = _attn_post_bwd(sv['o'], proj, dm.off['cz'], dm.cw, gc, dy,
                                                _col_block(dm.aw + dm.bw, dm.cw), tag + "attn_post",
                                                into=(dproj, _col_block(dm.off['cz'], dm.cw)))
        grads['out_norm'][i] = jnp.concatenate([g_ga[0], g_gb[0], g_gc[0]])
        dq_t, dk_cat, dv = _attn_bwd(sv['q_cat'], sv['k_cat'], sv['k_cat_t'], sv['v'], d_o,
                                     sv['lse'].reshape(dm.ch, 1, t), dsum.reshape(dm.ch, 1, t), dm.ch, scale,
                                     tag + "attn_bwd")
        dkv, dkr_rot, g_kn = _kv_prep_bwd(sv['kv'], dm.ch, kn_g, dk_cat, dv, tag + "kv_prep")
        grads['k_nope_norm'][i] = g_kn[0]
        grads['w_ukv'][i] = _matmul(sv['ckv_n'], dkv, 'tn', F32, tag + "d_w_ukv")
        dckv_n = _matmul(dkv, wl['w_ukv'], 'nt', BF16, tag + "d_ckv", b_layer=0)
        kr_width = dm.inp - dm.off['kr']
        dproj, d_ckv, d_kr, g_qn, g_qr, g_kr, g_kv = _mla_prep_bwd(
            proj, dm.off['q'], dm.off['ckv'], dm.off['kr'], dm.ch, dm.kvr, tabs, qn_g, qr_g, kr_g, kv_g,
            dq_t, scale, dckv_n, dkr_rot, kr_width, tag + "mla_prep", into=(dproj, _col_block(dm.off['q'], dm.qw)))
        grads['q_nope_norm'][i], grads['q_rope_norm'][i] = g_qn[0], g_qr[0, :ROPE]
        grads['k_rope_norm'][i], grads['kv_norm'][i] = g_kr[0, :ROPE], g_kv[0]
        dproj = _finish_dproj(dproj, dict(ckv=d_ckv, kr=d_kr),
                              dict(a=3 * dm.aw, b=4 * dm.bw, ckv=dm.kvr, q=dm.qw, cz=dm.cw, kr=kr_width), dm)
        if carry is None:
            d_w_in = _matmul(dproj, sv['hn'], 'tn', F32, tag + "d_w_in")
            d_hn = _matmul(dproj, wl['w_in'], 'nn', BF16, tag + "d_hn", b_layer=0)
        else:
            d_w_in, from_chips = _matmul(dproj, sv['hn'], 'tn', F32, tag + "d_w_in", rider=_owner_rider(sums))
            halves = shard_sums(mine, from_sibling, from_chips, gtag)
            d_hn, from_core = _matmul(dproj, wl['w_in'], 'nn', BF16, tag + "d_hn", b_layer=0,
                                      rider=_sibling_rider(halves, False))
            for n, own, recv in zip(BIG, halves, from_core):
                own_half[n][i + 1], sibling_half[n][i + 1] = own, recv
        grads['w_in'][i] = _unarrange_w_in(d_w_in, dm)
        dh, _, g_an = _norm_bwd(sv['h'], attn_norm[i][None, :], d_hn, dh1, tag + "norm1")
        grads['attn_norm'][i] = g_an[0]
        carry = [_shard_major(grads[n][i], BIG_AXIS[n] - 1) for n in BIG]
    grad_x = dh[None]

    from_sibling = _comm_call(_sibling_rider(carry, True), "l0_g_to_sibling")
    mine, sums = chip_sums(carry, from_sibling, "l0_g_")
    from_chips = _comm_call(_owner_rider(sums), "l0_g_to_owner_chips")
    halves = shard_sums(mine, from_sibling, from_chips, "l0_g_")
    from_core = _comm_call(_sibling_rider(halves, False), "l0_g_share_sibling")
    for n, own, recv in zip(BIG, halves, from_core):
        own_half[n][0], sibling_half[n][0] = own, recv

    out_g, out_d, out_m, out_v = {}, {}, {}, {}
    for n in BIG:
        out_g[n], out_d[n], out_m[n], out_v[n] = _adamw_two_halves(
            weights[n], jnp.stack(own_half[n]), jnp.stack(sibling_half[n]), core_flag, mom_m[n], mom_v[n],
            f"adamw_{n}")
    for out in (out_g, out_d, out_m, out_v):
        out['w_in'] = jnp.swapaxes(out['w_in'], 1, 2)
    grads = {n: jnp.stack(grads[n]) for n in SMALL}

    shapes = [grads[n].shape for n in SMALL]
    summed = _unpack(_sum_devices(_gather_devices(_pack([grads[n] for n in SMALL]), "gather_small_grads"),
                                  "sum_small_grads"), shapes)
    small_g = dict(zip(SMALL, summed))
    small_g['conv_w'] = lax.dynamic_slice_in_dim(small_g['conv_w'], shard * conv_w.shape[2], conv_w.shape[2], axis=2)
    local_shapes = [weights[n].shape for n in SMALL]
    d_s, m_s, v_s = _adamw(_pack([weights[n] for n in SMALL]), _pack([small_g[n] for n in SMALL]),
                           _pack([mom_m[n] for n in SMALL]), _pack([mom_v[n] for n in SMALL]), "adamw_small")
    for n, dd, mm, vv in zip(SMALL, _unpack(d_s, local_shapes), _unpack(m_s, local_shapes),
                             _unpack(v_s, local_shapes)):
        out_g[n], out_d[n], out_m[n], out_v[n] = small_g[n], dd, mm, vv

    return (loss, grad_x, *[out_g[n] for n in WEIGHTS], *[out_d[n] for n in WEIGHTS],
            *[out_m[n] for n in WEIGHTS], *[out_v[n] for n in WEIGHTS])
```

```python
import functools

import jax
import jax.numpy as jnp
from jax import lax
from jax.experimental import pallas as pl
from jax.experimental.pallas import tpu as pltpu

F32 = jnp.float32
BF16 = jnp.bfloat16
EPS = 1e-6
HEAD = 128
ROPE = 64
ROPE_BASE = 10000.0
CONV_TAPS = 3
N_SHARD = 4
N_DEV = 8
ADAM_LR = 0.001
ADAM_B1 = 0.9
ADAM_B2 = 0.999
ADAM_EPS = 1e-08
ADAM_WD = 0.01
ADAM_STEP = 10
MESH = pl.DeviceIdType.MESH
VMEM_LIMIT = 56 * 1024 * 1024
HALO_ROWS = 16
ROW_TILES = (512, 256, 128)

WEIGHTS = ['attn_norm', 'w_in', 'sgu_norm', 'w_spatial', 'b_spatial', 'conv_w', 'conv_b', 'kv_norm', 'w_ukv',
           'q_nope_norm', 'q_rope_norm', 'k_nope_norm', 'k_rope_norm', 'out_norm', 'w_out', 'ple_norm',
           'w_ple_gate', 'w_ple_proj']
BIG = ['w_in', 'w_ukv', 'w_out', 'w_ple_gate', 'w_ple_proj']
BIG_AXIS = {'w_in': 1, 'w_ukv': 2, 'w_out': 1, 'w_ple_gate': 1, 'w_ple_proj': 2}
SMALL = [n for n in WEIGHTS if n not in BIG]


def _pick(n, cands):
    for c in cands:
        if n % c == 0:
            return c
    return n


def _params(sem=None):
    return pltpu.CompilerParams(dimension_semantics=sem, vmem_limit_bytes=VMEM_LIMIT)


def _matmul(a, b, mode, out_dtype, name, add=None, b_layer=None, rider=None):
    b_shape = b.shape if b_layer is None else b.shape[1:]
    if mode == 'nn':
        (m, k), n = a.shape, b_shape[1]
    elif mode == 'nt':
        (m, k), n = a.shape, b_shape[0]
    else:
        (k, m), n = a.shape, b_shape[1]
    tm = _pick(m, (1280, 1024, 512, 256, 128))
    tn = _pick(n, ((2048,) if out_dtype == BF16 and add is None else ()) + (1536, 1024, 512, 256, 128))
    tk = k if k <= 2048 else _pick(k, (2048, 1536, 1024, 512, 256, 128))
    nk = k // tk
    if mode == 'tn':
        a_spec = pl.BlockSpec((tk, tm), lambda i, j, kk: (kk, i))
        dims = (((0,), (0,)), ((), ()))
    else:
        a_spec = pl.BlockSpec((tm, tk), lambda i, j, kk: (i, kk))
        dims = (((1,), (0,)), ((), ())) if mode == 'nn' else (((1,), (1,)), ((), ()))
    b_block = (tn, tk) if mode == 'nt' else (tk, tn)
    if b_layer is None:
        b_spec = pl.BlockSpec(b_block, (lambda i, j, kk: (j, kk)) if mode == 'nt' else (lambda i, j, kk: (kk, j)))
    else:
        b_spec = pl.BlockSpec((None,) + b_block, (lambda i, j, kk: (b_layer, j, kk)) if mode == 'nt'
                              else (lambda i, j, kk: (b_layer, kk, j)))
    o_spec = pl.BlockSpec((tm, tn), lambda i, j, kk: (i, j))
    has_add = add is not None

    def body(*refs):
        a_ref, b_ref = refs[0], refs[1]
        add_ref = refs[2] if has_add else None
        o_ref = refs[3] if has_add else refs[2]

        def product():
            return lax.dot_general(a_ref[...], b_ref[...], dims, preferred_element_type=F32)

        def finish(res):
            if has_add:
                res = res + add_ref[...]
            o_ref[...] = res.astype(out_dtype)

        if nk == 1:
            finish(product())
        else:
            acc_ref = refs[-1]
            kk = pl.program_id(2)

            @pl.when(kk == 0)
            def _():
                acc_ref[...] = product()

            @pl.when((kk > 0) & (kk < nk - 1))
            def _():
                acc_ref[...] += product()

            @pl.when(kk == nk - 1)
            def _():
                finish(acc_ref[...] + product())

    in_specs = [a_spec, b_spec] + ([o_spec] if has_add else [])
    args = [a, b] + ([add] if has_add else [])
    grid = (m // tm, n // tn, nk)
    scratch = [pltpu.VMEM((tm, tn), F32)] if nk > 1 else []
    if rider is None:
        return pl.pallas_call(
            body, name=name, grid=grid, in_specs=in_specs, out_specs=o_spec,
            out_shape=jax.ShapeDtypeStruct((m, n), out_dtype), scratch_shapes=scratch,
            compiler_params=_params(("parallel", "parallel", "arbitrary")),
        )(*args)

    n_in, n_rin, n_rout = len(args), len(rider.arrays), len(rider.out_shapes)

    def body_with_rider(*refs):
        r_in = refs[n_in:n_in + n_rin]
        r_out = refs[n_in + n_rin + 1:n_in + n_rin + 1 + n_rout]
        own = refs[:n_in] + refs[n_in + n_rin:n_in + n_rin + 1] + refs[n_in + n_rin + 1 + n_rout:len(refs) - 2]
        send_sems, recv_sems = refs[-2:]
        ids = [pl.program_id(ax) for ax in range(3)]

        @pl.when((ids[0] == 0) & (ids[1] == 0) & (ids[2] == 0))
        def _():
            rider.start(r_in, r_out, send_sems, recv_sems)

        body(*own)

        @pl.when((ids[0] == grid[0] - 1) & (ids[1] == grid[1] - 1) & (ids[2] == grid[2] - 1))
        def _():
            rider.finish(r_in, r_out, send_sems, recv_sems)

    res = pl.pallas_call(
        body_with_rider, name=name, grid=grid, in_specs=in_specs + [_ANY] * n_rin,
        out_specs=[o_spec] + [_ANY] * n_rout,
        out_shape=[jax.ShapeDtypeStruct((m, n), out_dtype)] + rider.out_shapes,
        scratch_shapes=scratch + rider.sems(), input_output_aliases=rider.aliases(n_in, 1),
        compiler_params=_params(("arbitrary", "arbitrary", "arbitrary")),
    )(*args, *rider.arrays)
    return res[0], res[1:]


def _rms(x, n):
    r = lax.rsqrt(jnp.sum(x * x, axis=-1, keepdims=True) * (1.0 / n) + EPS)
    return x * r, r


def _rms_bwd(dxhat, xhat, r, n):
    return r * (dxhat - xhat * (jnp.sum(dxhat * xhat, axis=-1, keepdims=True) * (1.0 / n)))


def _sigmoid(z):
    return 1.0 / (1.0 + jnp.exp(-z))


def _silu_and_grad(z):
    sig = _sigmoid(z)
    return z * sig, sig * (1.0 + z * (1.0 - sig))


def _colsum(x):
    return jnp.sum(x, axis=0, keepdims=True)


def _rope(t, cos_t, sin_a, sin_b):
    return t * cos_t + pltpu.roll(t, 96, 1) * sin_a + pltpu.roll(t, 32, 1) * sin_b


def _rope_bwd(d, cos_t, sin_a, sin_b):
    return d * cos_t + pltpu.roll(d * sin_a, 32, 1) + pltpu.roll(d * sin_b, 96, 1)


def _shift_down(g, first_row):
    row = lax.broadcasted_iota(jnp.int32, g.shape, 0)
    return jnp.where(row == 0, first_row, pltpu.roll(g, 1, 0))


def _shift_up(g, last_row):
    n = g.shape[0]
    row = lax.broadcasted_iota(jnp.int32, g.shape, 0)
    return jnp.where(row == n - 1, last_row, pltpu.roll(g, n - 1, 0))


def _row_spec(r, w, col=0):
    return pl.BlockSpec((r, w), lambda i: (i, col))


def _const_spec(shape):
    nd = len(shape)
    return pl.BlockSpec(shape, lambda i: (0,) * nd)


def _col_block(off, w):
    assert off % w == 0, (off, w)
    return off // w


def _zero_at_first_step(refs):
    @pl.when(pl.program_id(0) == 0)
    def _():
        for ref in refs:
            ref[...] = jnp.zeros(ref.shape, ref.dtype)


def _row_call(body, name, t, r, in_specs, args, out_specs, out_shapes, scratch=(), into=None):
    if into is None:
        return pl.pallas_call(
            body, name=name, grid=(t // r,), in_specs=in_specs, out_specs=out_specs, out_shape=out_shapes,
            scratch_shapes=list(scratch), compiler_params=_params(("arbitrary",)),
        )(*args)
    buf, col = into
    single = not isinstance(out_specs, (list, tuple))
    specs = [out_specs] if single else list(out_specs)
    shapes = [out_shapes] if single else list(out_shapes)
    width = shapes[0].shape[1]
    assert shapes[0].dtype == buf.dtype and buf.shape[0] == t
    specs[0] = _row_spec(r, width, col)
    shapes[0] = jax.ShapeDtypeStruct(buf.shape, buf.dtype)
    n_in = len(args)

    def body_in_place(*refs):
        body(*refs[:n_in], *refs[n_in + 1:])

    res = pl.pallas_call(
        body_in_place, name=name, grid=(t // r,), in_specs=list(in_specs) + [_ANY], out_specs=specs, out_shape=shapes,
        scratch_shapes=list(scratch), input_output_aliases={n_in: 0}, compiler_params=_params(("arbitrary",)),
    )(*args, buf)
    return res[0] if single else res


def _norm_fwd(h, g, name):
    t, d = h.shape
    r = _pick(t, ROW_TILES)

    def body(h_ref, g_ref, o_ref):
        xhat, _ = _rms(h_ref[...], d)
        o_ref[...] = (xhat * g_ref[...]).astype(BF16)

    return _row_call(body, name, t, r, [_row_spec(r, d), _const_spec((1, d))], (h, g),
                     _row_spec(r, d), jax.ShapeDtypeStruct((t, d), BF16))


def _norm_bwd(h, g, d_hn, d_res, name):
    t, d = h.shape
    r = _pick(t, ROW_TILES)

    def body(h_ref, g_ref, dy_ref, dres_ref, dh_ref, dhb_ref, dg_ref):
        _zero_at_first_step([dg_ref])
        xhat, rr = _rms(h_ref[...], d)
        dy = dy_ref[...].astype(F32)
        dg_ref[...] += _colsum(dy * xhat)
        dh = dres_ref[...] + _rms_bwd(dy * g_ref[...], xhat, rr, d)
        dh_ref[...] = dh
        dhb_ref[...] = dh.astype(BF16)

    return _row_call(body, name, t, r,
                     [_row_spec(r, d), _const_spec((1, d)), _row_spec(r, d), _row_spec(r, d)], (h, g, d_hn, d_res),
                     [_row_spec(r, d), _row_spec(r, d), _const_spec((1, d))],
                     [jax.ShapeDtypeStruct((t, d), F32), jax.ShapeDtypeStruct((t, d), BF16),
                      jax.ShapeDtypeStruct((1, d), F32)])


def _sgu_scores(v, gs_ref, ws_ref, bb_ref, s_scr, r, ah, keep=None):
    for kk in range(r // HEAD):
        for hh in range(ah):
            rows, cols = slice(kk * HEAD, (kk + 1) * HEAD), slice(hh * HEAD, (hh + 1) * HEAD)
            vhat, rv = _rms(v[rows, cols], HEAD)
            vn = vhat * gs_ref[pl.ds(hh, 1), :]
            s_scr[rows, cols] = jnp.dot(ws_ref[hh], vn.astype(BF16), preferred_element_type=F32) + bb_ref[hh]
            if keep is not None:
                keep[(kk, hh)] = (vhat, rv, vn)


def _sgu_fwd(proj, off, aw, gs, ws, bb, ga, name, into=None):
    t = proj.shape[0]
    ah = aw // HEAD
    r = _pick(t, ROW_TILES)
    cb = _col_block(off, aw)

    def body(u_ref, v_ref, z_ref, gs_ref, ws_ref, bb_ref, ga_ref, o_ref, s_scr):
        _sgu_scores(v_ref[...].astype(F32), gs_ref, ws_ref, bb_ref, s_scr, r, ah)
        sil, _ = _silu_and_grad(z_ref[...].astype(F32))
        yhat, _ = _rms(u_ref[...].astype(F32) * s_scr[...] * sil, aw)
        o_ref[...] = (yhat * ga_ref[...]).astype(BF16)

    return _row_call(
        body, name, t, r,
        [_row_spec(r, aw, cb), _row_spec(r, aw, cb + 1), _row_spec(r, aw, cb + 2), _const_spec((ah, HEAD)),
         _const_spec((ah, HEAD, HEAD)), _const_spec((ah, HEAD, HEAD)), _const_spec((1, aw))],
        (proj, proj, proj, gs, ws, bb, ga),
        _row_spec(r, aw), jax.ShapeDtypeStruct((t, aw), BF16), scratch=[pltpu.VMEM((r, aw), F32)], into=into)


def _sgu_bwd(proj, off, aw, gs, ws, ws_t, bb, ga, dy, out_width, name):
    t = proj.shape[0]
    ah = aw // HEAD
    r = _pick(t, ROW_TILES)
    assert off == 0
    cb = _col_block(off, aw)

    def body(u_ref, v_ref, z_ref, gs_ref, ws_ref, wst_ref, bb_ref, ga_ref, dy_ref,
             d_ref, dgs_ref, dws_ref, db_ref, dga_ref, s_scr, dv_scr):
        _zero_at_first_step([dgs_ref, dws_ref, db_ref, dga_ref])
        keep = {}
        _sgu_scores(v_ref[...].astype(F32), gs_ref, ws_ref, bb_ref, s_scr, r, ah, keep)
        u, z, s = u_ref[...].astype(F32), z_ref[...].astype(F32), s_scr[...]
        sil, dsil = _silu_and_grad(z)
        yhat, rr = _rms(u * s * sil, aw)
        dy_f = dy_ref[...].astype(F32)
        dga_ref[...] += _colsum(dy_f * yhat)
        dya = _rms_bwd(dy_f * ga_ref[...], yhat, rr, aw)
        d_ref[:, 0:aw] = (dya * s * sil).astype(BF16)
        d_ref[:, 2 * aw:3 * aw] = (dya * u * s * dsil).astype(BF16)
        ds = dya * u * sil
        for kk in range(r // HEAD):
            for hh in range(ah):
                rows, cols = slice(kk * HEAD, (kk + 1) * HEAD), slice(hh * HEAD, (hh + 1) * HEAD)
                vhat, rv, vn = keep[(kk, hh)]
                ds_blk = ds[rows, cols]
                db_ref[hh] += jnp.sum(ds_blk, axis=1, keepdims=True)
                ds_b = ds_blk.astype(BF16)
                dws_ref[hh] += lax.dot_general(ds_b, vn.astype(BF16), (((1,), (1,)), ((), ())),
                                               preferred_element_type=F32)
                dvn = jnp.dot(wst_ref[hh], ds_b, preferred_element_type=F32)
                dgs_ref[pl.ds(hh, 1), :] += _colsum(dvn * vhat)
                dv_scr[rows, cols] = _rms_bwd(dvn * gs_ref[pl.ds(hh, 1), :], vhat, rv, HEAD)
        d_ref[:, aw:2 * aw] = dv_scr[...].astype(BF16)

    return _row_call(
        body, name, t, r,
        [_row_spec(r, aw, cb), _row_spec(r, aw, cb + 1), _row_spec(r, aw, cb + 2), _const_spec((ah, HEAD)),
         _const_spec((ah, HEAD, HEAD)), _const_spec((ah, HEAD, HEAD)), _const_spec((ah, HEAD, HEAD)),
         _const_spec((1, aw)), _row_spec(r, aw, 0)],
        (proj, proj, proj, gs, ws, ws_t, bb, ga, dy),
        [_row_spec(r, 3 * aw), _const_spec((ah, HEAD)), _const_spec((ah, HEAD, HEAD)), _const_spec((ah, HEAD, 1)),
         _const_spec((1, aw))],
        [jax.ShapeDtypeStruct((t, out_width), BF16), jax.ShapeDtypeStruct((ah, HEAD), F32),
         jax.ShapeDtypeStruct((ah, HEAD, HEAD), F32), jax.ShapeDtypeStruct((ah, HEAD, 1), F32),
         jax.ShapeDtypeStruct((1, aw), F32)],
        scratch=[pltpu.VMEM((r, aw), F32), pltpu.VMEM((r, aw), F32)])


def _halo_specs(t, r, w, col, rows):
    per = r // rows
    last = t // rows - 1
    prev = pl.BlockSpec((rows, w), lambda i: (jnp.maximum(i * per - 1, 0), col))
    nxt = pl.BlockSpec((rows, w), lambda i: (jnp.minimum((i + 1) * per, last), col))
    return prev, nxt


def _edge_rows(prev_ref, next_ref, n_steps):
    i = pl.program_id(0)
    rows = prev_ref.shape[0]
    before = prev_ref[...].astype(F32)[rows - 1:rows, :] * (i > 0).astype(F32)
    after = next_ref[...].astype(F32)[0:1, :] * (i < n_steps - 1).astype(F32)
    return before, after


def _conv_fwd(proj, off, bw, cw, cb_, gb, name, into=None):
    t = proj.shape[0]
    r = _pick(t, ROW_TILES)
    n_steps = t // r
    c0 = _col_block(off, bw)
    cp, cn = _halo_specs(t, r, bw, c0 + 1, HALO_ROWS)
    hp, hn = _halo_specs(t, r, bw, c0 + 2, HALO_ROWS)

    def body(b_ref, c_ref, h_ref, z_ref, cp_ref, cn_ref, hp_ref, hn_ref, cw_ref, cb_ref, gb_ref, o_ref, yc_ref):
        g = c_ref[...].astype(F32) * h_ref[...].astype(F32)
        c_before, c_after = _edge_rows(cp_ref, cn_ref, n_steps)
        h_before, h_after = _edge_rows(hp_ref, hn_ref, n_steps)
        yconv = (cb_ref[...] + cw_ref[0:1, :] * _shift_down(g, c_before * h_before) + cw_ref[1:2, :] * g
                 + cw_ref[2:3, :] * _shift_up(g, c_after * h_after))
        yc_ref[...] = yconv
        sil, _ = _silu_and_grad(z_ref[...].astype(F32))
        yhat, _ = _rms(b_ref[...].astype(F32) * yconv * sil, bw)
        o_ref[...] = (yhat * gb_ref[...]).astype(BF16)

    return _row_call(
        body, name, t, r,
        [_row_spec(r, bw, c0), _row_spec(r, bw, c0 + 1), _row_spec(r, bw, c0 + 2), _row_spec(r, bw, c0 + 3),
         cp, cn, hp, hn, _const_spec((CONV_TAPS, bw)), _const_spec((1, bw)), _const_spec((1, bw))],
        (proj, proj, proj, proj, proj, proj, proj, proj, cw, cb_, gb),
        [_row_spec(r, bw), _row_spec(r, bw)],
        [jax.ShapeDtypeStruct((t, bw), BF16), jax.ShapeDtypeStruct((t, bw), F32)], into=into)


def _conv_bwd_gate(proj, off, bw, yconv, gb, dy, name):
    t = proj.shape[0]
    r = _pick(t, ROW_TILES)
    c0 = _col_block(off, bw)

    def body(b_ref, z_ref, yc_ref, gb_ref, dy_ref, dyc_ref, db_ref, dz_ref, dgb_ref, dcb_ref):
        _zero_at_first_step([dgb_ref, dcb_ref])
        b, z, yconv_v = b_ref[...].astype(F32), z_ref[...].astype(F32), yc_ref[...]
        sil, dsil = _silu_and_grad(z)
        yhat, rr = _rms(b * yconv_v * sil, bw)
        dy_f = dy_ref[...].astype(F32)
        dgb_ref[...] += _colsum(dy_f * yhat)
        dyb = _rms_bwd(dy_f * gb_ref[...], yhat, rr, bw)
        dyc = dyb * b * sil
        dyc_ref[...] = dyc
        dcb_ref[...] += _colsum(dyc)
        db_ref[...] = (dyb * yconv_v * sil).astype(BF16)
        dz_ref[...] = (dyb * b * yconv_v * dsil).astype(BF16)

    return _row_call(
        body, name, t, r,
        [_row_spec(r, bw, c0), _row_spec(r, bw, c0 + 3), _row_spec(r, bw), _const_spec((1, bw)), _row_spec(r, bw, 1)],
        (proj, proj, yconv, gb, dy),
        [_row_spec(r, bw), _row_spec(r, bw), _row_spec(r, bw), _const_spec((1, bw)), _const_spec((1, bw))],
        [jax.ShapeDtypeStruct((t, bw), F32), jax.ShapeDtypeStruct((t, bw), BF16), jax.ShapeDtypeStruct((t, bw), BF16),
         jax.ShapeDtypeStruct((1, bw), F32), jax.ShapeDtypeStruct((1, bw), F32)])


def _conv_bwd_taps(proj, off, bw, dyc, cw, d_gate_b, d_gate_z, name, into=None):
    t = proj.shape[0]
    r = _pick(t, ROW_TILES)
    n_steps = t // r
    c0 = _col_block(off, bw)
    cp, cn = _halo_specs(t, r, bw, c0 + 1, HALO_ROWS)
    hp, hn = _halo_specs(t, r, bw, c0 + 2, HALO_ROWS)
    dp, dn = _halo_specs(t, r, bw, 0, 8)

    def body(c_ref, h_ref, cp_ref, cn_ref, hp_ref, hn_ref, d_ref, dp_ref, dn_ref, cw_ref, dgb_ref, dgz_ref,
             db_ref, dcw_ref):
        _zero_at_first_step([dcw_ref])
        c, h, d = c_ref[...].astype(F32), h_ref[...].astype(F32), d_ref[...]
        g = c * h
        c_before, c_after = _edge_rows(cp_ref, cn_ref, n_steps)
        h_before, h_after = _edge_rows(hp_ref, hn_ref, n_steps)
        d_before, d_after = _edge_rows(dp_ref, dn_ref, n_steps)
        dg = (cw_ref[0:1, :] * _shift_up(d, d_after) + cw_ref[1:2, :] * d + cw_ref[2:3, :] * _shift_down(d, d_before))
        db_ref[:, 0:bw] = dgb_ref[...]
        db_ref[:, bw:2 * bw] = (dg * h).astype(BF16)
        db_ref[:, 2 * bw:3 * bw] = (dg * c).astype(BF16)
        db_ref[:, 3 * bw:4 * bw] = dgz_ref[...]
        dcw_ref[0:1, :] += _colsum(d * _shift_down(g, c_before * h_before))
        dcw_ref[1:2, :] += _colsum(d * g)
        dcw_ref[2:3, :] += _colsum(d * _shift_up(g, c_after * h_after))

    return _row_call(
        body, name, t, r,
        [_row_spec(r, bw, c0 + 1), _row_spec(r, bw, c0 + 2), cp, cn, hp, hn, _row_spec(r, bw), dp, dn,
         _const_spec((CONV_TAPS, bw)), _row_spec(r, bw), _row_spec(r, bw)],
        (proj, proj, proj, proj, proj, proj, dyc, dyc, dyc, cw, d_gate_b, d_gate_z),
        [_row_spec(r, 4 * bw), _const_spec((CONV_TAPS, bw))],
        [jax.ShapeDtypeStruct((t, 4 * bw), BF16), jax.ShapeDtypeStruct((CONV_TAPS, bw), F32)], into=into)


def _mla_prep_fwd(proj, q_off, ckv_off, kr_off, ch, kvr, tabs, qn_g, qr_g, kr_g, kv_g, name):
    t = proj.shape[0]
    r = _pick(t, ROW_TILES)
    qw = ch * 2 * HEAD
    cos_t, sin_a, sin_b = tabs

    def body(q_ref, ckv_ref, kr_ref, cos_ref, sa_ref, sb_ref, qn_ref, qr_ref, krg_ref, kvg_ref,
             qo_ref, co_ref, ko_ref):
        cos_v, sa, sb = cos_ref[...], sa_ref[...], sb_ref[...]
        for hh in range(ch):
            lo = hh * 2 * HEAD
            nhat, _ = _rms(q_ref[:, lo:lo + HEAD].astype(F32), HEAD)
            qo_ref[:, lo:lo + HEAD] = (nhat * qn_ref[...]).astype(BF16)
            rhat, _ = _rms(q_ref[:, lo + HEAD:lo + 2 * HEAD].astype(F32), ROPE)
            qo_ref[:, lo + HEAD:lo + 2 * HEAD] = _rope(rhat * qr_ref[...], cos_v, sa, sb).astype(BF16)
        khat, _ = _rms(kr_ref[...].astype(F32), ROPE)
        ko_ref[...] = _rope(khat * krg_ref[...], cos_v, sa, sb).astype(BF16)
        chat, _ = _rms(ckv_ref[...].astype(F32), kvr)
        co_ref[...] = (chat * kvg_ref[...]).astype(BF16)

    tab = _row_spec(r, HEAD)
    gain = _const_spec((1, HEAD))
    return _row_call(
        body, name, t, r,
        [_row_spec(r, qw, _col_block(q_off, qw)), _row_spec(r, kvr, _col_block(ckv_off, kvr)),
         _row_spec(r, HEAD, _col_block(kr_off, HEAD)), tab, tab, tab, gain, gain, gain, _const_spec((1, kvr))],
        (proj, proj, proj, cos_t, sin_a, sin_b, qn_g, qr_g, kr_g, kv_g),
        [_row_spec(r, qw), _row_spec(r, kvr), _row_spec(r, HEAD)],
        [jax.ShapeDtypeStruct((t, qw), BF16), jax.ShapeDtypeStruct((t, kvr), BF16),
         jax.ShapeDtypeStruct((t, HEAD), BF16)])


def _mla_prep_bwd(proj, q_off, ckv_off, kr_off, ch, kvr, tabs, qn_g, qr_g, kr_g, kv_g, dq_cat_t, dq_scale, dckv_n,
                  dkr_rot, kr_width, name, into=None):
    t = proj.shape[0]
    r = _pick(t, ROW_TILES)
    qw = ch * 2 * HEAD
    cos_t, sin_a, sin_b = tabs

    def body(q_ref, ckv_ref, kr_ref, cos_ref, sa_ref, sb_ref, qn_ref, qr_ref, krg_ref, kvg_ref,
             dq_ref, dc_ref, dk_ref, dqo_ref, dco_ref, dko_ref, dqn_ref, dqr_ref, dkrg_ref, dkvg_ref):
        _zero_at_first_step([dqn_ref, dqr_ref, dkrg_ref, dkvg_ref])
        cos_v, sa, sb = cos_ref[...], sa_ref[...], sb_ref[...]
        dq = dq_ref[...].T * dq_scale
        for hh in range(ch):
            lo = hh * 2 * HEAD
            nhat, nr = _rms(q_ref[:, lo:lo + HEAD].astype(F32), HEAD)
            d_n = dq[:, lo:lo + HEAD]
            dqn_ref[...] += _colsum(d_n * nhat)
            dqo_ref[:, lo:lo + HEAD] = _rms_bwd(d_n * qn_ref[...], nhat, nr, HEAD).astype(BF16)
            rhat, rr = _rms(q_ref[:, lo + HEAD:lo + 2 * HEAD].astype(F32), ROPE)
            d_t = _rope_bwd(dq[:, lo + HEAD:lo + 2 * HEAD], cos_v, sa, sb)
            dqr_ref[...] += _colsum(d_t * rhat)
            dqo_ref[:, lo + HEAD:lo + 2 * HEAD] = _rms_bwd(d_t * qr_ref[...], rhat, rr, ROPE).astype(BF16)
        khat, kr_r = _rms(kr_ref[...].astype(F32), ROPE)
        d_k = _rope_bwd(dk_ref[...], cos_v, sa, sb)
        dkrg_ref[...] += _colsum(d_k * khat)
        dko_ref[:, 0:HEAD] = _rms_bwd(d_k * krg_ref[...], khat, kr_r, ROPE).astype(BF16)
        if kr_width > HEAD:
            dko_ref[:, HEAD:kr_width] = jnp.zeros((r, kr_width - HEAD), BF16)
        chat, cr = _rms(ckv_ref[...].astype(F32), kvr)
        d_c = dc_ref[...].astype(F32)
        dkvg_ref[...] += _colsum(d_c * chat)
        dco_ref[...] = _rms_bwd(d_c * kvg_ref[...], chat, cr, kvr).astype(BF16)

    tab = _row_spec(r, HEAD)
    gain = _const_spec((1, HEAD))
    return _row_call(
        body, name, t, r,
        [_row_spec(r, qw, _col_block(q_off, qw)), _row_spec(r, kvr, _col_block(ckv_off, kvr)),
         _row_spec(r, HEAD, _col_block(kr_off, HEAD)), tab, tab, tab, gain, gain, gain, _const_spec((1, kvr)),
         pl.BlockSpec((qw, r), lambda i: (0, i)), _row_spec(r, kvr), _row_spec(r, HEAD)],
        (proj, proj, proj, cos_t, sin_a, sin_b, qn_g, qr_g, kr_g, kv_g, dq_cat_t, dckv_n, dkr_rot),
        [_row_spec(r, qw), _row_spec(r, kvr), _row_spec(r, kr_width), gain, gain, gain, _const_spec((1, kvr))],
        [jax.ShapeDtypeStruct((t, qw), BF16), jax.ShapeDtypeStruct((t, kvr), BF16),
         jax.ShapeDtypeStruct((t, kr_width), BF16), jax.ShapeDtypeStruct((1, HEAD), F32),
         jax.ShapeDtypeStruct((1, HEAD), F32), jax.ShapeDtypeStruct((1, HEAD), F32),
         jax.ShapeDtypeStruct((1, kvr), F32)], into=into)


def _kv_prep_fwd(kv, kr_rot, ch, kn_g, name):
    t = kv.shape[0]
    r = _pick(t, ROW_TILES)
    qw = ch * 2 * HEAD

    def body(kv_ref, kr_ref, kn_ref, ko_ref, kt_ref, vo_ref):
        ones = jnp.ones((r, HEAD), BF16)
        for hh in range(ch):
            lo = hh * 2 * HEAD
            nhat, _ = _rms(kv_ref[:, lo:lo + HEAD], HEAD)
            ko_ref[:, lo:lo + HEAD] = (nhat * kn_ref[...]).astype(BF16)
            ko_ref[:, lo + HEAD:lo + 2 * HEAD] = kr_ref[...]
            vo_ref[:, lo:lo + HEAD] = kv_ref[:, lo + HEAD:lo + 2 * HEAD].astype(BF16)
            vo_ref[:, lo + HEAD:lo + 2 * HEAD] = ones
        kt_ref[...] = ko_ref[...].astype(F32).T.astype(BF16)

    return _row_call(
        body, name, t, r, [_row_spec(r, qw), _row_spec(r, HEAD), _const_spec((1, HEAD))], (kv, kr_rot, kn_g),
        [_row_spec(r, qw), pl.BlockSpec((qw, r), lambda i: (0, i)), _row_spec(r, qw)],
        [jax.ShapeDtypeStruct((t, qw), BF16), jax.ShapeDtypeStruct((qw, t), BF16),
         jax.ShapeDtypeStruct((t, qw), BF16)])


def _kv_prep_bwd(kv, ch, kn_g, dk_cat, dv, name):
    t = kv.shape[0]
    r = _pick(t, ROW_TILES)
    qw = ch * 2 * HEAD

    def body(kv_ref, kn_ref, dk_ref, dv_ref, dkv_ref, dkr_ref, dkn_ref):
        _zero_at_first_step([dkn_ref])
        dkr = jnp.zeros((r, HEAD), F32)
        for hh in range(ch):
            lo = hh * 2 * HEAD
            nhat, nr = _rms(kv_ref[:, lo:lo + HEAD], HEAD)
            d_n = dk_ref[:, lo:lo + HEAD].astype(F32)
            dkn_ref[...] += _colsum(d_n * nhat)
            dkv_ref[:, lo:lo + HEAD] = _rms_bwd(d_n * kn_ref[...], nhat, nr, HEAD).astype(BF16)
            dkv_ref[:, lo + HEAD:lo + 2 * HEAD] = dv_ref[:, hh * HEAD:(hh + 1) * HEAD]
            dkr = dkr + dk_ref[:, lo + HEAD:lo + 2 * HEAD].astype(F32)
        dkr_ref[...] = dkr

    return _row_call(
        body, name, t, r,
        [_row_spec(r, qw), _const_spec((1, HEAD)), _row_spec(r, qw), _row_spec(r, ch * HEAD)], (kv, kn_g, dk_cat, dv),
        [_row_spec(r, qw), _row_spec(r, HEAD), _const_spec((1, HEAD))],
        [jax.ShapeDtypeStruct((t, qw), BF16), jax.ShapeDtypeStruct((t, HEAD), F32),
         jax.ShapeDtypeStruct((1, HEAD), F32)])


def _attn_tiles(t):
    return _pick(t, (2048, 1024, 512, 256, 128)), _pick(t, (1024, 512, 256, 128))


_NT = (((1,), (1,)), ((), ()))
LOG2E = 1.4426950408889634


def _attn_fwd(q_cat, k_cat, v_aug, ch, scale, name):
    t = q_cat.shape[0]
    tq, tk = _attn_tiles(t)
    nk = t // tk
    c2 = scale * LOG2E

    def body(q_ref, k_ref, v_ref, o_ref, lse_ref, s_scr, m_scr, acc_scr):
        j = pl.program_id(2)

        def scores(slot):
            s_scr[slot] = lax.dot_general(q_ref[...], k_ref[...], _NT, preferred_element_type=F32) * c2

        def absorb(slot):
            s = s_scr[slot]
            m_old = m_scr[...]
            m_new = jnp.maximum(m_old, jnp.max(s, axis=-1, keepdims=True))
            p = jnp.exp2(s - m_new).astype(BF16)
            acc_scr[...] = (jnp.exp2(m_old - m_new) * acc_scr[...]
                            + jnp.dot(p, v_ref[...], preferred_element_type=F32))
            m_scr[...] = m_new

        @pl.when(j == 0)
        def _():
            m_scr[...] = jnp.full(m_scr.shape, -jnp.inf, F32)
            acc_scr[...] = jnp.zeros(acc_scr.shape, F32)
            scores(0)

        for parity in (0, 1):
            @pl.when((j > 0) & (j < nk) & (j % 2 == parity))
            def _():
                scores(parity)
                absorb(1 - parity)

        @pl.when(j == nk)
        def _():
            absorb((nk - 1) % 2)
            acc = acc_scr[...]
            l_sum = acc[:, HEAD:]
            o_ref[...] = (acc[:, :HEAD] / l_sum).astype(BF16)
            lse_ref[0] = m_scr[...] + jnp.log(l_sum[:, 0:1]) * LOG2E

    return pl.pallas_call(
        body, name=name, grid=(ch, t // tq, nk + 1),
        in_specs=[pl.BlockSpec((tq, 2 * HEAD), lambda h, i, j: (i, h)),
                  pl.BlockSpec((tk, 2 * HEAD), lambda h, i, j: (jnp.minimum(j, nk - 1), h)),
                  pl.BlockSpec((tk, 2 * HEAD), lambda h, i, j: (jnp.maximum(j - 1, 0), h))],
        out_specs=[pl.BlockSpec((tq, HEAD), lambda h, i, j: (i, h)),
                   pl.BlockSpec((1, tq, 1), lambda h, i, j: (h, i, 0))],
        out_shape=[jax.ShapeDtypeStruct((t, ch * HEAD), BF16), jax.ShapeDtypeStruct((ch, t, 1), F32)],
        scratch_shapes=[pltpu.VMEM((2, tq, tk), F32), pltpu.VMEM((tq, 1), F32), pltpu.VMEM((tq, 2 * HEAD), F32)],
        compiler_params=_params(("parallel", "parallel", "arbitrary")),
    )(q_cat, k_cat, v_aug)


def _attn_bwd(q_cat, k_cat, k_cat_t, v_aug, do, lse_row, d_row, ch, scale, name):
    t = q_cat.shape[0]
    tk = _pick(t, (1024, 512, 256, 128))
    tq = _pick(t, (1024, 512, 256, 128))
    nk, nq = t // tk, t // tq
    c2 = scale * LOG2E

    def body(q_ref, do_ref, qp_ref, dop_ref, lse_ref, d_ref, k_ref, kt_ref, v_ref,
             dqt_ref, dk_ref, dv_ref, s_scr, dp_scr, dk_scr, dv_scr):
        ki, j = pl.program_id(1), pl.program_id(2)

        def products(slot):
            s_scr[slot] = lax.dot_general(k_ref[...], q_ref[...], _NT, preferred_element_type=F32) * c2
            dp_scr[slot] = lax.dot_general(v_ref[...], do_ref[...], _NT, preferred_element_type=F32)

        def absorb(slot):
            q, do_v = qp_ref[...], dop_ref[...]
            pt = jnp.exp2(s_scr[slot] - lse_ref[0])
            dv_scr[...] += jnp.dot(pt.astype(BF16), do_v, preferred_element_type=F32)
            dst = (pt * (dp_scr[slot] - d_ref[0])).astype(BF16)
            dk_scr[...] += jnp.dot(dst, q, preferred_element_type=F32)
            part = jnp.dot(kt_ref[...], dst, preferred_element_type=F32)
            cols = pl.ds(pl.multiple_of((j - 1) * tq, tq), tq)

            @pl.when(ki == 0)
            def _():
                dqt_ref[:, cols] = part

            @pl.when(ki > 0)
            def _():
                dqt_ref[:, cols] += part

        @pl.when(j == 0)
        def _():
            dk_scr[...] = jnp.zeros(dk_scr.shape, F32)
            dv_scr[...] = jnp.zeros(dv_scr.shape, F32)
            products(0)

        for parity in (0, 1):
            @pl.when((j > 0) & (j < nq) & (j % 2 == parity))
            def _():
                products(parity)
                absorb(1 - parity)

        @pl.when(j == nq)
        def _():
            absorb((nq - 1) % 2)
            dk_ref[...] = (dk_scr[...] * scale).astype(BF16)
            dv_ref[...] = dv_scr[...].astype(BF16)

    def cur(i):
        return jnp.minimum(i, nq - 1)

    def prev(i):
        return jnp.maximum(i - 1, 0)

    stat = pl.BlockSpec((1, 1, tq), lambda h, j, i: (h, 0, prev(i)))
    return pl.pallas_call(
        body, name=name, grid=(ch, nk, nq + 1),
        in_specs=[pl.BlockSpec((tq, 2 * HEAD), lambda h, j, i: (cur(i), h)),
                  pl.BlockSpec((tq, HEAD), lambda h, j, i: (cur(i), h)),
                  pl.BlockSpec((tq, 2 * HEAD), lambda h, j, i: (prev(i), h)),
                  pl.BlockSpec((tq, HEAD), lambda h, j, i: (prev(i), h)), stat, stat,
                  pl.BlockSpec((tk, 2 * HEAD), lambda h, j, i: (j, h)),
                  pl.BlockSpec((2 * HEAD, tk), lambda h, j, i: (h, j)),
                  pl.BlockSpec((tk, HEAD), lambda h, j, i: (j, 2 * h))],
        out_specs=[pl.BlockSpec((2 * HEAD, t), lambda h, j, i: (h, 0)),
                   pl.BlockSpec((tk, 2 * HEAD), lambda h, j, i: (j, h)),
                   pl.BlockSpec((tk, HEAD), lambda h, j, i: (j, h))],
        out_shape=[jax.ShapeDtypeStruct((ch * 2 * HEAD, t), F32), jax.ShapeDtypeStruct((t, ch * 2 * HEAD), BF16),
                   jax.ShapeDtypeStruct((t, ch * HEAD), BF16)],
        scratch_shapes=[pltpu.VMEM((2, tk, tq), F32), pltpu.VMEM((2, tk, tq), F32),
                        pltpu.VMEM((tk, 2 * HEAD), F32), pltpu.VMEM((tk, HEAD), F32)],
        compiler_params=_params(("parallel", "arbitrary", "arbitrary")),
    )(q_cat, do, q_cat, do, lse_row, d_row, k_cat, k_cat_t, v_aug)


def _attn_post_fwd(o, proj, z_off, cw, gc, name, into=None):
    t = o.shape[0]
    r = _pick(t, ROW_TILES)

    def body(o_ref, z_ref, gc_ref, y_ref):
        sil, _ = _silu_and_grad(z_ref[...].astype(F32))
        yhat, _ = _rms(o_ref[...].astype(F32) * sil, cw)
        y_ref[...] = (yhat * gc_ref[...]).astype(BF16)

    return _row_call(body, name, t, r,
                     [_row_spec(r, cw), _row_spec(r, cw, _col_block(z_off, cw)), _const_spec((1, cw))], (o, proj, gc),
                     _row_spec(r, cw), jax.ShapeDtypeStruct((t, cw), BF16), into=into)


def _attn_post_bwd(o, proj, z_off, cw, gc, dy, dy_col, name, into=None):
    t = o.shape[0]
    ch = cw // HEAD
    r = _pick(t, ROW_TILES)

    def body(o_ref, z_ref, gc_ref, dy_ref, dz_ref, do_ref, ds_ref, dgc_ref):
        _zero_at_first_step([dgc_ref])
        o_v, z = o_ref[...].astype(F32), z_ref[...].astype(F32)
        sil, dsil = _silu_and_grad(z)
        yhat, rr = _rms(o_v * sil, cw)
        dy_f = dy_ref[...].astype(F32)
        dgc_ref[...] += _colsum(dy_f * yhat)
        dyc = _rms_bwd(dy_f * gc_ref[...], yhat, rr, cw)
        do_b = (dyc * sil).astype(BF16)
        do_ref[...] = do_b
        dz_ref[...] = (dyc * o_v * dsil).astype(BF16)
        prod = do_b.astype(F32) * o_v
        for hh in range(ch):
            ds_ref[hh] = jnp.sum(prod[:, hh * HEAD:(hh + 1) * HEAD], axis=-1, keepdims=True)

    return _row_call(
        body, name, t, r,
        [_row_spec(r, cw), _row_spec(r, cw, _col_block(z_off, cw)), _const_spec((1, cw)), _row_spec(r, cw, dy_col)],
        (o, proj, gc, dy),
        [_row_spec(r, cw), _row_spec(r, cw), pl.BlockSpec((ch, r, 1), lambda i: (0, i, 0)), _const_spec((1, cw))],
        [jax.ShapeDtypeStruct((t, cw), BF16), jax.ShapeDtypeStruct((t, cw), BF16),
         jax.ShapeDtypeStruct((ch, t, 1), F32), jax.ShapeDtypeStruct((1, cw), F32)], into=into)


def _ple_proj_fwd(p_b, w, h1, gpre, name):
    t, k = p_b.shape
    d = w.shape[2]
    tm, tn = _pick(t, (1024, 512, 256, 128)), _pick(d, (1024, 512, 256, 128))

    def body(p_ref, w_ref, h_ref, g_ref, pp_ref, o_ref):
        pp = jnp.dot(p_ref[...], w_ref[...], preferred_element_type=F32)
        pp_ref[...] = pp
        o_ref[...] = h_ref[...] + _sigmoid(g_ref[...]) * pp

    tile = pl.BlockSpec((tm, tn), lambda i, j: (i, j))
    return pl.pallas_call(
        body, name=name, grid=(t // tm, d // tn),
        in_specs=[pl.BlockSpec((tm, k), lambda i, j: (i, 0)), pl.BlockSpec((None, k, tn), lambda i, j: (0, 0, j)),
                  tile, tile],
        out_specs=[tile, tile], out_shape=[jax.ShapeDtypeStruct((t, d), F32)] * 2,
        compiler_params=_params(("parallel", "parallel")))(p_b, w, h1, gpre)


def _ple_bwd(gpre, pp, dh, name):
    t, d = dh.shape
    r = _pick(t, ROW_TILES)

    def body(g_ref, p_ref, dh_ref, dg_ref, dp_ref):
        sig = _sigmoid(g_ref[...])
        dh_v = dh_ref[...]
        dg_ref[...] = (dh_v * p_ref[...] * sig * (1.0 - sig)).astype(BF16)
        dp_ref[...] = (dh_v * sig).astype(BF16)

    return _row_call(body, name, t, r, [_row_spec(r, d)] * 3, (gpre, pp, dh), [_row_spec(r, d)] * 2,
                     [jax.ShapeDtypeStruct((t, d), BF16)] * 2)


def _loss_and_grad(h, target, name):
    t, d = h.shape
    r = _pick(t, ROW_TILES)

    def body(h_ref, t_ref, l_ref, dh_ref):
        _zero_at_first_step([l_ref])
        err = h_ref[...] - t_ref[...]
        l_ref[...] += jnp.sum(jnp.sum(err * err, axis=-1, keepdims=True), axis=0, keepdims=True) * (0.5 / d)
        dh_ref[...] = err * (1.0 / d)

    return _row_call(body, name, t, r, [_row_spec(r, d)] * 2, (h, target), [_const_spec((1, 1)), _row_spec(r, d)],
                     [jax.ShapeDtypeStruct((1, 1), F32), jax.ShapeDtypeStruct((t, d), F32)])


def _ew_rows(rows, cols):
    cap = min(1024, max(8, (1 << 19) // max(cols, 1)))
    for cand in range(cap - cap % 8, 7, -8):
        if rows % cand == 0:
            return cand
    return rows


def _pair_sum_bf16(both, core_flag, b, name):
    _, n, rows, cols = both.shape
    rb = _ew_rows(rows, cols)

    def body(a_ref, flag_ref, b_ref, o_ref):
        mine = jnp.where(flag_ref[...] == 0.0, a_ref[0, 0], a_ref[1, 0])
        o_ref[0] = (mine + b_ref[0]).astype(BF16)

    spec = pl.BlockSpec((1, rb, cols), lambda s, i: (s, i, 0))
    return pl.pallas_call(
        body, name=name, grid=(n, rows // rb),
        in_specs=[pl.BlockSpec((2, 1, rb, cols), lambda s, i: (0, s, i, 0)),
                  pl.BlockSpec((1, 1), lambda s, i: (0, 0)), spec],
        out_specs=spec, out_shape=jax.ShapeDtypeStruct(b.shape, BF16),
        compiler_params=_params(("parallel", "parallel")))(both, core_flag, b)


def _shard_sum(a, b, recv, name):
    rows, cols = a.shape
    rb = _ew_rows(rows, cols)

    def body(a_ref, b_ref, r_ref, o_ref):
        o_ref[...] = ((a_ref[...] + b_ref[...]) + r_ref[0].astype(F32) + r_ref[1].astype(F32)
                      + r_ref[2].astype(F32))

    spec = pl.BlockSpec((rb, cols), lambda i: (i, 0))
    return pl.pallas_call(body, name=name, grid=(rows // rb,),
                          in_specs=[spec, spec, pl.BlockSpec((N_SHARD - 1, rb, cols), lambda i: (0, i, 0))],
                          out_specs=spec, out_shape=jax.ShapeDtypeStruct(a.shape, F32),
                          compiler_params=_params(("parallel",)))(a, b, recv)


def _sum_devices(g, name):
    n, rows, cols = g.shape
    rb = _ew_rows(rows, cols)

    def body(g_ref, o_ref):
        acc = g_ref[0]
        for k in range(1, n):
            acc = acc + g_ref[k]
        o_ref[...] = acc

    return pl.pallas_call(body, name=name, grid=(rows // rb,),
                          in_specs=[pl.BlockSpec((n, rb, cols), lambda i: (0, i, 0))],
                          out_specs=pl.BlockSpec((rb, cols), lambda i: (i, 0)),
                          out_shape=jax.ShapeDtypeStruct((rows, cols), F32),
                          compiler_params=_params(("parallel",)))(g)


def _adamw_update(w, g_v, m, v):
    m_new = ADAM_B1 * m + (1.0 - ADAM_B1) * g_v
    v_new = ADAM_B2 * v + (1.0 - ADAM_B2) * (g_v * g_v)
    m_hat = m_new / (1.0 - ADAM_B1 ** ADAM_STEP)
    v_hat = v_new / (1.0 - ADAM_B2 ** ADAM_STEP)
    return -ADAM_LR * (m_hat / (jnp.sqrt(v_hat) + ADAM_EPS) + ADAM_WD * w), m_new, v_new


def _adamw(w, g, m, v, name):
    rows, cols = w.shape
    rb = _ew_rows(rows, cols)

    def body(w_ref, g_ref, m_ref, v_ref, d_ref, mo_ref, vo_ref):
        d_ref[...], mo_ref[...], vo_ref[...] = _adamw_update(w_ref[...], g_ref[...], m_ref[...], v_ref[...])

    spec = pl.BlockSpec((rb, cols), lambda i: (i, 0))
    return pl.pallas_call(body, name=name, grid=(rows // rb,), in_specs=[spec] * 4, out_specs=[spec] * 3,
                          out_shape=[jax.ShapeDtypeStruct(w.shape, F32)] * 3,
                          compiler_params=_params(("parallel",)))(w, g, m, v)


def _adamw_two_halves(w, own, recv, core_flag, m, v, name):
    depth, rows, cols = w.shape
    assert rows % 2 == 0 and own.shape == (depth, rows // 2, cols)
    rb = _ew_rows(rows // 2, cols)
    nb = rows // 2 // rb

    def body(w_ref, own_ref, recv_ref, flag_ref, m_ref, v_ref, g_ref, d_ref, mo_ref, vo_ref):
        half = pl.program_id(1).astype(F32)
        g_v = jnp.where(flag_ref[...] == half, own_ref[...], recv_ref[...])
        g_ref[...] = g_v
        d_ref[...], mo_ref[...], vo_ref[...] = _adamw_update(w_ref[...], g_v, m_ref[...], v_ref[...])

    full = pl.BlockSpec((None, rb, cols), lambda l, k, i: (l, k * nb + i, 0))
    half_spec = pl.BlockSpec((None, rb, cols), lambda l, k, i: (l, i, 0))
    return pl.pallas_call(
        body, name=name, grid=(depth, 2, nb),
        in_specs=[full, half_spec, half_spec, pl.BlockSpec((1, 1), lambda l, k, i: (0, 0)), full, full],
        out_specs=[full] * 4, out_shape=[jax.ShapeDtypeStruct(w.shape, F32)] * 4,
        compiler_params=_params(("parallel", "parallel", "parallel")))(w, own, recv, core_flag, m, v)


def _place():
    return lax.axis_index("x"), lax.axis_index("y"), lax.axis_index("c")


def _other_chips(x, y):
    return [(1 - x, y), (x, 1 - y), (1 - x, 1 - y)]


_ANY = pl.BlockSpec(memory_space=pl.ANY)


class _Rider:
    def __init__(self, arrays, out_shapes, n_sems, start, finish, in_place=False):
        self.arrays, self.out_shapes, self.n_sems = list(arrays), list(out_shapes), n_sems
        self.start, self.finish, self.in_place = start, finish, in_place

    def sems(self):
        return [pltpu.SemaphoreType.DMA((self.n_sems,)), pltpu.SemaphoreType.DMA((self.n_sems,))]

    def aliases(self, first_in, first_out):
        return {first_in + a: first_out + a for a in range(len(self.arrays))} if self.in_place else {}


def _comm_call(rider, name):
    n_in, n_out = len(rider.arrays), len(rider.out_shapes)

    def body(*refs):
        ins, outs = refs[:n_in], refs[n_in:n_in + n_out]
        send_sems, recv_sems = refs[n_in + n_out:]
        rider.start(ins, outs, send_sems, recv_sems)
        rider.finish(ins, outs, send_sems, recv_sems)

    return pl.pallas_call(
        body, name=name, in_specs=[_ANY] * n_in, out_specs=[_ANY] * n_out, out_shape=rider.out_shapes,
        scratch_shapes=rider.sems(), input_output_aliases=rider.aliases(0, 0))(*rider.arrays)


def _half(shape, which):
    for axis, size in enumerate(shape):
        if size % 2 == 0:
            return (slice(None),) * axis + (pl.ds(which * (size // 2), size // 2),)
    raise ValueError(f"no axis of even length in {shape}")


def _start_then_wait(copies):
    def start(*refs):
        for send, _ in copies(*refs):
            send.start()

    def finish(*refs):
        pairs = copies(*refs)
        for _, landing in pairs:
            landing.wait_recv()
        for send, _ in pairs:
            send.wait_send()

    return start, finish


def _fetch_rider(shards):
    n, n_peer = len(shards), N_SHARD - 1
    shapes = [s.shape for s in shards]

    def copies(ins, outs, send_sems, recv_sems):
        x, y, c = _place()
        pairs = []
        for a in range(n):
            mine = _half(shapes[a], c)
            for k, (px, py) in enumerate(_other_chips(x, y)):
                def into(slot):
                    return pltpu.make_async_remote_copy(
                        src_ref=ins[a].at[mine], dst_ref=outs[a].at[(slot,) + mine],
                        send_sem=send_sems.at[a * n_peer + k], recv_sem=recv_sems.at[a * n_peer + k],
                        device_id=(px, py, c), device_id_type=MESH)
                pairs.append((into(2 * x + y), into(2 * px + py)))
        return pairs

    start, finish = _start_then_wait(copies)
    return _Rider(shards, [jax.ShapeDtypeStruct((N_SHARD,) + s.shape, s.dtype) for s in shards], n * n_peer,
                  start, finish)


def _forward_rider(gathered, shards):
    n, n_peer = len(gathered), N_SHARD - 1
    shapes = [s.shape for s in shards]

    def copies(ins, outs, send_sems, recv_sems):
        x, y, c = _place()
        pairs = []
        for a in range(n):
            for k, (px, py) in enumerate(_other_chips(x, y)):
                def half_of_slot(which):
                    rows = outs[a].at[(2 * px + py,) + _half(shapes[a], which)]
                    return pltpu.make_async_remote_copy(
                        src_ref=rows, dst_ref=rows, send_sem=send_sems.at[a * n_peer + k],
                        recv_sem=recv_sems.at[a * n_peer + k], device_id=(x, y, 1 - c), device_id_type=MESH)
                pairs.append((half_of_slot(c), half_of_slot(1 - c)))
        return pairs

    start, finish = _start_then_wait(copies)
    return _Rider(gathered, [jax.ShapeDtypeStruct(g.shape, g.dtype) for g in gathered], n * n_peer, start, finish,
                  in_place=True)


def _sibling_rider(arrs, other_half):
    n = len(arrs)

    def copies(ins, outs, send_sems, recv_sems):
        x, y, c = _place()
        pairs = []
        for a in range(n):
            cp = pltpu.make_async_remote_copy(
                src_ref=ins[a].at[1 - c] if other_half else ins[a], dst_ref=outs[a], send_sem=send_sems.at[a],
                recv_sem=recv_sems.at[a], device_id=(x, y, 1 - c), device_id_type=MESH)
            pairs.append((cp, cp))
        return pairs

    start, finish = _start_then_wait(copies)
    return _Rider(arrs, [jax.ShapeDtypeStruct(g.shape[1:] if other_half else g.shape, g.dtype) for g in arrs], n,
                  start, finish)


def _owner_rider(parts):
    n, n_peer = len(parts), N_SHARD - 1

    def copies(ins, outs, send_sems, recv_sems):
        x, y, c = _place()
        pairs = []
        for a in range(n):
            for k, (px, py) in enumerate(_other_chips(x, y)):
                cp = pltpu.make_async_remote_copy(
                    src_ref=ins[a].at[2 * px + py], dst_ref=outs[a].at[k], send_sem=send_sems.at[a * n_peer + k],
                    recv_sem=recv_sems.at[a * n_peer + k], device_id=(px, py, c), device_id_type=MESH)
                pairs.append((cp, cp))
        return pairs

    start, finish = _start_then_wait(copies)
    return _Rider(parts, [jax.ShapeDtypeStruct((n_peer,) + p.shape[1:], p.dtype) for p in parts], n * n_peer,
                  start, finish)


def _gather_devices(buf, name):
    n_peer = N_DEV - 1

    def body(in_ref, out_ref, send_sems, recv_sems, local_sem):
        x, y, c = _place()
        me = 4 * x + 2 * y + c
        mine = pltpu.make_async_copy(in_ref, out_ref.at[me], local_sem)
        mine.start()
        peers = []
        for k in range(1, N_DEV):
            fx, fy, fc = (k >> 2) & 1, (k >> 1) & 1, k & 1
            peers.append((x ^ fx, y ^ fy, c ^ fc))
        sends = []
        for k, peer in enumerate(peers):
            cp = pltpu.make_async_remote_copy(
                src_ref=in_ref, dst_ref=out_ref.at[me], send_sem=send_sems.at[k], recv_sem=recv_sems.at[k],
                device_id=peer, device_id_type=MESH)
            cp.start()
            sends.append(cp)
        for k, (px, py, pc) in enumerate(peers):
            pltpu.make_async_remote_copy(
                src_ref=in_ref, dst_ref=out_ref.at[4 * px + 2 * py + pc], send_sem=send_sems.at[k],
                recv_sem=recv_sems.at[k], device_id=(px, py, pc), device_id_type=MESH).wait_recv()
        for cp in sends:
            cp.wait_send()
        mine.wait()

    return pl.pallas_call(
        body, name=name, in_specs=[_ANY], out_specs=_ANY,
        out_shape=jax.ShapeDtypeStruct((N_DEV,) + buf.shape, buf.dtype),
        scratch_shapes=[pltpu.SemaphoreType.DMA((n_peer,)), pltpu.SemaphoreType.DMA((n_peer,)),
                        pltpu.SemaphoreType.DMA(())],
    )(buf)


class _Dims:
    def __init__(self, x, p, w_in, sgu_norm, conv_w, kv_norm, w_ukv, w_out):
        self.t, self.d = x.shape[1], x.shape[2]
        self.depth = w_in.shape[0]
        self.ple = p.shape[3]
        self.in_w = w_in.shape[2] * N_SHARD
        self.ah = sgu_norm.shape[1]
        self.aw = self.ah * HEAD
        self.bw = conv_w.shape[2] * N_SHARD
        self.kvr = kv_norm.shape[1]
        self.ch = w_ukv.shape[2] * N_SHARD // (2 * HEAD)
        self.cw = self.ch * HEAD
        self.mix = w_out.shape[1] * N_SHARD
        assert self.mix == self.aw + self.bw + self.cw and self.aw == self.bw
        self.qw = self.ch * 2 * HEAD
        segs = [('a', 3 * self.aw, 3 * self.aw), ('ckv', self.kvr, self.kvr), ('b', 4 * self.bw, 4 * self.bw),
                ('q', self.qw, self.qw), ('cz', self.cw, self.cw), ('kr', HEAD, HEAD)]
        off = 0
        self.off = {}
        for nm, width, align in segs:
            off = -(-off // align) * align
            self.off[nm] = off
            off += width
        self.inp = -(-off // 512) * 512
        q_real = self.ch * (HEAD + ROPE)
        widths = [3 * self.aw, 4 * self.bw, q_real, self.kvr, ROPE, self.cw]
        assert sum(widths) == self.in_w
        starts = [0]
        for wd in widths:
            starts.append(starts[-1] + wd)
        self.src = dict(zip(['a', 'b', 'q', 'ckv', 'kr', 'cz'], zip(starts[:-1], widths)))


def _rearrange_w_in(w, dm):
    lead, d = w.shape[:-2], w.shape[-1]
    axis = w.ndim - 2

    def rows(nm):
        s, wd = dm.src[nm]
        return lax.slice_in_dim(w, s, s + wd, axis=axis)

    pieces = {nm: rows(nm) for nm in ('a', 'b', 'ckv', 'cz')}
    q = rows('q').reshape(lead + (dm.ch, HEAD + ROPE, d))
    pieces['q'] = jnp.pad(q, [(0, 0)] * (len(lead) + 1) + [(0, HEAD - ROPE), (0, 0)]).reshape(lead + (dm.qw, d))
    pieces['kr'] = jnp.pad(rows('kr'), [(0, 0)] * len(lead) + [(0, HEAD - ROPE), (0, 0)])
    out, cur = [], 0
    for nm in sorted(dm.off, key=lambda k: dm.off[k]):
        if dm.off[nm] > cur:
            out.append(jnp.zeros(lead + (dm.off[nm] - cur, d), w.dtype))
        out.append(pieces[nm])
        cur = dm.off[nm] + pieces[nm].shape[axis]
    if dm.inp > cur:
        out.append(jnp.zeros(lead + (dm.inp - cur, d), w.dtype))
    return jnp.concatenate(out, axis=axis)


def _unarrange_w_in(g, dm):
    d = g.shape[1]

    def seg(nm, width):
        return g[dm.off[nm]:dm.off[nm] + width]

    q = seg('q', dm.qw).reshape(dm.ch, 2 * HEAD, d)[:, :HEAD + ROPE].reshape(dm.ch * (HEAD + ROPE), d)
    return jnp.concatenate([seg('a', 3 * dm.aw), seg('b', 4 * dm.bw), q, seg('ckv', dm.kvr), seg('kr', ROPE),
                            seg('cz', dm.cw)], axis=0)


def _finish_dproj(dproj, parts, widths, dm):
    t = dproj.shape[0]
    for nm, part in parts.items():
        dproj = lax.dynamic_update_slice(dproj, part, (0, dm.off[nm]))
    cur = 0
    for nm in sorted(dm.off, key=lambda k: dm.off[k]):
        if dm.off[nm] > cur:
            dproj = lax.dynamic_update_slice(dproj, jnp.zeros((t, dm.off[nm] - cur), BF16), (0, cur))
        cur = dm.off[nm] + widths[nm]
    assert cur == dm.inp
    return dproj


def _rope_tables(positions):
    inv = 1.0 / (ROPE_BASE ** (jnp.arange(0, ROPE, 2, dtype=F32) / ROPE))
    ang = positions.astype(F32)[:, None] * inv
    cos, sin = jnp.cos(ang), jnp.sin(ang)
    t = positions.shape[0]
    half = ROPE // 2
    cos_t = jnp.concatenate([cos, cos, jnp.zeros((t, HEAD - ROPE), F32)], axis=-1)
    sin_a = jnp.concatenate([-sin, jnp.zeros((t, HEAD - half), F32)], axis=-1)
    sin_b = jnp.concatenate([jnp.zeros((t, half), F32), sin, jnp.zeros((t, HEAD - ROPE), F32)], axis=-1)
    return cos_t, sin_a, sin_b


def _pad_gain(g):
    return jnp.pad(g, (0, HEAD - g.shape[0]))[None, :]


def _shard_major(g, axis):
    shape = g.shape
    g = g.reshape(shape[:axis] + (N_SHARD, shape[axis] // N_SHARD) + shape[axis + 1:])
    g = jnp.moveaxis(g, axis, 0)
    rows, cols = g.shape[1], g.shape[2]
    return jnp.swapaxes(g.reshape(N_SHARD, 2, rows // 2, cols), 0, 1)


def _pack(arrs):
    flat = jnp.concatenate([a.reshape(-1) for a in arrs])
    pad = (-flat.shape[0]) % (8 * HEAD)
    return jnp.pad(flat, (0, pad)).reshape(-1, HEAD)


def _unpack(buf, shapes):
    flat = buf.reshape(-1)
    out, cur = [], 0
    for s in shapes:
        size = 1
        for v in s:
            size *= v
        out.append(flat[cur:cur + size].reshape(s))
        cur += size
    return out


def kernel(x, p, positions, attn_norm, w_in, sgu_norm, w_spatial, b_spatial, conv_w, conv_b, kv_norm, w_ukv, q_nope_norm, q_rope_norm, k_nope_norm, k_rope_norm, out_norm, w_out, ple_norm, w_ple_gate, w_ple_proj, loss_target, m_attn_norm, m_w_in, m_sgu_norm, m_w_spatial, m_b_spatial, m_conv_w, m_conv_b, m_kv_norm, m_w_ukv, m_q_nope_norm, m_q_rope_norm, m_k_nope_norm, m_k_rope_norm, m_out_norm, m_w_out, m_ple_norm, m_w_ple_gate, m_w_ple_proj, v_attn_norm, v_w_in, v_sgu_norm, v_w_spatial, v_b_spatial, v_conv_w, v_conv_b, v_kv_norm, v_w_ukv, v_q_nope_norm, v_q_rope_norm, v_k_nope_norm, v_k_rope_norm, v_out_norm, v_w_out, v_ple_norm, v_w_ple_gate, v_w_ple_proj):
    weights = dict(attn_norm=attn_norm, w_in=w_in, sgu_norm=sgu_norm, w_spatial=w_spatial, b_spatial=b_spatial,
                   conv_w=conv_w, conv_b=conv_b, kv_norm=kv_norm, w_ukv=w_ukv, q_nope_norm=q_nope_norm,
                   q_rope_norm=q_rope_norm, k_nope_norm=k_nope_norm, k_rope_norm=k_rope_norm, out_norm=out_norm,
                   w_out=w_out, ple_norm=ple_norm, w_ple_gate=w_ple_gate, w_ple_proj=w_ple_proj)
    mom_m = dict(attn_norm=m_attn_norm, w_in=m_w_in, sgu_norm=m_sgu_norm, w_spatial=m_w_spatial,
                 b_spatial=m_b_spatial, conv_w=m_conv_w, conv_b=m_conv_b, kv_norm=m_kv_norm, w_ukv=m_w_ukv,
                 q_nope_norm=m_q_nope_norm, q_rope_norm=m_q_rope_norm, k_nope_norm=m_k_nope_norm,
                 k_rope_norm=m_k_rope_norm, out_norm=m_out_norm, w_out=m_w_out, ple_norm=m_ple_norm,
                 w_ple_gate=m_w_ple_gate, w_ple_proj=m_w_ple_proj)
    mom_v = dict(attn_norm=v_attn_norm, w_in=v_w_in, sgu_norm=v_sgu_norm, w_spatial=v_w_spatial,
                 b_spatial=v_b_spatial, conv_w=v_conv_w, conv_b=v_conv_b, kv_norm=v_kv_norm, w_ukv=v_w_ukv,
                 q_nope_norm=v_q_nope_norm, q_rope_norm=v_q_rope_norm, k_nope_norm=v_k_nope_norm,
                 k_rope_norm=v_k_rope_norm, out_norm=v_out_norm, w_out=v_w_out, ple_norm=v_ple_norm,
                 w_ple_gate=v_w_ple_gate, w_ple_proj=v_w_ple_proj)
    dm = _Dims(x, p, w_in, sgu_norm, conv_w, kv_norm, w_ukv, w_out)
    for group in (weights, mom_m, mom_v):
        group['w_in'] = jnp.swapaxes(group['w_in'], 1, 2)
    t, d, depth = dm.t, dm.d, dm.depth
    shard = 2 * lax.axis_index("x") + lax.axis_index("y")
    core = lax.axis_index("c")
    scale = float(HEAD + ROPE) ** -0.5

    def local_layer(i):
        return [weights[n][i:i + 1].astype(BF16) for n in BIG]

    def fill_own_slot(gathered, local):
        return [lax.dynamic_update_slice(g, mine[None], (shard,) + (0,) * mine.ndim)
                for g, mine in zip(gathered, local)]

    def layer_weights(filled):
        w = {n: jnp.concatenate([filled[j][s] for s in range(N_SHARD)], axis=BIG_AXIS[n])
             for j, n in enumerate(BIG)}
        w['w_in'] = _rearrange_w_in(w['w_in'], dm)
        return w

    first = local_layer(0) + [conv_w]
    fetched = _comm_call(_fetch_rider(first), "fetch_weights_l0")
    filled = fill_own_slot(_comm_call(_forward_rider(fetched, first), "forward_weights_l0"), first)
    layer_w = [layer_weights(filled)] + [None] * (depth - 1)
    conv_w_full = jnp.concatenate([filled[len(BIG)][s] for s in range(N_SHARD)], axis=2)

    tabs = _rope_tables(positions[0])
    h = x[0]
    saved = []
    for i in range(depth):
        tag = f"l{i}_"
        ga, gb, gc = (out_norm[i][None, :dm.aw], out_norm[i][None, dm.aw:dm.aw + dm.bw],
                      out_norm[i][None, dm.aw + dm.bw:])
        ws_b = w_spatial[i].astype(BF16)
        bb = jnp.broadcast_to(b_spatial[i][:, :, None], (dm.ah, HEAD, HEAD))
        qn_g, qr_g = q_nope_norm[i][None, :], _pad_gain(q_rope_norm[i])
        kn_g, kr_g = k_nope_norm[i][None, :], _pad_gain(k_rope_norm[i])
        kv_g = kv_norm[i][None, :]
        wl = layer_w[i]
        nxt = local_layer(i + 1) if i + 1 < depth else None
        hn = _norm_fwd(h, attn_norm[i][None, :], tag + "norm1")
        if nxt is None:
            proj = _matmul(hn, wl['w_in'], 'nt', BF16, tag + "proj", b_layer=0)
        else:
            proj, fetched = _matmul(hn, wl['w_in'], 'nt', BF16, tag + "proj", b_layer=0, rider=_fetch_rider(nxt))
        y = _sgu_fwd(proj, dm.off['a'], dm.aw, sgu_norm[i], ws_b, bb, ga, tag + "sgu",
                     into=(jnp.zeros((t, dm.mix), BF16), 0))
        y, yconv = _conv_fwd(proj, dm.off['b'], dm.bw, conv_w_full[i], conv_b[i][None, :], gb, tag + "conv",
                             into=(y, _col_block(dm.aw, dm.bw)))
        q_cat, ckv_n, kr_rot = _mla_prep_fwd(proj, dm.off['q'], dm.off['ckv'], dm.off['kr'], dm.ch, dm.kvr, tabs,
                                             qn_g, qr_g, kr_g, kv_g, tag + "mla_prep")
        kv = _matmul(ckv_n, wl['w_ukv'], 'nn', F32, tag + "kv_up", b_layer=0)
        k_cat, k_cat_t, v_aug = _kv_prep_fwd(kv, kr_rot, dm.ch, kn_g, tag + "kv_prep")
        o, lse = _attn_fwd(q_cat, k_cat, v_aug, dm.ch, scale, tag + "attn")
        y = _attn_post_fwd(o, proj, dm.off['cz'], dm.cw, gc, tag + "attn_post",
                           into=(y, _col_block(dm.aw + dm.bw, dm.cw)))
        if nxt is None:
            h1 = _matmul(y, wl['w_out'], 'nn', F32, tag + "out", add=h, b_layer=0)
        else:
            h1, gathered = _matmul(y, wl['w_out'], 'nn', F32, tag + "out", add=h, b_layer=0,
                                   rider=_forward_rider(list(fetched), nxt))
            layer_w[i + 1] = layer_weights(fill_own_slot(gathered, nxt))
        hn2 = _norm_fwd(h1, ple_norm[i][None, :], tag + "norm2")
        gpre = _matmul(hn2, wl['w_ple_gate'], 'nn', F32, tag + "gate", b_layer=0)
        p_b = p[i, 0].astype(BF16)
        pp, h2 = _ple_proj_fwd(p_b, wl['w_ple_proj'], h1, gpre, tag + "ple_proj")
        saved.append(dict(h=h, hn=hn, proj=proj, yconv=yconv, q_cat=q_cat, ckv_n=ckv_n, kv=kv, k_cat=k_cat,
                          k_cat_t=k_cat_t,
                          v=v_aug, o=o, lse=lse, y=y, h1=h1, hn2=hn2, gpre=gpre, pp=pp, p_b=p_b, ws_b=ws_b, bb=bb,
                          gains=(ga, gb, gc, qn_g, qr_g, kn_g, kr_g, kv_g)))
        h = h2

    loss_part, dh = _loss_and_grad(h, loss_target[0], "loss")
    loss = lax.psum(loss_part[0, 0], ("x", "y", "c"))

    core_flag = core.astype(F32).reshape(1, 1)

    def chip_sums(sm, from_sibling, tag):
        return sm, [_pair_sum_bf16(a, core_flag, b, f"{tag}chip_sum_{n}") for a, b, n in zip(sm, from_sibling, BIG)]

    def shard_sums(sm, from_sibling, from_chips, tag):
        out = []
        for a, b, r3, n in zip(sm, from_sibling, from_chips, BIG):
            own_a = lax.dynamic_slice(a, (core, shard, 0, 0), (1, 1) + a.shape[2:]).reshape(a.shape[2:])
            own_b = lax.dynamic_index_in_dim(b, shard, 0, keepdims=False)
            out.append(_shard_sum(own_a, own_b, r3, f"{tag}shard_sum_{n}"))
        return out

    grads = {n: [None] * depth for n in WEIGHTS}
    own_half = {n: [None] * depth for n in BIG}
    sibling_half = {n: [None] * depth for n in BIG}
    carry = None
    for i in reversed(range(depth)):
        tag = f"l{i}_b_"
        gtag = f"l{i + 1}_g_"
        sv = saved[i]
        wl = layer_w[i]
        ga, gb, gc, qn_g, qr_g, kn_g, kr_g, kv_g = sv['gains']
        proj = sv['proj']
        dgpre, dpp = _ple_bwd(sv['gpre'], sv['pp'], dh, tag + "ple")
        grads['w_ple_proj'][i] = _matmul(sv['p_b'], dpp, 'tn', F32, tag + "d_w_ple_proj")
        if carry is None:
            grads['w_ple_gate'][i] = _matmul(sv['hn2'], dgpre, 'tn', F32, tag + "d_w_gate")
        else:
            grads['w_ple_gate'][i], from_sibling = _matmul(sv['hn2'], dgpre, 'tn', F32, tag + "d_w_gate",
                                                           rider=_sibling_rider(carry, True))
            mine, sums = chip_sums(carry, from_sibling, gtag)
        d_hn2 = _matmul(dgpre, wl['w_ple_gate'], 'nt', BF16, tag + "d_hn2", b_layer=0)
        dh1, dh1_b, g_ple = _norm_bwd(sv['h1'], ple_norm[i][None, :], d_hn2, dh, tag + "norm2")
        grads['ple_norm'][i] = g_ple[0]
        grads['w_out'][i] = _matmul(sv['y'], dh1_b, 'tn', F32, tag + "d_w_out")
        dy = _matmul(dh1_b, wl['w_out'], 'nt', BF16, tag + "d_y", b_layer=0)
        ws_t = jnp.swapaxes(sv['ws_b'], 1, 2)
        dproj, g_sgu, g_ws, g_bs, g_ga = _sgu_bwd(proj, dm.off['a'], dm.aw, sgu_norm[i], sv['ws_b'], ws_t, sv['bb'],
                                                  ga, dy, dm.inp, tag + "sgu")
        grads['sgu_norm'][i], grads['w_spatial'][i], grads['b_spatial'][i] = g_sgu, g_ws, g_bs[:, :, 0]
        dyc, d_bb, d_bz, g_gb, g_cb = _conv_bwd_gate(proj, dm.off['b'], dm.bw, sv['yconv'], gb, dy, tag + "conv_gate")
        dproj, g_cw = _conv_bwd_taps(proj, dm.off['b'], dm.bw, dyc, conv_w_full[i], d_bb, d_bz, tag + "conv_taps",
                                     into=(dproj, _col_block(dm.off['b'], 4 * dm.bw)))
        grads['conv_b'][i], grads['conv_w'][i] = g_cb[0], g_cw
        dproj, d_o, dsum, g_gc = _attn_post_bwd(sv['o'], proj, dm.off['cz'], dm.cw, gc, dy,
                                                _col_block(dm.aw + dm.bw, dm.cw), tag + "attn_post",
                                                into=(dproj, _col_block(dm.off['cz'], dm.cw)))
        grads['out_norm'][i] = jnp.concatenate([g_ga[0], g_gb[0], g_gc[0]])
        dq_t, dk_cat, dv = _attn_bwd(sv['q_cat'], sv['k_cat'], sv['k_cat_t'], sv['v'], d_o,
                                     sv['lse'].reshape(dm.ch, 1, t), dsum.reshape(dm.ch, 1, t), dm.ch, scale,
                                     tag + "attn_bwd")
        dkv, dkr_rot, g_kn = _kv_prep_bwd(sv['kv'], dm.ch, kn_g, dk_cat, dv, tag + "kv_prep")
        grads['k_nope_norm'][i] = g_kn[0]
        grads['w_ukv'][i] = _matmul(sv['ckv_n'], dkv, 'tn', F32, tag + "d_w_ukv")
        dckv_n = _matmul(dkv, wl['w_ukv'], 'nt', BF16, tag + "d_ckv", b_layer=0)
        kr_width = dm.inp - dm.off['kr']
        dproj, d_ckv, d_kr, g_qn, g_qr, g_kr, g_kv = _mla_prep_bwd(
            proj, dm.off['q'], dm.off['ckv'], dm.off['kr'], dm.ch, dm.kvr, tabs, qn_g, qr_g, kr_g, kv_g,
            dq_t, scale, dckv_n, dkr_rot, kr_width, tag + "mla_prep", into=(dproj, _col_block(dm.off['q'], dm.qw)))
        grads['q_nope_norm'][i], grads['q_rope_norm'][i] = g_qn[0], g_qr[0, :ROPE]
        grads['k_rope_norm'][i], grads['kv_norm'][i] = g_kr[0, :ROPE], g_kv[0]
        dproj = _finish_dproj(dproj, dict(ckv=d_ckv, kr=d_kr),
                              dict(a=3 * dm.aw, b=4 * dm.bw, ckv=dm.kvr, q=dm.qw, cz=dm.cw, kr=kr_width), dm)
        if carry is None:
            d_w_in = _matmul(dproj, sv['hn'], 'tn', F32, tag + "d_w_in")
            d_hn = _matmul(dproj, wl['w_in'], 'nn', BF16, tag + "d_hn", b_layer=0)
        else:
            d_w_in, from_chips = _matmul(dproj, sv['hn'], 'tn', F32, tag + "d_w_in", rider=_owner_rider(sums))
            halves = shard_sums(mine, from_sibling, from_chips, gtag)
            d_hn, from_core = _matmul(dproj, wl['w_in'], 'nn', BF16, tag + "d_hn", b_layer=0,
                                      rider=_sibling_rider(halves, False))
            for n, own, recv in zip(BIG, halves, from_core):
                own_half[n][i + 1], sibling_half[n][i + 1] = own, recv
        grads['w_in'][i] = _unarrange_w_in(d_w_in, dm)
        dh, _, g_an = _norm_bwd(sv['h'], attn_norm[i][None, :], d_hn, dh1, tag + "norm1")
        grads['attn_norm'][i] = g_an[0]
        carry = [_shard_major(grads[n][i], BIG_AXIS[n] - 1) for n in BIG]
    grad_x = dh[None]

    from_sibling = _comm_call(_sibling_rider(carry, True), "l0_g_to_sibling")
    mine, sums = chip_sums(carry, from_sibling, "l0_g_")
    from_chips = _comm_call(_owner_rider(sums), "l0_g_to_owner_chips")
    halves = shard_sums(mine, from_sibling, from_chips, "l0_g_")
    from_core = _comm_call(_sibling_rider(halves, False), "l0_g_share_sibling")
    for n, own, recv in zip(BIG, halves, from_core):
        own_half[n][0], sibling_half[n][0] = own, recv

    out_g, out_d, out_m, out_v = {}, {}, {}, {}
    for n in BIG:
        out_g[n], out_d[n], out_m[n], out_v[n] = _adamw_two_halves(
            weights[n], jnp.stack(own_half[n]), jnp.stack(sibling_half[n]), core_flag, mom_m[n], mom_v[n],
            f"adamw_{n}")
    for out in (out_g, out_d, out_m, out_v):
        out['w_in'] = jnp.swapaxes(out['w_in'], 1, 2)
    grads = {n: jnp.stack(grads[n]) for n in SMALL}

    shapes = [grads[n].shape for n in SMALL]
    summed = _unpack(_sum_devices(_gather_devices(_pack([grads[n] for n in SMALL]), "gather_small_grads"),
                                  "sum_small_grads"), shapes)
    small_g = dict(zip(SMALL, summed))
    small_g['conv_w'] = lax.dynamic_slice_in_dim(small_g['conv_w'], shard * conv_w.shape[2], conv_w.shape[2], axis=2)
    local_shapes = [weights[n].shape for n in SMALL]
    d_s, m_s, v_s = _adamw(_pack([weights[n] for n in SMALL]), _pack([small_g[n] for n in SMALL]),
                           _pack([mom_m[n] for n in SMALL]), _pack([mom_v[n] for n in SMALL]), "adamw_small")
    for n, dd, mm, vv in zip(SMALL, _unpack(d_s, local_shapes), _unpack(m_s, local_shapes),
                             _unpack(v_s, local_shapes)):
        out_g[n], out_d[n], out_m[n], out_v[n] = small_g[n], dd, mm, vv

    return (loss, grad_x, *[out_g[n] for n in WEIGHTS], *[out_d[n] for n in WEIGHTS],
            *[out_m[n] for n in WEIGHTS], *[out_v[n] for n in WEIGHTS])
```
